```python
import math
import jax
import jax.numpy as jnp
from jax import lax
import numpy as np

D_MODEL = 1024
BATCH = 8
SEQ = 2048
DEPTH = 1

GRID_W = 64
CTX_LEN = 256
N_HEADS = 8
N_KV_HEADS = 2
HEAD_DIM = 64
GROUP = N_HEADS // N_KV_HEADS
ATTN_W = N_HEADS * HEAD_DIM
KV_W = N_KV_HEADS * HEAD_DIM
WINDOW = 128
BLOCK = 128
HYENA_W = D_MODEL // 2
HYENA_ORDER = 2
FILTER_BANDS = 16
FILTER_EMB = 1 + 2 * FILTER_BANDS
FILTER_HIDDEN = 64
DECAY_TARGET = 1e-2
FAST_DECAY_PCT = 0.3
SLOW_DECAY_PCT = 1.5
ROPE_BASE = 10000.0
AXIS_ROT = HEAD_DIM // 2
N_EXPERTS = 16
EC_CAPACITY = 2
D_FF = 2048
N_BRANCHES = 2
EPS = 1e-6
NEG = -1e30

OFF_Q = 3 * HYENA_W
OFF_K = OFF_Q + ATTN_W
OFF_V = OFF_K + KV_W
OFF_G = OFF_V + KV_W
IN_W = OFF_G + N_BRANCHES * D_MODEL

kernel_name = 'hybrid_hyena_swa_ec_dit_layer'


def _rms(x, g):
    xf = x.astype(jnp.float32)
    y = xf * lax.rsqrt(jnp.mean(xf * xf, axis=-1, keepdims=True) + EPS)
    return (y * g.astype(jnp.float32)).astype(x.dtype)


def _short_conv(u, w, b):
    L = u.shape[1]
    up = jnp.pad(u, ((0, 0), (1, 1), (0, 0)))
    return up[:, :L] * w[0] + up[:, 1:L + 1] * w[1] + up[:, 2:] * w[2] + b


def _hyena_filters(L, fw1, fb1, fw2, fb2, fw3, fb3, ffreq, fout):
    f32 = jnp.float32
    t = jnp.linspace(0.0, 1.0, L, dtype=f32)[:, None]
    w = 2.0 * math.pi * jnp.arange(L, dtype=f32)[:, None] / L
    fr = jnp.linspace(1e-4, FILTER_BANDS - 1, FILTER_BANDS, dtype=f32)[None, :]
    feat = jnp.concatenate([t, jnp.cos(fr * w), -jnp.sin(fr * w)], axis=-1)
    fq = ffreq.astype(f32)
    h = jnp.sin(fq * (feat @ fw1.astype(f32) + fb1.astype(f32)))
    h = jnp.sin(fq * (h @ fw2.astype(f32) + fb2.astype(f32)))
    h = jnp.sin(fq * (h @ fw3.astype(f32) + fb3.astype(f32)))
    h = (h @ fout.astype(f32)).reshape(L, HYENA_ORDER, 2, HYENA_W)
    min_decay = math.log(DECAY_TARGET) / SLOW_DECAY_PCT
    max_decay = math.log(DECAY_TARGET) / FAST_DECAY_PCT
    deltas = jnp.linspace(min_decay, max_decay, HYENA_W, dtype=f32)
    decay = jnp.exp(-t * jnp.abs(deltas))
    h = h * decay[:, None, None, :]
    k = jnp.concatenate([h[:, :, 0], jnp.zeros((1, HYENA_ORDER, HYENA_W), f32), h[:0:-1, :, 1]], axis=0)
    return jnp.fft.rfft(k, axis=0)


def _hyena(u, conv_w, conv_b, kf, hyena_bias):
    L = u.shape[1]
    z = _short_conv(u, conv_w, conv_b)
    v, x1, x2 = jnp.split(z, 3, axis=-1)
    y = v.astype(jnp.float32)
    for o, gate in enumerate((x1, x2)):
        yf = jnp.fft.rfft(y, n=2 * L, axis=1)
        y = jnp.fft.irfft(yf * kf[:, o], n=2 * L, axis=1)[:, :L] + hyena_bias[o].astype(jnp.float32) * y
        y = gate.astype(jnp.float32) * y
    return y.astype(u.dtype)


def _axial_rope(L):
    rows = L // GRID_W
    f32 = jnp.float32
    row = jnp.repeat(jnp.arange(rows, dtype=f32), GRID_W)
    col = jnp.tile(jnp.arange(GRID_W, dtype=f32), rows)
    inv = ROPE_BASE ** (-jnp.arange(0, AXIS_ROT, 2, dtype=f32) / AXIS_ROT)
    ang = jnp.concatenate([row[:, None] * inv, col[:, None] * inv], axis=-1)
    return jnp.cos(ang)[None, :, None, :], jnp.sin(ang)[None, :, None, :]


def _rope(x, cos, sin):
    xf = x.astype(jnp.float32)
    xe, xo = xf[..., 0::2], xf[..., 1::2]
    out = jnp.stack([xe * cos - xo * sin, xe * sin + xo * cos], axis=-1).reshape(x.shape)
    return out.astype(x.dtype)


def _window_attention(q, k, v, kc, vc, sink):
    B, L = q.shape[:2]
    nb = L // BLOCK
    nw = 3 * BLOCK
    nc = kc.shape[1]
    scale = HEAD_DIM ** -0.5
    qb = q.reshape(B, nb, BLOCK, N_KV_HEADS, GROUP, HEAD_DIM)

    def bands(t):
        tp = jnp.pad(t, ((0, 0), (BLOCK, BLOCK), (0, 0), (0, 0))).reshape(B, nb + 2, BLOCK, N_KV_HEADS, HEAD_DIM)
        return jnp.concatenate([tp[:, :-2], tp[:, 1:-1], tp[:, 2:]], axis=2)

    kw, vw = bands(k), bands(v)
    s_win = jnp.einsum('bnqhgd,bnkhd->bnhgqk', qb, kw).astype(jnp.float32) * scale
    rel = jnp.arange(nw)[None, :] - BLOCK - jnp.arange(BLOCK)[:, None]
    kpos = (jnp.arange(nb)[:, None] - 1) * BLOCK + jnp.arange(nw)[None, :]
    mask = (jnp.abs(rel) <= WINDOW)[None] & ((kpos >= 0) & (kpos < L))[:, None, :]
    s_win = jnp.where(mask[None, :, None, None], s_win, NEG)
    s_ctx = jnp.einsum('bnqhgd,bchd->bnhgqc', qb, kc).astype(jnp.float32) * scale
    snk = jnp.broadcast_to(sink.astype(jnp.float32).reshape(N_KV_HEADS, GROUP)[None, None, :, :, None, None],
                           s_win.shape[:-1] + (1,))
    p = jax.nn.softmax(jnp.concatenate([s_win, s_ctx, snk], axis=-1), axis=-1)
    o = (jnp.einsum('bnhgqk,bnkhd->bnqhgd', p[..., :nw].astype(v.dtype), vw)
         + jnp.einsum('bnhgqc,bchd->bnqhgd', p[..., nw:nw + nc].astype(v.dtype), vc))
    return o.reshape(B, L, ATTN_W)


def _ctx_attention(qc, kc, vc, sink):
    B, Lc = qc.shape[:2]
    q = qc.reshape(B, Lc, N_KV_HEADS, GROUP, HEAD_DIM)
    s = jnp.einsum('bqhgd,bkhd->bhgqk', q, kc).astype(jnp.float32) * (HEAD_DIM ** -0.5)
    snk = jnp.broadcast_to(sink.astype(jnp.float32).reshape(N_KV_HEADS, GROUP)[None, :, :, None, None],
                           s.shape[:-1] + (1,))
    p = jax.nn.softmax(jnp.concatenate([s, snk], axis=-1), axis=-1)
    o = jnp.einsum('bhgqk,bkhd->bqhgd', p[..., :Lc].astype(vc.dtype), vc)
    return o.reshape(B, Lc, ATTN_W)


def _merge(y_a, y_b, gate_logits, w_branch_a, w_branch_b, w_out):
    ga, gb = jnp.split(jax.nn.sigmoid(gate_logits), 2, axis=-1)
    return (ga * (y_a @ w_branch_a) + gb * (y_b @ w_branch_b)) @ w_out


def _expert_choice(h, router, w_gate, w_up, w_down):
    B, L, D = h.shape
    cap = EC_CAPACITY * L // N_EXPERTS
    aff = jax.nn.softmax((h @ router).astype(jnp.float32), axis=-1)
    g, idx = lax.top_k(jnp.swapaxes(aff, 1, 2), cap)
    idx_flat = idx.reshape(B, N_EXPERTS * cap)
    bidx = jnp.arange(B)[:, None]
    xin = h[bidx, idx_flat].reshape(B, N_EXPERTS, cap, D)
    a = jnp.einsum('becd,edf->becf', xin, w_gate)
    b = jnp.einsum('becd,edf->becf', xin, w_up)
    y = jnp.einsum('becf,efd->becd', jax.nn.silu(a) * b, w_down)
    y = (y * g[..., None].astype(y.dtype)).reshape(B, N_EXPERTS * cap, D)
    return jnp.zeros_like(h).at[bidx, idx_flat].add(y)


def _layer(x, ctx, mod_x, mod_c, norm1, norm2, w_in, conv_w, conv_b, filt, hyena_bias,
           q_norm, k_norm, attn_sink, w_branch_a, w_branch_b, w_out, router, w_gate, w_up, w_down,
           update_ctx):
    B, L, _ = x.shape
    Lc = ctx.shape[1]
    sh1, sc1, g1, sh2, sc2, g2 = jnp.split(mod_x, 6, axis=-1)
    ch1, cc1, cg1, ch2, cc2, cg2 = jnp.split(mod_c, 6, axis=-1)

    hc = _rms(ctx, norm1) * (1 + cc1) + ch1
    if update_ctx:
        zc = hc @ w_in
        kvc = zc[..., OFF_K:OFF_G]
    else:
        kvc = hc @ w_in[:, OFF_K:OFF_G]
    kc = _rms(kvc[..., :KV_W].reshape(B, Lc, N_KV_HEADS, HEAD_DIM), k_norm)
    vc = kvc[..., KV_W:].reshape(B, Lc, N_KV_HEADS, HEAD_DIM)

    hx = _rms(x, norm1) * (1 + sc1) + sh1
    zx = hx @ w_in
    y_a = _hyena(zx[..., :OFF_Q], conv_w, conv_b, _hyena_filters(L, *filt), hyena_bias)
    cos, sin = _axial_rope(L)
    q = _rope(_rms(zx[..., OFF_Q:OFF_K].reshape(B, L, N_HEADS, HEAD_DIM), q_norm), cos, sin)
    k = _rope(_rms(zx[..., OFF_K:OFF_V].reshape(B, L, N_KV_HEADS, HEAD_DIM), k_norm), cos, sin)
    v = zx[..., OFF_V:OFF_G].reshape(B, L, N_KV_HEADS, HEAD_DIM)
    y_b = _window_attention(q, k, v, kc, vc, attn_sink)
    x_new = x + g1 * _merge(y_a, y_b, zx[..., OFF_G:], w_branch_a, w_branch_b, w_out)
    hx2 = _rms(x_new, norm2) * (1 + sc2) + sh2
    x_new = x_new + g2 * _expert_choice(hx2, router, w_gate, w_up, w_down)

    ctx_new = ctx
    if update_ctx:
        y_ac = _hyena(zc[..., :OFF_Q], conv_w, conv_b, _hyena_filters(Lc, *filt), hyena_bias)
        qc = _rms(zc[..., OFF_Q:OFF_K].reshape(B, Lc, N_HEADS, HEAD_DIM), q_norm)
        y_bc = _ctx_attention(qc, kc, vc, attn_sink)
        ctx_new = ctx + cg1 * _merge(y_ac, y_bc, zc[..., OFF_G:], w_branch_a, w_branch_b, w_out)
        hc2 = _rms(ctx_new, norm2) * (1 + cc2) + ch2
        ctx_new = ctx_new + cg2 * _expert_choice(hc2, router, w_gate, w_up, w_down)
    return x_new, ctx_new


def setup_inputs(seed: int = 0) -> dict:
    key = jax.random.key(seed)
    ks = jax.random.split(key, 32)
    f32 = jnp.float32
    D = D_MODEL

    def nrm(k, shape, scale):
        return jax.random.normal(k, shape, f32) * scale

    return {
        'x': nrm(ks[0], (BATCH, SEQ, D), 1.0),
        'c': nrm(ks[1], (BATCH, D), 1.0),
        'ctx': nrm(ks[2], (BATCH, CTX_LEN, D), 1.0),
        'c_ctx': nrm(ks[3], (D,), 1.0),
        'ada_w': nrm(ks[4], (DEPTH, D, 6 * D), 0.5 * D ** -0.5),
        'ada_b': nrm(ks[5], (DEPTH, 6 * D), 0.02),
        'norm1': 1.0 + nrm(ks[6], (DEPTH, D), 0.02),
        'norm2': 1.0 + nrm(ks[7], (DEPTH, D), 0.02),
        'w_in': nrm(ks[8], (DEPTH, D, IN_W), D ** -0.5),
        'conv_w': nrm(ks[9], (DEPTH, 3, 3 * HYENA_W), 3 ** -0.5),
        'conv_b': nrm(ks[10], (DEPTH, 3 * HYENA_W), 0.02),
        'filt_w1': nrm(ks[11], (DEPTH, FILTER_EMB, FILTER_HIDDEN), FILTER_EMB ** -0.5),
        'filt_b1': nrm(ks[12], (DEPTH, FILTER_HIDDEN), 0.02),
        'filt_w2': nrm(ks[13], (DEPTH, FILTER_HIDDEN, FILTER_HIDDEN), FILTER_HIDDEN ** -0.5),
        'filt_b2': nrm(ks[14], (DEPTH, FILTER_HIDDEN), 0.02),
        'filt_w3': nrm(ks[15], (DEPTH, FILTER_HIDDEN, FILTER_HIDDEN), FILTER_HIDDEN ** -0.5),
        'filt_b3': nrm(ks[16], (DEPTH, FILTER_HIDDEN), 0.02),
        'filt_freq': 1.0 + nrm(ks[17], (DEPTH, FILTER_HIDDEN), 0.02),
        'filt_out': nrm(ks[18], (DEPTH, FILTER_HIDDEN, HYENA_ORDER * 2 * HYENA_W), 0.05 * FILTER_HIDDEN ** -0.5),
        'hyena_bias': nrm(ks[19], (DEPTH, HYENA_ORDER, HYENA_W), 0.5),
        'q_norm': 1.0 + nrm(ks[20], (DEPTH, HEAD_DIM), 0.02),
        'k_norm': 1.0 + nrm(ks[21], (DEPTH, HEAD_DIM), 0.02),
        'attn_sink': nrm(ks[22], (DEPTH, N_HEADS), 0.5),
        'w_branch_a': nrm(ks[23], (DEPTH, HYENA_W, D), HYENA_W ** -0.5),
        'w_branch_b': nrm(ks[24], (DEPTH, ATTN_W, D), ATTN_W ** -0.5),
        'w_out': nrm(ks[25], (DEPTH, D, D), D ** -0.5),
        'router': nrm(ks[26], (DEPTH, D, N_EXPERTS), D ** -0.5),
        'w_gate': nrm(ks[27], (DEPTH, N_EXPERTS, D, D_FF), D ** -0.5),
        'w_up': nrm(ks[28], (DEPTH, N_EXPERTS, D, D_FF), D ** -0.5),
        'w_down': nrm(ks[29], (DEPTH, N_EXPERTS, D_FF, D), D_FF ** -0.5),
    }


def reference(x, c, ctx, c_ctx, ada_w, ada_b, norm1, norm2, w_in, conv_w, conv_b,
              filt_w1, filt_b1, filt_w2, filt_b2, filt_w3, filt_b3, filt_freq, filt_out,
              hyena_bias, q_norm, k_norm, attn_sink, w_branch_a, w_branch_b, w_out,
              router, w_gate, w_up, w_down):
    for l in range(DEPTH):
        mod_x = (jax.nn.silu(c) @ ada_w[l] + ada_b[l])[:, None, :]
        mod_c = (jax.nn.silu(c_ctx) @ ada_w[l] + ada_b[l])[None, None, :]
        filt = (filt_w1[l], filt_b1[l], filt_w2[l], filt_b2[l], filt_w3[l], filt_b3[l], filt_freq[l], filt_out[l])
        x, ctx = _layer(x, ctx, mod_x, mod_c, norm1[l], norm2[l], w_in[l], conv_w[l], conv_b[l], filt,
                        hyena_bias[l], q_norm[l], k_norm[l], attn_sink[l], w_branch_a[l], w_branch_b[l],
                        w_out[l], router[l], w_gate[l], w_up[l], w_down[l], l < DEPTH - 1)
    return x
```

```python
import math

import jax
import jax.numpy as jnp
from jax import lax
from jax.experimental import pallas as pl
from jax.experimental.pallas import tpu as pltpu

F32 = jnp.float32
BF16 = jnp.bfloat16
I32 = jnp.int32
HIGHEST = lax.Precision.HIGHEST

D_MODEL = 1024
BATCH = 8
SEQ = 2048
GRID_W = 64
CTX_LEN = 256
N_HEADS = 8
N_KV_HEADS = 2
HEAD_DIM = 64
GROUP = N_HEADS // N_KV_HEADS
ATTN_W = N_HEADS * HEAD_DIM
KV_W = N_KV_HEADS * HEAD_DIM
WINDOW = 128
BLOCK = 128
HYENA_W = D_MODEL // 2
HYENA_ORDER = 2
FILTER_BANDS = 16
FILTER_EMB = 1 + 2 * FILTER_BANDS
FILTER_HIDDEN = 64
DECAY_TARGET = 1e-2
FAST_DECAY_PCT = 0.3
SLOW_DECAY_PCT = 1.5
ROPE_BASE = 10000.0
AXIS_ROT = HEAD_DIM // 2
N_EXPERTS = 16
EC_CAPACITY = 2
D_FF = 2048
EPS = 1e-6
NEG = -1e30

OFF_Q = 3 * HYENA_W
OFF_K = OFF_Q + ATTN_W
OFF_V = OFF_K + KV_W
OFF_G = OFF_V + KV_W
IN_W = OFF_G + 2 * D_MODEL

CAP = EC_CAPACITY * SEQ // N_EXPERTS
N_FFT = 2 * SEQ
MOD_ROWS = 16
LANES = 128

TM_IN = 512
TM_MERGE = 512
CT = 256
FC = 512
TF = 512
MC = 512
TS = 512
EXPERT_GROUP = 4
VMEM_LIMIT = 56 * 1024 * 1024


def _dot(a, b, precision=None):
    return jnp.dot(a, b, preferred_element_type=F32, precision=precision)


def _dot_nt(a, b, precision=None):
    return lax.dot_general(a, b, (((1,), (1,)), ((), ())), preferred_element_type=F32, precision=precision)


def _params(sem, vmem=VMEM_LIMIT):
    return pltpu.CompilerParams(dimension_semantics=sem, vmem_limit_bytes=vmem)


def _rms_mod(x, g, sc, sh):
    ms = jnp.mean(x * x, axis=-1, keepdims=True)
    return (x * lax.rsqrt(ms + EPS) * g) * (1.0 + sc) + sh


def _head_norm_rope(z, g, gmat, cos, sin, scale):
    ms = _dot((z * z).astype(BF16), gmat)
    y = z * lax.rsqrt(ms + EPS) * g
    if cos is not None:
        lane = lax.broadcasted_iota(I32, y.shape, 1)
        nxt = pltpu.roll(y, LANES - 1, axis=1)
        prv = pltpu.roll(y, 1, axis=1)
        y = y * cos + jnp.where((lane & 1) == 0, nxt, prv) * sin
    return y * scale


def _ada_body(c_ref, w_ref, b_ref, o_ref):
    c = c_ref[...]
    s = c * jax.nn.sigmoid(c)
    o_ref[...] = _dot(s, w_ref[...], HIGHEST) + b_ref[...]


def _ada(c16, w, b):
    d = D_MODEL
    return pl.pallas_call(
        _ada_body,
        grid=(6,),
        in_specs=[pl.BlockSpec((MOD_ROWS, d), lambda j: (0, 0)),
                  pl.BlockSpec((d, d), lambda j: (0, j)),
                  pl.BlockSpec((1, d), lambda j: (0, j))],
        out_specs=pl.BlockSpec((MOD_ROWS, d), lambda j: (0, j)),
        out_shape=jax.ShapeDtypeStruct((MOD_ROWS, 6 * d), F32),
        compiler_params=_params(("parallel",)),
        name="ada_mod",
    )(c16, w, b)


def _sign_rows(n):
    lane = lax.broadcasted_iota(I32, (8, n), 1)
    sub = lax.broadcasted_iota(I32, (8, n), 0)
    sg = jnp.where((lane & 1) == 0, 1.0, -1.0)
    return jnp.where(sub == 0, sg, 0.0).astype(BF16)


def _filt_body(feat_ref, w1_ref, b1_ref, w2_ref, b2_ref, w3_ref, b3_ref, fq_ref, fof_ref, fob_ref, dec_ref,
               c_ref, s_ref, kf_ref, kn_ref, h_ref):
    @pl.when((pl.program_id(0) == 0) & (pl.program_id(1) == 0))
    def _():
        fq = fq_ref[...]
        h = jnp.sin(fq * (_dot(feat_ref[...], w1_ref[...], HIGHEST) + b1_ref[...]))
        h = jnp.sin(fq * (_dot(h, w2_ref[...], HIGHEST) + b2_ref[...]))
        h_ref[...] = jnp.sin(fq * (_dot(h, w3_ref[...], HIGHEST) + b3_ref[...]))

    h = h_ref[...]
    dec = dec_ref[...]
    hf = _dot(h, fof_ref[...], HIGHEST) * dec
    hb = _dot(h, fob_ref[...], HIGHEST) * dec
    row = lax.broadcasted_iota(I32, hf.shape, 0)
    hb = jnp.where(row == 0, 0.0, hb)
    a = (hf + hb).astype(BF16)
    b = (hf - hb).astype(BF16)
    wrow = jnp.where(row == 0, 1.0 / N_FFT, 2.0 / N_FFT)
    kf_ref[0, 0] = _dot(c_ref[...], a) * wrow
    kf_ref[0, 1] = _dot(s_ref[...], b) * wrow
    kn_ref[0] = _dot(_sign_rows(SEQ), a)[0:1] * (1.0 / N_FFT)


def _filters(feat, w1, b1, w2, b2, w3, b3, fq, fout, decay, cmat, smat):
    nct = HYENA_W // CT
    full = lambda shape: pl.BlockSpec(shape, lambda o, c: (0,) * len(shape))
    return pl.pallas_call(
        _filt_body,
        grid=(HYENA_ORDER, nct),
        in_specs=[full((SEQ, FILTER_HIDDEN)), full((FILTER_HIDDEN, FILTER_HIDDEN)), full((1, FILTER_HIDDEN)),
                  full((FILTER_HIDDEN, FILTER_HIDDEN)), full((1, FILTER_HIDDEN)),
                  full((FILTER_HIDDEN, FILTER_HIDDEN)), full((1, FILTER_HIDDEN)), full((1, FILTER_HIDDEN)),
                  pl.BlockSpec((FILTER_HIDDEN, CT), lambda o, c: (0, (o * 2 + 0) * nct + c)),
                  pl.BlockSpec((FILTER_HIDDEN, CT), lambda o, c: (0, (o * 2 + 1) * nct + c)),
                  pl.BlockSpec((SEQ, CT), lambda o, c: (0, c)),
                  pl.BlockSpec((SEQ, SEQ), lambda o, c: (0, 0), pipeline_mode=pl.Buffered(1)),
                  pl.BlockSpec((SEQ, SEQ), lambda o, c: (0, 0), pipeline_mode=pl.Buffered(1))],
        out_specs=[pl.BlockSpec((1, 2, SEQ, CT), lambda o, c: (o, 0, 0, c)),
                   pl.BlockSpec((1, 1, CT), lambda o, c: (o, 0, c))],
        out_shape=[jax.ShapeDtypeStruct((HYENA_ORDER, 2, SEQ, HYENA_W), F32),
                   jax.ShapeDtypeStruct((HYENA_ORDER, 1, HYENA_W), F32)],
        scratch_shapes=[pltpu.VMEM((SEQ, FILTER_HIDDEN), F32)],
        compiler_params=_params(("arbitrary", "arbitrary")),
        name="hyena_filters",
    )(feat, w1, b1, w2, b2, w3, b3, fq, fout, fout, decay, cmat, smat)


def _inproj_body(x_ref, sc_ref, sh_ref, n1_ref, wh_ref, wq_ref, wkv_ref, wg_ref, gq_ref, gk_ref, gmat_ref,
                 cos_ref, sin_ref, zh_ref, q_ref, k_ref, v_ref, gate_ref):
    hx = _rms_mod(x_ref[0], n1_ref[...], sc_ref[0], sh_ref[0]).astype(BF16)
    zh_ref[0] = _dot(hx, wh_ref[...]).astype(BF16)
    gmat = gmat_ref[...]
    cos = cos_ref[...]
    sin = sin_ref[...]
    zq = _dot(hx, wq_ref[...])
    for s in range(ATTN_W // LANES):
        sl = slice(s * LANES, (s + 1) * LANES)
        q_ref[0, :, sl] = _head_norm_rope(zq[:, sl], gq_ref[:, sl], gmat, cos, sin, HEAD_DIM ** -0.5).astype(BF16)
    zkv = _dot(hx, wkv_ref[...])
    k_ref[0] = _head_norm_rope(zkv[:, :KV_W], gk_ref[...], gmat, cos, sin, 1.0).astype(BF16)
    v_ref[0] = zkv[:, KV_W:].astype(BF16)
    gate_ref[0] = jax.nn.sigmoid(_dot(hx, wg_ref[...])).astype(BF16)


def _inproj(x, mods3, norm1, wh, wq, wkv, wg, gq, gk, gmat, cos_t, sin_t):
    d = D_MODEL
    nt = SEQ // TM_IN
    const = lambda shape: pl.BlockSpec(shape, lambda b, i: (0,) * len(shape))
    tok = lambda w: pl.BlockSpec((1, TM_IN, w), lambda b, i: (b, i, 0))
    return pl.pallas_call(
        _inproj_body,
        grid=(BATCH, nt),
        in_specs=[tok(d),
                  pl.BlockSpec((1, 1, d), lambda b, i: (b, 0, 1)),
                  pl.BlockSpec((1, 1, d), lambda b, i: (b, 0, 0)),
                  const((1, d)), const((d, OFF_Q)), const((d, ATTN_W)), const((d, 2 * KV_W)), const((d, 2 * d)),
                  const((1, ATTN_W)), const((1, KV_W)), const((LANES, LANES)),
                  pl.BlockSpec((TM_IN, LANES), lambda b, i: (i, 0)),
                  pl.BlockSpec((TM_IN, LANES), lambda b, i: (i, 0))],
        out_specs=[tok(OFF_Q), tok(ATTN_W), tok(KV_W), tok(KV_W), tok(2 * d)],
        out_shape=[jax.ShapeDtypeStruct((BATCH, SEQ, OFF_Q), BF16),
                   jax.ShapeDtypeStruct((BATCH, SEQ, ATTN_W), BF16),
                   jax.ShapeDtypeStruct((BATCH, SEQ, KV_W), BF16),
                   jax.ShapeDtypeStruct((BATCH, SEQ, KV_W), BF16),
                   jax.ShapeDtypeStruct((BATCH, SEQ, 2 * d), BF16)],
        compiler_params=_params(("parallel", "parallel")),
        name="in_proj",
    )(x, mods3, mods3, norm1, wh, wq, wkv, wg, gq, gk, gmat, cos_t, sin_t)


def _ctx_body(c_ref, sc_ref, sh_ref, n1_ref, wkv_ref, gk_ref, gmat_ref, kc_ref, vc_ref):
    hc = _rms_mod(c_ref[0], n1_ref[...], sc_ref[0], sh_ref[0]).astype(BF16)
    z = _dot(hc, wkv_ref[...])
    kc_ref[0] = _head_norm_rope(z[:, :KV_W], gk_ref[...], gmat_ref[...], None, None, 1.0).astype(BF16)
    vc_ref[0] = z[:, KV_W:].astype(BF16)


def _ctx_proj(ctx, mods3, norm1, wkv, gk, gmat):
    d = D_MODEL
    const = lambda shape: pl.BlockSpec(shape, lambda b: (0,) * len(shape))
    return pl.pallas_call(
        _ctx_body,
        grid=(BATCH,),
        in_specs=[pl.BlockSpec((1, CTX_LEN, d), lambda b: (b, 0, 0)),
                  pl.BlockSpec((1, 1, d), lambda b: (BATCH, 0, 1)),
                  pl.BlockSpec((1, 1, d), lambda b: (BATCH, 0, 0)),
                  const((1, d)), const((d, 2 * KV_W)), const((1, KV_W)), const((LANES, LANES))],
        out_specs=[pl.BlockSpec((1, CTX_LEN, KV_W), lambda b: (b, 0, 0)),
                   pl.BlockSpec((1, CTX_LEN, KV_W), lambda b: (b, 0, 0))],
        out_shape=[jax.ShapeDtypeStruct((BATCH, CTX_LEN, KV_W), BF16),
                   jax.ShapeDtypeStruct((BATCH, CTX_LEN, KV_W), BF16)],
        compiler_params=_params(("parallel",)),
        name="ctx_proj",
    )(ctx, mods3, mods3, norm1, wkv, gk, gmat)


def _hyena_body(zv_ref, z1_ref, z2_ref, cwv_ref, cw1_ref, cw2_ref, cbv_ref, cb1_ref, cb2_ref, hb_ref, kf_ref,
                kn_ref, c_ref, s_ref, o_ref, y_ref, g_ref, yb_ref, zc_ref, zs_ref):
    row = lax.broadcasted_iota(I32, (SEQ, CT), 0)

    def short_conv(z_ref, w_ref, b_ref):
        z = z_ref[0].astype(F32)
        zm = jnp.where(row == 0, 0.0, pltpu.roll(z, 1, axis=0))
        zp = jnp.where(row == SEQ - 1, 0.0, pltpu.roll(z, SEQ - 1, axis=0))
        return zm * w_ref[0:1, :] + z * w_ref[1:2, :] + zp * w_ref[2:3, :] + b_ref[...]

    y_ref[...] = short_conv(zv_ref, cwv_ref, cbv_ref)
    sign8 = _sign_rows(SEQ)
    par = lax.broadcasted_iota(I32, (FC, CT), 0) & 1
    for o, (zr, cw, cb) in enumerate(((z1_ref, cw1_ref, cb1_ref), (z2_ref, cw2_ref, cb2_ref))):
        g_ref[...] = short_conv(zr, cw, cb)
        yb_ref[...] = y_ref[...].astype(BF16)

        def fwd(m, carry, o=o):
            r = pl.multiple_of(m * FC, FC)
            yb = yb_ref[...]
            xc = _dot(c_ref[pl.ds(r, FC), :], yb)
            xs = _dot(s_ref[pl.ds(r, FC), :], yb)
            kc = kf_ref[o, 0, pl.ds(r, FC), :]
            ks = kf_ref[o, 1, pl.ds(r, FC), :]
            zc_ref[pl.ds(r, FC), :] = (xc * kc - xs * ks).astype(BF16)
            zs_ref[pl.ds(r, FC), :] = (xc * ks + xs * kc).astype(BF16)
            return carry

        lax.fori_loop(0, SEQ // FC, fwd, 0)
        zn = _dot(sign8, yb_ref[...])[0:1] * kn_ref[o]
        bias = hb_ref[o:o + 1, :]

        def inv(m, carry, zn=zn, bias=bias):
            r = pl.multiple_of(m * FC, FC)
            y2 = _dot(c_ref[pl.ds(r, FC), :], zc_ref[...]) + _dot(s_ref[pl.ds(r, FC), :], zs_ref[...])
            y2 = y2 + jnp.where(par == 0, zn, -zn)
            y_ref[pl.ds(r, FC), :] = g_ref[pl.ds(r, FC), :] * (y2 + bias * y_ref[pl.ds(r, FC), :])
            return carry

        lax.fori_loop(0, SEQ // FC, inv, 0)
    o_ref[0] = y_ref[...].astype(BF16)


def _hyena(zh, conv_w, conv_b, hbias, kf, kn, cmat, smat):
    nct = HYENA_W // CT
    zspec = lambda k: pl.BlockSpec((1, SEQ, CT), lambda c, b: (b, 0, k * nct + c))
    wspec = lambda k: pl.BlockSpec((3, CT), lambda c, b: (0, k * nct + c))
    bspec = lambda k: pl.BlockSpec((1, CT), lambda c, b: (0, k * nct + c))
    return pl.pallas_call(
        _hyena_body,
        grid=(nct, BATCH),
        in_specs=[zspec(0), zspec(1), zspec(2), wspec(0), wspec(1), wspec(2), bspec(0), bspec(1), bspec(2),
                  pl.BlockSpec((HYENA_ORDER, CT), lambda c, b: (0, c)),
                  pl.BlockSpec((HYENA_ORDER, 2, SEQ, CT), lambda c, b: (0, 0, 0, c)),
                  pl.BlockSpec((HYENA_ORDER, 1, CT), lambda c, b: (0, 0, c)),
                  pl.BlockSpec((SEQ, SEQ), lambda c, b: (0, 0), pipeline_mode=pl.Buffered(1)),
                  pl.BlockSpec((SEQ, SEQ), lambda c, b: (0, 0), pipeline_mode=pl.Buffered(1))],
        out_specs=pl.BlockSpec((1, SEQ, CT), lambda c, b: (b, 0, c)),
        out_shape=jax.ShapeDtypeStruct((BATCH, SEQ, HYENA_W), BF16),
        scratch_shapes=[pltpu.VMEM((SEQ, CT), F32), pltpu.VMEM((SEQ, CT), F32), pltpu.VMEM((SEQ, CT), BF16),
                        pltpu.VMEM((SEQ, CT), BF16), pltpu.VMEM((SEQ, CT), BF16)],
        compiler_params=_params(("parallel", "parallel")),
        name="hyena_conv",
    )(zh, zh, zh, conv_w, conv_w, conv_w, conv_b, conv_b, conv_b, hbias, kf, kn, cmat, smat)


def _attn_body(sink_ref, q_ref, k_ref, v_ref, kc_ref, vc_ref, bias_ref, o_ref, kp_ref, vp_ref):
    nb = SEQ // BLOCK
    zpad = jnp.zeros((BLOCK, KV_W), BF16)
    kp_ref[0:BLOCK] = zpad
    kp_ref[BLOCK:BLOCK + SEQ] = k_ref[0]
    kp_ref[BLOCK + SEQ:] = zpad
    vp_ref[0:BLOCK] = zpad
    vp_ref[BLOCK:BLOCK + SEQ] = v_ref[0]
    vp_ref[BLOCK + SEQ:] = zpad
    lane = lax.broadcasted_iota(I32, (BLOCK, LANES), 1)
    low = lane < HEAD_DIM
    mask_lo = jnp.where(low, 1.0, 0.0).astype(BF16)
    mask_hi = jnp.where(low, 0.0, 1.0).astype(BF16)
    kc = kc_ref[0]
    vc = vc_ref[0]

    def block(n, carry):
        r = pl.multiple_of(n * BLOCK, BLOCK)
        kw = kp_ref[pl.ds(r, 3 * BLOCK), :]
        vw = vp_ref[pl.ds(r, 3 * BLOCK), :]
        bias = bias_ref[jnp.where(n == 0, 0, jnp.where(n == nb - 1, 2, 1))]
        scores = []
        for j in range(GROUP):
            qs = q_ref[0, pl.ds(r, BLOCK), j * LANES:(j + 1) * LANES]
            for msk in (mask_lo, mask_hi):
                qm = qs * msk
                scores.append((_dot_nt(qm, kw) + bias, _dot_nt(qm, kc)))
        probs = []
        for idx, (sw, sc) in enumerate(scores):
            snk = sink_ref[idx // 2 + GROUP * (idx % 2)]
            m = jnp.maximum(jnp.maximum(jnp.max(sw, axis=-1, keepdims=True),
                                        jnp.max(sc, axis=-1, keepdims=True)), snk)
            pw = jnp.exp(sw - m)
            pc = jnp.exp(sc - m)
            den = jnp.sum(pw, axis=-1, keepdims=True) + jnp.sum(pc, axis=-1, keepdims=True) + jnp.exp(snk - m)
            probs.append((pw.astype(BF16), pc.astype(BF16), 1.0 / den))
        outs = [(_dot(pw, vw) + _dot(pc, vc)) * rden for pw, pc, rden in probs]
        for j in range(GROUP):
            o_ref[0, pl.ds(r, BLOCK), j * LANES:(j + 1) * LANES] = jnp.where(
                low, outs[2 * j], outs[2 * j + 1]).astype(BF16)
        return carry

    lax.fori_loop(0, nb, block, 0)


def _attention(sink, q, k, v, kc, vc, bias):
    per_b = lambda n, w: pl.BlockSpec((1, n, w), lambda b: (b, 0, 0))
    return pl.pallas_call(
        _attn_body,
        grid=(BATCH,),
        in_specs=[pl.BlockSpec(memory_space=pltpu.SMEM),
                  per_b(SEQ, ATTN_W), per_b(SEQ, KV_W), per_b(SEQ, KV_W), per_b(CTX_LEN, KV_W), per_b(CTX_LEN, KV_W),
                  pl.BlockSpec((3, BLOCK, 3 * BLOCK), lambda b: (0, 0, 0))],
        out_specs=per_b(SEQ, ATTN_W),
        out_shape=jax.ShapeDtypeStruct((BATCH, SEQ, ATTN_W), BF16),
        scratch_shapes=[pltpu.VMEM((SEQ + 2 * BLOCK, KV_W), BF16), pltpu.VMEM((SEQ + 2 * BLOCK, KV_W), BF16)],
        compiler_params=_params(("parallel",)),
        name="window_attn",
    )(sink, q, k, v, kc, vc, bias)


def _merge_body(ya_ref, yb_ref, gt_ref, x_ref, g1_ref, sc2_ref, sh2_ref, n2_ref, wa_ref, wb_ref, wo_ref, rt_ref,
                xn_ref, h2_ref, aff_ref):
    d = D_MODEL
    ua = _dot(ya_ref[0], wa_ref[...])
    ub = _dot(yb_ref[0], wb_ref[...])
    u = gt_ref[0, :, :d].astype(F32) * ua + gt_ref[0, :, d:].astype(F32) * ub
    xn = x_ref[0] + g1_ref[0] * _dot(u.astype(BF16), wo_ref[...])
    xn_ref[0] = xn
    h2 = _rms_mod(xn, n2_ref[...], sc2_ref[0], sh2_ref[0])
    h2_ref[0] = h2.astype(BF16)
    logit = _dot_nt(rt_ref[...], h2, HIGHEST)
    e = jnp.exp(logit - jnp.max(logit, axis=0, keepdims=True))
    aff_ref[0] = e / jnp.sum(e, axis=0, keepdims=True)


def _merge(ya, yb, gates, x, mods3, norm2, wa, wb, wo, router_t):
    d = D_MODEL
    nt = SEQ // TM_MERGE
    const = lambda shape: pl.BlockSpec(shape, lambda b, i: (0,) * len(shape))
    tok = lambda w: pl.BlockSpec((1, TM_MERGE, w), lambda b, i: (b, i, 0))
    mod = lambda k: pl.BlockSpec((1, 1, d), lambda b, i: (b, 0, k))
    return pl.pallas_call(
        _merge_body,
        grid=(BATCH, nt),
        in_specs=[tok(HYENA_W), tok(ATTN_W), tok(2 * d), tok(d), mod(2), mod(4), mod(3), const((1, d)),
                  const((HYENA_W, d)), const((ATTN_W, d)), const((d, d)), const((N_EXPERTS, d))],
        out_specs=[tok(d), tok(d), pl.BlockSpec((1, N_EXPERTS, TM_MERGE), lambda b, i: (b, 0, i))],
        out_shape=[jax.ShapeDtypeStruct((BATCH, SEQ, d), F32),
                   jax.ShapeDtypeStruct((BATCH, SEQ, d), BF16),
                   jax.ShapeDtypeStruct((BATCH, N_EXPERTS, SEQ), F32)],
        compiler_params=_params(("parallel", "parallel")),
        name="merge_outproj",
    )(ya, yb, gates, x, mods3, mods3, mods3, norm2, wa, wb, wo, router_t)


def _thresh_body(aff_ref, lo_ref, hi_ref):
    aff = aff_ref[...]
    rows = aff.shape[0]
    bits = pltpu.bitcast(aff, I32)

    def bisect_bits(_, carry):
        lo, hi = carry
        mid = lo + ((hi - lo) >> 1)
        ge = jnp.sum((bits >= mid).astype(I32), axis=1, keepdims=True) >= CAP
        return jnp.where(ge, mid, lo), jnp.where(ge, hi, mid)

    lo0 = jnp.zeros((rows, 1), I32)
    hi0 = jnp.full((rows, 1), 0x3F800001, I32)
    thr_bits, _ = lax.fori_loop(0, 31, bisect_bits, (lo0, hi0))
    thr = pltpu.bitcast(thr_bits, F32)

    def bisect_val(_, carry):
        lo, hi = carry
        mid = 0.5 * (lo + hi)
        ge = jnp.sum(jnp.where(aff >= mid, 1.0, 0.0), axis=1, keepdims=True) >= CAP
        return jnp.where(ge, mid, lo), jnp.where(ge, hi, mid)

    lo, hi = lax.fori_loop(0, 30, bisect_val, (0.5 * thr, jnp.maximum(2.0 * thr, 1e-30)))
    lo_ref[...] = jnp.broadcast_to(lo, lo_ref.shape)
    hi_ref[...] = jnp.broadcast_to(hi, hi_ref.shape)


def _thresholds(aff_rows):
    rows = BATCH * N_EXPERTS
    return pl.pallas_call(
        _thresh_body,
        grid=(1,),
        in_specs=[pl.BlockSpec((rows, SEQ), lambda i: (0, 0))],
        out_specs=[pl.BlockSpec((rows, LANES), lambda i: (0, 0)), pl.BlockSpec((rows, LANES), lambda i: (0, 0))],
        out_shape=[jax.ShapeDtypeStruct((rows, LANES), F32), jax.ShapeDtypeStruct((rows, LANES), F32)],
        compiler_params=_params(("arbitrary",)),
        name="route_threshold",
    )(aff_rows)


def _route_body(aff_ref, lo_ref, hi_ref, h_ref, tri_ref, xin_ref, g_ref, pos_ref, pbuf_ref):
    aff = aff_ref[0]
    above = jnp.where(aff >= hi_ref[:, 0:1], 1.0, 0.0)
    band = jnp.where(aff >= lo_ref[:, 0:1], 1.0, 0.0) - above
    need = CAP - jnp.sum(above, axis=1, keepdims=True)
    tri = tri_ref[...]
    tie_rank = _dot(band.astype(BF16), tri)
    self = above + band * jnp.where(tie_rank < need, 1.0, 0.0)
    pos = _dot(self.astype(BF16), tri)
    posi = jnp.where(self > 0.5, pos.astype(I32), -1)
    pos_ref[0] = posi

    h = h_ref[0]
    slot = lax.broadcasted_iota(I32, (CAP, SEQ), 0)
    for grp in range(N_EXPERTS // EXPERT_GROUP):
        for i in range(EXPERT_GROUP):
            e = grp * EXPERT_GROUP + i
            hit = posi[e:e + 1, :] == slot
            pbuf_ref[i * CAP:(i + 1) * CAP, :] = jnp.where(hit, 1.0, 0.0).astype(BF16)
            g_ref[e] = jnp.sum(jnp.where(hit, aff[e:e + 1, :], 0.0), axis=1, keepdims=True)
        rows = _dot(pbuf_ref[...], h)
        for i in range(EXPERT_GROUP):
            xin_ref[grp * EXPERT_GROUP + i] = rows[i * CAP:(i + 1) * CAP].astype(BF16)


def _route(aff_t, lo, hi, h2, tri):
    d = D_MODEL
    return pl.pallas_call(
        _route_body,
        grid=(BATCH,),
        in_specs=[pl.BlockSpec((1, N_EXPERTS, SEQ), lambda b: (b, 0, 0)),
                  pl.BlockSpec((N_EXPERTS, LANES), lambda b: (b, 0)),
                  pl.BlockSpec((N_EXPERTS, LANES), lambda b: (b, 0)),
                  pl.BlockSpec((1, SEQ, d), lambda b: (b, 0, 0)),
                  pl.BlockSpec((SEQ, SEQ), lambda b: (0, 0), pipeline_mode=pl.Buffered(1))],
        out_specs=[pl.BlockSpec((N_EXPERTS, CAP, d), lambda b: (0, b, 0)),
                   pl.BlockSpec((N_EXPERTS, CAP, 1), lambda b: (0, b, 0)),
                   pl.BlockSpec((1, N_EXPERTS, SEQ), lambda b: (b, 0, 0))],
        out_shape=[jax.ShapeDtypeStruct((N_EXPERTS, BATCH * CAP, d), BF16),
                   jax.ShapeDtypeStruct((N_EXPERTS, BATCH * CAP, 1), F32),
                   jax.ShapeDtypeStruct((BATCH, N_EXPERTS, SEQ), I32)],
        scratch_shapes=[pltpu.VMEM((EXPERT_GROUP * CAP, SEQ), BF16)],
        compiler_params=_params(("parallel",)),
        name="route_gather",
    )(aff_t, lo, hi, h2, tri)


def _expert_body(x_ref, g_ref, wg_ref, wu_ref, wd_ref, o_ref, acc_ref, wgb_ref, wub_ref, wdb_ref):
    f = pl.program_id(1)
    wgb_ref[...] = wg_ref[0].astype(BF16)
    wub_ref[...] = wu_ref[0].astype(BF16)
    wdb_ref[...] = wd_ref[0].astype(BF16)

    @pl.when(f == 0)
    def _():
        acc_ref[...] = jnp.zeros_like(acc_ref)

    def chunk(m, carry):
        r = pl.multiple_of(m * MC, MC)
        xm = x_ref[0, pl.ds(r, MC), :]
        a = _dot(xm, wgb_ref[...])
        b = _dot(xm, wub_ref[...])
        hh = (a * jax.nn.sigmoid(a) * b).astype(BF16)
        acc_ref[pl.ds(r, MC), :] += _dot(hh, wdb_ref[...])
        return carry

    lax.fori_loop(0, BATCH * CAP // MC, chunk, 0)

    @pl.when(f == D_FF // TF - 1)
    def _():
        o_ref[0] = (acc_ref[...] * g_ref[0]).astype(BF16)


def _experts(xin, g, w_gate, w_up, w_down):
    d = D_MODEL
    rows = BATCH * CAP
    return pl.pallas_call(
        _expert_body,
        grid=(N_EXPERTS, D_FF // TF),
        in_specs=[pl.BlockSpec((1, rows, d), lambda e, f: (e, 0, 0)),
                  pl.BlockSpec((1, rows, 1), lambda e, f: (e, 0, 0)),
                  pl.BlockSpec((1, d, TF), lambda e, f: (e, 0, f)),
                  pl.BlockSpec((1, d, TF), lambda e, f: (e, 0, f)),
                  pl.BlockSpec((1, TF, d), lambda e, f: (e, f, 0))],
        out_specs=pl.BlockSpec((1, rows, d), lambda e, f: (e, 0, 0)),
        out_shape=jax.ShapeDtypeStruct((N_EXPERTS, rows, d), BF16),
        scratch_shapes=[pltpu.VMEM((rows, d), F32), pltpu.VMEM((d, TF), BF16), pltpu.VMEM((d, TF), BF16),
                        pltpu.VMEM((TF, d), BF16)],
        compiler_params=_params(("parallel", "arbitrary")),
        name="swiglu_experts",
    )(xin, g, w_gate, w_up, w_down)


def _scatter_body(pos_ref, y_ref, x_ref, g2_ref, emat_ref, o_ref):
    width = N_EXPERTS * CAP
    pc = pos_ref[0].astype(F32).astype(BF16)
    pb = _dot(pc, emat_ref[...])
    slot = (lax.broadcasted_iota(I32, (TS, width), 1) & (CAP - 1)).astype(F32)
    onehot = jnp.where(pb == slot, 1.0, 0.0).astype(BF16)
    y = y_ref[...].reshape(width, D_MODEL)
    o_ref[0] = x_ref[0] + g2_ref[0] * _dot(onehot, y)


def _scatter(pos_tok, y, xn, mods3, emat):
    d = D_MODEL
    return pl.pallas_call(
        _scatter_body,
        grid=(BATCH, SEQ // TS),
        in_specs=[pl.BlockSpec((1, TS, N_EXPERTS), lambda b, i: (b, i, 0)),
                  pl.BlockSpec((N_EXPERTS, CAP, d), lambda b, i: (0, b, 0)),
                  pl.BlockSpec((1, TS, d), lambda b, i: (b, i, 0)),
                  pl.BlockSpec((1, 1, d), lambda b, i: (b, 0, 5)),
                  pl.BlockSpec((N_EXPERTS, N_EXPERTS * CAP), lambda b, i: (0, 0))],
        out_specs=pl.BlockSpec((1, TS, d), lambda b, i: (b, i, 0)),
        out_shape=jax.ShapeDtypeStruct((BATCH, SEQ, d), F32),
        compiler_params=_params(("parallel", "parallel")),
        name="scatter_residual",
    )(pos_tok, y, xn, mods3, emat)


def _rope_tables():
    rows = SEQ // GRID_W
    row = jnp.repeat(jnp.arange(rows, dtype=F32), GRID_W)
    col = jnp.tile(jnp.arange(GRID_W, dtype=F32), rows)
    inv = ROPE_BASE ** (-jnp.arange(0, AXIS_ROT, 2, dtype=F32) / AXIS_ROT)
    ang = jnp.concatenate([row[:, None] * inv, col[:, None] * inv], axis=-1)
    cos = jnp.repeat(jnp.cos(ang), 2, axis=-1)
    sin = jnp.stack([-jnp.sin(ang), jnp.sin(ang)], axis=-1).reshape(SEQ, HEAD_DIM)
    reps = LANES // HEAD_DIM
    return jnp.tile(cos, (1, reps)), jnp.tile(sin, (1, reps))


def _dft_tables():
    step = 32
    f = jnp.arange(SEQ, dtype=I32)[:, None]
    unit = 2.0 * math.pi / N_FFT
    a = ((f * (step * jnp.arange(SEQ // step, dtype=I32))[None, :]) % N_FFT).astype(F32) * unit
    b = ((f * jnp.arange(step, dtype=I32)[None, :]) % N_FFT).astype(F32) * unit
    ca, sa, cb, sb = jnp.cos(a)[:, :, None], jnp.sin(a)[:, :, None], jnp.cos(b)[:, None, :], jnp.sin(b)[:, None, :]
    cmat = (ca * cb - sa * sb).reshape(SEQ, SEQ)
    smat = (sa * cb + ca * sb).reshape(SEQ, SEQ)
    return cmat.astype(BF16), smat.astype(BF16)


def _filter_features():
    t = jnp.linspace(0.0, 1.0, SEQ, dtype=F32)[:, None]
    w = 2.0 * math.pi * jnp.arange(SEQ, dtype=F32)[:, None] / SEQ
    fr = jnp.linspace(1e-4, FILTER_BANDS - 1, FILTER_BANDS, dtype=F32)[None, :]
    feat = jnp.concatenate([t, jnp.cos(fr * w), -jnp.sin(fr * w)], axis=-1)
    feat = jnp.pad(feat, ((0, 0), (0, FILTER_HIDDEN - FILTER_EMB)))
    min_decay = math.log(DECAY_TARGET) / SLOW_DECAY_PCT
    max_decay = math.log(DECAY_TARGET) / FAST_DECAY_PCT
    deltas = jnp.linspace(min_decay, max_decay, HYENA_W, dtype=F32)
    return feat, jnp.exp(-t * jnp.abs(deltas))


def _attn_bias():
    qi = jnp.arange(BLOCK)[:, None]
    kj = jnp.arange(3 * BLOCK)[None, :]
    band = jnp.abs(kj - BLOCK - qi) <= WINDOW
    first = band & (kj >= BLOCK)
    last = band & (kj < 2 * BLOCK)
    return jnp.where(jnp.stack([first, band, last]), 0.0, NEG).astype(F32)


def _head_pair_perm():
    cols = []
    for j in range(GROUP):
        for half in range(N_KV_HEADS):
            h = j + GROUP * half
            cols.extend(range(h * HEAD_DIM, (h + 1) * HEAD_DIM))
    return jnp.asarray(cols, dtype=I32)


def kernel(x, c, ctx, c_ctx, ada_w, ada_b, norm1, norm2, w_in, conv_w, conv_b, filt_w1, filt_b1, filt_w2, filt_b2,
           filt_w3, filt_b3, filt_freq, filt_out, hyena_bias, q_norm, k_norm, attn_sink, w_branch_a, w_branch_b,
           w_out, router, w_gate, w_up, w_down):
    d = D_MODEL
    assert ada_w.shape[0] == 1, "only the single-layer configuration is implemented"
    l = 0
    cos_t, sin_t = _rope_tables()
    cmat, smat = _dft_tables()
    feat, decay = _filter_features()
    bias = _attn_bias()
    perm = _head_pair_perm()
    gmat = jnp.kron(jnp.eye(LANES // HEAD_DIM, dtype=F32),
                    jnp.full((HEAD_DIM, HEAD_DIM), 1.0 / HEAD_DIM, F32)).astype(BF16)
    tri = (jnp.arange(SEQ)[:, None] < jnp.arange(SEQ)[None, :]).astype(BF16)
    emat = jnp.repeat(jnp.eye(N_EXPERTS, dtype=BF16), CAP, axis=1)
    c16 = jnp.zeros((MOD_ROWS, d), F32).at[:BATCH].set(c).at[BATCH].set(c_ctx)

    mods3 = _ada(c16, ada_w[l], ada_b[l][None, :]).reshape(MOD_ROWS, 1, 6 * d)
    n1 = norm1[l][None, :]
    w = w_in[l]
    wkv = w[:, OFF_K:OFF_G].astype(BF16)
    gk = jnp.tile(k_norm[l], N_KV_HEADS)[None, :]
    kf, kn = _filters(feat, jnp.pad(filt_w1[l], ((0, FILTER_HIDDEN - FILTER_EMB), (0, 0))), filt_b1[l][None, :],
                      filt_w2[l], filt_b2[l][None, :], filt_w3[l], filt_b3[l][None, :], filt_freq[l][None, :],
                      filt_out[l], decay, cmat, smat)
    kc, vc = _ctx_proj(ctx, mods3, n1, wkv, gk, gmat)
    zh, q, k, v, gates = _inproj(
        x, mods3, n1, w[:, :OFF_Q].astype(BF16), w[:, OFF_Q:OFF_K][:, perm].astype(BF16), wkv,
        w[:, OFF_G:].astype(BF16), jnp.tile(q_norm[l], N_HEADS)[None, :], gk, gmat, cos_t, sin_t)
    ya = _hyena(zh, conv_w[l], conv_b[l][None, :], hyena_bias[l], kf, kn, cmat, smat)
    yb = _attention(attn_sink[l], q, k, v, kc, vc, bias)
    xn, h2, aff_t = _merge(ya, yb, gates, x, mods3, norm2[l][None, :], w_branch_a[l].astype(BF16),
                           w_branch_b[l][perm].astype(BF16), w_out[l].astype(BF16), router[l].T)
    lo, hi = _thresholds(aff_t.reshape(BATCH * N_EXPERTS, SEQ))
    xin, g, pos = _route(aff_t, lo, hi, h2, tri)
    y = _experts(xin, g, w_gate[l], w_up[l], w_down[l])
    return _scatter(jnp.swapaxes(pos, 1, 2), y, xn, mods3, emat)
```

```python
import math

import jax
import jax.numpy as jnp
from jax import lax
from jax.experimental import pallas as pl
from jax.experimental.pallas import tpu as pltpu

F32 = jnp.float32
BF16 = jnp.bfloat16
I32 = jnp.int32
HIGHEST = lax.Precision.HIGHEST

D_MODEL = 1024
BATCH = 8
SEQ = 2048
GRID_W = 64
CTX_LEN = 256
N_HEADS = 8
N_KV_HEADS = 2
HEAD_DIM = 64
GROUP = N_HEADS // N_KV_HEADS
ATTN_W = N_HEADS * HEAD_DIM
KV_W = N_KV_HEADS * HEAD_DIM
WINDOW = 128
BLOCK = 128
HYENA_W = D_MODEL // 2
HYENA_ORDER = 2
FILTER_BANDS = 16
FILTER_EMB = 1 + 2 * FILTER_BANDS
FILTER_HIDDEN = 64
DECAY_TARGET = 1e-2
FAST_DECAY_PCT = 0.3
SLOW_DECAY_PCT = 1.5
ROPE_BASE = 10000.0
AXIS_ROT = HEAD_DIM // 2
N_EXPERTS = 16
EC_CAPACITY = 2
D_FF = 2048
EPS = 1e-6
NEG = -1e30

OFF_Q = 3 * HYENA_W
OFF_K = OFF_Q + ATTN_W
OFF_V = OFF_K + KV_W
OFF_G = OFF_V + KV_W
IN_W = OFF_G + 2 * D_MODEL

CAP = EC_CAPACITY * SEQ // N_EXPERTS
N_FFT = 2 * SEQ
HALF = SEQ // 2
MOD_ROWS = 16
LANES = 128

TM_IN = 512
TM_MERGE = 512
CT = 256
FC = 512
RB = 256
TF = 512
MC = 512
TS = 512
EXPERT_GROUP = 4
VMEM_LIMIT = 56 * 1024 * 1024


def _dot(a, b, precision=None):
    return jnp.dot(a, b, preferred_element_type=F32, precision=precision)


def _dot_nt(a, b, precision=None):
    return lax.dot_general(a, b, (((1,), (1,)), ((), ())), preferred_element_type=F32, precision=precision)


def _params(sem, vmem=VMEM_LIMIT):
    return pltpu.CompilerParams(dimension_semantics=sem, vmem_limit_bytes=vmem)


def _rms_mod(x, g, sc, sh):
    ms = jnp.mean(x * x, axis=-1, keepdims=True)
    return (x * lax.rsqrt(ms + EPS) * g) * (1.0 + sc) + sh


def _head_norm_rope(z, g, gmat, cos, sin, scale):
    ms = _dot((z * z).astype(BF16), gmat)
    y = z * lax.rsqrt(ms + EPS) * g
    if cos is not None:
        lane = lax.broadcasted_iota(I32, y.shape, 1)
        nxt = pltpu.roll(y, LANES - 1, axis=1)
        prv = pltpu.roll(y, 1, axis=1)
        y = y * cos + jnp.where((lane & 1) == 0, nxt, prv) * sin
    return y * scale


def _ada_body(c_ref, w_ref, b_ref, o_ref):
    c = c_ref[...]
    s = c * jax.nn.sigmoid(c)
    o_ref[...] = _dot(s, w_ref[...], HIGHEST) + b_ref[...]


def _ada(c16, w, b):
    d = D_MODEL
    return pl.pallas_call(
        _ada_body,
        grid=(6,),
        in_specs=[pl.BlockSpec((MOD_ROWS, d), lambda j: (0, 0)),
                  pl.BlockSpec((d, d), lambda j: (0, j)),
                  pl.BlockSpec((1, d), lambda j: (0, j))],
        out_specs=pl.BlockSpec((MOD_ROWS, d), lambda j: (0, j)),
        out_shape=jax.ShapeDtypeStruct((MOD_ROWS, 6 * d), F32),
        compiler_params=_params(("parallel",)),
        name="ada_mod",
    )(c16, w, b)


def _sign_rows(n):
    lane = lax.broadcasted_iota(I32, (8, n), 1)
    sub = lax.broadcasted_iota(I32, (8, n), 0)
    sg = jnp.where((lane & 1) == 0, 1.0, -1.0)
    return jnp.where(sub == 0, sg, 0.0).astype(BF16)


def _filt_body(feat_ref, w1_ref, b1_ref, w2_ref, b2_ref, w3_ref, b3_ref, fq_ref, fof_ref, fob_ref, dec_ref,
               rot_ref, t1_ref, t2_ref, kf_ref, kn_ref, h_ref):
    @pl.when((pl.program_id(0) == 0) & (pl.program_id(1) == 0))
    def _():
        fq = fq_ref[...]
        h = jnp.sin(fq * (_dot(feat_ref[...], w1_ref[...], HIGHEST) + b1_ref[...]))
        h = jnp.sin(fq * (_dot(h, w2_ref[...], HIGHEST) + b2_ref[...]))
        h_ref[...] = jnp.sin(fq * (_dot(h, w3_ref[...], HIGHEST) + b3_ref[...]))

    h = h_ref[...]
    dec = dec_ref[...]
    hf = _dot(h, fof_ref[...], HIGHEST) * dec
    hb = _dot(h, fob_ref[...], HIGHEST) * dec
    row = lax.broadcasted_iota(I32, hf.shape, 0)
    hb = jnp.where(row == 0, 0.0, hb)
    a = hf + hb
    b = hf - hb
    pa = (a[:HALF] + a[HALF:]).astype(BF16)
    ma = (a[:HALF] - a[HALF:]).astype(BF16)
    pb = (b[:HALF] + b[HALF:]).astype(BF16)
    mb = (b[:HALF] - b[HALF:]).astype(BF16)
    t1 = t1_ref[...]
    t2 = t2_ref[...]
    a1 = _dot(t1, pa)
    a2 = _dot(t2, ma)
    b1 = _dot(t1, pb)
    b2 = _dot(t2, mb)
    ce, se, co, so = rot_ref[0], rot_ref[1], rot_ref[2], rot_ref[3]
    kf_ref[0, 0] = a1[:HALF] * ce + a2[HALF:] * se
    kf_ref[0, 1] = b2[HALF:] * ce - b1[:HALF] * se
    kf_ref[0, 2] = a2[:HALF] * co + a1[HALF:] * so
    kf_ref[0, 3] = b1[HALF:] * co - b2[:HALF] * so
    kn_ref[0] = _dot(_sign_rows(HALF), ma)[0:1] * (1.0 / N_FFT)


def _filters(feat, w1, b1, w2, b2, w3, b3, fq, fout, decay, rot, t1, t2):
    nct = HYENA_W // CT
    full = lambda shape: pl.BlockSpec(shape, lambda o, c: (0,) * len(shape))
    return pl.pallas_call(
        _filt_body,
        grid=(HYENA_ORDER, nct),
        in_specs=[full((SEQ, FILTER_HIDDEN)), full((FILTER_HIDDEN, FILTER_HIDDEN)), full((1, FILTER_HIDDEN)),
                  full((FILTER_HIDDEN, FILTER_HIDDEN)), full((1, FILTER_HIDDEN)),
                  full((FILTER_HIDDEN, FILTER_HIDDEN)), full((1, FILTER_HIDDEN)), full((1, FILTER_HIDDEN)),
                  pl.BlockSpec((FILTER_HIDDEN, CT), lambda o, c: (0, (o * 2 + 0) * nct + c)),
                  pl.BlockSpec((FILTER_HIDDEN, CT), lambda o, c: (0, (o * 2 + 1) * nct + c)),
                  pl.BlockSpec((SEQ, CT), lambda o, c: (0, c)),
                  full((4, HALF, CT)),
                  pl.BlockSpec((SEQ, HALF), lambda o, c: (0, 0), pipeline_mode=pl.Buffered(1)),
                  pl.BlockSpec((SEQ, HALF), lambda o, c: (0, 0), pipeline_mode=pl.Buffered(1))],
        out_specs=[pl.BlockSpec((1, 4, HALF, CT), lambda o, c: (o, 0, 0, c)),
                   pl.BlockSpec((1, 1, CT), lambda o, c: (o, 0, c))],
        out_shape=[jax.ShapeDtypeStruct((HYENA_ORDER, 4, HALF, HYENA_W), F32),
                   jax.ShapeDtypeStruct((HYENA_ORDER, 1, HYENA_W), F32)],
        scratch_shapes=[pltpu.VMEM((SEQ, FILTER_HIDDEN), F32)],
        compiler_params=_params(("arbitrary", "arbitrary")),
        name="hyena_filters",
    )(feat, w1, b1, w2, b2, w3, b3, fq, fout, fout, decay, rot, t1, t2)


def _inproj_body(x_ref, sc_ref, sh_ref, n1_ref, wh_ref, wq_ref, wkv_ref, wg_ref, gq_ref, gk_ref, gmat_ref,
                 cos_ref, sin_ref, zh_ref, q_ref, k_ref, v_ref, gate_ref):
    hx = _rms_mod(x_ref[0], n1_ref[...], sc_ref[0], sh_ref[0]).astype(BF16)
    zh_ref[0] = _dot(hx, wh_ref[...]).astype(BF16)
    gmat = gmat_ref[...]
    cos = cos_ref[...]
    sin = sin_ref[...]
    zq = _dot(hx, wq_ref[...])
    for s in range(ATTN_W // LANES):
        sl = slice(s * LANES, (s + 1) * LANES)
        q_ref[0, :, sl] = _head_norm_rope(zq[:, sl], gq_ref[:, sl], gmat, cos, sin, HEAD_DIM ** -0.5).astype(BF16)
    zkv = _dot(hx, wkv_ref[...])
    k_ref[0] = _head_norm_rope(zkv[:, :KV_W], gk_ref[...], gmat, cos, sin, 1.0).astype(BF16)
    v_ref[0] = zkv[:, KV_W:].astype(BF16)
    gate_ref[0] = jax.nn.sigmoid(_dot(hx, wg_ref[...])).astype(BF16)


def _inproj(x, mods3, norm1, wh, wq, wkv, wg, gq, gk, gmat, cos_t, sin_t):
    d = D_MODEL
    nt = SEQ // TM_IN
    const = lambda shape: pl.BlockSpec(shape, lambda b, i: (0,) * len(shape))
    tok = lambda w: pl.BlockSpec((1, TM_IN, w), lambda b, i: (b, i, 0))
    return pl.pallas_call(
        _inproj_body,
        grid=(BATCH, nt),
        in_specs=[tok(d),
                  pl.BlockSpec((1, 1, d), lambda b, i: (b, 0, 1)),
                  pl.BlockSpec((1, 1, d), lambda b, i: (b, 0, 0)),
                  const((1, d)), const((d, OFF_Q)), const((d, ATTN_W)), const((d, 2 * KV_W)), const((d, 2 * d)),
                  const((1, ATTN_W)), const((1, KV_W)), const((LANES, LANES)),
                  pl.BlockSpec((TM_IN, LANES), lambda b, i: (i, 0)),
                  pl.BlockSpec((TM_IN, LANES), lambda b, i: (i, 0))],
        out_specs=[tok(OFF_Q), tok(ATTN_W), tok(KV_W), tok(KV_W), tok(2 * d)],
        out_shape=[jax.ShapeDtypeStruct((BATCH, SEQ, OFF_Q), BF16),
                   jax.ShapeDtypeStruct((BATCH, SEQ, ATTN_W), BF16),
                   jax.ShapeDtypeStruct((BATCH, SEQ, KV_W), BF16),
                   jax.ShapeDtypeStruct((BATCH, SEQ, KV_W), BF16),
                   jax.ShapeDtypeStruct((BATCH, SEQ, 2 * d), BF16)],
        compiler_params=_params(("parallel", "parallel")),
        name="in_proj",
    )(x, mods3, mods3, norm1, wh, wq, wkv, wg, gq, gk, gmat, cos_t, sin_t)


def _ctx_body(c_ref, sc_ref, sh_ref, n1_ref, wkv_ref, gk_ref, gmat_ref, kc_ref, vc_ref):
    hc = _rms_mod(c_ref[0], n1_ref[...], sc_ref[0], sh_ref[0]).astype(BF16)
    z = _dot(hc, wkv_ref[...])
    kc_ref[0] = _head_norm_rope(z[:, :KV_W], gk_ref[...], gmat_ref[...], None, None, 1.0).astype(BF16)
    vc_ref[0] = z[:, KV_W:].astype(BF16)


def _ctx_proj(ctx, mods3, norm1, wkv, gk, gmat):
    d = D_MODEL
    const = lambda shape: pl.BlockSpec(shape, lambda b: (0,) * len(shape))
    return pl.pallas_call(
        _ctx_body,
        grid=(BATCH,),
        in_specs=[pl.BlockSpec((1, CTX_LEN, d), lambda b: (b, 0, 0)),
                  pl.BlockSpec((1, 1, d), lambda b: (BATCH, 0, 1)),
                  pl.BlockSpec((1, 1, d), lambda b: (BATCH, 0, 0)),
                  const((1, d)), const((d, 2 * KV_W)), const((1, KV_W)), const((LANES, LANES))],
        out_specs=[pl.BlockSpec((1, CTX_LEN, KV_W), lambda b: (b, 0, 0)),
                   pl.BlockSpec((1, CTX_LEN, KV_W), lambda b: (b, 0, 0))],
        out_shape=[jax.ShapeDtypeStruct((BATCH, CTX_LEN, KV_W), BF16),
                   jax.ShapeDtypeStruct((BATCH, CTX_LEN, KV_W), BF16)],
        compiler_params=_params(("parallel",)),
        name="ctx_proj",
    )(ctx, mods3, mods3, norm1, wkv, gk, gmat)


def _hyena_body(zv_ref, z1_ref, z2_ref, cwv_ref, cw1_ref, cw2_ref, cbv_ref, cb1_ref, cb2_ref, hb_ref, kf_ref,
                kn_ref, t1_ref, t2_ref, t1t_ref, t2t_ref, o_ref, lo_ref, hi_ref, glo_ref, ghi_ref, p_ref, m_ref,
                za_ref, zb_ref):
    row = lax.broadcasted_iota(I32, (HALF, CT), 0)
    rr = lax.broadcasted_iota(I32, (RB, RB), 0)
    cc = lax.broadcasted_iota(I32, (RB, RB), 1)
    flip = jnp.where(rr + cc == RB - 1, 1.0, 0.0).astype(BF16)
    nrb = HALF // RB

    def folded_short_conv(z_ref, w_ref, b_ref, lo_out, hi_out):
        zlo = z_ref[0, 0:HALF, :].astype(F32)
        for j in range(nrb):
            hi_out[j * RB:(j + 1) * RB, :] = _dot(flip, z_ref[0, SEQ - RB * (j + 1):SEQ - RB * j, :])
        zhi = hi_out[...]
        w0, w1, w2 = w_ref[0:1, :], w_ref[1:2, :], w_ref[2:3, :]
        first, last = row == 0, row == HALF - 1
        lo_prev = jnp.where(first, 0.0, pltpu.roll(zlo, 1, axis=0))
        lo_next = jnp.where(last, zhi[HALF - 1:HALF, :], pltpu.roll(zlo, HALF - 1, axis=0))
        hi_prev = jnp.where(first, 0.0, pltpu.roll(zhi, 1, axis=0))
        hi_next = jnp.where(last, zlo[HALF - 1:HALF, :], pltpu.roll(zhi, HALF - 1, axis=0))
        lo_out[...] = lo_prev * w0 + zlo * w1 + lo_next * w2 + b_ref[...]
        hi_out[...] = hi_next * w0 + zhi * w1 + hi_prev * w2 + b_ref[...]

    folded_short_conv(zv_ref, cwv_ref, cbv_ref, lo_ref, hi_ref)
    sign8 = _sign_rows(HALF)
    odd = (lax.broadcasted_iota(I32, (FC, CT), 0) & 1) == 1
    for o, (zr, cw, cb) in enumerate(((z1_ref, cw1_ref, cb1_ref), (z2_ref, cw2_ref, cb2_ref))):
        folded_short_conv(zr, cw, cb, glo_ref, ghi_ref)
        p_ref[...] = (lo_ref[...] + hi_ref[...]).astype(BF16)
        m_ref[...] = (lo_ref[...] - hi_ref[...]).astype(BF16)
        pv = p_ref[...]
        mv = m_ref[...]
        for c in range(HALF // FC):
            ev = slice(c * FC, (c + 1) * FC)
            od = slice(HALF + c * FC, HALF + (c + 1) * FC)
            xce = _dot(t1_ref[ev, :], pv)
            xso = _dot(t1_ref[od, :], pv)
            xco = _dot(t2_ref[ev, :], mv)
            xse = _dot(t2_ref[od, :], mv)
            kce, kse, kco, kso = kf_ref[o, 0, ev, :], kf_ref[o, 1, ev, :], kf_ref[o, 2, ev, :], kf_ref[o, 3, ev, :]
            za_ref[ev, :] = (xce * kce - xse * kse).astype(BF16)
            za_ref[od, :] = (xco * kso + xso * kco).astype(BF16)
            zb_ref[ev, :] = (xco * kco - xso * kso).astype(BF16)
            zb_ref[od, :] = (xce * kse + xse * kce).astype(BF16)
        zn = _dot(sign8, mv)[0:1] * kn_ref[o]
        bias = hb_ref[o:o + 1, :]
        za = za_ref[...]
        zb = zb_ref[...]
        for c in range(HALF // FC):
            rs = slice(c * FC, (c + 1) * FC)
            half_p = _dot(t1t_ref[rs, :], za)
            half_m = _dot(t2t_ref[rs, :], zb) + jnp.where(odd, -zn, zn)
            lo_ref[rs, :] = glo_ref[rs, :] * (half_p + half_m + bias * lo_ref[rs, :])
            hi_ref[rs, :] = ghi_ref[rs, :] * (half_p - half_m + bias * hi_ref[rs, :])
    o_ref[0, 0:HALF, :] = lo_ref[...].astype(BF16)
    for j in range(nrb):
        o_ref[0, SEQ - RB * (j + 1):SEQ - RB * j, :] = _dot(
            flip, hi_ref[j * RB:(j + 1) * RB, :].astype(BF16)).astype(BF16)


def _hyena(zh, conv_w, conv_b, hbias, kf, kn, t1, t2, t1t, t2t):
    nct = HYENA_W // CT
    zspec = lambda k: pl.BlockSpec((1, SEQ, CT), lambda c, b: (b, 0, k * nct + c))
    wspec = lambda k: pl.BlockSpec((3, CT), lambda c, b: (0, k * nct + c))
    bspec = lambda k: pl.BlockSpec((1, CT), lambda c, b: (0, k * nct + c))
    table = lambda shape: pl.BlockSpec(shape, lambda c, b: (0, 0), pipeline_mode=pl.Buffered(1))
    half_f32 = pltpu.VMEM((HALF, CT), F32)
    return pl.pallas_call(
        _hyena_body,
        grid=(nct, BATCH),
        in_specs=[zspec(0), zspec(1), zspec(2), wspec(0), wspec(1), wspec(2), bspec(0), bspec(1), bspec(2),
                  pl.BlockSpec((HYENA_ORDER, CT), lambda c, b: (0, c)),
                  pl.BlockSpec((HYENA_ORDER, 4, HALF, CT), lambda c, b: (0, 0, 0, c)),
                  pl.BlockSpec((HYENA_ORDER, 1, CT), lambda c, b: (0, 0, c)),
                  table((SEQ, HALF)), table((SEQ, HALF)), table((HALF, SEQ)), table((HALF, SEQ))],
        out_specs=pl.BlockSpec((1, SEQ, CT), lambda c, b: (b, 0, c)),
        out_shape=jax.ShapeDtypeStruct((BATCH, SEQ, HYENA_W), BF16),
        scratch_shapes=[half_f32, half_f32, half_f32, half_f32,
                        pltpu.VMEM((HALF, CT), BF16), pltpu.VMEM((HALF, CT), BF16),
                        pltpu.VMEM((SEQ, CT), BF16), pltpu.VMEM((SEQ, CT), BF16)],
        compiler_params=_params(("parallel", "parallel")),
        name="hyena_conv",
    )(zh, zh, zh, conv_w, conv_w, conv_w, conv_b, conv_b, conv_b, hbias, kf, kn, t1, t2, t1t, t2t)


def _attn_body(sink_ref, q_ref, k_ref, v_ref, kc_ref, vc_ref, bias_ref, o_ref, kp_ref, vp_ref):
    nb = SEQ // BLOCK
    zpad = jnp.zeros((BLOCK, KV_W), BF16)
    kp_ref[0:BLOCK] = zpad
    kp_ref[BLOCK:BLOCK + SEQ] = k_ref[0]
    kp_ref[BLOCK + SEQ:] = zpad
    vp_ref[0:BLOCK] = zpad
    vp_ref[BLOCK:BLOCK + SEQ] = v_ref[0]
    vp_ref[BLOCK + SEQ:] = zpad
    lane = lax.broadcasted_iota(I32, (BLOCK, LANES), 1)
    low = lane < HEAD_DIM
    mask_lo = jnp.where(low, 1.0, 0.0).astype(BF16)
    mask_hi = jnp.where(low, 0.0, 1.0).astype(BF16)
    kc = kc_ref[0]
    vc = vc_ref[0]

    def block(n, carry):
        r = pl.multiple_of(n * BLOCK, BLOCK)
        kw = kp_ref[pl.ds(r, 3 * BLOCK), :]
        vw = vp_ref[pl.ds(r, 3 * BLOCK), :]
        bias = bias_ref[jnp.where(n == 0, 0, jnp.where(n == nb - 1, 2, 1))]
        scores = []
        for j in range(GROUP):
            qs = q_ref[0, pl.ds(r, BLOCK), j * LANES:(j + 1) * LANES]
            for msk in (mask_lo, mask_hi):
                qm = qs * msk
                scores.append((_dot_nt(qm, kw) + bias, _dot_nt(qm, kc)))
        probs = []
        for idx, (sw, sc) in enumerate(scores):
            snk = sink_ref[idx // 2 + GROUP * (idx % 2)]
            m = jnp.maximum(jnp.maximum(jnp.max(sw, axis=-1, keepdims=True),
                                        jnp.max(sc, axis=-1, keepdims=True)), snk)
            pw = jnp.exp(sw - m)
            pc = jnp.exp(sc - m)
            den = jnp.sum(pw, axis=-1, keepdims=True) + jnp.sum(pc, axis=-1, keepdims=True) + jnp.exp(snk - m)
            probs.append((pw.astype(BF16), pc.astype(BF16), 1.0 / den))
        outs = [(_dot(pw, vw) + _dot(pc, vc)) * rden for pw, pc, rden in probs]
        for j in range(GROUP):
            o_ref[0, pl.ds(r, BLOCK), j * LANES:(j + 1) * LANES] = jnp.where(
                low, outs[2 * j], outs[2 * j + 1]).astype(BF16)
        return carry

    lax.fori_loop(0, nb, block, 0)


def _attention(sink, q, k, v, kc, vc, bias):
    per_b = lambda n, w: pl.BlockSpec((1, n, w), lambda b: (b, 0, 0))
    return pl.pallas_call(
        _attn_body,
        grid=(BATCH,),
        in_specs=[pl.BlockSpec(memory_space=pltpu.SMEM),
                  per_b(SEQ, ATTN_W), per_b(SEQ, KV_W), per_b(SEQ, KV_W), per_b(CTX_LEN, KV_W), per_b(CTX_LEN, KV_W),
                  pl.BlockSpec((3, BLOCK, 3 * BLOCK), lambda b: (0, 0, 0))],
        out_specs=per_b(SEQ, ATTN_W),
        out_shape=jax.ShapeDtypeStruct((BATCH, SEQ, ATTN_W), BF16),
        scratch_shapes=[pltpu.VMEM((SEQ + 2 * BLOCK, KV_W), BF16), pltpu.VMEM((SEQ + 2 * BLOCK, KV_W), BF16)],
        compiler_params=_params(("parallel",)),
        name="window_attn",
    )(sink, q, k, v, kc, vc, bias)


def _merge_body(ya_ref, yb_ref, gt_ref, x_ref, g1_ref, sc2_ref, sh2_ref, n2_ref, wa_ref, wb_ref, wo_ref, rh_ref,
                rl_ref, xn_ref, h2_ref, aff_ref):
    d = D_MODEL
    ua = _dot(ya_ref[0], wa_ref[...])
    ub = _dot(yb_ref[0], wb_ref[...])
    u = gt_ref[0, :, :d].astype(F32) * ua + gt_ref[0, :, d:].astype(F32) * ub
    xn = x_ref[0] + g1_ref[0] * _dot(u.astype(BF16), wo_ref[...])
    xn_ref[0] = xn
    h2 = _rms_mod(xn, n2_ref[...], sc2_ref[0], sh2_ref[0])
    h2_hi = h2.astype(BF16)
    h2_ref[0] = h2_hi
    h2_lo = (h2 - h2_hi.astype(F32)).astype(BF16)
    rh = rh_ref[...]
    logit = (_dot(h2_hi, rh) + _dot(h2_lo, rh) + _dot(h2_hi, rl_ref[...])).T[:N_EXPERTS, :]
    e = jnp.exp(logit - jnp.max(logit, axis=0, keepdims=True))
    aff_ref[0] = e / jnp.sum(e, axis=0, keepdims=True)


def _merge(ya, yb, gates, x, mods3, norm2, wa, wb, wo, router_hi, router_lo):
    d = D_MODEL
    nt = SEQ // TM_MERGE
    const = lambda shape: pl.BlockSpec(shape, lambda b, i: (0,) * len(shape))
    tok = lambda w: pl.BlockSpec((1, TM_MERGE, w), lambda b, i: (b, i, 0))
    mod = lambda k: pl.BlockSpec((1, 1, d), lambda b, i: (b, 0, k))
    return pl.pallas_call(
        _merge_body,
        grid=(BATCH, nt),
        in_specs=[tok(HYENA_W), tok(ATTN_W), tok(2 * d), tok(d), mod(2), mod(4), mod(3), const((1, d)),
                  const((HYENA_W, d)), const((ATTN_W, d)), const((d, d)), const((d, LANES)), const((d, LANES))],
        out_specs=[tok(d), tok(d), pl.BlockSpec((1, N_EXPERTS, TM_MERGE), lambda b, i: (b, 0, i))],
        out_shape=[jax.ShapeDtypeStruct((BATCH, SEQ, d), F32),
                   jax.ShapeDtypeStruct((BATCH, SEQ, d), BF16),
                   jax.ShapeDtypeStruct((BATCH, N_EXPERTS, SEQ), F32)],
        compiler_params=_params(("parallel", "parallel")),
        name="merge_outproj",
    )(ya, yb, gates, x, mods3, mods3, mods3, norm2, wa, wb, wo, router_hi, router_lo)


def _thresh_body(aff_ref, lo_ref, hi_ref):
    aff = aff_ref[...]
    rows = aff.shape[0]
    bits = pltpu.bitcast(aff, I32)

    def bisect_bits(_, carry):
        lo, hi = carry
        mid = lo + ((hi - lo) >> 1)
        ge = jnp.sum((bits >= mid).astype(I32), axis=1, keepdims=True) >= CAP
        return jnp.where(ge, mid, lo), jnp.where(ge, hi, mid)

    lo0 = jnp.zeros((rows, 1), I32)
    hi0 = jnp.full((rows, 1), 0x3F800001, I32)
    thr_bits, _ = lax.fori_loop(0, 31, bisect_bits, (lo0, hi0))
    thr = pltpu.bitcast(thr_bits, F32)

    def bisect_val(_, carry):
        lo, hi = carry
        mid = 0.5 * (lo + hi)
        ge = jnp.sum(jnp.where(aff >= mid, 1.0, 0.0), axis=1, keepdims=True) >= CAP
        return jnp.where(ge, mid, lo), jnp.where(ge, hi, mid)

    lo, hi = lax.fori_loop(0, 30, bisect_val, (0.5 * thr, jnp.maximum(2.0 * thr, 1e-30)))
    lo_ref[...] = jnp.broadcast_to(lo, lo_ref.shape)
    hi_ref[...] = jnp.broadcast_to(hi, hi_ref.shape)


def _thresholds(aff_rows):
    rows = BATCH * N_EXPERTS
    return pl.pallas_call(
        _thresh_body,
        grid=(1,),
        in_specs=[pl.BlockSpec((rows, SEQ), lambda i: (0, 0))],
        out_specs=[pl.BlockSpec((rows, LANES), lambda i: (0, 0)), pl.BlockSpec((rows, LANES), lambda i: (0, 0))],
        out_shape=[jax.ShapeDtypeStruct((rows, LANES), F32), jax.ShapeDtypeStruct((rows, LANES), F32)],
        compiler_params=_params(("arbitrary",)),
        name="route_threshold",
    )(aff_rows)


def _route_body(aff_ref, lo_ref, hi_ref, h_ref, tri_ref, xin_ref, g_ref, pos_ref, pbuf_ref):
    aff = aff_ref[0]
    above = jnp.where(aff >= hi_ref[:, 0:1], 1.0, 0.0)
    band = jnp.where(aff >= lo_ref[:, 0:1], 1.0, 0.0) - above
    need = CAP - jnp.sum(above, axis=1, keepdims=True)
    tri = tri_ref[...]
    tie_rank = _dot(band.astype(BF16), tri)
    self = above + band * jnp.where(tie_rank < need, 1.0, 0.0)
    pos = _dot(self.astype(BF16), tri)
    posi = jnp.where(self > 0.5, pos.astype(I32), -1)
    pos_ref[0] = posi

    h = h_ref[0]
    slot = lax.broadcasted_iota(I32, (CAP, SEQ), 0)
    for grp in range(N_EXPERTS // EXPERT_GROUP):
        for i in range(EXPERT_GROUP):
            e = grp * EXPERT_GROUP + i
            hit = posi[e:e + 1, :] == slot
            pbuf_ref[i * CAP:(i + 1) * CAP, :] = jnp.where(hit, 1.0, 0.0).astype(BF16)
            g_ref[e] = jnp.sum(jnp.where(hit, aff[e:e + 1, :], 0.0), axis=1, keepdims=True)
        rows = _dot(pbuf_ref[...], h)
        for i in range(EXPERT_GROUP):
            xin_ref[grp * EXPERT_GROUP + i] = rows[i * CAP:(i + 1) * CAP].astype(BF16)


def _route(aff_t, lo, hi, h2, tri):
    d = D_MODEL
    return pl.pallas_call(
        _route_body,
        grid=(BATCH,),
        in_specs=[pl.BlockSpec((1, N_EXPERTS, SEQ), lambda b: (b, 0, 0)),
                  pl.BlockSpec((N_EXPERTS, LANES), lambda b: (b, 0)),
                  pl.BlockSpec((N_EXPERTS, LANES), lambda b: (b, 0)),
                  pl.BlockSpec((1, SEQ, d), lambda b: (b, 0, 0)),
                  pl.BlockSpec((SEQ, SEQ), lambda b: (0, 0), pipeline_mode=pl.Buffered(1))],
        out_specs=[pl.BlockSpec((N_EXPERTS, CAP, d), lambda b: (0, b, 0)),
                   pl.BlockSpec((N_EXPERTS, CAP, 1), lambda b: (0, b, 0)),
                   pl.BlockSpec((1, N_EXPERTS, SEQ), lambda b: (b, 0, 0))],
        out_shape=[jax.ShapeDtypeStruct((N_EXPERTS, BATCH * CAP, d), BF16),
                   jax.ShapeDtypeStruct((N_EXPERTS, BATCH * CAP, 1), F32),
                   jax.ShapeDtypeStruct((BATCH, N_EXPERTS, SEQ), I32)],
        scratch_shapes=[pltpu.VMEM((EXPERT_GROUP * CAP, SEQ), BF16)],
        compiler_params=_params(("parallel",)),
        name="route_gather",
    )(aff_t, lo, hi, h2, tri)


def _expert_body(x_ref, g_ref, wg_ref, wu_ref, wd_ref, o_ref, acc_ref, wgb_ref, wub_ref, wdb_ref):
    f = pl.program_id(1)
    wgb_ref[...] = wg_ref[0].astype(BF16)
    wub_ref[...] = wu_ref[0].astype(BF16)
    wdb_ref[...] = wd_ref[0].astype(BF16)

    @pl.when(f == 0)
    def _():
        acc_ref[...] = jnp.zeros_like(acc_ref)

    def chunk(m, carry):
        r = pl.multiple_of(m * MC, MC)
        xm = x_ref[0, pl.ds(r, MC), :]
        a = _dot(xm, wgb_ref[...])
        b = _dot(xm, wub_ref[...])
        hh = (a * jax.nn.sigmoid(a) * b).astype(BF16)
        acc_ref[pl.ds(r, MC), :] += _dot(hh, wdb_ref[...])
        return carry

    lax.fori_loop(0, BATCH * CAP // MC, chunk, 0)

    @pl.when(f == D_FF // TF - 1)
    def _():
        o_ref[0] = (acc_ref[...] * g_ref[0]).astype(BF16)


def _experts(xin, g, w_gate, w_up, w_down):
    d = D_MODEL
    rows = BATCH * CAP
    return pl.pallas_call(
        _expert_body,
        grid=(N_EXPERTS, D_FF // TF),
        in_specs=[pl.BlockSpec((1, rows, d), lambda e, f: (e, 0, 0)),
                  pl.BlockSpec((1, rows, 1), lambda e, f: (e, 0, 0)),
                  pl.BlockSpec((1, d, TF), lambda e, f: (e, 0, f)),
                  pl.BlockSpec((1, d, TF), lambda e, f: (e, 0, f)),
                  pl.BlockSpec((1, TF, d), lambda e, f: (e, f, 0))],
        out_specs=pl.BlockSpec((1, rows, d), lambda e, f: (e, 0, 0)),
        out_shape=jax.ShapeDtypeStruct((N_EXPERTS, rows, d), BF16),
        scratch_shapes=[pltpu.VMEM((rows, d), F32), pltpu.VMEM((d, TF), BF16), pltpu.VMEM((d, TF), BF16),
                        pltpu.VMEM((TF, d), BF16)],
        compiler_params=_params(("parallel", "arbitrary")),
        name="swiglu_experts",
    )(xin, g, w_gate, w_up, w_down)


def _scatter_body(pos_ref, y_ref, x_ref, g2_ref, emat_ref, o_ref):
    width = N_EXPERTS * CAP
    pc = pos_ref[0].astype(F32).astype(BF16)
    pb = _dot(pc, emat_ref[...])
    slot = (lax.broadcasted_iota(I32, (TS, width), 1) & (CAP - 1)).astype(F32)
    onehot = jnp.where(pb == slot, 1.0, 0.0).astype(BF16)
    y = y_ref[...].reshape(width, D_MODEL)
    o_ref[0] = x_ref[0] + g2_ref[0] * _dot(onehot, y)


def _scatter(pos_tok, y, xn, mods3, emat):
    d = D_MODEL
    return pl.pallas_call(
        _scatter_body,
        grid=(BATCH, SEQ // TS),
        in_specs=[pl.BlockSpec((1, TS, N_EXPERTS), lambda b, i: (b, i, 0)),
                  pl.BlockSpec((N_EXPERTS, CAP, d), lambda b, i: (0, b, 0)),
                  pl.BlockSpec((1, TS, d), lambda b, i: (b, i, 0)),
                  pl.BlockSpec((1, 1, d), lambda b, i: (b, 0, 5)),
                  pl.BlockSpec((N_EXPERTS, N_EXPERTS * CAP), lambda b, i: (0, 0))],
        out_specs=pl.BlockSpec((1, TS, d), lambda b, i: (b, i, 0)),
        out_shape=jax.ShapeDtypeStruct((BATCH, SEQ, d), F32),
        compiler_params=_params(("parallel", "parallel")),
        name="scatter_residual",
    )(pos_tok, y, xn, mods3, emat)


def _rope_tables():
    rows = SEQ // GRID_W
    row = jnp.repeat(jnp.arange(rows, dtype=F32), GRID_W)
    col = jnp.tile(jnp.arange(GRID_W, dtype=F32), rows)
    inv = ROPE_BASE ** (-jnp.arange(0, AXIS_ROT, 2, dtype=F32) / AXIS_ROT)
    ang = jnp.concatenate([row[:, None] * inv, col[:, None] * inv], axis=-1)
    cos = jnp.repeat(jnp.cos(ang), 2, axis=-1)
    sin = jnp.stack([-jnp.sin(ang), jnp.sin(ang)], axis=-1).reshape(SEQ, HEAD_DIM)
    reps = LANES // HEAD_DIM
    return jnp.tile(cos, (1, reps)), jnp.tile(sin, (1, reps))


def _cos_sin_outer(f, mults):
    unit = math.pi / N_FFT
    step = 32
    coarse = 2 * step * jnp.arange(mults.shape[0] // step, dtype=I32)
    fine = mults[:step]
    a = ((f[:, None] * coarse[None, :]) % (2 * N_FFT)).astype(F32) * unit
    b = ((f[:, None] * fine[None, :]) % (2 * N_FFT)).astype(F32) * unit
    ca, sa, cb, sb = jnp.cos(a)[:, :, None], jnp.sin(a)[:, :, None], jnp.cos(b)[:, None, :], jnp.sin(b)[:, None, :]
    n = f.shape[0]
    return (ca * cb - sa * sb).reshape(n, -1), (sa * cb + ca * sb).reshape(n, -1)


def _dft_tables():
    idx = jnp.arange(HALF, dtype=I32)
    t2p1 = 2 * idx + 1
    ce, se = _cos_sin_outer(2 * idx, t2p1)
    co, so = _cos_sin_outer(2 * idx + 1, t2p1)
    t1 = jnp.concatenate([ce, so], axis=0).astype(BF16)
    t2 = jnp.concatenate([co, se], axis=0).astype(BF16)
    return t1, t2, t1.T, t2.T


def _phase_tables():
    idx = jnp.arange(HALF, dtype=F32)
    w = jnp.full((HALF,), 2.0 / N_FFT, F32)
    we = w.at[0].set(1.0 / N_FFT)
    pe = (math.pi / N_FFT) * (2.0 * idx)
    po = (math.pi / N_FFT) * (2.0 * idx + 1.0)
    rot = jnp.stack([we * jnp.cos(pe), we * jnp.sin(pe), w * jnp.cos(po), w * jnp.sin(po)])
    return jnp.broadcast_to(rot[:, :, None], (4, HALF, CT))


def _filter_features():
    t = jnp.linspace(0.0, 1.0, SEQ, dtype=F32)[:, None]
    w = 2.0 * math.pi * jnp.arange(SEQ, dtype=F32)[:, None] / SEQ
    fr = jnp.linspace(1e-4, FILTER_BANDS - 1, FILTER_BANDS, dtype=F32)[None, :]
    feat = jnp.concatenate([t, jnp.cos(fr * w), -jnp.sin(fr * w)], axis=-1)
    feat = jnp.pad(feat, ((0, 0), (0, FILTER_HIDDEN - FILTER_EMB)))
    fold = jnp.concatenate([jnp.arange(HALF), jnp.arange(SEQ - 1, HALF - 1, -1)])
    min_decay = math.log(DECAY_TARGET) / SLOW_DECAY_PCT
    max_decay = math.log(DECAY_TARGET) / FAST_DECAY_PCT
    deltas = jnp.linspace(min_decay, max_decay, HYENA_W, dtype=F32)
    return feat[fold], jnp.exp(-t * jnp.abs(deltas))[fold]


def _attn_bias():
    qi = jnp.arange(BLOCK)[:, None]
    kj = jnp.arange(3 * BLOCK)[None, :]
    band = jnp.abs(kj - BLOCK - qi) <= WINDOW
    first = band & (kj >= BLOCK)
    last = band & (kj < 2 * BLOCK)
    return jnp.where(jnp.stack([first, band, last]), 0.0, NEG).astype(F32)


def _head_pair_perm():
    cols = []
    for j in range(GROUP):
        for half in range(N_KV_HEADS):
            h = j + GROUP * half
            cols.extend(range(h * HEAD_DIM, (h + 1) * HEAD_DIM))
    return jnp.asarray(cols, dtype=I32)


def kernel(x, c, ctx, c_ctx, ada_w, ada_b, norm1, norm2, w_in, conv_w, conv_b, filt_w1, filt_b1, filt_w2, filt_b2,
           filt_w3, filt_b3, filt_freq, filt_out, hyena_bias, q_norm, k_norm, attn_sink, w_branch_a, w_branch_b,
           w_out, router, w_gate, w_up, w_down):
    d = D_MODEL
    assert ada_w.shape[0] == 1, "only the single-layer configuration is implemented"
    l = 0
    cos_t, sin_t = _rope_tables()
    t1, t2, t1t, t2t = _dft_tables()
    rot = _phase_tables()
    feat, decay = _filter_features()
    bias = _attn_bias()
    perm = _head_pair_perm()
    gmat = jnp.kron(jnp.eye(LANES // HEAD_DIM, dtype=F32),
                    jnp.full((HEAD_DIM, HEAD_DIM), 1.0 / HEAD_DIM, F32)).astype(BF16)
    tri = (jnp.arange(SEQ)[:, None] < jnp.arange(SEQ)[None, :]).astype(BF16)
    emat = jnp.repeat(jnp.eye(N_EXPERTS, dtype=BF16), CAP, axis=1)
    c16 = jnp.zeros((MOD_ROWS, d), F32).at[:BATCH].set(c).at[BATCH].set(c_ctx)

    mods3 = _ada(c16, ada_w[l], ada_b[l][None, :]).reshape(MOD_ROWS, 1, 6 * d)
    n1 = norm1[l][None, :]
    w = w_in[l]
    wkv = w[:, OFF_K:OFF_G].astype(BF16)
    gk = jnp.tile(k_norm[l], N_KV_HEADS)[None, :]
    kf, kn = _filters(feat, jnp.pad(filt_w1[l], ((0, FILTER_HIDDEN - FILTER_EMB), (0, 0))), filt_b1[l][None, :],
                      filt_w2[l], filt_b2[l][None, :], filt_w3[l], filt_b3[l][None, :], filt_freq[l][None, :],
                      filt_out[l], decay, rot, t1, t2)
    kc, vc = _ctx_proj(ctx, mods3, n1, wkv, gk, gmat)
    zh, q, k, v, gates = _inproj(
        x, mods3, n1, w[:, :OFF_Q].astype(BF16), w[:, OFF_Q:OFF_K][:, perm].astype(BF16), wkv,
        w[:, OFF_G:].astype(BF16), jnp.tile(q_norm[l], N_HEADS)[None, :], gk, gmat, cos_t, sin_t)
    ya = _hyena(zh, conv_w[l], conv_b[l][None, :], hyena_bias[l], kf, kn, t1, t2, t1t, t2t)
    yb = _attention(attn_sink[l], q, k, v, kc, vc, bias)
    router_pad = jnp.pad(router[l], ((0, 0), (0, LANES - N_EXPERTS)))
    router_hi = router_pad.astype(BF16)
    router_lo = (router_pad - router_hi.astype(F32)).astype(BF16)
    xn, h2, aff_t = _merge(ya, yb, gates, x, mods3, norm2[l][None, :], w_branch_a[l].astype(BF16),
                           w_branch_b[l][perm].astype(BF16), w_out[l].astype(BF16), router_hi, router_lo)
    lo, hi = _thresholds(aff_t.reshape(BATCH * N_EXPERTS, SEQ))
    xin, g, pos = _route(aff_t, lo, hi, h2, tri)
    y = _experts(xin, g, w_gate[l], w_up[l], w_down[l])
    return _scatter(jnp.swapaxes(pos, 1, 2), y, xn, mods3, emat)
```

```python
import math

import jax
import jax.numpy as jnp
from jax import lax
from jax.experimental import pallas as pl
from jax.experimental.pallas import tpu as pltpu

F32 = jnp.float32
BF16 = jnp.bfloat16
I32 = jnp.int32
HIGHEST = lax.Precision.HIGHEST

D_MODEL = 1024
BATCH = 8
SEQ = 2048
GRID_W = 64
CTX_LEN = 256
N_HEADS = 8
N_KV_HEADS = 2
HEAD_DIM = 64
GROUP = N_HEADS // N_KV_HEADS
ATTN_W = N_HEADS * HEAD_DIM
KV_W = N_KV_HEADS * HEAD_DIM
WINDOW = 128
BLOCK = 128
HYENA_W = D_MODEL // 2
HYENA_ORDER = 2
FILTER_BANDS = 16
FILTER_EMB = 1 + 2 * FILTER_BANDS
FILTER_HIDDEN = 64
DECAY_TARGET = 1e-2
FAST_DECAY_PCT = 0.3
SLOW_DECAY_PCT = 1.5
ROPE_BASE = 10000.0
AXIS_ROT = HEAD_DIM // 2
N_EXPERTS = 16
EC_CAPACITY = 2
D_FF = 2048
EPS = 1e-6
NEG = -1e30
LOG2E = math.log2(math.e)

OFF_Q = 3 * HYENA_W
OFF_K = OFF_Q + ATTN_W
OFF_V = OFF_K + KV_W
OFF_G = OFF_V + KV_W
IN_W = OFF_G + 2 * D_MODEL

CAP = EC_CAPACITY * SEQ // N_EXPERTS
N_FFT = 2 * SEQ
HALF = SEQ // 2
MOD_ROWS = 16
LANES = 128

TM_IN = 512
TM_MERGE = 1024
SUB_MERGE = 512
CT = 256
FC = 512
RB = 256
TF = 512
MC = 512
TS = 512
EXPERT_GROUP = 4
VMEM_LIMIT = 56 * 1024 * 1024


def _dot(a, b, precision=None):
    return jnp.dot(a, b, preferred_element_type=F32, precision=precision)


def _dot_nt(a, b, precision=None):
    return lax.dot_general(a, b, (((1,), (1,)), ((), ())), preferred_element_type=F32, precision=precision)


def _params(sem, vmem=VMEM_LIMIT):
    return pltpu.CompilerParams(dimension_semantics=sem, vmem_limit_bytes=vmem)


def _rms_mod(x, g, sc, sh):
    ms = jnp.mean(x * x, axis=-1, keepdims=True)
    return (x * lax.rsqrt(ms + EPS) * g) * (1.0 + sc) + sh


def _head_norm_rope(z, g, gmat, cos, sin, scale):
    ms = _dot((z * z).astype(BF16), gmat)
    y = z * lax.rsqrt(ms + EPS) * g
    if cos is not None:
        lane = lax.broadcasted_iota(I32, y.shape, 1)
        nxt = pltpu.roll(y, LANES - 1, axis=1)
        prv = pltpu.roll(y, 1, axis=1)
        y = y * cos + jnp.where((lane & 1) == 0, nxt, prv) * sin
    return y * scale


def _ada_body(c_ref, w_ref, b_ref, o_ref):
    c = c_ref[...]
    s = c * jax.nn.sigmoid(c)
    o_ref[...] = _dot(s, w_ref[...], HIGHEST) + b_ref[...]


def _ada(c16, w, b):
    d = D_MODEL
    return pl.pallas_call(
        _ada_body,
        grid=(6,),
        in_specs=[pl.BlockSpec((MOD_ROWS, d), lambda j: (0, 0)),
                  pl.BlockSpec((d, d), lambda j: (0, j)),
                  pl.BlockSpec((1, d), lambda j: (0, j))],
        out_specs=pl.BlockSpec((MOD_ROWS, d), lambda j: (0, j)),
        out_shape=jax.ShapeDtypeStruct((MOD_ROWS, 6 * d), F32),
        compiler_params=_params(("parallel",)),
        name="ada_mod",
    )(c16, w, b)


def _sign_rows(n):
    lane = lax.broadcasted_iota(I32, (8, n), 1)
    sub = lax.broadcasted_iota(I32, (8, n), 0)
    sg = jnp.where((lane & 1) == 0, 1.0, -1.0)
    return jnp.where(sub == 0, sg, 0.0).astype(BF16)


def _filt_body(feat_ref, w1_ref, b1_ref, w2_ref, b2_ref, w3_ref, b3_ref, fq_ref, fof_ref, fob_ref, dec_ref,
               rot_ref, t1_ref, t2_ref, kf_ref, kn_ref, h_ref):
    @pl.when((pl.program_id(0) == 0) & (pl.program_id(1) == 0))
    def _():
        fq = fq_ref[...]
        h = jnp.sin(fq * (_dot(feat_ref[...], w1_ref[...], HIGHEST) + b1_ref[...]))
        h = jnp.sin(fq * (_dot(h, w2_ref[...], HIGHEST) + b2_ref[...]))
        h_ref[...] = jnp.sin(fq * (_dot(h, w3_ref[...], HIGHEST) + b3_ref[...]))

    h = h_ref[...]
    dec = dec_ref[...]
    hf = _dot(h, fof_ref[...], HIGHEST) * dec
    hb = _dot(h, fob_ref[...], HIGHEST) * dec
    row = lax.broadcasted_iota(I32, hf.shape, 0)
    hb = jnp.where(row == 0, 0.0, hb)
    a = hf + hb
    b = hf - hb
    pa = (a[:HALF] + a[HALF:]).astype(BF16)
    ma = (a[:HALF] - a[HALF:]).astype(BF16)
    pb = (b[:HALF] + b[HALF:]).astype(BF16)
    mb = (b[:HALF] - b[HALF:]).astype(BF16)
    t1 = t1_ref[...]
    t2 = t2_ref[...]
    a1 = _dot(t1, pa)
    a2 = _dot(t2, ma)
    b1 = _dot(t1, pb)
    b2 = _dot(t2, mb)
    ce, se, co, so = rot_ref[0], rot_ref[1], rot_ref[2], rot_ref[3]
    kf_ref[0, 0] = a1[:HALF] * ce + a2[HALF:] * se
    kf_ref[0, 1] = b2[HALF:] * ce - b1[:HALF] * se
    kf_ref[0, 2] = a2[:HALF] * co + a1[HALF:] * so
    kf_ref[0, 3] = b1[HALF:] * co - b2[:HALF] * so
    kn_ref[0] = _dot(_sign_rows(HALF), ma)[0:1] * (1.0 / N_FFT)


def _filters(feat, w1, b1, w2, b2, w3, b3, fq, fout, decay, rot, t1, t2):
    nct = HYENA_W // CT
    full = lambda shape: pl.BlockSpec(shape, lambda o, c: (0,) * len(shape))
    return pl.pallas_call(
        _filt_body,
        grid=(HYENA_ORDER, nct),
        in_specs=[full((SEQ, FILTER_HIDDEN)), full((FILTER_HIDDEN, FILTER_HIDDEN)), full((1, FILTER_HIDDEN)),
                  full((FILTER_HIDDEN, FILTER_HIDDEN)), full((1, FILTER_HIDDEN)),
                  full((FILTER_HIDDEN, FILTER_HIDDEN)), full((1, FILTER_HIDDEN)), full((1, FILTER_HIDDEN)),
                  pl.BlockSpec((FILTER_HIDDEN, CT), lambda o, c: (0, (o * 2 + 0) * nct + c)),
                  pl.BlockSpec((FILTER_HIDDEN, CT), lambda o, c: (0, (o * 2 + 1) * nct + c)),
                  pl.BlockSpec((SEQ, CT), lambda o, c: (0, c)),
                  full((4, HALF, CT)),
                  pl.BlockSpec((SEQ, HALF), lambda o, c: (0, 0), pipeline_mode=pl.Buffered(1)),
                  pl.BlockSpec((SEQ, HALF), lambda o, c: (0, 0), pipeline_mode=pl.Buffered(1))],
        out_specs=[pl.BlockSpec((1, 4, HALF, CT), lambda o, c: (o, 0, 0, c)),
                   pl.BlockSpec((1, 1, CT), lambda o, c: (o, 0, c))],
        out_shape=[jax.ShapeDtypeStruct((HYENA_ORDER, 4, HALF, HYENA_W), F32),
                   jax.ShapeDtypeStruct((HYENA_ORDER, 1, HYENA_W), F32)],
        scratch_shapes=[pltpu.VMEM((SEQ, FILTER_HIDDEN), F32)],
        compiler_params=_params(("arbitrary", "arbitrary")),
        name="hyena_filters",
    )(feat, w1, b1, w2, b2, w3, b3, fq, fout, fout, decay, rot, t1, t2)


def _inproj_body(x_ref, sc_ref, sh_ref, n1_ref, wh_ref, wq_ref, wkv_ref, wg_ref, gq_ref, gk_ref, gmat_ref,
                 cos_ref, sin_ref, zh_ref, q_ref, k_ref, v_ref, gate_ref):
    hx = _rms_mod(x_ref[0], n1_ref[...], sc_ref[0], sh_ref[0]).astype(BF16)
    zh_ref[0] = _dot(hx, wh_ref[...]).astype(BF16)
    gmat = gmat_ref[...]
    cos = cos_ref[...]
    sin = sin_ref[...]
    zq = _dot(hx, wq_ref[...])
    for s in range(ATTN_W // LANES):
        sl = slice(s * LANES, (s + 1) * LANES)
        q_ref[0, :, sl] = _head_norm_rope(zq[:, sl], gq_ref[:, sl], gmat, cos, sin, LOG2E * HEAD_DIM ** -0.5).astype(BF16)
    zkv = _dot(hx, wkv_ref[...])
    k_ref[0] = _head_norm_rope(zkv[:, :KV_W], gk_ref[...], gmat, cos, sin, 1.0).astype(BF16)
    v_ref[0] = zkv[:, KV_W:].astype(BF16)
    gate_ref[0] = jax.nn.sigmoid(_dot(hx, wg_ref[...])).astype(BF16)


def _inproj(x, mods3, norm1, wh, wq, wkv, wg, gq, gk, gmat, cos_t, sin_t):
    d = D_MODEL
    nt = SEQ // TM_IN
    const = lambda shape: pl.BlockSpec(shape, lambda b, i: (0,) * len(shape))
    tok = lambda w: pl.BlockSpec((1, TM_IN, w), lambda b, i: (b, i, 0))
    return pl.pallas_call(
        _inproj_body,
        grid=(BATCH, nt),
        in_specs=[tok(d),
                  pl.BlockSpec((1, 1, d), lambda b, i: (b, 0, 1)),
                  pl.BlockSpec((1, 1, d), lambda b, i: (b, 0, 0)),
                  const((1, d)), const((d, OFF_Q)), const((d, ATTN_W)), const((d, 2 * KV_W)), const((d, 2 * d)),
                  const((1, ATTN_W)), const((1, KV_W)), const((LANES, LANES)),
                  pl.BlockSpec((TM_IN, LANES), lambda b, i: (i, 0)),
                  pl.BlockSpec((TM_IN, LANES), lambda b, i: (i, 0))],
        out_specs=[tok(OFF_Q), tok(ATTN_W), tok(KV_W), tok(KV_W), tok(2 * d)],
        out_shape=[jax.ShapeDtypeStruct((BATCH, SEQ, OFF_Q), BF16),
                   jax.ShapeDtypeStruct((BATCH, SEQ, ATTN_W), BF16),
                   jax.ShapeDtypeStruct((BATCH, SEQ, KV_W), BF16),
                   jax.ShapeDtypeStruct((BATCH, SEQ, KV_W), BF16),
                   jax.ShapeDtypeStruct((BATCH, SEQ, 2 * d), BF16)],
        compiler_params=_params(("parallel", "parallel")),
        name="in_proj",
    )(x, mods3, mods3, norm1, wh, wq, wkv, wg, gq, gk, gmat, cos_t, sin_t)


def _ctx_body(c_ref, sc_ref, sh_ref, n1_ref, wkv_ref, gk_ref, gmat_ref, kc_ref, vc_ref):
    hc = _rms_mod(c_ref[0], n1_ref[...], sc_ref[0], sh_ref[0]).astype(BF16)
    z = _dot(hc, wkv_ref[...])
    kc_ref[0] = _head_norm_rope(z[:, :KV_W], gk_ref[...], gmat_ref[...], None, None, 1.0).astype(BF16)
    vc_ref[0] = z[:, KV_W:].astype(BF16)


def _ctx_proj(ctx, mods3, norm1, wkv, gk, gmat):
    d = D_MODEL
    const = lambda shape: pl.BlockSpec(shape, lambda b: (0,) * len(shape))
    return pl.pallas_call(
        _ctx_body,
        grid=(BATCH,),
        in_specs=[pl.BlockSpec((1, CTX_LEN, d), lambda b: (b, 0, 0)),
                  pl.BlockSpec((1, 1, d), lambda b: (BATCH, 0, 1)),
                  pl.BlockSpec((1, 1, d), lambda b: (BATCH, 0, 0)),
                  const((1, d)), const((d, 2 * KV_W)), const((1, KV_W)), const((LANES, LANES))],
        out_specs=[pl.BlockSpec((1, CTX_LEN, KV_W), lambda b: (b, 0, 0)),
                   pl.BlockSpec((1, CTX_LEN, KV_W), lambda b: (b, 0, 0))],
        out_shape=[jax.ShapeDtypeStruct((BATCH, CTX_LEN, KV_W), BF16),
                   jax.ShapeDtypeStruct((BATCH, CTX_LEN, KV_W), BF16)],
        compiler_params=_params(("parallel",)),
        name="ctx_proj",
    )(ctx, mods3, mods3, norm1, wkv, gk, gmat)


def _hyena_body(zv_ref, z1_ref, z2_ref, cwv_ref, cw1_ref, cw2_ref, cbv_ref, cb1_ref, cb2_ref, hb_ref, kf_ref,
                kn_ref, t1_ref, t2_ref, t1t_ref, t2t_ref, o_ref, lo_ref, hi_ref, glo_ref, ghi_ref, p_ref, m_ref,
                za_ref, zb_ref):
    row = lax.broadcasted_iota(I32, (HALF, CT), 0)
    rr = lax.broadcasted_iota(I32, (RB, RB), 0)
    cc = lax.broadcasted_iota(I32, (RB, RB), 1)
    flip = jnp.where(rr + cc == RB - 1, 1.0, 0.0).astype(BF16)
    nrb = HALF // RB

    def folded_short_conv(z_ref, w_ref, b_ref, lo_out, hi_out):
        zlo = z_ref[0, 0:HALF, :].astype(F32)
        for j in range(nrb):
            hi_out[j * RB:(j + 1) * RB, :] = _dot(flip, z_ref[0, SEQ - RB * (j + 1):SEQ - RB * j, :])
        zhi = hi_out[...]
        w0, w1, w2 = w_ref[0:1, :], w_ref[1:2, :], w_ref[2:3, :]
        first, last = row == 0, row == HALF - 1
        lo_prev = jnp.where(first, 0.0, pltpu.roll(zlo, 1, axis=0))
        lo_next = jnp.where(last, zhi[HALF - 1:HALF, :], pltpu.roll(zlo, HALF - 1, axis=0))
        hi_prev = jnp.where(first, 0.0, pltpu.roll(zhi, 1, axis=0))
        hi_next = jnp.where(last, zlo[HALF - 1:HALF, :], pltpu.roll(zhi, HALF - 1, axis=0))
        lo_out[...] = lo_prev * w0 + zlo * w1 + lo_next * w2 + b_ref[...]
        hi_out[...] = hi_next * w0 + zhi * w1 + hi_prev * w2 + b_ref[...]

    folded_short_conv(zv_ref, cwv_ref, cbv_ref, lo_ref, hi_ref)
    sign8 = _sign_rows(HALF)
    odd = (lax.broadcasted_iota(I32, (FC, CT), 0) & 1) == 1
    for o, (zr, cw, cb) in enumerate(((z1_ref, cw1_ref, cb1_ref), (z2_ref, cw2_ref, cb2_ref))):
        folded_short_conv(zr, cw, cb, glo_ref, ghi_ref)
        p_ref[...] = (lo_ref[...] + hi_ref[...]).astype(BF16)
        m_ref[...] = (lo_ref[...] - hi_ref[...]).astype(BF16)
        pv = p_ref[...]
        mv = m_ref[...]
        for c in range(HALF // FC):
            ev = slice(c * FC, (c + 1) * FC)
            od = slice(HALF + c * FC, HALF + (c + 1) * FC)
            xce = _dot(t1_ref[ev, :], pv)
            xso = _dot(t1_ref[od, :], pv)
            xco = _dot(t2_ref[ev, :], mv)
            xse = _dot(t2_ref[od, :], mv)
            kce, kse, kco, kso = kf_ref[o, 0, ev, :], kf_ref[o, 1, ev, :], kf_ref[o, 2, ev, :], kf_ref[o, 3, ev, :]
            za_ref[ev, :] = (xce * kce - xse * kse).astype(BF16)
            za_ref[od, :] = (xco * kso + xso * kco).astype(BF16)
            zb_ref[ev, :] = (xco * kco - xso * kso).astype(BF16)
            zb_ref[od, :] = (xce * kse + xse * kce).astype(BF16)
        zn = _dot(sign8, mv)[0:1] * kn_ref[o]
        bias = hb_ref[o:o + 1, :]
        za = za_ref[...]
        zb = zb_ref[...]
        for c in range(HALF // FC):
            rs = slice(c * FC, (c + 1) * FC)
            half_p = _dot(t1t_ref[rs, :], za)
            half_m = _dot(t2t_ref[rs, :], zb) + jnp.where(odd, -zn, zn)
            lo_ref[rs, :] = glo_ref[rs, :] * (half_p + half_m + bias * lo_ref[rs, :])
            hi_ref[rs, :] = ghi_ref[rs, :] * (half_p - half_m + bias * hi_ref[rs, :])
    o_ref[0, 0:HALF, :] = lo_ref[...].astype(BF16)
    for j in range(nrb):
        o_ref[0, SEQ - RB * (j + 1):SEQ - RB * j, :] = _dot(
            flip, hi_ref[j * RB:(j + 1) * RB, :].astype(BF16)).astype(BF16)


def _hyena(zh, conv_w, conv_b, hbias, kf, kn, t1, t2, t1t, t2t):
    nct = HYENA_W // CT
    zspec = lambda k: pl.BlockSpec((1, SEQ, CT), lambda c, b: (b, 0, k * nct + c))
    wspec = lambda k: pl.BlockSpec((3, CT), lambda c, b: (0, k * nct + c))
    bspec = lambda k: pl.BlockSpec((1, CT), lambda c, b: (0, k * nct + c))
    table = lambda shape: pl.BlockSpec(shape, lambda c, b: (0, 0), pipeline_mode=pl.Buffered(1))
    half_f32 = pltpu.VMEM((HALF, CT), F32)
    return pl.pallas_call(
        _hyena_body,
        grid=(nct, BATCH),
        in_specs=[zspec(0), zspec(1), zspec(2), wspec(0), wspec(1), wspec(2), bspec(0), bspec(1), bspec(2),
                  pl.BlockSpec((HYENA_ORDER, CT), lambda c, b: (0, c)),
                  pl.BlockSpec((HYENA_ORDER, 4, HALF, CT), lambda c, b: (0, 0, 0, c)),
                  pl.BlockSpec((HYENA_ORDER, 1, CT), lambda c, b: (0, 0, c)),
                  table((SEQ, HALF)), table((SEQ, HALF)), table((HALF, SEQ)), table((HALF, SEQ))],
        out_specs=pl.BlockSpec((1, SEQ, CT), lambda c, b: (b, 0, c)),
        out_shape=jax.ShapeDtypeStruct((BATCH, SEQ, HYENA_W), BF16),
        scratch_shapes=[half_f32, half_f32, half_f32, half_f32,
                        pltpu.VMEM((HALF, CT), BF16), pltpu.VMEM((HALF, CT), BF16),
                        pltpu.VMEM((SEQ, CT), BF16), pltpu.VMEM((SEQ, CT), BF16)],
        compiler_params=_params(("parallel", "parallel")),
        name="hyena_conv",
    )(zh, zh, zh, conv_w, conv_w, conv_w, conv_b, conv_b, conv_b, hbias, kf, kn, t1, t2, t1t, t2t)


def _attn_body(sink_ref, q_ref, k_ref, v_ref, kc_ref, vc_ref, bias_ref, o_ref, kp_ref, vlo_ref, vhi_ref):
    nb = SEQ // BLOCK
    lane = lax.broadcasted_iota(I32, (BLOCK, LANES), 1)
    low = lane < HEAD_DIM
    mask_lo = jnp.where(low, 1.0, 0.0).astype(BF16)
    mask_hi = jnp.where(low, 0.0, 1.0).astype(BF16)

    def with_ones(v):
        lane_v = lax.broadcasted_iota(I32, v.shape, 1) < HEAD_DIM
        one = jnp.ones_like(v)
        return jnp.where(lane_v, v, one), jnp.where(lane_v, one, v)

    zpad = jnp.zeros((BLOCK, KV_W), BF16)
    kp_ref[0:BLOCK] = zpad
    kp_ref[BLOCK:BLOCK + SEQ] = k_ref[0]
    kp_ref[BLOCK + SEQ:] = zpad
    v_lo, v_hi = with_ones(v_ref[0])
    for ref, val in ((vlo_ref, v_lo), (vhi_ref, v_hi)):
        ref[0:BLOCK] = zpad
        ref[BLOCK:BLOCK + SEQ] = val
        ref[BLOCK + SEQ:] = zpad
    kc = kc_ref[0]
    vc_pair = with_ones(vc_ref[0])

    def block(n, carry):
        r = pl.multiple_of(n * BLOCK, BLOCK)
        kw = kp_ref[pl.ds(r, 3 * BLOCK), :]
        vw_pair = (vlo_ref[pl.ds(r, 3 * BLOCK), :], vhi_ref[pl.ds(r, 3 * BLOCK), :])
        bias = bias_ref[jnp.where(n == 0, 0, jnp.where(n == nb - 1, 2, 1))]
        scores = []
        for j in range(GROUP):
            qs = q_ref[0, pl.ds(r, BLOCK), j * LANES:(j + 1) * LANES]
            for msk in (mask_lo, mask_hi):
                qm = qs * msk
                scores.append((_dot_nt(qm, kw) + bias, _dot_nt(qm, kc)))
        probs = []
        for idx, (sw, sc) in enumerate(scores):
            snk = sink_ref[idx // 2 + GROUP * (idx % 2)]
            m = jnp.maximum(jnp.maximum(jnp.max(sw, axis=-1, keepdims=True),
                                        jnp.max(sc, axis=-1, keepdims=True)), snk)
            probs.append((jnp.exp2(sw - m).astype(BF16), jnp.exp2(sc - m).astype(BF16), jnp.exp2(snk - m)))
        outs = []
        for idx, (pw, pc, psink) in enumerate(probs):
            acc = _dot(pw, vw_pair[idx % 2]) + _dot(pc, vc_pair[idx % 2])
            den = pltpu.roll(acc, HEAD_DIM, axis=1) + psink
            outs.append(acc / den)
        for j in range(GROUP):
            o_ref[0, pl.ds(r, BLOCK), j * LANES:(j + 1) * LANES] = jnp.where(
                low, outs[2 * j], outs[2 * j + 1]).astype(BF16)
        return carry

    lax.fori_loop(0, nb, block, 0)


def _attention(sink, q, k, v, kc, vc, bias):
    per_b = lambda n, w: pl.BlockSpec((1, n, w), lambda b: (b, 0, 0))
    return pl.pallas_call(
        _attn_body,
        grid=(BATCH,),
        in_specs=[pl.BlockSpec(memory_space=pltpu.SMEM),
                  per_b(SEQ, ATTN_W), per_b(SEQ, KV_W), per_b(SEQ, KV_W), per_b(CTX_LEN, KV_W), per_b(CTX_LEN, KV_W),
                  pl.BlockSpec((3, BLOCK, 3 * BLOCK), lambda b: (0, 0, 0))],
        out_specs=per_b(SEQ, ATTN_W),
        out_shape=jax.ShapeDtypeStruct((BATCH, SEQ, ATTN_W), BF16),
        scratch_shapes=[pltpu.VMEM((SEQ + 2 * BLOCK, KV_W), BF16)] * 3,
        compiler_params=_params(("parallel",)),
        name="window_attn",
    )(sink, q, k, v, kc, vc, bias)


def _merge_body(ya_ref, yb_ref, gt_ref, x_ref, g1_ref, sc2_ref, sh2_ref, n2_ref, wa_ref, wb_ref, wo_ref, rh_ref,
                rl_ref, xn_ref, h2_ref, aff_ref):
    d = D_MODEL
    groups = [slice(i * SUB_MERGE, (i + 1) * SUB_MERGE) for i in range(TM_MERGE // SUB_MERGE)]
    branch = [(_dot(ya_ref[0, s, :], wa_ref[...]), _dot(yb_ref[0, s, :], wb_ref[...])) for s in groups]
    mixed = [(gt_ref[0, s, :d].astype(F32) * ua + gt_ref[0, s, d:].astype(F32) * ub).astype(BF16)
             for s, (ua, ub) in zip(groups, branch)]
    proj = [_dot(u, wo_ref[...]) for u in mixed]
    h2_parts = []
    for s, m in zip(groups, proj):
        xn = x_ref[0, s, :] + g1_ref[0] * m
        xn_ref[0, s, :] = xn
        h2 = _rms_mod(xn, n2_ref[...], sc2_ref[0], sh2_ref[0])
        h2_hi = h2.astype(BF16)
        h2_ref[0, s, :] = h2_hi
        h2_parts.append((h2_hi, (h2 - h2_hi.astype(F32)).astype(BF16)))
    rh = rh_ref[...]
    logits = [_dot(hi, rh) + _dot(lo, rh) + _dot(hi, rl_ref[...]) for hi, lo in h2_parts]
    for s, lg in zip(groups, logits):
        lt = lg.T[:N_EXPERTS, :]
        e = jnp.exp(lt - jnp.max(lt, axis=0, keepdims=True))
        aff_ref[0, :, s] = e / jnp.sum(e, axis=0, keepdims=True)


def _merge(ya, yb, gates, x, mods3, norm2, wa, wb, wo, router_hi, router_lo):
    d = D_MODEL
    nt = SEQ // TM_MERGE
    const = lambda shape: pl.BlockSpec(shape, lambda b, i: (0,) * len(shape))
    tok = lambda w: pl.BlockSpec((1, TM_MERGE, w), lambda b, i: (b, i, 0))
    mod = lambda k: pl.BlockSpec((1, 1, d), lambda b, i: (b, 0, k))
    return pl.pallas_call(
        _merge_body,
        grid=(BATCH, nt),
        in_specs=[tok(HYENA_W), tok(ATTN_W), tok(2 * d), tok(d), mod(2), mod(4), mod(3), const((1, d)),
                  const((HYENA_W, d)), const((ATTN_W, d)), const((d, d)), const((d, LANES)), const((d, LANES))],
        out_specs=[tok(d), tok(d), pl.BlockSpec((1, N_EXPERTS, TM_MERGE), lambda b, i: (b, 0, i))],
        out_shape=[jax.ShapeDtypeStruct((BATCH, SEQ, d), F32),
                   jax.ShapeDtypeStruct((BATCH, SEQ, d), BF16),
                   jax.ShapeDtypeStruct((BATCH, N_EXPERTS, SEQ), F32)],
        compiler_params=_params(("parallel", "parallel")),
        name="merge_outproj",
    )(ya, yb, gates, x, mods3, mods3, mods3, norm2, wa, wb, wo, router_hi, router_lo)


def _thresh_body(aff_ref, lo_ref, hi_ref):
    aff = aff_ref[...]
    rows = aff.shape[0]
    bits = pltpu.bitcast(aff, I32)

    def bisect_bits(_, carry):
        lo, hi = carry
        mid = lo + ((hi - lo) >> 1)
        ge = jnp.sum((bits >= mid).astype(I32), axis=1, keepdims=True) >= CAP
        return jnp.where(ge, mid, lo), jnp.where(ge, hi, mid)

    lo0 = jnp.zeros((rows, 1), I32)
    hi0 = jnp.full((rows, 1), 0x3F800001, I32)
    thr_bits, _ = lax.fori_loop(0, 31, bisect_bits, (lo0, hi0))
    thr = pltpu.bitcast(thr_bits, F32)

    def bisect_val(_, carry):
        lo, hi = carry
        mid = 0.5 * (lo + hi)
        ge = jnp.sum(jnp.where(aff >= mid, 1.0, 0.0), axis=1, keepdims=True) >= CAP
        return jnp.where(ge, mid, lo), jnp.where(ge, hi, mid)

    lo, hi = lax.fori_loop(0, 30, bisect_val, (0.5 * thr, jnp.maximum(2.0 * thr, 1e-30)))
    lo_ref[...] = jnp.broadcast_to(lo, lo_ref.shape)
    hi_ref[...] = jnp.broadcast_to(hi, hi_ref.shape)


def _thresholds(aff_rows):
    rows = BATCH * N_EXPERTS
    return pl.pallas_call(
        _thresh_body,
        grid=(1,),
        in_specs=[pl.BlockSpec((rows, SEQ), lambda i: (0, 0))],
        out_specs=[pl.BlockSpec((rows, LANES), lambda i: (0, 0)), pl.BlockSpec((rows, LANES), lambda i: (0, 0))],
        out_shape=[jax.ShapeDtypeStruct((rows, LANES), F32), jax.ShapeDtypeStruct((rows, LANES), F32)],
        compiler_params=_params(("arbitrary",)),
        name="route_threshold",
    )(aff_rows)


def _route_body(aff_ref, lo_ref, hi_ref, h_ref, tri_ref, xin_ref, g_ref, pos_ref, pbuf_ref):
    aff = aff_ref[0]
    above = jnp.where(aff >= hi_ref[:, 0:1], 1.0, 0.0)
    band = jnp.where(aff >= lo_ref[:, 0:1], 1.0, 0.0) - above
    need = CAP - jnp.sum(above, axis=1, keepdims=True)
    tri = tri_ref[...]
    tie_rank = _dot(band.astype(BF16), tri)
    self = above + band * jnp.where(tie_rank < need, 1.0, 0.0)
    pos = _dot(self.astype(BF16), tri)
    posi = jnp.where(self > 0.5, pos.astype(I32), -1)
    pos_ref[0] = posi

    h = h_ref[0]
    slot = lax.broadcasted_iota(I32, (CAP, SEQ), 0)
    for grp in range(N_EXPERTS // EXPERT_GROUP):
        for i in range(EXPERT_GROUP):
            e = grp * EXPERT_GROUP + i
            hit = posi[e:e + 1, :] == slot
            pbuf_ref[i * CAP:(i + 1) * CAP, :] = jnp.where(hit, 1.0, 0.0).astype(BF16)
            g_ref[e] = jnp.sum(jnp.where(hit, aff[e:e + 1, :], 0.0), axis=1, keepdims=True)
        rows = _dot(pbuf_ref[...], h)
        for i in range(EXPERT_GROUP):
            xin_ref[grp * EXPERT_GROUP + i] = rows[i * CAP:(i + 1) * CAP].astype(BF16)


def _route(aff_t, lo, hi, h2, tri):
    d = D_MODEL
    return pl.pallas_call(
        _route_body,
        grid=(BATCH,),
        in_specs=[pl.BlockSpec((1, N_EXPERTS, SEQ), lambda b: (b, 0, 0)),
                  pl.BlockSpec((N_EXPERTS, LANES), lambda b: (b, 0)),
                  pl.BlockSpec((N_EXPERTS, LANES), lambda b: (b, 0)),
                  pl.BlockSpec((1, SEQ, d), lambda b: (b, 0, 0)),
                  pl.BlockSpec((SEQ, SEQ), lambda b: (0, 0), pipeline_mode=pl.Buffered(1))],
        out_specs=[pl.BlockSpec((N_EXPERTS, CAP, d), lambda b: (0, b, 0)),
                   pl.BlockSpec((N_EXPERTS, CAP, 1), lambda b: (0, b, 0)),
                   pl.BlockSpec((1, N_EXPERTS, SEQ), lambda b: (b, 0, 0))],
        out_shape=[jax.ShapeDtypeStruct((N_EXPERTS, BATCH * CAP, d), BF16),
                   jax.ShapeDtypeStruct((N_EXPERTS, BATCH * CAP, 1), F32),
                   jax.ShapeDtypeStruct((BATCH, N_EXPERTS, SEQ), I32)],
        scratch_shapes=[pltpu.VMEM((EXPERT_GROUP * CAP, SEQ), BF16)],
        compiler_params=_params(("parallel",)),
        name="route_gather",
    )(aff_t, lo, hi, h2, tri)


def _expert_body(x_ref, g_ref, wg_ref, wu_ref, wd_ref, o_ref, acc_ref, wgb_ref, wub_ref, wdb_ref):
    f = pl.program_id(1)
    wgb_ref[...] = wg_ref[0].astype(BF16)
    wub_ref[...] = wu_ref[0].astype(BF16)
    wdb_ref[...] = wd_ref[0].astype(BF16)

    @pl.when(f == 0)
    def _():
        acc_ref[...] = jnp.zeros_like(acc_ref)

    for m in range(BATCH * CAP // MC):
        rows = slice(m * MC, (m + 1) * MC)
        xm = x_ref[0, rows, :]
        a = _dot(xm, wgb_ref[...])
        b = _dot(xm, wub_ref[...])
        hh = (a * jax.nn.sigmoid(a) * b).astype(BF16)
        acc_ref[rows, :] += _dot(hh, wdb_ref[...])

    @pl.when(f == D_FF // TF - 1)
    def _():
        o_ref[0] = (acc_ref[...] * g_ref[0]).astype(BF16)


def _experts(xin, g, w_gate, w_up, w_down):
    d = D_MODEL
    rows = BATCH * CAP
    return pl.pallas_call(
        _expert_body,
        grid=(N_EXPERTS, D_FF // TF),
        in_specs=[pl.BlockSpec((1, rows, d), lambda e, f: (e, 0, 0)),
                  pl.BlockSpec((1, rows, 1), lambda e, f: (e, 0, 0)),
                  pl.BlockSpec((1, d, TF), lambda e, f: (e, 0, f)),
                  pl.BlockSpec((1, d, TF), lambda e, f: (e, 0, f)),
                  pl.BlockSpec((1, TF, d), lambda e, f: (e, f, 0))],
        out_specs=pl.BlockSpec((1, rows, d), lambda e, f: (e, 0, 0)),
        out_shape=jax.ShapeDtypeStruct((N_EXPERTS, rows, d), BF16),
        scratch_shapes=[pltpu.VMEM((rows, d), F32), pltpu.VMEM((d, TF), BF16), pltpu.VMEM((d, TF), BF16),
                        pltpu.VMEM((TF, d), BF16)],
        compiler_params=_params(("parallel", "arbitrary")),
        name="swiglu_experts",
    )(xin, g, w_gate, w_up, w_down)


def _scatter_body(pos_ref, y_ref, x_ref, g2_ref, emat_ref, o_ref):
    width = N_EXPERTS * CAP
    pc = pos_ref[0].astype(F32).astype(BF16)
    pb = _dot(pc, emat_ref[...])
    slot = (lax.broadcasted_iota(I32, (TS, width), 1) & (CAP - 1)).astype(F32)
    onehot = jnp.where(pb == slot, 1.0, 0.0).astype(BF16)
    y = y_ref[...].reshape(width, D_MODEL)
    o_ref[0] = x_ref[0] + g2_ref[0] * _dot(onehot, y)


def _scatter(pos_tok, y, xn, mods3, emat):
    d = D_MODEL
    return pl.pallas_call(
        _scatter_body,
        grid=(BATCH, SEQ // TS),
        in_specs=[pl.BlockSpec((1, TS, N_EXPERTS), lambda b, i: (b, i, 0)),
                  pl.BlockSpec((N_EXPERTS, CAP, d), lambda b, i: (0, b, 0)),
                  pl.BlockSpec((1, TS, d), lambda b, i: (b, i, 0)),
                  pl.BlockSpec((1, 1, d), lambda b, i: (b, 0, 5)),
                  pl.BlockSpec((N_EXPERTS, N_EXPERTS * CAP), lambda b, i: (0, 0))],
        out_specs=pl.BlockSpec((1, TS, d), lambda b, i: (b, i, 0)),
        out_shape=jax.ShapeDtypeStruct((BATCH, SEQ, d), F32),
        compiler_params=_params(("parallel", "parallel")),
        name="scatter_residual",
    )(pos_tok, y, xn, mods3, emat)


def _rope_tables():
    rows = SEQ // GRID_W
    row = jnp.repeat(jnp.arange(rows, dtype=F32), GRID_W)
    col = jnp.tile(jnp.arange(GRID_W, dtype=F32), rows)
    inv = ROPE_BASE ** (-jnp.arange(0, AXIS_ROT, 2, dtype=F32) / AXIS_ROT)
    ang = jnp.concatenate([row[:, None] * inv, col[:, None] * inv], axis=-1)
    cos = jnp.repeat(jnp.cos(ang), 2, axis=-1)
    sin = jnp.stack([-jnp.sin(ang), jnp.sin(ang)], axis=-1).reshape(SEQ, HEAD_DIM)
    reps = LANES // HEAD_DIM
    return jnp.tile(cos, (1, reps)), jnp.tile(sin, (1, reps))


def _cos_sin_outer(f, mults):
    unit = math.pi / N_FFT
    step = 32
    coarse = 2 * step * jnp.arange(mults.shape[0] // step, dtype=I32)
    fine = mults[:step]
    a = ((f[:, None] * coarse[None, :]) % (2 * N_FFT)).astype(F32) * unit
    b = ((f[:, None] * fine[None, :]) % (2 * N_FFT)).astype(F32) * unit
    ca, sa, cb, sb = jnp.cos(a)[:, :, None], jnp.sin(a)[:, :, None], jnp.cos(b)[:, None, :], jnp.sin(b)[:, None, :]
    n = f.shape[0]
    return (ca * cb - sa * sb).reshape(n, -1), (sa * cb + ca * sb).reshape(n, -1)


def _dft_tables():
    idx = jnp.arange(HALF, dtype=I32)
    t2p1 = 2 * idx + 1
    ce, se = _cos_sin_outer(2 * idx, t2p1)
    co, so = _cos_sin_outer(2 * idx + 1, t2p1)
    t1 = jnp.concatenate([ce, so], axis=0).astype(BF16)
    t2 = jnp.concatenate([co, se], axis=0).astype(BF16)
    return t1, t2, t1.T, t2.T


def _phase_tables():
    idx = jnp.arange(HALF, dtype=F32)
    w = jnp.full((HALF,), 2.0 / N_FFT, F32)
    we = w.at[0].set(1.0 / N_FFT)
    pe = (math.pi / N_FFT) * (2.0 * idx)
    po = (math.pi / N_FFT) * (2.0 * idx + 1.0)
    rot = jnp.stack([we * jnp.cos(pe), we * jnp.sin(pe), w * jnp.cos(po), w * jnp.sin(po)])
    return jnp.broadcast_to(rot[:, :, None], (4, HALF, CT))


def _filter_features():
    t = jnp.linspace(0.0, 1.0, SEQ, dtype=F32)[:, None]
    w = 2.0 * math.pi * jnp.arange(SEQ, dtype=F32)[:, None] / SEQ
    fr = jnp.linspace(1e-4, FILTER_BANDS - 1, FILTER_BANDS, dtype=F32)[None, :]
    feat = jnp.concatenate([t, jnp.cos(fr * w), -jnp.sin(fr * w)], axis=-1)
    feat = jnp.pad(feat, ((0, 0), (0, FILTER_HIDDEN - FILTER_EMB)))
    min_decay = math.log(DECAY_TARGET) / SLOW_DECAY_PCT
    max_decay = math.log(DECAY_TARGET) / FAST_DECAY_PCT
    deltas = jnp.linspace(min_decay, max_decay, HYENA_W, dtype=F32)
    return _fold_rows(feat), _fold_rows(jnp.exp(-t * jnp.abs(deltas)))


def _attn_bias():
    qi = jnp.arange(BLOCK)[:, None]
    kj = jnp.arange(3 * BLOCK)[None, :]
    band = jnp.abs(kj - BLOCK - qi) <= WINDOW
    first = band & (kj >= BLOCK)
    last = band & (kj < 2 * BLOCK)
    return jnp.where(jnp.stack([first, band, last]), 0.0, NEG).astype(F32)


def _pair_heads(w, axis):
    heads = [lax.slice_in_dim(w, h * HEAD_DIM, (h + 1) * HEAD_DIM, axis=axis) for h in range(N_HEADS)]
    return jnp.concatenate([heads[j + GROUP * half] for j in range(GROUP) for half in range(N_KV_HEADS)], axis=axis)


def _fold_rows(a):
    return jnp.concatenate([a[:HALF], a[HALF:][::-1]], axis=0)


def kernel(x, c, ctx, c_ctx, ada_w, ada_b, norm1, norm2, w_in, conv_w, conv_b, filt_w1, filt_b1, filt_w2, filt_b2,
           filt_w3, filt_b3, filt_freq, filt_out, hyena_bias, q_norm, k_norm, attn_sink, w_branch_a, w_branch_b,
           w_out, router, w_gate, w_up, w_down):
    d = D_MODEL
    assert ada_w.shape[0] == 1, "only the single-layer configuration is implemented"
    l = 0
    cos_t, sin_t = _rope_tables()
    t1, t2, t1t, t2t = _dft_tables()
    rot = _phase_tables()
    feat, decay = _filter_features()
    bias = _attn_bias()
    gmat = jnp.kron(jnp.eye(LANES // HEAD_DIM, dtype=F32),
                    jnp.full((HEAD_DIM, HEAD_DIM), 1.0 / HEAD_DIM, F32)).astype(BF16)
    tri = (jnp.arange(SEQ)[:, None] < jnp.arange(SEQ)[None, :]).astype(BF16)
    emat = jnp.repeat(jnp.eye(N_EXPERTS, dtype=BF16), CAP, axis=1)
    c16 = jnp.zeros((MOD_ROWS, d), F32).at[:BATCH].set(c).at[BATCH].set(c_ctx)

    mods3 = _ada(c16, ada_w[l], ada_b[l][None, :]).reshape(MOD_ROWS, 1, 6 * d)
    n1 = norm1[l][None, :]
    w = w_in[l]
    wkv = w[:, OFF_K:OFF_G].astype(BF16)
    gk = jnp.tile(k_norm[l], N_KV_HEADS)[None, :]
    kf, kn = _filters(feat, jnp.pad(filt_w1[l], ((0, FILTER_HIDDEN - FILTER_EMB), (0, 0))), filt_b1[l][None, :],
                      filt_w2[l], filt_b2[l][None, :], filt_w3[l], filt_b3[l][None, :], filt_freq[l][None, :],
                      filt_out[l], decay, rot, t1, t2)
    kc, vc = _ctx_proj(ctx, mods3, n1, wkv, gk, gmat)
    zh, q, k, v, gates = _inproj(
        x, mods3, n1, w[:, :OFF_Q].astype(BF16), _pair_heads(w[:, OFF_Q:OFF_K], 1).astype(BF16), wkv,
        w[:, OFF_G:].astype(BF16), jnp.tile(q_norm[l], N_HEADS)[None, :], gk, gmat, cos_t, sin_t)
    ya = _hyena(zh, conv_w[l], conv_b[l][None, :], hyena_bias[l], kf, kn, t1, t2, t1t, t2t)
    yb = _attention(attn_sink[l] * LOG2E, q, k, v, kc, vc, bias)
    router_pad = jnp.pad(router[l], ((0, 0), (0, LANES - N_EXPERTS)))
    router_hi = router_pad.astype(BF16)
    router_lo = (router_pad - router_hi.astype(F32)).astype(BF16)
    xn, h2, aff_t = _merge(ya, yb, gates, x, mods3, norm2[l][None, :], w_branch_a[l].astype(BF16),
                           _pair_heads(w_branch_b[l], 0).astype(BF16), w_out[l].astype(BF16), router_hi, router_lo)
    lo, hi = _thresholds(aff_t.reshape(BATCH * N_EXPERTS, SEQ))
    xin, g, pos = _route(aff_t, lo, hi, h2, tri)
    y = _experts(xin, g, w_gate[l], w_up[l], w_down[l])
    return _scatter(jnp.swapaxes(pos, 1, 2), y, xn, mods3, emat)
```

```python
import math

import jax
import jax.numpy as jnp
from jax import lax
from jax.experimental import pallas as pl
from jax.experimental.pallas import tpu as pltpu

F32 = jnp.float32
BF16 = jnp.bfloat16
I32 = jnp.int32
HIGHEST = lax.Precision.HIGHEST

D_MODEL = 1024
BATCH = 8
SEQ = 2048
GRID_W = 64
CTX_LEN = 256
N_HEADS = 8
N_KV_HEADS = 2
HEAD_DIM = 64
GROUP = N_HEADS // N_KV_HEADS
ATTN_W = N_HEADS * HEAD_DIM
KV_W = N_KV_HEADS * HEAD_DIM
WINDOW = 128
BLOCK = 128
HYENA_W = D_MODEL // 2
HYENA_ORDER = 2
FILTER_BANDS = 16
FILTER_EMB = 1 + 2 * FILTER_BANDS
FILTER_HIDDEN = 64
DECAY_TARGET = 1e-2
FAST_DECAY_PCT = 0.3
SLOW_DECAY_PCT = 1.5
ROPE_BASE = 10000.0
AXIS_ROT = HEAD_DIM // 2
N_EXPERTS = 16
EC_CAPACITY = 2
D_FF = 2048
EPS = 1e-6
NEG = -1e30
LOG2E = math.log2(math.e)

OFF_Q = 3 * HYENA_W
OFF_K = OFF_Q + ATTN_W
OFF_V = OFF_K + KV_W
OFF_G = OFF_V + KV_W
IN_W = OFF_G + 2 * D_MODEL

CAP = EC_CAPACITY * SEQ // N_EXPERTS
N_FFT = 2 * SEQ
HALF = SEQ // 2
MOD_ROWS = 16
LANES = 128

TM_IN = 512
TM_MERGE = 1024
SUB_MERGE = 512
CT = 256
FC = 512
RB = 256
TF = 512
MC = 512
TS = 512
EXPERT_GROUP = 4
VMEM_LIMIT = 56 * 1024 * 1024


def _dot(a, b, precision=None):
    return jnp.dot(a, b, preferred_element_type=F32, precision=precision)


def _dot_nt(a, b, precision=None):
    return lax.dot_general(a, b, (((1,), (1,)), ((), ())), preferred_element_type=F32, precision=precision)


def _params(sem, vmem=VMEM_LIMIT):
    return pltpu.CompilerParams(dimension_semantics=sem, vmem_limit_bytes=vmem)


def _rms_mod(x, g, sc, sh):
    ms = jnp.mean(x * x, axis=-1, keepdims=True)
    return (x * lax.rsqrt(ms + EPS) * g) * (1.0 + sc) + sh


def _head_norm_rope(z, g, gmat, cos, sin, scale):
    ms = _dot((z * z).astype(BF16), gmat)
    y = z * lax.rsqrt(ms + EPS) * g
    if cos is not None:
        lane = lax.broadcasted_iota(I32, y.shape, 1)
        nxt = pltpu.roll(y, LANES - 1, axis=1)
        prv = pltpu.roll(y, 1, axis=1)
        y = y * cos + jnp.where((lane & 1) == 0, nxt, prv) * sin
    return y * scale


def _ada_body(c_ref, w_ref, b_ref, o_ref):
    c = c_ref[...]
    s = c * jax.nn.sigmoid(c)
    o_ref[...] = _dot(s, w_ref[...], HIGHEST) + b_ref[...]


def _ada(c16, w, b):
    d = D_MODEL
    return pl.pallas_call(
        _ada_body,
        grid=(6,),
        in_specs=[pl.BlockSpec((MOD_ROWS, d), lambda j: (0, 0)),
                  pl.BlockSpec((d, d), lambda j: (0, j)),
                  pl.BlockSpec((1, d), lambda j: (0, j))],
        out_specs=pl.BlockSpec((MOD_ROWS, d), lambda j: (0, j)),
        out_shape=jax.ShapeDtypeStruct((MOD_ROWS, 6 * d), F32),
        compiler_params=_params(("parallel",)),
        name="ada_mod",
    )(c16, w, b)


def _sign_rows(n):
    lane = lax.broadcasted_iota(I32, (8, n), 1)
    sub = lax.broadcasted_iota(I32, (8, n), 0)
    sg = jnp.where((lane & 1) == 0, 1.0, -1.0)
    return jnp.where(sub == 0, sg, 0.0).astype(BF16)


def _filt_body(feat_ref, w1_ref, b1_ref, w2_ref, b2_ref, w3_ref, b3_ref, fq_ref, fof_ref, fob_ref, dec_ref,
               rot_ref, t1_ref, t2_ref, kf_ref, kn_ref, h_ref):
    @pl.when((pl.program_id(0) == 0) & (pl.program_id(1) == 0))
    def _():
        fq = fq_ref[...]
        h = jnp.sin(fq * (_dot(feat_ref[...], w1_ref[...], HIGHEST) + b1_ref[...]))
        h = jnp.sin(fq * (_dot(h, w2_ref[...], HIGHEST) + b2_ref[...]))
        h_ref[...] = jnp.sin(fq * (_dot(h, w3_ref[...], HIGHEST) + b3_ref[...]))

    h = h_ref[...]
    dec = dec_ref[...]
    hf = _dot(h, fof_ref[...], HIGHEST) * dec
    hb = _dot(h, fob_ref[...], HIGHEST) * dec
    row = lax.broadcasted_iota(I32, hf.shape, 0)
    hb = jnp.where(row == 0, 0.0, hb)
    a = hf + hb
    b = hf - hb
    pa = (a[:HALF] + a[HALF:]).astype(BF16)
    ma = (a[:HALF] - a[HALF:]).astype(BF16)
    pb = (b[:HALF] + b[HALF:]).astype(BF16)
    mb = (b[:HALF] - b[HALF:]).astype(BF16)
    t1 = t1_ref[...]
    t2 = t2_ref[...]
    a1 = _dot(t1, pa)
    a2 = _dot(t2, ma)
    b1 = _dot(t1, pb)
    b2 = _dot(t2, mb)
    ce, se, co, so = rot_ref[0], rot_ref[1], rot_ref[2], rot_ref[3]
    kf_ref[0, 0] = a1[:HALF] * ce + a2[HALF:] * se
    kf_ref[0, 1] = b2[HALF:] * ce - b1[:HALF] * se
    kf_ref[0, 2] = a2[:HALF] * co + a1[HALF:] * so
    kf_ref[0, 3] = b1[HALF:] * co - b2[:HALF] * so
    kn_ref[0] = _dot(_sign_rows(HALF), ma)[0:1] * (1.0 / N_FFT)


def _filters(feat, w1, b1, w2, b2, w3, b3, fq, fout, decay, rot, t1, t2):
    nct = HYENA_W // CT
    full = lambda shape: pl.BlockSpec(shape, lambda o, c: (0,) * len(shape))
    return pl.pallas_call(
        _filt_body,
        grid=(HYENA_ORDER, nct),
        in_specs=[full((SEQ, FILTER_HIDDEN)), full((FILTER_HIDDEN, FILTER_HIDDEN)), full((1, FILTER_HIDDEN)),
                  full((FILTER_HIDDEN, FILTER_HIDDEN)), full((1, FILTER_HIDDEN)),
                  full((FILTER_HIDDEN, FILTER_HIDDEN)), full((1, FILTER_HIDDEN)), full((1, FILTER_HIDDEN)),
                  pl.BlockSpec((FILTER_HIDDEN, CT), lambda o, c: (0, (o * 2 + 0) * nct + c)),
                  pl.BlockSpec((FILTER_HIDDEN, CT), lambda o, c: (0, (o * 2 + 1) * nct + c)),
                  pl.BlockSpec((SEQ, CT), lambda o, c: (0, c)),
                  full((4, HALF, CT)),
                  pl.BlockSpec((SEQ, HALF), lambda o, c: (0, 0), pipeline_mode=pl.Buffered(1)),
                  pl.BlockSpec((SEQ, HALF), lambda o, c: (0, 0), pipeline_mode=pl.Buffered(1))],
        out_specs=[pl.BlockSpec((1, 4, HALF, CT), lambda o, c: (o, 0, 0, c)),
                   pl.BlockSpec((1, 1, CT), lambda o, c: (o, 0, c))],
        out_shape=[jax.ShapeDtypeStruct((HYENA_ORDER, 4, HALF, HYENA_W), F32),
                   jax.ShapeDtypeStruct((HYENA_ORDER, 1, HYENA_W), F32)],
        scratch_shapes=[pltpu.VMEM((SEQ, FILTER_HIDDEN), F32)],
        compiler_params=_params(("arbitrary", "arbitrary")),
        name="hyena_filters",
    )(feat, w1, b1, w2, b2, w3, b3, fq, fout, fout, decay, rot, t1, t2)


def _inproj_body(x_ref, sc_ref, sh_ref, n1_ref, wh_ref, wq_ref, wkv_ref, wg_ref, gq_ref, gk_ref, gmat_ref,
                 cos_ref, sin_ref, zh_ref, q_ref, k_ref, v_ref, gate_ref):
    hx = _rms_mod(x_ref[0], n1_ref[...], sc_ref[0], sh_ref[0]).astype(BF16)
    zh_ref[0] = _dot(hx, wh_ref[...]).astype(BF16)
    gmat = gmat_ref[...]
    cos = cos_ref[...]
    sin = sin_ref[...]
    zq = _dot(hx, wq_ref[...])
    for s in range(ATTN_W // LANES):
        sl = slice(s * LANES, (s + 1) * LANES)
        q_ref[0, :, sl] = _head_norm_rope(zq[:, sl], gq_ref[:, sl], gmat, cos, sin, LOG2E * HEAD_DIM ** -0.5).astype(BF16)
    zkv = _dot(hx, wkv_ref[...])
    k_ref[0] = _head_norm_rope(zkv[:, :KV_W], gk_ref[...], gmat, cos, sin, 1.0).astype(BF16)
    v_ref[0] = zkv[:, KV_W:].astype(BF16)
    gate_ref[0] = jax.nn.sigmoid(_dot(hx, wg_ref[...])).astype(BF16)


def _inproj(x, mods3, norm1, wh, wq, wkv, wg, gq, gk, gmat, cos_t, sin_t):
    d = D_MODEL
    nt = SEQ // TM_IN
    const = lambda shape: pl.BlockSpec(shape, lambda b, i: (0,) * len(shape))
    tok = lambda w: pl.BlockSpec((1, TM_IN, w), lambda b, i: (b, i, 0))
    return pl.pallas_call(
        _inproj_body,
        grid=(BATCH, nt),
        in_specs=[tok(d),
                  pl.BlockSpec((1, 1, d), lambda b, i: (b, 0, 1)),
                  pl.BlockSpec((1, 1, d), lambda b, i: (b, 0, 0)),
                  const((1, d)), const((d, OFF_Q)), const((d, ATTN_W)), const((d, 2 * KV_W)), const((d, 2 * d)),
                  const((1, ATTN_W)), const((1, KV_W)), const((LANES, LANES)),
                  pl.BlockSpec((TM_IN, LANES), lambda b, i: (i, 0)),
                  pl.BlockSpec((TM_IN, LANES), lambda b, i: (i, 0))],
        out_specs=[tok(OFF_Q), tok(ATTN_W), tok(KV_W), tok(KV_W), tok(2 * d)],
        out_shape=[jax.ShapeDtypeStruct((BATCH, SEQ, OFF_Q), BF16),
                   jax.ShapeDtypeStruct((BATCH, SEQ, ATTN_W), BF16),
                   jax.ShapeDtypeStruct((BATCH, SEQ, KV_W), BF16),
                   jax.ShapeDtypeStruct((BATCH, SEQ, KV_W), BF16),
                   jax.ShapeDtypeStruct((BATCH, SEQ, 2 * d), BF16)],
        compiler_params=_params(("parallel", "parallel")),
        name="in_proj",
    )(x, mods3, mods3, norm1, wh, wq, wkv, wg, gq, gk, gmat, cos_t, sin_t)


def _ctx_body(c_ref, sc_ref, sh_ref, n1_ref, wkv_ref, gk_ref, gmat_ref, kc_ref, vc_ref):
    hc = _rms_mod(c_ref[0], n1_ref[...], sc_ref[0], sh_ref[0]).astype(BF16)
    z = _dot(hc, wkv_ref[...])
    kc_ref[0] = _head_norm_rope(z[:, :KV_W], gk_ref[...], gmat_ref[...], None, None, 1.0).astype(BF16)
    vc_ref[0] = z[:, KV_W:].astype(BF16)


def _ctx_proj(ctx, mods3, norm1, wkv, gk, gmat):
    d = D_MODEL
    const = lambda shape: pl.BlockSpec(shape, lambda b: (0,) * len(shape))
    return pl.pallas_call(
        _ctx_body,
        grid=(BATCH,),
        in_specs=[pl.BlockSpec((1, CTX_LEN, d), lambda b: (b, 0, 0)),
                  pl.BlockSpec((1, 1, d), lambda b: (BATCH, 0, 1)),
                  pl.BlockSpec((1, 1, d), lambda b: (BATCH, 0, 0)),
                  const((1, d)), const((d, 2 * KV_W)), const((1, KV_W)), const((LANES, LANES))],
        out_specs=[pl.BlockSpec((1, CTX_LEN, KV_W), lambda b: (b, 0, 0)),
                   pl.BlockSpec((1, CTX_LEN, KV_W), lambda b: (b, 0, 0))],
        out_shape=[jax.ShapeDtypeStruct((BATCH, CTX_LEN, KV_W), BF16),
                   jax.ShapeDtypeStruct((BATCH, CTX_LEN, KV_W), BF16)],
        compiler_params=_params(("parallel",)),
        name="ctx_proj",
    )(ctx, mods3, mods3, norm1, wkv, gk, gmat)


def _hyena_body(zv_ref, z1_ref, z2_ref, cwv_ref, cw1_ref, cw2_ref, cbv_ref, cb1_ref, cb2_ref, hb_ref, kf_ref,
                kn_ref, t1_ref, t2_ref, t1t_ref, t2t_ref, o_ref, lo_ref, hi_ref, glo_ref, ghi_ref, p_ref, m_ref,
                za_ref, zb_ref):
    row = lax.broadcasted_iota(I32, (HALF, CT), 0)
    rr = lax.broadcasted_iota(I32, (RB, RB), 0)
    cc = lax.broadcasted_iota(I32, (RB, RB), 1)
    flip = jnp.where(rr + cc == RB - 1, 1.0, 0.0).astype(BF16)
    nrb = HALF // RB

    def folded_short_conv(z_ref, w_ref, b_ref, lo_out, hi_out):
        zlo = z_ref[0, 0:HALF, :].astype(F32)
        for j in range(nrb):
            hi_out[j * RB:(j + 1) * RB, :] = _dot(flip, z_ref[0, SEQ - RB * (j + 1):SEQ - RB * j, :])
        zhi = hi_out[...]
        w0, w1, w2 = w_ref[0:1, :], w_ref[1:2, :], w_ref[2:3, :]
        first, last = row == 0, row == HALF - 1
        lo_prev = jnp.where(first, 0.0, pltpu.roll(zlo, 1, axis=0))
        lo_next = jnp.where(last, zhi[HALF - 1:HALF, :], pltpu.roll(zlo, HALF - 1, axis=0))
        hi_prev = jnp.where(first, 0.0, pltpu.roll(zhi, 1, axis=0))
        hi_next = jnp.where(last, zlo[HALF - 1:HALF, :], pltpu.roll(zhi, HALF - 1, axis=0))
        lo_out[...] = lo_prev * w0 + zlo * w1 + lo_next * w2 + b_ref[...]
        hi_out[...] = hi_next * w0 + zhi * w1 + hi_prev * w2 + b_ref[...]

    folded_short_conv(zv_ref, cwv_ref, cbv_ref, lo_ref, hi_ref)
    sign8 = _sign_rows(HALF)
    odd = (lax.broadcasted_iota(I32, (FC, CT), 0) & 1) == 1
    for o, (zr, cw, cb) in enumerate(((z1_ref, cw1_ref, cb1_ref), (z2_ref, cw2_ref, cb2_ref))):
        folded_short_conv(zr, cw, cb, glo_ref, ghi_ref)
        p_ref[...] = (lo_ref[...] + hi_ref[...]).astype(BF16)
        m_ref[...] = (lo_ref[...] - hi_ref[...]).astype(BF16)
        pv = p_ref[...]
        mv = m_ref[...]
        for c in range(HALF // FC):
            ev = slice(c * FC, (c + 1) * FC)
            od = slice(HALF + c * FC, HALF + (c + 1) * FC)
            xce = _dot(t1_ref[ev, :], pv)
            xso = _dot(t1_ref[od, :], pv)
            xco = _dot(t2_ref[ev, :], mv)
            xse = _dot(t2_ref[od, :], mv)
            kce, kse, kco, kso = kf_ref[o, 0, ev, :], kf_ref[o, 1, ev, :], kf_ref[o, 2, ev, :], kf_ref[o, 3, ev, :]
            za_ref[ev, :] = (xce * kce - xse * kse).astype(BF16)
            za_ref[od, :] = (xco * kso + xso * kco).astype(BF16)
            zb_ref[ev, :] = (xco * kco - xso * kso).astype(BF16)
            zb_ref[od, :] = (xce * kse + xse * kce).astype(BF16)
        zn = _dot(sign8, mv)[0:1] * kn_ref[o]
        bias = hb_ref[o:o + 1, :]
        za = za_ref[...]
        zb = zb_ref[...]
        for c in range(HALF // FC):
            rs = slice(c * FC, (c + 1) * FC)
            half_p = _dot(t1t_ref[rs, :], za)
            half_m = _dot(t2t_ref[rs, :], zb) + jnp.where(odd, -zn, zn)
            lo_ref[rs, :] = glo_ref[rs, :] * (half_p + half_m + bias * lo_ref[rs, :])
            hi_ref[rs, :] = ghi_ref[rs, :] * (half_p - half_m + bias * hi_ref[rs, :])
    o_ref[0, 0:HALF, :] = lo_ref[...].astype(BF16)
    for j in range(nrb):
        o_ref[0, SEQ - RB * (j + 1):SEQ - RB * j, :] = _dot(
            flip, hi_ref[j * RB:(j + 1) * RB, :].astype(BF16)).astype(BF16)


def _hyena(zh, conv_w, conv_b, hbias, kf, kn, t1, t2, t1t, t2t):
    nct = HYENA_W // CT
    zspec = lambda k: pl.BlockSpec((1, SEQ, CT), lambda c, b: (b, 0, k * nct + c))
    wspec = lambda k: pl.BlockSpec((3, CT), lambda c, b: (0, k * nct + c))
    bspec = lambda k: pl.BlockSpec((1, CT), lambda c, b: (0, k * nct + c))
    table = lambda shape: pl.BlockSpec(shape, lambda c, b: (0, 0), pipeline_mode=pl.Buffered(1))
    half_f32 = pltpu.VMEM((HALF, CT), F32)
    return pl.pallas_call(
        _hyena_body,
        grid=(nct, BATCH),
        in_specs=[zspec(0), zspec(1), zspec(2), wspec(0), wspec(1), wspec(2), bspec(0), bspec(1), bspec(2),
                  pl.BlockSpec((HYENA_ORDER, CT), lambda c, b: (0, c)),
                  pl.BlockSpec((HYENA_ORDER, 4, HALF, CT), lambda c, b: (0, 0, 0, c)),
                  pl.BlockSpec((HYENA_ORDER, 1, CT), lambda c, b: (0, 0, c)),
                  table((SEQ, HALF)), table((SEQ, HALF)), table((HALF, SEQ)), table((HALF, SEQ))],
        out_specs=pl.BlockSpec((1, SEQ, CT), lambda c, b: (b, 0, c)),
        out_shape=jax.ShapeDtypeStruct((BATCH, SEQ, HYENA_W), BF16),
        scratch_shapes=[half_f32, half_f32, half_f32, half_f32,
                        pltpu.VMEM((HALF, CT), BF16), pltpu.VMEM((HALF, CT), BF16),
                        pltpu.VMEM((SEQ, CT), BF16), pltpu.VMEM((SEQ, CT), BF16)],
        compiler_params=_params(("parallel", "parallel")),
        name="hyena_conv",
    )(zh, zh, zh, conv_w, conv_w, conv_w, conv_b, conv_b, conv_b, hbias, kf, kn, t1, t2, t1t, t2t)


def _attn_body(sink_ref, q_ref, k_ref, v_ref, kc_ref, vc_ref, bias_ref, o_ref, kp_ref, vlo_ref, vhi_ref):
    nb = SEQ // BLOCK
    lane = lax.broadcasted_iota(I32, (BLOCK, LANES), 1)
    low = lane < HEAD_DIM
    mask_lo = jnp.where(low, 1.0, 0.0).astype(BF16)
    mask_hi = jnp.where(low, 0.0, 1.0).astype(BF16)

    def with_ones(v):
        lane_v = lax.broadcasted_iota(I32, v.shape, 1) < HEAD_DIM
        one = jnp.ones_like(v)
        return jnp.where(lane_v, v, one), jnp.where(lane_v, one, v)

    zpad = jnp.zeros((BLOCK, KV_W), BF16)
    kp_ref[0:BLOCK] = zpad
    kp_ref[BLOCK:BLOCK + SEQ] = k_ref[0]
    kp_ref[BLOCK + SEQ:] = zpad
    v_lo, v_hi = with_ones(v_ref[0])
    for ref, val in ((vlo_ref, v_lo), (vhi_ref, v_hi)):
        ref[0:BLOCK] = zpad
        ref[BLOCK:BLOCK + SEQ] = val
        ref[BLOCK + SEQ:] = zpad
    kc = kc_ref[0]
    vc_pair = with_ones(vc_ref[0])

    def block(n, carry):
        r = pl.multiple_of(n * BLOCK, BLOCK)
        kw = kp_ref[pl.ds(r, 3 * BLOCK), :]
        vw_pair = (vlo_ref[pl.ds(r, 3 * BLOCK), :], vhi_ref[pl.ds(r, 3 * BLOCK), :])
        bias = bias_ref[jnp.where(n == 0, 0, jnp.where(n == nb - 1, 2, 1))]
        scores = []
        for j in range(GROUP):
            qs = q_ref[0, pl.ds(r, BLOCK), j * LANES:(j + 1) * LANES]
            for msk in (mask_lo, mask_hi):
                qm = qs * msk
                scores.append((_dot_nt(qm, kw) + bias, _dot_nt(qm, kc)))
        probs = []
        for idx, (sw, sc) in enumerate(scores):
            snk = sink_ref[idx // 2 + GROUP * (idx % 2)]
            m = jnp.maximum(jnp.maximum(jnp.max(sw, axis=-1, keepdims=True),
                                        jnp.max(sc, axis=-1, keepdims=True)), snk)
            probs.append((jnp.exp2(sw - m).astype(BF16), jnp.exp2(sc - m).astype(BF16), jnp.exp2(snk - m)))
        outs = []
        for idx, (pw, pc, psink) in enumerate(probs):
            acc = _dot(pw, vw_pair[idx % 2]) + _dot(pc, vc_pair[idx % 2])
            den = pltpu.roll(acc, HEAD_DIM, axis=1) + psink
            outs.append(acc / den)
        for j in range(GROUP):
            o_ref[0, pl.ds(r, BLOCK), j * LANES:(j + 1) * LANES] = jnp.where(
                low, outs[2 * j], outs[2 * j + 1]).astype(BF16)
        return carry

    lax.fori_loop(0, nb, block, 0)


def _attention(sink, q, k, v, kc, vc, bias):
    per_b = lambda n, w: pl.BlockSpec((1, n, w), lambda b: (b, 0, 0))
    return pl.pallas_call(
        _attn_body,
        grid=(BATCH,),
        in_specs=[pl.BlockSpec(memory_space=pltpu.SMEM),
                  per_b(SEQ, ATTN_W), per_b(SEQ, KV_W), per_b(SEQ, KV_W), per_b(CTX_LEN, KV_W), per_b(CTX_LEN, KV_W),
                  pl.BlockSpec((3, BLOCK, 3 * BLOCK), lambda b: (0, 0, 0))],
        out_specs=per_b(SEQ, ATTN_W),
        out_shape=jax.ShapeDtypeStruct((BATCH, SEQ, ATTN_W), BF16),
        scratch_shapes=[pltpu.VMEM((SEQ + 2 * BLOCK, KV_W), BF16)] * 3,
        compiler_params=_params(("parallel",)),
        name="window_attn",
    )(sink, q, k, v, kc, vc, bias)


def _merge_body(ya_ref, yb_ref, gt_ref, x_ref, g1_ref, sc2_ref, sh2_ref, n2_ref, wa_ref, wb_ref, wo_ref, rh_ref,
                rl_ref, xn_ref, h2_ref, aff_ref):
    d = D_MODEL
    groups = [slice(i * SUB_MERGE, (i + 1) * SUB_MERGE) for i in range(TM_MERGE // SUB_MERGE)]
    branch = [(_dot(ya_ref[0, s, :], wa_ref[...]), _dot(yb_ref[0, s, :], wb_ref[...])) for s in groups]
    mixed = [(gt_ref[0, s, :d].astype(F32) * ua + gt_ref[0, s, d:].astype(F32) * ub).astype(BF16)
             for s, (ua, ub) in zip(groups, branch)]
    proj = [_dot(u, wo_ref[...]) for u in mixed]
    h2_parts = []
    for s, m in zip(groups, proj):
        xn = x_ref[0, s, :] + g1_ref[0] * m
        xn_ref[0, s, :] = xn
        h2 = _rms_mod(xn, n2_ref[...], sc2_ref[0], sh2_ref[0])
        h2_hi = h2.astype(BF16)
        h2_ref[0, s, :] = h2_hi
        h2_parts.append((h2_hi, (h2 - h2_hi.astype(F32)).astype(BF16)))
    rh = rh_ref[...]
    logits = [_dot_nt(rh, hi) + _dot_nt(rh, lo) + _dot_nt(rl_ref[...], hi) for hi, lo in h2_parts]
    for s, lt in zip(groups, logits):
        e = jnp.exp(lt - jnp.max(lt, axis=0, keepdims=True))
        aff_ref[0, :, s] = e / jnp.sum(e, axis=0, keepdims=True)


def _merge(ya, yb, gates, x, mods3, norm2, wa, wb, wo, router_hi, router_lo):
    d = D_MODEL
    nt = SEQ // TM_MERGE
    const = lambda shape: pl.BlockSpec(shape, lambda b, i: (0,) * len(shape))
    tok = lambda w: pl.BlockSpec((1, TM_MERGE, w), lambda b, i: (b, i, 0))
    mod = lambda k: pl.BlockSpec((1, 1, d), lambda b, i: (b, 0, k))
    return pl.pallas_call(
        _merge_body,
        grid=(BATCH, nt),
        in_specs=[tok(HYENA_W), tok(ATTN_W), tok(2 * d), tok(d), mod(2), mod(4), mod(3), const((1, d)),
                  const((HYENA_W, d)), const((ATTN_W, d)), const((d, d)), const((N_EXPERTS, d)), const((N_EXPERTS, d))],
        out_specs=[tok(d), tok(d), pl.BlockSpec((1, N_EXPERTS, TM_MERGE), lambda b, i: (b, 0, i))],
        out_shape=[jax.ShapeDtypeStruct((BATCH, SEQ, d), F32),
                   jax.ShapeDtypeStruct((BATCH, SEQ, d), BF16),
                   jax.ShapeDtypeStruct((BATCH, N_EXPERTS, SEQ), F32)],
        compiler_params=_params(("parallel", "parallel")),
        name="merge_outproj",
    )(ya, yb, gates, x, mods3, mods3, mods3, norm2, wa, wb, wo, router_hi, router_lo)


def _thresh_body(aff_ref, lo_ref, hi_ref):
    aff = aff_ref[...]
    rows = aff.shape[0]
    bits = pltpu.bitcast(aff, I32)

    def bisect_bits(_, carry):
        lo, hi = carry
        mid = lo + ((hi - lo) >> 1)
        ge = jnp.sum((bits >= mid).astype(I32), axis=1, keepdims=True) >= CAP
        return jnp.where(ge, mid, lo), jnp.where(ge, hi, mid)

    lo0 = jnp.zeros((rows, 1), I32)
    hi0 = jnp.full((rows, 1), 0x3F800001, I32)
    thr_bits, _ = lax.fori_loop(0, 31, bisect_bits, (lo0, hi0))
    thr = pltpu.bitcast(thr_bits, F32)

    def bisect_val(_, carry):
        lo, hi = carry
        mid = 0.5 * (lo + hi)
        ge = jnp.sum(jnp.where(aff >= mid, 1.0, 0.0), axis=1, keepdims=True) >= CAP
        return jnp.where(ge, mid, lo), jnp.where(ge, hi, mid)

    lo, hi = lax.fori_loop(0, 30, bisect_val, (0.5 * thr, jnp.maximum(2.0 * thr, 1e-30)))
    lo_ref[...] = jnp.broadcast_to(lo, lo_ref.shape)
    hi_ref[...] = jnp.broadcast_to(hi, hi_ref.shape)


def _thresholds(aff_rows):
    rows = BATCH * N_EXPERTS
    return pl.pallas_call(
        _thresh_body,
        grid=(1,),
        in_specs=[pl.BlockSpec((rows, SEQ), lambda i: (0, 0))],
        out_specs=[pl.BlockSpec((rows, LANES), lambda i: (0, 0)), pl.BlockSpec((rows, LANES), lambda i: (0, 0))],
        out_shape=[jax.ShapeDtypeStruct((rows, LANES), F32), jax.ShapeDtypeStruct((rows, LANES), F32)],
        compiler_params=_params(("arbitrary",)),
        name="route_threshold",
    )(aff_rows)


def _prefix_counts(mask):
    r = lax.broadcasted_iota(I32, (LANES, LANES), 0)
    c = lax.broadcasted_iota(I32, (LANES, LANES), 1)
    upper = jnp.where(r <= c, 1.0, 0.0).astype(BF16)
    offset = jnp.zeros((mask.shape[0], 1), F32)
    blocks = []
    for j in range(mask.shape[1] // LANES):
        blk = mask[:, j * LANES:(j + 1) * LANES]
        inc = _dot(blk.astype(BF16), upper)
        blocks.append(inc - blk + offset)
        offset = offset + inc[:, LANES - 1:LANES]
    return jnp.concatenate(blocks, axis=1)


def _route_body(aff_ref, lo_ref, hi_ref, h_ref, xin_ref, g_ref, pos_ref, pbuf_ref):
    aff = aff_ref[0]
    above = jnp.where(aff >= hi_ref[:, 0:1], 1.0, 0.0)
    band = jnp.where(aff >= lo_ref[:, 0:1], 1.0, 0.0) - above
    need = CAP - jnp.sum(above, axis=1, keepdims=True)
    tie_rank = _prefix_counts(band)
    self = above + band * jnp.where(tie_rank < need, 1.0, 0.0)
    pos = _prefix_counts(self)
    posi = jnp.where(self > 0.5, pos.astype(I32), -1)
    pos_ref[0] = posi

    h = h_ref[0]
    slot = lax.broadcasted_iota(I32, (CAP, SEQ), 0)
    for grp in range(N_EXPERTS // EXPERT_GROUP):
        for i in range(EXPERT_GROUP):
            e = grp * EXPERT_GROUP + i
            hit = posi[e:e + 1, :] == slot
            pbuf_ref[i * CAP:(i + 1) * CAP, :] = jnp.where(hit, 1.0, 0.0).astype(BF16)
            g_ref[e] = jnp.sum(jnp.where(hit, aff[e:e + 1, :], 0.0), axis=1, keepdims=True)
        rows = _dot(pbuf_ref[...], h)
        for i in range(EXPERT_GROUP):
            xin_ref[grp * EXPERT_GROUP + i] = rows[i * CAP:(i + 1) * CAP].astype(BF16)


def _route(aff_t, lo, hi, h2):
    d = D_MODEL
    return pl.pallas_call(
        _route_body,
        grid=(BATCH,),
        in_specs=[pl.BlockSpec((1, N_EXPERTS, SEQ), lambda b: (b, 0, 0)),
                  pl.BlockSpec((N_EXPERTS, LANES), lambda b: (b, 0)),
                  pl.BlockSpec((N_EXPERTS, LANES), lambda b: (b, 0)),
                  pl.BlockSpec((1, SEQ, d), lambda b: (b, 0, 0))],
        out_specs=[pl.BlockSpec((N_EXPERTS, CAP, d), lambda b: (0, b, 0)),
                   pl.BlockSpec((N_EXPERTS, CAP, 1), lambda b: (0, b, 0)),
                   pl.BlockSpec((1, N_EXPERTS, SEQ), lambda b: (b, 0, 0))],
        out_shape=[jax.ShapeDtypeStruct((N_EXPERTS, BATCH * CAP, d), BF16),
                   jax.ShapeDtypeStruct((N_EXPERTS, BATCH * CAP, 1), F32),
                   jax.ShapeDtypeStruct((BATCH, N_EXPERTS, SEQ), I32)],
        scratch_shapes=[pltpu.VMEM((EXPERT_GROUP * CAP, SEQ), BF16)],
        compiler_params=_params(("parallel",)),
        name="route_gather",
    )(aff_t, lo, hi, h2)


def _expert_body(x_ref, g_ref, wg_ref, wu_ref, wd_ref, o_ref, acc_ref, wgb_ref, wub_ref, wdb_ref):
    f = pl.program_id(1)
    wgb_ref[...] = wg_ref[0].astype(BF16)
    wub_ref[...] = wu_ref[0].astype(BF16)
    wdb_ref[...] = wd_ref[0].astype(BF16)

    @pl.when(f == 0)
    def _():
        acc_ref[...] = jnp.zeros_like(acc_ref)

    for m in range(BATCH * CAP // MC):
        rows = slice(m * MC, (m + 1) * MC)
        xm = x_ref[0, rows, :]
        a = _dot(xm, wgb_ref[...])
        b = _dot(xm, wub_ref[...])
        hh = (a * jax.nn.sigmoid(a) * b).astype(BF16)
        acc_ref[rows, :] += _dot(hh, wdb_ref[...])

    @pl.when(f == D_FF // TF - 1)
    def _():
        o_ref[0] = (acc_ref[...] * g_ref[0]).astype(BF16)


def _experts(xin, g, w_gate, w_up, w_down):
    d = D_MODEL
    rows = BATCH * CAP
    return pl.pallas_call(
        _expert_body,
        grid=(N_EXPERTS, D_FF // TF),
        in_specs=[pl.BlockSpec((1, rows, d), lambda e, f: (e, 0, 0)),
                  pl.BlockSpec((1, rows, 1), lambda e, f: (e, 0, 0)),
                  pl.BlockSpec((1, d, TF), lambda e, f: (e, 0, f)),
                  pl.BlockSpec((1, d, TF), lambda e, f: (e, 0, f)),
                  pl.BlockSpec((1, TF, d), lambda e, f: (e, f, 0))],
        out_specs=pl.BlockSpec((1, rows, d), lambda e, f: (e, 0, 0)),
        out_shape=jax.ShapeDtypeStruct((N_EXPERTS, rows, d), BF16),
        scratch_shapes=[pltpu.VMEM((rows, d), F32), pltpu.VMEM((d, TF), BF16), pltpu.VMEM((d, TF), BF16),
                        pltpu.VMEM((TF, d), BF16)],
        compiler_params=_params(("parallel", "arbitrary")),
        name="swiglu_experts",
    )(xin, g, w_gate, w_up, w_down)


def _scatter_body(pos_ref, y_ref, x_ref, g2_ref, o_ref):
    pos = pos_ref[0]
    slot = lax.broadcasted_iota(I32, (TS, CAP), 1)
    onehot = jnp.concatenate([jnp.where(pos[:, e:e + 1] == slot, 1.0, 0.0).astype(BF16) for e in range(N_EXPERTS)],
                             axis=1)
    y = y_ref[...].reshape(N_EXPERTS * CAP, D_MODEL)
    o_ref[0] = x_ref[0] + g2_ref[0] * _dot(onehot, y)


def _scatter(pos_tok, y, xn, mods3):
    d = D_MODEL
    return pl.pallas_call(
        _scatter_body,
        grid=(BATCH, SEQ // TS),
        in_specs=[pl.BlockSpec((1, TS, N_EXPERTS), lambda b, i: (b, i, 0)),
                  pl.BlockSpec((N_EXPERTS, CAP, d), lambda b, i: (0, b, 0)),
                  pl.BlockSpec((1, TS, d), lambda b, i: (b, i, 0)),
                  pl.BlockSpec((1, 1, d), lambda b, i: (b, 0, 5))],
        out_specs=pl.BlockSpec((1, TS, d), lambda b, i: (b, i, 0)),
        out_shape=jax.ShapeDtypeStruct((BATCH, SEQ, d), F32),
        compiler_params=_params(("parallel", "parallel")),
        name="scatter_residual",
    )(pos_tok, y, xn, mods3)


def _rope_tables():
    rows = SEQ // GRID_W
    row = jnp.repeat(jnp.arange(rows, dtype=F32), GRID_W)
    col = jnp.tile(jnp.arange(GRID_W, dtype=F32), rows)
    inv = ROPE_BASE ** (-jnp.arange(0, AXIS_ROT, 2, dtype=F32) / AXIS_ROT)
    ang = jnp.concatenate([row[:, None] * inv, col[:, None] * inv], axis=-1)
    cos = jnp.repeat(jnp.cos(ang), 2, axis=-1)
    sin = jnp.stack([-jnp.sin(ang), jnp.sin(ang)], axis=-1).reshape(SEQ, HEAD_DIM)
    reps = LANES // HEAD_DIM
    return jnp.tile(cos, (1, reps)), jnp.tile(sin, (1, reps))


def _cos_sin_outer(f, mults):
    unit = math.pi / N_FFT
    step = 32
    coarse = 2 * step * jnp.arange(mults.shape[0] // step, dtype=I32)
    fine = mults[:step]
    a = ((f[:, None] * coarse[None, :]) % (2 * N_FFT)).astype(F32) * unit
    b = ((f[:, None] * fine[None, :]) % (2 * N_FFT)).astype(F32) * unit
    ca, sa, cb, sb = jnp.cos(a)[:, :, None], jnp.sin(a)[:, :, None], jnp.cos(b)[:, None, :], jnp.sin(b)[:, None, :]
    n = f.shape[0]
    return (ca * cb - sa * sb).reshape(n, -1), (sa * cb + ca * sb).reshape(n, -1)


def _dft_tables():
    idx = jnp.arange(HALF, dtype=I32)
    t2p1 = 2 * idx + 1
    ce, se = _cos_sin_outer(2 * idx, t2p1)
    co, so = _cos_sin_outer(2 * idx + 1, t2p1)
    t1 = jnp.concatenate([ce, so], axis=0).astype(BF16)
    t2 = jnp.concatenate([co, se], axis=0).astype(BF16)
    return t1, t2, t1.T, t2.T


def _phase_tables():
    idx = jnp.arange(HALF, dtype=F32)
    w = jnp.full((HALF,), 2.0 / N_FFT, F32)
    we = w.at[0].set(1.0 / N_FFT)
    pe = (math.pi / N_FFT) * (2.0 * idx)
    po = (math.pi / N_FFT) * (2.0 * idx + 1.0)
    rot = jnp.stack([we * jnp.cos(pe), we * jnp.sin(pe), w * jnp.cos(po), w * jnp.sin(po)])
    return jnp.broadcast_to(rot[:, :, None], (4, HALF, CT))


def _filter_features():
    t = jnp.linspace(0.0, 1.0, SEQ, dtype=F32)[:, None]
    w = 2.0 * math.pi * jnp.arange(SEQ, dtype=F32)[:, None] / SEQ
    fr = jnp.linspace(1e-4, FILTER_BANDS - 1, FILTER_BANDS, dtype=F32)[None, :]
    feat = jnp.concatenate([t, jnp.cos(fr * w), -jnp.sin(fr * w)], axis=-1)
    feat = jnp.pad(feat, ((0, 0), (0, FILTER_HIDDEN - FILTER_EMB)))
    min_decay = math.log(DECAY_TARGET) / SLOW_DECAY_PCT
    max_decay = math.log(DECAY_TARGET) / FAST_DECAY_PCT
    deltas = jnp.linspace(min_decay, max_decay, HYENA_W, dtype=F32)
    return _fold_rows(feat), _fold_rows(jnp.exp(-t * jnp.abs(deltas)))


def _attn_bias():
    qi = jnp.arange(BLOCK)[:, None]
    kj = jnp.arange(3 * BLOCK)[None, :]
    band = jnp.abs(kj - BLOCK - qi) <= WINDOW
    first = band & (kj >= BLOCK)
    last = band & (kj < 2 * BLOCK)
    return jnp.where(jnp.stack([first, band, last]), 0.0, NEG).astype(F32)


def _pair_heads(w, axis):
    heads = [lax.slice_in_dim(w, h * HEAD_DIM, (h + 1) * HEAD_DIM, axis=axis) for h in range(N_HEADS)]
    return jnp.concatenate([heads[j + GROUP * half] for j in range(GROUP) for half in range(N_KV_HEADS)], axis=axis)


def _fold_rows(a):
    return jnp.concatenate([a[:HALF], a[HALF:][::-1]], axis=0)


def kernel(x, c, ctx, c_ctx, ada_w, ada_b, norm1, norm2, w_in, conv_w, conv_b, filt_w1, filt_b1, filt_w2, filt_b2,
           filt_w3, filt_b3, filt_freq, filt_out, hyena_bias, q_norm, k_norm, attn_sink, w_branch_a, w_branch_b,
           w_out, router, w_gate, w_up, w_down):
    d = D_MODEL
    assert ada_w.shape[0] == 1, "only the single-layer configuration is implemented"
    l = 0
    cos_t, sin_t = _rope_tables()
    t1, t2, t1t, t2t = _dft_tables()
    rot = _phase_tables()
    feat, decay = _filter_features()
    bias = _attn_bias()
    gmat = jnp.kron(jnp.eye(LANES // HEAD_DIM, dtype=F32),
                    jnp.full((HEAD_DIM, HEAD_DIM), 1.0 / HEAD_DIM, F32)).astype(BF16)
    c16 = jnp.zeros((MOD_ROWS, d), F32).at[:BATCH].set(c).at[BATCH].set(c_ctx)

    mods3 = _ada(c16, ada_w[l], ada_b[l][None, :]).reshape(MOD_ROWS, 1, 6 * d)
    n1 = norm1[l][None, :]
    w = w_in[l]
    wkv = w[:, OFF_K:OFF_G].astype(BF16)
    gk = jnp.tile(k_norm[l], N_KV_HEADS)[None, :]
    kf, kn = _filters(feat, jnp.pad(filt_w1[l], ((0, FILTER_HIDDEN - FILTER_EMB), (0, 0))), filt_b1[l][None, :],
                      filt_w2[l], filt_b2[l][None, :], filt_w3[l], filt_b3[l][None, :], filt_freq[l][None, :],
                      filt_out[l], decay, rot, t1, t2)
    kc, vc = _ctx_proj(ctx, mods3, n1, wkv, gk, gmat)
    zh, q, k, v, gates = _inproj(
        x, mods3, n1, w[:, :OFF_Q].astype(BF16), _pair_heads(w[:, OFF_Q:OFF_K], 1).astype(BF16), wkv,
        w[:, OFF_G:].astype(BF16), jnp.tile(q_norm[l], N_HEADS)[None, :], gk, gmat, cos_t, sin_t)
    ya = _hyena(zh, conv_w[l], conv_b[l][None, :], hyena_bias[l], kf, kn, t1, t2, t1t, t2t)
    yb = _attention(attn_sink[l] * LOG2E, q, k, v, kc, vc, bias)
    router_t = router[l].T
    router_hi = router_t.astype(BF16)
    router_lo = (router_t - router_hi.astype(F32)).astype(BF16)
    xn, h2, aff_t = _merge(ya, yb, gates, x, mods3, norm2[l][None, :], w_branch_a[l].astype(BF16),
                           _pair_heads(w_branch_b[l], 0).astype(BF16), w_out[l].astype(BF16), router_hi, router_lo)
    lo, hi = _thresholds(aff_t.reshape(BATCH * N_EXPERTS, SEQ))
    xin, g, pos = _route(aff_t, lo, hi, h2)
    y = _experts(xin, g, w_gate[l], w_up[l], w_down[l])
    return _scatter(jnp.swapaxes(pos, 1, 2), y, xn, mods3)
```

```python
import math

import numpy as np
import jax
import jax.numpy as jnp
from jax import lax
from jax.experimental import pallas as pl
from jax.experimental.pallas import tpu as pltpu

F32 = jnp.float32
BF16 = jnp.bfloat16
I32 = jnp.int32
HIGHEST = lax.Precision.HIGHEST

D_MODEL = 1024
BATCH = 8
SEQ = 2048
GRID_W = 64
CTX_LEN = 256
N_HEADS = 8
N_KV_HEADS = 2
HEAD_DIM = 64
GROUP = N_HEADS // N_KV_HEADS
ATTN_W = N_HEADS * HEAD_DIM
KV_W = N_KV_HEADS * HEAD_DIM
WINDOW = 128
BLOCK = 128
HYENA_W = D_MODEL // 2
HYENA_ORDER = 2
FILTER_BANDS = 16
FILTER_EMB = 1 + 2 * FILTER_BANDS
FILTER_HIDDEN = 64
DECAY_TARGET = 1e-2
FAST_DECAY_PCT = 0.3
SLOW_DECAY_PCT = 1.5
ROPE_BASE = 10000.0
AXIS_ROT = HEAD_DIM // 2
N_EXPERTS = 16
EC_CAPACITY = 2
D_FF = 2048
EPS = 1e-6
NEG = -1e30
LOG2E = math.log2(math.e)

OFF_Q = 3 * HYENA_W
OFF_K = OFF_Q + ATTN_W
OFF_V = OFF_K + KV_W
OFF_G = OFF_V + KV_W
IN_W = OFF_G + 2 * D_MODEL

CAP = EC_CAPACITY * SEQ // N_EXPERTS
N_FFT = 2 * SEQ
HALF = SEQ // 2
MOD_ROWS = 16
LANES = 128

TM_IN = 512
TM_MERGE = 1024
SUB_MERGE = 512
CT = 256
FC = 512
RB = 256
TF = 512
MC = 512
TS = 512
EXPERT_GROUP = 4
VMEM_LIMIT = 56 * 1024 * 1024


def _dot(a, b, precision=None):
    return jnp.dot(a, b, preferred_element_type=F32, precision=precision)


def _dot_nt(a, b, precision=None):
    return lax.dot_general(a, b, (((1,), (1,)), ((), ())), preferred_element_type=F32, precision=precision)


def _params(sem, vmem=VMEM_LIMIT):
    return pltpu.CompilerParams(dimension_semantics=sem, vmem_limit_bytes=vmem)


def _rms_mod(x, g, sc, sh):
    ms = jnp.mean(x * x, axis=-1, keepdims=True)
    return (x * lax.rsqrt(ms + EPS) * g) * (1.0 + sc) + sh


def _head_norm_rope(z, g, gmat, cos, sin, scale):
    ms = _dot((z * z).astype(BF16), gmat)
    y = z * lax.rsqrt(ms + EPS) * g
    if cos is not None:
        lane = lax.broadcasted_iota(I32, y.shape, 1)
        nxt = pltpu.roll(y, LANES - 1, axis=1)
        prv = pltpu.roll(y, 1, axis=1)
        y = y * cos + jnp.where((lane & 1) == 0, nxt, prv) * sin
    return y * scale


def _split_bf16(x):
    hi = x.astype(BF16)
    return hi, (x - hi.astype(F32)).astype(BF16)


def _ada_body(c_ref, w_ref, b_ref, o_ref):
    c = c_ref[...]
    s_hi, s_lo = _split_bf16(c * jax.nn.sigmoid(c))
    w_hi, w_lo = _split_bf16(w_ref[...])
    o_ref[...] = _dot(s_hi, w_hi) + _dot(s_lo, w_hi) + _dot(s_hi, w_lo) + b_ref[...]


def _ada(c16, w, b):
    d = D_MODEL
    return pl.pallas_call(
        _ada_body,
        grid=(6,),
        in_specs=[pl.BlockSpec((MOD_ROWS, d), lambda j: (0, 0)),
                  pl.BlockSpec((d, d), lambda j: (0, j)),
                  pl.BlockSpec((1, d), lambda j: (0, j))],
        out_specs=pl.BlockSpec((MOD_ROWS, d), lambda j: (0, j)),
        out_shape=jax.ShapeDtypeStruct((MOD_ROWS, 6 * d), F32),
        compiler_params=_params(("parallel",)),
        name="ada_mod",
    )(c16, w, b)


def _sign_rows(n):
    lane = lax.broadcasted_iota(I32, (8, n), 1)
    sub = lax.broadcasted_iota(I32, (8, n), 0)
    sg = jnp.where((lane & 1) == 0, 1.0, -1.0)
    return jnp.where(sub == 0, sg, 0.0).astype(BF16)


def _filt_body(feat_ref, w1_ref, b1_ref, w2_ref, b2_ref, w3_ref, b3_ref, fq_ref, fof_ref, fob_ref, dec_ref,
               rot_ref, t1_ref, t2_ref, kf_ref, kn_ref, h_ref):
    @pl.when((pl.program_id(0) == 0) & (pl.program_id(1) == 0))
    def _():
        fq = fq_ref[...]
        h = jnp.sin(fq * (_dot(w1_ref[...], feat_ref[...], HIGHEST) + b1_ref[...]))
        h = jnp.sin(fq * (_dot(w2_ref[...], h, HIGHEST) + b2_ref[...]))
        h = jnp.sin(fq * (_dot(w3_ref[...], h, HIGHEST) + b3_ref[...]))
        h_ref[...] = h.T

    h = h_ref[...]
    dec = dec_ref[...]
    hf = _dot(h, fof_ref[...], HIGHEST) * dec
    hb = _dot(h, fob_ref[...], HIGHEST) * dec
    row = lax.broadcasted_iota(I32, hf.shape, 0)
    hb = jnp.where(row == 0, 0.0, hb)
    a = hf + hb
    b = hf - hb
    pa = (a[:HALF] + a[HALF:]).astype(BF16)
    ma = (a[:HALF] - a[HALF:]).astype(BF16)
    pb = (b[:HALF] + b[HALF:]).astype(BF16)
    mb = (b[:HALF] - b[HALF:]).astype(BF16)
    t1 = t1_ref[...]
    t2 = t2_ref[...]
    a1 = _dot(t1, pa)
    a2 = _dot(t2, ma)
    b1 = _dot(t1, pb)
    b2 = _dot(t2, mb)
    ce, se, co, so = rot_ref[0], rot_ref[1], rot_ref[2], rot_ref[3]
    kf_ref[0, 0] = a1[:HALF] * ce + a2[HALF:] * se
    kf_ref[0, 1] = b2[HALF:] * ce - b1[:HALF] * se
    kf_ref[0, 2] = a2[:HALF] * co + a1[HALF:] * so
    kf_ref[0, 3] = b1[HALF:] * co - b2[:HALF] * so
    kn_ref[0] = _dot(_sign_rows(HALF), ma)[0:1] * (1.0 / N_FFT)


def _filters(feat, w1, b1, w2, b2, w3, b3, fq, fout, decay, rot, t1, t2):
    nct = HYENA_W // CT
    full = lambda shape: pl.BlockSpec(shape, lambda o, c: (0,) * len(shape))
    return pl.pallas_call(
        _filt_body,
        grid=(HYENA_ORDER, nct),
        in_specs=[full((FILTER_HIDDEN, SEQ)), full((FILTER_HIDDEN, FILTER_HIDDEN)), full((FILTER_HIDDEN, 1)),
                  full((FILTER_HIDDEN, FILTER_HIDDEN)), full((FILTER_HIDDEN, 1)),
                  full((FILTER_HIDDEN, FILTER_HIDDEN)), full((FILTER_HIDDEN, 1)), full((FILTER_HIDDEN, 1)),
                  pl.BlockSpec((FILTER_HIDDEN, CT), lambda o, c: (0, (o * 2 + 0) * nct + c)),
                  pl.BlockSpec((FILTER_HIDDEN, CT), lambda o, c: (0, (o * 2 + 1) * nct + c)),
                  pl.BlockSpec((SEQ, CT), lambda o, c: (0, c)),
                  full((4, HALF, CT)),
                  pl.BlockSpec((SEQ, HALF), lambda o, c: (0, 0), pipeline_mode=pl.Buffered(1)),
                  pl.BlockSpec((SEQ, HALF), lambda o, c: (0, 0), pipeline_mode=pl.Buffered(1))],
        out_specs=[pl.BlockSpec((1, 4, HALF, CT), lambda o, c: (o, 0, 0, c)),
                   pl.BlockSpec((1, 1, CT), lambda o, c: (o, 0, c))],
        out_shape=[jax.ShapeDtypeStruct((HYENA_ORDER, 4, HALF, HYENA_W), F32),
                   jax.ShapeDtypeStruct((HYENA_ORDER, 1, HYENA_W), F32)],
        scratch_shapes=[pltpu.VMEM((SEQ, FILTER_HIDDEN), F32)],
        compiler_params=_params(("arbitrary", "arbitrary")),
        name="hyena_filters",
    )(feat, w1, b1, w2, b2, w3, b3, fq, fout, fout, decay, rot, t1, t2)


def _inproj_body(x_ref, sc_ref, sh_ref, n1_ref, wh_ref, wq_ref, wkv_ref, wg_ref, gq_ref, gk_ref, gmat_ref,
                 cos_ref, sin_ref, zh_ref, q_ref, k_ref, v_ref, gate_ref):
    hx = _rms_mod(x_ref[0], n1_ref[...], sc_ref[0], sh_ref[0]).astype(BF16)
    zh_ref[0] = _dot(hx, wh_ref[...]).astype(BF16)
    gmat = gmat_ref[...]
    cos = cos_ref[...]
    sin = sin_ref[...]
    zq = _dot(hx, wq_ref[...])
    for s in range(ATTN_W // LANES):
        sl = slice(s * LANES, (s + 1) * LANES)
        q_ref[0, :, sl] = _head_norm_rope(zq[:, sl], gq_ref[:, sl], gmat, cos, sin, LOG2E * HEAD_DIM ** -0.5).astype(BF16)
    zkv = _dot(hx, wkv_ref[...])
    k_ref[0] = _head_norm_rope(zkv[:, :KV_W], gk_ref[...], gmat, cos, sin, 1.0).astype(BF16)
    v_ref[0] = zkv[:, KV_W:].astype(BF16)
    gate_ref[0] = jax.nn.sigmoid(_dot(hx, wg_ref[...])).astype(BF16)


def _inproj(x, mods3, norm1, wh, wq, wkv, wg, gq, gk, gmat, cos_t, sin_t):
    d = D_MODEL
    nt = SEQ // TM_IN
    const = lambda shape: pl.BlockSpec(shape, lambda b, i: (0,) * len(shape))
    tok = lambda w: pl.BlockSpec((1, TM_IN, w), lambda b, i: (b, i, 0))
    return pl.pallas_call(
        _inproj_body,
        grid=(BATCH, nt),
        in_specs=[tok(d),
                  pl.BlockSpec((1, 1, d), lambda b, i: (b, 0, 1)),
                  pl.BlockSpec((1, 1, d), lambda b, i: (b, 0, 0)),
                  const((1, d)), const((d, OFF_Q)), const((d, ATTN_W)), const((d, 2 * KV_W)), const((d, 2 * d)),
                  const((1, ATTN_W)), const((1, KV_W)), const((LANES, LANES)),
                  pl.BlockSpec((TM_IN, LANES), lambda b, i: (i, 0)),
                  pl.BlockSpec((TM_IN, LANES), lambda b, i: (i, 0))],
        out_specs=[tok(OFF_Q), tok(ATTN_W), tok(KV_W), tok(KV_W), tok(2 * d)],
        out_shape=[jax.ShapeDtypeStruct((BATCH, SEQ, OFF_Q), BF16),
                   jax.ShapeDtypeStruct((BATCH, SEQ, ATTN_W), BF16),
                   jax.ShapeDtypeStruct((BATCH, SEQ, KV_W), BF16),
                   jax.ShapeDtypeStruct((BATCH, SEQ, KV_W), BF16),
                   jax.ShapeDtypeStruct((BATCH, SEQ, 2 * d), BF16)],
        compiler_params=_params(("parallel", "parallel")),
        name="in_proj",
    )(x, mods3, mods3, norm1, wh, wq, wkv, wg, gq, gk, gmat, cos_t, sin_t)


def _ctx_body(c_ref, sc_ref, sh_ref, n1_ref, wkv_ref, gk_ref, gmat_ref, kc_ref, vc_ref):
    hc = _rms_mod(c_ref[0], n1_ref[...], sc_ref[0], sh_ref[0]).astype(BF16)
    z = _dot(hc, wkv_ref[...])
    kc_ref[0] = _head_norm_rope(z[:, :KV_W], gk_ref[...], gmat_ref[...], None, None, 1.0).astype(BF16)
    vc_ref[0] = z[:, KV_W:].astype(BF16)


def _ctx_proj(ctx, mods3, norm1, wkv, gk, gmat):
    d = D_MODEL
    const = lambda shape: pl.BlockSpec(shape, lambda b: (0,) * len(shape))
    return pl.pallas_call(
        _ctx_body,
        grid=(BATCH,),
        in_specs=[pl.BlockSpec((1, CTX_LEN, d), lambda b: (b, 0, 0)),
                  pl.BlockSpec((1, 1, d), lambda b: (BATCH, 0, 1)),
                  pl.BlockSpec((1, 1, d), lambda b: (BATCH, 0, 0)),
                  const((1, d)), const((d, 2 * KV_W)), const((1, KV_W)), const((LANES, LANES))],
        out_specs=[pl.BlockSpec((1, CTX_LEN, KV_W), lambda b: (b, 0, 0)),
                   pl.BlockSpec((1, CTX_LEN, KV_W), lambda b: (b, 0, 0))],
        out_shape=[jax.ShapeDtypeStruct((BATCH, CTX_LEN, KV_W), BF16),
                   jax.ShapeDtypeStruct((BATCH, CTX_LEN, KV_W), BF16)],
        compiler_params=_params(("parallel",)),
        name="ctx_proj",
    )(ctx, mods3, mods3, norm1, wkv, gk, gmat)


def _hyena_body(zv_ref, z1_ref, z2_ref, cwv_ref, cw1_ref, cw2_ref, cbv_ref, cb1_ref, cb2_ref, hb_ref, kf_ref,
                kn_ref, t1_ref, t2_ref, t1t_ref, t2t_ref, o_ref, lo_ref, hi_ref, glo_ref, ghi_ref, p_ref, m_ref,
                za_ref, zb_ref):
    row = lax.broadcasted_iota(I32, (HALF, CT), 0)
    rr = lax.broadcasted_iota(I32, (RB, RB), 0)
    cc = lax.broadcasted_iota(I32, (RB, RB), 1)
    flip = jnp.where(rr + cc == RB - 1, 1.0, 0.0).astype(BF16)
    nrb = HALF // RB

    def folded_short_conv(z_ref, w_ref, b_ref, lo_out, hi_out):
        zlo = z_ref[0, 0:HALF, :].astype(F32)
        for j in range(nrb):
            hi_out[j * RB:(j + 1) * RB, :] = _dot(flip, z_ref[0, SEQ - RB * (j + 1):SEQ - RB * j, :])
        zhi = hi_out[...]
        w0, w1, w2 = w_ref[0:1, :], w_ref[1:2, :], w_ref[2:3, :]
        first, last = row == 0, row == HALF - 1
        lo_prev = jnp.where(first, 0.0, pltpu.roll(zlo, 1, axis=0))
        lo_next = jnp.where(last, zhi[HALF - 1:HALF, :], pltpu.roll(zlo, HALF - 1, axis=0))
        hi_prev = jnp.where(first, 0.0, pltpu.roll(zhi, 1, axis=0))
        hi_next = jnp.where(last, zlo[HALF - 1:HALF, :], pltpu.roll(zhi, HALF - 1, axis=0))
        lo_out[...] = lo_prev * w0 + zlo * w1 + lo_next * w2 + b_ref[...]
        hi_out[...] = hi_next * w0 + zhi * w1 + hi_prev * w2 + b_ref[...]

    folded_short_conv(zv_ref, cwv_ref, cbv_ref, lo_ref, hi_ref)
    sign8 = _sign_rows(HALF)
    odd = (lax.broadcasted_iota(I32, (FC, CT), 0) & 1) == 1
    for o, (zr, cw, cb) in enumerate(((z1_ref, cw1_ref, cb1_ref), (z2_ref, cw2_ref, cb2_ref))):
        folded_short_conv(zr, cw, cb, glo_ref, ghi_ref)
        p_ref[...] = (lo_ref[...] + hi_ref[...]).astype(BF16)
        m_ref[...] = (lo_ref[...] - hi_ref[...]).astype(BF16)
        pv = p_ref[...]
        mv = m_ref[...]
        for c in range(HALF // FC):
            ev = slice(c * FC, (c + 1) * FC)
            od = slice(HALF + c * FC, HALF + (c + 1) * FC)
            xce = _dot(t1_ref[ev, :], pv)
            xso = _dot(t1_ref[od, :], pv)
            xco = _dot(t2_ref[ev, :], mv)
            xse = _dot(t2_ref[od, :], mv)
            kce, kse, kco, kso = kf_ref[o, 0, ev, :], kf_ref[o, 1, ev, :], kf_ref[o, 2, ev, :], kf_ref[o, 3, ev, :]
            za_ref[ev, :] = (xce * kce - xse * kse).astype(BF16)
            za_ref[od, :] = (xco * kso + xso * kco).astype(BF16)
            zb_ref[ev, :] = (xco * kco - xso * kso).astype(BF16)
            zb_ref[od, :] = (xce * kse + xse * kce).astype(BF16)
        zn = _dot(sign8, mv)[0:1] * kn_ref[o]
        bias = hb_ref[o:o + 1, :]
        za = za_ref[...]
        zb = zb_ref[...]
        for c in range(HALF // FC):
            rs = slice(c * FC, (c + 1) * FC)
            half_p = _dot(t1t_ref[rs, :], za)
            half_m = _dot(t2t_ref[rs, :], zb) + jnp.where(odd, -zn, zn)
            lo_ref[rs, :] = glo_ref[rs, :] * (half_p + half_m + bias * lo_ref[rs, :])
            hi_ref[rs, :] = ghi_ref[rs, :] * (half_p - half_m + bias * hi_ref[rs, :])
    o_ref[0, 0:HALF, :] = lo_ref[...].astype(BF16)
    for j in range(nrb):
        o_ref[0, SEQ - RB * (j + 1):SEQ - RB * j, :] = _dot(
            flip, hi_ref[j * RB:(j + 1) * RB, :].astype(BF16)).astype(BF16)


def _hyena(zh, conv_w, conv_b, hbias, kf, kn, t1, t2, t1t, t2t):
    nct = HYENA_W // CT
    zspec = lambda k: pl.BlockSpec((1, SEQ, CT), lambda c, b: (b, 0, k * nct + c))
    wspec = lambda k: pl.BlockSpec((3, CT), lambda c, b: (0, k * nct + c))
    bspec = lambda k: pl.BlockSpec((1, CT), lambda c, b: (0, k * nct + c))
    table = lambda shape: pl.BlockSpec(shape, lambda c, b: (0, 0), pipeline_mode=pl.Buffered(1))
    half_f32 = pltpu.VMEM((HALF, CT), F32)
    return pl.pallas_call(
        _hyena_body,
        grid=(nct, BATCH),
        in_specs=[zspec(0), zspec(1), zspec(2), wspec(0), wspec(1), wspec(2), bspec(0), bspec(1), bspec(2),
                  pl.BlockSpec((HYENA_ORDER, CT), lambda c, b: (0, c)),
                  pl.BlockSpec((HYENA_ORDER, 4, HALF, CT), lambda c, b: (0, 0, 0, c)),
                  pl.BlockSpec((HYENA_ORDER, 1, CT), lambda c, b: (0, 0, c)),
                  table((SEQ, HALF)), table((SEQ, HALF)), table((HALF, SEQ)), table((HALF, SEQ))],
        out_specs=pl.BlockSpec((1, SEQ, CT), lambda c, b: (b, 0, c)),
        out_shape=jax.ShapeDtypeStruct((BATCH, SEQ, HYENA_W), BF16),
        scratch_shapes=[half_f32, half_f32, half_f32, half_f32,
                        pltpu.VMEM((HALF, CT), BF16), pltpu.VMEM((HALF, CT), BF16),
                        pltpu.VMEM((SEQ, CT), BF16), pltpu.VMEM((SEQ, CT), BF16)],
        compiler_params=_params(("parallel", "parallel")),
        name="hyena_conv",
    )(zh, zh, zh, conv_w, conv_w, conv_w, conv_b, conv_b, conv_b, hbias, kf, kn, t1, t2, t1t, t2t)


def _attn_body(sink_ref, q_ref, k_ref, v_ref, kc_ref, vc_ref, bias_ref, o_ref, kp_ref, vlo_ref, vhi_ref):
    nb = SEQ // BLOCK
    lane = lax.broadcasted_iota(I32, (BLOCK, LANES), 1)
    low = lane < HEAD_DIM
    mask_lo = jnp.where(low, 1.0, 0.0).astype(BF16)
    mask_hi = jnp.where(low, 0.0, 1.0).astype(BF16)

    def with_ones(v):
        lane_v = lax.broadcasted_iota(I32, v.shape, 1) < HEAD_DIM
        one = jnp.ones_like(v)
        return jnp.where(lane_v, v, one), jnp.where(lane_v, one, v)

    zpad = jnp.zeros((BLOCK, KV_W), BF16)
    kp_ref[0:BLOCK] = zpad
    kp_ref[BLOCK:BLOCK + SEQ] = k_ref[0]
    kp_ref[BLOCK + SEQ:] = zpad
    v_lo, v_hi = with_ones(v_ref[0])
    for ref, val in ((vlo_ref, v_lo), (vhi_ref, v_hi)):
        ref[0:BLOCK] = zpad
        ref[BLOCK:BLOCK + SEQ] = val
        ref[BLOCK + SEQ:] = zpad
    kc = kc_ref[0]
    vc_pair = with_ones(vc_ref[0])

    def block(n, carry):
        r = pl.multiple_of(n * BLOCK, BLOCK)
        kw = kp_ref[pl.ds(r, 3 * BLOCK), :]
        vw_pair = (vlo_ref[pl.ds(r, 3 * BLOCK), :], vhi_ref[pl.ds(r, 3 * BLOCK), :])
        bias = bias_ref[jnp.where(n == 0, 0, jnp.where(n == nb - 1, 2, 1))]
        scores = []
        for j in range(GROUP):
            qs = q_ref[0, pl.ds(r, BLOCK), j * LANES:(j + 1) * LANES]
            for msk in (mask_lo, mask_hi):
                qm = qs * msk
                scores.append((_dot_nt(qm, kw) + bias, _dot_nt(qm, kc)))
        probs = []
        for idx, (sw, sc) in enumerate(scores):
            snk = sink_ref[idx // 2 + GROUP * (idx % 2)]
            m = jnp.maximum(jnp.maximum(jnp.max(sw, axis=-1, keepdims=True),
                                        jnp.max(sc, axis=-1, keepdims=True)), snk)
            probs.append((jnp.exp2(sw - m).astype(BF16), jnp.exp2(sc - m).astype(BF16), jnp.exp2(snk - m)))
        outs = []
        for idx, (pw, pc, psink) in enumerate(probs):
            acc = _dot(pw, vw_pair[idx % 2]) + _dot(pc, vc_pair[idx % 2])
            den = pltpu.roll(acc, HEAD_DIM, axis=1) + psink
            outs.append(acc / den)
        for j in range(GROUP):
            o_ref[0, pl.ds(r, BLOCK), j * LANES:(j + 1) * LANES] = jnp.where(
                low, outs[2 * j], outs[2 * j + 1]).astype(BF16)
        return carry

    lax.fori_loop(0, nb, block, 0)


def _attention(sink, q, k, v, kc, vc, bias):
    per_b = lambda n, w: pl.BlockSpec((1, n, w), lambda b: (b, 0, 0))
    return pl.pallas_call(
        _attn_body,
        grid=(BATCH,),
        in_specs=[pl.BlockSpec(memory_space=pltpu.SMEM),
                  per_b(SEQ, ATTN_W), per_b(SEQ, KV_W), per_b(SEQ, KV_W), per_b(CTX_LEN, KV_W), per_b(CTX_LEN, KV_W),
                  pl.BlockSpec((3, BLOCK, 3 * BLOCK), lambda b: (0, 0, 0))],
        out_specs=per_b(SEQ, ATTN_W),
        out_shape=jax.ShapeDtypeStruct((BATCH, SEQ, ATTN_W), BF16),
        scratch_shapes=[pltpu.VMEM((SEQ + 2 * BLOCK, KV_W), BF16)] * 3,
        compiler_params=_params(("parallel",)),
        name="window_attn",
    )(sink, q, k, v, kc, vc, bias)


def _merge_body(ya_ref, yb_ref, gt_ref, x_ref, g1_ref, sc2_ref, sh2_ref, n2_ref, wa_ref, wb_ref, wo_ref, rh_ref,
                rl_ref, xn_ref, h2_ref, aff_ref):
    d = D_MODEL
    groups = [slice(i * SUB_MERGE, (i + 1) * SUB_MERGE) for i in range(TM_MERGE // SUB_MERGE)]
    branch = [(_dot(ya_ref[0, s, :], wa_ref[...]), _dot(yb_ref[0, s, :], wb_ref[...])) for s in groups]
    mixed = [(gt_ref[0, s, :d].astype(F32) * ua + gt_ref[0, s, d:].astype(F32) * ub).astype(BF16)
             for s, (ua, ub) in zip(groups, branch)]
    proj = [_dot(u, wo_ref[...]) for u in mixed]
    h2_parts = []
    for s, m in zip(groups, proj):
        xn = x_ref[0, s, :] + g1_ref[0] * m
        xn_ref[0, s, :] = xn
        h2 = _rms_mod(xn, n2_ref[...], sc2_ref[0], sh2_ref[0])
        h2_hi = h2.astype(BF16)
        h2_ref[0, s, :] = h2_hi
        h2_parts.append((h2_hi, (h2 - h2_hi.astype(F32)).astype(BF16)))
    rh = rh_ref[...]
    logits = [_dot_nt(rh, hi) + _dot_nt(rh, lo) + _dot_nt(rl_ref[...], hi) for hi, lo in h2_parts]
    for s, lt in zip(groups, logits):
        e = jnp.exp(lt - jnp.max(lt, axis=0, keepdims=True))
        aff_ref[0, :, s] = e / jnp.sum(e, axis=0, keepdims=True)


def _merge(ya, yb, gates, x, mods3, norm2, wa, wb, wo, router_hi, router_lo):
    d = D_MODEL
    nt = SEQ // TM_MERGE
    const = lambda shape: pl.BlockSpec(shape, lambda b, i: (0,) * len(shape))
    tok = lambda w: pl.BlockSpec((1, TM_MERGE, w), lambda b, i: (b, i, 0))
    mod = lambda k: pl.BlockSpec((1, 1, d), lambda b, i: (b, 0, k))
    return pl.pallas_call(
        _merge_body,
        grid=(BATCH, nt),
        in_specs=[tok(HYENA_W), tok(ATTN_W), tok(2 * d), tok(d), mod(2), mod(4), mod(3), const((1, d)),
                  const((HYENA_W, d)), const((ATTN_W, d)), const((d, d)), const((N_EXPERTS, d)), const((N_EXPERTS, d))],
        out_specs=[tok(d), tok(d), pl.BlockSpec((1, N_EXPERTS, TM_MERGE), lambda b, i: (b, 0, i))],
        out_shape=[jax.ShapeDtypeStruct((BATCH, SEQ, d), F32),
                   jax.ShapeDtypeStruct((BATCH, SEQ, d), BF16),
                   jax.ShapeDtypeStruct((BATCH, N_EXPERTS, SEQ), F32)],
        compiler_params=_params(("parallel", "parallel")),
        name="merge_outproj",
    )(ya, yb, gates, x, mods3, mods3, mods3, norm2, wa, wb, wo, router_hi, router_lo)


def _thresh_body(aff_ref, lo_ref, hi_ref):
    aff = aff_ref[...]
    rows = aff.shape[0]
    bits = pltpu.bitcast(aff, I32)

    def bisect_bits(_, carry):
        lo, hi = carry
        mid = lo + ((hi - lo) >> 1)
        ge = jnp.sum((bits >= mid).astype(I32), axis=1, keepdims=True) >= CAP
        return jnp.where(ge, mid, lo), jnp.where(ge, hi, mid)

    lo0 = jnp.zeros((rows, 1), I32)
    hi0 = jnp.full((rows, 1), 0x3F800001, I32)
    thr_bits, _ = lax.fori_loop(0, 31, bisect_bits, (lo0, hi0))
    thr = pltpu.bitcast(thr_bits, F32)

    def bisect_val(_, carry):
        lo, hi = carry
        mid = 0.5 * (lo + hi)
        ge = jnp.sum(jnp.where(aff >= mid, 1.0, 0.0), axis=1, keepdims=True) >= CAP
        return jnp.where(ge, mid, lo), jnp.where(ge, hi, mid)

    lo, hi = lax.fori_loop(0, 30, bisect_val, (0.5 * thr, jnp.maximum(2.0 * thr, 1e-30)))
    lo_ref[...] = jnp.broadcast_to(lo, lo_ref.shape)
    hi_ref[...] = jnp.broadcast_to(hi, hi_ref.shape)


def _thresholds(aff_rows):
    rows = BATCH * N_EXPERTS
    return pl.pallas_call(
        _thresh_body,
        grid=(1,),
        in_specs=[pl.BlockSpec((rows, SEQ), lambda i: (0, 0))],
        out_specs=[pl.BlockSpec((rows, LANES), lambda i: (0, 0)), pl.BlockSpec((rows, LANES), lambda i: (0, 0))],
        out_shape=[jax.ShapeDtypeStruct((rows, LANES), F32), jax.ShapeDtypeStruct((rows, LANES), F32)],
        compiler_params=_params(("arbitrary",)),
        name="route_threshold",
    )(aff_rows)


def _prefix_counts(mask):
    r = lax.broadcasted_iota(I32, (LANES, LANES), 0)
    c = lax.broadcasted_iota(I32, (LANES, LANES), 1)
    upper = jnp.where(r <= c, 1.0, 0.0).astype(BF16)
    offset = jnp.zeros((mask.shape[0], 1), F32)
    blocks = []
    for j in range(mask.shape[1] // LANES):
        blk = mask[:, j * LANES:(j + 1) * LANES]
        inc = _dot(blk.astype(BF16), upper)
        blocks.append(inc - blk + offset)
        offset = offset + inc[:, LANES - 1:LANES]
    return jnp.concatenate(blocks, axis=1)


def _route_body(aff_ref, lo_ref, hi_ref, h_ref, xin_ref, g_ref, pos_ref, pbuf_ref):
    aff = aff_ref[0]
    above = jnp.where(aff >= hi_ref[:, 0:1], 1.0, 0.0)
    band = jnp.where(aff >= lo_ref[:, 0:1], 1.0, 0.0) - above
    need = CAP - jnp.sum(above, axis=1, keepdims=True)
    tie_rank = _prefix_counts(band)
    self = above + band * jnp.where(tie_rank < need, 1.0, 0.0)
    pos = _prefix_counts(self)
    posi = jnp.where(self > 0.5, pos.astype(I32), -1)
    pos_ref[0] = posi

    h = h_ref[0]
    slot = lax.broadcasted_iota(I32, (CAP, SEQ), 0)
    for grp in range(N_EXPERTS // EXPERT_GROUP):
        for i in range(EXPERT_GROUP):
            e = grp * EXPERT_GROUP + i
            hit = posi[e:e + 1, :] == slot
            pbuf_ref[i * CAP:(i + 1) * CAP, :] = jnp.where(hit, 1.0, 0.0).astype(BF16)
            g_ref[e] = jnp.sum(jnp.where(hit, aff[e:e + 1, :], 0.0), axis=1, keepdims=True)
        rows = _dot(pbuf_ref[...], h)
        for i in range(EXPERT_GROUP):
            xin_ref[grp * EXPERT_GROUP + i] = rows[i * CAP:(i + 1) * CAP].astype(BF16)


def _route(aff_t, lo, hi, h2):
    d = D_MODEL
    return pl.pallas_call(
        _route_body,
        grid=(BATCH,),
        in_specs=[pl.BlockSpec((1, N_EXPERTS, SEQ), lambda b: (b, 0, 0)),
                  pl.BlockSpec((N_EXPERTS, LANES), lambda b: (b, 0)),
                  pl.BlockSpec((N_EXPERTS, LANES), lambda b: (b, 0)),
                  pl.BlockSpec((1, SEQ, d), lambda b: (b, 0, 0))],
        out_specs=[pl.BlockSpec((N_EXPERTS, CAP, d), lambda b: (0, b, 0)),
                   pl.BlockSpec((N_EXPERTS, CAP, 1), lambda b: (0, b, 0)),
                   pl.BlockSpec((1, N_EXPERTS, SEQ), lambda b: (b, 0, 0))],
        out_shape=[jax.ShapeDtypeStruct((N_EXPERTS, BATCH * CAP, d), BF16),
                   jax.ShapeDtypeStruct((N_EXPERTS, BATCH * CAP, 1), F32),
                   jax.ShapeDtypeStruct((BATCH, N_EXPERTS, SEQ), I32)],
        scratch_shapes=[pltpu.VMEM((EXPERT_GROUP * CAP, SEQ), BF16)],
        compiler_params=_params(("parallel",)),
        name="route_gather",
    )(aff_t, lo, hi, h2)


def _expert_body(x_ref, g_ref, wg_ref, wu_ref, wd_ref, o_ref, acc_ref, wgb_ref, wub_ref, wdb_ref):
    f = pl.program_id(1)
    wgb_ref[...] = wg_ref[0].astype(BF16)
    wub_ref[...] = wu_ref[0].astype(BF16)
    wdb_ref[...] = wd_ref[0].astype(BF16)

    @pl.when(f == 0)
    def _():
        acc_ref[...] = jnp.zeros_like(acc_ref)

    for m in range(BATCH * CAP // MC):
        rows = slice(m * MC, (m + 1) * MC)
        xm = x_ref[0, rows, :]
        a = _dot(xm, wgb_ref[...])
        b = _dot(xm, wub_ref[...])
        hh = (a * jax.nn.sigmoid(a) * b).astype(BF16)
        acc_ref[rows, :] += _dot(hh, wdb_ref[...])

    @pl.when(f == D_FF // TF - 1)
    def _():
        o_ref[0] = (acc_ref[...] * g_ref[0]).astype(BF16)


def _experts(xin, g, w_gate, w_up, w_down):
    d = D_MODEL
    rows = BATCH * CAP
    return pl.pallas_call(
        _expert_body,
        grid=(N_EXPERTS, D_FF // TF),
        in_specs=[pl.BlockSpec((1, rows, d), lambda e, f: (e, 0, 0)),
                  pl.BlockSpec((1, rows, 1), lambda e, f: (e, 0, 0)),
                  pl.BlockSpec((1, d, TF), lambda e, f: (e, 0, f)),
                  pl.BlockSpec((1, d, TF), lambda e, f: (e, 0, f)),
                  pl.BlockSpec((1, TF, d), lambda e, f: (e, f, 0))],
        out_specs=pl.BlockSpec((1, rows, d), lambda e, f: (e, 0, 0)),
        out_shape=jax.ShapeDtypeStruct((N_EXPERTS, rows, d), BF16),
        scratch_shapes=[pltpu.VMEM((rows, d), F32), pltpu.VMEM((d, TF), BF16), pltpu.VMEM((d, TF), BF16),
                        pltpu.VMEM((TF, d), BF16)],
        compiler_params=_params(("parallel", "arbitrary")),
        name="swiglu_experts",
    )(xin, g, w_gate, w_up, w_down)


def _scatter_body(pos_ref, y_ref, x_ref, g2_ref, o_ref):
    pos = pos_ref[0]
    slot = lax.broadcasted_iota(I32, (TS, CAP), 1)
    onehot = jnp.concatenate([jnp.where(pos[:, e:e + 1] == slot, 1.0, 0.0).astype(BF16) for e in range(N_EXPERTS)],
                             axis=1)
    y = y_ref[...].reshape(N_EXPERTS * CAP, D_MODEL)
    o_ref[0] = x_ref[0] + g2_ref[0] * _dot(onehot, y)


def _scatter(pos_tok, y, xn, mods3):
    d = D_MODEL
    return pl.pallas_call(
        _scatter_body,
        grid=(BATCH, SEQ // TS),
        in_specs=[pl.BlockSpec((1, TS, N_EXPERTS), lambda b, i: (b, i, 0)),
                  pl.BlockSpec((N_EXPERTS, CAP, d), lambda b, i: (0, b, 0)),
                  pl.BlockSpec((1, TS, d), lambda b, i: (b, i, 0)),
                  pl.BlockSpec((1, 1, d), lambda b, i: (b, 0, 5))],
        out_specs=pl.BlockSpec((1, TS, d), lambda b, i: (b, i, 0)),
        out_shape=jax.ShapeDtypeStruct((BATCH, SEQ, d), F32),
        compiler_params=_params(("parallel", "parallel")),
        name="scatter_residual",
    )(pos_tok, y, xn, mods3)


def _rope_tables():
    rows = SEQ // GRID_W
    row = np.repeat(np.arange(rows, dtype=np.float32), GRID_W)
    col = np.tile(np.arange(GRID_W, dtype=np.float32), rows)
    inv = (ROPE_BASE ** (-np.arange(0, AXIS_ROT, 2, dtype=np.float32) / AXIS_ROT)).astype(np.float32)
    ang = np.concatenate([row[:, None] * inv, col[:, None] * inv], axis=-1).astype(np.float64)
    cos = np.repeat(np.cos(ang), 2, axis=-1)
    sin = np.stack([-np.sin(ang), np.sin(ang)], axis=-1).reshape(SEQ, HEAD_DIM)
    reps = LANES // HEAD_DIM
    return jnp.asarray(np.tile(cos, (1, reps)), F32), jnp.asarray(np.tile(sin, (1, reps)), F32)


def _cos_sin_outer(f, mults):
    unit = math.pi / N_FFT
    step = 32
    coarse = 2 * step * jnp.arange(mults.shape[0] // step, dtype=I32)
    fine = mults[:step]
    a = ((f[:, None] * coarse[None, :]) % (2 * N_FFT)).astype(F32) * unit
    b = ((f[:, None] * fine[None, :]) % (2 * N_FFT)).astype(F32) * unit
    ca, sa, cb, sb = jnp.cos(a)[:, :, None], jnp.sin(a)[:, :, None], jnp.cos(b)[:, None, :], jnp.sin(b)[:, None, :]
    n = f.shape[0]
    return (ca * cb - sa * sb).reshape(n, -1), (sa * cb + ca * sb).reshape(n, -1)


def _dft_tables():
    idx = jnp.arange(HALF, dtype=I32)
    t2p1 = 2 * idx + 1
    ce, se = _cos_sin_outer(2 * idx, t2p1)
    co, so = _cos_sin_outer(2 * idx + 1, t2p1)
    t1 = jnp.concatenate([ce, so], axis=0).astype(BF16)
    t2 = jnp.concatenate([co, se], axis=0).astype(BF16)
    return t1, t2, t1.T, t2.T


def _phase_tables():
    idx = np.arange(HALF, dtype=np.float64)
    w = np.full((HALF,), 2.0 / N_FFT)
    we = w.copy()
    we[0] = 1.0 / N_FFT
    pe = (math.pi / N_FFT) * (2.0 * idx)
    po = (math.pi / N_FFT) * (2.0 * idx + 1.0)
    rot = np.stack([we * np.cos(pe), we * np.sin(pe), w * np.cos(po), w * np.sin(po)])
    return jnp.asarray(np.broadcast_to(rot[:, :, None], (4, HALF, CT)), F32)


def _fold_rows(a):
    return np.concatenate([a[:HALF], a[HALF:][::-1]], axis=0)


def _filter_features():
    t = np.linspace(0.0, 1.0, SEQ, dtype=np.float32).astype(np.float64)[:, None]
    w = 2.0 * math.pi * np.arange(SEQ, dtype=np.float64)[:, None] / SEQ
    fr = np.linspace(1e-4, FILTER_BANDS - 1, FILTER_BANDS, dtype=np.float32).astype(np.float64)[None, :]
    feat = np.concatenate([t, np.cos(fr * w), -np.sin(fr * w)], axis=-1)
    feat = np.pad(feat, ((0, 0), (0, FILTER_HIDDEN - FILTER_EMB)))
    min_decay = math.log(DECAY_TARGET) / SLOW_DECAY_PCT
    max_decay = math.log(DECAY_TARGET) / FAST_DECAY_PCT
    deltas = np.linspace(min_decay, max_decay, HYENA_W, dtype=np.float32).astype(np.float64)
    decay = np.exp(-t * np.abs(deltas))
    return jnp.asarray(_fold_rows(feat).T, F32), jnp.asarray(_fold_rows(decay), F32)


def _attn_bias():
    qi = np.arange(BLOCK)[:, None]
    kj = np.arange(3 * BLOCK)[None, :]
    band = np.abs(kj - BLOCK - qi) <= WINDOW
    first = band & (kj >= BLOCK)
    last = band & (kj < 2 * BLOCK)
    return jnp.asarray(np.where(np.stack([first, band, last]), 0.0, NEG), F32)


def _pair_heads(w, axis):
    heads = [lax.slice_in_dim(w, h * HEAD_DIM, (h + 1) * HEAD_DIM, axis=axis) for h in range(N_HEADS)]
    return jnp.concatenate([heads[j + GROUP * half] for j in range(GROUP) for half in range(N_KV_HEADS)], axis=axis)


def kernel(x, c, ctx, c_ctx, ada_w, ada_b, norm1, norm2, w_in, conv_w, conv_b, filt_w1, filt_b1, filt_w2, filt_b2,
           filt_w3, filt_b3, filt_freq, filt_out, hyena_bias, q_norm, k_norm, attn_sink, w_branch_a, w_branch_b,
           w_out, router, w_gate, w_up, w_down):
    d = D_MODEL
    assert ada_w.shape[0] == 1, "only the single-layer configuration is implemented"
    l = 0
    cos_t, sin_t = _rope_tables()
    t1, t2, t1t, t2t = _dft_tables()
    rot = _phase_tables()
    feat, decay = _filter_features()
    bias = _attn_bias()
    gmat = jnp.asarray(np.kron(np.eye(LANES // HEAD_DIM), np.full((HEAD_DIM, HEAD_DIM), 1.0 / HEAD_DIM)), BF16)
    c16 = jnp.zeros((MOD_ROWS, d), F32).at[:BATCH].set(c).at[BATCH].set(c_ctx)

    mods3 = _ada(c16, ada_w[l], ada_b[l][None, :]).reshape(MOD_ROWS, 1, 6 * d)
    n1 = norm1[l][None, :]
    w = w_in[l]
    wkv = w[:, OFF_K:OFF_G].astype(BF16)
    gk = jnp.tile(k_norm[l], N_KV_HEADS)[None, :]
    kf, kn = _filters(feat, jnp.pad(filt_w1[l].T, ((0, 0), (0, FILTER_HIDDEN - FILTER_EMB))), filt_b1[l][:, None],
                      filt_w2[l].T, filt_b2[l][:, None], filt_w3[l].T, filt_b3[l][:, None], filt_freq[l][:, None],
                      filt_out[l], decay, rot, t1, t2)
    kc, vc = _ctx_proj(ctx, mods3, n1, wkv, gk, gmat)
    zh, q, k, v, gates = _inproj(
        x, mods3, n1, w[:, :OFF_Q].astype(BF16), _pair_heads(w[:, OFF_Q:OFF_K], 1).astype(BF16), wkv,
        w[:, OFF_G:].astype(BF16), jnp.tile(q_norm[l], N_HEADS)[None, :], gk, gmat, cos_t, sin_t)
    ya = _hyena(zh, conv_w[l], conv_b[l][None, :], hyena_bias[l], kf, kn, t1, t2, t1t, t2t)
    yb = _attention(attn_sink[l] * LOG2E, q, k, v, kc, vc, bias)
    router_t = router[l].T
    router_hi = router_t.astype(BF16)
    router_lo = (router_t - router_hi.astype(F32)).astype(BF16)
    xn, h2, aff_t = _merge(ya, yb, gates, x, mods3, norm2[l][None, :], w_branch_a[l].astype(BF16),
                           _pair_heads(w_branch_b[l], 0).astype(BF16), w_out[l].astype(BF16), router_hi, router_lo)
    lo, hi = _thresholds(aff_t.reshape(BATCH * N_EXPERTS, SEQ))
    xin, g, pos = _route(aff_t, lo, hi, h2)
    y = _experts(xin, g, w_gate[l], w_up[l], w_down[l])
    return _scatter(jnp.swapaxes(pos, 1, 2), y, xn, mods3)
```

```python
import math

import numpy as np
import jax
import jax.numpy as jnp
from jax import lax
from jax.experimental import pallas as pl
from jax.experimental.pallas import tpu as pltpu

F32 = jnp.float32
BF16 = jnp.bfloat16
I32 = jnp.int32
HIGHEST = lax.Precision.HIGHEST

D_MODEL = 1024
BATCH = 8
SEQ = 2048
GRID_W = 64
CTX_LEN = 256
N_HEADS = 8
N_KV_HEADS = 2
HEAD_DIM = 64
GROUP = N_HEADS // N_KV_HEADS
ATTN_W = N_HEADS * HEAD_DIM
KV_W = N_KV_HEADS * HEAD_DIM
WINDOW = 128
BLOCK = 128
HYENA_W = D_MODEL // 2
HYENA_ORDER = 2
FILTER_BANDS = 16
FILTER_EMB = 1 + 2 * FILTER_BANDS
FILTER_HIDDEN = 64
DECAY_TARGET = 1e-2
FAST_DECAY_PCT = 0.3
SLOW_DECAY_PCT = 1.5
ROPE_BASE = 10000.0
AXIS_ROT = HEAD_DIM // 2
N_EXPERTS = 16
EC_CAPACITY = 2
D_FF = 2048
EPS = 1e-6
NEG = -1e30
LOG2E = math.log2(math.e)

OFF_Q = 3 * HYENA_W
OFF_K = OFF_Q + ATTN_W
OFF_V = OFF_K + KV_W
OFF_G = OFF_V + KV_W
IN_W = OFF_G + 2 * D_MODEL

CAP = EC_CAPACITY * SEQ // N_EXPERTS
N_FFT = 2 * SEQ
HALF = SEQ // 2
MOD_ROWS = 16
LANES = 128

TM_IN = 1024
SUB_IN = 512
TM_MERGE = 1024
SUB_MERGE = 512
CT = 256
FC = 512
RB = 256
TF = 512
MC = 512
TS = 512
EXPERT_GROUP = 4
VMEM_LIMIT = 56 * 1024 * 1024


def _dot(a, b, precision=None):
    return jnp.dot(a, b, preferred_element_type=F32, precision=precision)


def _dot_nt(a, b, precision=None):
    return lax.dot_general(a, b, (((1,), (1,)), ((), ())), preferred_element_type=F32, precision=precision)


def _params(sem, vmem=VMEM_LIMIT):
    return pltpu.CompilerParams(dimension_semantics=sem, vmem_limit_bytes=vmem)


def _rms_mod(x, g, sc, sh):
    ms = jnp.mean(x * x, axis=-1, keepdims=True)
    return (x * lax.rsqrt(ms + EPS) * g) * (1.0 + sc) + sh


def _head_norm_rope(z, g, gmat, cos, sin, scale):
    ms = _dot((z * z).astype(BF16), gmat)
    y = z * lax.rsqrt(ms + EPS) * g
    if cos is not None:
        slabs = []
        for s in range(z.shape[1] // LANES):
            ys = y[:, s * LANES:(s + 1) * LANES]
            lane = lax.broadcasted_iota(I32, ys.shape, 1)
            nxt = pltpu.roll(ys, LANES - 1, axis=1)
            prv = pltpu.roll(ys, 1, axis=1)
            slabs.append(ys * cos + jnp.where((lane & 1) == 0, nxt, prv) * sin)
        y = slabs[0] if len(slabs) == 1 else jnp.concatenate(slabs, axis=1)
    return y * scale


def _split_bf16(x):
    hi = x.astype(BF16)
    return hi, (x - hi.astype(F32)).astype(BF16)


def _ada_body(c_ref, w_ref, b_ref, o_ref):
    c = c_ref[...]
    s_hi, s_lo = _split_bf16(c * jax.nn.sigmoid(c))
    w_hi, w_lo = _split_bf16(w_ref[...])
    o_ref[...] = _dot(s_hi, w_hi) + _dot(s_lo, w_hi) + _dot(s_hi, w_lo) + b_ref[...]


def _ada(c16, w, b):
    d = D_MODEL
    return pl.pallas_call(
        _ada_body,
        grid=(6,),
        in_specs=[pl.BlockSpec((MOD_ROWS, d), lambda j: (0, 0)),
                  pl.BlockSpec((d, d), lambda j: (0, j)),
                  pl.BlockSpec((1, d), lambda j: (0, j))],
        out_specs=pl.BlockSpec((MOD_ROWS, d), lambda j: (0, j)),
        out_shape=jax.ShapeDtypeStruct((MOD_ROWS, 6 * d), F32),
        compiler_params=_params(("parallel",)),
        name="ada_mod",
    )(c16, w, b)


def _sign_rows(n):
    lane = lax.broadcasted_iota(I32, (8, n), 1)
    sub = lax.broadcasted_iota(I32, (8, n), 0)
    sg = jnp.where((lane & 1) == 0, 1.0, -1.0)
    return jnp.where(sub == 0, sg, 0.0).astype(BF16)


def _filt_body(feat_ref, w1_ref, b1_ref, w2_ref, b2_ref, w3_ref, b3_ref, fq_ref, fof_ref, fob_ref, dec_ref,
               rot_ref, t1_ref, t2_ref, kf_ref, kn_ref, hh_ref, hl_ref):
    @pl.when((pl.program_id(0) == 0) & (pl.program_id(1) == 0))
    def _():
        fq = fq_ref[...]
        h = jnp.sin(fq * (_dot(w1_ref[...], feat_ref[...], HIGHEST) + b1_ref[...]))
        h = jnp.sin(fq * (_dot(w2_ref[...], h, HIGHEST) + b2_ref[...]))
        h = jnp.sin(fq * (_dot(w3_ref[...], h, HIGHEST) + b3_ref[...]))
        hh_ref[...], hl_ref[...] = _split_bf16(h.T)

    def taps(fo_ref):
        f_hi, f_lo = _split_bf16(fo_ref[...])
        return _dot(hh_ref[...], f_hi) + _dot(hl_ref[...], f_hi) + _dot(hh_ref[...], f_lo)

    dec = dec_ref[...]
    hf = taps(fof_ref) * dec
    hb = taps(fob_ref) * dec
    row = lax.broadcasted_iota(I32, hf.shape, 0)
    hb = jnp.where(row == 0, 0.0, hb)
    a = hf + hb
    b = hf - hb
    pa = (a[:HALF] + a[HALF:]).astype(BF16)
    ma = (a[:HALF] - a[HALF:]).astype(BF16)
    pb = (b[:HALF] + b[HALF:]).astype(BF16)
    mb = (b[:HALF] - b[HALF:]).astype(BF16)
    t1 = t1_ref[...]
    t2 = t2_ref[...]
    a1 = _dot(t1, pa)
    a2 = _dot(t2, ma)
    b1 = _dot(t1, pb)
    b2 = _dot(t2, mb)
    ce, se, co, so = rot_ref[0], rot_ref[1], rot_ref[2], rot_ref[3]
    kf_ref[0, 0] = a1[:HALF] * ce + a2[HALF:] * se
    kf_ref[0, 1] = b2[HALF:] * ce - b1[:HALF] * se
    kf_ref[0, 2] = a2[:HALF] * co + a1[HALF:] * so
    kf_ref[0, 3] = b1[HALF:] * co - b2[:HALF] * so
    kn_ref[0] = _dot(_sign_rows(HALF), ma)[0:1] * (1.0 / N_FFT)


def _filters(feat, w1, b1, w2, b2, w3, b3, fq, fout, decay, rot, t1, t2):
    nct = HYENA_W // CT
    full = lambda shape: pl.BlockSpec(shape, lambda o, c: (0,) * len(shape))
    return pl.pallas_call(
        _filt_body,
        grid=(HYENA_ORDER, nct),
        in_specs=[full((FILTER_HIDDEN, SEQ)), full((FILTER_HIDDEN, FILTER_HIDDEN)), full((FILTER_HIDDEN, 1)),
                  full((FILTER_HIDDEN, FILTER_HIDDEN)), full((FILTER_HIDDEN, 1)),
                  full((FILTER_HIDDEN, FILTER_HIDDEN)), full((FILTER_HIDDEN, 1)), full((FILTER_HIDDEN, 1)),
                  pl.BlockSpec((FILTER_HIDDEN, CT), lambda o, c: (0, (o * 2 + 0) * nct + c)),
                  pl.BlockSpec((FILTER_HIDDEN, CT), lambda o, c: (0, (o * 2 + 1) * nct + c)),
                  pl.BlockSpec((SEQ, CT), lambda o, c: (0, c)),
                  full((4, HALF, CT)),
                  pl.BlockSpec((SEQ, HALF), lambda o, c: (0, 0), pipeline_mode=pl.Buffered(1)),
                  pl.BlockSpec((SEQ, HALF), lambda o, c: (0, 0), pipeline_mode=pl.Buffered(1))],
        out_specs=[pl.BlockSpec((1, 4, HALF, CT), lambda o, c: (o, 0, 0, c)),
                   pl.BlockSpec((1, 1, CT), lambda o, c: (o, 0, c))],
        out_shape=[jax.ShapeDtypeStruct((HYENA_ORDER, 4, HALF, HYENA_W), F32),
                   jax.ShapeDtypeStruct((HYENA_ORDER, 1, HYENA_W), F32)],
        scratch_shapes=[pltpu.VMEM((SEQ, FILTER_HIDDEN), BF16)] * 2,
        compiler_params=_params(("arbitrary", "arbitrary")),
        name="hyena_filters",
    )(feat, w1, b1, w2, b2, w3, b3, fq, fout, fout, decay, rot, t1, t2)


def _inproj_body(x_ref, sc_ref, sh_ref, n1_ref, wh_ref, wq_ref, wkv_ref, wg_ref, gq_ref, gk_ref, gmat_ref,
                 cos_ref, sin_ref, zh_ref, q_ref, k_ref, v_ref, gate_ref):
    groups = [slice(i * SUB_IN, (i + 1) * SUB_IN) for i in range(TM_IN // SUB_IN)]
    hx = [_rms_mod(x_ref[0, s, :], n1_ref[...], sc_ref[0], sh_ref[0]).astype(BF16) for s in groups]
    for s, h in zip(groups, hx):
        zh_ref[0, s, :] = _dot(h, wh_ref[...]).astype(BF16)
    pair = 2 * LANES
    zq = [_dot(h, wq_ref[...]) for h in hx]
    for s, z in zip(groups, zq):
        for c in range(ATTN_W // pair):
            sl = slice(c * pair, (c + 1) * pair)
            q_ref[0, s, sl] = _head_norm_rope(z[:, sl], gq_ref[:, sl], gmat_ref[...], cos_ref[s, :], sin_ref[s, :],
                                              LOG2E * HEAD_DIM ** -0.5).astype(BF16)
    zkv = [_dot(h, wkv_ref[...]) for h in hx]
    for s, z in zip(groups, zkv):
        k_ref[0, s, :] = _head_norm_rope(z[:, :KV_W], gk_ref[...], gmat_ref[0:KV_W, 0:KV_W], cos_ref[s, :],
                                         sin_ref[s, :], 1.0).astype(BF16)
        v_ref[0, s, :] = z[:, KV_W:].astype(BF16)
    for s, h in zip(groups, hx):
        gate_ref[0, s, :] = jax.nn.sigmoid(_dot(h, wg_ref[...])).astype(BF16)


def _inproj(x, mods3, norm1, wh, wq, wkv, wg, gq, gk, gmat, cos_t, sin_t):
    d = D_MODEL
    nt = SEQ // TM_IN
    const = lambda shape: pl.BlockSpec(shape, lambda b, i: (0,) * len(shape))
    tok = lambda w: pl.BlockSpec((1, TM_IN, w), lambda b, i: (b, i, 0))
    return pl.pallas_call(
        _inproj_body,
        grid=(BATCH, nt),
        in_specs=[tok(d),
                  pl.BlockSpec((1, 1, d), lambda b, i: (b, 0, 1)),
                  pl.BlockSpec((1, 1, d), lambda b, i: (b, 0, 0)),
                  const((1, d)), const((d, OFF_Q)), const((d, ATTN_W)), const((d, 2 * KV_W)), const((d, 2 * d)),
                  const((1, ATTN_W)), const((1, KV_W)), const((2 * LANES, 2 * LANES)),
                  pl.BlockSpec((TM_IN, LANES), lambda b, i: (i, 0)),
                  pl.BlockSpec((TM_IN, LANES), lambda b, i: (i, 0))],
        out_specs=[tok(OFF_Q), tok(ATTN_W), tok(KV_W), tok(KV_W), tok(2 * d)],
        out_shape=[jax.ShapeDtypeStruct((BATCH, SEQ, OFF_Q), BF16),
                   jax.ShapeDtypeStruct((BATCH, SEQ, ATTN_W), BF16),
                   jax.ShapeDtypeStruct((BATCH, SEQ, KV_W), BF16),
                   jax.ShapeDtypeStruct((BATCH, SEQ, KV_W), BF16),
                   jax.ShapeDtypeStruct((BATCH, SEQ, 2 * d), BF16)],
        compiler_params=_params(("parallel", "parallel")),
        name="in_proj",
    )(x, mods3, mods3, norm1, wh, wq, wkv, wg, gq, gk, gmat, cos_t, sin_t)


def _ctx_body(c_ref, sc_ref, sh_ref, n1_ref, wkv_ref, gk_ref, gmat_ref, kc_ref, vc_ref):
    hc = _rms_mod(c_ref[0], n1_ref[...], sc_ref[0], sh_ref[0]).astype(BF16)
    z = _dot(hc, wkv_ref[...])
    kc_ref[0] = _head_norm_rope(z[:, :KV_W], gk_ref[...], gmat_ref[0:KV_W, 0:KV_W], None, None, 1.0).astype(BF16)
    vc_ref[0] = z[:, KV_W:].astype(BF16)


def _ctx_proj(ctx, mods3, norm1, wkv, gk, gmat):
    d = D_MODEL
    const = lambda shape: pl.BlockSpec(shape, lambda b: (0,) * len(shape))
    return pl.pallas_call(
        _ctx_body,
        grid=(BATCH,),
        in_specs=[pl.BlockSpec((1, CTX_LEN, d), lambda b: (b, 0, 0)),
                  pl.BlockSpec((1, 1, d), lambda b: (BATCH, 0, 1)),
                  pl.BlockSpec((1, 1, d), lambda b: (BATCH, 0, 0)),
                  const((1, d)), const((d, 2 * KV_W)), const((1, KV_W)), const((2 * LANES, 2 * LANES))],
        out_specs=[pl.BlockSpec((1, CTX_LEN, KV_W), lambda b: (b, 0, 0)),
                   pl.BlockSpec((1, CTX_LEN, KV_W), lambda b: (b, 0, 0))],
        out_shape=[jax.ShapeDtypeStruct((BATCH, CTX_LEN, KV_W), BF16),
                   jax.ShapeDtypeStruct((BATCH, CTX_LEN, KV_W), BF16)],
        compiler_params=_params(("parallel",)),
        name="ctx_proj",
    )(ctx, mods3, mods3, norm1, wkv, gk, gmat)


def _hyena_body(zv_ref, z1_ref, z2_ref, cwv_ref, cw1_ref, cw2_ref, cbv_ref, cb1_ref, cb2_ref, hb_ref, kf_ref,
                kn_ref, t1_ref, t2_ref, t1t_ref, t2t_ref, o_ref, lo_ref, hi_ref, glo_ref, ghi_ref, p_ref, m_ref,
                za_ref, zb_ref):
    row = lax.broadcasted_iota(I32, (HALF, CT), 0)
    rr = lax.broadcasted_iota(I32, (RB, RB), 0)
    cc = lax.broadcasted_iota(I32, (RB, RB), 1)
    flip = jnp.where(rr + cc == RB - 1, 1.0, 0.0).astype(BF16)
    nrb = HALF // RB

    def folded_short_conv(z_ref, w_ref, b_ref, lo_out, hi_out):
        zlo = z_ref[0, 0:HALF, :].astype(F32)
        for j in range(nrb):
            hi_out[j * RB:(j + 1) * RB, :] = _dot(flip, z_ref[0, SEQ - RB * (j + 1):SEQ - RB * j, :])
        zhi = hi_out[...]
        w0, w1, w2 = w_ref[0:1, :], w_ref[1:2, :], w_ref[2:3, :]
        first, last = row == 0, row == HALF - 1
        lo_prev = jnp.where(first, 0.0, pltpu.roll(zlo, 1, axis=0))
        lo_next = jnp.where(last, zhi[HALF - 1:HALF, :], pltpu.roll(zlo, HALF - 1, axis=0))
        hi_prev = jnp.where(first, 0.0, pltpu.roll(zhi, 1, axis=0))
        hi_next = jnp.where(last, zlo[HALF - 1:HALF, :], pltpu.roll(zhi, HALF - 1, axis=0))
        lo_out[...] = lo_prev * w0 + zlo * w1 + lo_next * w2 + b_ref[...]
        hi_out[...] = hi_next * w0 + zhi * w1 + hi_prev * w2 + b_ref[...]

    folded_short_conv(zv_ref, cwv_ref, cbv_ref, lo_ref, hi_ref)
    sign8 = _sign_rows(HALF)
    odd = (lax.broadcasted_iota(I32, (FC, CT), 0) & 1) == 1
    for o, (zr, cw, cb) in enumerate(((z1_ref, cw1_ref, cb1_ref), (z2_ref, cw2_ref, cb2_ref))):
        folded_short_conv(zr, cw, cb, glo_ref, ghi_ref)
        p_ref[...] = (lo_ref[...] + hi_ref[...]).astype(BF16)
        m_ref[...] = (lo_ref[...] - hi_ref[...]).astype(BF16)
        pv = p_ref[...]
        mv = m_ref[...]
        for c in range(HALF // FC):
            ev = slice(c * FC, (c + 1) * FC)
            od = slice(HALF + c * FC, HALF + (c + 1) * FC)
            xce = _dot(t1_ref[ev, :], pv)
            xso = _dot(t1_ref[od, :], pv)
            xco = _dot(t2_ref[ev, :], mv)
            xse = _dot(t2_ref[od, :], mv)
            kce, kse, kco, kso = kf_ref[o, 0, ev, :], kf_ref[o, 1, ev, :], kf_ref[o, 2, ev, :], kf_ref[o, 3, ev, :]
            za_ref[ev, :] = (xce * kce - xse * kse).astype(BF16)
            za_ref[od, :] = (xco * kso + xso * kco).astype(BF16)
            zb_ref[ev, :] = (xco * kco - xso * kso).astype(BF16)
            zb_ref[od, :] = (xce * kse + xse * kce).astype(BF16)
        zn = _dot(sign8, mv)[0:1] * kn_ref[o]
        bias = hb_ref[o:o + 1, :]
        za = za_ref[...]
        zb = zb_ref[...]
        for c in range(HALF // FC):
            rs = slice(c * FC, (c + 1) * FC)
            half_p = _dot(t1t_ref[rs, :], za)
            half_m = _dot(t2t_ref[rs, :], zb) + jnp.where(odd, -zn, zn)
            lo_ref[rs, :] = glo_ref[rs, :] * (half_p + half_m + bias * lo_ref[rs, :])
            hi_ref[rs, :] = ghi_ref[rs, :] * (half_p - half_m + bias * hi_ref[rs, :])
    o_ref[0, 0:HALF, :] = lo_ref[...].astype(BF16)
    for j in range(nrb):
        o_ref[0, SEQ - RB * (j + 1):SEQ - RB * j, :] = _dot(
            flip, hi_ref[j * RB:(j + 1) * RB, :].astype(BF16)).astype(BF16)


def _hyena(zh, conv_w, conv_b, hbias, kf, kn, t1, t2, t1t, t2t):
    nct = HYENA_W // CT
    zspec = lambda k: pl.BlockSpec((1, SEQ, CT), lambda c, b: (b, 0, k * nct + c))
    wspec = lambda k: pl.BlockSpec((3, CT), lambda c, b: (0, k * nct + c))
    bspec = lambda k: pl.BlockSpec((1, CT), lambda c, b: (0, k * nct + c))
    table = lambda shape: pl.BlockSpec(shape, lambda c, b: (0, 0), pipeline_mode=pl.Buffered(1))
    half_f32 = pltpu.VMEM((HALF, CT), F32)
    return pl.pallas_call(
        _hyena_body,
        grid=(nct, BATCH),
        in_specs=[zspec(0), zspec(1), zspec(2), wspec(0), wspec(1), wspec(2), bspec(0), bspec(1), bspec(2),
                  pl.BlockSpec((HYENA_ORDER, CT), lambda c, b: (0, c)),
                  pl.BlockSpec((HYENA_ORDER, 4, HALF, CT), lambda c, b: (0, 0, 0, c)),
                  pl.BlockSpec((HYENA_ORDER, 1, CT), lambda c, b: (0, 0, c)),
                  table((SEQ, HALF)), table((SEQ, HALF)), table((HALF, SEQ)), table((HALF, SEQ))],
        out_specs=pl.BlockSpec((1, SEQ, CT), lambda c, b: (b, 0, c)),
        out_shape=jax.ShapeDtypeStruct((BATCH, SEQ, HYENA_W), BF16),
        scratch_shapes=[half_f32, half_f32, half_f32, half_f32,
                        pltpu.VMEM((HALF, CT), BF16), pltpu.VMEM((HALF, CT), BF16),
                        pltpu.VMEM((SEQ, CT), BF16), pltpu.VMEM((SEQ, CT), BF16)],
        compiler_params=_params(("parallel", "parallel")),
        name="hyena_conv",
    )(zh, zh, zh, conv_w, conv_w, conv_w, conv_b, conv_b, conv_b, hbias, kf, kn, t1, t2, t1t, t2t)


def _attn_body(sink_ref, q_ref, k_ref, v_ref, kc_ref, vc_ref, bias_ref, o_ref, kp_ref, vlo_ref, vhi_ref):
    nb = SEQ // BLOCK
    lane = lax.broadcasted_iota(I32, (BLOCK, LANES), 1)
    low = lane < HEAD_DIM
    mask_lo = jnp.where(low, 1.0, 0.0).astype(BF16)
    mask_hi = jnp.where(low, 0.0, 1.0).astype(BF16)

    def with_ones(v):
        lane_v = lax.broadcasted_iota(I32, v.shape, 1) < HEAD_DIM
        one = jnp.ones_like(v)
        return jnp.where(lane_v, v, one), jnp.where(lane_v, one, v)

    zpad = jnp.zeros((BLOCK, KV_W), BF16)
    kp_ref[0:BLOCK] = zpad
    kp_ref[BLOCK:BLOCK + SEQ] = k_ref[0]
    kp_ref[BLOCK + SEQ:] = zpad
    v_lo, v_hi = with_ones(v_ref[0])
    for ref, val in ((vlo_ref, v_lo), (vhi_ref, v_hi)):
        ref[0:BLOCK] = zpad
        ref[BLOCK:BLOCK + SEQ] = val
        ref[BLOCK + SEQ:] = zpad
    kc = kc_ref[0]
    vc_pair = with_ones(vc_ref[0])

    def block(n, carry):
        r = pl.multiple_of(n * BLOCK, BLOCK)
        kw = kp_ref[pl.ds(r, 3 * BLOCK), :]
        vw_pair = (vlo_ref[pl.ds(r, 3 * BLOCK), :], vhi_ref[pl.ds(r, 3 * BLOCK), :])
        bias = bias_ref[jnp.where(n == 0, 0, jnp.where(n == nb - 1, 2, 1))]
        scores = []
        for j in range(GROUP):
            qs = q_ref[0, pl.ds(r, BLOCK), j * LANES:(j + 1) * LANES]
            for msk in (mask_lo, mask_hi):
                qm = qs * msk
                scores.append((_dot_nt(qm, kw) + bias, _dot_nt(qm, kc)))
        probs = []
        for idx, (sw, sc) in enumerate(scores):
            snk = sink_ref[idx // 2 + GROUP * (idx % 2)]
            m = jnp.maximum(jnp.maximum(jnp.max(sw, axis=-1, keepdims=True),
                                        jnp.max(sc, axis=-1, keepdims=True)), snk)
            probs.append((jnp.exp2(sw - m).astype(BF16), jnp.exp2(sc - m).astype(BF16), jnp.exp2(snk - m)))
        outs = []
        for idx, (pw, pc, psink) in enumerate(probs):
            acc = _dot(pw, vw_pair[idx % 2]) + _dot(pc, vc_pair[idx % 2])
            den = pltpu.roll(acc, HEAD_DIM, axis=1) + psink
            outs.append(acc / den)
        for j in range(GROUP):
            o_ref[0, pl.ds(r, BLOCK), j * LANES:(j + 1) * LANES] = jnp.where(
                low, outs[2 * j], outs[2 * j + 1]).astype(BF16)
        return carry

    lax.fori_loop(0, nb, block, 0)


def _attention(sink, q, k, v, kc, vc, bias):
    per_b = lambda n, w: pl.BlockSpec((1, n, w), lambda b: (b, 0, 0))
    return pl.pallas_call(
        _attn_body,
        grid=(BATCH,),
        in_specs=[pl.BlockSpec(memory_space=pltpu.SMEM),
                  per_b(SEQ, ATTN_W), per_b(SEQ, KV_W), per_b(SEQ, KV_W), per_b(CTX_LEN, KV_W), per_b(CTX_LEN, KV_W),
                  pl.BlockSpec((3, BLOCK, 3 * BLOCK), lambda b: (0, 0, 0))],
        out_specs=per_b(SEQ, ATTN_W),
        out_shape=jax.ShapeDtypeStruct((BATCH, SEQ, ATTN_W), BF16),
        scratch_shapes=[pltpu.VMEM((SEQ + 2 * BLOCK, KV_W), BF16)] * 3,
        compiler_params=_params(("parallel",)),
        name="window_attn",
    )(sink, q, k, v, kc, vc, bias)


def _merge_body(ya_ref, yb_ref, gt_ref, x_ref, g1_ref, sc2_ref, sh2_ref, n2_ref, wa_ref, wb_ref, wo_ref, rh_ref,
                rl_ref, xn_ref, h2_ref, aff_ref):
    d = D_MODEL
    groups = [slice(i * SUB_MERGE, (i + 1) * SUB_MERGE) for i in range(TM_MERGE // SUB_MERGE)]
    branch = [(_dot(ya_ref[0, s, :], wa_ref[...]), _dot(yb_ref[0, s, :], wb_ref[...])) for s in groups]
    mixed = [(gt_ref[0, s, :d].astype(F32) * ua + gt_ref[0, s, d:].astype(F32) * ub).astype(BF16)
             for s, (ua, ub) in zip(groups, branch)]
    proj = [_dot(u, wo_ref[...]) for u in mixed]
    h2_parts = []
    for s, m in zip(groups, proj):
        xn = x_ref[0, s, :] + g1_ref[0] * m
        xn_ref[0, s, :] = xn
        h2 = _rms_mod(xn, n2_ref[...], sc2_ref[0], sh2_ref[0])
        h2_hi = h2.astype(BF16)
        h2_ref[0, s, :] = h2_hi
        h2_parts.append((h2_hi, (h2 - h2_hi.astype(F32)).astype(BF16)))
    rh = rh_ref[...]
    logits = [_dot_nt(rh, hi) + _dot_nt(rh, lo) + _dot_nt(rl_ref[...], hi) for hi, lo in h2_parts]
    for s, lt in zip(groups, logits):
        e = jnp.exp(lt - jnp.max(lt, axis=0, keepdims=True))
        aff_ref[0, :, s] = e / jnp.sum(e, axis=0, keepdims=True)


def _merge(ya, yb, gates, x, mods3, norm2, wa, wb, wo, router_hi, router_lo):
    d = D_MODEL
    nt = SEQ // TM_MERGE
    const = lambda shape: pl.BlockSpec(shape, lambda b, i: (0,) * len(shape))
    tok = lambda w: pl.BlockSpec((1, TM_MERGE, w), lambda b, i: (b, i, 0))
    mod = lambda k: pl.BlockSpec((1, 1, d), lambda b, i: (b, 0, k))
    return pl.pallas_call(
        _merge_body,
        grid=(BATCH, nt),
        in_specs=[tok(HYENA_W), tok(ATTN_W), tok(2 * d), tok(d), mod(2), mod(4), mod(3), const((1, d)),
                  const((HYENA_W, d)), const((ATTN_W, d)), const((d, d)), const((N_EXPERTS, d)), const((N_EXPERTS, d))],
        out_specs=[tok(d), tok(d), pl.BlockSpec((1, N_EXPERTS, TM_MERGE), lambda b, i: (b, 0, i))],
        out_shape=[jax.ShapeDtypeStruct((BATCH, SEQ, d), F32),
                   jax.ShapeDtypeStruct((BATCH, SEQ, d), BF16),
                   jax.ShapeDtypeStruct((BATCH, N_EXPERTS, SEQ), F32)],
        compiler_params=_params(("parallel", "parallel")),
        name="merge_outproj",
    )(ya, yb, gates, x, mods3, mods3, mods3, norm2, wa, wb, wo, router_hi, router_lo)


def _thresh_body(aff_ref, lo_ref, hi_ref):
    aff = aff_ref[...]
    rows = aff.shape[0]
    bits = pltpu.bitcast(aff, I32)

    def bisect_bits(_, carry):
        lo, hi = carry
        mid = lo + ((hi - lo) >> 1)
        ge = jnp.sum((bits >= mid).astype(I32), axis=1, keepdims=True) >= CAP
        return jnp.where(ge, mid, lo), jnp.where(ge, hi, mid)

    lo0 = jnp.zeros((rows, 1), I32)
    hi0 = jnp.full((rows, 1), 0x3F800001, I32)
    thr_bits, _ = lax.fori_loop(0, 31, bisect_bits, (lo0, hi0))
    thr = pltpu.bitcast(thr_bits, F32)

    def bisect_val(_, carry):
        lo, hi = carry
        mid = 0.5 * (lo + hi)
        ge = jnp.sum(jnp.where(aff >= mid, 1.0, 0.0), axis=1, keepdims=True) >= CAP
        return jnp.where(ge, mid, lo), jnp.where(ge, hi, mid)

    lo, hi = lax.fori_loop(0, 30, bisect_val, (0.5 * thr, jnp.maximum(2.0 * thr, 1e-30)))
    lo_ref[...] = jnp.broadcast_to(lo, lo_ref.shape)
    hi_ref[...] = jnp.broadcast_to(hi, hi_ref.shape)


def _thresholds(aff_rows):
    rows = BATCH * N_EXPERTS
    return pl.pallas_call(
        _thresh_body,
        grid=(1,),
        in_specs=[pl.BlockSpec((rows, SEQ), lambda i: (0, 0))],
        out_specs=[pl.BlockSpec((rows, LANES), lambda i: (0, 0)), pl.BlockSpec((rows, LANES), lambda i: (0, 0))],
        out_shape=[jax.ShapeDtypeStruct((rows, LANES), F32), jax.ShapeDtypeStruct((rows, LANES), F32)],
        compiler_params=_params(("arbitrary",)),
        name="route_threshold",
    )(aff_rows)


def _prefix_counts(mask):
    r = lax.broadcasted_iota(I32, (LANES, LANES), 0)
    c = lax.broadcasted_iota(I32, (LANES, LANES), 1)
    upper = jnp.where(r <= c, 1.0, 0.0).astype(BF16)
    offset = jnp.zeros((mask.shape[0], 1), F32)
    blocks = []
    for j in range(mask.shape[1] // LANES):
        blk = mask[:, j * LANES:(j + 1) * LANES]
        inc = _dot(blk.astype(BF16), upper)
        blocks.append(inc - blk + offset)
        offset = offset + inc[:, LANES - 1:LANES]
    return jnp.concatenate(blocks, axis=1)


def _route_body(aff_ref, lo_ref, hi_ref, h_ref, xin_ref, g_ref, pos_ref, pbuf_ref):
    aff = aff_ref[0]
    above = jnp.where(aff >= hi_ref[:, 0:1], 1.0, 0.0)
    band = jnp.where(aff >= lo_ref[:, 0:1], 1.0, 0.0) - above
    need = CAP - jnp.sum(above, axis=1, keepdims=True)
    tie_rank = _prefix_counts(band)
    self = above + band * jnp.where(tie_rank < need, 1.0, 0.0)
    pos = _prefix_counts(self)
    posi = jnp.where(self > 0.5, pos.astype(I32), -1)
    pos_ref[0] = posi

    h = h_ref[0]
    slot = lax.broadcasted_iota(I32, (CAP, SEQ), 0)
    for grp in range(N_EXPERTS // EXPERT_GROUP):
        for i in range(EXPERT_GROUP):
            e = grp * EXPERT_GROUP + i
            hit = posi[e:e + 1, :] == slot
            pbuf_ref[i * CAP:(i + 1) * CAP, :] = jnp.where(hit, 1.0, 0.0).astype(BF16)
            g_ref[e] = jnp.sum(jnp.where(hit, aff[e:e + 1, :], 0.0), axis=1, keepdims=True)
        rows = _dot(pbuf_ref[...], h)
        for i in range(EXPERT_GROUP):
            xin_ref[grp * EXPERT_GROUP + i] = rows[i * CAP:(i + 1) * CAP].astype(BF16)


def _route(aff_t, lo, hi, h2):
    d = D_MODEL
    return pl.pallas_call(
        _route_body,
        grid=(BATCH,),
        in_specs=[pl.BlockSpec((1, N_EXPERTS, SEQ), lambda b: (b, 0, 0)),
                  pl.BlockSpec((N_EXPERTS, LANES), lambda b: (b, 0)),
                  pl.BlockSpec((N_EXPERTS, LANES), lambda b: (b, 0)),
                  pl.BlockSpec((1, SEQ, d), lambda b: (b, 0, 0))],
        out_specs=[pl.BlockSpec((N_EXPERTS, CAP, d), lambda b: (0, b, 0)),
                   pl.BlockSpec((N_EXPERTS, CAP, 1), lambda b: (0, b, 0)),
                   pl.BlockSpec((1, N_EXPERTS, SEQ), lambda b: (b, 0, 0))],
        out_shape=[jax.ShapeDtypeStruct((N_EXPERTS, BATCH * CAP, d), BF16),
                   jax.ShapeDtypeStruct((N_EXPERTS, BATCH * CAP, 1), F32),
                   jax.ShapeDtypeStruct((BATCH, N_EXPERTS, SEQ), I32)],
        scratch_shapes=[pltpu.VMEM((EXPERT_GROUP * CAP, SEQ), BF16)],
        compiler_params=_params(("parallel",)),
        name="route_gather",
    )(aff_t, lo, hi, h2)


def _expert_body(x_ref, g_ref, wg_ref, wu_ref, wd_ref, o_ref, acc_ref, wgb_ref, wub_ref, wdb_ref):
    f = pl.program_id(1)
    wgb_ref[...] = wg_ref[0].astype(BF16)
    wub_ref[...] = wu_ref[0].astype(BF16)
    wdb_ref[...] = wd_ref[0].astype(BF16)

    @pl.when(f == 0)
    def _():
        acc_ref[...] = jnp.zeros_like(acc_ref)

    for m in range(BATCH * CAP // MC):
        rows = slice(m * MC, (m + 1) * MC)
        xm = x_ref[0, rows, :]
        a = _dot(xm, wgb_ref[...])
        b = _dot(xm, wub_ref[...])
        hh = (a * jax.nn.sigmoid(a) * b).astype(BF16)
        acc_ref[rows, :] += _dot(hh, wdb_ref[...])

    @pl.when(f == D_FF // TF - 1)
    def _():
        o_ref[0] = (acc_ref[...] * g_ref[0]).astype(BF16)


def _experts(xin, g, w_gate, w_up, w_down):
    d = D_MODEL
    rows = BATCH * CAP
    return pl.pallas_call(
        _expert_body,
        grid=(N_EXPERTS, D_FF // TF),
        in_specs=[pl.BlockSpec((1, rows, d), lambda e, f: (e, 0, 0)),
                  pl.BlockSpec((1, rows, 1), lambda e, f: (e, 0, 0)),
                  pl.BlockSpec((1, d, TF), lambda e, f: (e, 0, f)),
                  pl.BlockSpec((1, d, TF), lambda e, f: (e, 0, f)),
                  pl.BlockSpec((1, TF, d), lambda e, f: (e, f, 0))],
        out_specs=pl.BlockSpec((1, rows, d), lambda e, f: (e, 0, 0)),
        out_shape=jax.ShapeDtypeStruct((N_EXPERTS, rows, d), BF16),
        scratch_shapes=[pltpu.VMEM((rows, d), F32), pltpu.VMEM((d, TF), BF16), pltpu.VMEM((d, TF), BF16),
                        pltpu.VMEM((TF, d), BF16)],
        compiler_params=_params(("parallel", "arbitrary")),
        name="swiglu_experts",
    )(xin, g, w_gate, w_up, w_down)


def _scatter_body(pos_ref, y_ref, x_ref, g2_ref, o_ref):
    pos = pos_ref[0]
    slot = lax.broadcasted_iota(I32, (TS, CAP), 1)
    onehot = jnp.concatenate([jnp.where(pos[:, e:e + 1] == slot, 1.0, 0.0).astype(BF16) for e in range(N_EXPERTS)],
                             axis=1)
    y = y_ref[...].reshape(N_EXPERTS * CAP, D_MODEL)
    o_ref[0] = x_ref[0] + g2_ref[0] * _dot(onehot, y)


def _scatter(pos_tok, y, xn, mods3):
    d = D_MODEL
    return pl.pallas_call(
        _scatter_body,
        grid=(BATCH, SEQ // TS),
        in_specs=[pl.BlockSpec((1, TS, N_EXPERTS), lambda b, i: (b, i, 0)),
                  pl.BlockSpec((N_EXPERTS, CAP, d), lambda b, i: (0, b, 0)),
                  pl.BlockSpec((1, TS, d), lambda b, i: (b, i, 0)),
                  pl.BlockSpec((1, 1, d), lambda b, i: (b, 0, 5))],
        out_specs=pl.BlockSpec((1, TS, d), lambda b, i: (b, i, 0)),
        out_shape=jax.ShapeDtypeStruct((BATCH, SEQ, d), F32),
        compiler_params=_params(("parallel", "parallel")),
        name="scatter_residual",
    )(pos_tok, y, xn, mods3)


def _rope_tables():
    rows = SEQ // GRID_W
    row = np.repeat(np.arange(rows, dtype=np.float32), GRID_W)
    col = np.tile(np.arange(GRID_W, dtype=np.float32), rows)
    inv = (ROPE_BASE ** (-np.arange(0, AXIS_ROT, 2, dtype=np.float32) / AXIS_ROT)).astype(np.float32)
    ang = np.concatenate([row[:, None] * inv, col[:, None] * inv], axis=-1).astype(np.float64)
    cos = np.repeat(np.cos(ang), 2, axis=-1)
    sin = np.stack([-np.sin(ang), np.sin(ang)], axis=-1).reshape(SEQ, HEAD_DIM)
    reps = LANES // HEAD_DIM
    return jnp.asarray(np.tile(cos, (1, reps)), F32), jnp.asarray(np.tile(sin, (1, reps)), F32)


def _cos_sin_outer(f, mults):
    unit = math.pi / N_FFT
    step = 32
    coarse = 2 * step * jnp.arange(mults.shape[0] // step, dtype=I32)
    fine = mults[:step]
    a = ((f[:, None] * coarse[None, :]) % (2 * N_FFT)).astype(F32) * unit
    b = ((f[:, None] * fine[None, :]) % (2 * N_FFT)).astype(F32) * unit
    ca, sa, cb, sb = jnp.cos(a)[:, :, None], jnp.sin(a)[:, :, None], jnp.cos(b)[:, None, :], jnp.sin(b)[:, None, :]
    n = f.shape[0]
    return (ca * cb - sa * sb).reshape(n, -1), (sa * cb + ca * sb).reshape(n, -1)


def _dft_tables():
    idx = jnp.arange(HALF, dtype=I32)
    t2p1 = 2 * idx + 1
    ce, se = _cos_sin_outer(2 * idx, t2p1)
    co, so = _cos_sin_outer(2 * idx + 1, t2p1)
    t1 = jnp.concatenate([ce, so], axis=0).astype(BF16)
    t2 = jnp.concatenate([co, se], axis=0).astype(BF16)
    return t1, t2, t1.T, t2.T


def _phase_tables():
    idx = np.arange(HALF, dtype=np.float64)
    w = np.full((HALF,), 2.0 / N_FFT)
    we = w.copy()
    we[0] = 1.0 / N_FFT
    pe = (math.pi / N_FFT) * (2.0 * idx)
    po = (math.pi / N_FFT) * (2.0 * idx + 1.0)
    rot = np.stack([we * np.cos(pe), we * np.sin(pe), w * np.cos(po), w * np.sin(po)])
    return jnp.asarray(np.broadcast_to(rot[:, :, None], (4, HALF, CT)), F32)


def _fold_rows(a):
    return np.concatenate([a[:HALF], a[HALF:][::-1]], axis=0)


def _filter_features():
    t = np.linspace(0.0, 1.0, SEQ, dtype=np.float32).astype(np.float64)[:, None]
    w = 2.0 * math.pi * np.arange(SEQ, dtype=np.float64)[:, None] / SEQ
    fr = np.linspace(1e-4, FILTER_BANDS - 1, FILTER_BANDS, dtype=np.float32).astype(np.float64)[None, :]
    feat = np.concatenate([t, np.cos(fr * w), -np.sin(fr * w)], axis=-1)
    feat = np.pad(feat, ((0, 0), (0, FILTER_HIDDEN - FILTER_EMB)))
    min_decay = math.log(DECAY_TARGET) / SLOW_DECAY_PCT
    max_decay = math.log(DECAY_TARGET) / FAST_DECAY_PCT
    deltas = np.linspace(min_decay, max_decay, HYENA_W, dtype=np.float32).astype(np.float64)
    decay = np.exp(-t * np.abs(deltas))
    return jnp.asarray(_fold_rows(feat).T, F32), jnp.asarray(_fold_rows(decay), F32)


def _attn_bias():
    qi = np.arange(BLOCK)[:, None]
    kj = np.arange(3 * BLOCK)[None, :]
    band = np.abs(kj - BLOCK - qi) <= WINDOW
    first = band & (kj >= BLOCK)
    last = band & (kj < 2 * BLOCK)
    return jnp.asarray(np.where(np.stack([first, band, last]), 0.0, NEG), F32)


def _pair_heads(w, axis):
    heads = [lax.slice_in_dim(w, h * HEAD_DIM, (h + 1) * HEAD_DIM, axis=axis) for h in range(N_HEADS)]
    return jnp.concatenate([heads[j + GROUP * half] for j in range(GROUP) for half in range(N_KV_HEADS)], axis=axis)


def kernel(x, c, ctx, c_ctx, ada_w, ada_b, norm1, norm2, w_in, conv_w, conv_b, filt_w1, filt_b1, filt_w2, filt_b2,
           filt_w3, filt_b3, filt_freq, filt_out, hyena_bias, q_norm, k_norm, attn_sink, w_branch_a, w_branch_b,
           w_out, router, w_gate, w_up, w_down):
    d = D_MODEL
    assert ada_w.shape[0] == 1, "only the single-layer configuration is implemented"
    l = 0
    cos_t, sin_t = _rope_tables()
    t1, t2, t1t, t2t = _dft_tables()
    rot = _phase_tables()
    feat, decay = _filter_features()
    bias = _attn_bias()
    gmat = jnp.asarray(np.kron(np.eye(2 * LANES // HEAD_DIM), np.full((HEAD_DIM, HEAD_DIM), 1.0 / HEAD_DIM)), BF16)
    c16 = jnp.zeros((MOD_ROWS, d), F32).at[:BATCH].set(c).at[BATCH].set(c_ctx)

    mods3 = _ada(c16, ada_w[l], ada_b[l][None, :]).reshape(MOD_ROWS, 1, 6 * d)
    n1 = norm1[l][None, :]
    w = w_in[l]
    wkv = w[:, OFF_K:OFF_G].astype(BF16)
    gk = jnp.tile(k_norm[l], N_KV_HEADS)[None, :]
    kf, kn = _filters(feat, jnp.pad(filt_w1[l].T, ((0, 0), (0, FILTER_HIDDEN - FILTER_EMB))), filt_b1[l][:, None],
                      filt_w2[l].T, filt_b2[l][:, None], filt_w3[l].T, filt_b3[l][:, None], filt_freq[l][:, None],
                      filt_out[l], decay, rot, t1, t2)
    kc, vc = _ctx_proj(ctx, mods3, n1, wkv, gk, gmat)
    zh, q, k, v, gates = _inproj(
        x, mods3, n1, w[:, :OFF_Q].astype(BF16), _pair_heads(w[:, OFF_Q:OFF_K], 1).astype(BF16), wkv,
        w[:, OFF_G:].astype(BF16), jnp.tile(q_norm[l], N_HEADS)[None, :], gk, gmat, cos_t, sin_t)
    ya = _hyena(zh, conv_w[l], conv_b[l][None, :], hyena_bias[l], kf, kn, t1, t2, t1t, t2t)
    yb = _attention(attn_sink[l] * LOG2E, q, k, v, kc, vc, bias)
    router_t = router[l].T
    router_hi = router_t.astype(BF16)
    router_lo = (router_t - router_hi.astype(F32)).astype(BF16)
    xn, h2, aff_t = _merge(ya, yb, gates, x, mods3, norm2[l][None, :], w_branch_a[l].astype(BF16),
                           _pair_heads(w_branch_b[l], 0).astype(BF16), w_out[l].astype(BF16), router_hi, router_lo)
    lo, hi = _thresholds(aff_t.reshape(BATCH * N_EXPERTS, SEQ))
    xin, g, pos = _route(aff_t, lo, hi, h2)
    y = _experts(xin, g, w_gate[l], w_up[l], w_down[l])
    return _scatter(jnp.swapaxes(pos, 1, 2), y, xn, mods3)
```

```python
import math

import numpy as np
import jax
import jax.numpy as jnp
from jax import lax
from jax.experimental import pallas as pl
from jax.experimental.pallas import tpu as pltpu

F32 = jnp.float32
BF16 = jnp.bfloat16
I32 = jnp.int32
HIGHEST = lax.Precision.HIGHEST

D_MODEL = 1024
BATCH = 8
SEQ = 2048
GRID_W = 64
CTX_LEN = 256
N_HEADS = 8
N_KV_HEADS = 2
HEAD_DIM = 64
GROUP = N_HEADS // N_KV_HEADS
ATTN_W = N_HEADS * HEAD_DIM
KV_W = N_KV_HEADS * HEAD_DIM
WINDOW = 128
BLOCK = 128
HYENA_W = D_MODEL // 2
HYENA_ORDER = 2
FILTER_BANDS = 16
FILTER_EMB = 1 + 2 * FILTER_BANDS
FILTER_HIDDEN = 64
DECAY_TARGET = 1e-2
FAST_DECAY_PCT = 0.3
SLOW_DECAY_PCT = 1.5
ROPE_BASE = 10000.0
AXIS_ROT = HEAD_DIM // 2
N_EXPERTS = 16
EC_CAPACITY = 2
D_FF = 2048
EPS = 1e-6
NEG = -1e30
LOG2E = math.log2(math.e)

OFF_Q = 3 * HYENA_W
OFF_K = OFF_Q + ATTN_W
OFF_V = OFF_K + KV_W
OFF_G = OFF_V + KV_W
IN_W = OFF_G + 2 * D_MODEL

CAP = EC_CAPACITY * SEQ // N_EXPERTS
N_FFT = 2 * SEQ
HALF = SEQ // 2
MOD_ROWS = 16
LANES = 128

TM_IN = 1024
SUB_IN = 512
TM_MERGE = 1024
SUB_MERGE = 512
CT = 256
FC = 512
RB = 256
TF = 512
MC = 512
TS = 512
EXPERT_GROUP = 4
VMEM_LIMIT = 56 * 1024 * 1024


def _dot(a, b, precision=None):
    return jnp.dot(a, b, preferred_element_type=F32, precision=precision)


def _dot_nt(a, b, precision=None):
    return lax.dot_general(a, b, (((1,), (1,)), ((), ())), preferred_element_type=F32, precision=precision)


def _params(sem, vmem=VMEM_LIMIT):
    return pltpu.CompilerParams(dimension_semantics=sem, vmem_limit_bytes=vmem)


def _rms_mod(x, g, sc, sh):
    ms = jnp.mean(x * x, axis=-1, keepdims=True)
    return (x * lax.rsqrt(ms + EPS) * g) * (1.0 + sc) + sh


def _head_norm_rope(z, g, gmat, cos, sin, scale):
    ms = _dot((z * z).astype(BF16), gmat)
    y = z * lax.rsqrt(ms + EPS) * g
    if cos is not None:
        slabs = []
        for s in range(z.shape[1] // LANES):
            ys = y[:, s * LANES:(s + 1) * LANES]
            lane = lax.broadcasted_iota(I32, ys.shape, 1)
            nxt = pltpu.roll(ys, LANES - 1, axis=1)
            prv = pltpu.roll(ys, 1, axis=1)
            slabs.append(ys * cos + jnp.where((lane & 1) == 0, nxt, prv) * sin)
        y = slabs[0] if len(slabs) == 1 else jnp.concatenate(slabs, axis=1)
    return y * scale


def _split_bf16(x):
    hi = x.astype(BF16)
    return hi, (x - hi.astype(F32)).astype(BF16)


def _ada_body(c_ref, w_ref, b_ref, o_ref):
    c = c_ref[...]
    s_hi, s_lo = _split_bf16(c * jax.nn.sigmoid(c))
    w_hi, w_lo = _split_bf16(w_ref[...])
    o_ref[...] = _dot(s_hi, w_hi) + _dot(s_lo, w_hi) + _dot(s_hi, w_lo) + b_ref[...]


def _ada(c16, w, b):
    d = D_MODEL
    return pl.pallas_call(
        _ada_body,
        grid=(6,),
        in_specs=[pl.BlockSpec((MOD_ROWS, d), lambda j: (0, 0)),
                  pl.BlockSpec((d, d), lambda j: (0, j)),
                  pl.BlockSpec((1, d), lambda j: (0, j))],
        out_specs=pl.BlockSpec((MOD_ROWS, d), lambda j: (0, j)),
        out_shape=jax.ShapeDtypeStruct((MOD_ROWS, 6 * d), F32),
        compiler_params=_params(("parallel",)),
        name="ada_mod",
    )(c16, w, b)


def _sign_rows(n):
    lane = lax.broadcasted_iota(I32, (8, n), 1)
    sub = lax.broadcasted_iota(I32, (8, n), 0)
    sg = jnp.where((lane & 1) == 0, 1.0, -1.0)
    return jnp.where(sub == 0, sg, 0.0).astype(BF16)


def _filt_body(feat_ref, w1_ref, b1_ref, w2_ref, b2_ref, w3_ref, b3_ref, fq_ref, fof_ref, fob_ref, dec_ref,
               rot_ref, t1_ref, t2_ref, kf_ref, kn_ref, hh_ref, hl_ref):
    @pl.when((pl.program_id(0) == 0) & (pl.program_id(1) == 0))
    def _():
        fq = fq_ref[...]
        h = jnp.sin(fq * (_dot(w1_ref[...], feat_ref[...], HIGHEST) + b1_ref[...]))
        h = jnp.sin(fq * (_dot(w2_ref[...], h, HIGHEST) + b2_ref[...]))
        h = jnp.sin(fq * (_dot(w3_ref[...], h, HIGHEST) + b3_ref[...]))
        hh_ref[...], hl_ref[...] = _split_bf16(h.T)

    def taps(fo_ref):
        f_hi, f_lo = _split_bf16(fo_ref[...])
        return _dot(hh_ref[...], f_hi) + _dot(hl_ref[...], f_hi) + _dot(hh_ref[...], f_lo)

    dec = dec_ref[...]
    hf = taps(fof_ref) * dec
    hb = taps(fob_ref) * dec
    row = lax.broadcasted_iota(I32, hf.shape, 0)
    hb = jnp.where(row == 0, 0.0, hb)
    a = hf + hb
    b = hf - hb
    pa = (a[:HALF] + a[HALF:]).astype(BF16)
    ma = (a[:HALF] - a[HALF:]).astype(BF16)
    pb = (b[:HALF] + b[HALF:]).astype(BF16)
    mb = (b[:HALF] - b[HALF:]).astype(BF16)
    t1 = t1_ref[...]
    t2 = t2_ref[...]
    a1 = _dot(t1, pa)
    a2 = _dot(t2, ma)
    b1 = _dot(t1, pb)
    b2 = _dot(t2, mb)
    ce, se, co, so = rot_ref[0], rot_ref[1], rot_ref[2], rot_ref[3]
    kf_ref[0, 0] = a1[:HALF] * ce + a2[HALF:] * se
    kf_ref[0, 1] = b2[HALF:] * ce - b1[:HALF] * se
    kf_ref[0, 2] = a2[:HALF] * co + a1[HALF:] * so
    kf_ref[0, 3] = b1[HALF:] * co - b2[:HALF] * so
    kn_ref[0] = _dot(_sign_rows(HALF), ma)[0:1] * (1.0 / N_FFT)


def _filters(feat, w1, b1, w2, b2, w3, b3, fq, fout, decay, rot, t1, t2):
    nct = HYENA_W // CT
    full = lambda shape: pl.BlockSpec(shape, lambda o, c: (0,) * len(shape))
    return pl.pallas_call(
        _filt_body,
        grid=(HYENA_ORDER, nct),
        in_specs=[full((FILTER_HIDDEN, SEQ)), full((FILTER_HIDDEN, FILTER_HIDDEN)), full((FILTER_HIDDEN, 1)),
                  full((FILTER_HIDDEN, FILTER_HIDDEN)), full((FILTER_HIDDEN, 1)),
                  full((FILTER_HIDDEN, FILTER_HIDDEN)), full((FILTER_HIDDEN, 1)), full((FILTER_HIDDEN, 1)),
                  pl.BlockSpec((FILTER_HIDDEN, CT), lambda o, c: (0, (o * 2 + 0) * nct + c)),
                  pl.BlockSpec((FILTER_HIDDEN, CT), lambda o, c: (0, (o * 2 + 1) * nct + c)),
                  pl.BlockSpec((SEQ, CT), lambda o, c: (0, c)),
                  full((4, HALF, CT)),
                  pl.BlockSpec((SEQ, HALF), lambda o, c: (0, 0), pipeline_mode=pl.Buffered(1)),
                  pl.BlockSpec((SEQ, HALF), lambda o, c: (0, 0), pipeline_mode=pl.Buffered(1))],
        out_specs=[pl.BlockSpec((1, 4, HALF, CT), lambda o, c: (o, 0, 0, c)),
                   pl.BlockSpec((1, 1, CT), lambda o, c: (o, 0, c))],
        out_shape=[jax.ShapeDtypeStruct((HYENA_ORDER, 4, HALF, HYENA_W), F32),
                   jax.ShapeDtypeStruct((HYENA_ORDER, 1, HYENA_W), F32)],
        scratch_shapes=[pltpu.VMEM((SEQ, FILTER_HIDDEN), BF16)] * 2,
        compiler_params=_params(("arbitrary", "arbitrary")),
        name="hyena_filters",
    )(feat, w1, b1, w2, b2, w3, b3, fq, fout, fout, decay, rot, t1, t2)


def _inproj_body(x_ref, sc_ref, sh_ref, n1_ref, wh_ref, wq_ref, wkv_ref, wg_ref, gq_ref, gk_ref, gmat_ref,
                 cos_ref, sin_ref, zh_ref, q_ref, k_ref, v_ref, gate_ref):
    groups = [slice(i * SUB_IN, (i + 1) * SUB_IN) for i in range(TM_IN // SUB_IN)]
    hx = [_rms_mod(x_ref[0, s, :], n1_ref[...], sc_ref[0], sh_ref[0]).astype(BF16) for s in groups]
    for s, h in zip(groups, hx):
        zh_ref[0, s, :] = _dot(h, wh_ref[...]).astype(BF16)
    pair = 2 * LANES
    zq = [_dot(h, wq_ref[...]) for h in hx]
    for s, z in zip(groups, zq):
        for c in range(ATTN_W // pair):
            sl = slice(c * pair, (c + 1) * pair)
            q_ref[0, s, sl] = _head_norm_rope(z[:, sl], gq_ref[:, sl], gmat_ref[...], cos_ref[s, :], sin_ref[s, :],
                                              LOG2E * HEAD_DIM ** -0.5).astype(BF16)
    zkv = [_dot(h, wkv_ref[...]) for h in hx]
    for s, z in zip(groups, zkv):
        k_ref[0, s, :] = _head_norm_rope(z[:, :KV_W], gk_ref[...], gmat_ref[0:KV_W, 0:KV_W], cos_ref[s, :],
                                         sin_ref[s, :], 1.0).astype(BF16)
        v_ref[0, s, :] = z[:, KV_W:].astype(BF16)
    for s, h in zip(groups, hx):
        gate_ref[0, s, :] = jax.nn.sigmoid(_dot(h, wg_ref[...])).astype(BF16)


def _pack_in_weights(w):
    return jnp.concatenate([w[:, :OFF_Q], _pair_heads(w[:, OFF_Q:OFF_K], 1), w[:, OFF_G:], w[:, OFF_K:OFF_G]],
                           axis=1).astype(BF16)


W_OFF_H = 0
W_OFF_Q = OFF_Q
W_OFF_G = OFF_Q + ATTN_W
W_OFF_KV = W_OFF_G + 2 * D_MODEL


def _inproj(x, mods3, norm1, w_packed, gq, gk, gmat, cos_t, sin_t):
    d = D_MODEL
    nt = SEQ // TM_IN
    const = lambda shape: pl.BlockSpec(shape, lambda b, i: (0,) * len(shape))

    def wcol(width, off):
        assert off % width == 0
        return pl.BlockSpec((d, width), lambda b, i: (0, off // width))

    tok = lambda w: pl.BlockSpec((1, TM_IN, w), lambda b, i: (b, i, 0))
    return pl.pallas_call(
        _inproj_body,
        grid=(BATCH, nt),
        in_specs=[tok(d),
                  pl.BlockSpec((1, 1, d), lambda b, i: (b, 0, 1)),
                  pl.BlockSpec((1, 1, d), lambda b, i: (b, 0, 0)),
                  const((1, d)), wcol(OFF_Q, W_OFF_H), wcol(ATTN_W, W_OFF_Q), wcol(2 * KV_W, W_OFF_KV),
                  wcol(2 * d, W_OFF_G), const((1, ATTN_W)), const((1, KV_W)), const((2 * LANES, 2 * LANES)),
                  pl.BlockSpec((TM_IN, LANES), lambda b, i: (i, 0)),
                  pl.BlockSpec((TM_IN, LANES), lambda b, i: (i, 0))],
        out_specs=[tok(OFF_Q), tok(ATTN_W), tok(KV_W), tok(KV_W), tok(2 * d)],
        out_shape=[jax.ShapeDtypeStruct((BATCH, SEQ, OFF_Q), BF16),
                   jax.ShapeDtypeStruct((BATCH, SEQ, ATTN_W), BF16),
                   jax.ShapeDtypeStruct((BATCH, SEQ, KV_W), BF16),
                   jax.ShapeDtypeStruct((BATCH, SEQ, KV_W), BF16),
                   jax.ShapeDtypeStruct((BATCH, SEQ, 2 * d), BF16)],
        compiler_params=_params(("parallel", "parallel")),
        name="in_proj",
    )(x, mods3, mods3, norm1, w_packed, w_packed, w_packed, w_packed, gq, gk, gmat, cos_t, sin_t)


def _ctx_body(c_ref, sc_ref, sh_ref, n1_ref, wkv_ref, gk_ref, gmat_ref, kc_ref, vc_ref):
    hc = _rms_mod(c_ref[0], n1_ref[...], sc_ref[0], sh_ref[0]).astype(BF16)
    z = _dot(hc, wkv_ref[...])
    kc_ref[0] = _head_norm_rope(z[:, :KV_W], gk_ref[...], gmat_ref[0:KV_W, 0:KV_W], None, None, 1.0).astype(BF16)
    vc_ref[0] = z[:, KV_W:].astype(BF16)


def _ctx_proj(ctx, mods3, norm1, w_packed, gk, gmat):
    d = D_MODEL
    const = lambda shape: pl.BlockSpec(shape, lambda b: (0,) * len(shape))
    return pl.pallas_call(
        _ctx_body,
        grid=(BATCH,),
        in_specs=[pl.BlockSpec((1, CTX_LEN, d), lambda b: (b, 0, 0)),
                  pl.BlockSpec((1, 1, d), lambda b: (BATCH, 0, 1)),
                  pl.BlockSpec((1, 1, d), lambda b: (BATCH, 0, 0)),
                  const((1, d)), pl.BlockSpec((d, 2 * KV_W), lambda b: (0, W_OFF_KV // (2 * KV_W))),
                  const((1, KV_W)), const((2 * LANES, 2 * LANES))],
        out_specs=[pl.BlockSpec((1, CTX_LEN, KV_W), lambda b: (b, 0, 0)),
                   pl.BlockSpec((1, CTX_LEN, KV_W), lambda b: (b, 0, 0))],
        out_shape=[jax.ShapeDtypeStruct((BATCH, CTX_LEN, KV_W), BF16),
                   jax.ShapeDtypeStruct((BATCH, CTX_LEN, KV_W), BF16)],
        compiler_params=_params(("parallel",)),
        name="ctx_proj",
    )(ctx, mods3, mods3, norm1, w_packed, gk, gmat)


def _hyena_body(zv_ref, z1_ref, z2_ref, cwv_ref, cw1_ref, cw2_ref, cbv_ref, cb1_ref, cb2_ref, hb_ref, kf_ref,
                kn_ref, t1_ref, t2_ref, t1t_ref, t2t_ref, o_ref, lo_ref, hi_ref, glo_ref, ghi_ref, p_ref, m_ref,
                za_ref, zb_ref):
    row = lax.broadcasted_iota(I32, (HALF, CT), 0)
    rr = lax.broadcasted_iota(I32, (RB, RB), 0)
    cc = lax.broadcasted_iota(I32, (RB, RB), 1)
    flip = jnp.where(rr + cc == RB - 1, 1.0, 0.0).astype(BF16)
    nrb = HALF // RB

    def folded_short_conv(z_ref, w_ref, b_ref, lo_out, hi_out):
        zlo = z_ref[0, 0:HALF, :].astype(F32)
        for j in range(nrb):
            hi_out[j * RB:(j + 1) * RB, :] = _dot(flip, z_ref[0, SEQ - RB * (j + 1):SEQ - RB * j, :])
        zhi = hi_out[...]
        w0, w1, w2 = w_ref[0:1, :], w_ref[1:2, :], w_ref[2:3, :]
        first, last = row == 0, row == HALF - 1
        lo_prev = jnp.where(first, 0.0, pltpu.roll(zlo, 1, axis=0))
        lo_next = jnp.where(last, zhi[HALF - 1:HALF, :], pltpu.roll(zlo, HALF - 1, axis=0))
        hi_prev = jnp.where(first, 0.0, pltpu.roll(zhi, 1, axis=0))
        hi_next = jnp.where(last, zlo[HALF - 1:HALF, :], pltpu.roll(zhi, HALF - 1, axis=0))
        lo_out[...] = lo_prev * w0 + zlo * w1 + lo_next * w2 + b_ref[...]
        hi_out[...] = hi_next * w0 + zhi * w1 + hi_prev * w2 + b_ref[...]

    folded_short_conv(zv_ref, cwv_ref, cbv_ref, lo_ref, hi_ref)
    sign8 = _sign_rows(HALF)
    odd = (lax.broadcasted_iota(I32, (FC, CT), 0) & 1) == 1
    for o, (zr, cw, cb) in enumerate(((z1_ref, cw1_ref, cb1_ref), (z2_ref, cw2_ref, cb2_ref))):
        folded_short_conv(zr, cw, cb, glo_ref, ghi_ref)
        p_ref[...] = (lo_ref[...] + hi_ref[...]).astype(BF16)
        m_ref[...] = (lo_ref[...] - hi_ref[...]).astype(BF16)
        pv = p_ref[...]
        mv = m_ref[...]
        for c in range(HALF // FC):
            ev = slice(c * FC, (c + 1) * FC)
            od = slice(HALF + c * FC, HALF + (c + 1) * FC)
            xce = _dot(t1_ref[ev, :], pv)
            xso = _dot(t1_ref[od, :], pv)
            xco = _dot(t2_ref[ev, :], mv)
            xse = _dot(t2_ref[od, :], mv)
            kce, kse, kco, kso = kf_ref[o, 0, ev, :], kf_ref[o, 1, ev, :], kf_ref[o, 2, ev, :], kf_ref[o, 3, ev, :]
            za_ref[ev, :] = (xce * kce - xse * kse).astype(BF16)
            za_ref[od, :] = (xco * kso + xso * kco).astype(BF16)
            zb_ref[ev, :] = (xco * kco - xso * kso).astype(BF16)
            zb_ref[od, :] = (xce * kse + xse * kce).astype(BF16)
        zn = _dot(sign8, mv)[0:1] * kn_ref[o]
        bias = hb_ref[o:o + 1, :]
        za = za_ref[...]
        zb = zb_ref[...]
        for c in range(HALF // FC):
            rs = slice(c * FC, (c + 1) * FC)
            half_p = _dot(t1t_ref[rs, :], za)
            half_m = _dot(t2t_ref[rs, :], zb) + jnp.where(odd, -zn, zn)
            lo_ref[rs, :] = glo_ref[rs, :] * (half_p + half_m + bias * lo_ref[rs, :])
            hi_ref[rs, :] = ghi_ref[rs, :] * (half_p - half_m + bias * hi_ref[rs, :])
    o_ref[0, 0:HALF, :] = lo_ref[...].astype(BF16)
    for j in range(nrb):
        o_ref[0, SEQ - RB * (j + 1):SEQ - RB * j, :] = _dot(
            flip, hi_ref[j * RB:(j + 1) * RB, :].astype(BF16)).astype(BF16)


def _hyena(zh, conv_w, conv_b, hbias, kf, kn, t1, t2, t1t, t2t):
    nct = HYENA_W // CT
    zspec = lambda k: pl.BlockSpec((1, SEQ, CT), lambda c, b: (b, 0, k * nct + c))
    wspec = lambda k: pl.BlockSpec((3, CT), lambda c, b: (0, k * nct + c))
    bspec = lambda k: pl.BlockSpec((1, CT), lambda c, b: (0, k * nct + c))
    table = lambda shape: pl.BlockSpec(shape, lambda c, b: (0, 0), pipeline_mode=pl.Buffered(1))
    half_f32 = pltpu.VMEM((HALF, CT), F32)
    return pl.pallas_call(
        _hyena_body,
        grid=(nct, BATCH),
        in_specs=[zspec(0), zspec(1), zspec(2), wspec(0), wspec(1), wspec(2), bspec(0), bspec(1), bspec(2),
                  pl.BlockSpec((HYENA_ORDER, CT), lambda c, b: (0, c)),
                  pl.BlockSpec((HYENA_ORDER, 4, HALF, CT), lambda c, b: (0, 0, 0, c)),
                  pl.BlockSpec((HYENA_ORDER, 1, CT), lambda c, b: (0, 0, c)),
                  table((SEQ, HALF)), table((SEQ, HALF)), table((HALF, SEQ)), table((HALF, SEQ))],
        out_specs=pl.BlockSpec((1, SEQ, CT), lambda c, b: (b, 0, c)),
        out_shape=jax.ShapeDtypeStruct((BATCH, SEQ, HYENA_W), BF16),
        scratch_shapes=[half_f32, half_f32, half_f32, half_f32,
                        pltpu.VMEM((HALF, CT), BF16), pltpu.VMEM((HALF, CT), BF16),
                        pltpu.VMEM((SEQ, CT), BF16), pltpu.VMEM((SEQ, CT), BF16)],
        compiler_params=_params(("parallel", "parallel")),
        name="hyena_conv",
    )(zh, zh, zh, conv_w, conv_w, conv_w, conv_b, conv_b, conv_b, hbias, kf, kn, t1, t2, t1t, t2t)


def _attn_body(sink_ref, q_ref, k_ref, v_ref, kc_ref, vc_ref, bias_ref, o_ref, kp_ref, vlo_ref, vhi_ref):
    nb = SEQ // BLOCK
    lane = lax.broadcasted_iota(I32, (BLOCK, LANES), 1)
    low = lane < HEAD_DIM
    mask_lo = jnp.where(low, 1.0, 0.0).astype(BF16)
    mask_hi = jnp.where(low, 0.0, 1.0).astype(BF16)

    def with_ones(v):
        lane_v = lax.broadcasted_iota(I32, v.shape, 1) < HEAD_DIM
        one = jnp.ones_like(v)
        return jnp.where(lane_v, v, one), jnp.where(lane_v, one, v)

    zpad = jnp.zeros((BLOCK, KV_W), BF16)
    kp_ref[0:BLOCK] = zpad
    kp_ref[BLOCK:BLOCK + SEQ] = k_ref[0]
    kp_ref[BLOCK + SEQ:] = zpad
    v_lo, v_hi = with_ones(v_ref[0])
    for ref, val in ((vlo_ref, v_lo), (vhi_ref, v_hi)):
        ref[0:BLOCK] = zpad
        ref[BLOCK:BLOCK + SEQ] = val
        ref[BLOCK + SEQ:] = zpad
    kc = kc_ref[0]
    vc_pair = with_ones(vc_ref[0])

    def block(n, carry):
        r = pl.multiple_of(n * BLOCK, BLOCK)
        kw = kp_ref[pl.ds(r, 3 * BLOCK), :]
        vw_pair = (vlo_ref[pl.ds(r, 3 * BLOCK), :], vhi_ref[pl.ds(r, 3 * BLOCK), :])
        bias = bias_ref[jnp.where(n == 0, 0, jnp.where(n == nb - 1, 2, 1))]
        scores = []
        for j in range(GROUP):
            qs = q_ref[0, pl.ds(r, BLOCK), j * LANES:(j + 1) * LANES]
            for msk in (mask_lo, mask_hi):
                qm = qs * msk
                scores.append((_dot_nt(qm, kw) + bias, _dot_nt(qm, kc)))
        probs = []
        for idx, (sw, sc) in enumerate(scores):
            snk = sink_ref[idx // 2 + GROUP * (idx % 2)]
            m = jnp.maximum(jnp.maximum(jnp.max(sw, axis=-1, keepdims=True),
                                        jnp.max(sc, axis=-1, keepdims=True)), snk)
            probs.append((jnp.exp2(sw - m).astype(BF16), jnp.exp2(sc - m).astype(BF16), jnp.exp2(snk - m)))
        outs = []
        for idx, (pw, pc, psink) in enumerate(probs):
            acc = _dot(pw, vw_pair[idx % 2]) + _dot(pc, vc_pair[idx % 2])
            den = pltpu.roll(acc, HEAD_DIM, axis=1) + psink
            outs.append(acc / den)
        for j in range(GROUP):
            o_ref[0, pl.ds(r, BLOCK), j * LANES:(j + 1) * LANES] = jnp.where(
                low, outs[2 * j], outs[2 * j + 1]).astype(BF16)
        return carry

    lax.fori_loop(0, nb, block, 0)


def _attention(sink, q, k, v, kc, vc, bias):
    per_b = lambda n, w: pl.BlockSpec((1, n, w), lambda b: (b, 0, 0))
    return pl.pallas_call(
        _attn_body,
        grid=(BATCH,),
        in_specs=[pl.BlockSpec(memory_space=pltpu.SMEM),
                  per_b(SEQ, ATTN_W), per_b(SEQ, KV_W), per_b(SEQ, KV_W), per_b(CTX_LEN, KV_W), per_b(CTX_LEN, KV_W),
                  pl.BlockSpec((3, BLOCK, 3 * BLOCK), lambda b: (0, 0, 0))],
        out_specs=per_b(SEQ, ATTN_W),
        out_shape=jax.ShapeDtypeStruct((BATCH, SEQ, ATTN_W), BF16),
        scratch_shapes=[pltpu.VMEM((SEQ + 2 * BLOCK, KV_W), BF16)] * 3,
        compiler_params=_params(("parallel",)),
        name="window_attn",
    )(sink, q, k, v, kc, vc, bias)


def _merge_body(ya_ref, yb_ref, gt_ref, x_ref, g1_ref, sc2_ref, sh2_ref, n2_ref, wa_ref, wb_ref, wo_ref, rh_ref,
                rl_ref, xn_ref, h2_ref, aff_ref):
    d = D_MODEL
    groups = [slice(i * SUB_MERGE, (i + 1) * SUB_MERGE) for i in range(TM_MERGE // SUB_MERGE)]
    branch = [(_dot(ya_ref[0, s, :], wa_ref[...]), _dot(yb_ref[0, s, :], wb_ref[...])) for s in groups]
    mixed = [(gt_ref[0, s, :d].astype(F32) * ua + gt_ref[0, s, d:].astype(F32) * ub).astype(BF16)
             for s, (ua, ub) in zip(groups, branch)]
    proj = [_dot(u, wo_ref[...]) for u in mixed]
    h2_parts = []
    for s, m in zip(groups, proj):
        xn = x_ref[0, s, :] + g1_ref[0] * m
        xn_ref[0, s, :] = xn
        h2 = _rms_mod(xn, n2_ref[...], sc2_ref[0], sh2_ref[0])
        h2_hi = h2.astype(BF16)
        h2_ref[0, s, :] = h2_hi
        h2_parts.append((h2_hi, (h2 - h2_hi.astype(F32)).astype(BF16)))
    rh = rh_ref[...]
    logits = [_dot_nt(rh, hi) + _dot_nt(rh, lo) + _dot_nt(rl_ref[...], hi) for hi, lo in h2_parts]
    for s, lt in zip(groups, logits):
        e = jnp.exp(lt - jnp.max(lt, axis=0, keepdims=True))
        aff_ref[0, :, s] = e / jnp.sum(e, axis=0, keepdims=True)


def _merge(ya, yb, gates, x, mods3, norm2, wa, wb, wo, router_hi, router_lo):
    d = D_MODEL
    nt = SEQ // TM_MERGE
    const = lambda shape: pl.BlockSpec(shape, lambda b, i: (0,) * len(shape))
    tok = lambda w: pl.BlockSpec((1, TM_MERGE, w), lambda b, i: (b, i, 0))
    mod = lambda k: pl.BlockSpec((1, 1, d), lambda b, i: (b, 0, k))
    return pl.pallas_call(
        _merge_body,
        grid=(BATCH, nt),
        in_specs=[tok(HYENA_W), tok(ATTN_W), tok(2 * d), tok(d), mod(2), mod(4), mod(3), const((1, d)),
                  const((HYENA_W, d)), const((ATTN_W, d)), const((d, d)), const((N_EXPERTS, d)), const((N_EXPERTS, d))],
        out_specs=[tok(d), tok(d), pl.BlockSpec((1, N_EXPERTS, TM_MERGE), lambda b, i: (b, 0, i))],
        out_shape=[jax.ShapeDtypeStruct((BATCH, SEQ, d), F32),
                   jax.ShapeDtypeStruct((BATCH, SEQ, d), BF16),
                   jax.ShapeDtypeStruct((BATCH, N_EXPERTS, SEQ), F32)],
        compiler_params=_params(("parallel", "parallel")),
        name="merge_outproj",
    )(ya, yb, gates, x, mods3, mods3, mods3, norm2, wa, wb, wo, router_hi, router_lo)


def _thresh_body(aff_ref, lo_ref, hi_ref):
    aff = aff_ref[...]
    rows = aff.shape[0]
    bits = pltpu.bitcast(aff, I32)

    def bisect_bits(_, carry):
        lo, hi = carry
        mid = lo + ((hi - lo) >> 1)
        ge = jnp.sum((bits >= mid).astype(I32), axis=1, keepdims=True) >= CAP
        return jnp.where(ge, mid, lo), jnp.where(ge, hi, mid)

    lo0 = jnp.zeros((rows, 1), I32)
    hi0 = jnp.full((rows, 1), 0x3F800001, I32)
    thr_bits, _ = lax.fori_loop(0, 31, bisect_bits, (lo0, hi0))
    thr = pltpu.bitcast(thr_bits, F32)

    def bisect_val(_, carry):
        lo, hi = carry
        mid = 0.5 * (lo + hi)
        ge = jnp.sum(jnp.where(aff >= mid, 1.0, 0.0), axis=1, keepdims=True) >= CAP
        return jnp.where(ge, mid, lo), jnp.where(ge, hi, mid)

    lo, hi = lax.fori_loop(0, 30, bisect_val, (0.5 * thr, jnp.maximum(2.0 * thr, 1e-30)))
    lo_ref[...] = jnp.broadcast_to(lo, lo_ref.shape)
    hi_ref[...] = jnp.broadcast_to(hi, hi_ref.shape)


def _thresholds(aff_rows):
    rows = BATCH * N_EXPERTS
    return pl.pallas_call(
        _thresh_body,
        grid=(1,),
        in_specs=[pl.BlockSpec((rows, SEQ), lambda i: (0, 0))],
        out_specs=[pl.BlockSpec((rows, LANES), lambda i: (0, 0)), pl.BlockSpec((rows, LANES), lambda i: (0, 0))],
        out_shape=[jax.ShapeDtypeStruct((rows, LANES), F32), jax.ShapeDtypeStruct((rows, LANES), F32)],
        compiler_params=_params(("arbitrary",)),
        name="route_threshold",
    )(aff_rows)


def _prefix_counts(mask):
    r = lax.broadcasted_iota(I32, (LANES, LANES), 0)
    c = lax.broadcasted_iota(I32, (LANES, LANES), 1)
    upper = jnp.where(r <= c, 1.0, 0.0).astype(BF16)
    offset = jnp.zeros((mask.shape[0], 1), F32)
    blocks = []
    for j in range(mask.shape[1] // LANES):
        blk = mask[:, j * LANES:(j + 1) * LANES]
        inc = _dot(blk.astype(BF16), upper)
        blocks.append(inc - blk + offset)
        offset = offset + inc[:, LANES - 1:LANES]
    return jnp.concatenate(blocks, axis=1)


def _route_body(aff_ref, lo_ref, hi_ref, h_ref, xin_ref, g_ref, pos_ref, pbuf_ref):
    aff = aff_ref[0]
    above = jnp.where(aff >= hi_ref[:, 0:1], 1.0, 0.0)
    band = jnp.where(aff >= lo_ref[:, 0:1], 1.0, 0.0) - above
    need = CAP - jnp.sum(above, axis=1, keepdims=True)
    tie_rank = _prefix_counts(band)
    self = above + band * jnp.where(tie_rank < need, 1.0, 0.0)
    pos = _prefix_counts(self)
    posi = jnp.where(self > 0.5, pos.astype(I32), -1)
    pos_ref[0] = posi

    h = h_ref[0]
    slot = lax.broadcasted_iota(I32, (CAP, SEQ), 0)
    for grp in range(N_EXPERTS // EXPERT_GROUP):
        for i in range(EXPERT_GROUP):
            e = grp * EXPERT_GROUP + i
            hit = posi[e:e + 1, :] == slot
            pbuf_ref[i * CAP:(i + 1) * CAP, :] = jnp.where(hit, 1.0, 0.0).astype(BF16)
            g_ref[e] = jnp.sum(jnp.where(hit, aff[e:e + 1, :], 0.0), axis=1, keepdims=True)
        rows = _dot(pbuf_ref[...], h)
        for i in range(EXPERT_GROUP):
            xin_ref[grp * EXPERT_GROUP + i] = rows[i * CAP:(i + 1) * CAP].astype(BF16)


def _route(aff_t, lo, hi, h2):
    d = D_MODEL
    return pl.pallas_call(
        _route_body,
        grid=(BATCH,),
        in_specs=[pl.BlockSpec((1, N_EXPERTS, SEQ), lambda b: (b, 0, 0)),
                  pl.BlockSpec((N_EXPERTS, LANES), lambda b: (b, 0)),
                  pl.BlockSpec((N_EXPERTS, LANES), lambda b: (b, 0)),
                  pl.BlockSpec((1, SEQ, d), lambda b: (b, 0, 0))],
        out_specs=[pl.BlockSpec((N_EXPERTS, CAP, d), lambda b: (0, b, 0)),
                   pl.BlockSpec((N_EXPERTS, CAP, 1), lambda b: (0, b, 0)),
                   pl.BlockSpec((1, N_EXPERTS, SEQ), lambda b: (b, 0, 0))],
        out_shape=[jax.ShapeDtypeStruct((N_EXPERTS, BATCH * CAP, d), BF16),
                   jax.ShapeDtypeStruct((N_EXPERTS, BATCH * CAP, 1), F32),
                   jax.ShapeDtypeStruct((BATCH, N_EXPERTS, SEQ), I32)],
        scratch_shapes=[pltpu.VMEM((EXPERT_GROUP * CAP, SEQ), BF16)],
        compiler_params=_params(("parallel",)),
        name="route_gather",
    )(aff_t, lo, hi, h2)


def _expert_body(x_ref, g_ref, wg_ref, wu_ref, wd_ref, o_ref, acc_ref, wgb_ref, wub_ref, wdb_ref):
    f = pl.program_id(1)
    wgb_ref[...] = wg_ref[0].astype(BF16)
    wub_ref[...] = wu_ref[0].astype(BF16)
    wdb_ref[...] = wd_ref[0].astype(BF16)

    @pl.when((pl.program_id(0) == 0) & (f == 0))
    def _():
        acc_ref[...] = jnp.zeros_like(acc_ref)

    carry = f > 0
    for m in range(BATCH * CAP // MC):
        rows = slice(m * MC, (m + 1) * MC)
        xm = x_ref[0, rows, :]
        a = _dot(xm, wgb_ref[...])
        b = _dot(xm, wub_ref[...])
        hh = (a * jax.nn.sigmoid(a) * b).astype(BF16)
        acc_ref[rows, :] = jnp.where(carry, acc_ref[rows, :], 0.0) + _dot(hh, wdb_ref[...])

    @pl.when(f == D_FF // TF - 1)
    def _():
        o_ref[0] = (acc_ref[...] * g_ref[0]).astype(BF16)


def _experts(xin, g, w_gate, w_up, w_down):
    d = D_MODEL
    rows = BATCH * CAP
    return pl.pallas_call(
        _expert_body,
        grid=(N_EXPERTS, D_FF // TF),
        in_specs=[pl.BlockSpec((1, rows, d), lambda e, f: (e, 0, 0)),
                  pl.BlockSpec((1, rows, 1), lambda e, f: (e, 0, 0)),
                  pl.BlockSpec((1, d, TF), lambda e, f: (e, 0, f)),
                  pl.BlockSpec((1, d, TF), lambda e, f: (e, 0, f)),
                  pl.BlockSpec((1, TF, d), lambda e, f: (e, f, 0))],
        out_specs=pl.BlockSpec((1, rows, d), lambda e, f: (e, 0, 0)),
        out_shape=jax.ShapeDtypeStruct((N_EXPERTS, rows, d), BF16),
        scratch_shapes=[pltpu.VMEM((rows, d), F32), pltpu.VMEM((d, TF), BF16), pltpu.VMEM((d, TF), BF16),
                        pltpu.VMEM((TF, d), BF16)],
        compiler_params=_params(("parallel", "arbitrary")),
        name="swiglu_experts",
    )(xin, g, w_gate, w_up, w_down)


def _scatter_body(pos_ref, y_ref, x_ref, g2_ref, o_ref):
    pos = pos_ref[0]
    slot = lax.broadcasted_iota(I32, (TS, CAP), 1)
    onehot = jnp.concatenate([jnp.where(pos[:, e:e + 1] == slot, 1.0, 0.0).astype(BF16) for e in range(N_EXPERTS)],
                             axis=1)
    y = y_ref[...].reshape(N_EXPERTS * CAP, D_MODEL)
    o_ref[0] = x_ref[0] + g2_ref[0] * _dot(onehot, y)


def _scatter(pos_tok, y, xn, mods3):
    d = D_MODEL
    return pl.pallas_call(
        _scatter_body,
        grid=(BATCH, SEQ // TS),
        in_specs=[pl.BlockSpec((1, TS, N_EXPERTS), lambda b, i: (b, i, 0)),
                  pl.BlockSpec((N_EXPERTS, CAP, d), lambda b, i: (0, b, 0)),
                  pl.BlockSpec((1, TS, d), lambda b, i: (b, i, 0)),
                  pl.BlockSpec((1, 1, d), lambda b, i: (b, 0, 5))],
        out_specs=pl.BlockSpec((1, TS, d), lambda b, i: (b, i, 0)),
        out_shape=jax.ShapeDtypeStruct((BATCH, SEQ, d), F32),
        compiler_params=_params(("parallel", "parallel")),
        name="scatter_residual",
    )(pos_tok, y, xn, mods3)


def _rope_tables():
    rows = SEQ // GRID_W
    row = np.repeat(np.arange(rows, dtype=np.float32), GRID_W)
    col = np.tile(np.arange(GRID_W, dtype=np.float32), rows)
    inv = (ROPE_BASE ** (-np.arange(0, AXIS_ROT, 2, dtype=np.float32) / AXIS_ROT)).astype(np.float32)
    ang = np.concatenate([row[:, None] * inv, col[:, None] * inv], axis=-1).astype(np.float64)
    cos = np.repeat(np.cos(ang), 2, axis=-1)
    sin = np.stack([-np.sin(ang), np.sin(ang)], axis=-1).reshape(SEQ, HEAD_DIM)
    reps = LANES // HEAD_DIM
    return jnp.asarray(np.tile(cos, (1, reps)), F32), jnp.asarray(np.tile(sin, (1, reps)), F32)


def _cos_sin_outer(f, mults):
    unit = math.pi / N_FFT
    step = 32
    coarse = 2 * step * jnp.arange(mults.shape[0] // step, dtype=I32)
    fine = mults[:step]
    a = ((f[:, None] * coarse[None, :]) % (2 * N_FFT)).astype(F32) * unit
    b = ((f[:, None] * fine[None, :]) % (2 * N_FFT)).astype(F32) * unit
    ca, sa, cb, sb = jnp.cos(a)[:, :, None], jnp.sin(a)[:, :, None], jnp.cos(b)[:, None, :], jnp.sin(b)[:, None, :]
    n = f.shape[0]
    return (ca * cb - sa * sb).reshape(n, -1), (sa * cb + ca * sb).reshape(n, -1)


def _dft_tables():
    idx = jnp.arange(HALF, dtype=I32)
    t2p1 = 2 * idx + 1
    ce, se = _cos_sin_outer(2 * idx, t2p1)
    co, so = _cos_sin_outer(2 * idx + 1, t2p1)
    t1 = jnp.concatenate([ce, so], axis=0).astype(BF16)
    t2 = jnp.concatenate([co, se], axis=0).astype(BF16)
    return t1, t2, t1.T, t2.T


def _phase_tables():
    idx = np.arange(HALF, dtype=np.float64)
    w = np.full((HALF,), 2.0 / N_FFT)
    we = w.copy()
    we[0] = 1.0 / N_FFT
    pe = (math.pi / N_FFT) * (2.0 * idx)
    po = (math.pi / N_FFT) * (2.0 * idx + 1.0)
    rot = np.stack([we * np.cos(pe), we * np.sin(pe), w * np.cos(po), w * np.sin(po)])
    return jnp.asarray(np.broadcast_to(rot[:, :, None], (4, HALF, CT)), F32)


def _fold_rows(a):
    return np.concatenate([a[:HALF], a[HALF:][::-1]], axis=0)


def _filter_features():
    t = np.linspace(0.0, 1.0, SEQ, dtype=np.float32).astype(np.float64)[:, None]
    w = 2.0 * math.pi * np.arange(SEQ, dtype=np.float64)[:, None] / SEQ
    fr = np.linspace(1e-4, FILTER_BANDS - 1, FILTER_BANDS, dtype=np.float32).astype(np.float64)[None, :]
    feat = np.concatenate([t, np.cos(fr * w), -np.sin(fr * w)], axis=-1)
    feat = np.pad(feat, ((0, 0), (0, FILTER_HIDDEN - FILTER_EMB)))
    min_decay = math.log(DECAY_TARGET) / SLOW_DECAY_PCT
    max_decay = math.log(DECAY_TARGET) / FAST_DECAY_PCT
    deltas = np.linspace(min_decay, max_decay, HYENA_W, dtype=np.float32).astype(np.float64)
    decay = np.exp(-t * np.abs(deltas))
    return jnp.asarray(_fold_rows(feat).T, F32), jnp.asarray(_fold_rows(decay), F32)


def _attn_bias():
    qi = np.arange(BLOCK)[:, None]
    kj = np.arange(3 * BLOCK)[None, :]
    band = np.abs(kj - BLOCK - qi) <= WINDOW
    first = band & (kj >= BLOCK)
    last = band & (kj < 2 * BLOCK)
    return jnp.asarray(np.where(np.stack([first, band, last]), 0.0, NEG), F32)


def _pair_heads(w, axis):
    heads = [lax.slice_in_dim(w, h * HEAD_DIM, (h + 1) * HEAD_DIM, axis=axis) for h in range(N_HEADS)]
    return jnp.concatenate([heads[j + GROUP * half] for j in range(GROUP) for half in range(N_KV_HEADS)], axis=axis)


def kernel(x, c, ctx, c_ctx, ada_w, ada_b, norm1, norm2, w_in, conv_w, conv_b, filt_w1, filt_b1, filt_w2, filt_b2,
           filt_w3, filt_b3, filt_freq, filt_out, hyena_bias, q_norm, k_norm, attn_sink, w_branch_a, w_branch_b,
           w_out, router, w_gate, w_up, w_down):
    d = D_MODEL
    assert ada_w.shape[0] == 1, "only the single-layer configuration is implemented"
    l = 0
    cos_t, sin_t = _rope_tables()
    t1, t2, t1t, t2t = _dft_tables()
    rot = _phase_tables()
    feat, decay = _filter_features()
    bias = _attn_bias()
    gmat = jnp.asarray(np.kron(np.eye(2 * LANES // HEAD_DIM), np.full((HEAD_DIM, HEAD_DIM), 1.0 / HEAD_DIM)), BF16)
    c16 = jnp.concatenate([c, c_ctx[None, :], jnp.zeros((MOD_ROWS - BATCH - 1, d), F32)], axis=0)

    mods3 = _ada(c16, ada_w[l], ada_b[l][None, :]).reshape(MOD_ROWS, 1, 6 * d)
    n1 = norm1[l][None, :]
    w_packed = _pack_in_weights(w_in[l])
    gk = jnp.tile(k_norm[l], N_KV_HEADS)[None, :]
    kf, kn = _filters(feat, jnp.pad(filt_w1[l].T, ((0, 0), (0, FILTER_HIDDEN - FILTER_EMB))), filt_b1[l][:, None],
                      filt_w2[l].T, filt_b2[l][:, None], filt_w3[l].T, filt_b3[l][:, None], filt_freq[l][:, None],
                      filt_out[l], decay, rot, t1, t2)
    kc, vc = _ctx_proj(ctx, mods3, n1, w_packed, gk, gmat)
    zh, q, k, v, gates = _inproj(x, mods3, n1, w_packed, jnp.tile(q_norm[l], N_HEADS)[None, :], gk, gmat, cos_t, sin_t)
    ya = _hyena(zh, conv_w[l], conv_b[l][None, :], hyena_bias[l], kf, kn, t1, t2, t1t, t2t)
    yb = _attention(attn_sink[l] * LOG2E, q, k, v, kc, vc, bias)
    router_t = router[l].T
    router_hi = router_t.astype(BF16)
    router_lo = (router_t - router_hi.astype(F32)).astype(BF16)
    xn, h2, aff_t = _merge(ya, yb, gates, x, mods3, norm2[l][None, :], w_branch_a[l].astype(BF16),
                           _pair_heads(w_branch_b[l], 0).astype(BF16), w_out[l].astype(BF16), router_hi, router_lo)
    lo, hi = _thresholds(aff_t.reshape(BATCH * N_EXPERTS, SEQ))
    xin, g, pos = _route(aff_t, lo, hi, h2)
    y = _experts(xin, g, w_gate[l], w_up[l], w_down[l])
    return _scatter(jnp.swapaxes(pos, 1, 2), y, xn, mods3)
```

```python
import math

import numpy as np
import jax
import jax.numpy as jnp
from jax import lax
from jax.experimental import pallas as pl
from jax.experimental.pallas import tpu as pltpu

F32 = jnp.float32
BF16 = jnp.bfloat16
I32 = jnp.int32
HIGHEST = lax.Precision.HIGHEST

D_MODEL = 1024
BATCH = 8
SEQ = 2048
GRID_W = 64
CTX_LEN = 256
N_HEADS = 8
N_KV_HEADS = 2
HEAD_DIM = 64
GROUP = N_HEADS // N_KV_HEADS
ATTN_W = N_HEADS * HEAD_DIM
KV_W = N_KV_HEADS * HEAD_DIM
WINDOW = 128
BLOCK = 128
HYENA_W = D_MODEL // 2
HYENA_ORDER = 2
FILTER_BANDS = 16
FILTER_EMB = 1 + 2 * FILTER_BANDS
FILTER_HIDDEN = 64
DECAY_TARGET = 1e-2
FAST_DECAY_PCT = 0.3
SLOW_DECAY_PCT = 1.5
ROPE_BASE = 10000.0
AXIS_ROT = HEAD_DIM // 2
N_EXPERTS = 16
EC_CAPACITY = 2
D_FF = 2048
EPS = 1e-6
NEG = -1e30
LOG2E = math.log2(math.e)

OFF_Q = 3 * HYENA_W
OFF_K = OFF_Q + ATTN_W
OFF_V = OFF_K + KV_W
OFF_G = OFF_V + KV_W
IN_W = OFF_G + 2 * D_MODEL

CAP = EC_CAPACITY * SEQ // N_EXPERTS
N_FFT = 2 * SEQ
HALF = SEQ // 2
MOD_ROWS = 16
LANES = 128

TM_IN = 1024
SUB_IN = 512
TM_MERGE = 1024
SUB_MERGE = 512
CT = 256
FC = 512
RB = 256
TF = 512
MC = 512
TS = 512
EXPERT_GROUP = 4
VMEM_LIMIT = 56 * 1024 * 1024


def _dot(a, b, precision=None):
    return jnp.dot(a, b, preferred_element_type=F32, precision=precision)


def _dot_nt(a, b, precision=None):
    return lax.dot_general(a, b, (((1,), (1,)), ((), ())), preferred_element_type=F32, precision=precision)


def _params(sem, vmem=VMEM_LIMIT):
    return pltpu.CompilerParams(dimension_semantics=sem, vmem_limit_bytes=vmem)


def _rms_mod(x, g, sc, sh):
    ms = jnp.mean(x * x, axis=-1, keepdims=True)
    return (x * lax.rsqrt(ms + EPS) * g) * (1.0 + sc) + sh


def _head_norm_rope(z, g, gmat, cos, sin, scale):
    ms = _dot((z * z).astype(BF16), gmat)
    y = z * lax.rsqrt(ms + EPS) * g
    if cos is not None:
        slabs = []
        for s in range(z.shape[1] // LANES):
            ys = y[:, s * LANES:(s + 1) * LANES]
            lane = lax.broadcasted_iota(I32, ys.shape, 1)
            nxt = pltpu.roll(ys, LANES - 1, axis=1)
            prv = pltpu.roll(ys, 1, axis=1)
            slabs.append(ys * cos + jnp.where((lane & 1) == 0, nxt, prv) * sin)
        y = slabs[0] if len(slabs) == 1 else jnp.concatenate(slabs, axis=1)
    return y * scale


def _split_bf16(x):
    hi = x.astype(BF16)
    return hi, (x - hi.astype(F32)).astype(BF16)


def _ada_body(c_ref, w_ref, b_ref, o_ref):
    c = c_ref[...]
    s_hi, s_lo = _split_bf16(c * jax.nn.sigmoid(c))
    w_hi, w_lo = _split_bf16(w_ref[...])
    o_ref[...] = _dot(s_hi, w_hi) + _dot(s_lo, w_hi) + _dot(s_hi, w_lo) + b_ref[...]


def _ada(c16, w, b):
    d = D_MODEL
    return pl.pallas_call(
        _ada_body,
        grid=(6,),
        in_specs=[pl.BlockSpec((MOD_ROWS, d), lambda j: (0, 0)),
                  pl.BlockSpec((d, d), lambda j: (0, j)),
                  pl.BlockSpec((1, d), lambda j: (0, j))],
        out_specs=pl.BlockSpec((MOD_ROWS, d), lambda j: (0, j)),
        out_shape=jax.ShapeDtypeStruct((MOD_ROWS, 6 * d), F32),
        compiler_params=_params(("parallel",)),
        name="ada_mod",
    )(c16, w, b)


def _sign_rows(n):
    lane = lax.broadcasted_iota(I32, (8, n), 1)
    sub = lax.broadcasted_iota(I32, (8, n), 0)
    sg = jnp.where((lane & 1) == 0, 1.0, -1.0)
    return jnp.where(sub == 0, sg, 0.0).astype(BF16)


def _filt_body(feat_ref, w1_ref, b1_ref, w2_ref, b2_ref, w3_ref, b3_ref, fq_ref, fof_ref, fob_ref, dec_ref,
               rot_ref, t1_ref, t2_ref, kf_ref, kn_ref, hh_ref, hl_ref):
    @pl.when((pl.program_id(0) == 0) & (pl.program_id(1) == 0))
    def _():
        fq = fq_ref[...]
        h = jnp.sin(fq * (_dot(w1_ref[...], feat_ref[...], HIGHEST) + b1_ref[...]))
        h = jnp.sin(fq * (_dot(w2_ref[...], h, HIGHEST) + b2_ref[...]))
        h = jnp.sin(fq * (_dot(w3_ref[...], h, HIGHEST) + b3_ref[...]))
        hh_ref[...], hl_ref[...] = _split_bf16(h.T)

    def taps(fo_ref):
        f_hi, f_lo = _split_bf16(fo_ref[...])
        return _dot(hh_ref[...], f_hi) + _dot(hl_ref[...], f_hi) + _dot(hh_ref[...], f_lo)

    dec = dec_ref[...]
    hf = taps(fof_ref) * dec
    hb = taps(fob_ref) * dec
    row = lax.broadcasted_iota(I32, hf.shape, 0)
    hb = jnp.where(row == 0, 0.0, hb)
    a = hf + hb
    b = hf - hb
    pa = (a[:HALF] + a[HALF:]).astype(BF16)
    ma = (a[:HALF] - a[HALF:]).astype(BF16)
    pb = (b[:HALF] + b[HALF:]).astype(BF16)
    mb = (b[:HALF] - b[HALF:]).astype(BF16)
    t1 = t1_ref[...]
    t2 = t2_ref[...]
    a1 = _dot(t1, pa)
    a2 = _dot(t2, ma)
    b1 = _dot(t1, pb)
    b2 = _dot(t2, mb)
    ce, se, co, so = rot_ref[0], rot_ref[1], rot_ref[2], rot_ref[3]
    kf_ref[0, 0] = a1[:HALF] * ce + a2[HALF:] * se
    kf_ref[0, 1] = b2[HALF:] * ce - b1[:HALF] * se
    kf_ref[0, 2] = a2[:HALF] * co + a1[HALF:] * so
    kf_ref[0, 3] = b1[HALF:] * co - b2[:HALF] * so
    kn_ref[0] = _dot(_sign_rows(HALF), ma)[0:1] * (1.0 / N_FFT)


def _filters(feat, w1, b1, w2, b2, w3, b3, fq, fout, decay, rot, t1, t2):
    nct = HYENA_W // CT
    full = lambda shape: pl.BlockSpec(shape, lambda o, c: (0,) * len(shape))
    return pl.pallas_call(
        _filt_body,
        grid=(HYENA_ORDER, nct),
        in_specs=[full((FILTER_HIDDEN, SEQ)), full((FILTER_HIDDEN, FILTER_HIDDEN)), full((FILTER_HIDDEN, 1)),
                  full((FILTER_HIDDEN, FILTER_HIDDEN)), full((FILTER_HIDDEN, 1)),
                  full((FILTER_HIDDEN, FILTER_HIDDEN)), full((FILTER_HIDDEN, 1)), full((FILTER_HIDDEN, 1)),
                  pl.BlockSpec((FILTER_HIDDEN, CT), lambda o, c: (0, (o * 2 + 0) * nct + c)),
                  pl.BlockSpec((FILTER_HIDDEN, CT), lambda o, c: (0, (o * 2 + 1) * nct + c)),
                  pl.BlockSpec((SEQ, CT), lambda o, c: (0, c)),
                  full((4, HALF, CT)),
                  pl.BlockSpec((SEQ, HALF), lambda o, c: (0, 0), pipeline_mode=pl.Buffered(1)),
                  pl.BlockSpec((SEQ, HALF), lambda o, c: (0, 0), pipeline_mode=pl.Buffered(1))],
        out_specs=[pl.BlockSpec((1, 4, HALF, CT), lambda o, c: (o, 0, 0, c)),
                   pl.BlockSpec((1, 1, CT), lambda o, c: (o, 0, c))],
        out_shape=[jax.ShapeDtypeStruct((HYENA_ORDER, 4, HALF, HYENA_W), F32),
                   jax.ShapeDtypeStruct((HYENA_ORDER, 1, HYENA_W), F32)],
        scratch_shapes=[pltpu.VMEM((SEQ, FILTER_HIDDEN), BF16)] * 2,
        compiler_params=_params(("arbitrary", "arbitrary")),
        name="hyena_filters",
    )(feat, w1, b1, w2, b2, w3, b3, fq, fout, fout, decay, rot, t1, t2)


def _inproj_body(x_ref, sc_ref, sh_ref, n1_ref, wh_ref, wq_ref, wkv_ref, wg_ref, gq_ref, gk_ref, gmat_ref,
                 cos_ref, sin_ref, zh_ref, q_ref, k_ref, v_ref, gate_ref):
    groups = [slice(i * SUB_IN, (i + 1) * SUB_IN) for i in range(TM_IN // SUB_IN)]
    hx = [_rms_mod(x_ref[0, s, :], n1_ref[...], sc_ref[0], sh_ref[0]).astype(BF16) for s in groups]
    for s, h in zip(groups, hx):
        zh_ref[0, s, :] = _dot(h, wh_ref[...]).astype(BF16)
    pair = 2 * LANES
    zq = [_dot(h, wq_ref[...]) for h in hx]
    for s, z in zip(groups, zq):
        for c in range(ATTN_W // pair):
            sl = slice(c * pair, (c + 1) * pair)
            q_ref[0, s, sl] = _head_norm_rope(z[:, sl], gq_ref[:, sl], gmat_ref[...], cos_ref[s, :], sin_ref[s, :],
                                              LOG2E * HEAD_DIM ** -0.5).astype(BF16)
    zkv = [_dot(h, wkv_ref[...]) for h in hx]
    for s, z in zip(groups, zkv):
        k_ref[0, s, :] = _head_norm_rope(z[:, :KV_W], gk_ref[...], gmat_ref[0:KV_W, 0:KV_W], cos_ref[s, :],
                                         sin_ref[s, :], 1.0).astype(BF16)
        v_ref[0, s, :] = z[:, KV_W:].astype(BF16)
    for s, h in zip(groups, hx):
        gate_ref[0, s, :] = jax.nn.sigmoid(_dot(h, wg_ref[...])).astype(BF16)


def _pack_in_weights(w):
    parts = [w[:, :OFF_Q], _pair_heads(w[:, OFF_Q:OFF_K], 1), w[:, OFF_G:], w[:, OFF_K:OFF_G]]
    return jnp.concatenate([p.astype(BF16) for p in parts], axis=1)


W_OFF_H = 0
W_OFF_Q = OFF_Q
W_OFF_G = OFF_Q + ATTN_W
W_OFF_KV = W_OFF_G + 2 * D_MODEL


def _inproj(x, mods3, norm1, w_packed, gq, gk, gmat, cos_t, sin_t):
    d = D_MODEL
    nt = SEQ // TM_IN
    const = lambda shape: pl.BlockSpec(shape, lambda b, i: (0,) * len(shape))

    def wcol(width, off):
        assert off % width == 0
        return pl.BlockSpec((d, width), lambda b, i: (0, off // width))

    tok = lambda w: pl.BlockSpec((1, TM_IN, w), lambda b, i: (b, i, 0))
    return pl.pallas_call(
        _inproj_body,
        grid=(BATCH, nt),
        in_specs=[tok(d),
                  pl.BlockSpec((1, 1, d), lambda b, i: (b, 0, 1)),
                  pl.BlockSpec((1, 1, d), lambda b, i: (b, 0, 0)),
                  const((1, d)), wcol(OFF_Q, W_OFF_H), wcol(ATTN_W, W_OFF_Q), wcol(2 * KV_W, W_OFF_KV),
                  wcol(2 * d, W_OFF_G), const((1, ATTN_W)), const((1, KV_W)), const((2 * LANES, 2 * LANES)),
                  pl.BlockSpec((TM_IN, LANES), lambda b, i: (i, 0)),
                  pl.BlockSpec((TM_IN, LANES), lambda b, i: (i, 0))],
        out_specs=[tok(OFF_Q), tok(ATTN_W), tok(KV_W), tok(KV_W), tok(2 * d)],
        out_shape=[jax.ShapeDtypeStruct((BATCH, SEQ, OFF_Q), BF16),
                   jax.ShapeDtypeStruct((BATCH, SEQ, ATTN_W), BF16),
                   jax.ShapeDtypeStruct((BATCH, SEQ, KV_W), BF16),
                   jax.ShapeDtypeStruct((BATCH, SEQ, KV_W), BF16),
                   jax.ShapeDtypeStruct((BATCH, SEQ, 2 * d), BF16)],
        compiler_params=_params(("parallel", "parallel")),
        name="in_proj",
    )(x, mods3, mods3, norm1, w_packed, w_packed, w_packed, w_packed, gq, gk, gmat, cos_t, sin_t)


def _ctx_body(c_ref, sc_ref, sh_ref, n1_ref, wkv_ref, gk_ref, gmat_ref, kc_ref, vc_ref):
    hc = _rms_mod(c_ref[0], n1_ref[...], sc_ref[0], sh_ref[0]).astype(BF16)
    z = _dot(hc, wkv_ref[...])
    kc_ref[0] = _head_norm_rope(z[:, :KV_W], gk_ref[...], gmat_ref[0:KV_W, 0:KV_W], None, None, 1.0).astype(BF16)
    vc_ref[0] = z[:, KV_W:].astype(BF16)


def _ctx_proj(ctx, mods3, norm1, w_packed, gk, gmat):
    d = D_MODEL
    const = lambda shape: pl.BlockSpec(shape, lambda b: (0,) * len(shape))
    return pl.pallas_call(
        _ctx_body,
        grid=(BATCH,),
        in_specs=[pl.BlockSpec((1, CTX_LEN, d), lambda b: (b, 0, 0)),
                  pl.BlockSpec((1, 1, d), lambda b: (BATCH, 0, 1)),
                  pl.BlockSpec((1, 1, d), lambda b: (BATCH, 0, 0)),
                  const((1, d)), pl.BlockSpec((d, 2 * KV_W), lambda b: (0, W_OFF_KV // (2 * KV_W))),
                  const((1, KV_W)), const((2 * LANES, 2 * LANES))],
        out_specs=[pl.BlockSpec((1, CTX_LEN, KV_W), lambda b: (b, 0, 0)),
                   pl.BlockSpec((1, CTX_LEN, KV_W), lambda b: (b, 0, 0))],
        out_shape=[jax.ShapeDtypeStruct((BATCH, CTX_LEN, KV_W), BF16),
                   jax.ShapeDtypeStruct((BATCH, CTX_LEN, KV_W), BF16)],
        compiler_params=_params(("parallel",)),
        name="ctx_proj",
    )(ctx, mods3, mods3, norm1, w_packed, gk, gmat)


def _hyena_body(zv_ref, z1_ref, z2_ref, cwv_ref, cw1_ref, cw2_ref, cbv_ref, cb1_ref, cb2_ref, hb_ref, kf_ref,
                kn_ref, t1_ref, t2_ref, t1t_ref, t2t_ref, o_ref, lo_ref, hi_ref, glo_ref, ghi_ref, p_ref, m_ref,
                za_ref, zb_ref):
    row = lax.broadcasted_iota(I32, (HALF, CT), 0)
    rr = lax.broadcasted_iota(I32, (RB, RB), 0)
    cc = lax.broadcasted_iota(I32, (RB, RB), 1)
    flip = jnp.where(rr + cc == RB - 1, 1.0, 0.0).astype(BF16)
    nrb = HALF // RB

    def folded_short_conv(z_ref, w_ref, b_ref, lo_out, hi_out):
        zlo = z_ref[0, 0:HALF, :].astype(F32)
        for j in range(nrb):
            hi_out[j * RB:(j + 1) * RB, :] = _dot(flip, z_ref[0, SEQ - RB * (j + 1):SEQ - RB * j, :])
        zhi = hi_out[...]
        w0, w1, w2 = w_ref[0:1, :], w_ref[1:2, :], w_ref[2:3, :]
        first, last = row == 0, row == HALF - 1
        lo_prev = jnp.where(first, 0.0, pltpu.roll(zlo, 1, axis=0))
        lo_next = jnp.where(last, zhi[HALF - 1:HALF, :], pltpu.roll(zlo, HALF - 1, axis=0))
        hi_prev = jnp.where(first, 0.0, pltpu.roll(zhi, 1, axis=0))
        hi_next = jnp.where(last, zlo[HALF - 1:HALF, :], pltpu.roll(zhi, HALF - 1, axis=0))
        lo_out[...] = lo_prev * w0 + zlo * w1 + lo_next * w2 + b_ref[...]
        hi_out[...] = hi_next * w0 + zhi * w1 + hi_prev * w2 + b_ref[...]

    folded_short_conv(zv_ref, cwv_ref, cbv_ref, lo_ref, hi_ref)
    sign8 = _sign_rows(HALF)
    odd = (lax.broadcasted_iota(I32, (FC, CT), 0) & 1) == 1
    for o, (zr, cw, cb) in enumerate(((z1_ref, cw1_ref, cb1_ref), (z2_ref, cw2_ref, cb2_ref))):
        folded_short_conv(zr, cw, cb, glo_ref, ghi_ref)
        p_ref[...] = (lo_ref[...] + hi_ref[...]).astype(BF16)
        m_ref[...] = (lo_ref[...] - hi_ref[...]).astype(BF16)
        pv = p_ref[...]
        mv = m_ref[...]
        for c in range(HALF // FC):
            ev = slice(c * FC, (c + 1) * FC)
            od = slice(HALF + c * FC, HALF + (c + 1) * FC)
            xce = _dot(t1_ref[ev, :], pv)
            xso = _dot(t1_ref[od, :], pv)
            xco = _dot(t2_ref[ev, :], mv)
            xse = _dot(t2_ref[od, :], mv)
            kce, kse, kco, kso = kf_ref[o, 0, ev, :], kf_ref[o, 1, ev, :], kf_ref[o, 2, ev, :], kf_ref[o, 3, ev, :]
            za_ref[ev, :] = (xce * kce - xse * kse).astype(BF16)
            za_ref[od, :] = (xco * kso + xso * kco).astype(BF16)
            zb_ref[ev, :] = (xco * kco - xso * kso).astype(BF16)
            zb_ref[od, :] = (xce * kse + xse * kce).astype(BF16)
        zn = _dot(sign8, mv)[0:1] * kn_ref[o]
        bias = hb_ref[o:o + 1, :]
        za = za_ref[...]
        zb = zb_ref[...]
        for c in range(HALF // FC):
            rs = slice(c * FC, (c + 1) * FC)
            half_p = _dot(t1t_ref[rs, :], za)
            half_m = _dot(t2t_ref[rs, :], zb) + jnp.where(odd, -zn, zn)
            lo_ref[rs, :] = glo_ref[rs, :] * (half_p + half_m + bias * lo_ref[rs, :])
            hi_ref[rs, :] = ghi_ref[rs, :] * (half_p - half_m + bias * hi_ref[rs, :])
    o_ref[0, 0:HALF, :] = lo_ref[...].astype(BF16)
    for j in range(nrb):
        o_ref[0, SEQ - RB * (j + 1):SEQ - RB * j, :] = _dot(
            flip, hi_ref[j * RB:(j + 1) * RB, :].astype(BF16)).astype(BF16)


def _hyena(zh, conv_w, conv_b, hbias, kf, kn, t1, t2, t1t, t2t):
    nct = HYENA_W // CT
    zspec = lambda k: pl.BlockSpec((1, SEQ, CT), lambda c, b: (b, 0, k * nct + c))
    wspec = lambda k: pl.BlockSpec((3, CT), lambda c, b: (0, k * nct + c))
    bspec = lambda k: pl.BlockSpec((1, CT), lambda c, b: (0, k * nct + c))
    table = lambda shape: pl.BlockSpec(shape, lambda c, b: (0, 0), pipeline_mode=pl.Buffered(1))
    half_f32 = pltpu.VMEM((HALF, CT), F32)
    return pl.pallas_call(
        _hyena_body,
        grid=(nct, BATCH),
        in_specs=[zspec(0), zspec(1), zspec(2), wspec(0), wspec(1), wspec(2), bspec(0), bspec(1), bspec(2),
                  pl.BlockSpec((HYENA_ORDER, CT), lambda c, b: (0, c)),
                  pl.BlockSpec((HYENA_ORDER, 4, HALF, CT), lambda c, b: (0, 0, 0, c)),
                  pl.BlockSpec((HYENA_ORDER, 1, CT), lambda c, b: (0, 0, c)),
                  table((SEQ, HALF)), table((SEQ, HALF)), table((HALF, SEQ)), table((HALF, SEQ))],
        out_specs=pl.BlockSpec((1, SEQ, CT), lambda c, b: (b, 0, c)),
        out_shape=jax.ShapeDtypeStruct((BATCH, SEQ, HYENA_W), BF16),
        scratch_shapes=[half_f32, half_f32, half_f32, half_f32,
                        pltpu.VMEM((HALF, CT), BF16), pltpu.VMEM((HALF, CT), BF16),
                        pltpu.VMEM((SEQ, CT), BF16), pltpu.VMEM((SEQ, CT), BF16)],
        compiler_params=_params(("parallel", "parallel")),
        name="hyena_conv",
    )(zh, zh, zh, conv_w, conv_w, conv_w, conv_b, conv_b, conv_b, hbias, kf, kn, t1, t2, t1t, t2t)


def _attn_body(sink_ref, q_ref, k_ref, v_ref, kc_ref, vc_ref, bias_ref, o_ref, kp_ref, vlo_ref, vhi_ref):
    nb = SEQ // BLOCK
    lane = lax.broadcasted_iota(I32, (BLOCK, LANES), 1)
    low = lane < HEAD_DIM
    mask_lo = jnp.where(low, 1.0, 0.0).astype(BF16)
    mask_hi = jnp.where(low, 0.0, 1.0).astype(BF16)

    def with_ones(v):
        lane_v = lax.broadcasted_iota(I32, v.shape, 1) < HEAD_DIM
        one = jnp.ones_like(v)
        return jnp.where(lane_v, v, one), jnp.where(lane_v, one, v)

    zpad = jnp.zeros((BLOCK, KV_W), BF16)
    kp_ref[0:BLOCK] = zpad
    kp_ref[BLOCK:BLOCK + SEQ] = k_ref[0]
    kp_ref[BLOCK + SEQ:] = zpad
    v_lo, v_hi = with_ones(v_ref[0])
    for ref, val in ((vlo_ref, v_lo), (vhi_ref, v_hi)):
        ref[0:BLOCK] = zpad
        ref[BLOCK:BLOCK + SEQ] = val
        ref[BLOCK + SEQ:] = zpad
    kc = kc_ref[0]
    vc_pair = with_ones(vc_ref[0])

    def block(n, carry):
        r = pl.multiple_of(n * BLOCK, BLOCK)
        kw = kp_ref[pl.ds(r, 3 * BLOCK), :]
        vw_pair = (vlo_ref[pl.ds(r, 3 * BLOCK), :], vhi_ref[pl.ds(r, 3 * BLOCK), :])
        bias = bias_ref[jnp.where(n == 0, 0, jnp.where(n == nb - 1, 2, 1))]
        scores = []
        for j in range(GROUP):
            qs = q_ref[0, pl.ds(r, BLOCK), j * LANES:(j + 1) * LANES]
            for msk in (mask_lo, mask_hi):
                qm = qs * msk
                scores.append((_dot_nt(qm, kw) + bias, _dot_nt(qm, kc)))
        probs = []
        for idx, (sw, sc) in enumerate(scores):
            snk = sink_ref[idx // 2 + GROUP * (idx % 2)]
            m = jnp.maximum(jnp.maximum(jnp.max(sw, axis=-1, keepdims=True),
                                        jnp.max(sc, axis=-1, keepdims=True)), snk)
            probs.append((jnp.exp2(sw - m).astype(BF16), jnp.exp2(sc - m).astype(BF16), jnp.exp2(snk - m)))
        outs = []
        for idx, (pw, pc, psink) in enumerate(probs):
            acc = _dot(pw, vw_pair[idx % 2]) + _dot(pc, vc_pair[idx % 2])
            den = pltpu.roll(acc, HEAD_DIM, axis=1) + psink
            outs.append(acc / den)
        for j in range(GROUP):
            o_ref[0, pl.ds(r, BLOCK), j * LANES:(j + 1) * LANES] = jnp.where(
                low, outs[2 * j], outs[2 * j + 1]).astype(BF16)
        return carry

    lax.fori_loop(0, nb, block, 0)


def _attention(sink, q, k, v, kc, vc, bias):
    per_b = lambda n, w: pl.BlockSpec((1, n, w), lambda b: (b, 0, 0))
    return pl.pallas_call(
        _attn_body,
        grid=(BATCH,),
        in_specs=[pl.BlockSpec(memory_space=pltpu.SMEM),
                  per_b(SEQ, ATTN_W), per_b(SEQ, KV_W), per_b(SEQ, KV_W), per_b(CTX_LEN, KV_W), per_b(CTX_LEN, KV_W),
                  pl.BlockSpec((3, BLOCK, 3 * BLOCK), lambda b: (0, 0, 0))],
        out_specs=per_b(SEQ, ATTN_W),
        out_shape=jax.ShapeDtypeStruct((BATCH, SEQ, ATTN_W), BF16),
        scratch_shapes=[pltpu.VMEM((SEQ + 2 * BLOCK, KV_W), BF16)] * 3,
        compiler_params=_params(("parallel",)),
        name="window_attn",
    )(sink, q, k, v, kc, vc, bias)


def _merge_body(ya_ref, yb_ref, gt_ref, x_ref, g1_ref, sc2_ref, sh2_ref, n2_ref, wa_ref, wb_ref, wo_ref, rh_ref,
                rl_ref, xn_ref, h2_ref, aff_ref):
    d = D_MODEL
    groups = [slice(i * SUB_MERGE, (i + 1) * SUB_MERGE) for i in range(TM_MERGE // SUB_MERGE)]
    branch = [(_dot(ya_ref[0, s, :], wa_ref[...]), _dot(yb_ref[0, s, :], wb_ref[...])) for s in groups]
    mixed = [(gt_ref[0, s, :d].astype(F32) * ua + gt_ref[0, s, d:].astype(F32) * ub).astype(BF16)
             for s, (ua, ub) in zip(groups, branch)]
    proj = [_dot(u, wo_ref[...]) for u in mixed]
    h2_parts = []
    for s, m in zip(groups, proj):
        xn = x_ref[0, s, :] + g1_ref[0] * m
        xn_ref[0, s, :] = xn
        h2 = _rms_mod(xn, n2_ref[...], sc2_ref[0], sh2_ref[0])
        h2_hi = h2.astype(BF16)
        h2_ref[0, s, :] = h2_hi
        h2_parts.append((h2_hi, (h2 - h2_hi.astype(F32)).astype(BF16)))
    rh = rh_ref[...]
    logits = [_dot_nt(rh, hi) + _dot_nt(rh, lo) + _dot_nt(rl_ref[...], hi) for hi, lo in h2_parts]
    for s, lt in zip(groups, logits):
        e = jnp.exp(lt - jnp.max(lt, axis=0, keepdims=True))
        aff_ref[0, :, s] = e / jnp.sum(e, axis=0, keepdims=True)


def _merge(ya, yb, gates, x, mods3, norm2, wa, wb, wo, router_hi, router_lo):
    d = D_MODEL
    nt = SEQ // TM_MERGE
    const = lambda shape: pl.BlockSpec(shape, lambda b, i: (0,) * len(shape))
    tok = lambda w: pl.BlockSpec((1, TM_MERGE, w), lambda b, i: (b, i, 0))
    mod = lambda k: pl.BlockSpec((1, 1, d), lambda b, i: (b, 0, k))
    return pl.pallas_call(
        _merge_body,
        grid=(BATCH, nt),
        in_specs=[tok(HYENA_W), tok(ATTN_W), tok(2 * d), tok(d), mod(2), mod(4), mod(3), const((1, d)),
                  const((HYENA_W, d)), const((ATTN_W, d)), const((d, d)), const((N_EXPERTS, d)), const((N_EXPERTS, d))],
        out_specs=[tok(d), tok(d), pl.BlockSpec((1, N_EXPERTS, TM_MERGE), lambda b, i: (b, 0, i))],
        out_shape=[jax.ShapeDtypeStruct((BATCH, SEQ, d), F32),
                   jax.ShapeDtypeStruct((BATCH, SEQ, d), BF16),
                   jax.ShapeDtypeStruct((BATCH, N_EXPERTS, SEQ), F32)],
        compiler_params=_params(("parallel", "parallel")),
        name="merge_outproj",
    )(ya, yb, gates, x, mods3, mods3, mods3, norm2, wa, wb, wo, router_hi, router_lo)


def _thresh_body(aff_ref, lo_ref, hi_ref):
    aff = aff_ref[...]
    rows = aff.shape[0]
    bits = pltpu.bitcast(aff, I32)

    def bisect_bits(_, carry):
        lo, hi = carry
        mid = lo + ((hi - lo) >> 1)
        ge = jnp.sum((bits >= mid).astype(I32), axis=1, keepdims=True) >= CAP
        return jnp.where(ge, mid, lo), jnp.where(ge, hi, mid)

    lo0 = jnp.zeros((rows, 1), I32)
    hi0 = jnp.full((rows, 1), 0x3F800001, I32)
    thr_bits, _ = lax.fori_loop(0, 31, bisect_bits, (lo0, hi0))
    thr = pltpu.bitcast(thr_bits, F32)

    def bisect_val(_, carry):
        lo, hi = carry
        mid = 0.5 * (lo + hi)
        ge = jnp.sum(jnp.where(aff >= mid, 1.0, 0.0), axis=1, keepdims=True) >= CAP
        return jnp.where(ge, mid, lo), jnp.where(ge, hi, mid)

    lo, hi = lax.fori_loop(0, 30, bisect_val, (0.5 * thr, jnp.maximum(2.0 * thr, 1e-30)))
    lo_ref[...] = jnp.broadcast_to(lo, lo_ref.shape)
    hi_ref[...] = jnp.broadcast_to(hi, hi_ref.shape)


def _thresholds(aff_rows):
    rows = BATCH * N_EXPERTS
    return pl.pallas_call(
        _thresh_body,
        grid=(1,),
        in_specs=[pl.BlockSpec((rows, SEQ), lambda i: (0, 0))],
        out_specs=[pl.BlockSpec((rows, LANES), lambda i: (0, 0)), pl.BlockSpec((rows, LANES), lambda i: (0, 0))],
        out_shape=[jax.ShapeDtypeStruct((rows, LANES), F32), jax.ShapeDtypeStruct((rows, LANES), F32)],
        compiler_params=_params(("arbitrary",)),
        name="route_threshold",
    )(aff_rows)


def _prefix_counts(mask):
    r = lax.broadcasted_iota(I32, (LANES, LANES), 0)
    c = lax.broadcasted_iota(I32, (LANES, LANES), 1)
    upper = jnp.where(r <= c, 1.0, 0.0).astype(BF16)
    offset = jnp.zeros((mask.shape[0], 1), F32)
    blocks = []
    for j in range(mask.shape[1] // LANES):
        blk = mask[:, j * LANES:(j + 1) * LANES]
        inc = _dot(blk.astype(BF16), upper)
        blocks.append(inc - blk + offset)
        offset = offset + inc[:, LANES - 1:LANES]
    return jnp.concatenate(blocks, axis=1)


def _route_body(aff_ref, lo_ref, hi_ref, h_ref, xin_ref, g_ref, pos_ref, pbuf_ref):
    aff = aff_ref[0]
    above = jnp.where(aff >= hi_ref[:, 0:1], 1.0, 0.0)
    band = jnp.where(aff >= lo_ref[:, 0:1], 1.0, 0.0) - above
    need = CAP - jnp.sum(above, axis=1, keepdims=True)
    tie_rank = _prefix_counts(band)
    self = above + band * jnp.where(tie_rank < need, 1.0, 0.0)
    pos = _prefix_counts(self)
    posi = jnp.where(self > 0.5, pos.astype(I32), -1)
    pos_ref[0] = posi

    h = h_ref[0]
    slot = lax.broadcasted_iota(I32, (CAP, SEQ), 0)
    for grp in range(N_EXPERTS // EXPERT_GROUP):
        for i in range(EXPERT_GROUP):
            e = grp * EXPERT_GROUP + i
            hit = posi[e:e + 1, :] == slot
            pbuf_ref[i * CAP:(i + 1) * CAP, :] = jnp.where(hit, 1.0, 0.0).astype(BF16)
            g_ref[e] = jnp.sum(jnp.where(hit, aff[e:e + 1, :], 0.0), axis=1, keepdims=True)
        rows = _dot(pbuf_ref[...], h)
        for i in range(EXPERT_GROUP):
            xin_ref[grp * EXPERT_GROUP + i] = rows[i * CAP:(i + 1) * CAP].astype(BF16)


def _route(aff_t, lo, hi, h2):
    d = D_MODEL
    return pl.pallas_call(
        _route_body,
        grid=(BATCH,),
        in_specs=[pl.BlockSpec((1, N_EXPERTS, SEQ), lambda b: (b, 0, 0)),
                  pl.BlockSpec((N_EXPERTS, LANES), lambda b: (b, 0)),
                  pl.BlockSpec((N_EXPERTS, LANES), lambda b: (b, 0)),
                  pl.BlockSpec((1, SEQ, d), lambda b: (b, 0, 0))],
        out_specs=[pl.BlockSpec((N_EXPERTS, CAP, d), lambda b: (0, b, 0)),
                   pl.BlockSpec((N_EXPERTS, CAP, 1), lambda b: (0, b, 0)),
                   pl.BlockSpec((1, N_EXPERTS, SEQ), lambda b: (b, 0, 0))],
        out_shape=[jax.ShapeDtypeStruct((N_EXPERTS, BATCH * CAP, d), BF16),
                   jax.ShapeDtypeStruct((N_EXPERTS, BATCH * CAP, 1), F32),
                   jax.ShapeDtypeStruct((BATCH, N_EXPERTS, SEQ), I32)],
        scratch_shapes=[pltpu.VMEM((EXPERT_GROUP * CAP, SEQ), BF16)],
        compiler_params=_params(("parallel",)),
        name="route_gather",
    )(aff_t, lo, hi, h2)


def _expert_body(x_ref, g_ref, wg_ref, wu_ref, wd_ref, o_ref, acc_ref, wgb_ref, wub_ref, wdb_ref):
    f = pl.program_id(1)

    @pl.when((pl.program_id(0) == 0) & (f == 0))
    def _():
        acc_ref[...] = jnp.zeros_like(acc_ref)

    carry = f > 0
    for m in range(BATCH * CAP // MC):
        rows = slice(m * MC, (m + 1) * MC)
        xm = x_ref[0, rows, :]
        if m == 0:
            wgb_ref[...] = wg_ref[0].astype(BF16)
        a = _dot(xm, wgb_ref[...])
        if m == 0:
            wub_ref[...] = wu_ref[0].astype(BF16)
        b = _dot(xm, wub_ref[...])
        hh = (a * jax.nn.sigmoid(a) * b).astype(BF16)
        if m == 0:
            wdb_ref[...] = wd_ref[0].astype(BF16)
        acc_ref[rows, :] = jnp.where(carry, acc_ref[rows, :], 0.0) + _dot(hh, wdb_ref[...])

    @pl.when(f == D_FF // TF - 1)
    def _():
        o_ref[0] = (acc_ref[...] * g_ref[0]).astype(BF16)


def _experts(xin, g, w_gate, w_up, w_down):
    d = D_MODEL
    rows = BATCH * CAP
    return pl.pallas_call(
        _expert_body,
        grid=(N_EXPERTS, D_FF // TF),
        in_specs=[pl.BlockSpec((1, rows, d), lambda e, f: (e, 0, 0)),
                  pl.BlockSpec((1, rows, 1), lambda e, f: (e, 0, 0)),
                  pl.BlockSpec((1, d, TF), lambda e, f: (e, 0, f)),
                  pl.BlockSpec((1, d, TF), lambda e, f: (e, 0, f)),
                  pl.BlockSpec((1, TF, d), lambda e, f: (e, f, 0))],
        out_specs=pl.BlockSpec((1, rows, d), lambda e, f: (e, 0, 0)),
        out_shape=jax.ShapeDtypeStruct((N_EXPERTS, rows, d), BF16),
        scratch_shapes=[pltpu.VMEM((rows, d), F32), pltpu.VMEM((d, TF), BF16), pltpu.VMEM((d, TF), BF16),
                        pltpu.VMEM((TF, d), BF16)],
        compiler_params=_params(("parallel", "arbitrary")),
        name="swiglu_experts",
    )(xin, g, w_gate, w_up, w_down)


def _scatter_body(pos_ref, y_ref, x_ref, g2_ref, o_ref):
    pos = pos_ref[0]
    slot = lax.broadcasted_iota(I32, (TS, CAP), 1)
    onehot = jnp.concatenate([jnp.where(pos[:, e:e + 1] == slot, 1.0, 0.0).astype(BF16) for e in range(N_EXPERTS)],
                             axis=1)
    y = y_ref[...].reshape(N_EXPERTS * CAP, D_MODEL)
    o_ref[0] = x_ref[0] + g2_ref[0] * _dot(onehot, y)


def _scatter(pos_tok, y, xn, mods3):
    d = D_MODEL
    return pl.pallas_call(
        _scatter_body,
        grid=(BATCH, SEQ // TS),
        in_specs=[pl.BlockSpec((1, TS, N_EXPERTS), lambda b, i: (b, i, 0)),
                  pl.BlockSpec((N_EXPERTS, CAP, d), lambda b, i: (0, b, 0)),
                  pl.BlockSpec((1, TS, d), lambda b, i: (b, i, 0)),
                  pl.BlockSpec((1, 1, d), lambda b, i: (b, 0, 5))],
        out_specs=pl.BlockSpec((1, TS, d), lambda b, i: (b, i, 0)),
        out_shape=jax.ShapeDtypeStruct((BATCH, SEQ, d), F32),
        compiler_params=_params(("parallel", "parallel")),
        name="scatter_residual",
    )(pos_tok, y, xn, mods3)


def _rope_tables():
    rows = SEQ // GRID_W
    row = np.repeat(np.arange(rows, dtype=np.float32), GRID_W)
    col = np.tile(np.arange(GRID_W, dtype=np.float32), rows)
    inv = (ROPE_BASE ** (-np.arange(0, AXIS_ROT, 2, dtype=np.float32) / AXIS_ROT)).astype(np.float32)
    ang = np.concatenate([row[:, None] * inv, col[:, None] * inv], axis=-1).astype(np.float64)
    cos = np.repeat(np.cos(ang), 2, axis=-1)
    sin = np.stack([-np.sin(ang), np.sin(ang)], axis=-1).reshape(SEQ, HEAD_DIM)
    reps = LANES // HEAD_DIM
    return jnp.asarray(np.tile(cos, (1, reps)), F32), jnp.asarray(np.tile(sin, (1, reps)), F32)


def _dft_tables():
    idx = np.arange(HALF, dtype=np.int64)
    t2p1 = 2 * idx + 1

    def cos_sin(f):
        ang = ((f[:, None] * t2p1[None, :]) % (2 * N_FFT)) * (math.pi / N_FFT)
        return np.cos(ang), np.sin(ang)

    ce, se = cos_sin(2 * idx)
    co, so = cos_sin(2 * idx + 1)
    t1 = jnp.asarray(np.concatenate([ce, so], axis=0), F32).astype(BF16)
    t2 = jnp.asarray(np.concatenate([co, se], axis=0), F32).astype(BF16)
    return t1, t2, t1.T, t2.T


def _phase_tables():
    idx = np.arange(HALF, dtype=np.float64)
    w = np.full((HALF,), 2.0 / N_FFT)
    we = w.copy()
    we[0] = 1.0 / N_FFT
    pe = (math.pi / N_FFT) * (2.0 * idx)
    po = (math.pi / N_FFT) * (2.0 * idx + 1.0)
    rot = np.stack([we * np.cos(pe), we * np.sin(pe), w * np.cos(po), w * np.sin(po)])
    return jnp.asarray(np.broadcast_to(rot[:, :, None], (4, HALF, CT)), F32)


def _fold_rows(a):
    return np.concatenate([a[:HALF], a[HALF:][::-1]], axis=0)


def _filter_features():
    t = np.linspace(0.0, 1.0, SEQ, dtype=np.float32).astype(np.float64)[:, None]
    w = 2.0 * math.pi * np.arange(SEQ, dtype=np.float64)[:, None] / SEQ
    fr = np.linspace(1e-4, FILTER_BANDS - 1, FILTER_BANDS, dtype=np.float32).astype(np.float64)[None, :]
    feat = np.concatenate([t, np.cos(fr * w), -np.sin(fr * w)], axis=-1)
    feat = np.pad(feat, ((0, 0), (0, FILTER_HIDDEN - FILTER_EMB)))
    min_decay = math.log(DECAY_TARGET) / SLOW_DECAY_PCT
    max_decay = math.log(DECAY_TARGET) / FAST_DECAY_PCT
    deltas = np.linspace(min_decay, max_decay, HYENA_W, dtype=np.float32).astype(np.float64)
    decay = np.exp(-t * np.abs(deltas))
    return jnp.asarray(_fold_rows(feat).T, F32), jnp.asarray(_fold_rows(decay), F32)


def _attn_bias():
    qi = np.arange(BLOCK)[:, None]
    kj = np.arange(3 * BLOCK)[None, :]
    band = np.abs(kj - BLOCK - qi) <= WINDOW
    first = band & (kj >= BLOCK)
    last = band & (kj < 2 * BLOCK)
    return jnp.asarray(np.where(np.stack([first, band, last]), 0.0, NEG), F32)


def _pair_heads(w, axis):
    heads = [lax.slice_in_dim(w, h * HEAD_DIM, (h + 1) * HEAD_DIM, axis=axis) for h in range(N_HEADS)]
    return jnp.concatenate([heads[j + GROUP * half] for j in range(GROUP) for half in range(N_KV_HEADS)], axis=axis)


def kernel(x, c, ctx, c_ctx, ada_w, ada_b, norm1, norm2, w_in, conv_w, conv_b, filt_w1, filt_b1, filt_w2, filt_b2,
           filt_w3, filt_b3, filt_freq, filt_out, hyena_bias, q_norm, k_norm, attn_sink, w_branch_a, w_branch_b,
           w_out, router, w_gate, w_up, w_down):
    d = D_MODEL
    assert ada_w.shape[0] == 1, "only the single-layer configuration is implemented"
    l = 0
    cos_t, sin_t = _rope_tables()
    t1, t2, t1t, t2t = _dft_tables()
    rot = _phase_tables()
    feat, decay = _filter_features()
    bias = _attn_bias()
    gmat = jnp.asarray(np.kron(np.eye(2 * LANES // HEAD_DIM), np.full((HEAD_DIM, HEAD_DIM), 1.0 / HEAD_DIM)), BF16)
    c16 = jnp.concatenate([c, c_ctx[None, :], jnp.zeros((MOD_ROWS - BATCH - 1, d), F32)], axis=0)

    mods3 = _ada(c16, ada_w[l], ada_b[l][None, :]).reshape(MOD_ROWS, 1, 6 * d)
    n1 = norm1[l][None, :]
    w_packed = _pack_in_weights(w_in[l])
    gk = jnp.tile(k_norm[l], N_KV_HEADS)[None, :]
    kf, kn = _filters(feat, jnp.pad(filt_w1[l].T, ((0, 0), (0, FILTER_HIDDEN - FILTER_EMB))), filt_b1[l][:, None],
                      filt_w2[l].T, filt_b2[l][:, None], filt_w3[l].T, filt_b3[l][:, None], filt_freq[l][:, None],
                      filt_out[l], decay, rot, t1, t2)
    kc, vc = _ctx_proj(ctx, mods3, n1, w_packed, gk, gmat)
    zh, q, k, v, gates = _inproj(x, mods3, n1, w_packed, jnp.tile(q_norm[l], N_HEADS)[None, :], gk, gmat, cos_t, sin_t)
    ya = _hyena(zh, conv_w[l], conv_b[l][None, :], hyena_bias[l], kf, kn, t1, t2, t1t, t2t)
    yb = _attention(attn_sink[l] * LOG2E, q, k, v, kc, vc, bias)
    router_t = router[l].T
    router_hi = router_t.astype(BF16)
    router_lo = (router_t - router_hi.astype(F32)).astype(BF16)
    xn, h2, aff_t = _merge(ya, yb, gates, x, mods3, norm2[l][None, :], w_branch_a[l].astype(BF16),
                           _pair_heads(w_branch_b[l], 0).astype(BF16), w_out[l].astype(BF16), router_hi, router_lo)
    lo, hi = _thresholds(aff_t.reshape(BATCH * N_EXPERTS, SEQ))
    xin, g, pos = _route(aff_t, lo, hi, h2)
    y = _experts(xin, g, w_gate[l], w_up[l], w_down[l])
    return _scatter(jnp.swapaxes(pos, 1, 2), y, xn, mods3)
```

```python
import math

import numpy as np
import jax
import jax.numpy as jnp
from jax import lax
from jax.experimental import pallas as pl
from jax.experimental.pallas import tpu as pltpu

F32 = jnp.float32
BF16 = jnp.bfloat16
I32 = jnp.int32
HIGHEST = lax.Precision.HIGHEST

D_MODEL = 1024
BATCH = 8
SEQ = 2048
GRID_W = 64
CTX_LEN = 256
N_HEADS = 8
N_KV_HEADS = 2
HEAD_DIM = 64
GROUP = N_HEADS // N_KV_HEADS
ATTN_W = N_HEADS * HEAD_DIM
KV_W = N_KV_HEADS * HEAD_DIM
WINDOW = 128
BLOCK = 128
HYENA_W = D_MODEL // 2
HYENA_ORDER = 2
FILTER_BANDS = 16
FILTER_EMB = 1 + 2 * FILTER_BANDS
FILTER_HIDDEN = 64
DECAY_TARGET = 1e-2
FAST_DECAY_PCT = 0.3
SLOW_DECAY_PCT = 1.5
ROPE_BASE = 10000.0
AXIS_ROT = HEAD_DIM // 2
N_EXPERTS = 16
EC_CAPACITY = 2
D_FF = 2048
EPS = 1e-6
NEG = -1e30
LOG2E = math.log2(math.e)

OFF_Q = 3 * HYENA_W
OFF_K = OFF_Q + ATTN_W
OFF_V = OFF_K + KV_W
OFF_G = OFF_V + KV_W
IN_W = OFF_G + 2 * D_MODEL

CAP = EC_CAPACITY * SEQ // N_EXPERTS
N_FFT = 2 * SEQ
HALF = SEQ // 2
MOD_ROWS = 16
LANES = 128

TM_IN = 1024
SUB_IN = 512
TM_MERGE = 1024
SUB_MERGE = 512
CT = 256
FC = 512
RB = 256
TF = 512
MC = 512
TS = 512
EXPERT_GROUP = 4
VMEM_LIMIT = 56 * 1024 * 1024


def _dot(a, b, precision=None):
    return jnp.dot(a, b, preferred_element_type=F32, precision=precision)


def _dot_nt(a, b, precision=None):
    return lax.dot_general(a, b, (((1,), (1,)), ((), ())), preferred_element_type=F32, precision=precision)


def _params(sem, vmem=VMEM_LIMIT):
    return pltpu.CompilerParams(dimension_semantics=sem, vmem_limit_bytes=vmem)


def _rms_mod(x, g, sc, sh):
    ms = jnp.mean(x * x, axis=-1, keepdims=True)
    return (x * lax.rsqrt(ms + EPS) * g) * (1.0 + sc) + sh


def _head_norm_rope(z, g, gmat, cos, sin, scale):
    ms = _dot((z * z).astype(BF16), gmat)
    y = z * lax.rsqrt(ms + EPS) * g
    if cos is not None:
        slabs = []
        for s in range(z.shape[1] // LANES):
            ys = y[:, s * LANES:(s + 1) * LANES]
            lane = lax.broadcasted_iota(I32, ys.shape, 1)
            nxt = pltpu.roll(ys, LANES - 1, axis=1)
            prv = pltpu.roll(ys, 1, axis=1)
            slabs.append(ys * cos + jnp.where((lane & 1) == 0, nxt, prv) * sin)
        y = slabs[0] if len(slabs) == 1 else jnp.concatenate(slabs, axis=1)
    return y * scale


def _split_bf16(x):
    hi = x.astype(BF16)
    return hi, (x - hi.astype(F32)).astype(BF16)


def _ada_body(c_ref, w_ref, b_ref, o_ref):
    c = c_ref[...]
    s_hi, s_lo = _split_bf16(c * jax.nn.sigmoid(c))
    w_hi, w_lo = _split_bf16(w_ref[...])
    o_ref[...] = _dot(s_hi, w_hi) + _dot(s_lo, w_hi) + _dot(s_hi, w_lo) + b_ref[...]


def _ada(c16, w, b):
    d = D_MODEL
    return pl.pallas_call(
        _ada_body,
        grid=(6,),
        in_specs=[pl.BlockSpec((MOD_ROWS, d), lambda j: (0, 0)),
                  pl.BlockSpec((d, d), lambda j: (0, j)),
                  pl.BlockSpec((1, d), lambda j: (0, j))],
        out_specs=pl.BlockSpec((MOD_ROWS, d), lambda j: (0, j)),
        out_shape=jax.ShapeDtypeStruct((MOD_ROWS, 6 * d), F32),
        compiler_params=_params(("parallel",)),
        name="ada_mod",
    )(c16, w, b)


def _sign_rows(n):
    lane = lax.broadcasted_iota(I32, (8, n), 1)
    sub = lax.broadcasted_iota(I32, (8, n), 0)
    sg = jnp.where((lane & 1) == 0, 1.0, -1.0)
    return jnp.where(sub == 0, sg, 0.0).astype(BF16)


def _filt_body(feat_ref, w1_ref, b1_ref, w2_ref, b2_ref, w3_ref, b3_ref, fq_ref, fof_ref, fob_ref, dec_ref,
               rot_ref, t1_ref, t2_ref, kf_ref, kn_ref, hh_ref, hl_ref):
    @pl.when((pl.program_id(0) == 0) & (pl.program_id(1) == 0))
    def _():
        fq = fq_ref[...]
        h = jnp.sin(fq * (_dot(w1_ref[...], feat_ref[...], HIGHEST) + b1_ref[...]))
        h = jnp.sin(fq * (_dot(w2_ref[...], h, HIGHEST) + b2_ref[...]))
        h = jnp.sin(fq * (_dot(w3_ref[...], h, HIGHEST) + b3_ref[...]))
        hh_ref[...], hl_ref[...] = _split_bf16(h.T)

    def taps(fo_ref):
        f_hi, f_lo = _split_bf16(fo_ref[...])
        return _dot(hh_ref[...], f_hi) + _dot(hl_ref[...], f_hi) + _dot(hh_ref[...], f_lo)

    dec = dec_ref[...]
    hf = taps(fof_ref) * dec
    hb = taps(fob_ref) * dec
    row = lax.broadcasted_iota(I32, hf.shape, 0)
    hb = jnp.where(row == 0, 0.0, hb)
    a = hf + hb
    b = hf - hb
    pa = (a[:HALF] + a[HALF:]).astype(BF16)
    ma = (a[:HALF] - a[HALF:]).astype(BF16)
    pb = (b[:HALF] + b[HALF:]).astype(BF16)
    mb = (b[:HALF] - b[HALF:]).astype(BF16)
    t1 = t1_ref[...]
    t2 = t2_ref[...]
    a1 = _dot(t1, pa)
    a2 = _dot(t2, ma)
    b1 = _dot(t1, pb)
    b2 = _dot(t2, mb)
    ce, se, co, so = rot_ref[0], rot_ref[1], rot_ref[2], rot_ref[3]
    kf_ref[0, 0] = a1[:HALF] * ce + a2[HALF:] * se
    kf_ref[0, 1] = b2[HALF:] * ce - b1[:HALF] * se
    kf_ref[0, 2] = a2[:HALF] * co + a1[HALF:] * so
    kf_ref[0, 3] = b1[HALF:] * co - b2[:HALF] * so
    kn_ref[0] = _dot(_sign_rows(HALF), ma)[0:1] * (1.0 / N_FFT)


def _filters(feat, w1, b1, w2, b2, w3, b3, fq, fout, decay, rot, t1, t2):
    nct = HYENA_W // CT
    full = lambda shape: pl.BlockSpec(shape, lambda o, c: (0,) * len(shape))
    return pl.pallas_call(
        _filt_body,
        grid=(HYENA_ORDER, nct),
        in_specs=[full((FILTER_HIDDEN, SEQ)), full((FILTER_HIDDEN, FILTER_HIDDEN)), full((FILTER_HIDDEN, 1)),
                  full((FILTER_HIDDEN, FILTER_HIDDEN)), full((FILTER_HIDDEN, 1)),
                  full((FILTER_HIDDEN, FILTER_HIDDEN)), full((FILTER_HIDDEN, 1)), full((FILTER_HIDDEN, 1)),
                  pl.BlockSpec((FILTER_HIDDEN, CT), lambda o, c: (0, (o * 2 + 0) * nct + c)),
                  pl.BlockSpec((FILTER_HIDDEN, CT), lambda o, c: (0, (o * 2 + 1) * nct + c)),
                  pl.BlockSpec((SEQ, CT), lambda o, c: (0, c)),
                  full((4, HALF, CT)),
                  pl.BlockSpec((SEQ, HALF), lambda o, c: (0, 0), pipeline_mode=pl.Buffered(1)),
                  pl.BlockSpec((SEQ, HALF), lambda o, c: (0, 0), pipeline_mode=pl.Buffered(1))],
        out_specs=[pl.BlockSpec((1, 4, HALF, CT), lambda o, c: (o, 0, 0, c)),
                   pl.BlockSpec((1, 1, CT), lambda o, c: (o, 0, c))],
        out_shape=[jax.ShapeDtypeStruct((HYENA_ORDER, 4, HALF, HYENA_W), F32),
                   jax.ShapeDtypeStruct((HYENA_ORDER, 1, HYENA_W), F32)],
        scratch_shapes=[pltpu.VMEM((SEQ, FILTER_HIDDEN), BF16)] * 2,
        compiler_params=_params(("arbitrary", "arbitrary")),
        name="hyena_filters",
    )(feat, w1, b1, w2, b2, w3, b3, fq, fout, fout, decay, rot, t1, t2)


def _pair_head_lanes(z):
    slabs = [z[:, s * LANES:(s + 1) * LANES] for s in range(ATTN_W // LANES)]
    swapped = [pltpu.roll(sl, HEAD_DIM, axis=1) for sl in slabs]
    low = lax.broadcasted_iota(I32, slabs[0].shape, 1) < HEAD_DIM
    out = []
    for j in range(GROUP):
        first, second = j, j + GROUP
        lo_src = slabs[first // 2] if first % 2 == 0 else swapped[first // 2]
        hi_src = slabs[second // 2] if second % 2 == 1 else swapped[second // 2]
        out.append(jnp.where(low, lo_src, hi_src))
    return jnp.concatenate(out, axis=1)


def _inproj_body(x_ref, sc_ref, sh_ref, n1_ref, wh_ref, wq_ref, wkv_ref, wg_ref, gq_ref, gk_ref, gmat_ref,
                 cos_ref, sin_ref, zh_ref, q_ref, k_ref, v_ref, gate_ref):
    groups = [slice(i * SUB_IN, (i + 1) * SUB_IN) for i in range(TM_IN // SUB_IN)]
    hx = [_rms_mod(x_ref[0, s, :], n1_ref[...], sc_ref[0], sh_ref[0]).astype(BF16) for s in groups]
    for s, h in zip(groups, hx):
        zh_ref[0, s, :] = _dot(h, wh_ref[...]).astype(BF16)
    pair = 2 * LANES
    zq = [_pair_head_lanes(_dot(h, wq_ref[...])) for h in hx]
    for s, z in zip(groups, zq):
        for c in range(ATTN_W // pair):
            sl = slice(c * pair, (c + 1) * pair)
            q_ref[0, s, sl] = _head_norm_rope(z[:, sl], gq_ref[:, sl], gmat_ref[...], cos_ref[s, :], sin_ref[s, :],
                                              LOG2E * HEAD_DIM ** -0.5).astype(BF16)
    zkv = [_dot(h, wkv_ref[...]) for h in hx]
    for s, z in zip(groups, zkv):
        k_ref[0, s, :] = _head_norm_rope(z[:, :KV_W], gk_ref[...], gmat_ref[0:KV_W, 0:KV_W], cos_ref[s, :],
                                         sin_ref[s, :], 1.0).astype(BF16)
        v_ref[0, s, :] = z[:, KV_W:].astype(BF16)
    for s, h in zip(groups, hx):
        gate_ref[0, s, :] = jax.nn.sigmoid(_dot(h, wg_ref[...])).astype(BF16)


def _pack_in_weights(w):
    wb = w.astype(BF16)
    return jnp.concatenate([wb[:, :OFF_K], wb[:, OFF_G:], wb[:, OFF_K:OFF_G]], axis=1)


W_OFF_H = 0
W_OFF_Q = OFF_Q
W_OFF_G = OFF_Q + ATTN_W
W_OFF_KV = W_OFF_G + 2 * D_MODEL


def _inproj(x, mods3, norm1, w_packed, gq, gk, gmat, cos_t, sin_t):
    d = D_MODEL
    nt = SEQ // TM_IN
    const = lambda shape: pl.BlockSpec(shape, lambda b, i: (0,) * len(shape))

    def wcol(width, off):
        assert off % width == 0
        return pl.BlockSpec((d, width), lambda b, i: (0, off // width))

    tok = lambda w: pl.BlockSpec((1, TM_IN, w), lambda b, i: (b, i, 0))
    return pl.pallas_call(
        _inproj_body,
        grid=(BATCH, nt),
        in_specs=[tok(d),
                  pl.BlockSpec((1, 1, d), lambda b, i: (b, 0, 1)),
                  pl.BlockSpec((1, 1, d), lambda b, i: (b, 0, 0)),
                  const((1, d)), wcol(OFF_Q, W_OFF_H), wcol(ATTN_W, W_OFF_Q), wcol(2 * KV_W, W_OFF_KV),
                  wcol(2 * d, W_OFF_G), const((1, ATTN_W)), const((1, KV_W)), const((2 * LANES, 2 * LANES)),
                  pl.BlockSpec((TM_IN, LANES), lambda b, i: (i, 0)),
                  pl.BlockSpec((TM_IN, LANES), lambda b, i: (i, 0))],
        out_specs=[tok(OFF_Q), tok(ATTN_W), tok(KV_W), tok(KV_W), tok(2 * d)],
        out_shape=[jax.ShapeDtypeStruct((BATCH, SEQ, OFF_Q), BF16),
                   jax.ShapeDtypeStruct((BATCH, SEQ, ATTN_W), BF16),
                   jax.ShapeDtypeStruct((BATCH, SEQ, KV_W), BF16),
                   jax.ShapeDtypeStruct((BATCH, SEQ, KV_W), BF16),
                   jax.ShapeDtypeStruct((BATCH, SEQ, 2 * d), BF16)],
        compiler_params=_params(("parallel", "parallel")),
        name="in_proj",
    )(x, mods3, mods3, norm1, w_packed, w_packed, w_packed, w_packed, gq, gk, gmat, cos_t, sin_t)


def _ctx_body(c_ref, sc_ref, sh_ref, n1_ref, wkv_ref, gk_ref, gmat_ref, kc_ref, vc_ref):
    hc = _rms_mod(c_ref[0], n1_ref[...], sc_ref[0], sh_ref[0]).astype(BF16)
    z = _dot(hc, wkv_ref[...])
    kc_ref[0] = _head_norm_rope(z[:, :KV_W], gk_ref[...], gmat_ref[0:KV_W, 0:KV_W], None, None, 1.0).astype(BF16)
    vc_ref[0] = z[:, KV_W:].astype(BF16)


def _ctx_proj(ctx, mods3, norm1, w_packed, gk, gmat):
    d = D_MODEL
    const = lambda shape: pl.BlockSpec(shape, lambda b: (0,) * len(shape))
    return pl.pallas_call(
        _ctx_body,
        grid=(BATCH,),
        in_specs=[pl.BlockSpec((1, CTX_LEN, d), lambda b: (b, 0, 0)),
                  pl.BlockSpec((1, 1, d), lambda b: (BATCH, 0, 1)),
                  pl.BlockSpec((1, 1, d), lambda b: (BATCH, 0, 0)),
                  const((1, d)), pl.BlockSpec((d, 2 * KV_W), lambda b: (0, W_OFF_KV // (2 * KV_W))),
                  const((1, KV_W)), const((2 * LANES, 2 * LANES))],
        out_specs=[pl.BlockSpec((1, CTX_LEN, KV_W), lambda b: (b, 0, 0)),
                   pl.BlockSpec((1, CTX_LEN, KV_W), lambda b: (b, 0, 0))],
        out_shape=[jax.ShapeDtypeStruct((BATCH, CTX_LEN, KV_W), BF16),
                   jax.ShapeDtypeStruct((BATCH, CTX_LEN, KV_W), BF16)],
        compiler_params=_params(("parallel",)),
        name="ctx_proj",
    )(ctx, mods3, mods3, norm1, w_packed, gk, gmat)


def _hyena_body(zv_ref, z1_ref, z2_ref, cwv_ref, cw1_ref, cw2_ref, cbv_ref, cb1_ref, cb2_ref, hb_ref, kf_ref,
                kn_ref, t1_ref, t2_ref, t1t_ref, t2t_ref, o_ref, lo_ref, hi_ref, glo_ref, ghi_ref, p_ref, m_ref,
                za_ref, zb_ref):
    row = lax.broadcasted_iota(I32, (HALF, CT), 0)
    rr = lax.broadcasted_iota(I32, (RB, RB), 0)
    cc = lax.broadcasted_iota(I32, (RB, RB), 1)
    flip = jnp.where(rr + cc == RB - 1, 1.0, 0.0).astype(BF16)
    nrb = HALF // RB

    def folded_short_conv(z_ref, w_ref, b_ref, lo_out, hi_out):
        zlo = z_ref[0, 0:HALF, :].astype(F32)
        for j in range(nrb):
            hi_out[j * RB:(j + 1) * RB, :] = _dot(flip, z_ref[0, SEQ - RB * (j + 1):SEQ - RB * j, :])
        zhi = hi_out[...]
        w0, w1, w2 = w_ref[0:1, :], w_ref[1:2, :], w_ref[2:3, :]
        first, last = row == 0, row == HALF - 1
        lo_prev = jnp.where(first, 0.0, pltpu.roll(zlo, 1, axis=0))
        lo_next = jnp.where(last, zhi[HALF - 1:HALF, :], pltpu.roll(zlo, HALF - 1, axis=0))
        hi_prev = jnp.where(first, 0.0, pltpu.roll(zhi, 1, axis=0))
        hi_next = jnp.where(last, zlo[HALF - 1:HALF, :], pltpu.roll(zhi, HALF - 1, axis=0))
        lo_out[...] = lo_prev * w0 + zlo * w1 + lo_next * w2 + b_ref[...]
        hi_out[...] = hi_next * w0 + zhi * w1 + hi_prev * w2 + b_ref[...]

    folded_short_conv(zv_ref, cwv_ref, cbv_ref, lo_ref, hi_ref)
    sign8 = _sign_rows(HALF)
    odd = (lax.broadcasted_iota(I32, (FC, CT), 0) & 1) == 1
    for o, (zr, cw, cb) in enumerate(((z1_ref, cw1_ref, cb1_ref), (z2_ref, cw2_ref, cb2_ref))):
        folded_short_conv(zr, cw, cb, glo_ref, ghi_ref)
        p_ref[...] = (lo_ref[...] + hi_ref[...]).astype(BF16)
        m_ref[...] = (lo_ref[...] - hi_ref[...]).astype(BF16)
        pv = p_ref[...]
        mv = m_ref[...]
        for c in range(HALF // FC):
            ev = slice(c * FC, (c + 1) * FC)
            od = slice(HALF + c * FC, HALF + (c + 1) * FC)
            xce = _dot(t1_ref[ev, :], pv)
            xso = _dot(t1_ref[od, :], pv)
            xco = _dot(t2_ref[ev, :], mv)
            xse = _dot(t2_ref[od, :], mv)
            kce, kse, kco, kso = kf_ref[o, 0, ev, :], kf_ref[o, 1, ev, :], kf_ref[o, 2, ev, :], kf_ref[o, 3, ev, :]
            za_ref[ev, :] = (xce * kce - xse * kse).astype(BF16)
            za_ref[od, :] = (xco * kso + xso * kco).astype(BF16)
            zb_ref[ev, :] = (xco * kco - xso * kso).astype(BF16)
            zb_ref[od, :] = (xce * kse + xse * kce).astype(BF16)
        zn = _dot(sign8, mv)[0:1] * kn_ref[o]
        bias = hb_ref[o:o + 1, :]
        za = za_ref[...]
        zb = zb_ref[...]
        for c in range(HALF // FC):
            rs = slice(c * FC, (c + 1) * FC)
            half_p = _dot(t1t_ref[rs, :], za)
            half_m = _dot(t2t_ref[rs, :], zb) + jnp.where(odd, -zn, zn)
            lo_ref[rs, :] = glo_ref[rs, :] * (half_p + half_m + bias * lo_ref[rs, :])
            hi_ref[rs, :] = ghi_ref[rs, :] * (half_p - half_m + bias * hi_ref[rs, :])
    o_ref[0, 0:HALF, :] = lo_ref[...].astype(BF16)
    for j in range(nrb):
        o_ref[0, SEQ - RB * (j + 1):SEQ - RB * j, :] = _dot(
            flip, hi_ref[j * RB:(j + 1) * RB, :].astype(BF16)).astype(BF16)


def _hyena(zh, conv_w, conv_b, hbias, kf, kn, t1, t2, t1t, t2t):
    nct = HYENA_W // CT
    zspec = lambda k: pl.BlockSpec((1, SEQ, CT), lambda c, b: (b, 0, k * nct + c))
    wspec = lambda k: pl.BlockSpec((3, CT), lambda c, b: (0, k * nct + c))
    bspec = lambda k: pl.BlockSpec((1, CT), lambda c, b: (0, k * nct + c))
    table = lambda shape: pl.BlockSpec(shape, lambda c, b: (0, 0), pipeline_mode=pl.Buffered(1))
    half_f32 = pltpu.VMEM((HALF, CT), F32)
    return pl.pallas_call(
        _hyena_body,
        grid=(nct, BATCH),
        in_specs=[zspec(0), zspec(1), zspec(2), wspec(0), wspec(1), wspec(2), bspec(0), bspec(1), bspec(2),
                  pl.BlockSpec((HYENA_ORDER, CT), lambda c, b: (0, c)),
                  pl.BlockSpec((HYENA_ORDER, 4, HALF, CT), lambda c, b: (0, 0, 0, c)),
                  pl.BlockSpec((HYENA_ORDER, 1, CT), lambda c, b: (0, 0, c)),
                  table((SEQ, HALF)), table((SEQ, HALF)), table((HALF, SEQ)), table((HALF, SEQ))],
        out_specs=pl.BlockSpec((1, SEQ, CT), lambda c, b: (b, 0, c)),
        out_shape=jax.ShapeDtypeStruct((BATCH, SEQ, HYENA_W), BF16),
        scratch_shapes=[half_f32, half_f32, half_f32, half_f32,
                        pltpu.VMEM((HALF, CT), BF16), pltpu.VMEM((HALF, CT), BF16),
                        pltpu.VMEM((SEQ, CT), BF16), pltpu.VMEM((SEQ, CT), BF16)],
        compiler_params=_params(("parallel", "parallel")),
        name="hyena_conv",
    )(zh, zh, zh, conv_w, conv_w, conv_w, conv_b, conv_b, conv_b, hbias, kf, kn, t1, t2, t1t, t2t)


def _attn_body(sink_ref, q_ref, k_ref, v_ref, kc_ref, vc_ref, bias_ref, o_ref, kp_ref, vlo_ref, vhi_ref):
    nb = SEQ // BLOCK
    lane = lax.broadcasted_iota(I32, (BLOCK, LANES), 1)
    low = lane < HEAD_DIM
    mask_lo = jnp.where(low, 1.0, 0.0).astype(BF16)
    mask_hi = jnp.where(low, 0.0, 1.0).astype(BF16)

    def with_ones(v):
        lane_v = lax.broadcasted_iota(I32, v.shape, 1) < HEAD_DIM
        one = jnp.ones_like(v)
        return jnp.where(lane_v, v, one), jnp.where(lane_v, one, v)

    zpad = jnp.zeros((BLOCK, KV_W), BF16)
    kp_ref[0:BLOCK] = zpad
    kp_ref[BLOCK:BLOCK + SEQ] = k_ref[0]
    kp_ref[BLOCK + SEQ:] = zpad
    v_lo, v_hi = with_ones(v_ref[0])
    for ref, val in ((vlo_ref, v_lo), (vhi_ref, v_hi)):
        ref[0:BLOCK] = zpad
        ref[BLOCK:BLOCK + SEQ] = val
        ref[BLOCK + SEQ:] = zpad
    kc = kc_ref[0]
    vc_pair = with_ones(vc_ref[0])

    def block(n, carry):
        r = pl.multiple_of(n * BLOCK, BLOCK)
        kw = kp_ref[pl.ds(r, 3 * BLOCK), :]
        vw_pair = (vlo_ref[pl.ds(r, 3 * BLOCK), :], vhi_ref[pl.ds(r, 3 * BLOCK), :])
        bias = bias_ref[jnp.where(n == 0, 0, jnp.where(n == nb - 1, 2, 1))]
        scores = []
        for j in range(GROUP):
            qs = q_ref[0, pl.ds(r, BLOCK), j * LANES:(j + 1) * LANES]
            for msk in (mask_lo, mask_hi):
                qm = qs * msk
                scores.append((_dot_nt(qm, kw) + bias, _dot_nt(qm, kc)))
        probs = []
        for idx, (sw, sc) in enumerate(scores):
            snk = sink_ref[idx // 2 + GROUP * (idx % 2)]
            m = jnp.maximum(jnp.maximum(jnp.max(sw, axis=-1, keepdims=True),
                                        jnp.max(sc, axis=-1, keepdims=True)), snk)
            probs.append((jnp.exp2(sw - m).astype(BF16), jnp.exp2(sc - m).astype(BF16), jnp.exp2(snk - m)))
        outs = []
        for idx, (pw, pc, psink) in enumerate(probs):
            acc = _dot(pw, vw_pair[idx % 2]) + _dot(pc, vc_pair[idx % 2])
            den = pltpu.roll(acc, HEAD_DIM, axis=1) + psink
            outs.append(acc / den)
        for j in range(GROUP):
            o_ref[0, pl.ds(r, BLOCK), j * LANES:(j + 1) * LANES] = jnp.where(
                low, outs[2 * j], outs[2 * j + 1]).astype(BF16)
        return carry

    lax.fori_loop(0, nb, block, 0)


def _attention(sink, q, k, v, kc, vc, bias):
    per_b = lambda n, w: pl.BlockSpec((1, n, w), lambda b: (b, 0, 0))
    return pl.pallas_call(
        _attn_body,
        grid=(BATCH,),
        in_specs=[pl.BlockSpec(memory_space=pltpu.SMEM),
                  per_b(SEQ, ATTN_W), per_b(SEQ, KV_W), per_b(SEQ, KV_W), per_b(CTX_LEN, KV_W), per_b(CTX_LEN, KV_W),
                  pl.BlockSpec((3, BLOCK, 3 * BLOCK), lambda b: (0, 0, 0))],
        out_specs=per_b(SEQ, ATTN_W),
        out_shape=jax.ShapeDtypeStruct((BATCH, SEQ, ATTN_W), BF16),
        scratch_shapes=[pltpu.VMEM((SEQ + 2 * BLOCK, KV_W), BF16)] * 3,
        compiler_params=_params(("parallel",)),
        name="window_attn",
    )(sink, q, k, v, kc, vc, bias)


def _merge_body(ya_ref, yb_ref, gt_ref, x_ref, g1_ref, sc2_ref, sh2_ref, n2_ref, wa_ref, wb_ref, wo_ref, rt_ref,
                xn_ref, h2_ref, aff_ref):
    d = D_MODEL
    groups = [slice(i * SUB_MERGE, (i + 1) * SUB_MERGE) for i in range(TM_MERGE // SUB_MERGE)]
    branch = [(_dot(ya_ref[0, s, :], wa_ref[...]), _dot(yb_ref[0, s, :], wb_ref[...])) for s in groups]
    mixed = [(gt_ref[0, s, :d].astype(F32) * ua + gt_ref[0, s, d:].astype(F32) * ub).astype(BF16)
             for s, (ua, ub) in zip(groups, branch)]
    proj = [_dot(u, wo_ref[...]) for u in mixed]
    h2_rows = []
    for s, m in zip(groups, proj):
        xn = x_ref[0, s, :] + g1_ref[0] * m
        xn_ref[0, s, :] = xn
        h2 = _rms_mod(xn, n2_ref[...], sc2_ref[0], sh2_ref[0]).astype(BF16)
        h2_ref[0, s, :] = h2
        h2_rows.append(h2)
    logits = [_dot_nt(rt_ref[...], h2) for h2 in h2_rows]
    for s, lt in zip(groups, logits):
        e = jnp.exp(lt - jnp.max(lt, axis=0, keepdims=True))
        aff_ref[0, :, s] = e / jnp.sum(e, axis=0, keepdims=True)


def _merge(ya, yb, gates, x, mods3, norm2, wa, wb, wo, router_t):
    d = D_MODEL
    nt = SEQ // TM_MERGE
    const = lambda shape: pl.BlockSpec(shape, lambda b, i: (0,) * len(shape))
    tok = lambda w: pl.BlockSpec((1, TM_MERGE, w), lambda b, i: (b, i, 0))
    mod = lambda k: pl.BlockSpec((1, 1, d), lambda b, i: (b, 0, k))
    return pl.pallas_call(
        _merge_body,
        grid=(BATCH, nt),
        in_specs=[tok(HYENA_W), tok(ATTN_W), tok(2 * d), tok(d), mod(2), mod(4), mod(3), const((1, d)),
                  const((HYENA_W, d)), const((ATTN_W, d)), const((d, d)), const((N_EXPERTS, d))],
        out_specs=[tok(d), tok(d), pl.BlockSpec((1, N_EXPERTS, TM_MERGE), lambda b, i: (b, 0, i))],
        out_shape=[jax.ShapeDtypeStruct((BATCH, SEQ, d), F32),
                   jax.ShapeDtypeStruct((BATCH, SEQ, d), BF16),
                   jax.ShapeDtypeStruct((BATCH, N_EXPERTS, SEQ), F32)],
        compiler_params=_params(("parallel", "parallel")),
        name="merge_outproj",
    )(ya, yb, gates, x, mods3, mods3, mods3, norm2, wa, wb, wo, router_t)


def _thresh_body(aff_ref, lo_ref, hi_ref):
    aff = aff_ref[...]
    rows = aff.shape[0]
    bits = pltpu.bitcast(aff, I32)

    def bisect_bits(_, carry):
        lo, hi = carry
        mid = lo + ((hi - lo) >> 1)
        ge = jnp.sum((bits >= mid).astype(I32), axis=1, keepdims=True) >= CAP
        return jnp.where(ge, mid, lo), jnp.where(ge, hi, mid)

    lo0 = jnp.zeros((rows, 1), I32)
    hi0 = jnp.full((rows, 1), 0x3F800001, I32)
    thr_bits, _ = lax.fori_loop(0, 31, bisect_bits, (lo0, hi0))
    thr = pltpu.bitcast(thr_bits, F32)

    def bisect_val(_, carry):
        lo, hi = carry
        mid = 0.5 * (lo + hi)
        ge = jnp.sum(jnp.where(aff >= mid, 1.0, 0.0), axis=1, keepdims=True) >= CAP
        return jnp.where(ge, mid, lo), jnp.where(ge, hi, mid)

    lo, hi = lax.fori_loop(0, 30, bisect_val, (0.5 * thr, jnp.maximum(2.0 * thr, 1e-30)))
    lo_ref[...] = jnp.broadcast_to(lo, lo_ref.shape)
    hi_ref[...] = jnp.broadcast_to(hi, hi_ref.shape)


def _thresholds(aff_rows):
    rows = BATCH * N_EXPERTS
    return pl.pallas_call(
        _thresh_body,
        grid=(1,),
        in_specs=[pl.BlockSpec((rows, SEQ), lambda i: (0, 0))],
        out_specs=[pl.BlockSpec((rows, LANES), lambda i: (0, 0)), pl.BlockSpec((rows, LANES), lambda i: (0, 0))],
        out_shape=[jax.ShapeDtypeStruct((rows, LANES), F32), jax.ShapeDtypeStruct((rows, LANES), F32)],
        compiler_params=_params(("arbitrary",)),
        name="route_threshold",
    )(aff_rows)


def _prefix_counts(mask):
    r = lax.broadcasted_iota(I32, (LANES, LANES), 0)
    c = lax.broadcasted_iota(I32, (LANES, LANES), 1)
    upper = jnp.where(r <= c, 1.0, 0.0).astype(BF16)
    offset = jnp.zeros((mask.shape[0], 1), F32)
    blocks = []
    for j in range(mask.shape[1] // LANES):
        blk = mask[:, j * LANES:(j + 1) * LANES]
        inc = _dot(blk.astype(BF16), upper)
        blocks.append(inc - blk + offset)
        offset = offset + inc[:, LANES - 1:LANES]
    return jnp.concatenate(blocks, axis=1)


def _route_body(aff_ref, lo_ref, hi_ref, h_ref, xin_ref, g_ref, pos_ref, pbuf_ref):
    aff = aff_ref[0]
    above = jnp.where(aff >= hi_ref[:, 0:1], 1.0, 0.0)
    band = jnp.where(aff >= lo_ref[:, 0:1], 1.0, 0.0) - above
    need = CAP - jnp.sum(above, axis=1, keepdims=True)
    tie_rank = _prefix_counts(band)
    self = above + band * jnp.where(tie_rank < need, 1.0, 0.0)
    pos = _prefix_counts(self)
    posi = jnp.where(self > 0.5, pos.astype(I32), -1)
    pos_ref[0] = posi

    h = h_ref[0]
    slot = lax.broadcasted_iota(I32, (CAP, SEQ), 0)
    for grp in range(N_EXPERTS // EXPERT_GROUP):
        for i in range(EXPERT_GROUP):
            e = grp * EXPERT_GROUP + i
            hit = posi[e:e + 1, :] == slot
            pbuf_ref[i * CAP:(i + 1) * CAP, :] = jnp.where(hit, 1.0, 0.0).astype(BF16)
            g_ref[e] = jnp.sum(jnp.where(hit, aff[e:e + 1, :], 0.0), axis=1, keepdims=True)
        rows = _dot(pbuf_ref[...], h)
        for i in range(EXPERT_GROUP):
            xin_ref[grp * EXPERT_GROUP + i] = rows[i * CAP:(i + 1) * CAP].astype(BF16)


def _route(aff_t, lo, hi, h2):
    d = D_MODEL
    return pl.pallas_call(
        _route_body,
        grid=(BATCH,),
        in_specs=[pl.BlockSpec((1, N_EXPERTS, SEQ), lambda b: (b, 0, 0)),
                  pl.BlockSpec((N_EXPERTS, LANES), lambda b: (b, 0)),
                  pl.BlockSpec((N_EXPERTS, LANES), lambda b: (b, 0)),
                  pl.BlockSpec((1, SEQ, d), lambda b: (b, 0, 0))],
        out_specs=[pl.BlockSpec((N_EXPERTS, CAP, d), lambda b: (0, b, 0)),
                   pl.BlockSpec((N_EXPERTS, CAP, 1), lambda b: (0, b, 0)),
                   pl.BlockSpec((1, N_EXPERTS, SEQ), lambda b: (b, 0, 0))],
        out_shape=[jax.ShapeDtypeStruct((N_EXPERTS, BATCH * CAP, d), BF16),
                   jax.ShapeDtypeStruct((N_EXPERTS, BATCH * CAP, 1), F32),
                   jax.ShapeDtypeStruct((BATCH, N_EXPERTS, SEQ), I32)],
        scratch_shapes=[pltpu.VMEM((EXPERT_GROUP * CAP, SEQ), BF16)],
        compiler_params=_params(("parallel",)),
        name="route_gather",
    )(aff_t, lo, hi, h2)


def _expert_body(x_ref, g_ref, wg_ref, wu_ref, wd_ref, o_ref, acc_ref, wgb_ref, wub_ref, wdb_ref):
    f = pl.program_id(1)

    @pl.when((pl.program_id(0) == 0) & (f == 0))
    def _():
        acc_ref[...] = jnp.zeros_like(acc_ref)

    carry = f > 0
    for m in range(BATCH * CAP // MC):
        rows = slice(m * MC, (m + 1) * MC)
        xm = x_ref[0, rows, :]
        if m == 0:
            wgb_ref[...] = wg_ref[0].astype(BF16)
        a = _dot(xm, wgb_ref[...])
        if m == 0:
            wub_ref[...] = wu_ref[0].astype(BF16)
        b = _dot(xm, wub_ref[...])
        hh = (a * jax.nn.sigmoid(a) * b).astype(BF16)
        if m == 0:
            wdb_ref[...] = wd_ref[0].astype(BF16)
        acc_ref[rows, :] = jnp.where(carry, acc_ref[rows, :], 0.0) + _dot(hh, wdb_ref[...])

    @pl.when(f == D_FF // TF - 1)
    def _():
        o_ref[0] = (acc_ref[...] * g_ref[0]).astype(BF16)


def _experts(xin, g, w_gate, w_up, w_down):
    d = D_MODEL
    rows = BATCH * CAP
    return pl.pallas_call(
        _expert_body,
        grid=(N_EXPERTS, D_FF // TF),
        in_specs=[pl.BlockSpec((1, rows, d), lambda e, f: (e, 0, 0)),
                  pl.BlockSpec((1, rows, 1), lambda e, f: (e, 0, 0)),
                  pl.BlockSpec((1, d, TF), lambda e, f: (e, 0, f)),
                  pl.BlockSpec((1, d, TF), lambda e, f: (e, 0, f)),
                  pl.BlockSpec((1, TF, d), lambda e, f: (e, f, 0))],
        out_specs=pl.BlockSpec((1, rows, d), lambda e, f: (e, 0, 0)),
        out_shape=jax.ShapeDtypeStruct((N_EXPERTS, rows, d), BF16),
        scratch_shapes=[pltpu.VMEM((rows, d), F32), pltpu.VMEM((d, TF), BF16), pltpu.VMEM((d, TF), BF16),
                        pltpu.VMEM((TF, d), BF16)],
        compiler_params=_params(("parallel", "arbitrary")),
        name="swiglu_experts",
    )(xin, g, w_gate, w_up, w_down)


def _scatter_body(pos_ref, y_ref, x_ref, g2_ref, o_ref):
    pos = pos_ref[0].T
    slot = lax.broadcasted_iota(I32, (TS, CAP), 1)
    onehot = jnp.concatenate([jnp.where(pos[:, e:e + 1] == slot, 1.0, 0.0).astype(BF16) for e in range(N_EXPERTS)],
                             axis=1)
    y = y_ref[...].reshape(N_EXPERTS * CAP, D_MODEL)
    o_ref[0] = x_ref[0] + g2_ref[0] * _dot(onehot, y)


def _scatter(pos, y, xn, mods3):
    d = D_MODEL
    return pl.pallas_call(
        _scatter_body,
        grid=(BATCH, SEQ // TS),
        in_specs=[pl.BlockSpec((1, N_EXPERTS, TS), lambda b, i: (b, 0, i)),
                  pl.BlockSpec((N_EXPERTS, CAP, d), lambda b, i: (0, b, 0)),
                  pl.BlockSpec((1, TS, d), lambda b, i: (b, i, 0)),
                  pl.BlockSpec((1, 1, d), lambda b, i: (b, 0, 5))],
        out_specs=pl.BlockSpec((1, TS, d), lambda b, i: (b, i, 0)),
        out_shape=jax.ShapeDtypeStruct((BATCH, SEQ, d), F32),
        compiler_params=_params(("parallel", "parallel")),
        name="scatter_residual",
    )(pos, y, xn, mods3)


def _rope_tables():
    rows = SEQ // GRID_W
    row = np.repeat(np.arange(rows, dtype=np.float32), GRID_W)
    col = np.tile(np.arange(GRID_W, dtype=np.float32), rows)
    inv = (ROPE_BASE ** (-np.arange(0, AXIS_ROT, 2, dtype=np.float32) / AXIS_ROT)).astype(np.float32)
    ang = np.concatenate([row[:, None] * inv, col[:, None] * inv], axis=-1).astype(np.float64)
    cos = np.repeat(np.cos(ang), 2, axis=-1)
    sin = np.stack([-np.sin(ang), np.sin(ang)], axis=-1).reshape(SEQ, HEAD_DIM)
    reps = LANES // HEAD_DIM
    return jnp.asarray(np.tile(cos, (1, reps)), F32), jnp.asarray(np.tile(sin, (1, reps)), F32)


def _dft_tables():
    idx = np.arange(HALF, dtype=np.int64)
    t2p1 = 2 * idx + 1

    def cos_sin(f):
        ang = ((f[:, None] * t2p1[None, :]) % (2 * N_FFT)) * (math.pi / N_FFT)
        return np.cos(ang), np.sin(ang)

    ce, se = cos_sin(2 * idx)
    co, so = cos_sin(2 * idx + 1)
    t1 = jnp.asarray(np.concatenate([ce, so], axis=0), F32).astype(BF16)
    t2 = jnp.asarray(np.concatenate([co, se], axis=0), F32).astype(BF16)
    return t1, t2, t1.T, t2.T


def _phase_tables():
    idx = np.arange(HALF, dtype=np.float64)
    w = np.full((HALF,), 2.0 / N_FFT)
    we = w.copy()
    we[0] = 1.0 / N_FFT
    pe = (math.pi / N_FFT) * (2.0 * idx)
    po = (math.pi / N_FFT) * (2.0 * idx + 1.0)
    rot = np.stack([we * np.cos(pe), we * np.sin(pe), w * np.cos(po), w * np.sin(po)])
    return jnp.asarray(np.broadcast_to(rot[:, :, None], (4, HALF, CT)), F32)


def _fold_rows(a):
    return np.concatenate([a[:HALF], a[HALF:][::-1]], axis=0)


def _filter_features():
    t = np.linspace(0.0, 1.0, SEQ, dtype=np.float32).astype(np.float64)[:, None]
    w = 2.0 * math.pi * np.arange(SEQ, dtype=np.float64)[:, None] / SEQ
    fr = np.linspace(1e-4, FILTER_BANDS - 1, FILTER_BANDS, dtype=np.float32).astype(np.float64)[None, :]
    feat = np.concatenate([t, np.cos(fr * w), -np.sin(fr * w)], axis=-1)
    feat = np.pad(feat, ((0, 0), (0, FILTER_HIDDEN - FILTER_EMB)))
    min_decay = math.log(DECAY_TARGET) / SLOW_DECAY_PCT
    max_decay = math.log(DECAY_TARGET) / FAST_DECAY_PCT
    deltas = np.linspace(min_decay, max_decay, HYENA_W, dtype=np.float32).astype(np.float64)
    decay = np.exp(-t * np.abs(deltas))
    return jnp.asarray(_fold_rows(feat).T, F32), jnp.asarray(_fold_rows(decay), F32)


def _attn_bias():
    qi = np.arange(BLOCK)[:, None]
    kj = np.arange(3 * BLOCK)[None, :]
    band = np.abs(kj - BLOCK - qi) <= WINDOW
    first = band & (kj >= BLOCK)
    last = band & (kj < 2 * BLOCK)
    return jnp.asarray(np.where(np.stack([first, band, last]), 0.0, NEG), F32)


def _pair_heads(w, axis):
    heads = [lax.slice_in_dim(w, h * HEAD_DIM, (h + 1) * HEAD_DIM, axis=axis) for h in range(N_HEADS)]
    return jnp.concatenate([heads[j + GROUP * half] for j in range(GROUP) for half in range(N_KV_HEADS)], axis=axis)


def kernel(x, c, ctx, c_ctx, ada_w, ada_b, norm1, norm2, w_in, conv_w, conv_b, filt_w1, filt_b1, filt_w2, filt_b2,
           filt_w3, filt_b3, filt_freq, filt_out, hyena_bias, q_norm, k_norm, attn_sink, w_branch_a, w_branch_b,
           w_out, router, w_gate, w_up, w_down):
    d = D_MODEL
    assert ada_w.shape[0] == 1, "only the single-layer configuration is implemented"
    l = 0
    cos_t, sin_t = _rope_tables()
    t1, t2, t1t, t2t = _dft_tables()
    rot = _phase_tables()
    feat, decay = _filter_features()
    bias = _attn_bias()
    gmat = jnp.asarray(np.kron(np.eye(2 * LANES // HEAD_DIM), np.full((HEAD_DIM, HEAD_DIM), 1.0 / HEAD_DIM)), BF16)
    c16 = jnp.concatenate([c, c_ctx[None, :], jnp.zeros((MOD_ROWS - BATCH - 1, d), F32)], axis=0)

    mods3 = _ada(c16, ada_w[l], ada_b[l][None, :]).reshape(MOD_ROWS, 1, 6 * d)
    n1 = norm1[l][None, :]
    w_packed = _pack_in_weights(w_in[l])
    gk = jnp.tile(k_norm[l], N_KV_HEADS)[None, :]
    kf, kn = _filters(feat, jnp.pad(filt_w1[l].T, ((0, 0), (0, FILTER_HIDDEN - FILTER_EMB))), filt_b1[l][:, None],
                      filt_w2[l].T, filt_b2[l][:, None], filt_w3[l].T, filt_b3[l][:, None], filt_freq[l][:, None],
                      filt_out[l], decay, rot, t1, t2)
    kc, vc = _ctx_proj(ctx, mods3, n1, w_packed, gk, gmat)
    zh, q, k, v, gates = _inproj(x, mods3, n1, w_packed, jnp.tile(q_norm[l], N_HEADS)[None, :], gk, gmat, cos_t, sin_t)
    ya = _hyena(zh, conv_w[l], conv_b[l][None, :], hyena_bias[l], kf, kn, t1, t2, t1t, t2t)
    yb = _attention(attn_sink[l] * LOG2E, q, k, v, kc, vc, bias)
    xn, h2, aff_t = _merge(ya, yb, gates, x, mods3, norm2[l][None, :], w_branch_a[l].astype(BF16),
                           _pair_heads(w_branch_b[l], 0).astype(BF16), w_out[l].astype(BF16), router[l].T.astype(BF16))
    lo, hi = _thresholds(aff_t.reshape(BATCH * N_EXPERTS, SEQ))
    xin, g, pos = _route(aff_t, lo, hi, h2)
    y = _experts(xin, g, w_gate[l], w_up[l], w_down[l])
    return _scatter(pos, y, xn, mods3)
```

```python
import math

import numpy as np
import jax
import jax.numpy as jnp
from jax import lax
from jax.experimental import pallas as pl
from jax.experimental.pallas import tpu as pltpu

F32 = jnp.float32
BF16 = jnp.bfloat16
I32 = jnp.int32
HIGHEST = lax.Precision.HIGHEST

D_MODEL = 1024
BATCH = 8
SEQ = 2048
GRID_W = 64
CTX_LEN = 256
N_HEADS = 8
N_KV_HEADS = 2
HEAD_DIM = 64
GROUP = N_HEADS // N_KV_HEADS
ATTN_W = N_HEADS * HEAD_DIM
KV_W = N_KV_HEADS * HEAD_DIM
WINDOW = 128
BLOCK = 128
HYENA_W = D_MODEL // 2
HYENA_ORDER = 2
FILTER_BANDS = 16
FILTER_EMB = 1 + 2 * FILTER_BANDS
FILTER_HIDDEN = 64
DECAY_TARGET = 1e-2
FAST_DECAY_PCT = 0.3
SLOW_DECAY_PCT = 1.5
ROPE_BASE = 10000.0
AXIS_ROT = HEAD_DIM // 2
N_EXPERTS = 16
EC_CAPACITY = 2
D_FF = 2048
EPS = 1e-6
NEG = -1e30
LOG2E = math.log2(math.e)

OFF_Q = 3 * HYENA_W
OFF_K = OFF_Q + ATTN_W
OFF_V = OFF_K + KV_W
OFF_G = OFF_V + KV_W
IN_W = OFF_G + 2 * D_MODEL

CAP = EC_CAPACITY * SEQ // N_EXPERTS
N_FFT = 2 * SEQ
HALF = SEQ // 2
MOD_ROWS = 16
LANES = 128

TM_IN = 1024
SUB_IN = 512
TM_MERGE = 1024
SUB_MERGE = 512
CT = 256
FC = 512
RB = 256
TF = 512
MC = 512
TS = 512
EXPERT_GROUP = 4
VMEM_LIMIT = 56 * 1024 * 1024


def _dot(a, b, precision=None):
    return jnp.dot(a, b, preferred_element_type=F32, precision=precision)


def _dot_nt(a, b, precision=None):
    return lax.dot_general(a, b, (((1,), (1,)), ((), ())), preferred_element_type=F32, precision=precision)


def _params(sem, vmem=VMEM_LIMIT):
    return pltpu.CompilerParams(dimension_semantics=sem, vmem_limit_bytes=vmem)


def _rms_mod(x, g, sc, sh):
    ms = jnp.mean(x * x, axis=-1, keepdims=True)
    return (x * lax.rsqrt(ms + EPS) * g) * (1.0 + sc) + sh


def _head_norm_rope(z, g, gmat, cos, sin, scale):
    ms = _dot((z * z).astype(BF16), gmat)
    y = z * lax.rsqrt(ms + EPS) * g
    if cos is not None:
        slabs = []
        for s in range(z.shape[1] // LANES):
            ys = y[:, s * LANES:(s + 1) * LANES]
            lane = lax.broadcasted_iota(I32, ys.shape, 1)
            nxt = pltpu.roll(ys, LANES - 1, axis=1)
            prv = pltpu.roll(ys, 1, axis=1)
            slabs.append(ys * cos + jnp.where((lane & 1) == 0, nxt, prv) * sin)
        y = slabs[0] if len(slabs) == 1 else jnp.concatenate(slabs, axis=1)
    return y * scale


def _split_bf16(x):
    hi = x.astype(BF16)
    return hi, (x - hi.astype(F32)).astype(BF16)


def _ada_body(c_ref, w_ref, b_ref, o_ref):
    c = c_ref[...]
    s_hi, s_lo = _split_bf16(c * jax.nn.sigmoid(c))
    w_hi, w_lo = _split_bf16(w_ref[...])
    o_ref[...] = _dot(s_hi, w_hi) + _dot(s_lo, w_hi) + _dot(s_hi, w_lo) + b_ref[...]


def _ada(c16, w, b):
    d = D_MODEL
    return pl.pallas_call(
        _ada_body,
        grid=(6,),
        in_specs=[pl.BlockSpec((MOD_ROWS, d), lambda j: (0, 0)),
                  pl.BlockSpec((d, d), lambda j: (0, j)),
                  pl.BlockSpec((1, d), lambda j: (0, j))],
        out_specs=pl.BlockSpec((MOD_ROWS, d), lambda j: (0, j)),
        out_shape=jax.ShapeDtypeStruct((MOD_ROWS, 6 * d), F32),
        compiler_params=_params(("parallel",)),
        name="ada_mod",
    )(c16, w, b)


def _sign_rows(n):
    lane = lax.broadcasted_iota(I32, (8, n), 1)
    sub = lax.broadcasted_iota(I32, (8, n), 0)
    sg = jnp.where((lane & 1) == 0, 1.0, -1.0)
    return jnp.where(sub == 0, sg, 0.0).astype(BF16)


def _filt_body(feat_ref, w1_ref, b1_ref, w2_ref, b2_ref, w3_ref, b3_ref, fq_ref, fof_ref, fob_ref, dec_ref,
               rot_ref, t1_ref, t2_ref, kf_ref, kn_ref, hh_ref, hl_ref):
    @pl.when((pl.program_id(0) == 0) & (pl.program_id(1) == 0))
    def _():
        fq = fq_ref[...]
        h = jnp.sin(fq * (_dot(w1_ref[...], feat_ref[...], HIGHEST) + b1_ref[...]))
        h = jnp.sin(fq * (_dot(w2_ref[...], h, HIGHEST) + b2_ref[...]))
        h = jnp.sin(fq * (_dot(w3_ref[...], h, HIGHEST) + b3_ref[...]))
        hh_ref[...], hl_ref[...] = _split_bf16(h.T)

    def taps(fo_ref):
        f_hi, f_lo = _split_bf16(fo_ref[...])
        return _dot(hh_ref[...], f_hi) + _dot(hl_ref[...], f_hi) + _dot(hh_ref[...], f_lo)

    dec = dec_ref[...]
    hf = taps(fof_ref) * dec
    hb = taps(fob_ref) * dec
    row = lax.broadcasted_iota(I32, hf.shape, 0)
    hb = jnp.where(row == 0, 0.0, hb)
    a = hf + hb
    b = hf - hb
    pa = (a[:HALF] + a[HALF:]).astype(BF16)
    ma = (a[:HALF] - a[HALF:]).astype(BF16)
    pb = (b[:HALF] + b[HALF:]).astype(BF16)
    mb = (b[:HALF] - b[HALF:]).astype(BF16)
    t1 = t1_ref[...]
    t2 = t2_ref[...]
    a1 = _dot(t1, pa)
    a2 = _dot(t2, ma)
    b1 = _dot(t1, pb)
    b2 = _dot(t2, mb)
    ce, se, co, so = rot_ref[0], rot_ref[1], rot_ref[2], rot_ref[3]
    kf_ref[0, 0] = a1[:HALF] * ce + a2[HALF:] * se
    kf_ref[0, 1] = b2[HALF:] * ce - b1[:HALF] * se
    kf_ref[0, 2] = a2[:HALF] * co + a1[HALF:] * so
    kf_ref[0, 3] = b1[HALF:] * co - b2[:HALF] * so
    kn_ref[0] = _dot(_sign_rows(HALF), ma)[0:1] * (1.0 / N_FFT)


def _filters(feat, w1, b1, w2, b2, w3, b3, fq, fout, decay, rot, t1, t2):
    nct = HYENA_W // CT
    full = lambda shape: pl.BlockSpec(shape, lambda o, c: (0,) * len(shape))
    return pl.pallas_call(
        _filt_body,
        grid=(HYENA_ORDER, nct),
        in_specs=[full((FILTER_HIDDEN, SEQ)), full((FILTER_HIDDEN, FILTER_HIDDEN)), full((FILTER_HIDDEN, 1)),
                  full((FILTER_HIDDEN, FILTER_HIDDEN)), full((FILTER_HIDDEN, 1)),
                  full((FILTER_HIDDEN, FILTER_HIDDEN)), full((FILTER_HIDDEN, 1)), full((FILTER_HIDDEN, 1)),
                  pl.BlockSpec((FILTER_HIDDEN, CT), lambda o, c: (0, (o * 2 + 0) * nct + c)),
                  pl.BlockSpec((FILTER_HIDDEN, CT), lambda o, c: (0, (o * 2 + 1) * nct + c)),
                  pl.BlockSpec((SEQ, CT), lambda o, c: (0, c)),
                  full((4, HALF, CT)),
                  pl.BlockSpec((SEQ, HALF), lambda o, c: (0, 0), pipeline_mode=pl.Buffered(1)),
                  pl.BlockSpec((SEQ, HALF), lambda o, c: (0, 0), pipeline_mode=pl.Buffered(1))],
        out_specs=[pl.BlockSpec((1, 4, HALF, CT), lambda o, c: (o, 0, 0, c)),
                   pl.BlockSpec((1, 1, CT), lambda o, c: (o, 0, c))],
        out_shape=[jax.ShapeDtypeStruct((HYENA_ORDER, 4, HALF, HYENA_W), F32),
                   jax.ShapeDtypeStruct((HYENA_ORDER, 1, HYENA_W), F32)],
        scratch_shapes=[pltpu.VMEM((SEQ, FILTER_HIDDEN), BF16)] * 2,
        compiler_params=_params(("arbitrary", "arbitrary")),
        name="hyena_filters",
    )(feat, w1, b1, w2, b2, w3, b3, fq, fout, fout, decay, rot, t1, t2)


def _pair_head_lanes(z):
    slabs = [z[:, s * LANES:(s + 1) * LANES] for s in range(ATTN_W // LANES)]
    swapped = [pltpu.roll(sl, HEAD_DIM, axis=1) for sl in slabs]
    low = lax.broadcasted_iota(I32, slabs[0].shape, 1) < HEAD_DIM
    out = []
    for j in range(GROUP):
        first, second = j, j + GROUP
        lo_src = slabs[first // 2] if first % 2 == 0 else swapped[first // 2]
        hi_src = slabs[second // 2] if second % 2 == 1 else swapped[second // 2]
        out.append(jnp.where(low, lo_src, hi_src))
    return jnp.concatenate(out, axis=1)


def _inproj_body(x_ref, sc_ref, sh_ref, n1_ref, wh_ref, wq_ref, wkv_ref, wg_ref, gq_ref, gk_ref, gmat_ref,
                 cos_ref, sin_ref, zh_ref, q_ref, k_ref, v_ref, gate_ref):
    groups = [slice(i * SUB_IN, (i + 1) * SUB_IN) for i in range(TM_IN // SUB_IN)]
    hx = [_rms_mod(x_ref[0, s, :], n1_ref[...], sc_ref[0], sh_ref[0]).astype(BF16) for s in groups]
    for s, h in zip(groups, hx):
        zh_ref[0, s, :] = _dot(h, wh_ref[...]).astype(BF16)
    pair = 2 * LANES
    zq = [_pair_head_lanes(_dot(h, wq_ref[...])) for h in hx]
    for s, z in zip(groups, zq):
        for c in range(ATTN_W // pair):
            sl = slice(c * pair, (c + 1) * pair)
            q_ref[0, s, sl] = _head_norm_rope(z[:, sl], gq_ref[:, sl], gmat_ref[...], cos_ref[s, :], sin_ref[s, :],
                                              LOG2E * HEAD_DIM ** -0.5).astype(BF16)
    zkv = [_dot(h, wkv_ref[...]) for h in hx]
    for s, z in zip(groups, zkv):
        k_ref[0, s, :] = _head_norm_rope(z[:, :KV_W], gk_ref[...], gmat_ref[0:KV_W, 0:KV_W], cos_ref[s, :],
                                         sin_ref[s, :], 1.0).astype(BF16)
        v_ref[0, s, :] = z[:, KV_W:].astype(BF16)
    for s, h in zip(groups, hx):
        gate_ref[0, s, :] = jax.nn.sigmoid(_dot(h, wg_ref[...])).astype(BF16)


def _pack_in_weights(w):
    wb = w.astype(BF16)
    return jnp.concatenate([wb[:, :OFF_K], wb[:, OFF_G:], wb[:, OFF_K:OFF_G]], axis=1)


W_OFF_H = 0
W_OFF_Q = OFF_Q
W_OFF_G = OFF_Q + ATTN_W
W_OFF_KV = W_OFF_G + 2 * D_MODEL


def _inproj(x, mods3, norm1, w_packed, gq, gk, gmat, cos_t, sin_t):
    d = D_MODEL
    nt = SEQ // TM_IN
    const = lambda shape: pl.BlockSpec(shape, lambda b, i: (0,) * len(shape))

    def wcol(width, off):
        assert off % width == 0
        return pl.BlockSpec((d, width), lambda b, i: (0, off // width))

    tok = lambda w: pl.BlockSpec((1, TM_IN, w), lambda b, i: (b, i, 0))
    return pl.pallas_call(
        _inproj_body,
        grid=(BATCH, nt),
        in_specs=[tok(d),
                  pl.BlockSpec((1, 1, d), lambda b, i: (b, 0, 1)),
                  pl.BlockSpec((1, 1, d), lambda b, i: (b, 0, 0)),
                  const((1, d)), wcol(OFF_Q, W_OFF_H), wcol(ATTN_W, W_OFF_Q), wcol(2 * KV_W, W_OFF_KV),
                  wcol(2 * d, W_OFF_G), const((1, ATTN_W)), const((1, KV_W)), const((2 * LANES, 2 * LANES)),
                  pl.BlockSpec((TM_IN, LANES), lambda b, i: (i, 0)),
                  pl.BlockSpec((TM_IN, LANES), lambda b, i: (i, 0))],
        out_specs=[tok(OFF_Q), tok(ATTN_W), tok(KV_W), tok(KV_W), tok(2 * d)],
        out_shape=[jax.ShapeDtypeStruct((BATCH, SEQ, OFF_Q), BF16),
                   jax.ShapeDtypeStruct((BATCH, SEQ, ATTN_W), BF16),
                   jax.ShapeDtypeStruct((BATCH, SEQ, KV_W), BF16),
                   jax.ShapeDtypeStruct((BATCH, SEQ, KV_W), BF16),
                   jax.ShapeDtypeStruct((BATCH, SEQ, 2 * d), BF16)],
        compiler_params=_params(("parallel", "parallel")),
        name="in_proj",
    )(x, mods3, mods3, norm1, w_packed, w_packed, w_packed, w_packed, gq, gk, gmat, cos_t, sin_t)


def _ctx_body(c_ref, sc_ref, sh_ref, n1_ref, wkv_ref, gk_ref, gmat_ref, kc_ref, vc_ref):
    hc = _rms_mod(c_ref[0], n1_ref[...], sc_ref[0], sh_ref[0]).astype(BF16)
    z = _dot(hc, wkv_ref[...])
    kc_ref[0] = _head_norm_rope(z[:, :KV_W], gk_ref[...], gmat_ref[0:KV_W, 0:KV_W], None, None, 1.0).astype(BF16)
    vc_ref[0] = z[:, KV_W:].astype(BF16)


def _ctx_proj(ctx, mods3, norm1, w_packed, gk, gmat):
    d = D_MODEL
    const = lambda shape: pl.BlockSpec(shape, lambda b: (0,) * len(shape))
    return pl.pallas_call(
        _ctx_body,
        grid=(BATCH,),
        in_specs=[pl.BlockSpec((1, CTX_LEN, d), lambda b: (b, 0, 0)),
                  pl.BlockSpec((1, 1, d), lambda b: (BATCH, 0, 1)),
                  pl.BlockSpec((1, 1, d), lambda b: (BATCH, 0, 0)),
                  const((1, d)), pl.BlockSpec((d, 2 * KV_W), lambda b: (0, W_OFF_KV // (2 * KV_W))),
                  const((1, KV_W)), const((2 * LANES, 2 * LANES))],
        out_specs=[pl.BlockSpec((1, CTX_LEN, KV_W), lambda b: (b, 0, 0)),
                   pl.BlockSpec((1, CTX_LEN, KV_W), lambda b: (b, 0, 0))],
        out_shape=[jax.ShapeDtypeStruct((BATCH, CTX_LEN, KV_W), BF16),
                   jax.ShapeDtypeStruct((BATCH, CTX_LEN, KV_W), BF16)],
        compiler_params=_params(("parallel",)),
        name="ctx_proj",
    )(ctx, mods3, mods3, norm1, w_packed, gk, gmat)


def _hyena_body(zv_ref, z1_ref, z2_ref, cwv_ref, cw1_ref, cw2_ref, cbv_ref, cb1_ref, cb2_ref, hb_ref, kf_ref,
                kn_ref, t1_ref, t2_ref, t1t_ref, t2t_ref, o_ref, lo_ref, hi_ref, glo_ref, ghi_ref, p_ref, m_ref,
                za_ref, zb_ref):
    row = lax.broadcasted_iota(I32, (HALF, CT), 0)
    rr = lax.broadcasted_iota(I32, (RB, RB), 0)
    cc = lax.broadcasted_iota(I32, (RB, RB), 1)
    flip = jnp.where(rr + cc == RB - 1, 1.0, 0.0).astype(BF16)
    nrb = HALF // RB

    def folded_short_conv(z_ref, w_ref, b_ref, lo_out, hi_out):
        zlo = z_ref[0, 0:HALF, :].astype(F32)
        for j in range(nrb):
            hi_out[j * RB:(j + 1) * RB, :] = _dot(flip, z_ref[0, SEQ - RB * (j + 1):SEQ - RB * j, :])
        zhi = hi_out[...]
        w0, w1, w2 = w_ref[0:1, :], w_ref[1:2, :], w_ref[2:3, :]
        first, last = row == 0, row == HALF - 1
        lo_prev = jnp.where(first, 0.0, pltpu.roll(zlo, 1, axis=0))
        lo_next = jnp.where(last, zhi[HALF - 1:HALF, :], pltpu.roll(zlo, HALF - 1, axis=0))
        hi_prev = jnp.where(first, 0.0, pltpu.roll(zhi, 1, axis=0))
        hi_next = jnp.where(last, zlo[HALF - 1:HALF, :], pltpu.roll(zhi, HALF - 1, axis=0))
        lo_out[...] = lo_prev * w0 + zlo * w1 + lo_next * w2 + b_ref[...]
        hi_out[...] = hi_next * w0 + zhi * w1 + hi_prev * w2 + b_ref[...]

    folded_short_conv(zv_ref, cwv_ref, cbv_ref, lo_ref, hi_ref)
    sign8 = _sign_rows(HALF)
    odd = (lax.broadcasted_iota(I32, (FC, CT), 0) & 1) == 1
    for o, (zr, cw, cb) in enumerate(((z1_ref, cw1_ref, cb1_ref), (z2_ref, cw2_ref, cb2_ref))):
        folded_short_conv(zr, cw, cb, glo_ref, ghi_ref)
        p_ref[...] = (lo_ref[...] + hi_ref[...]).astype(BF16)
        m_ref[...] = (lo_ref[...] - hi_ref[...]).astype(BF16)
        pv = p_ref[...]
        mv = m_ref[...]
        for c in range(HALF // FC):
            ev = slice(c * FC, (c + 1) * FC)
            od = slice(HALF + c * FC, HALF + (c + 1) * FC)
            xce = _dot(t1_ref[ev, :], pv)
            xso = _dot(t1_ref[od, :], pv)
            xco = _dot(t2_ref[ev, :], mv)
            xse = _dot(t2_ref[od, :], mv)
            kce, kse, kco, kso = kf_ref[o, 0, ev, :], kf_ref[o, 1, ev, :], kf_ref[o, 2, ev, :], kf_ref[o, 3, ev, :]
            za_ref[ev, :] = (xce * kce - xse * kse).astype(BF16)
            za_ref[od, :] = (xco * kso + xso * kco).astype(BF16)
            zb_ref[ev, :] = (xco * kco - xso * kso).astype(BF16)
            zb_ref[od, :] = (xce * kse + xse * kce).astype(BF16)
        zn = _dot(sign8, mv)[0:1] * kn_ref[o]
        bias = hb_ref[o:o + 1, :]
        za = za_ref[...]
        zb = zb_ref[...]
        for c in range(HALF // FC):
            rs = slice(c * FC, (c + 1) * FC)
            half_p = _dot(t1t_ref[rs, :], za)
            half_m = _dot(t2t_ref[rs, :], zb) + jnp.where(odd, -zn, zn)
            lo_ref[rs, :] = glo_ref[rs, :] * (half_p + half_m + bias * lo_ref[rs, :])
            hi_ref[rs, :] = ghi_ref[rs, :] * (half_p - half_m + bias * hi_ref[rs, :])
    o_ref[0, 0:HALF, :] = lo_ref[...].astype(BF16)
    for j in range(nrb):
        o_ref[0, SEQ - RB * (j + 1):SEQ - RB * j, :] = _dot(
            flip, hi_ref[j * RB:(j + 1) * RB, :].astype(BF16)).astype(BF16)


def _hyena(zh, conv_w, conv_b, hbias, kf, kn, t1, t2, t1t, t2t):
    nct = HYENA_W // CT
    zspec = lambda k: pl.BlockSpec((1, SEQ, CT), lambda c, b: (b, 0, k * nct + c))
    wspec = lambda k: pl.BlockSpec((3, CT), lambda c, b: (0, k * nct + c))
    bspec = lambda k: pl.BlockSpec((1, CT), lambda c, b: (0, k * nct + c))
    table = lambda shape: pl.BlockSpec(shape, lambda c, b: (0, 0), pipeline_mode=pl.Buffered(1))
    half_f32 = pltpu.VMEM((HALF, CT), F32)
    return pl.pallas_call(
        _hyena_body,
        grid=(nct, BATCH),
        in_specs=[zspec(0), zspec(1), zspec(2), wspec(0), wspec(1), wspec(2), bspec(0), bspec(1), bspec(2),
                  pl.BlockSpec((HYENA_ORDER, CT), lambda c, b: (0, c)),
                  pl.BlockSpec((HYENA_ORDER, 4, HALF, CT), lambda c, b: (0, 0, 0, c)),
                  pl.BlockSpec((HYENA_ORDER, 1, CT), lambda c, b: (0, 0, c)),
                  table((SEQ, HALF)), table((SEQ, HALF)), table((HALF, SEQ)), table((HALF, SEQ))],
        out_specs=pl.BlockSpec((1, SEQ, CT), lambda c, b: (b, 0, c)),
        out_shape=jax.ShapeDtypeStruct((BATCH, SEQ, HYENA_W), BF16),
        scratch_shapes=[half_f32, half_f32, half_f32, half_f32,
                        pltpu.VMEM((HALF, CT), BF16), pltpu.VMEM((HALF, CT), BF16),
                        pltpu.VMEM((SEQ, CT), BF16), pltpu.VMEM((SEQ, CT), BF16)],
        compiler_params=_params(("parallel", "parallel")),
        name="hyena_conv",
    )(zh, zh, zh, conv_w, conv_w, conv_w, conv_b, conv_b, conv_b, hbias, kf, kn, t1, t2, t1t, t2t)


def _attn_body(sink_ref, q_ref, k_ref, v_ref, kc_ref, vc_ref, bias_ref, o_ref, kp_ref, vlo_ref, vhi_ref):
    nb = SEQ // BLOCK
    lane = lax.broadcasted_iota(I32, (BLOCK, LANES), 1)
    low = lane < HEAD_DIM
    mask_lo = jnp.where(low, 1.0, 0.0).astype(BF16)
    mask_hi = jnp.where(low, 0.0, 1.0).astype(BF16)

    def with_ones(v):
        lane_v = lax.broadcasted_iota(I32, v.shape, 1) < HEAD_DIM
        one = jnp.ones_like(v)
        return jnp.where(lane_v, v, one), jnp.where(lane_v, one, v)

    zpad = jnp.zeros((BLOCK, KV_W), BF16)
    kp_ref[0:BLOCK] = zpad
    kp_ref[BLOCK:BLOCK + SEQ] = k_ref[0]
    kp_ref[BLOCK + SEQ:] = zpad
    v_lo, v_hi = with_ones(v_ref[0])
    for ref, val in ((vlo_ref, v_lo), (vhi_ref, v_hi)):
        ref[0:BLOCK] = zpad
        ref[BLOCK:BLOCK + SEQ] = val
        ref[BLOCK + SEQ:] = zpad
    kc = kc_ref[0]
    vc_pair = with_ones(vc_ref[0])

    def block(n, carry):
        r = pl.multiple_of(n * BLOCK, BLOCK)
        kw = kp_ref[pl.ds(r, 3 * BLOCK), :]
        vw_pair = (vlo_ref[pl.ds(r, 3 * BLOCK), :], vhi_ref[pl.ds(r, 3 * BLOCK), :])
        bias = bias_ref[jnp.where(n == 0, 0, jnp.where(n == nb - 1, 2, 1))]
        scores = []
        for j in range(GROUP):
            qs = q_ref[0, pl.ds(r, BLOCK), j * LANES:(j + 1) * LANES]
            for msk in (mask_lo, mask_hi):
                qm = qs * msk
                scores.append((_dot_nt(qm, kw) + bias, _dot_nt(qm, kc)))
        probs = []
        for idx, (sw, sc) in enumerate(scores):
            snk = sink_ref[idx // 2 + GROUP * (idx % 2)]
            m = jnp.maximum(jnp.maximum(jnp.max(sw, axis=-1, keepdims=True),
                                        jnp.max(sc, axis=-1, keepdims=True)), snk)
            probs.append((jnp.exp2(sw - m).astype(BF16), jnp.exp2(sc - m).astype(BF16), jnp.exp2(snk - m)))
        outs = []
        for idx, (pw, pc, psink) in enumerate(probs):
            acc = _dot(pw, vw_pair[idx % 2]) + _dot(pc, vc_pair[idx % 2])
            den = pltpu.roll(acc, HEAD_DIM, axis=1) + psink
            outs.append(acc / den)
        for j in range(GROUP):
            o_ref[0, pl.ds(r, BLOCK), j * LANES:(j + 1) * LANES] = jnp.where(
                low, outs[2 * j], outs[2 * j + 1]).astype(BF16)
        return carry

    lax.fori_loop(0, nb, block, 0)


def _attention(sink, q, k, v, kc, vc, bias):
    per_b = lambda n, w: pl.BlockSpec((1, n, w), lambda b: (b, 0, 0))
    return pl.pallas_call(
        _attn_body,
        grid=(BATCH,),
        in_specs=[pl.BlockSpec(memory_space=pltpu.SMEM),
                  per_b(SEQ, ATTN_W), per_b(SEQ, KV_W), per_b(SEQ, KV_W), per_b(CTX_LEN, KV_W), per_b(CTX_LEN, KV_W),
                  pl.BlockSpec((3, BLOCK, 3 * BLOCK), lambda b: (0, 0, 0))],
        out_specs=per_b(SEQ, ATTN_W),
        out_shape=jax.ShapeDtypeStruct((BATCH, SEQ, ATTN_W), BF16),
        scratch_shapes=[pltpu.VMEM((SEQ + 2 * BLOCK, KV_W), BF16)] * 3,
        compiler_params=_params(("parallel",)),
        name="window_attn",
    )(sink, q, k, v, kc, vc, bias)


def _merge_body(ya_ref, yb_ref, gt_ref, x_ref, g1_ref, sc2_ref, sh2_ref, n2_ref, wa_ref, wb_ref, wo_ref, rt_ref,
                xn_ref, h2_ref, aff_ref):
    d = D_MODEL
    groups = [slice(i * SUB_MERGE, (i + 1) * SUB_MERGE) for i in range(TM_MERGE // SUB_MERGE)]
    branch = [(_dot(ya_ref[0, s, :], wa_ref[...]), _dot(yb_ref[0, s, :], wb_ref[...])) for s in groups]
    mixed = [(gt_ref[0, s, :d].astype(F32) * ua + gt_ref[0, s, d:].astype(F32) * ub).astype(BF16)
             for s, (ua, ub) in zip(groups, branch)]
    proj = [_dot(u, wo_ref[...]) for u in mixed]
    h2_rows = []
    for s, m in zip(groups, proj):
        xn = x_ref[0, s, :] + g1_ref[0] * m
        xn_ref[0, s, :] = xn
        h2 = _rms_mod(xn, n2_ref[...], sc2_ref[0], sh2_ref[0]).astype(BF16)
        h2_ref[0, s, :] = h2
        h2_rows.append(h2)
    logits = [_dot_nt(rt_ref[...], h2) for h2 in h2_rows]
    for s, lt in zip(groups, logits):
        e = jnp.exp(lt - jnp.max(lt, axis=0, keepdims=True))
        aff_ref[0, :, s] = e / jnp.sum(e, axis=0, keepdims=True)


def _merge(ya, yb, gates, x, mods3, norm2, wa, wb, wo, router_t):
    d = D_MODEL
    nt = SEQ // TM_MERGE
    const = lambda shape: pl.BlockSpec(shape, lambda b, i: (0,) * len(shape))
    tok = lambda w: pl.BlockSpec((1, TM_MERGE, w), lambda b, i: (b, i, 0))
    mod = lambda k: pl.BlockSpec((1, 1, d), lambda b, i: (b, 0, k))
    return pl.pallas_call(
        _merge_body,
        grid=(BATCH, nt),
        in_specs=[tok(HYENA_W), tok(ATTN_W), tok(2 * d), tok(d), mod(2), mod(4), mod(3), const((1, d)),
                  const((HYENA_W, d)), const((ATTN_W, d)), const((d, d)), const((N_EXPERTS, d))],
        out_specs=[tok(d), tok(d), pl.BlockSpec((1, N_EXPERTS, TM_MERGE), lambda b, i: (b, 0, i))],
        out_shape=[jax.ShapeDtypeStruct((BATCH, SEQ, d), F32),
                   jax.ShapeDtypeStruct((BATCH, SEQ, d), BF16),
                   jax.ShapeDtypeStruct((BATCH, N_EXPERTS, SEQ), F32)],
        compiler_params=_params(("parallel", "parallel")),
        name="merge_outproj",
    )(ya, yb, gates, x, mods3, mods3, mods3, norm2, wa, wb, wo, router_t)


def _thresh_body(aff_ref, lo_ref, hi_ref):
    aff = aff_ref[...]
    rows = aff.shape[0]
    bits = pltpu.bitcast(aff, I32)

    def bisect_bits(_, carry):
        lo, hi = carry
        mid = lo + ((hi - lo) >> 1)
        ge = jnp.sum((bits >= mid).astype(I32), axis=1, keepdims=True) >= CAP
        return jnp.where(ge, mid, lo), jnp.where(ge, hi, mid)

    lo0 = jnp.zeros((rows, 1), I32)
    hi0 = jnp.full((rows, 1), 0x3F800001, I32)
    thr_bits, _ = lax.fori_loop(0, 31, bisect_bits, (lo0, hi0))
    thr = pltpu.bitcast(thr_bits, F32)

    def bisect_val(_, carry):
        lo, hi = carry
        mid = 0.5 * (lo + hi)
        ge = jnp.sum(jnp.where(aff >= mid, 1.0, 0.0), axis=1, keepdims=True) >= CAP
        return jnp.where(ge, mid, lo), jnp.where(ge, hi, mid)

    lo, hi = lax.fori_loop(0, 30, bisect_val, (0.5 * thr, jnp.maximum(2.0 * thr, 1e-30)))
    lo_ref[...] = jnp.broadcast_to(lo, lo_ref.shape)
    hi_ref[...] = jnp.broadcast_to(hi, hi_ref.shape)


def _thresholds(aff_rows):
    rows = BATCH * N_EXPERTS
    return pl.pallas_call(
        _thresh_body,
        grid=(1,),
        in_specs=[pl.BlockSpec((rows, SEQ), lambda i: (0, 0))],
        out_specs=[pl.BlockSpec((rows, LANES), lambda i: (0, 0)), pl.BlockSpec((rows, LANES), lambda i: (0, 0))],
        out_shape=[jax.ShapeDtypeStruct((rows, LANES), F32), jax.ShapeDtypeStruct((rows, LANES), F32)],
        compiler_params=_params(("arbitrary",)),
        name="route_threshold",
    )(aff_rows)


def _prefix_counts(mask):
    r = lax.broadcasted_iota(I32, (LANES, LANES), 0)
    c = lax.broadcasted_iota(I32, (LANES, LANES), 1)
    upper = jnp.where(r <= c, 1.0, 0.0).astype(BF16)
    offset = jnp.zeros((mask.shape[0], 1), F32)
    blocks = []
    for j in range(mask.shape[1] // LANES):
        blk = mask[:, j * LANES:(j + 1) * LANES]
        inc = _dot(blk.astype(BF16), upper)
        blocks.append(inc - blk + offset)
        offset = offset + inc[:, LANES - 1:LANES]
    return jnp.concatenate(blocks, axis=1)


def _route_body(aff_ref, lo_ref, hi_ref, h_ref, slot_ref, xin_ref, pos_ref, pbuf_ref):
    aff = aff_ref[0]
    above = jnp.where(aff >= hi_ref[:, 0:1], 1.0, 0.0)
    band = jnp.where(aff >= lo_ref[:, 0:1], 1.0, 0.0) - above
    need = CAP - jnp.sum(above, axis=1, keepdims=True)
    tie_rank = _prefix_counts(band)
    self = above + band * jnp.where(tie_rank < need, 1.0, 0.0)
    pos = _prefix_counts(self)
    posi = jnp.where(self > 0.5, pos.astype(I32), -1)
    pos_ref[0] = posi

    h = h_ref[0]
    slot = slot_ref[...]
    posb = jnp.where(self > 0.5, pos, -1.0).astype(BF16)
    one = jnp.ones((CAP, SEQ), BF16)
    zero = jnp.zeros((CAP, SEQ), BF16)
    for grp in range(N_EXPERTS // EXPERT_GROUP):
        for i in range(EXPERT_GROUP):
            e = grp * EXPERT_GROUP + i
            pbuf_ref[i * CAP:(i + 1) * CAP, :] = jnp.where(posb[e:e + 1, :] == slot, one, zero)
        rows = _dot(pbuf_ref[...], h)
        for i in range(EXPERT_GROUP):
            xin_ref[grp * EXPERT_GROUP + i] = rows[i * CAP:(i + 1) * CAP].astype(BF16)


def _route(aff_t, lo, hi, h2, slot_rows):
    d = D_MODEL
    return pl.pallas_call(
        _route_body,
        grid=(BATCH,),
        in_specs=[pl.BlockSpec((1, N_EXPERTS, SEQ), lambda b: (b, 0, 0)),
                  pl.BlockSpec((N_EXPERTS, LANES), lambda b: (b, 0)),
                  pl.BlockSpec((N_EXPERTS, LANES), lambda b: (b, 0)),
                  pl.BlockSpec((1, SEQ, d), lambda b: (b, 0, 0)),
                  pl.BlockSpec((CAP, SEQ), lambda b: (0, 0))],
        out_specs=[pl.BlockSpec((N_EXPERTS, CAP, d), lambda b: (0, b, 0)),
                   pl.BlockSpec((1, N_EXPERTS, SEQ), lambda b: (b, 0, 0))],
        out_shape=[jax.ShapeDtypeStruct((N_EXPERTS, BATCH * CAP, d), BF16),
                   jax.ShapeDtypeStruct((BATCH, N_EXPERTS, SEQ), I32)],
        scratch_shapes=[pltpu.VMEM((EXPERT_GROUP * CAP, SEQ), BF16)],
        compiler_params=_params(("parallel",)),
        name="route_gather",
    )(aff_t, lo, hi, h2, slot_rows)


def _expert_body(x_ref, wg_ref, wu_ref, wd_ref, o_ref, acc_ref, wgb_ref, wub_ref, wdb_ref):
    f = pl.program_id(1)

    @pl.when((pl.program_id(0) == 0) & (f == 0))
    def _():
        acc_ref[...] = jnp.zeros_like(acc_ref)

    carry = f > 0
    for m in range(BATCH * CAP // MC):
        rows = slice(m * MC, (m + 1) * MC)
        xm = x_ref[0, rows, :]
        if m == 0:
            wgb_ref[...] = wg_ref[0].astype(BF16)
        a = _dot(xm, wgb_ref[...])
        if m == 0:
            wub_ref[...] = wu_ref[0].astype(BF16)
        b = _dot(xm, wub_ref[...])
        hh = (a * jax.nn.sigmoid(a) * b).astype(BF16)
        if m == 0:
            wdb_ref[...] = wd_ref[0].astype(BF16)
        acc_ref[rows, :] = jnp.where(carry, acc_ref[rows, :], 0.0) + _dot(hh, wdb_ref[...])

    @pl.when(f == D_FF // TF - 1)
    def _():
        o_ref[0] = acc_ref[...].astype(BF16)


def _experts(xin, w_gate, w_up, w_down):
    d = D_MODEL
    rows = BATCH * CAP
    return pl.pallas_call(
        _expert_body,
        grid=(N_EXPERTS, D_FF // TF),
        in_specs=[pl.BlockSpec((1, rows, d), lambda e, f: (e, 0, 0)),
                  pl.BlockSpec((1, d, TF), lambda e, f: (e, 0, f)),
                  pl.BlockSpec((1, d, TF), lambda e, f: (e, 0, f)),
                  pl.BlockSpec((1, TF, d), lambda e, f: (e, f, 0))],
        out_specs=pl.BlockSpec((1, rows, d), lambda e, f: (e, 0, 0)),
        out_shape=jax.ShapeDtypeStruct((N_EXPERTS, rows, d), BF16),
        scratch_shapes=[pltpu.VMEM((rows, d), F32), pltpu.VMEM((d, TF), BF16), pltpu.VMEM((d, TF), BF16),
                        pltpu.VMEM((TF, d), BF16)],
        compiler_params=_params(("parallel", "arbitrary")),
        name="swiglu_experts",
    )(xin, w_gate, w_up, w_down)


def _scatter_body(pos_ref, aff_ref, y_ref, x_ref, g2_ref, o_ref):
    pos = pos_ref[0].T
    aff = aff_ref[0].T
    slot = lax.broadcasted_iota(I32, (TS, CAP), 1)
    onehot = jnp.concatenate([jnp.where(pos[:, e:e + 1] == slot, aff[:, e:e + 1], 0.0).astype(BF16)
                              for e in range(N_EXPERTS)], axis=1)
    y = y_ref[...].reshape(N_EXPERTS * CAP, D_MODEL)
    o_ref[0] = x_ref[0] + g2_ref[0] * _dot(onehot, y)


def _scatter(pos, aff_t, y, xn, mods3):
    d = D_MODEL
    return pl.pallas_call(
        _scatter_body,
        grid=(BATCH, SEQ // TS),
        in_specs=[pl.BlockSpec((1, N_EXPERTS, TS), lambda b, i: (b, 0, i)),
                  pl.BlockSpec((1, N_EXPERTS, TS), lambda b, i: (b, 0, i)),
                  pl.BlockSpec((N_EXPERTS, CAP, d), lambda b, i: (0, b, 0)),
                  pl.BlockSpec((1, TS, d), lambda b, i: (b, i, 0)),
                  pl.BlockSpec((1, 1, d), lambda b, i: (b, 0, 5))],
        out_specs=pl.BlockSpec((1, TS, d), lambda b, i: (b, i, 0)),
        out_shape=jax.ShapeDtypeStruct((BATCH, SEQ, d), F32),
        compiler_params=_params(("parallel", "parallel")),
        name="scatter_residual",
    )(pos, aff_t, y, xn, mods3)


def _rope_tables():
    rows = SEQ // GRID_W
    row = np.repeat(np.arange(rows, dtype=np.float32), GRID_W)
    col = np.tile(np.arange(GRID_W, dtype=np.float32), rows)
    inv = (ROPE_BASE ** (-np.arange(0, AXIS_ROT, 2, dtype=np.float32) / AXIS_ROT)).astype(np.float32)
    ang = np.concatenate([row[:, None] * inv, col[:, None] * inv], axis=-1).astype(np.float64)
    cos = np.repeat(np.cos(ang), 2, axis=-1)
    sin = np.stack([-np.sin(ang), np.sin(ang)], axis=-1).reshape(SEQ, HEAD_DIM)
    reps = LANES // HEAD_DIM
    return jnp.asarray(np.tile(cos, (1, reps)), F32), jnp.asarray(np.tile(sin, (1, reps)), F32)


def _dft_tables():
    idx = np.arange(HALF, dtype=np.int64)
    t2p1 = 2 * idx + 1

    def cos_sin(f):
        ang = ((f[:, None] * t2p1[None, :]) % (2 * N_FFT)) * (math.pi / N_FFT)
        return np.cos(ang), np.sin(ang)

    ce, se = cos_sin(2 * idx)
    co, so = cos_sin(2 * idx + 1)
    t1 = jnp.asarray(np.concatenate([ce, so], axis=0), F32).astype(BF16)
    t2 = jnp.asarray(np.concatenate([co, se], axis=0), F32).astype(BF16)
    return t1, t2, t1.T, t2.T


def _phase_tables():
    idx = np.arange(HALF, dtype=np.float64)
    w = np.full((HALF,), 2.0 / N_FFT)
    we = w.copy()
    we[0] = 1.0 / N_FFT
    pe = (math.pi / N_FFT) * (2.0 * idx)
    po = (math.pi / N_FFT) * (2.0 * idx + 1.0)
    rot = np.stack([we * np.cos(pe), we * np.sin(pe), w * np.cos(po), w * np.sin(po)])
    return jnp.asarray(np.broadcast_to(rot[:, :, None], (4, HALF, CT)), F32)


def _fold_rows(a):
    return np.concatenate([a[:HALF], a[HALF:][::-1]], axis=0)


def _filter_features():
    t = np.linspace(0.0, 1.0, SEQ, dtype=np.float32).astype(np.float64)[:, None]
    w = 2.0 * math.pi * np.arange(SEQ, dtype=np.float64)[:, None] / SEQ
    fr = np.linspace(1e-4, FILTER_BANDS - 1, FILTER_BANDS, dtype=np.float32).astype(np.float64)[None, :]
    feat = np.concatenate([t, np.cos(fr * w), -np.sin(fr * w)], axis=-1)
    feat = np.pad(feat, ((0, 0), (0, FILTER_HIDDEN - FILTER_EMB)))
    min_decay = math.log(DECAY_TARGET) / SLOW_DECAY_PCT
    max_decay = math.log(DECAY_TARGET) / FAST_DECAY_PCT
    deltas = np.linspace(min_decay, max_decay, HYENA_W, dtype=np.float32).astype(np.float64)
    decay = np.exp(-t * np.abs(deltas))
    return jnp.asarray(_fold_rows(feat).T, F32), jnp.asarray(_fold_rows(decay), F32)


def _attn_bias():
    qi = np.arange(BLOCK)[:, None]
    kj = np.arange(3 * BLOCK)[None, :]
    band = np.abs(kj - BLOCK - qi) <= WINDOW
    first = band & (kj >= BLOCK)
    last = band & (kj < 2 * BLOCK)
    return jnp.asarray(np.where(np.stack([first, band, last]), 0.0, NEG), F32)


def _pair_heads(w, axis):
    heads = [lax.slice_in_dim(w, h * HEAD_DIM, (h + 1) * HEAD_DIM, axis=axis) for h in range(N_HEADS)]
    return jnp.concatenate([heads[j + GROUP * half] for j in range(GROUP) for half in range(N_KV_HEADS)], axis=axis)


def kernel(x, c, ctx, c_ctx, ada_w, ada_b, norm1, norm2, w_in, conv_w, conv_b, filt_w1, filt_b1, filt_w2, filt_b2,
           filt_w3, filt_b3, filt_freq, filt_out, hyena_bias, q_norm, k_norm, attn_sink, w_branch_a, w_branch_b,
           w_out, router, w_gate, w_up, w_down):
    d = D_MODEL
    assert ada_w.shape[0] == 1, "only the single-layer configuration is implemented"
    l = 0
    cos_t, sin_t = _rope_tables()
    t1, t2, t1t, t2t = _dft_tables()
    rot = _phase_tables()
    feat, decay = _filter_features()
    bias = _attn_bias()
    gmat = jnp.asarray(np.kron(np.eye(2 * LANES // HEAD_DIM), np.full((HEAD_DIM, HEAD_DIM), 1.0 / HEAD_DIM)), BF16)
    c16 = jnp.concatenate([c, c_ctx[None, :], jnp.zeros((MOD_ROWS - BATCH - 1, d), F32)], axis=0)

    mods3 = _ada(c16, ada_w[l], ada_b[l][None, :]).reshape(MOD_ROWS, 1, 6 * d)
    n1 = norm1[l][None, :]
    w_packed = _pack_in_weights(w_in[l])
    gk = jnp.tile(k_norm[l], N_KV_HEADS)[None, :]
    kf, kn = _filters(feat, jnp.pad(filt_w1[l].T, ((0, 0), (0, FILTER_HIDDEN - FILTER_EMB))), filt_b1[l][:, None],
                      filt_w2[l].T, filt_b2[l][:, None], filt_w3[l].T, filt_b3[l][:, None], filt_freq[l][:, None],
                      filt_out[l], decay, rot, t1, t2)
    kc, vc = _ctx_proj(ctx, mods3, n1, w_packed, gk, gmat)
    zh, q, k, v, gates = _inproj(x, mods3, n1, w_packed, jnp.tile(q_norm[l], N_HEADS)[None, :], gk, gmat, cos_t, sin_t)
    ya = _hyena(zh, conv_w[l], conv_b[l][None, :], hyena_bias[l], kf, kn, t1, t2, t1t, t2t)
    yb = _attention(attn_sink[l] * LOG2E, q, k, v, kc, vc, bias)
    xn, h2, aff_t = _merge(ya, yb, gates, x, mods3, norm2[l][None, :], w_branch_a[l].astype(BF16),
                           _pair_heads(w_branch_b[l], 0).astype(BF16), w_out[l].astype(BF16), router[l].T.astype(BF16))
    lo, hi = _thresholds(aff_t.reshape(BATCH * N_EXPERTS, SEQ))
    slot_rows = jnp.asarray(np.broadcast_to(np.arange(CAP)[:, None], (CAP, SEQ)), BF16)
    xin, pos = _route(aff_t, lo, hi, h2, slot_rows)
    y = _experts(xin, w_gate[l], w_up[l], w_down[l])
    return _scatter(pos, aff_t, y, xn, mods3)
```

```python
import math

import numpy as np
import jax
import jax.numpy as jnp
from jax import lax
from jax.experimental import pallas as pl
from jax.experimental.pallas import tpu as pltpu

F32 = jnp.float32
BF16 = jnp.bfloat16
I32 = jnp.int32
HIGHEST = lax.Precision.HIGHEST

D_MODEL = 1024
BATCH = 8
SEQ = 2048
GRID_W = 64
CTX_LEN = 256
N_HEADS = 8
N_KV_HEADS = 2
HEAD_DIM = 64
GROUP = N_HEADS // N_KV_HEADS
ATTN_W = N_HEADS * HEAD_DIM
KV_W = N_KV_HEADS * HEAD_DIM
WINDOW = 128
BLOCK = 128
HYENA_W = D_MODEL // 2
HYENA_ORDER = 2
FILTER_BANDS = 16
FILTER_EMB = 1 + 2 * FILTER_BANDS
FILTER_HIDDEN = 64
DECAY_TARGET = 1e-2
FAST_DECAY_PCT = 0.3
SLOW_DECAY_PCT = 1.5
ROPE_BASE = 10000.0
AXIS_ROT = HEAD_DIM // 2
N_EXPERTS = 16
EC_CAPACITY = 2
D_FF = 2048
EPS = 1e-6
NEG = -1e30
LOG2E = math.log2(math.e)

OFF_Q = 3 * HYENA_W
OFF_K = OFF_Q + ATTN_W
OFF_V = OFF_K + KV_W
OFF_G = OFF_V + KV_W
IN_W = OFF_G + 2 * D_MODEL

CAP = EC_CAPACITY * SEQ // N_EXPERTS
N_FFT = 2 * SEQ
HALF = SEQ // 2
MOD_ROWS = 16
LANES = 128

TM_IN = 1024
SUB_IN = 512
TM_MERGE = 1024
SUB_MERGE = 512
CT = 256
FC = 512
RB = 256
TF = 512
MC = 512
TS = 1024
SUB_S = 512
CTX_STEP = 4
EXPERT_GROUP = 4
VMEM_LIMIT = 56 * 1024 * 1024


def _dot(a, b, precision=None):
    return jnp.dot(a, b, preferred_element_type=F32, precision=precision)


def _dot_nt(a, b, precision=None):
    return lax.dot_general(a, b, (((1,), (1,)), ((), ())), preferred_element_type=F32, precision=precision)


def _params(sem, vmem=VMEM_LIMIT):
    return pltpu.CompilerParams(dimension_semantics=sem, vmem_limit_bytes=vmem)


def _rms_mod(x, g, sc, sh):
    ms = jnp.mean(x * x, axis=-1, keepdims=True)
    return (x * lax.rsqrt(ms + EPS) * g) * (1.0 + sc) + sh


def _head_norm_rope(z, g, gmat, cos, sin, scale):
    ms = _dot((z * z).astype(BF16), gmat)
    y = z * lax.rsqrt(ms + EPS) * g
    if cos is not None:
        slabs = []
        for s in range(z.shape[1] // LANES):
            ys = y[:, s * LANES:(s + 1) * LANES]
            lane = lax.broadcasted_iota(I32, ys.shape, 1)
            nxt = pltpu.roll(ys, LANES - 1, axis=1)
            prv = pltpu.roll(ys, 1, axis=1)
            slabs.append(ys * cos + jnp.where((lane & 1) == 0, nxt, prv) * sin)
        y = slabs[0] if len(slabs) == 1 else jnp.concatenate(slabs, axis=1)
    return y * scale


def _split_bf16(x):
    hi = x.astype(BF16)
    return hi, (x - hi.astype(F32)).astype(BF16)


def _ada_body(c_ref, w_ref, b_ref, o_ref):
    c = c_ref[...]
    s_hi, s_lo = _split_bf16(c * jax.nn.sigmoid(c))
    w_hi, w_lo = _split_bf16(w_ref[...])
    o_ref[...] = _dot(s_hi, w_hi) + _dot(s_lo, w_hi) + _dot(s_hi, w_lo) + b_ref[...]


def _ada(c16, w, b):
    d = D_MODEL
    return pl.pallas_call(
        _ada_body,
        grid=(6,),
        in_specs=[pl.BlockSpec((MOD_ROWS, d), lambda j: (0, 0)),
                  pl.BlockSpec((d, d), lambda j: (0, j)),
                  pl.BlockSpec((1, d), lambda j: (0, j))],
        out_specs=pl.BlockSpec((MOD_ROWS, d), lambda j: (0, j)),
        out_shape=jax.ShapeDtypeStruct((MOD_ROWS, 6 * d), F32),
        compiler_params=_params(("parallel",)),
        name="ada_mod",
    )(c16, w, b)


def _sign_rows(n):
    lane = lax.broadcasted_iota(I32, (8, n), 1)
    sub = lax.broadcasted_iota(I32, (8, n), 0)
    sg = jnp.where((lane & 1) == 0, 1.0, -1.0)
    return jnp.where(sub == 0, sg, 0.0).astype(BF16)


def _filt_body(feat_ref, w1_ref, b1_ref, w2_ref, b2_ref, w3_ref, b3_ref, fq_ref, fof_ref, fob_ref, dec_ref,
               rot_ref, t1_ref, t2_ref, kf_ref, kn_ref, hh_ref, hl_ref):
    @pl.when((pl.program_id(0) == 0) & (pl.program_id(1) == 0))
    def _():
        fq = fq_ref[...]
        h = jnp.sin(fq * (_dot(w1_ref[...], feat_ref[...], HIGHEST) + b1_ref[...]))
        h = jnp.sin(fq * (_dot(w2_ref[...], h, HIGHEST) + b2_ref[...]))
        h = jnp.sin(fq * (_dot(w3_ref[...], h, HIGHEST) + b3_ref[...]))
        hh_ref[...], hl_ref[...] = _split_bf16(h.T)

    def taps(fo_ref):
        f_hi, f_lo = _split_bf16(fo_ref[...])
        return _dot(hh_ref[...], f_hi) + _dot(hl_ref[...], f_hi) + _dot(hh_ref[...], f_lo)

    dec = dec_ref[...]
    hf = taps(fof_ref) * dec
    hb = taps(fob_ref) * dec
    row = lax.broadcasted_iota(I32, hf.shape, 0)
    hb = jnp.where(row == 0, 0.0, hb)
    a = hf + hb
    b = hf - hb
    pa = (a[:HALF] + a[HALF:]).astype(BF16)
    ma = (a[:HALF] - a[HALF:]).astype(BF16)
    pb = (b[:HALF] + b[HALF:]).astype(BF16)
    mb = (b[:HALF] - b[HALF:]).astype(BF16)
    t1 = t1_ref[...]
    t2 = t2_ref[...]
    a1 = _dot(t1, pa)
    a2 = _dot(t2, ma)
    b1 = _dot(t1, pb)
    b2 = _dot(t2, mb)
    ce, se, co, so = rot_ref[0], rot_ref[1], rot_ref[2], rot_ref[3]
    kf_ref[0, 0] = a1[:HALF] * ce + a2[HALF:] * se
    kf_ref[0, 1] = b2[HALF:] * ce - b1[:HALF] * se
    kf_ref[0, 2] = a2[:HALF] * co + a1[HALF:] * so
    kf_ref[0, 3] = b1[HALF:] * co - b2[:HALF] * so
    kn_ref[0] = _dot(_sign_rows(HALF), ma)[0:1] * (1.0 / N_FFT)


def _filters(feat, w1, b1, w2, b2, w3, b3, fq, fout, decay, rot, t1, t2):
    nct = HYENA_W // CT
    full = lambda shape: pl.BlockSpec(shape, lambda o, c: (0,) * len(shape))
    return pl.pallas_call(
        _filt_body,
        grid=(HYENA_ORDER, nct),
        in_specs=[full((FILTER_HIDDEN, SEQ)), full((FILTER_HIDDEN, FILTER_HIDDEN)), full((FILTER_HIDDEN, 1)),
                  full((FILTER_HIDDEN, FILTER_HIDDEN)), full((FILTER_HIDDEN, 1)),
                  full((FILTER_HIDDEN, FILTER_HIDDEN)), full((FILTER_HIDDEN, 1)), full((FILTER_HIDDEN, 1)),
                  pl.BlockSpec((FILTER_HIDDEN, CT), lambda o, c: (0, (o * 2 + 0) * nct + c)),
                  pl.BlockSpec((FILTER_HIDDEN, CT), lambda o, c: (0, (o * 2 + 1) * nct + c)),
                  pl.BlockSpec((SEQ, CT), lambda o, c: (0, c)),
                  full((4, HALF, CT)),
                  pl.BlockSpec((SEQ, HALF), lambda o, c: (0, 0), pipeline_mode=pl.Buffered(1)),
                  pl.BlockSpec((SEQ, HALF), lambda o, c: (0, 0), pipeline_mode=pl.Buffered(1))],
        out_specs=[pl.BlockSpec((1, 4, HALF, CT), lambda o, c: (o, 0, 0, c)),
                   pl.BlockSpec((1, 1, CT), lambda o, c: (o, 0, c))],
        out_shape=[jax.ShapeDtypeStruct((HYENA_ORDER, 4, HALF, HYENA_W), F32),
                   jax.ShapeDtypeStruct((HYENA_ORDER, 1, HYENA_W), F32)],
        scratch_shapes=[pltpu.VMEM((SEQ, FILTER_HIDDEN), BF16)] * 2,
        compiler_params=_params(("arbitrary", "arbitrary")),
        name="hyena_filters",
    )(feat, w1, b1, w2, b2, w3, b3, fq, fout, fout, decay, rot, t1, t2)


def _pair_head_lanes(z):
    slabs = [z[:, s * LANES:(s + 1) * LANES] for s in range(ATTN_W // LANES)]
    swapped = [pltpu.roll(sl, HEAD_DIM, axis=1) for sl in slabs]
    low = lax.broadcasted_iota(I32, slabs[0].shape, 1) < HEAD_DIM
    out = []
    for j in range(GROUP):
        first, second = j, j + GROUP
        lo_src = slabs[first // 2] if first % 2 == 0 else swapped[first // 2]
        hi_src = slabs[second // 2] if second % 2 == 1 else swapped[second // 2]
        out.append(jnp.where(low, lo_src, hi_src))
    return jnp.concatenate(out, axis=1)


def _inproj_body(x_ref, sc_ref, sh_ref, n1_ref, wh_ref, wq_ref, wkv_ref, wg_ref, gq_ref, gk_ref, gmat_ref,
                 cos_ref, sin_ref, zh_ref, q_ref, k_ref, v_ref, gate_ref):
    groups = [slice(i * SUB_IN, (i + 1) * SUB_IN) for i in range(TM_IN // SUB_IN)]
    hx = [_rms_mod(x_ref[0, s, :], n1_ref[...], sc_ref[0], sh_ref[0]).astype(BF16) for s in groups]
    for s, h in zip(groups, hx):
        zh_ref[0, s, :] = _dot(h, wh_ref[...]).astype(BF16)
    pair = 2 * LANES
    zq = [_pair_head_lanes(_dot(h, wq_ref[...])) for h in hx]
    for s, z in zip(groups, zq):
        for c in range(ATTN_W // pair):
            sl = slice(c * pair, (c + 1) * pair)
            q_ref[0, s, sl] = _head_norm_rope(z[:, sl], gq_ref[:, sl], gmat_ref[...], cos_ref[s, :], sin_ref[s, :],
                                              LOG2E * HEAD_DIM ** -0.5).astype(BF16)
    zkv = [_dot(h, wkv_ref[...]) for h in hx]
    for s, z in zip(groups, zkv):
        k_ref[0, s, :] = _head_norm_rope(z[:, :KV_W], gk_ref[...], gmat_ref[0:KV_W, 0:KV_W], cos_ref[s, :],
                                         sin_ref[s, :], 1.0).astype(BF16)
        v_ref[0, s, :] = z[:, KV_W:].astype(BF16)
    for s, h in zip(groups, hx):
        gate_ref[0, s, :] = jax.nn.sigmoid(_dot(h, wg_ref[...])).astype(BF16)


def _pack_in_weights(w):
    wb = w.astype(BF16)
    return jnp.concatenate([wb[:, :OFF_K], wb[:, OFF_G:], wb[:, OFF_K:OFF_G]], axis=1)


W_OFF_H = 0
W_OFF_Q = OFF_Q
W_OFF_G = OFF_Q + ATTN_W
W_OFF_KV = W_OFF_G + 2 * D_MODEL


def _inproj(x, mods3, norm1, w_packed, gq, gk, gmat, cos_t, sin_t):
    d = D_MODEL
    nt = SEQ // TM_IN
    const = lambda shape: pl.BlockSpec(shape, lambda b, i: (0,) * len(shape))

    def wcol(width, off):
        assert off % width == 0
        return pl.BlockSpec((d, width), lambda b, i: (0, off // width))

    tok = lambda w: pl.BlockSpec((1, TM_IN, w), lambda b, i: (b, i, 0))
    return pl.pallas_call(
        _inproj_body,
        grid=(BATCH, nt),
        in_specs=[tok(d),
                  pl.BlockSpec((1, 1, d), lambda b, i: (b, 0, 1)),
                  pl.BlockSpec((1, 1, d), lambda b, i: (b, 0, 0)),
                  const((1, d)), wcol(OFF_Q, W_OFF_H), wcol(ATTN_W, W_OFF_Q), wcol(2 * KV_W, W_OFF_KV),
                  wcol(2 * d, W_OFF_G), const((1, ATTN_W)), const((1, KV_W)), const((2 * LANES, 2 * LANES)),
                  pl.BlockSpec((TM_IN, LANES), lambda b, i: (i, 0)),
                  pl.BlockSpec((TM_IN, LANES), lambda b, i: (i, 0))],
        out_specs=[tok(OFF_Q), tok(ATTN_W), tok(KV_W), tok(KV_W), tok(2 * d)],
        out_shape=[jax.ShapeDtypeStruct((BATCH, SEQ, OFF_Q), BF16),
                   jax.ShapeDtypeStruct((BATCH, SEQ, ATTN_W), BF16),
                   jax.ShapeDtypeStruct((BATCH, SEQ, KV_W), BF16),
                   jax.ShapeDtypeStruct((BATCH, SEQ, KV_W), BF16),
                   jax.ShapeDtypeStruct((BATCH, SEQ, 2 * d), BF16)],
        compiler_params=_params(("parallel", "parallel")),
        name="in_proj",
    )(x, mods3, mods3, norm1, w_packed, w_packed, w_packed, w_packed, gq, gk, gmat, cos_t, sin_t)


def _ctx_body(c_ref, sc_ref, sh_ref, n1_ref, wkv_ref, gk_ref, gmat_ref, kc_ref, vc_ref):
    for i in range(CTX_STEP):
        hc = _rms_mod(c_ref[i], n1_ref[...], sc_ref[0], sh_ref[0]).astype(BF16)
        z = _dot(hc, wkv_ref[...])
        kc_ref[i] = _head_norm_rope(z[:, :KV_W], gk_ref[...], gmat_ref[0:KV_W, 0:KV_W], None, None, 1.0).astype(BF16)
        vc_ref[i] = z[:, KV_W:].astype(BF16)


def _ctx_proj(ctx, mods3, norm1, w_packed, gk, gmat):
    d = D_MODEL
    const = lambda shape: pl.BlockSpec(shape, lambda b: (0,) * len(shape))
    return pl.pallas_call(
        _ctx_body,
        grid=(BATCH // CTX_STEP,),
        in_specs=[pl.BlockSpec((CTX_STEP, CTX_LEN, d), lambda b: (b, 0, 0)),
                  pl.BlockSpec((1, 1, d), lambda b: (BATCH, 0, 1)),
                  pl.BlockSpec((1, 1, d), lambda b: (BATCH, 0, 0)),
                  const((1, d)), pl.BlockSpec((d, 2 * KV_W), lambda b: (0, W_OFF_KV // (2 * KV_W))),
                  const((1, KV_W)), const((2 * LANES, 2 * LANES))],
        out_specs=[pl.BlockSpec((CTX_STEP, CTX_LEN, KV_W), lambda b: (b, 0, 0)),
                   pl.BlockSpec((CTX_STEP, CTX_LEN, KV_W), lambda b: (b, 0, 0))],
        out_shape=[jax.ShapeDtypeStruct((BATCH, CTX_LEN, KV_W), BF16),
                   jax.ShapeDtypeStruct((BATCH, CTX_LEN, KV_W), BF16)],
        compiler_params=_params(("parallel",)),
        name="ctx_proj",
    )(ctx, mods3, mods3, norm1, w_packed, gk, gmat)


def _hyena_body(zv_ref, z1_ref, z2_ref, cwv_ref, cw1_ref, cw2_ref, cbv_ref, cb1_ref, cb2_ref, hb_ref, kf_ref,
                kn_ref, t1_ref, t2_ref, t1t_ref, t2t_ref, o_ref, lo_ref, hi_ref, glo_ref, ghi_ref, p_ref, m_ref,
                za_ref, zb_ref):
    row = lax.broadcasted_iota(I32, (HALF, CT), 0)
    rr = lax.broadcasted_iota(I32, (RB, RB), 0)
    cc = lax.broadcasted_iota(I32, (RB, RB), 1)
    flip = jnp.where(rr + cc == RB - 1, 1.0, 0.0).astype(BF16)
    nrb = HALF // RB

    def folded_short_conv(z_ref, w_ref, b_ref, lo_out, hi_out):
        zlo = z_ref[0, 0:HALF, :].astype(F32)
        for j in range(nrb):
            hi_out[j * RB:(j + 1) * RB, :] = _dot(flip, z_ref[0, SEQ - RB * (j + 1):SEQ - RB * j, :])
        zhi = hi_out[...]
        w0, w1, w2 = w_ref[0:1, :], w_ref[1:2, :], w_ref[2:3, :]
        first, last = row == 0, row == HALF - 1
        lo_prev = jnp.where(first, 0.0, pltpu.roll(zlo, 1, axis=0))
        lo_next = jnp.where(last, zhi[HALF - 1:HALF, :], pltpu.roll(zlo, HALF - 1, axis=0))
        hi_prev = jnp.where(first, 0.0, pltpu.roll(zhi, 1, axis=0))
        hi_next = jnp.where(last, zlo[HALF - 1:HALF, :], pltpu.roll(zhi, HALF - 1, axis=0))
        lo_out[...] = lo_prev * w0 + zlo * w1 + lo_next * w2 + b_ref[...]
        hi_out[...] = hi_next * w0 + zhi * w1 + hi_prev * w2 + b_ref[...]

    folded_short_conv(zv_ref, cwv_ref, cbv_ref, lo_ref, hi_ref)
    sign8 = _sign_rows(HALF)
    odd = (lax.broadcasted_iota(I32, (FC, CT), 0) & 1) == 1
    for o, (zr, cw, cb) in enumerate(((z1_ref, cw1_ref, cb1_ref), (z2_ref, cw2_ref, cb2_ref))):
        folded_short_conv(zr, cw, cb, glo_ref, ghi_ref)
        p_ref[...] = (lo_ref[...] + hi_ref[...]).astype(BF16)
        m_ref[...] = (lo_ref[...] - hi_ref[...]).astype(BF16)
        pv = p_ref[...]
        mv = m_ref[...]
        for c in range(HALF // FC):
            ev = slice(c * FC, (c + 1) * FC)
            od = slice(HALF + c * FC, HALF + (c + 1) * FC)
            xce = _dot(t1_ref[ev, :], pv)
            xso = _dot(t1_ref[od, :], pv)
            xco = _dot(t2_ref[ev, :], mv)
            xse = _dot(t2_ref[od, :], mv)
            kce, kse, kco, kso = kf_ref[o, 0, ev, :], kf_ref[o, 1, ev, :], kf_ref[o, 2, ev, :], kf_ref[o, 3, ev, :]
            za_ref[ev, :] = (xce * kce - xse * kse).astype(BF16)
            za_ref[od, :] = (xco * kso + xso * kco).astype(BF16)
            zb_ref[ev, :] = (xco * kco - xso * kso).astype(BF16)
            zb_ref[od, :] = (xce * kse + xse * kce).astype(BF16)
        zn = _dot(sign8, mv)[0:1] * kn_ref[o]
        bias = hb_ref[o:o + 1, :]
        za = za_ref[...]
        zb = zb_ref[...]
        for c in range(HALF // FC):
            rs = slice(c * FC, (c + 1) * FC)
            half_p = _dot(t1t_ref[rs, :], za)
            half_m = _dot(t2t_ref[rs, :], zb) + jnp.where(odd, -zn, zn)
            lo_ref[rs, :] = glo_ref[rs, :] * (half_p + half_m + bias * lo_ref[rs, :])
            hi_ref[rs, :] = ghi_ref[rs, :] * (half_p - half_m + bias * hi_ref[rs, :])
    o_ref[0, 0:HALF, :] = lo_ref[...].astype(BF16)
    for j in range(nrb):
        o_ref[0, SEQ - RB * (j + 1):SEQ - RB * j, :] = _dot(
            flip, hi_ref[j * RB:(j + 1) * RB, :].astype(BF16)).astype(BF16)


def _hyena(zh, conv_w, conv_b, hbias, kf, kn, t1, t2, t1t, t2t):
    nct = HYENA_W // CT
    zspec = lambda k: pl.BlockSpec((1, SEQ, CT), lambda c, b: (b, 0, k * nct + c))
    wspec = lambda k: pl.BlockSpec((3, CT), lambda c, b: (0, k * nct + c))
    bspec = lambda k: pl.BlockSpec((1, CT), lambda c, b: (0, k * nct + c))
    table = lambda shape: pl.BlockSpec(shape, lambda c, b: (0, 0), pipeline_mode=pl.Buffered(1))
    half_f32 = pltpu.VMEM((HALF, CT), F32)
    return pl.pallas_call(
        _hyena_body,
        grid=(nct, BATCH),
        in_specs=[zspec(0), zspec(1), zspec(2), wspec(0), wspec(1), wspec(2), bspec(0), bspec(1), bspec(2),
                  pl.BlockSpec((HYENA_ORDER, CT), lambda c, b: (0, c)),
                  pl.BlockSpec((HYENA_ORDER, 4, HALF, CT), lambda c, b: (0, 0, 0, c)),
                  pl.BlockSpec((HYENA_ORDER, 1, CT), lambda c, b: (0, 0, c)),
                  table((SEQ, HALF)), table((SEQ, HALF)), table((HALF, SEQ)), table((HALF, SEQ))],
        out_specs=pl.BlockSpec((1, SEQ, CT), lambda c, b: (b, 0, c)),
        out_shape=jax.ShapeDtypeStruct((BATCH, SEQ, HYENA_W), BF16),
        scratch_shapes=[half_f32, half_f32, half_f32, half_f32,
                        pltpu.VMEM((HALF, CT), BF16), pltpu.VMEM((HALF, CT), BF16),
                        pltpu.VMEM((SEQ, CT), BF16), pltpu.VMEM((SEQ, CT), BF16)],
        compiler_params=_params(("parallel", "parallel")),
        name="hyena_conv",
    )(zh, zh, zh, conv_w, conv_w, conv_w, conv_b, conv_b, conv_b, hbias, kf, kn, t1, t2, t1t, t2t)


def _attn_body(sink_ref, q_ref, k_ref, v_ref, kc_ref, vc_ref, bias_ref, o_ref, kp_ref, vlo_ref, vhi_ref):
    nb = SEQ // BLOCK
    lane = lax.broadcasted_iota(I32, (BLOCK, LANES), 1)
    low = lane < HEAD_DIM
    mask_lo = jnp.where(low, 1.0, 0.0).astype(BF16)
    mask_hi = jnp.where(low, 0.0, 1.0).astype(BF16)

    def with_ones(v):
        lane_v = lax.broadcasted_iota(I32, v.shape, 1) < HEAD_DIM
        one = jnp.ones_like(v)
        return jnp.where(lane_v, v, one), jnp.where(lane_v, one, v)

    zpad = jnp.zeros((BLOCK, KV_W), BF16)
    kp_ref[0:BLOCK] = zpad
    kp_ref[BLOCK:BLOCK + SEQ] = k_ref[0]
    kp_ref[BLOCK + SEQ:] = zpad
    v_lo, v_hi = with_ones(v_ref[0])
    for ref, val in ((vlo_ref, v_lo), (vhi_ref, v_hi)):
        ref[0:BLOCK] = zpad
        ref[BLOCK:BLOCK + SEQ] = val
        ref[BLOCK + SEQ:] = zpad
    kc = kc_ref[0]
    vc_pair = with_ones(vc_ref[0])

    def block(n, carry):
        r = pl.multiple_of(n * BLOCK, BLOCK)
        kw = kp_ref[pl.ds(r, 3 * BLOCK), :]
        vw_pair = (vlo_ref[pl.ds(r, 3 * BLOCK), :], vhi_ref[pl.ds(r, 3 * BLOCK), :])
        bias = bias_ref[jnp.where(n == 0, 0, jnp.where(n == nb - 1, 2, 1))]
        scores = []
        for j in range(GROUP):
            qs = q_ref[0, pl.ds(r, BLOCK), j * LANES:(j + 1) * LANES]
            for msk in (mask_lo, mask_hi):
                qm = qs * msk
                scores.append((_dot_nt(qm, kw) + bias, _dot_nt(qm, kc)))
        probs = []
        for idx, (sw, sc) in enumerate(scores):
            snk = sink_ref[idx // 2 + GROUP * (idx % 2)]
            m = jnp.maximum(jnp.maximum(jnp.max(sw, axis=-1, keepdims=True),
                                        jnp.max(sc, axis=-1, keepdims=True)), snk)
            probs.append((jnp.exp2(sw - m).astype(BF16), jnp.exp2(sc - m).astype(BF16), jnp.exp2(snk - m)))
        outs = []
        for idx, (pw, pc, psink) in enumerate(probs):
            acc = _dot(pw, vw_pair[idx % 2]) + _dot(pc, vc_pair[idx % 2])
            den = pltpu.roll(acc, HEAD_DIM, axis=1) + psink
            outs.append(acc / den)
        for j in range(GROUP):
            o_ref[0, pl.ds(r, BLOCK), j * LANES:(j + 1) * LANES] = jnp.where(
                low, outs[2 * j], outs[2 * j + 1]).astype(BF16)
        return carry

    lax.fori_loop(0, nb, block, 0)


def _attention(sink, q, k, v, kc, vc, bias):
    per_b = lambda n, w: pl.BlockSpec((1, n, w), lambda b: (b, 0, 0))
    return pl.pallas_call(
        _attn_body,
        grid=(BATCH,),
        in_specs=[pl.BlockSpec(memory_space=pltpu.SMEM),
                  per_b(SEQ, ATTN_W), per_b(SEQ, KV_W), per_b(SEQ, KV_W), per_b(CTX_LEN, KV_W), per_b(CTX_LEN, KV_W),
                  pl.BlockSpec((3, BLOCK, 3 * BLOCK), lambda b: (0, 0, 0))],
        out_specs=per_b(SEQ, ATTN_W),
        out_shape=jax.ShapeDtypeStruct((BATCH, SEQ, ATTN_W), BF16),
        scratch_shapes=[pltpu.VMEM((SEQ + 2 * BLOCK, KV_W), BF16)] * 3,
        compiler_params=_params(("parallel",)),
        name="window_attn",
    )(sink, q, k, v, kc, vc, bias)


def _merge_body(ya_ref, yb_ref, gt_ref, x_ref, g1_ref, sc2_ref, sh2_ref, n2_ref, wa_ref, wb_ref, wo_ref, rt_ref,
                xn_ref, h2_ref, aff_ref):
    d = D_MODEL
    groups = [slice(i * SUB_MERGE, (i + 1) * SUB_MERGE) for i in range(TM_MERGE // SUB_MERGE)]
    branch = [(_dot(ya_ref[0, s, :], wa_ref[...]), _dot(yb_ref[0, s, :], wb_ref[...])) for s in groups]
    mixed = [(gt_ref[0, s, :d].astype(F32) * ua + gt_ref[0, s, d:].astype(F32) * ub).astype(BF16)
             for s, (ua, ub) in zip(groups, branch)]
    proj = [_dot(u, wo_ref[...]) for u in mixed]
    h2_rows = []
    for s, m in zip(groups, proj):
        xn = x_ref[0, s, :] + g1_ref[0] * m
        xn_ref[0, s, :] = xn
        h2 = _rms_mod(xn, n2_ref[...], sc2_ref[0], sh2_ref[0]).astype(BF16)
        h2_ref[0, s, :] = h2
        h2_rows.append(h2)
    logits = [_dot_nt(rt_ref[...], h2) for h2 in h2_rows]
    for s, lt in zip(groups, logits):
        e = jnp.exp(lt - jnp.max(lt, axis=0, keepdims=True))
        aff_ref[0, :, s] = e / jnp.sum(e, axis=0, keepdims=True)


def _merge(ya, yb, gates, x, mods3, norm2, wa, wb, wo, router_t):
    d = D_MODEL
    nt = SEQ // TM_MERGE
    const = lambda shape: pl.BlockSpec(shape, lambda b, i: (0,) * len(shape))
    tok = lambda w: pl.BlockSpec((1, TM_MERGE, w), lambda b, i: (b, i, 0))
    mod = lambda k: pl.BlockSpec((1, 1, d), lambda b, i: (b, 0, k))
    return pl.pallas_call(
        _merge_body,
        grid=(BATCH, nt),
        in_specs=[tok(HYENA_W), tok(ATTN_W), tok(2 * d), tok(d), mod(2), mod(4), mod(3), const((1, d)),
                  const((HYENA_W, d)), const((ATTN_W, d)), const((d, d)), const((N_EXPERTS, d))],
        out_specs=[tok(d), tok(d), pl.BlockSpec((1, N_EXPERTS, TM_MERGE), lambda b, i: (b, 0, i))],
        out_shape=[jax.ShapeDtypeStruct((BATCH, SEQ, d), F32),
                   jax.ShapeDtypeStruct((BATCH, SEQ, d), BF16),
                   jax.ShapeDtypeStruct((BATCH, N_EXPERTS, SEQ), F32)],
        compiler_params=_params(("parallel", "parallel")),
        name="merge_outproj",
    )(ya, yb, gates, x, mods3, mods3, mods3, norm2, wa, wb, wo, router_t)


def _thresh_body(aff_ref, lo_ref, hi_ref):
    aff = aff_ref[...]
    rows = aff.shape[0]
    bits = pltpu.bitcast(aff, I32)

    def bisect_bits(_, carry):
        lo, hi = carry
        mid = lo + ((hi - lo) >> 1)
        ge = jnp.sum((bits >= mid).astype(I32), axis=1, keepdims=True) >= CAP
        return jnp.where(ge, mid, lo), jnp.where(ge, hi, mid)

    lo0 = jnp.zeros((rows, 1), I32)
    hi0 = jnp.full((rows, 1), 0x3F800001, I32)
    thr_bits, _ = lax.fori_loop(0, 31, bisect_bits, (lo0, hi0))
    thr = pltpu.bitcast(thr_bits, F32)

    def bisect_val(_, carry):
        lo, hi = carry
        mid = 0.5 * (lo + hi)
        ge = jnp.sum(jnp.where(aff >= mid, 1.0, 0.0), axis=1, keepdims=True) >= CAP
        return jnp.where(ge, mid, lo), jnp.where(ge, hi, mid)

    lo, hi = lax.fori_loop(0, 30, bisect_val, (0.5 * thr, jnp.maximum(2.0 * thr, 1e-30)))
    lo_ref[...] = jnp.broadcast_to(lo, lo_ref.shape)
    hi_ref[...] = jnp.broadcast_to(hi, hi_ref.shape)


def _thresholds(aff_rows):
    rows = BATCH * N_EXPERTS
    return pl.pallas_call(
        _thresh_body,
        grid=(1,),
        in_specs=[pl.BlockSpec((rows, SEQ), lambda i: (0, 0))],
        out_specs=[pl.BlockSpec((rows, LANES), lambda i: (0, 0)), pl.BlockSpec((rows, LANES), lambda i: (0, 0))],
        out_shape=[jax.ShapeDtypeStruct((rows, LANES), F32), jax.ShapeDtypeStruct((rows, LANES), F32)],
        compiler_params=_params(("arbitrary",)),
        name="route_threshold",
    )(aff_rows)


def _prefix_counts(mask):
    r = lax.broadcasted_iota(I32, (LANES, LANES), 0)
    c = lax.broadcasted_iota(I32, (LANES, LANES), 1)
    upper = jnp.where(r <= c, 1.0, 0.0).astype(BF16)
    offset = jnp.zeros((mask.shape[0], 1), F32)
    blocks = []
    for j in range(mask.shape[1] // LANES):
        blk = mask[:, j * LANES:(j + 1) * LANES]
        inc = _dot(blk.astype(BF16), upper)
        blocks.append(inc - blk + offset)
        offset = offset + inc[:, LANES - 1:LANES]
    return jnp.concatenate(blocks, axis=1)


def _route_body(aff_ref, lo_ref, hi_ref, h_ref, slot_ref, xin_ref, pos_ref, pbuf_ref):
    aff = aff_ref[0]
    above = jnp.where(aff >= hi_ref[:, 0:1], 1.0, 0.0)
    band = jnp.where(aff >= lo_ref[:, 0:1], 1.0, 0.0) - above
    need = CAP - jnp.sum(above, axis=1, keepdims=True)
    tie_rank = _prefix_counts(band)
    self = above + band * jnp.where(tie_rank < need, 1.0, 0.0)
    pos = _prefix_counts(self)
    posi = jnp.where(self > 0.5, pos.astype(I32), -1)
    pos_ref[0] = posi

    h = h_ref[0]
    slot = slot_ref[...]
    posb = jnp.where(self > 0.5, pos, -1.0).astype(BF16)
    one = jnp.ones((CAP, SEQ), BF16)
    zero = jnp.zeros((CAP, SEQ), BF16)
    for grp in range(N_EXPERTS // EXPERT_GROUP):
        for i in range(EXPERT_GROUP):
            e = grp * EXPERT_GROUP + i
            pbuf_ref[i * CAP:(i + 1) * CAP, :] = jnp.where(posb[e:e + 1, :] == slot, one, zero)
        rows = _dot(pbuf_ref[...], h)
        for i in range(EXPERT_GROUP):
            xin_ref[grp * EXPERT_GROUP + i] = rows[i * CAP:(i + 1) * CAP].astype(BF16)


def _route(aff_t, lo, hi, h2, slot_rows):
    d = D_MODEL
    return pl.pallas_call(
        _route_body,
        grid=(BATCH,),
        in_specs=[pl.BlockSpec((1, N_EXPERTS, SEQ), lambda b: (b, 0, 0)),
                  pl.BlockSpec((N_EXPERTS, LANES), lambda b: (b, 0)),
                  pl.BlockSpec((N_EXPERTS, LANES), lambda b: (b, 0)),
                  pl.BlockSpec((1, SEQ, d), lambda b: (b, 0, 0)),
                  pl.BlockSpec((CAP, SEQ), lambda b: (0, 0))],
        out_specs=[pl.BlockSpec((N_EXPERTS, CAP, d), lambda b: (0, b, 0)),
                   pl.BlockSpec((1, N_EXPERTS, SEQ), lambda b: (b, 0, 0))],
        out_shape=[jax.ShapeDtypeStruct((N_EXPERTS, BATCH * CAP, d), BF16),
                   jax.ShapeDtypeStruct((BATCH, N_EXPERTS, SEQ), I32)],
        scratch_shapes=[pltpu.VMEM((EXPERT_GROUP * CAP, SEQ), BF16)],
        compiler_params=_params(("parallel",)),
        name="route_gather",
    )(aff_t, lo, hi, h2, slot_rows)


def _expert_body(x_ref, wg_ref, wu_ref, wd_ref, o_ref, acc_ref, wgb_ref, wub_ref, wdb_ref):
    f = pl.program_id(1)

    @pl.when((pl.program_id(0) == 0) & (f == 0))
    def _():
        acc_ref[...] = jnp.zeros_like(acc_ref)

    carry = f > 0
    for m in range(BATCH * CAP // MC):
        rows = slice(m * MC, (m + 1) * MC)
        xm = x_ref[0, rows, :]
        if m == 0:
            wgb_ref[...] = wg_ref[0].astype(BF16)
        a = _dot(xm, wgb_ref[...])
        if m == 0:
            wub_ref[...] = wu_ref[0].astype(BF16)
        b = _dot(xm, wub_ref[...])
        hh = (a * jax.nn.sigmoid(a) * b).astype(BF16)
        if m == 0:
            wdb_ref[...] = wd_ref[0].astype(BF16)
        acc_ref[rows, :] = jnp.where(carry, acc_ref[rows, :], 0.0) + _dot(hh, wdb_ref[...])

    @pl.when(f == D_FF // TF - 1)
    def _():
        o_ref[0] = acc_ref[...].astype(BF16)


def _experts(xin, w_gate, w_up, w_down):
    d = D_MODEL
    rows = BATCH * CAP
    return pl.pallas_call(
        _expert_body,
        grid=(N_EXPERTS, D_FF // TF),
        in_specs=[pl.BlockSpec((1, rows, d), lambda e, f: (e, 0, 0)),
                  pl.BlockSpec((1, d, TF), lambda e, f: (e, 0, f)),
                  pl.BlockSpec((1, d, TF), lambda e, f: (e, 0, f)),
                  pl.BlockSpec((1, TF, d), lambda e, f: (e, f, 0))],
        out_specs=pl.BlockSpec((1, rows, d), lambda e, f: (e, 0, 0)),
        out_shape=jax.ShapeDtypeStruct((N_EXPERTS, rows, d), BF16),
        scratch_shapes=[pltpu.VMEM((rows, d), F32), pltpu.VMEM((d, TF), BF16), pltpu.VMEM((d, TF), BF16),
                        pltpu.VMEM((TF, d), BF16)],
        compiler_params=_params(("parallel", "arbitrary")),
        name="swiglu_experts",
    )(xin, w_gate, w_up, w_down)


def _scatter_body(pos_ref, aff_ref, y_ref, x_ref, g2_ref, o_ref):
    pos = pos_ref[0].T
    aff = aff_ref[0].T
    slot = lax.broadcasted_iota(I32, (SUB_S, CAP), 1)
    groups = [slice(i * SUB_S, (i + 1) * SUB_S) for i in range(TS // SUB_S)]
    onehots = [jnp.concatenate([jnp.where(pos[s, e:e + 1] == slot, aff[s, e:e + 1], 0.0).astype(BF16)
                                for e in range(N_EXPERTS)], axis=1) for s in groups]
    y = y_ref[...].reshape(N_EXPERTS * CAP, D_MODEL)
    mixed = [_dot(oh, y) for oh in onehots]
    for s, m in zip(groups, mixed):
        o_ref[0, s, :] = x_ref[0, s, :] + g2_ref[0] * m


def _scatter(pos, aff_t, y, xn, mods3):
    d = D_MODEL
    return pl.pallas_call(
        _scatter_body,
        grid=(BATCH, SEQ // TS),
        in_specs=[pl.BlockSpec((1, N_EXPERTS, TS), lambda b, i: (b, 0, i)),
                  pl.BlockSpec((1, N_EXPERTS, TS), lambda b, i: (b, 0, i)),
                  pl.BlockSpec((N_EXPERTS, CAP, d), lambda b, i: (0, b, 0)),
                  pl.BlockSpec((1, TS, d), lambda b, i: (b, i, 0)),
                  pl.BlockSpec((1, 1, d), lambda b, i: (b, 0, 5))],
        out_specs=pl.BlockSpec((1, TS, d), lambda b, i: (b, i, 0)),
        out_shape=jax.ShapeDtypeStruct((BATCH, SEQ, d), F32),
        compiler_params=_params(("parallel", "parallel")),
        name="scatter_residual",
    )(pos, aff_t, y, xn, mods3)


def _rope_tables():
    rows = SEQ // GRID_W
    row = np.repeat(np.arange(rows, dtype=np.float32), GRID_W)
    col = np.tile(np.arange(GRID_W, dtype=np.float32), rows)
    inv = (ROPE_BASE ** (-np.arange(0, AXIS_ROT, 2, dtype=np.float32) / AXIS_ROT)).astype(np.float32)
    ang = np.concatenate([row[:, None] * inv, col[:, None] * inv], axis=-1).astype(np.float64)
    cos = np.repeat(np.cos(ang), 2, axis=-1)
    sin = np.stack([-np.sin(ang), np.sin(ang)], axis=-1).reshape(SEQ, HEAD_DIM)
    reps = LANES // HEAD_DIM
    return jnp.asarray(np.tile(cos, (1, reps)), F32), jnp.asarray(np.tile(sin, (1, reps)), F32)


def _dft_tables():
    idx = np.arange(HALF, dtype=np.int64)
    t2p1 = 2 * idx + 1

    def cos_sin(f):
        ang = ((f[:, None] * t2p1[None, :]) % (2 * N_FFT)) * (math.pi / N_FFT)
        return np.cos(ang), np.sin(ang)

    ce, se = cos_sin(2 * idx)
    co, so = cos_sin(2 * idx + 1)
    t1 = jnp.asarray(np.concatenate([ce, so], axis=0), F32).astype(BF16)
    t2 = jnp.asarray(np.concatenate([co, se], axis=0), F32).astype(BF16)
    return t1, t2, t1.T, t2.T


def _phase_tables():
    idx = np.arange(HALF, dtype=np.float64)
    w = np.full((HALF,), 2.0 / N_FFT)
    we = w.copy()
    we[0] = 1.0 / N_FFT
    pe = (math.pi / N_FFT) * (2.0 * idx)
    po = (math.pi / N_FFT) * (2.0 * idx + 1.0)
    rot = np.stack([we * np.cos(pe), we * np.sin(pe), w * np.cos(po), w * np.sin(po)])
    return jnp.asarray(np.broadcast_to(rot[:, :, None], (4, HALF, CT)), F32)


def _fold_rows(a):
    return np.concatenate([a[:HALF], a[HALF:][::-1]], axis=0)


def _filter_features():
    t = np.linspace(0.0, 1.0, SEQ, dtype=np.float32).astype(np.float64)[:, None]
    w = 2.0 * math.pi * np.arange(SEQ, dtype=np.float64)[:, None] / SEQ
    fr = np.linspace(1e-4, FILTER_BANDS - 1, FILTER_BANDS, dtype=np.float32).astype(np.float64)[None, :]
    feat = np.concatenate([t, np.cos(fr * w), -np.sin(fr * w)], axis=-1)
    feat = np.pad(feat, ((0, 0), (0, FILTER_HIDDEN - FILTER_EMB)))
    min_decay = math.log(DECAY_TARGET) / SLOW_DECAY_PCT
    max_decay = math.log(DECAY_TARGET) / FAST_DECAY_PCT
    deltas = np.linspace(min_decay, max_decay, HYENA_W, dtype=np.float32).astype(np.float64)
    decay = np.exp(-t * np.abs(deltas))
    return jnp.asarray(_fold_rows(feat).T, F32), jnp.asarray(_fold_rows(decay), F32)


def _attn_bias():
    qi = np.arange(BLOCK)[:, None]
    kj = np.arange(3 * BLOCK)[None, :]
    band = np.abs(kj - BLOCK - qi) <= WINDOW
    first = band & (kj >= BLOCK)
    last = band & (kj < 2 * BLOCK)
    return jnp.asarray(np.where(np.stack([first, band, last]), 0.0, NEG), F32)


def _pair_heads(w, axis):
    heads = [lax.slice_in_dim(w, h * HEAD_DIM, (h + 1) * HEAD_DIM, axis=axis) for h in range(N_HEADS)]
    return jnp.concatenate([heads[j + GROUP * half] for j in range(GROUP) for half in range(N_KV_HEADS)], axis=axis)


def kernel(x, c, ctx, c_ctx, ada_w, ada_b, norm1, norm2, w_in, conv_w, conv_b, filt_w1, filt_b1, filt_w2, filt_b2,
           filt_w3, filt_b3, filt_freq, filt_out, hyena_bias, q_norm, k_norm, attn_sink, w_branch_a, w_branch_b,
           w_out, router, w_gate, w_up, w_down):
    d = D_MODEL
    assert ada_w.shape[0] == 1, "only the single-layer configuration is implemented"
    l = 0
    cos_t, sin_t = _rope_tables()
    t1, t2, t1t, t2t = _dft_tables()
    rot = _phase_tables()
    feat, decay = _filter_features()
    bias = _attn_bias()
    gmat = jnp.asarray(np.kron(np.eye(2 * LANES // HEAD_DIM), np.full((HEAD_DIM, HEAD_DIM), 1.0 / HEAD_DIM)), BF16)
    c16 = jnp.concatenate([c, c_ctx[None, :], jnp.zeros((MOD_ROWS - BATCH - 1, d), F32)], axis=0)

    mods3 = _ada(c16, ada_w[l], ada_b[l][None, :]).reshape(MOD_ROWS, 1, 6 * d)
    n1 = norm1[l][None, :]
    w_packed = _pack_in_weights(w_in[l])
    gk = jnp.tile(k_norm[l], N_KV_HEADS)[None, :]
    kf, kn = _filters(feat, jnp.pad(filt_w1[l].T, ((0, 0), (0, FILTER_HIDDEN - FILTER_EMB))), filt_b1[l][:, None],
                      filt_w2[l].T, filt_b2[l][:, None], filt_w3[l].T, filt_b3[l][:, None], filt_freq[l][:, None],
                      filt_out[l], decay, rot, t1, t2)
    kc, vc = _ctx_proj(ctx, mods3, n1, w_packed, gk, gmat)
    zh, q, k, v, gates = _inproj(x, mods3, n1, w_packed, jnp.tile(q_norm[l], N_HEADS)[None, :], gk, gmat, cos_t, sin_t)
    ya = _hyena(zh, conv_w[l], conv_b[l][None, :], hyena_bias[l], kf, kn, t1, t2, t1t, t2t)
    yb = _attention(attn_sink[l] * LOG2E, q, k, v, kc, vc, bias)
    xn, h2, aff_t = _merge(ya, yb, gates, x, mods3, norm2[l][None, :], w_branch_a[l].astype(BF16),
                           _pair_heads(w_branch_b[l], 0).astype(BF16), w_out[l].astype(BF16), router[l].T.astype(BF16))
    lo, hi = _thresholds(aff_t.reshape(BATCH * N_EXPERTS, SEQ))
    slot_rows = jnp.asarray(np.broadcast_to(np.arange(CAP)[:, None], (CAP, SEQ)), BF16)
    xin, pos = _route(aff_t, lo, hi, h2, slot_rows)
    y = _experts(xin, w_gate[l], w_up[l], w_down[l])
    return _scatter(pos, aff_t, y, xn, mods3)
```

```python
import math

import numpy as np
import jax
import jax.numpy as jnp
from jax import lax
from jax.experimental import pallas as pl
from jax.experimental.pallas import tpu as pltpu

F32 = jnp.float32
BF16 = jnp.bfloat16
I32 = jnp.int32
HIGHEST = lax.Precision.HIGHEST

D_MODEL = 1024
BATCH = 8
SEQ = 2048
GRID_W = 64
CTX_LEN = 256
N_HEADS = 8
N_KV_HEADS = 2
HEAD_DIM = 64
GROUP = N_HEADS // N_KV_HEADS
ATTN_W = N_HEADS * HEAD_DIM
KV_W = N_KV_HEADS * HEAD_DIM
WINDOW = 128
BLOCK = 128
HYENA_W = D_MODEL // 2
HYENA_ORDER = 2
FILTER_BANDS = 16
FILTER_EMB = 1 + 2 * FILTER_BANDS
FILTER_HIDDEN = 64
DECAY_TARGET = 1e-2
FAST_DECAY_PCT = 0.3
SLOW_DECAY_PCT = 1.5
ROPE_BASE = 10000.0
AXIS_ROT = HEAD_DIM // 2
N_EXPERTS = 16
EC_CAPACITY = 2
D_FF = 2048
EPS = 1e-6
NEG = -1e30
LOG2E = math.log2(math.e)

OFF_Q = 3 * HYENA_W
OFF_K = OFF_Q + ATTN_W
OFF_V = OFF_K + KV_W
OFF_G = OFF_V + KV_W
IN_W = OFF_G + 2 * D_MODEL

CAP = EC_CAPACITY * SEQ // N_EXPERTS
N_FFT = 2 * SEQ
HALF = SEQ // 2
MOD_ROWS = 16
LANES = 128

TM_IN = 1024
SUB_IN = 256
TM_MERGE = 1024
SUB_MERGE = 512
CT = 256
FC = 512
RB = 256
TF = 512
MC = 512
TS = 1024
SUB_S = 512
CTX_STEP = 4
EXPERT_GROUP = 4
VMEM_LIMIT = 56 * 1024 * 1024


def _dot(a, b, precision=None):
    return jnp.dot(a, b, preferred_element_type=F32, precision=precision)


def _dot_nt(a, b, precision=None):
    return lax.dot_general(a, b, (((1,), (1,)), ((), ())), preferred_element_type=F32, precision=precision)


def _params(sem, vmem=VMEM_LIMIT):
    return pltpu.CompilerParams(dimension_semantics=sem, vmem_limit_bytes=vmem)


def _rms_mod(x, g, sc, sh):
    ms = jnp.mean(x * x, axis=-1, keepdims=True)
    return (x * lax.rsqrt(ms + EPS) * g) * (1.0 + sc) + sh


def _head_norm_rope(z, g, gmat, cos, sin, scale):
    ms = _dot((z * z).astype(BF16), gmat)
    y = z * lax.rsqrt(ms + EPS) * g
    if cos is not None:
        slabs = []
        for s in range(z.shape[1] // LANES):
            ys = y[:, s * LANES:(s + 1) * LANES]
            lane = lax.broadcasted_iota(I32, ys.shape, 1)
            nxt = pltpu.roll(ys, LANES - 1, axis=1)
            prv = pltpu.roll(ys, 1, axis=1)
            slabs.append(ys * cos + jnp.where((lane & 1) == 0, nxt, prv) * sin)
        y = slabs[0] if len(slabs) == 1 else jnp.concatenate(slabs, axis=1)
    return y * scale


def _split_bf16(x):
    hi = x.astype(BF16)
    return hi, (x - hi.astype(F32)).astype(BF16)


def _ada_body(c_ref, w_ref, b_ref, o_ref):
    c = c_ref[...]
    s_hi, s_lo = _split_bf16(c * jax.nn.sigmoid(c))
    w_hi, w_lo = _split_bf16(w_ref[...])
    o_ref[...] = _dot(s_hi, w_hi) + _dot(s_lo, w_hi) + _dot(s_hi, w_lo) + b_ref[...]


def _ada(c16, w, b):
    d = D_MODEL
    return pl.pallas_call(
        _ada_body,
        grid=(6,),
        in_specs=[pl.BlockSpec((MOD_ROWS, d), lambda j: (0, 0)),
                  pl.BlockSpec((d, d), lambda j: (0, j)),
                  pl.BlockSpec((1, d), lambda j: (0, j))],
        out_specs=pl.BlockSpec((MOD_ROWS, d), lambda j: (0, j)),
        out_shape=jax.ShapeDtypeStruct((MOD_ROWS, 6 * d), F32),
        compiler_params=_params(("parallel",)),
        name="ada_mod",
    )(c16, w, b)


def _sign_rows(n):
    lane = lax.broadcasted_iota(I32, (8, n), 1)
    sub = lax.broadcasted_iota(I32, (8, n), 0)
    sg = jnp.where((lane & 1) == 0, 1.0, -1.0)
    return jnp.where(sub == 0, sg, 0.0).astype(BF16)


def _filt_body(feat_ref, w1_ref, b1_ref, w2_ref, b2_ref, w3_ref, b3_ref, fq_ref, fof_ref, fob_ref, dec_ref,
               rot_ref, t1_ref, t2_ref, kf_ref, kn_ref, hh_ref, hl_ref):
    @pl.when((pl.program_id(0) == 0) & (pl.program_id(1) == 0))
    def _():
        fq = fq_ref[...]
        h = jnp.sin(fq * (_dot(w1_ref[...], feat_ref[...], HIGHEST) + b1_ref[...]))
        h = jnp.sin(fq * (_dot(w2_ref[...], h, HIGHEST) + b2_ref[...]))
        h = jnp.sin(fq * (_dot(w3_ref[...], h, HIGHEST) + b3_ref[...]))
        hh_ref[...], hl_ref[...] = _split_bf16(h.T)

    def taps(fo_ref):
        f_hi, f_lo = _split_bf16(fo_ref[...])
        return _dot(hh_ref[...], f_hi) + _dot(hl_ref[...], f_hi) + _dot(hh_ref[...], f_lo)

    dec = dec_ref[...]
    hf = taps(fof_ref) * dec
    hb = taps(fob_ref) * dec
    row = lax.broadcasted_iota(I32, hf.shape, 0)
    hb = jnp.where(row == 0, 0.0, hb)
    a = hf + hb
    b = hf - hb
    pa = (a[:HALF] + a[HALF:]).astype(BF16)
    ma = (a[:HALF] - a[HALF:]).astype(BF16)
    pb = (b[:HALF] + b[HALF:]).astype(BF16)
    mb = (b[:HALF] - b[HALF:]).astype(BF16)
    t1 = t1_ref[...]
    t2 = t2_ref[...]
    a1 = _dot(t1, pa)
    a2 = _dot(t2, ma)
    b1 = _dot(t1, pb)
    b2 = _dot(t2, mb)
    ce, se, co, so = rot_ref[0], rot_ref[1], rot_ref[2], rot_ref[3]
    kf_ref[0, 0] = a1[:HALF] * ce + a2[HALF:] * se
    kf_ref[0, 1] = b2[HALF:] * ce - b1[:HALF] * se
    kf_ref[0, 2] = a2[:HALF] * co + a1[HALF:] * so
    kf_ref[0, 3] = b1[HALF:] * co - b2[:HALF] * so
    kn_ref[0] = _dot(_sign_rows(HALF), ma)[0:1] * (1.0 / N_FFT)


def _filters(feat, w1, b1, w2, b2, w3, b3, fq, fout, decay, rot, t1, t2):
    nct = HYENA_W // CT
    full = lambda shape: pl.BlockSpec(shape, lambda o, c: (0,) * len(shape))
    return pl.pallas_call(
        _filt_body,
        grid=(HYENA_ORDER, nct),
        in_specs=[full((FILTER_HIDDEN, SEQ)), full((FILTER_HIDDEN, FILTER_HIDDEN)), full((FILTER_HIDDEN, 1)),
                  full((FILTER_HIDDEN, FILTER_HIDDEN)), full((FILTER_HIDDEN, 1)),
                  full((FILTER_HIDDEN, FILTER_HIDDEN)), full((FILTER_HIDDEN, 1)), full((FILTER_HIDDEN, 1)),
                  pl.BlockSpec((FILTER_HIDDEN, CT), lambda o, c: (0, (o * 2 + 0) * nct + c)),
                  pl.BlockSpec((FILTER_HIDDEN, CT), lambda o, c: (0, (o * 2 + 1) * nct + c)),
                  pl.BlockSpec((SEQ, CT), lambda o, c: (0, c)),
                  full((4, HALF, CT)),
                  pl.BlockSpec((SEQ, HALF), lambda o, c: (0, 0), pipeline_mode=pl.Buffered(1)),
                  pl.BlockSpec((SEQ, HALF), lambda o, c: (0, 0), pipeline_mode=pl.Buffered(1))],
        out_specs=[pl.BlockSpec((1, 4, HALF, CT), lambda o, c: (o, 0, 0, c)),
                   pl.BlockSpec((1, 1, CT), lambda o, c: (o, 0, c))],
        out_shape=[jax.ShapeDtypeStruct((HYENA_ORDER, 4, HALF, HYENA_W), F32),
                   jax.ShapeDtypeStruct((HYENA_ORDER, 1, HYENA_W), F32)],
        scratch_shapes=[pltpu.VMEM((SEQ, FILTER_HIDDEN), BF16)] * 2,
        compiler_params=_params(("arbitrary", "arbitrary")),
        name="hyena_filters",
    )(feat, w1, b1, w2, b2, w3, b3, fq, fout, fout, decay, rot, t1, t2)


def _pair_head_lanes(z):
    slabs = [z[:, s * LANES:(s + 1) * LANES] for s in range(ATTN_W // LANES)]
    swapped = [pltpu.roll(sl, HEAD_DIM, axis=1) for sl in slabs]
    low = lax.broadcasted_iota(I32, slabs[0].shape, 1) < HEAD_DIM
    out = []
    for j in range(GROUP):
        first, second = j, j + GROUP
        lo_src = slabs[first // 2] if first % 2 == 0 else swapped[first // 2]
        hi_src = slabs[second // 2] if second % 2 == 1 else swapped[second // 2]
        out.append(jnp.where(low, lo_src, hi_src))
    return jnp.concatenate(out, axis=1)


def _inproj_body(x_ref, sc_ref, sh_ref, n1_ref, wh_ref, wq_ref, wkv_ref, wg_ref, gq_ref, gk_ref, gmat_ref,
                 cos_ref, sin_ref, zh_ref, q_ref, k_ref, v_ref, gate_ref):
    groups = [slice(i * SUB_IN, (i + 1) * SUB_IN) for i in range(TM_IN // SUB_IN)]
    hx = [_rms_mod(x_ref[0, s, :], n1_ref[...], sc_ref[0], sh_ref[0]).astype(BF16) for s in groups]
    for s, h in zip(groups, hx):
        zh_ref[0, s, :] = _dot(h, wh_ref[...]).astype(BF16)
    pair = 2 * LANES
    zq = [_pair_head_lanes(_dot(h, wq_ref[...])) for h in hx]
    for s, z in zip(groups, zq):
        for c in range(ATTN_W // pair):
            sl = slice(c * pair, (c + 1) * pair)
            q_ref[0, s, sl] = _head_norm_rope(z[:, sl], gq_ref[:, sl], gmat_ref[...], cos_ref[s, :], sin_ref[s, :],
                                              LOG2E * HEAD_DIM ** -0.5).astype(BF16)
    zkv = [_dot(h, wkv_ref[...]) for h in hx]
    for s, z in zip(groups, zkv):
        k_ref[0, s, :] = _head_norm_rope(z[:, :KV_W], gk_ref[...], gmat_ref[0:KV_W, 0:KV_W], cos_ref[s, :],
                                         sin_ref[s, :], 1.0).astype(BF16)
        v_ref[0, s, :] = z[:, KV_W:].astype(BF16)
    for s, h in zip(groups, hx):
        gate_ref[0, s, :] = jax.nn.sigmoid(_dot(h, wg_ref[...])).astype(BF16)


def _pack_in_weights(w):
    wb = w.astype(BF16)
    return jnp.concatenate([wb[:, :OFF_K], wb[:, OFF_G:], wb[:, OFF_K:OFF_G]], axis=1)


W_OFF_H = 0
W_OFF_Q = OFF_Q
W_OFF_G = OFF_Q + ATTN_W
W_OFF_KV = W_OFF_G + 2 * D_MODEL


def _inproj(x, mods3, norm1, w_packed, gq, gk, gmat, cos_t, sin_t):
    d = D_MODEL
    nt = SEQ // TM_IN
    const = lambda shape: pl.BlockSpec(shape, lambda b, i: (0,) * len(shape))

    def wcol(width, off):
        assert off % width == 0
        return pl.BlockSpec((d, width), lambda b, i: (0, off // width))

    tok = lambda w: pl.BlockSpec((1, TM_IN, w), lambda b, i: (b, i, 0))
    return pl.pallas_call(
        _inproj_body,
        grid=(BATCH, nt),
        in_specs=[tok(d),
                  pl.BlockSpec((1, 1, d), lambda b, i: (b, 0, 1)),
                  pl.BlockSpec((1, 1, d), lambda b, i: (b, 0, 0)),
                  const((1, d)), wcol(OFF_Q, W_OFF_H), wcol(ATTN_W, W_OFF_Q), wcol(2 * KV_W, W_OFF_KV),
                  wcol(2 * d, W_OFF_G), const((1, ATTN_W)), const((1, KV_W)), const((2 * LANES, 2 * LANES)),
                  pl.BlockSpec((TM_IN, LANES), lambda b, i: (i, 0)),
                  pl.BlockSpec((TM_IN, LANES), lambda b, i: (i, 0))],
        out_specs=[tok(OFF_Q), tok(ATTN_W), tok(KV_W), tok(KV_W), tok(2 * d)],
        out_shape=[jax.ShapeDtypeStruct((BATCH, SEQ, OFF_Q), BF16),
                   jax.ShapeDtypeStruct((BATCH, SEQ, ATTN_W), BF16),
                   jax.ShapeDtypeStruct((BATCH, SEQ, KV_W), BF16),
                   jax.ShapeDtypeStruct((BATCH, SEQ, KV_W), BF16),
                   jax.ShapeDtypeStruct((BATCH, SEQ, 2 * d), BF16)],
        compiler_params=_params(("parallel", "parallel")),
        name="in_proj",
    )(x, mods3, mods3, norm1, w_packed, w_packed, w_packed, w_packed, gq, gk, gmat, cos_t, sin_t)


def _ctx_body(c_ref, sc_ref, sh_ref, n1_ref, wkv_ref, gk_ref, gmat_ref, kc_ref, vc_ref):
    for i in range(CTX_STEP):
        hc = _rms_mod(c_ref[i], n1_ref[...], sc_ref[0], sh_ref[0]).astype(BF16)
        z = _dot(hc, wkv_ref[...])
        kc_ref[i] = _head_norm_rope(z[:, :KV_W], gk_ref[...], gmat_ref[0:KV_W, 0:KV_W], None, None, 1.0).astype(BF16)
        vc_ref[i] = z[:, KV_W:].astype(BF16)


def _ctx_proj(ctx, mods3, norm1, w_packed, gk, gmat):
    d = D_MODEL
    const = lambda shape: pl.BlockSpec(shape, lambda b: (0,) * len(shape))
    return pl.pallas_call(
        _ctx_body,
        grid=(BATCH // CTX_STEP,),
        in_specs=[pl.BlockSpec((CTX_STEP, CTX_LEN, d), lambda b: (b, 0, 0)),
                  pl.BlockSpec((1, 1, d), lambda b: (BATCH, 0, 1)),
                  pl.BlockSpec((1, 1, d), lambda b: (BATCH, 0, 0)),
                  const((1, d)), pl.BlockSpec((d, 2 * KV_W), lambda b: (0, W_OFF_KV // (2 * KV_W))),
                  const((1, KV_W)), const((2 * LANES, 2 * LANES))],
        out_specs=[pl.BlockSpec((CTX_STEP, CTX_LEN, KV_W), lambda b: (b, 0, 0)),
                   pl.BlockSpec((CTX_STEP, CTX_LEN, KV_W), lambda b: (b, 0, 0))],
        out_shape=[jax.ShapeDtypeStruct((BATCH, CTX_LEN, KV_W), BF16),
                   jax.ShapeDtypeStruct((BATCH, CTX_LEN, KV_W), BF16)],
        compiler_params=_params(("parallel",)),
        name="ctx_proj",
    )(ctx, mods3, mods3, norm1, w_packed, gk, gmat)


def _hyena_body(zv_ref, z1_ref, z2_ref, cwv_ref, cw1_ref, cw2_ref, cbv_ref, cb1_ref, cb2_ref, hb_ref, kf_ref,
                kn_ref, t1_ref, t2_ref, t1t_ref, t2t_ref, o_ref, lo_ref, hi_ref, glo_ref, ghi_ref, p_ref, m_ref,
                za_ref, zb_ref):
    row = lax.broadcasted_iota(I32, (HALF, CT), 0)
    rr = lax.broadcasted_iota(I32, (RB, RB), 0)
    cc = lax.broadcasted_iota(I32, (RB, RB), 1)
    flip = jnp.where(rr + cc == RB - 1, 1.0, 0.0).astype(BF16)
    nrb = HALF // RB

    def folded_short_conv(z_ref, w_ref, b_ref, lo_out, hi_out):
        zlo = z_ref[0, 0:HALF, :].astype(F32)
        for j in range(nrb):
            hi_out[j * RB:(j + 1) * RB, :] = _dot(flip, z_ref[0, SEQ - RB * (j + 1):SEQ - RB * j, :])
        zhi = hi_out[...]
        w0, w1, w2 = w_ref[0:1, :], w_ref[1:2, :], w_ref[2:3, :]
        first, last = row == 0, row == HALF - 1
        lo_prev = jnp.where(first, 0.0, pltpu.roll(zlo, 1, axis=0))
        lo_next = jnp.where(last, zhi[HALF - 1:HALF, :], pltpu.roll(zlo, HALF - 1, axis=0))
        hi_prev = jnp.where(first, 0.0, pltpu.roll(zhi, 1, axis=0))
        hi_next = jnp.where(last, zlo[HALF - 1:HALF, :], pltpu.roll(zhi, HALF - 1, axis=0))
        lo_out[...] = lo_prev * w0 + zlo * w1 + lo_next * w2 + b_ref[...]
        hi_out[...] = hi_next * w0 + zhi * w1 + hi_prev * w2 + b_ref[...]

    folded_short_conv(zv_ref, cwv_ref, cbv_ref, lo_ref, hi_ref)
    sign8 = _sign_rows(HALF)
    odd = (lax.broadcasted_iota(I32, (FC, CT), 0) & 1) == 1
    for o, (zr, cw, cb) in enumerate(((z1_ref, cw1_ref, cb1_ref), (z2_ref, cw2_ref, cb2_ref))):
        folded_short_conv(zr, cw, cb, glo_ref, ghi_ref)
        p_ref[...] = (lo_ref[...] + hi_ref[...]).astype(BF16)
        m_ref[...] = (lo_ref[...] - hi_ref[...]).astype(BF16)
        pv = p_ref[...]
        mv = m_ref[...]
        for c in range(HALF // FC):
            ev = slice(c * FC, (c + 1) * FC)
            od = slice(HALF + c * FC, HALF + (c + 1) * FC)
            xce = _dot(t1_ref[ev, :], pv)
            xso = _dot(t1_ref[od, :], pv)
            xco = _dot(t2_ref[ev, :], mv)
            xse = _dot(t2_ref[od, :], mv)
            kce, kse, kco, kso = kf_ref[o, 0, ev, :], kf_ref[o, 1, ev, :], kf_ref[o, 2, ev, :], kf_ref[o, 3, ev, :]
            za_ref[ev, :] = (xce * kce - xse * kse).astype(BF16)
            za_ref[od, :] = (xco * kso + xso * kco).astype(BF16)
            zb_ref[ev, :] = (xco * kco - xso * kso).astype(BF16)
            zb_ref[od, :] = (xce * kse + xse * kce).astype(BF16)
        zn = _dot(sign8, mv)[0:1] * kn_ref[o]
        bias = hb_ref[o:o + 1, :]
        za = za_ref[...]
        zb = zb_ref[...]
        for c in range(HALF // FC):
            rs = slice(c * FC, (c + 1) * FC)
            half_p = _dot(t1t_ref[rs, :], za)
            half_m = _dot(t2t_ref[rs, :], zb) + jnp.where(odd, -zn, zn)
            lo_ref[rs, :] = glo_ref[rs, :] * (half_p + half_m + bias * lo_ref[rs, :])
            hi_ref[rs, :] = ghi_ref[rs, :] * (half_p - half_m + bias * hi_ref[rs, :])
    o_ref[0, 0:HALF, :] = lo_ref[...].astype(BF16)
    for j in range(nrb):
        o_ref[0, SEQ - RB * (j + 1):SEQ - RB * j, :] = _dot(
            flip, hi_ref[j * RB:(j + 1) * RB, :].astype(BF16)).astype(BF16)


def _hyena(zh, conv_w, conv_b, hbias, kf, kn, t1, t2, t1t, t2t):
    nct = HYENA_W // CT
    zspec = lambda k: pl.BlockSpec((1, SEQ, CT), lambda c, b: (b, 0, k * nct + c))
    wspec = lambda k: pl.BlockSpec((3, CT), lambda c, b: (0, k * nct + c))
    bspec = lambda k: pl.BlockSpec((1, CT), lambda c, b: (0, k * nct + c))
    table = lambda shape: pl.BlockSpec(shape, lambda c, b: (0, 0), pipeline_mode=pl.Buffered(1))
    half_f32 = pltpu.VMEM((HALF, CT), F32)
    return pl.pallas_call(
        _hyena_body,
        grid=(nct, BATCH),
        in_specs=[zspec(0), zspec(1), zspec(2), wspec(0), wspec(1), wspec(2), bspec(0), bspec(1), bspec(2),
                  pl.BlockSpec((HYENA_ORDER, CT), lambda c, b: (0, c)),
                  pl.BlockSpec((HYENA_ORDER, 4, HALF, CT), lambda c, b: (0, 0, 0, c)),
                  pl.BlockSpec((HYENA_ORDER, 1, CT), lambda c, b: (0, 0, c)),
                  table((SEQ, HALF)), table((SEQ, HALF)), table((HALF, SEQ)), table((HALF, SEQ))],
        out_specs=pl.BlockSpec((1, SEQ, CT), lambda c, b: (b, 0, c)),
        out_shape=jax.ShapeDtypeStruct((BATCH, SEQ, HYENA_W), BF16),
        scratch_shapes=[half_f32, half_f32, half_f32, half_f32,
                        pltpu.VMEM((HALF, CT), BF16), pltpu.VMEM((HALF, CT), BF16),
                        pltpu.VMEM((SEQ, CT), BF16), pltpu.VMEM((SEQ, CT), BF16)],
        compiler_params=_params(("parallel", "parallel")),
        name="hyena_conv",
    )(zh, zh, zh, conv_w, conv_w, conv_w, conv_b, conv_b, conv_b, hbias, kf, kn, t1, t2, t1t, t2t)


def _attn_body(sink_ref, q_ref, k_ref, v_ref, kc_ref, vc_ref, bias_ref, o_ref, kp_ref, vlo_ref, vhi_ref):
    nb = SEQ // BLOCK
    lane = lax.broadcasted_iota(I32, (BLOCK, LANES), 1)
    low = lane < HEAD_DIM
    mask_lo = jnp.where(low, 1.0, 0.0).astype(BF16)
    mask_hi = jnp.where(low, 0.0, 1.0).astype(BF16)

    def with_ones(v):
        lane_v = lax.broadcasted_iota(I32, v.shape, 1) < HEAD_DIM
        one = jnp.ones_like(v)
        return jnp.where(lane_v, v, one), jnp.where(lane_v, one, v)

    zpad = jnp.zeros((BLOCK, KV_W), BF16)
    kp_ref[0:BLOCK] = zpad
    kp_ref[BLOCK:BLOCK + SEQ] = k_ref[0]
    kp_ref[BLOCK + SEQ:] = zpad
    v_lo, v_hi = with_ones(v_ref[0])
    for ref, val in ((vlo_ref, v_lo), (vhi_ref, v_hi)):
        ref[0:BLOCK] = zpad
        ref[BLOCK:BLOCK + SEQ] = val
        ref[BLOCK + SEQ:] = zpad
    kc = kc_ref[0]
    vc_pair = with_ones(vc_ref[0])

    def block(n, carry):
        r = pl.multiple_of(n * BLOCK, BLOCK)
        kw = kp_ref[pl.ds(r, 3 * BLOCK), :]
        vw_pair = (vlo_ref[pl.ds(r, 3 * BLOCK), :], vhi_ref[pl.ds(r, 3 * BLOCK), :])
        bias = bias_ref[jnp.where(n == 0, 0, jnp.where(n == nb - 1, 2, 1))]
        scores = []
        for j in range(GROUP):
            qs = q_ref[0, pl.ds(r, BLOCK), j * LANES:(j + 1) * LANES]
            for msk in (mask_lo, mask_hi):
                qm = qs * msk
                scores.append((_dot_nt(qm, kw) + bias, _dot_nt(qm, kc)))
        probs = []
        for idx, (sw, sc) in enumerate(scores):
            snk = sink_ref[idx // 2 + GROUP * (idx % 2)]
            m = jnp.maximum(jnp.maximum(jnp.max(sw, axis=-1, keepdims=True),
                                        jnp.max(sc, axis=-1, keepdims=True)), snk)
            probs.append((jnp.exp2(sw - m).astype(BF16), jnp.exp2(sc - m).astype(BF16), jnp.exp2(snk - m)))
        outs = []
        for idx, (pw, pc, psink) in enumerate(probs):
            acc = _dot(pw, vw_pair[idx % 2]) + _dot(pc, vc_pair[idx % 2])
            den = pltpu.roll(acc, HEAD_DIM, axis=1) + psink
            outs.append(acc / den)
        for j in range(GROUP):
            o_ref[0, pl.ds(r, BLOCK), j * LANES:(j + 1) * LANES] = jnp.where(
                low, outs[2 * j], outs[2 * j + 1]).astype(BF16)
        return carry

    lax.fori_loop(0, nb, block, 0)


def _attention(sink, q, k, v, kc, vc, bias):
    per_b = lambda n, w: pl.BlockSpec((1, n, w), lambda b: (b, 0, 0))
    return pl.pallas_call(
        _attn_body,
        grid=(BATCH,),
        in_specs=[pl.BlockSpec(memory_space=pltpu.SMEM),
                  per_b(SEQ, ATTN_W), per_b(SEQ, KV_W), per_b(SEQ, KV_W), per_b(CTX_LEN, KV_W), per_b(CTX_LEN, KV_W),
                  pl.BlockSpec((3, BLOCK, 3 * BLOCK), lambda b: (0, 0, 0))],
        out_specs=per_b(SEQ, ATTN_W),
        out_shape=jax.ShapeDtypeStruct((BATCH, SEQ, ATTN_W), BF16),
        scratch_shapes=[pltpu.VMEM((SEQ + 2 * BLOCK, KV_W), BF16)] * 3,
        compiler_params=_params(("parallel",)),
        name="window_attn",
    )(sink, q, k, v, kc, vc, bias)


def _merge_body(ya_ref, yb_ref, gt_ref, x_ref, g1_ref, sc2_ref, sh2_ref, n2_ref, wa_ref, wb_ref, wo_ref, rt_ref,
                xn_ref, h2_ref, aff_ref):
    d = D_MODEL
    groups = [slice(i * SUB_MERGE, (i + 1) * SUB_MERGE) for i in range(TM_MERGE // SUB_MERGE)]
    branch = [(_dot(ya_ref[0, s, :], wa_ref[...]), _dot(yb_ref[0, s, :], wb_ref[...])) for s in groups]
    mixed = [(gt_ref[0, s, :d].astype(F32) * ua + gt_ref[0, s, d:].astype(F32) * ub).astype(BF16)
             for s, (ua, ub) in zip(groups, branch)]
    proj = [_dot(u, wo_ref[...]) for u in mixed]
    h2_rows = []
    for s, m in zip(groups, proj):
        xn = x_ref[0, s, :] + g1_ref[0] * m
        xn_ref[0, s, :] = xn
        h2 = _rms_mod(xn, n2_ref[...], sc2_ref[0], sh2_ref[0]).astype(BF16)
        h2_ref[0, s, :] = h2
        h2_rows.append(h2)
    logits = [_dot_nt(rt_ref[...], h2) for h2 in h2_rows]
    for s, lt in zip(groups, logits):
        e = jnp.exp(lt - jnp.max(lt, axis=0, keepdims=True))
        aff_ref[0, :, s] = e / jnp.sum(e, axis=0, keepdims=True)


def _merge(ya, yb, gates, x, mods3, norm2, wa, wb, wo, router_t):
    d = D_MODEL
    nt = SEQ // TM_MERGE
    const = lambda shape: pl.BlockSpec(shape, lambda b, i: (0,) * len(shape))
    tok = lambda w: pl.BlockSpec((1, TM_MERGE, w), lambda b, i: (b, i, 0))
    mod = lambda k: pl.BlockSpec((1, 1, d), lambda b, i: (b, 0, k))
    return pl.pallas_call(
        _merge_body,
        grid=(BATCH, nt),
        in_specs=[tok(HYENA_W), tok(ATTN_W), tok(2 * d), tok(d), mod(2), mod(4), mod(3), const((1, d)),
                  const((HYENA_W, d)), const((ATTN_W, d)), const((d, d)), const((N_EXPERTS, d))],
        out_specs=[tok(d), tok(d), pl.BlockSpec((1, N_EXPERTS, TM_MERGE), lambda b, i: (b, 0, i))],
        out_shape=[jax.ShapeDtypeStruct((BATCH, SEQ, d), F32),
                   jax.ShapeDtypeStruct((BATCH, SEQ, d), BF16),
                   jax.ShapeDtypeStruct((BATCH, N_EXPERTS, SEQ), F32)],
        compiler_params=_params(("parallel", "parallel")),
        name="merge_outproj",
    )(ya, yb, gates, x, mods3, mods3, mods3, norm2, wa, wb, wo, router_t)


def _thresh_body(aff_ref, lo_ref, hi_ref):
    aff = aff_ref[...]
    rows = aff.shape[0]
    bits = pltpu.bitcast(aff, I32)

    def bisect_bits(_, carry):
        lo, hi = carry
        mid = lo + ((hi - lo) >> 1)
        ge = jnp.sum((bits >= mid).astype(I32), axis=1, keepdims=True) >= CAP
        return jnp.where(ge, mid, lo), jnp.where(ge, hi, mid)

    lo0 = jnp.zeros((rows, 1), I32)
    hi0 = jnp.full((rows, 1), 0x3F800001, I32)
    thr_bits, _ = lax.fori_loop(0, 31, bisect_bits, (lo0, hi0))
    thr = pltpu.bitcast(thr_bits, F32)

    def bisect_val(_, carry):
        lo, hi = carry
        mid = 0.5 * (lo + hi)
        ge = jnp.sum(jnp.where(aff >= mid, 1.0, 0.0), axis=1, keepdims=True) >= CAP
        return jnp.where(ge, mid, lo), jnp.where(ge, hi, mid)

    lo, hi = lax.fori_loop(0, 30, bisect_val, (0.5 * thr, jnp.maximum(2.0 * thr, 1e-30)))
    lo_ref[...] = jnp.broadcast_to(lo, lo_ref.shape)
    hi_ref[...] = jnp.broadcast_to(hi, hi_ref.shape)


def _thresholds(aff_rows):
    rows = BATCH * N_EXPERTS
    return pl.pallas_call(
        _thresh_body,
        grid=(1,),
        in_specs=[pl.BlockSpec((rows, SEQ), lambda i: (0, 0))],
        out_specs=[pl.BlockSpec((rows, LANES), lambda i: (0, 0)), pl.BlockSpec((rows, LANES), lambda i: (0, 0))],
        out_shape=[jax.ShapeDtypeStruct((rows, LANES), F32), jax.ShapeDtypeStruct((rows, LANES), F32)],
        compiler_params=_params(("arbitrary",)),
        name="route_threshold",
    )(aff_rows)


def _prefix_counts(mask):
    r = lax.broadcasted_iota(I32, (LANES, LANES), 0)
    c = lax.broadcasted_iota(I32, (LANES, LANES), 1)
    upper = jnp.where(r <= c, 1.0, 0.0).astype(BF16)
    offset = jnp.zeros((mask.shape[0], 1), F32)
    blocks = []
    for j in range(mask.shape[1] // LANES):
        blk = mask[:, j * LANES:(j + 1) * LANES]
        inc = _dot(blk.astype(BF16), upper)
        blocks.append(inc - blk + offset)
        offset = offset + inc[:, LANES - 1:LANES]
    return jnp.concatenate(blocks, axis=1)


def _route_body(aff_ref, lo_ref, hi_ref, h_ref, slot_ref, xin_ref, pos_ref, pbuf_ref):
    aff = aff_ref[0]
    above = jnp.where(aff >= hi_ref[:, 0:1], 1.0, 0.0)
    band = jnp.where(aff >= lo_ref[:, 0:1], 1.0, 0.0) - above
    need = CAP - jnp.sum(above, axis=1, keepdims=True)
    tie_rank = _prefix_counts(band)
    self = above + band * jnp.where(tie_rank < need, 1.0, 0.0)
    pos = _prefix_counts(self)
    posi = jnp.where(self > 0.5, pos.astype(I32), -1)
    pos_ref[0] = posi

    h = h_ref[0]
    slot = slot_ref[...]
    posb = jnp.where(self > 0.5, pos, -1.0).astype(BF16)
    one = jnp.ones((CAP, SEQ), BF16)
    zero = jnp.zeros((CAP, SEQ), BF16)
    for grp in range(N_EXPERTS // EXPERT_GROUP):
        for i in range(EXPERT_GROUP):
            e = grp * EXPERT_GROUP + i
            pbuf_ref[i * CAP:(i + 1) * CAP, :] = jnp.where(posb[e:e + 1, :] == slot, one, zero)
        rows = _dot(pbuf_ref[...], h)
        for i in range(EXPERT_GROUP):
            xin_ref[grp * EXPERT_GROUP + i] = rows[i * CAP:(i + 1) * CAP].astype(BF16)


def _route(aff_t, lo, hi, h2, slot_rows):
    d = D_MODEL
    return pl.pallas_call(
        _route_body,
        grid=(BATCH,),
        in_specs=[pl.BlockSpec((1, N_EXPERTS, SEQ), lambda b: (b, 0, 0)),
                  pl.BlockSpec((N_EXPERTS, LANES), lambda b: (b, 0)),
                  pl.BlockSpec((N_EXPERTS, LANES), lambda b: (b, 0)),
                  pl.BlockSpec((1, SEQ, d), lambda b: (b, 0, 0)),
                  pl.BlockSpec((CAP, SEQ), lambda b: (0, 0))],
        out_specs=[pl.BlockSpec((N_EXPERTS, CAP, d), lambda b: (0, b, 0)),
                   pl.BlockSpec((1, N_EXPERTS, SEQ), lambda b: (b, 0, 0))],
        out_shape=[jax.ShapeDtypeStruct((N_EXPERTS, BATCH * CAP, d), BF16),
                   jax.ShapeDtypeStruct((BATCH, N_EXPERTS, SEQ), I32)],
        scratch_shapes=[pltpu.VMEM((EXPERT_GROUP * CAP, SEQ), BF16)],
        compiler_params=_params(("parallel",)),
        name="route_gather",
    )(aff_t, lo, hi, h2, slot_rows)


def _expert_body(x_ref, wg_ref, wu_ref, wd_ref, o_ref, acc_ref, wgb_ref, wub_ref, wdb_ref):
    f = pl.program_id(1)

    @pl.when((pl.program_id(0) == 0) & (f == 0))
    def _():
        acc_ref[...] = jnp.zeros_like(acc_ref)

    carry = f > 0
    for m in range(BATCH * CAP // MC):
        rows = slice(m * MC, (m + 1) * MC)
        xm = x_ref[0, rows, :]
        if m == 0:
            wgb_ref[...] = wg_ref[0].astype(BF16)
        a = _dot(xm, wgb_ref[...])
        if m == 0:
            wub_ref[...] = wu_ref[0].astype(BF16)
        b = _dot(xm, wub_ref[...])
        hh = (a * jax.nn.sigmoid(a) * b).astype(BF16)
        if m == 0:
            wdb_ref[...] = wd_ref[0].astype(BF16)
        total = jnp.where(carry, acc_ref[rows, :], 0.0) + _dot(hh, wdb_ref[...])
        acc_ref[rows, :] = total
        o_ref[0, rows, :] = total.astype(BF16)


def _experts(xin, w_gate, w_up, w_down):
    d = D_MODEL
    rows = BATCH * CAP
    return pl.pallas_call(
        _expert_body,
        grid=(N_EXPERTS, D_FF // TF),
        in_specs=[pl.BlockSpec((1, rows, d), lambda e, f: (e, 0, 0)),
                  pl.BlockSpec((1, d, TF), lambda e, f: (e, 0, f)),
                  pl.BlockSpec((1, d, TF), lambda e, f: (e, 0, f)),
                  pl.BlockSpec((1, TF, d), lambda e, f: (e, f, 0))],
        out_specs=pl.BlockSpec((1, rows, d), lambda e, f: (e, 0, 0)),
        out_shape=jax.ShapeDtypeStruct((N_EXPERTS, rows, d), BF16),
        scratch_shapes=[pltpu.VMEM((rows, d), F32), pltpu.VMEM((d, TF), BF16), pltpu.VMEM((d, TF), BF16),
                        pltpu.VMEM((TF, d), BF16)],
        compiler_params=_params(("parallel", "arbitrary")),
        name="swiglu_experts",
    )(xin, w_gate, w_up, w_down)


def _scatter_body(pos_ref, aff_ref, y_ref, x_ref, g2_ref, o_ref):
    pos = pos_ref[0].T
    aff = aff_ref[0].T
    slot = lax.broadcasted_iota(I32, (SUB_S, CAP), 1)
    groups = [slice(i * SUB_S, (i + 1) * SUB_S) for i in range(TS // SUB_S)]
    onehots = [jnp.concatenate([jnp.where(pos[s, e:e + 1] == slot, aff[s, e:e + 1], 0.0).astype(BF16)
                                for e in range(N_EXPERTS)], axis=1) for s in groups]
    y = y_ref[...].reshape(N_EXPERTS * CAP, D_MODEL)
    mixed = [_dot(oh, y) for oh in onehots]
    for s, m in zip(groups, mixed):
        o_ref[0, s, :] = x_ref[0, s, :] + g2_ref[0] * m


def _scatter(pos, aff_t, y, xn, mods3):
    d = D_MODEL
    return pl.pallas_call(
        _scatter_body,
        grid=(BATCH, SEQ // TS),
        in_specs=[pl.BlockSpec((1, N_EXPERTS, TS), lambda b, i: (b, 0, i)),
                  pl.BlockSpec((1, N_EXPERTS, TS), lambda b, i: (b, 0, i)),
                  pl.BlockSpec((N_EXPERTS, CAP, d), lambda b, i: (0, b, 0)),
                  pl.BlockSpec((1, TS, d), lambda b, i: (b, i, 0)),
                  pl.BlockSpec((1, 1, d), lambda b, i: (b, 0, 5))],
        out_specs=pl.BlockSpec((1, TS, d), lambda b, i: (b, i, 0)),
        out_shape=jax.ShapeDtypeStruct((BATCH, SEQ, d), F32),
        compiler_params=_params(("parallel", "parallel")),
        name="scatter_residual",
    )(pos, aff_t, y, xn, mods3)


def _rope_tables():
    rows = SEQ // GRID_W
    row = np.repeat(np.arange(rows, dtype=np.float32), GRID_W)
    col = np.tile(np.arange(GRID_W, dtype=np.float32), rows)
    inv = (ROPE_BASE ** (-np.arange(0, AXIS_ROT, 2, dtype=np.float32) / AXIS_ROT)).astype(np.float32)
    ang = np.concatenate([row[:, None] * inv, col[:, None] * inv], axis=-1).astype(np.float64)
    cos = np.repeat(np.cos(ang), 2, axis=-1)
    sin = np.stack([-np.sin(ang), np.sin(ang)], axis=-1).reshape(SEQ, HEAD_DIM)
    reps = LANES // HEAD_DIM
    return jnp.asarray(np.tile(cos, (1, reps)), F32), jnp.asarray(np.tile(sin, (1, reps)), F32)


def _dft_tables():
    idx = np.arange(HALF, dtype=np.int64)
    t2p1 = 2 * idx + 1

    def cos_sin(f):
        ang = ((f[:, None] * t2p1[None, :]) % (2 * N_FFT)) * (math.pi / N_FFT)
        return np.cos(ang), np.sin(ang)

    ce, se = cos_sin(2 * idx)
    co, so = cos_sin(2 * idx + 1)
    t1 = jnp.asarray(np.concatenate([ce, so], axis=0), F32).astype(BF16)
    t2 = jnp.asarray(np.concatenate([co, se], axis=0), F32).astype(BF16)
    return t1, t2, t1.T, t2.T


def _phase_tables():
    idx = np.arange(HALF, dtype=np.float64)
    w = np.full((HALF,), 2.0 / N_FFT)
    we = w.copy()
    we[0] = 1.0 / N_FFT
    pe = (math.pi / N_FFT) * (2.0 * idx)
    po = (math.pi / N_FFT) * (2.0 * idx + 1.0)
    rot = np.stack([we * np.cos(pe), we * np.sin(pe), w * np.cos(po), w * np.sin(po)])
    return jnp.asarray(np.broadcast_to(rot[:, :, None], (4, HALF, CT)), F32)


def _fold_rows(a):
    return np.concatenate([a[:HALF], a[HALF:][::-1]], axis=0)


def _filter_features():
    t = np.linspace(0.0, 1.0, SEQ, dtype=np.float32).astype(np.float64)[:, None]
    w = 2.0 * math.pi * np.arange(SEQ, dtype=np.float64)[:, None] / SEQ
    fr = np.linspace(1e-4, FILTER_BANDS - 1, FILTER_BANDS, dtype=np.float32).astype(np.float64)[None, :]
    feat = np.concatenate([t, np.cos(fr * w), -np.sin(fr * w)], axis=-1)
    feat = np.pad(feat, ((0, 0), (0, FILTER_HIDDEN - FILTER_EMB)))
    min_decay = math.log(DECAY_TARGET) / SLOW_DECAY_PCT
    max_decay = math.log(DECAY_TARGET) / FAST_DECAY_PCT
    deltas = np.linspace(min_decay, max_decay, HYENA_W, dtype=np.float32).astype(np.float64)
    decay = np.exp(-t * np.abs(deltas))
    return jnp.asarray(_fold_rows(feat).T, F32), jnp.asarray(_fold_rows(decay), F32)


def _attn_bias():
    qi = np.arange(BLOCK)[:, None]
    kj = np.arange(3 * BLOCK)[None, :]
    band = np.abs(kj - BLOCK - qi) <= WINDOW
    first = band & (kj >= BLOCK)
    last = band & (kj < 2 * BLOCK)
    return jnp.asarray(np.where(np.stack([first, band, last]), 0.0, NEG), F32)


def _pair_heads(w, axis):
    heads = [lax.slice_in_dim(w, h * HEAD_DIM, (h + 1) * HEAD_DIM, axis=axis) for h in range(N_HEADS)]
    return jnp.concatenate([heads[j + GROUP * half] for j in range(GROUP) for half in range(N_KV_HEADS)], axis=axis)


def kernel(x, c, ctx, c_ctx, ada_w, ada_b, norm1, norm2, w_in, conv_w, conv_b, filt_w1, filt_b1, filt_w2, filt_b2,
           filt_w3, filt_b3, filt_freq, filt_out, hyena_bias, q_norm, k_norm, attn_sink, w_branch_a, w_branch_b,
           w_out, router, w_gate, w_up, w_down):
    d = D_MODEL
    assert ada_w.shape[0] == 1, "only the single-layer configuration is implemented"
    l = 0
    cos_t, sin_t = _rope_tables()
    t1, t2, t1t, t2t = _dft_tables()
    rot = _phase_tables()
    feat, decay = _filter_features()
    bias = _attn_bias()
    gmat = jnp.asarray(np.kron(np.eye(2 * LANES // HEAD_DIM), np.full((HEAD_DIM, HEAD_DIM), 1.0 / HEAD_DIM)), BF16)
    c16 = jnp.concatenate([c, c_ctx[None, :], jnp.zeros((MOD_ROWS - BATCH - 1, d), F32)], axis=0)

    mods3 = _ada(c16, ada_w[l], ada_b[l][None, :]).reshape(MOD_ROWS, 1, 6 * d)
    n1 = norm1[l][None, :]
    w_packed = _pack_in_weights(w_in[l])
    gk = jnp.tile(k_norm[l], N_KV_HEADS)[None, :]
    kf, kn = _filters(feat, jnp.pad(filt_w1[l].T, ((0, 0), (0, FILTER_HIDDEN - FILTER_EMB))), filt_b1[l][:, None],
                      filt_w2[l].T, filt_b2[l][:, None], filt_w3[l].T, filt_b3[l][:, None], filt_freq[l][:, None],
                      filt_out[l], decay, rot, t1, t2)
    kc, vc = _ctx_proj(ctx, mods3, n1, w_packed, gk, gmat)
    zh, q, k, v, gates = _inproj(x, mods3, n1, w_packed, jnp.tile(q_norm[l], N_HEADS)[None, :], gk, gmat, cos_t, sin_t)
    ya = _hyena(zh, conv_w[l], conv_b[l][None, :], hyena_bias[l], kf, kn, t1, t2, t1t, t2t)
    yb = _attention(attn_sink[l] * LOG2E, q, k, v, kc, vc, bias)
    xn, h2, aff_t = _merge(ya, yb, gates, x, mods3, norm2[l][None, :], w_branch_a[l].astype(BF16),
                           _pair_heads(w_branch_b[l], 0).astype(BF16), w_out[l].astype(BF16), router[l].T.astype(BF16))
    lo, hi = _thresholds(aff_t.reshape(BATCH * N_EXPERTS, SEQ))
    slot_rows = jnp.asarray(np.broadcast_to(np.arange(CAP)[:, None], (CAP, SEQ)), BF16)
    xin, pos = _route(aff_t, lo, hi, h2, slot_rows)
    y = _experts(xin, w_gate[l], w_up[l], w_down[l])
    return _scatter(pos, aff_t, y, xn, mods3)
```

```python
import math

import numpy as np
import jax
import jax.numpy as jnp
from jax import lax
from jax.experimental import pallas as pl
from jax.experimental.pallas import tpu as pltpu

F32 = jnp.float32
BF16 = jnp.bfloat16
I32 = jnp.int32
HIGHEST = lax.Precision.HIGHEST

D_MODEL = 1024
BATCH = 8
SEQ = 2048
GRID_W = 64
CTX_LEN = 256
N_HEADS = 8
N_KV_HEADS = 2
HEAD_DIM = 64
GROUP = N_HEADS // N_KV_HEADS
ATTN_W = N_HEADS * HEAD_DIM
KV_W = N_KV_HEADS * HEAD_DIM
WINDOW = 128
BLOCK = 128
HYENA_W = D_MODEL // 2
HYENA_ORDER = 2
FILTER_BANDS = 16
FILTER_EMB = 1 + 2 * FILTER_BANDS
FILTER_HIDDEN = 64
DECAY_TARGET = 1e-2
FAST_DECAY_PCT = 0.3
SLOW_DECAY_PCT = 1.5
ROPE_BASE = 10000.0
AXIS_ROT = HEAD_DIM // 2
N_EXPERTS = 16
EC_CAPACITY = 2
D_FF = 2048
EPS = 1e-6
NEG = -1e30
LOG2E = math.log2(math.e)

OFF_Q = 3 * HYENA_W
OFF_K = OFF_Q + ATTN_W
OFF_V = OFF_K + KV_W
OFF_G = OFF_V + KV_W
IN_W = OFF_G + 2 * D_MODEL

CAP = EC_CAPACITY * SEQ // N_EXPERTS
N_FFT = 2 * SEQ
HALF = SEQ // 2
MOD_ROWS = 16
LANES = 128

TM_IN = 1024
SUB_IN = 256
TM_MERGE = 1024
SUB_MERGE = 512
CT = 256
FC = 512
RB = 256
TF = 512
MC = 512
TS = 1024
SUB_S = 512
CTX_STEP = 4
EXPERT_GROUP = 4
ATTN_UNROLL = 8
VMEM_LIMIT = 56 * 1024 * 1024


def _dot(a, b, precision=None):
    return jnp.dot(a, b, preferred_element_type=F32, precision=precision)


def _dot_nt(a, b, precision=None):
    return lax.dot_general(a, b, (((1,), (1,)), ((), ())), preferred_element_type=F32, precision=precision)


def _params(sem, vmem=VMEM_LIMIT):
    return pltpu.CompilerParams(dimension_semantics=sem, vmem_limit_bytes=vmem)


def _rms_mod(x, g, sc, sh):
    ms = jnp.mean(x * x, axis=-1, keepdims=True)
    return (x * lax.rsqrt(ms + EPS) * g) * (1.0 + sc) + sh


def _head_norm_rope(z, g, gmat, cos, sin, scale):
    ms = _dot((z * z).astype(BF16), gmat)
    y = z * lax.rsqrt(ms + EPS) * g
    if cos is not None:
        slabs = []
        for s in range(z.shape[1] // LANES):
            ys = y[:, s * LANES:(s + 1) * LANES]
            lane = lax.broadcasted_iota(I32, ys.shape, 1)
            nxt = pltpu.roll(ys, LANES - 1, axis=1)
            prv = pltpu.roll(ys, 1, axis=1)
            slabs.append(ys * cos + jnp.where((lane & 1) == 0, nxt, prv) * sin)
        y = slabs[0] if len(slabs) == 1 else jnp.concatenate(slabs, axis=1)
    return y * scale


def _split_bf16(x):
    hi = x.astype(BF16)
    return hi, (x - hi.astype(F32)).astype(BF16)


def _ada_body(c_ref, w_ref, b_ref, o_ref):
    c = c_ref[...]
    s_hi, s_lo = _split_bf16(c * jax.nn.sigmoid(c))
    w_hi, w_lo = _split_bf16(w_ref[...])
    o_ref[...] = _dot(s_hi, w_hi) + _dot(s_lo, w_hi) + _dot(s_hi, w_lo) + b_ref[...]


def _ada(c16, w, b):
    d = D_MODEL
    return pl.pallas_call(
        _ada_body,
        grid=(6,),
        in_specs=[pl.BlockSpec((MOD_ROWS, d), lambda j: (0, 0)),
                  pl.BlockSpec((d, d), lambda j: (0, j)),
                  pl.BlockSpec((1, d), lambda j: (0, j))],
        out_specs=pl.BlockSpec((MOD_ROWS, d), lambda j: (0, j)),
        out_shape=jax.ShapeDtypeStruct((MOD_ROWS, 6 * d), F32),
        compiler_params=_params(("parallel",)),
        name="ada_mod",
    )(c16, w, b)


def _sign_rows(n):
    lane = lax.broadcasted_iota(I32, (8, n), 1)
    sub = lax.broadcasted_iota(I32, (8, n), 0)
    sg = jnp.where((lane & 1) == 0, 1.0, -1.0)
    return jnp.where(sub == 0, sg, 0.0).astype(BF16)


def _filt_body(feat_ref, w1_ref, b1_ref, w2_ref, b2_ref, w3_ref, b3_ref, fq_ref, fof_ref, fob_ref, dec_ref,
               rot_ref, t1_ref, t2_ref, kf_ref, kn_ref, hh_ref, hl_ref):
    @pl.when((pl.program_id(0) == 0) & (pl.program_id(1) == 0))
    def _():
        fq = fq_ref[...]
        h = jnp.sin(fq * (_dot(w1_ref[...], feat_ref[...], HIGHEST) + b1_ref[...]))
        h = jnp.sin(fq * (_dot(w2_ref[...], h, HIGHEST) + b2_ref[...]))
        h = jnp.sin(fq * (_dot(w3_ref[...], h, HIGHEST) + b3_ref[...]))
        hh_ref[...], hl_ref[...] = _split_bf16(h.T)

    def taps(fo_ref):
        f_hi, f_lo = _split_bf16(fo_ref[...])
        return _dot(hh_ref[...], f_hi) + _dot(hl_ref[...], f_hi) + _dot(hh_ref[...], f_lo)

    dec = dec_ref[...]
    hf = taps(fof_ref) * dec
    hb = taps(fob_ref) * dec
    row = lax.broadcasted_iota(I32, hf.shape, 0)
    hb = jnp.where(row == 0, 0.0, hb)
    a = hf + hb
    b = hf - hb
    pa = (a[:HALF] + a[HALF:]).astype(BF16)
    ma = (a[:HALF] - a[HALF:]).astype(BF16)
    pb = (b[:HALF] + b[HALF:]).astype(BF16)
    mb = (b[:HALF] - b[HALF:]).astype(BF16)
    t1 = t1_ref[...]
    t2 = t2_ref[...]
    a1 = _dot(t1, pa)
    a2 = _dot(t2, ma)
    b1 = _dot(t1, pb)
    b2 = _dot(t2, mb)
    ce, se, co, so = rot_ref[0], rot_ref[1], rot_ref[2], rot_ref[3]
    kf_ref[0, 0] = a1[:HALF] * ce + a2[HALF:] * se
    kf_ref[0, 1] = b2[HALF:] * ce - b1[:HALF] * se
    kf_ref[0, 2] = a2[:HALF] * co + a1[HALF:] * so
    kf_ref[0, 3] = b1[HALF:] * co - b2[:HALF] * so
    kn_ref[0] = _dot(_sign_rows(HALF), ma)[0:1] * (1.0 / N_FFT)


def _filters(feat, w1, b1, w2, b2, w3, b3, fq, fout, decay, rot, t1, t2):
    nct = HYENA_W // CT
    full = lambda shape: pl.BlockSpec(shape, lambda o, c: (0,) * len(shape))
    return pl.pallas_call(
        _filt_body,
        grid=(HYENA_ORDER, nct),
        in_specs=[full((FILTER_HIDDEN, SEQ)), full((FILTER_HIDDEN, FILTER_HIDDEN)), full((FILTER_HIDDEN, 1)),
                  full((FILTER_HIDDEN, FILTER_HIDDEN)), full((FILTER_HIDDEN, 1)),
                  full((FILTER_HIDDEN, FILTER_HIDDEN)), full((FILTER_HIDDEN, 1)), full((FILTER_HIDDEN, 1)),
                  pl.BlockSpec((FILTER_HIDDEN, CT), lambda o, c: (0, (o * 2 + 0) * nct + c)),
                  pl.BlockSpec((FILTER_HIDDEN, CT), lambda o, c: (0, (o * 2 + 1) * nct + c)),
                  pl.BlockSpec((SEQ, CT), lambda o, c: (0, c)),
                  full((4, HALF, CT)),
                  pl.BlockSpec((SEQ, HALF), lambda o, c: (0, 0), pipeline_mode=pl.Buffered(1)),
                  pl.BlockSpec((SEQ, HALF), lambda o, c: (0, 0), pipeline_mode=pl.Buffered(1))],
        out_specs=[pl.BlockSpec((1, 4, HALF, CT), lambda o, c: (o, 0, 0, c)),
                   pl.BlockSpec((1, 1, CT), lambda o, c: (o, 0, c))],
        out_shape=[jax.ShapeDtypeStruct((HYENA_ORDER, 4, HALF, HYENA_W), F32),
                   jax.ShapeDtypeStruct((HYENA_ORDER, 1, HYENA_W), F32)],
        scratch_shapes=[pltpu.VMEM((SEQ, FILTER_HIDDEN), BF16)] * 2,
        compiler_params=_params(("arbitrary", "arbitrary")),
        name="hyena_filters",
    )(feat, w1, b1, w2, b2, w3, b3, fq, fout, fout, decay, rot, t1, t2)


def _pair_head_lanes(z):
    slabs = [z[:, s * LANES:(s + 1) * LANES] for s in range(ATTN_W // LANES)]
    swapped = [pltpu.roll(sl, HEAD_DIM, axis=1) for sl in slabs]
    low = lax.broadcasted_iota(I32, slabs[0].shape, 1) < HEAD_DIM
    out = []
    for j in range(GROUP):
        first, second = j, j + GROUP
        lo_src = slabs[first // 2] if first % 2 == 0 else swapped[first // 2]
        hi_src = slabs[second // 2] if second % 2 == 1 else swapped[second // 2]
        out.append(jnp.where(low, lo_src, hi_src))
    return jnp.concatenate(out, axis=1)


def _inproj_body(x_ref, sc_ref, sh_ref, n1_ref, wh_ref, wq_ref, wkv_ref, wg_ref, gq_ref, gk_ref, gmat_ref,
                 cos_ref, sin_ref, zh_ref, q_ref, k_ref, v_ref, gate_ref):
    groups = [slice(i * SUB_IN, (i + 1) * SUB_IN) for i in range(TM_IN // SUB_IN)]
    hx = [_rms_mod(x_ref[0, s, :], n1_ref[...], sc_ref[0], sh_ref[0]).astype(BF16) for s in groups]
    for s, h in zip(groups, hx):
        zh_ref[0, s, :] = _dot(h, wh_ref[...]).astype(BF16)
    pair = 2 * LANES
    zq = [_pair_head_lanes(_dot(h, wq_ref[...])) for h in hx]
    for s, z in zip(groups, zq):
        for c in range(ATTN_W // pair):
            sl = slice(c * pair, (c + 1) * pair)
            q_ref[0, s, sl] = _head_norm_rope(z[:, sl], gq_ref[:, sl], gmat_ref[...], cos_ref[s, :], sin_ref[s, :],
                                              LOG2E * HEAD_DIM ** -0.5).astype(BF16)
    zkv = [_dot(h, wkv_ref[...]) for h in hx]
    for s, z in zip(groups, zkv):
        k_ref[0, s, :] = _head_norm_rope(z[:, :KV_W], gk_ref[...], gmat_ref[0:KV_W, 0:KV_W], cos_ref[s, :],
                                         sin_ref[s, :], 1.0).astype(BF16)
        v_ref[0, s, :] = z[:, KV_W:].astype(BF16)
    for s, h in zip(groups, hx):
        gate_ref[0, s, :] = jax.nn.sigmoid(_dot(h, wg_ref[...])).astype(BF16)


def _pack_in_weights(w):
    wb = w.astype(BF16)
    return jnp.concatenate([wb[:, :OFF_K], wb[:, OFF_G:], wb[:, OFF_K:OFF_G]], axis=1)


W_OFF_H = 0
W_OFF_Q = OFF_Q
W_OFF_G = OFF_Q + ATTN_W
W_OFF_KV = W_OFF_G + 2 * D_MODEL


def _inproj(x, mods3, norm1, w_packed, gq, gk, gmat, cos_t, sin_t):
    d = D_MODEL
    nt = SEQ // TM_IN
    const = lambda shape: pl.BlockSpec(shape, lambda b, i: (0,) * len(shape))

    def wcol(width, off):
        assert off % width == 0
        return pl.BlockSpec((d, width), lambda b, i: (0, off // width))

    tok = lambda w: pl.BlockSpec((1, TM_IN, w), lambda b, i: (b, i, 0))
    return pl.pallas_call(
        _inproj_body,
        grid=(BATCH, nt),
        in_specs=[tok(d),
                  pl.BlockSpec((1, 1, d), lambda b, i: (b, 0, 1)),
                  pl.BlockSpec((1, 1, d), lambda b, i: (b, 0, 0)),
                  const((1, d)), wcol(OFF_Q, W_OFF_H), wcol(ATTN_W, W_OFF_Q), wcol(2 * KV_W, W_OFF_KV),
                  wcol(2 * d, W_OFF_G), const((1, ATTN_W)), const((1, KV_W)), const((2 * LANES, 2 * LANES)),
                  pl.BlockSpec((TM_IN, LANES), lambda b, i: (i, 0)),
                  pl.BlockSpec((TM_IN, LANES), lambda b, i: (i, 0))],
        out_specs=[tok(OFF_Q), tok(ATTN_W), tok(KV_W), tok(KV_W), tok(2 * d)],
        out_shape=[jax.ShapeDtypeStruct((BATCH, SEQ, OFF_Q), BF16),
                   jax.ShapeDtypeStruct((BATCH, SEQ, ATTN_W), BF16),
                   jax.ShapeDtypeStruct((BATCH, SEQ, KV_W), BF16),
                   jax.ShapeDtypeStruct((BATCH, SEQ, KV_W), BF16),
                   jax.ShapeDtypeStruct((BATCH, SEQ, 2 * d), BF16)],
        compiler_params=_params(("parallel", "parallel")),
        name="in_proj",
    )(x, mods3, mods3, norm1, w_packed, w_packed, w_packed, w_packed, gq, gk, gmat, cos_t, sin_t)


def _ctx_body(c_ref, sc_ref, sh_ref, n1_ref, wkv_ref, gk_ref, gmat_ref, kc_ref, vc_ref):
    for i in range(CTX_STEP):
        hc = _rms_mod(c_ref[i], n1_ref[...], sc_ref[0], sh_ref[0]).astype(BF16)
        z = _dot(hc, wkv_ref[...])
        kc_ref[i] = _head_norm_rope(z[:, :KV_W], gk_ref[...], gmat_ref[0:KV_W, 0:KV_W], None, None, 1.0).astype(BF16)
        vc_ref[i] = z[:, KV_W:].astype(BF16)


def _ctx_proj(ctx, mods3, norm1, w_packed, gk, gmat):
    d = D_MODEL
    const = lambda shape: pl.BlockSpec(shape, lambda b: (0,) * len(shape))
    return pl.pallas_call(
        _ctx_body,
        grid=(BATCH // CTX_STEP,),
        in_specs=[pl.BlockSpec((CTX_STEP, CTX_LEN, d), lambda b: (b, 0, 0)),
                  pl.BlockSpec((1, 1, d), lambda b: (BATCH, 0, 1)),
                  pl.BlockSpec((1, 1, d), lambda b: (BATCH, 0, 0)),
                  const((1, d)), pl.BlockSpec((d, 2 * KV_W), lambda b: (0, W_OFF_KV // (2 * KV_W))),
                  const((1, KV_W)), const((2 * LANES, 2 * LANES))],
        out_specs=[pl.BlockSpec((CTX_STEP, CTX_LEN, KV_W), lambda b: (b, 0, 0)),
                   pl.BlockSpec((CTX_STEP, CTX_LEN, KV_W), lambda b: (b, 0, 0))],
        out_shape=[jax.ShapeDtypeStruct((BATCH, CTX_LEN, KV_W), BF16),
                   jax.ShapeDtypeStruct((BATCH, CTX_LEN, KV_W), BF16)],
        compiler_params=_params(("parallel",)),
        name="ctx_proj",
    )(ctx, mods3, mods3, norm1, w_packed, gk, gmat)


def _hyena_body(zv_ref, z1_ref, z2_ref, cwv_ref, cw1_ref, cw2_ref, cbv_ref, cb1_ref, cb2_ref, hb_ref, kf_ref,
                kn_ref, t1_ref, t2_ref, t1t_ref, t2t_ref, o_ref, lo_ref, hi_ref, glo_ref, ghi_ref, p_ref, m_ref,
                za_ref, zb_ref):
    row = lax.broadcasted_iota(I32, (HALF, CT), 0)
    rr = lax.broadcasted_iota(I32, (RB, RB), 0)
    cc = lax.broadcasted_iota(I32, (RB, RB), 1)
    flip = jnp.where(rr + cc == RB - 1, 1.0, 0.0).astype(BF16)
    nrb = HALF // RB

    def folded_short_conv(z_ref, w_ref, b_ref, lo_out, hi_out):
        zlo = z_ref[0, 0:HALF, :].astype(F32)
        for j in range(nrb):
            hi_out[j * RB:(j + 1) * RB, :] = _dot(flip, z_ref[0, SEQ - RB * (j + 1):SEQ - RB * j, :])
        zhi = hi_out[...]
        w0, w1, w2 = w_ref[0:1, :], w_ref[1:2, :], w_ref[2:3, :]
        first, last = row == 0, row == HALF - 1
        lo_prev = jnp.where(first, 0.0, pltpu.roll(zlo, 1, axis=0))
        lo_next = jnp.where(last, zhi[HALF - 1:HALF, :], pltpu.roll(zlo, HALF - 1, axis=0))
        hi_prev = jnp.where(first, 0.0, pltpu.roll(zhi, 1, axis=0))
        hi_next = jnp.where(last, zlo[HALF - 1:HALF, :], pltpu.roll(zhi, HALF - 1, axis=0))
        lo_out[...] = lo_prev * w0 + zlo * w1 + lo_next * w2 + b_ref[...]
        hi_out[...] = hi_next * w0 + zhi * w1 + hi_prev * w2 + b_ref[...]

    folded_short_conv(zv_ref, cwv_ref, cbv_ref, lo_ref, hi_ref)
    sign8 = _sign_rows(HALF)
    odd = (lax.broadcasted_iota(I32, (FC, CT), 0) & 1) == 1
    for o, (zr, cw, cb) in enumerate(((z1_ref, cw1_ref, cb1_ref), (z2_ref, cw2_ref, cb2_ref))):
        folded_short_conv(zr, cw, cb, glo_ref, ghi_ref)
        p_ref[...] = (lo_ref[...] + hi_ref[...]).astype(BF16)
        m_ref[...] = (lo_ref[...] - hi_ref[...]).astype(BF16)
        pv = p_ref[...]
        mv = m_ref[...]
        for c in range(HALF // FC):
            ev = slice(c * FC, (c + 1) * FC)
            od = slice(HALF + c * FC, HALF + (c + 1) * FC)
            xce = _dot(t1_ref[ev, :], pv)
            xso = _dot(t1_ref[od, :], pv)
            xco = _dot(t2_ref[ev, :], mv)
            xse = _dot(t2_ref[od, :], mv)
            kce, kse, kco, kso = kf_ref[o, 0, ev, :], kf_ref[o, 1, ev, :], kf_ref[o, 2, ev, :], kf_ref[o, 3, ev, :]
            za_ref[ev, :] = (xce * kce - xse * kse).astype(BF16)
            za_ref[od, :] = (xco * kso + xso * kco).astype(BF16)
            zb_ref[ev, :] = (xco * kco - xso * kso).astype(BF16)
            zb_ref[od, :] = (xce * kse + xse * kce).astype(BF16)
        zn = _dot(sign8, mv)[0:1] * kn_ref[o]
        bias = hb_ref[o:o + 1, :]
        za = za_ref[...]
        zb = zb_ref[...]
        for c in range(HALF // FC):
            rs = slice(c * FC, (c + 1) * FC)
            half_p = _dot(t1t_ref[rs, :], za)
            half_m = _dot(t2t_ref[rs, :], zb) + jnp.where(odd, -zn, zn)
            lo_ref[rs, :] = glo_ref[rs, :] * (half_p + half_m + bias * lo_ref[rs, :])
            hi_ref[rs, :] = ghi_ref[rs, :] * (half_p - half_m + bias * hi_ref[rs, :])
    o_ref[0, 0:HALF, :] = lo_ref[...].astype(BF16)
    for j in range(nrb):
        o_ref[0, SEQ - RB * (j + 1):SEQ - RB * j, :] = _dot(
            flip, hi_ref[j * RB:(j + 1) * RB, :].astype(BF16)).astype(BF16)


def _hyena(zh, conv_w, conv_b, hbias, kf, kn, t1, t2, t1t, t2t):
    nct = HYENA_W // CT
    zspec = lambda k: pl.BlockSpec((1, SEQ, CT), lambda c, b: (b, 0, k * nct + c))
    wspec = lambda k: pl.BlockSpec((3, CT), lambda c, b: (0, k * nct + c))
    bspec = lambda k: pl.BlockSpec((1, CT), lambda c, b: (0, k * nct + c))
    table = lambda shape: pl.BlockSpec(shape, lambda c, b: (0, 0), pipeline_mode=pl.Buffered(1))
    half_f32 = pltpu.VMEM((HALF, CT), F32)
    return pl.pallas_call(
        _hyena_body,
        grid=(nct, BATCH),
        in_specs=[zspec(0), zspec(1), zspec(2), wspec(0), wspec(1), wspec(2), bspec(0), bspec(1), bspec(2),
                  pl.BlockSpec((HYENA_ORDER, CT), lambda c, b: (0, c)),
                  pl.BlockSpec((HYENA_ORDER, 4, HALF, CT), lambda c, b: (0, 0, 0, c)),
                  pl.BlockSpec((HYENA_ORDER, 1, CT), lambda c, b: (0, 0, c)),
                  table((SEQ, HALF)), table((SEQ, HALF)), table((HALF, SEQ)), table((HALF, SEQ))],
        out_specs=pl.BlockSpec((1, SEQ, CT), lambda c, b: (b, 0, c)),
        out_shape=jax.ShapeDtypeStruct((BATCH, SEQ, HYENA_W), BF16),
        scratch_shapes=[half_f32, half_f32, half_f32, half_f32,
                        pltpu.VMEM((HALF, CT), BF16), pltpu.VMEM((HALF, CT), BF16),
                        pltpu.VMEM((SEQ, CT), BF16), pltpu.VMEM((SEQ, CT), BF16)],
        compiler_params=_params(("parallel", "parallel")),
        name="hyena_conv",
    )(zh, zh, zh, conv_w, conv_w, conv_w, conv_b, conv_b, conv_b, hbias, kf, kn, t1, t2, t1t, t2t)


def _attn_body(sink_ref, q_ref, k_ref, v_ref, kc_ref, vc_ref, bias_ref, o_ref, kp_ref, vlo_ref, vhi_ref):
    nb = SEQ // BLOCK
    lane = lax.broadcasted_iota(I32, (BLOCK, LANES), 1)
    low = lane < HEAD_DIM
    mask_lo = jnp.where(low, 1.0, 0.0).astype(BF16)
    mask_hi = jnp.where(low, 0.0, 1.0).astype(BF16)

    def with_ones(v):
        lane_v = lax.broadcasted_iota(I32, v.shape, 1) < HEAD_DIM
        one = jnp.ones_like(v)
        return jnp.where(lane_v, v, one), jnp.where(lane_v, one, v)

    zpad = jnp.zeros((BLOCK, KV_W), BF16)
    kp_ref[0:BLOCK] = zpad
    kp_ref[BLOCK:BLOCK + SEQ] = k_ref[0]
    kp_ref[BLOCK + SEQ:] = zpad
    v_lo, v_hi = with_ones(v_ref[0])
    for ref, val in ((vlo_ref, v_lo), (vhi_ref, v_hi)):
        ref[0:BLOCK] = zpad
        ref[BLOCK:BLOCK + SEQ] = val
        ref[BLOCK + SEQ:] = zpad
    kc = kc_ref[0]
    vc_pair = with_ones(vc_ref[0])

    def scores_of(n):
        r = pl.multiple_of(n * BLOCK, BLOCK)
        kw = kp_ref[pl.ds(r, 3 * BLOCK), :]
        bias = bias_ref[jnp.where(n == 0, 0, jnp.where(n == nb - 1, 2, 1))]
        scores = []
        for j in range(GROUP):
            qs = q_ref[0, pl.ds(r, BLOCK), j * LANES:(j + 1) * LANES]
            for msk in (mask_lo, mask_hi):
                qm = qs * msk
                scores.append((_dot_nt(qm, kw) + bias, _dot_nt(qm, kc)))
        return scores

    def finish(n, scores):
        r = pl.multiple_of(n * BLOCK, BLOCK)
        vw_pair = (vlo_ref[pl.ds(r, 3 * BLOCK), :], vhi_ref[pl.ds(r, 3 * BLOCK), :])
        probs = []
        for idx, (sw, sc) in enumerate(scores):
            snk = sink_ref[idx // 2 + GROUP * (idx % 2)]
            m = jnp.maximum(jnp.maximum(jnp.max(sw, axis=-1, keepdims=True),
                                        jnp.max(sc, axis=-1, keepdims=True)), snk)
            probs.append((jnp.exp2(sw - m).astype(BF16), jnp.exp2(sc - m).astype(BF16), jnp.exp2(snk - m)))
        outs = []
        for idx, (pw, pc, psink) in enumerate(probs):
            acc = _dot(pw, vw_pair[idx % 2]) + _dot(pc, vc_pair[idx % 2])
            den = pltpu.roll(acc, HEAD_DIM, axis=1) + psink
            outs.append(acc / den)
        for j in range(GROUP):
            o_ref[0, pl.ds(r, BLOCK), j * LANES:(j + 1) * LANES] = jnp.where(
                low, outs[2 * j], outs[2 * j + 1]).astype(BF16)

    def block_group(g, carry):
        first = g * ATTN_UNROLL
        pending = scores_of(first)
        for i in range(ATTN_UNROLL):
            nxt = scores_of(first + i + 1) if i + 1 < ATTN_UNROLL else None
            finish(first + i, pending)
            pending = nxt
        return carry

    lax.fori_loop(0, nb // ATTN_UNROLL, block_group, 0)


def _attention(sink, q, k, v, kc, vc, bias):
    per_b = lambda n, w: pl.BlockSpec((1, n, w), lambda b: (b, 0, 0))
    return pl.pallas_call(
        _attn_body,
        grid=(BATCH,),
        in_specs=[pl.BlockSpec(memory_space=pltpu.SMEM),
                  per_b(SEQ, ATTN_W), per_b(SEQ, KV_W), per_b(SEQ, KV_W), per_b(CTX_LEN, KV_W), per_b(CTX_LEN, KV_W),
                  pl.BlockSpec((3, BLOCK, 3 * BLOCK), lambda b: (0, 0, 0))],
        out_specs=per_b(SEQ, ATTN_W),
        out_shape=jax.ShapeDtypeStruct((BATCH, SEQ, ATTN_W), BF16),
        scratch_shapes=[pltpu.VMEM((SEQ + 2 * BLOCK, KV_W), BF16)] * 3,
        compiler_params=_params(("parallel",)),
        name="window_attn",
    )(sink, q, k, v, kc, vc, bias)


def _merge_body(ya_ref, yb_ref, gt_ref, x_ref, g1_ref, sc2_ref, sh2_ref, n2_ref, wa_ref, wb_ref, wo_ref, rt_ref,
                xn_ref, h2_ref, aff_ref):
    d = D_MODEL
    groups = [slice(i * SUB_MERGE, (i + 1) * SUB_MERGE) for i in range(TM_MERGE // SUB_MERGE)]
    branch = [(_dot(ya_ref[0, s, :], wa_ref[...]), _dot(yb_ref[0, s, :], wb_ref[...])) for s in groups]
    mixed = [(gt_ref[0, s, :d].astype(F32) * ua + gt_ref[0, s, d:].astype(F32) * ub).astype(BF16)
             for s, (ua, ub) in zip(groups, branch)]
    proj = [_dot(u, wo_ref[...]) for u in mixed]
    h2_rows = []
    for s, m in zip(groups, proj):
        xn = x_ref[0, s, :] + g1_ref[0] * m
        xn_ref[0, s, :] = xn
        h2 = _rms_mod(xn, n2_ref[...], sc2_ref[0], sh2_ref[0]).astype(BF16)
        h2_ref[0, s, :] = h2
        h2_rows.append(h2)
    logits = [_dot_nt(rt_ref[...], h2) for h2 in h2_rows]
    for s, lt in zip(groups, logits):
        e = jnp.exp(lt - jnp.max(lt, axis=0, keepdims=True))
        aff_ref[0, :, s] = e / jnp.sum(e, axis=0, keepdims=True)


def _merge(ya, yb, gates, x, mods3, norm2, wa, wb, wo, router_t):
    d = D_MODEL
    nt = SEQ // TM_MERGE
    const = lambda shape: pl.BlockSpec(shape, lambda b, i: (0,) * len(shape))
    tok = lambda w: pl.BlockSpec((1, TM_MERGE, w), lambda b, i: (b, i, 0))
    mod = lambda k: pl.BlockSpec((1, 1, d), lambda b, i: (b, 0, k))
    return pl.pallas_call(
        _merge_body,
        grid=(BATCH, nt),
        in_specs=[tok(HYENA_W), tok(ATTN_W), tok(2 * d), tok(d), mod(2), mod(4), mod(3), const((1, d)),
                  const((HYENA_W, d)), const((ATTN_W, d)), const((d, d)), const((N_EXPERTS, d))],
        out_specs=[tok(d), tok(d), pl.BlockSpec((1, N_EXPERTS, TM_MERGE), lambda b, i: (b, 0, i))],
        out_shape=[jax.ShapeDtypeStruct((BATCH, SEQ, d), F32),
                   jax.ShapeDtypeStruct((BATCH, SEQ, d), BF16),
                   jax.ShapeDtypeStruct((BATCH, N_EXPERTS, SEQ), F32)],
        compiler_params=_params(("parallel", "parallel")),
        name="merge_outproj",
    )(ya, yb, gates, x, mods3, mods3, mods3, norm2, wa, wb, wo, router_t)


def _thresh_body(aff_ref, lo_ref, hi_ref):
    aff = aff_ref[...]
    rows = aff.shape[0]
    bits = pltpu.bitcast(aff, I32)

    def bisect_bits(_, carry):
        lo, hi = carry
        mid = lo + ((hi - lo) >> 1)
        ge = jnp.sum((bits >= mid).astype(I32), axis=1, keepdims=True) >= CAP
        return jnp.where(ge, mid, lo), jnp.where(ge, hi, mid)

    lo0 = jnp.zeros((rows, 1), I32)
    hi0 = jnp.full((rows, 1), 0x3F800001, I32)
    thr_bits, _ = lax.fori_loop(0, 31, bisect_bits, (lo0, hi0))
    thr = pltpu.bitcast(thr_bits, F32)

    def bisect_val(_, carry):
        lo, hi = carry
        mid = 0.5 * (lo + hi)
        ge = jnp.sum(jnp.where(aff >= mid, 1.0, 0.0), axis=1, keepdims=True) >= CAP
        return jnp.where(ge, mid, lo), jnp.where(ge, hi, mid)

    lo, hi = lax.fori_loop(0, 30, bisect_val, (0.5 * thr, jnp.maximum(2.0 * thr, 1e-30)))
    lo_ref[...] = jnp.broadcast_to(lo, lo_ref.shape)
    hi_ref[...] = jnp.broadcast_to(hi, hi_ref.shape)


def _thresholds(aff_rows):
    rows = BATCH * N_EXPERTS
    return pl.pallas_call(
        _thresh_body,
        grid=(1,),
        in_specs=[pl.BlockSpec((rows, SEQ), lambda i: (0, 0))],
        out_specs=[pl.BlockSpec((rows, LANES), lambda i: (0, 0)), pl.BlockSpec((rows, LANES), lambda i: (0, 0))],
        out_shape=[jax.ShapeDtypeStruct((rows, LANES), F32), jax.ShapeDtypeStruct((rows, LANES), F32)],
        compiler_params=_params(("arbitrary",)),
        name="route_threshold",
    )(aff_rows)


def _prefix_counts(mask):
    r = lax.broadcasted_iota(I32, (LANES, LANES), 0)
    c = lax.broadcasted_iota(I32, (LANES, LANES), 1)
    upper = jnp.where(r <= c, 1.0, 0.0).astype(BF16)
    offset = jnp.zeros((mask.shape[0], 1), F32)
    blocks = []
    for j in range(mask.shape[1] // LANES):
        blk = mask[:, j * LANES:(j + 1) * LANES]
        inc = _dot(blk.astype(BF16), upper)
        blocks.append(inc - blk + offset)
        offset = offset + inc[:, LANES - 1:LANES]
    return jnp.concatenate(blocks, axis=1)


def _route_body(aff_ref, lo_ref, hi_ref, h_ref, slot_ref, xin_ref, pos_ref, pbuf_ref):
    aff = aff_ref[0]
    above = jnp.where(aff >= hi_ref[:, 0:1], 1.0, 0.0)
    band = jnp.where(aff >= lo_ref[:, 0:1], 1.0, 0.0) - above
    need = CAP - jnp.sum(above, axis=1, keepdims=True)
    tie_rank = _prefix_counts(band)
    self = above + band * jnp.where(tie_rank < need, 1.0, 0.0)
    pos = _prefix_counts(self)
    posi = jnp.where(self > 0.5, pos.astype(I32), -1)
    pos_ref[0] = posi

    h = h_ref[0]
    slot = slot_ref[...]
    posb = jnp.where(self > 0.5, pos, -1.0).astype(BF16)
    one = jnp.ones((CAP, SEQ), BF16)
    zero = jnp.zeros((CAP, SEQ), BF16)
    for grp in range(N_EXPERTS // EXPERT_GROUP):
        for i in range(EXPERT_GROUP):
            e = grp * EXPERT_GROUP + i
            pbuf_ref[i * CAP:(i + 1) * CAP, :] = jnp.where(posb[e:e + 1, :] == slot, one, zero)
        rows = _dot(pbuf_ref[...], h)
        for i in range(EXPERT_GROUP):
            xin_ref[grp * EXPERT_GROUP + i] = rows[i * CAP:(i + 1) * CAP].astype(BF16)


def _route(aff_t, lo, hi, h2, slot_rows):
    d = D_MODEL
    return pl.pallas_call(
        _route_body,
        grid=(BATCH,),
        in_specs=[pl.BlockSpec((1, N_EXPERTS, SEQ), lambda b: (b, 0, 0)),
                  pl.BlockSpec((N_EXPERTS, LANES), lambda b: (b, 0)),
                  pl.BlockSpec((N_EXPERTS, LANES), lambda b: (b, 0)),
                  pl.BlockSpec((1, SEQ, d), lambda b: (b, 0, 0)),
                  pl.BlockSpec((CAP, SEQ), lambda b: (0, 0))],
        out_specs=[pl.BlockSpec((N_EXPERTS, CAP, d), lambda b: (0, b, 0)),
                   pl.BlockSpec((1, N_EXPERTS, SEQ), lambda b: (b, 0, 0))],
        out_shape=[jax.ShapeDtypeStruct((N_EXPERTS, BATCH * CAP, d), BF16),
                   jax.ShapeDtypeStruct((BATCH, N_EXPERTS, SEQ), I32)],
        scratch_shapes=[pltpu.VMEM((EXPERT_GROUP * CAP, SEQ), BF16)],
        compiler_params=_params(("parallel",)),
        name="route_gather",
    )(aff_t, lo, hi, h2, slot_rows)


def _expert_body(x_ref, wg_ref, wu_ref, wd_ref, o_ref, acc_ref, wgb_ref, wub_ref, wdb_ref):
    f = pl.program_id(1)

    @pl.when((pl.program_id(0) == 0) & (f == 0))
    def _():
        acc_ref[...] = jnp.zeros_like(acc_ref)

    carry = f > 0
    for m in range(BATCH * CAP // MC):
        rows = slice(m * MC, (m + 1) * MC)
        xm = x_ref[0, rows, :]
        if m == 0:
            wgb_ref[...] = wg_ref[0].astype(BF16)
        a = _dot(xm, wgb_ref[...])
        if m == 0:
            wub_ref[...] = wu_ref[0].astype(BF16)
        b = _dot(xm, wub_ref[...])
        hh = (a * jax.nn.sigmoid(a) * b).astype(BF16)
        if m == 0:
            wdb_ref[...] = wd_ref[0].astype(BF16)
        total = jnp.where(carry, acc_ref[rows, :], 0.0) + _dot(hh, wdb_ref[...])
        acc_ref[rows, :] = total
        o_ref[0, rows, :] = total.astype(BF16)


def _experts(xin, w_gate, w_up, w_down):
    d = D_MODEL
    rows = BATCH * CAP
    return pl.pallas_call(
        _expert_body,
        grid=(N_EXPERTS, D_FF // TF),
        in_specs=[pl.BlockSpec((1, rows, d), lambda e, f: (e, 0, 0)),
                  pl.BlockSpec((1, d, TF), lambda e, f: (e, 0, f)),
                  pl.BlockSpec((1, d, TF), lambda e, f: (e, 0, f)),
                  pl.BlockSpec((1, TF, d), lambda e, f: (e, f, 0))],
        out_specs=pl.BlockSpec((1, rows, d), lambda e, f: (e, 0, 0)),
        out_shape=jax.ShapeDtypeStruct((N_EXPERTS, rows, d), BF16),
        scratch_shapes=[pltpu.VMEM((rows, d), F32), pltpu.VMEM((d, TF), BF16), pltpu.VMEM((d, TF), BF16),
                        pltpu.VMEM((TF, d), BF16)],
        compiler_params=_params(("parallel", "arbitrary")),
        name="swiglu_experts",
    )(xin, w_gate, w_up, w_down)


def _scatter_body(pos_ref, aff_ref, y_ref, x_ref, g2_ref, o_ref):
    pos = pos_ref[0].T
    aff = aff_ref[0].T
    slot = lax.broadcasted_iota(I32, (SUB_S, CAP), 1)
    groups = [slice(i * SUB_S, (i + 1) * SUB_S) for i in range(TS // SUB_S)]
    onehots = [jnp.concatenate([jnp.where(pos[s, e:e + 1] == slot, aff[s, e:e + 1], 0.0).astype(BF16)
                                for e in range(N_EXPERTS)], axis=1) for s in groups]
    y = y_ref[...].reshape(N_EXPERTS * CAP, D_MODEL)
    mixed = [_dot(oh, y) for oh in onehots]
    for s, m in zip(groups, mixed):
        o_ref[0, s, :] = x_ref[0, s, :] + g2_ref[0] * m


def _scatter(pos, aff_t, y, xn, mods3):
    d = D_MODEL
    return pl.pallas_call(
        _scatter_body,
        grid=(BATCH, SEQ // TS),
        in_specs=[pl.BlockSpec((1, N_EXPERTS, TS), lambda b, i: (b, 0, i)),
                  pl.BlockSpec((1, N_EXPERTS, TS), lambda b, i: (b, 0, i)),
                  pl.BlockSpec((N_EXPERTS, CAP, d), lambda b, i: (0, b, 0)),
                  pl.BlockSpec((1, TS, d), lambda b, i: (b, i, 0)),
                  pl.BlockSpec((1, 1, d), lambda b, i: (b, 0, 5))],
        out_specs=pl.BlockSpec((1, TS, d), lambda b, i: (b, i, 0)),
        out_shape=jax.ShapeDtypeStruct((BATCH, SEQ, d), F32),
        compiler_params=_params(("parallel", "parallel")),
        name="scatter_residual",
    )(pos, aff_t, y, xn, mods3)


def _rope_tables():
    rows = SEQ // GRID_W
    row = np.repeat(np.arange(rows, dtype=np.float32), GRID_W)
    col = np.tile(np.arange(GRID_W, dtype=np.float32), rows)
    inv = (ROPE_BASE ** (-np.arange(0, AXIS_ROT, 2, dtype=np.float32) / AXIS_ROT)).astype(np.float32)
    ang = np.concatenate([row[:, None] * inv, col[:, None] * inv], axis=-1).astype(np.float64)
    cos = np.repeat(np.cos(ang), 2, axis=-1)
    sin = np.stack([-np.sin(ang), np.sin(ang)], axis=-1).reshape(SEQ, HEAD_DIM)
    reps = LANES // HEAD_DIM
    return jnp.asarray(np.tile(cos, (1, reps)), F32), jnp.asarray(np.tile(sin, (1, reps)), F32)


def _dft_tables():
    idx = np.arange(HALF, dtype=np.int64)
    t2p1 = 2 * idx + 1

    def cos_sin(f):
        ang = ((f[:, None] * t2p1[None, :]) % (2 * N_FFT)) * (math.pi / N_FFT)
        return np.cos(ang), np.sin(ang)

    ce, se = cos_sin(2 * idx)
    co, so = cos_sin(2 * idx + 1)
    t1 = jnp.asarray(np.concatenate([ce, so], axis=0), F32).astype(BF16)
    t2 = jnp.asarray(np.concatenate([co, se], axis=0), F32).astype(BF16)
    return t1, t2, t1.T, t2.T


def _phase_tables():
    idx = np.arange(HALF, dtype=np.float64)
    w = np.full((HALF,), 2.0 / N_FFT)
    we = w.copy()
    we[0] = 1.0 / N_FFT
    pe = (math.pi / N_FFT) * (2.0 * idx)
    po = (math.pi / N_FFT) * (2.0 * idx + 1.0)
    rot = np.stack([we * np.cos(pe), we * np.sin(pe), w * np.cos(po), w * np.sin(po)])
    return jnp.asarray(np.broadcast_to(rot[:, :, None], (4, HALF, CT)), F32)


def _fold_rows(a):
    return np.concatenate([a[:HALF], a[HALF:][::-1]], axis=0)


def _filter_features():
    t = np.linspace(0.0, 1.0, SEQ, dtype=np.float32).astype(np.float64)[:, None]
    w = 2.0 * math.pi * np.arange(SEQ, dtype=np.float64)[:, None] / SEQ
    fr = np.linspace(1e-4, FILTER_BANDS - 1, FILTER_BANDS, dtype=np.float32).astype(np.float64)[None, :]
    feat = np.concatenate([t, np.cos(fr * w), -np.sin(fr * w)], axis=-1)
    feat = np.pad(feat, ((0, 0), (0, FILTER_HIDDEN - FILTER_EMB)))
    min_decay = math.log(DECAY_TARGET) / SLOW_DECAY_PCT
    max_decay = math.log(DECAY_TARGET) / FAST_DECAY_PCT
    deltas = np.linspace(min_decay, max_decay, HYENA_W, dtype=np.float32).astype(np.float64)
    decay = np.exp(-t * np.abs(deltas))
    return jnp.asarray(_fold_rows(feat).T, F32), jnp.asarray(_fold_rows(decay), F32)


def _attn_bias():
    qi = np.arange(BLOCK)[:, None]
    kj = np.arange(3 * BLOCK)[None, :]
    band = np.abs(kj - BLOCK - qi) <= WINDOW
    first = band & (kj >= BLOCK)
    last = band & (kj < 2 * BLOCK)
    return jnp.asarray(np.where(np.stack([first, band, last]), 0.0, NEG), F32)


def _pair_heads(w, axis):
    heads = [lax.slice_in_dim(w, h * HEAD_DIM, (h + 1) * HEAD_DIM, axis=axis) for h in range(N_HEADS)]
    return jnp.concatenate([heads[j + GROUP * half] for j in range(GROUP) for half in range(N_KV_HEADS)], axis=axis)


def kernel(x, c, ctx, c_ctx, ada_w, ada_b, norm1, norm2, w_in, conv_w, conv_b, filt_w1, filt_b1, filt_w2, filt_b2,
           filt_w3, filt_b3, filt_freq, filt_out, hyena_bias, q_norm, k_norm, attn_sink, w_branch_a, w_branch_b,
           w_out, router, w_gate, w_up, w_down):
    d = D_MODEL
    assert ada_w.shape[0] == 1, "only the single-layer configuration is implemented"
    l = 0
    cos_t, sin_t = _rope_tables()
    t1, t2, t1t, t2t = _dft_tables()
    rot = _phase_tables()
    feat, decay = _filter_features()
    bias = _attn_bias()
    gmat = jnp.asarray(np.kron(np.eye(2 * LANES // HEAD_DIM), np.full((HEAD_DIM, HEAD_DIM), 1.0 / HEAD_DIM)), BF16)
    c16 = jnp.concatenate([c, c_ctx[None, :], jnp.zeros((MOD_ROWS - BATCH - 1, d), F32)], axis=0)

    mods3 = _ada(c16, ada_w[l], ada_b[l][None, :]).reshape(MOD_ROWS, 1, 6 * d)
    n1 = norm1[l][None, :]
    w_packed = _pack_in_weights(w_in[l])
    gk = jnp.tile(k_norm[l], N_KV_HEADS)[None, :]
    kf, kn = _filters(feat, jnp.pad(filt_w1[l].T, ((0, 0), (0, FILTER_HIDDEN - FILTER_EMB))), filt_b1[l][:, None],
                      filt_w2[l].T, filt_b2[l][:, None], filt_w3[l].T, filt_b3[l][:, None], filt_freq[l][:, None],
                      filt_out[l], decay, rot, t1, t2)
    kc, vc = _ctx_proj(ctx, mods3, n1, w_packed, gk, gmat)
    zh, q, k, v, gates = _inproj(x, mods3, n1, w_packed, jnp.tile(q_norm[l], N_HEADS)[None, :], gk, gmat, cos_t, sin_t)
    ya = _hyena(zh, conv_w[l], conv_b[l][None, :], hyena_bias[l], kf, kn, t1, t2, t1t, t2t)
    yb = _attention(attn_sink[l] * LOG2E, q, k, v, kc, vc, bias)
    xn, h2, aff_t = _merge(ya, yb, gates, x, mods3, norm2[l][None, :], w_branch_a[l].astype(BF16),
                           _pair_heads(w_branch_b[l], 0).astype(BF16), w_out[l].astype(BF16), router[l].T.astype(BF16))
    lo, hi = _thresholds(aff_t.reshape(BATCH * N_EXPERTS, SEQ))
    slot_rows = jnp.asarray(np.broadcast_to(np.arange(CAP)[:, None], (CAP, SEQ)), BF16)
    xin, pos = _route(aff_t, lo, hi, h2, slot_rows)
    y = _experts(xin, w_gate[l], w_up[l], w_down[l])
    return _scatter(pos, aff_t, y, xn, mods3)
```

```python
import math

import numpy as np
import jax
import jax.numpy as jnp
from jax import lax
from jax.experimental import pallas as pl
from jax.experimental.pallas import tpu as pltpu

F32 = jnp.float32
BF16 = jnp.bfloat16
I32 = jnp.int32
HIGHEST = lax.Precision.HIGHEST

D_MODEL = 1024
BATCH = 8
SEQ = 2048
GRID_W = 64
CTX_LEN = 256
N_HEADS = 8
N_KV_HEADS = 2
HEAD_DIM = 64
GROUP = N_HEADS // N_KV_HEADS
ATTN_W = N_HEADS * HEAD_DIM
KV_W = N_KV_HEADS * HEAD_DIM
WINDOW = 128
BLOCK = 128
HYENA_W = D_MODEL // 2
HYENA_ORDER = 2
FILTER_BANDS = 16
FILTER_EMB = 1 + 2 * FILTER_BANDS
FILTER_HIDDEN = 64
DECAY_TARGET = 1e-2
FAST_DECAY_PCT = 0.3
SLOW_DECAY_PCT = 1.5
ROPE_BASE = 10000.0
AXIS_ROT = HEAD_DIM // 2
N_EXPERTS = 16
EC_CAPACITY = 2
D_FF = 2048
EPS = 1e-6
NEG = -1e30
LOG2E = math.log2(math.e)

OFF_Q = 3 * HYENA_W
OFF_K = OFF_Q + ATTN_W
OFF_V = OFF_K + KV_W
OFF_G = OFF_V + KV_W
IN_W = OFF_G + 2 * D_MODEL

CAP = EC_CAPACITY * SEQ // N_EXPERTS
N_FFT = 2 * SEQ
HALF = SEQ // 2
MOD_ROWS = 16
LANES = 128

TM_IN = 1024
SUB_IN = 256
TM_MERGE = 1024
SUB_MERGE = 512
CT = 256
FC = 512
RB = 256
TF = 512
MC = 512
TS = 1024
SUB_S = 512
CTX_STEP = 4
EXPERT_GROUP = 4
ATTN_UNROLL = 8
VMEM_LIMIT = 56 * 1024 * 1024


def _dot(a, b, precision=None):
    return jnp.dot(a, b, preferred_element_type=F32, precision=precision)


def _dot_nt(a, b, precision=None):
    return lax.dot_general(a, b, (((1,), (1,)), ((), ())), preferred_element_type=F32, precision=precision)


def _params(sem, vmem=VMEM_LIMIT):
    return pltpu.CompilerParams(dimension_semantics=sem, vmem_limit_bytes=vmem)


def _rms_mod(x, g, sc, sh):
    ms = jnp.mean(x * x, axis=-1, keepdims=True)
    return (x * lax.rsqrt(ms + EPS) * g) * (1.0 + sc) + sh


def _head_norm_rope(z, g, gmat, cos, sin, scale):
    ms = _dot((z * z).astype(BF16), gmat)
    y = z * lax.rsqrt(ms + EPS) * g
    if cos is not None:
        slabs = []
        for s in range(z.shape[1] // LANES):
            ys = y[:, s * LANES:(s + 1) * LANES]
            lane = lax.broadcasted_iota(I32, ys.shape, 1)
            nxt = pltpu.roll(ys, LANES - 1, axis=1)
            prv = pltpu.roll(ys, 1, axis=1)
            slabs.append(ys * cos + jnp.where((lane & 1) == 0, nxt, prv) * sin)
        y = slabs[0] if len(slabs) == 1 else jnp.concatenate(slabs, axis=1)
    return y * scale


def _split_bf16(x):
    hi = x.astype(BF16)
    return hi, (x - hi.astype(F32)).astype(BF16)


def _ada_body(c_ref, w_ref, b_ref, o_ref):
    c = c_ref[...]
    s_hi, s_lo = _split_bf16(c * jax.nn.sigmoid(c))
    w_hi, w_lo = _split_bf16(w_ref[...])
    o_ref[...] = _dot(s_hi, w_hi) + _dot(s_lo, w_hi) + _dot(s_hi, w_lo) + b_ref[...]


def _ada(c16, w, b):
    d = D_MODEL
    return pl.pallas_call(
        _ada_body,
        grid=(6,),
        in_specs=[pl.BlockSpec((MOD_ROWS, d), lambda j: (0, 0)),
                  pl.BlockSpec((d, d), lambda j: (0, j)),
                  pl.BlockSpec((1, d), lambda j: (0, j))],
        out_specs=pl.BlockSpec((MOD_ROWS, d), lambda j: (0, j)),
        out_shape=jax.ShapeDtypeStruct((MOD_ROWS, 6 * d), F32),
        compiler_params=_params(("parallel",)),
        name="ada_mod",
    )(c16, w, b)


def _sign_rows(n):
    lane = lax.broadcasted_iota(I32, (8, n), 1)
    sub = lax.broadcasted_iota(I32, (8, n), 0)
    sg = jnp.where((lane & 1) == 0, 1.0, -1.0)
    return jnp.where(sub == 0, sg, 0.0).astype(BF16)


def _filt_body(feat_ref, w1_ref, b1_ref, w2_ref, b2_ref, w3_ref, b3_ref, fq_ref, fof_ref, fob_ref, dec_ref,
               rot_ref, t1_ref, t2_ref, kf_ref, kn_ref, hh_ref, hl_ref):
    @pl.when((pl.program_id(0) == 0) & (pl.program_id(1) == 0))
    def _():
        fq = fq_ref[...]
        h = jnp.sin(fq * (_dot(w1_ref[...], feat_ref[...], HIGHEST) + b1_ref[...]))
        h = jnp.sin(fq * (_dot(w2_ref[...], h, HIGHEST) + b2_ref[...]))
        h = jnp.sin(fq * (_dot(w3_ref[...], h, HIGHEST) + b3_ref[...]))
        hh_ref[...], hl_ref[...] = _split_bf16(h.T)

    def taps(fo_ref):
        f_hi, f_lo = _split_bf16(fo_ref[...])
        return _dot(hh_ref[...], f_hi) + _dot(hl_ref[...], f_hi) + _dot(hh_ref[...], f_lo)

    dec = dec_ref[...]
    hf = taps(fof_ref) * dec
    hb = taps(fob_ref) * dec
    row = lax.broadcasted_iota(I32, hf.shape, 0)
    hb = jnp.where(row == 0, 0.0, hb)
    a = hf + hb
    b = hf - hb
    pa = (a[:HALF] + a[HALF:]).astype(BF16)
    ma = (a[:HALF] - a[HALF:]).astype(BF16)
    pb = (b[:HALF] + b[HALF:]).astype(BF16)
    mb = (b[:HALF] - b[HALF:]).astype(BF16)
    t1 = t1_ref[...]
    t2 = t2_ref[...]
    a1 = _dot(t1, pa)
    a2 = _dot(t2, ma)
    b1 = _dot(t1, pb)
    b2 = _dot(t2, mb)
    ce, se, co, so = rot_ref[0], rot_ref[1], rot_ref[2], rot_ref[3]
    kf_ref[0, 0] = a1[:HALF] * ce + a2[HALF:] * se
    kf_ref[0, 1] = b2[HALF:] * ce - b1[:HALF] * se
    kf_ref[0, 2] = a2[:HALF] * co + a1[HALF:] * so
    kf_ref[0, 3] = b1[HALF:] * co - b2[:HALF] * so
    kn_ref[0] = _dot(_sign_rows(HALF), ma)[0:1] * (1.0 / N_FFT)


def _filters(feat, w1, b1, w2, b2, w3, b3, fq, fout, decay, rot, t1, t2):
    nct = HYENA_W // CT
    full = lambda shape: pl.BlockSpec(shape, lambda o, c: (0,) * len(shape))
    return pl.pallas_call(
        _filt_body,
        grid=(HYENA_ORDER, nct),
        in_specs=[full((FILTER_HIDDEN, SEQ)), full((FILTER_HIDDEN, FILTER_HIDDEN)), full((FILTER_HIDDEN, 1)),
                  full((FILTER_HIDDEN, FILTER_HIDDEN)), full((FILTER_HIDDEN, 1)),
                  full((FILTER_HIDDEN, FILTER_HIDDEN)), full((FILTER_HIDDEN, 1)), full((FILTER_HIDDEN, 1)),
                  pl.BlockSpec((FILTER_HIDDEN, CT), lambda o, c: (0, (o * 2 + 0) * nct + c)),
                  pl.BlockSpec((FILTER_HIDDEN, CT), lambda o, c: (0, (o * 2 + 1) * nct + c)),
                  pl.BlockSpec((SEQ, CT), lambda o, c: (0, c)),
                  full((4, HALF, CT)),
                  pl.BlockSpec((SEQ, HALF), lambda o, c: (0, 0), pipeline_mode=pl.Buffered(1)),
                  pl.BlockSpec((SEQ, HALF), lambda o, c: (0, 0), pipeline_mode=pl.Buffered(1))],
        out_specs=[pl.BlockSpec((1, 4, HALF, CT), lambda o, c: (o, 0, 0, c)),
                   pl.BlockSpec((1, 1, CT), lambda o, c: (o, 0, c))],
        out_shape=[jax.ShapeDtypeStruct((HYENA_ORDER, 4, HALF, HYENA_W), F32),
                   jax.ShapeDtypeStruct((HYENA_ORDER, 1, HYENA_W), F32)],
        scratch_shapes=[pltpu.VMEM((SEQ, FILTER_HIDDEN), BF16)] * 2,
        compiler_params=_params(("arbitrary", "arbitrary")),
        name="hyena_filters",
    )(feat, w1, b1, w2, b2, w3, b3, fq, fout, fout, decay, rot, t1, t2)


def _pair_head_lanes(z):
    slabs = [z[:, s * LANES:(s + 1) * LANES] for s in range(ATTN_W // LANES)]
    swapped = [pltpu.roll(sl, HEAD_DIM, axis=1) for sl in slabs]
    low = lax.broadcasted_iota(I32, slabs[0].shape, 1) < HEAD_DIM
    out = []
    for j in range(GROUP):
        first, second = j, j + GROUP
        lo_src = slabs[first // 2] if first % 2 == 0 else swapped[first // 2]
        hi_src = slabs[second // 2] if second % 2 == 1 else swapped[second // 2]
        out.append(jnp.where(low, lo_src, hi_src))
    return jnp.concatenate(out, axis=1)


def _inproj_body(x_ref, sc_ref, sh_ref, n1_ref, wh_ref, wq_ref, wkv_ref, wg_ref, gq_ref, gk_ref, gmat_ref,
                 cos_ref, sin_ref, zh_ref, q_ref, k_ref, v_ref, gate_ref):
    groups = [slice(i * SUB_IN, (i + 1) * SUB_IN) for i in range(TM_IN // SUB_IN)]
    hx = [_rms_mod(x_ref[0, s, :], n1_ref[...], sc_ref[0], sh_ref[0]).astype(BF16) for s in groups]
    for s, h in zip(groups, hx):
        zh_ref[0, s, :] = _dot(h, wh_ref[...]).astype(BF16)
    pair = 2 * LANES
    zq = [_pair_head_lanes(_dot(h, wq_ref[...])) for h in hx]
    for s, z in zip(groups, zq):
        for c in range(ATTN_W // pair):
            sl = slice(c * pair, (c + 1) * pair)
            q_ref[0, s, sl] = _head_norm_rope(z[:, sl], gq_ref[:, sl], gmat_ref[...], cos_ref[s, :], sin_ref[s, :],
                                              LOG2E * HEAD_DIM ** -0.5).astype(BF16)
    zkv = [_dot(h, wkv_ref[...]) for h in hx]
    for s, z in zip(groups, zkv):
        k_ref[0, s, :] = _head_norm_rope(z[:, :KV_W], gk_ref[...], gmat_ref[0:KV_W, 0:KV_W], cos_ref[s, :],
                                         sin_ref[s, :], 1.0).astype(BF16)
        v_ref[0, s, :] = z[:, KV_W:].astype(BF16)
    for s, h in zip(groups, hx):
        gate_ref[0, s, :] = jax.nn.sigmoid(_dot(h, wg_ref[...])).astype(BF16)


def _pack_in_weights(w):
    wb = w.astype(BF16)
    return jnp.concatenate([wb[:, :OFF_K], wb[:, OFF_G:], wb[:, OFF_K:OFF_G]], axis=1)


W_OFF_H = 0
W_OFF_Q = OFF_Q
W_OFF_G = OFF_Q + ATTN_W
W_OFF_KV = W_OFF_G + 2 * D_MODEL


def _inproj(x, mods3, norm1, w_packed, gq, gk, gmat, cos_t, sin_t):
    d = D_MODEL
    nt = SEQ // TM_IN
    const = lambda shape: pl.BlockSpec(shape, lambda b, i: (0,) * len(shape))

    def wcol(width, off):
        assert off % width == 0
        return pl.BlockSpec((d, width), lambda b, i: (0, off // width))

    tok = lambda w: pl.BlockSpec((1, TM_IN, w), lambda b, i: (b, i, 0))
    return pl.pallas_call(
        _inproj_body,
        grid=(BATCH, nt),
        in_specs=[tok(d),
                  pl.BlockSpec((1, 1, d), lambda b, i: (b, 0, 1)),
                  pl.BlockSpec((1, 1, d), lambda b, i: (b, 0, 0)),
                  const((1, d)), wcol(OFF_Q, W_OFF_H), wcol(ATTN_W, W_OFF_Q), wcol(2 * KV_W, W_OFF_KV),
                  wcol(2 * d, W_OFF_G), const((1, ATTN_W)), const((1, KV_W)), const((2 * LANES, 2 * LANES)),
                  pl.BlockSpec((TM_IN, LANES), lambda b, i: (i, 0)),
                  pl.BlockSpec((TM_IN, LANES), lambda b, i: (i, 0))],
        out_specs=[tok(OFF_Q), tok(ATTN_W), tok(KV_W), tok(KV_W), tok(2 * d)],
        out_shape=[jax.ShapeDtypeStruct((BATCH, SEQ, OFF_Q), BF16),
                   jax.ShapeDtypeStruct((BATCH, SEQ, ATTN_W), BF16),
                   jax.ShapeDtypeStruct((BATCH, SEQ, KV_W), BF16),
                   jax.ShapeDtypeStruct((BATCH, SEQ, KV_W), BF16),
                   jax.ShapeDtypeStruct((BATCH, SEQ, 2 * d), BF16)],
        compiler_params=_params(("parallel", "parallel")),
        name="in_proj",
    )(x, mods3, mods3, norm1, w_packed, w_packed, w_packed, w_packed, gq, gk, gmat, cos_t, sin_t)


def _ctx_body(c_ref, sc_ref, sh_ref, n1_ref, wkv_ref, gk_ref, gmat_ref, kc_ref, vc_ref):
    for i in range(CTX_STEP):
        hc = _rms_mod(c_ref[i], n1_ref[...], sc_ref[0], sh_ref[0]).astype(BF16)
        z = _dot(hc, wkv_ref[...])
        kc_ref[i] = _head_norm_rope(z[:, :KV_W], gk_ref[...], gmat_ref[0:KV_W, 0:KV_W], None, None, 1.0).astype(BF16)
        vc_ref[i] = z[:, KV_W:].astype(BF16)


def _ctx_proj(ctx, mods3, norm1, w_packed, gk, gmat):
    d = D_MODEL
    const = lambda shape: pl.BlockSpec(shape, lambda b: (0,) * len(shape))
    return pl.pallas_call(
        _ctx_body,
        grid=(BATCH // CTX_STEP,),
        in_specs=[pl.BlockSpec((CTX_STEP, CTX_LEN, d), lambda b: (b, 0, 0)),
                  pl.BlockSpec((1, 1, d), lambda b: (BATCH, 0, 1)),
                  pl.BlockSpec((1, 1, d), lambda b: (BATCH, 0, 0)),
                  const((1, d)), pl.BlockSpec((d, 2 * KV_W), lambda b: (0, W_OFF_KV // (2 * KV_W))),
                  const((1, KV_W)), const((2 * LANES, 2 * LANES))],
        out_specs=[pl.BlockSpec((CTX_STEP, CTX_LEN, KV_W), lambda b: (b, 0, 0)),
                   pl.BlockSpec((CTX_STEP, CTX_LEN, KV_W), lambda b: (b, 0, 0))],
        out_shape=[jax.ShapeDtypeStruct((BATCH, CTX_LEN, KV_W), BF16),
                   jax.ShapeDtypeStruct((BATCH, CTX_LEN, KV_W), BF16)],
        compiler_params=_params(("parallel",)),
        name="ctx_proj",
    )(ctx, mods3, mods3, norm1, w_packed, gk, gmat)


def _hyena_body(zv_ref, z1_ref, z2_ref, cwv_ref, cw1_ref, cw2_ref, cbv_ref, cb1_ref, cb2_ref, hb_ref, kf_ref,
                kn_ref, t1_ref, t2_ref, t1t_ref, t2t_ref, o_ref, lo_ref, hi_ref, g0lo_ref, g0hi_ref, g1lo_ref, g1hi_ref,
                p_ref, m_ref, za_ref, zb_ref):
    row = lax.broadcasted_iota(I32, (HALF, CT), 0)
    rr = lax.broadcasted_iota(I32, (RB, RB), 0)
    cc = lax.broadcasted_iota(I32, (RB, RB), 1)
    flip = jnp.where(rr + cc == RB - 1, 1.0, 0.0).astype(BF16)
    nrb = HALF // RB

    def reverse_upper_half(z_ref, hi_out):
        for j in range(nrb):
            hi_out[j * RB:(j + 1) * RB, :] = _dot(flip, z_ref[0, SEQ - RB * (j + 1):SEQ - RB * j, :])

    def folded_short_conv(z_ref, w_ref, b_ref, lo_out, hi_out):
        zlo = z_ref[0, 0:HALF, :].astype(F32)
        zhi = hi_out[...]
        w0, w1, w2 = w_ref[0:1, :], w_ref[1:2, :], w_ref[2:3, :]
        first, last = row == 0, row == HALF - 1
        lo_prev = jnp.where(first, 0.0, pltpu.roll(zlo, 1, axis=0))
        lo_next = jnp.where(last, zhi[HALF - 1:HALF, :], pltpu.roll(zlo, HALF - 1, axis=0))
        hi_prev = jnp.where(first, 0.0, pltpu.roll(zhi, 1, axis=0))
        hi_next = jnp.where(last, zlo[HALF - 1:HALF, :], pltpu.roll(zhi, HALF - 1, axis=0))
        lo_out[...] = lo_prev * w0 + zlo * w1 + lo_next * w2 + b_ref[...]
        hi_out[...] = hi_next * w0 + zhi * w1 + hi_prev * w2 + b_ref[...]

    gates = ((g0lo_ref, g0hi_ref), (g1lo_ref, g1hi_ref))
    reverse_upper_half(zv_ref, hi_ref)
    reverse_upper_half(z1_ref, g0hi_ref)
    reverse_upper_half(z2_ref, g1hi_ref)
    folded_short_conv(zv_ref, cwv_ref, cbv_ref, lo_ref, hi_ref)
    folded_short_conv(z1_ref, cw1_ref, cb1_ref, g0lo_ref, g0hi_ref)
    folded_short_conv(z2_ref, cw2_ref, cb2_ref, g1lo_ref, g1hi_ref)
    sign8 = _sign_rows(HALF)
    odd = (lax.broadcasted_iota(I32, (FC, CT), 0) & 1) == 1
    for o, (glo_ref, ghi_ref) in enumerate(gates):
        p_ref[...] = (lo_ref[...] + hi_ref[...]).astype(BF16)
        m_ref[...] = (lo_ref[...] - hi_ref[...]).astype(BF16)
        pv = p_ref[...]
        mv = m_ref[...]
        for c in range(HALF // FC):
            ev = slice(c * FC, (c + 1) * FC)
            od = slice(HALF + c * FC, HALF + (c + 1) * FC)
            xce = _dot(t1_ref[ev, :], pv)
            xso = _dot(t1_ref[od, :], pv)
            xco = _dot(t2_ref[ev, :], mv)
            xse = _dot(t2_ref[od, :], mv)
            kce, kse, kco, kso = kf_ref[o, 0, ev, :], kf_ref[o, 1, ev, :], kf_ref[o, 2, ev, :], kf_ref[o, 3, ev, :]
            za_ref[ev, :] = (xce * kce - xse * kse).astype(BF16)
            za_ref[od, :] = (xco * kso + xso * kco).astype(BF16)
            zb_ref[ev, :] = (xco * kco - xso * kso).astype(BF16)
            zb_ref[od, :] = (xce * kse + xse * kce).astype(BF16)
        zn = _dot(sign8, mv)[0:1] * kn_ref[o]
        bias = hb_ref[o:o + 1, :]
        za = za_ref[...]
        zb = zb_ref[...]
        for c in range(HALF // FC):
            rs = slice(c * FC, (c + 1) * FC)
            half_p = _dot(t1t_ref[rs, :], za)
            half_m = _dot(t2t_ref[rs, :], zb) + jnp.where(odd, -zn, zn)
            lo_ref[rs, :] = glo_ref[rs, :] * (half_p + half_m + bias * lo_ref[rs, :])
            hi_ref[rs, :] = ghi_ref[rs, :] * (half_p - half_m + bias * hi_ref[rs, :])
    o_ref[0, 0:HALF, :] = lo_ref[...].astype(BF16)
    for j in range(nrb):
        o_ref[0, SEQ - RB * (j + 1):SEQ - RB * j, :] = _dot(
            flip, hi_ref[j * RB:(j + 1) * RB, :].astype(BF16)).astype(BF16)


def _hyena(zh, conv_w, conv_b, hbias, kf, kn, t1, t2, t1t, t2t):
    nct = HYENA_W // CT
    zspec = lambda k: pl.BlockSpec((1, SEQ, CT), lambda c, b: (b, 0, k * nct + c))
    wspec = lambda k: pl.BlockSpec((3, CT), lambda c, b: (0, k * nct + c))
    bspec = lambda k: pl.BlockSpec((1, CT), lambda c, b: (0, k * nct + c))
    table = lambda shape: pl.BlockSpec(shape, lambda c, b: (0, 0), pipeline_mode=pl.Buffered(1))
    half_f32 = pltpu.VMEM((HALF, CT), F32)
    return pl.pallas_call(
        _hyena_body,
        grid=(nct, BATCH),
        in_specs=[zspec(0), zspec(1), zspec(2), wspec(0), wspec(1), wspec(2), bspec(0), bspec(1), bspec(2),
                  pl.BlockSpec((HYENA_ORDER, CT), lambda c, b: (0, c)),
                  pl.BlockSpec((HYENA_ORDER, 4, HALF, CT), lambda c, b: (0, 0, 0, c)),
                  pl.BlockSpec((HYENA_ORDER, 1, CT), lambda c, b: (0, 0, c)),
                  table((SEQ, HALF)), table((SEQ, HALF)), table((HALF, SEQ)), table((HALF, SEQ))],
        out_specs=pl.BlockSpec((1, SEQ, CT), lambda c, b: (b, 0, c)),
        out_shape=jax.ShapeDtypeStruct((BATCH, SEQ, HYENA_W), BF16),
        scratch_shapes=[half_f32] * 6 + [
            pltpu.VMEM((HALF, CT), BF16), pltpu.VMEM((HALF, CT), BF16),
            pltpu.VMEM((SEQ, CT), BF16), pltpu.VMEM((SEQ, CT), BF16)],
        compiler_params=_params(("parallel", "parallel")),
        name="hyena_conv",
    )(zh, zh, zh, conv_w, conv_w, conv_w, conv_b, conv_b, conv_b, hbias, kf, kn, t1, t2, t1t, t2t)


def _attn_body(sink_ref, q_ref, k_ref, v_ref, kc_ref, vc_ref, bias_ref, o_ref, kp_ref, vlo_ref, vhi_ref):
    nb = SEQ // BLOCK
    lane = lax.broadcasted_iota(I32, (BLOCK, LANES), 1)
    low = lane < HEAD_DIM
    mask_lo = jnp.where(low, 1.0, 0.0).astype(BF16)
    mask_hi = jnp.where(low, 0.0, 1.0).astype(BF16)

    def with_ones(v):
        lane_v = lax.broadcasted_iota(I32, v.shape, 1) < HEAD_DIM
        one = jnp.ones_like(v)
        return jnp.where(lane_v, v, one), jnp.where(lane_v, one, v)

    zpad = jnp.zeros((BLOCK, KV_W), BF16)
    kp_ref[0:BLOCK] = zpad
    kp_ref[BLOCK:BLOCK + SEQ] = k_ref[0]
    kp_ref[BLOCK + SEQ:] = zpad
    v_lo, v_hi = with_ones(v_ref[0])
    for ref, val in ((vlo_ref, v_lo), (vhi_ref, v_hi)):
        ref[0:BLOCK] = zpad
        ref[BLOCK:BLOCK + SEQ] = val
        ref[BLOCK + SEQ:] = zpad
    kc = kc_ref[0]
    vc_pair = with_ones(vc_ref[0])

    def scores_of(n):
        r = pl.multiple_of(n * BLOCK, BLOCK)
        kw = kp_ref[pl.ds(r, 3 * BLOCK), :]
        bias = bias_ref[jnp.where(n == 0, 0, jnp.where(n == nb - 1, 2, 1))]
        scores = []
        for j in range(GROUP):
            qs = q_ref[0, pl.ds(r, BLOCK), j * LANES:(j + 1) * LANES]
            for msk in (mask_lo, mask_hi):
                qm = qs * msk
                scores.append((_dot_nt(qm, kw) + bias, _dot_nt(qm, kc)))
        return scores

    def finish(n, scores):
        r = pl.multiple_of(n * BLOCK, BLOCK)
        vw_pair = (vlo_ref[pl.ds(r, 3 * BLOCK), :], vhi_ref[pl.ds(r, 3 * BLOCK), :])
        probs = []
        for idx, (sw, sc) in enumerate(scores):
            snk = sink_ref[idx // 2 + GROUP * (idx % 2)]
            m = jnp.maximum(jnp.maximum(jnp.max(sw, axis=-1, keepdims=True),
                                        jnp.max(sc, axis=-1, keepdims=True)), snk)
            probs.append((jnp.exp2(sw - m).astype(BF16), jnp.exp2(sc - m).astype(BF16), jnp.exp2(snk - m)))
        outs = []
        for idx, (pw, pc, psink) in enumerate(probs):
            acc = _dot(pw, vw_pair[idx % 2]) + _dot(pc, vc_pair[idx % 2])
            den = pltpu.roll(acc, HEAD_DIM, axis=1) + psink
            outs.append(acc / den)
        for j in range(GROUP):
            o_ref[0, pl.ds(r, BLOCK), j * LANES:(j + 1) * LANES] = jnp.where(
                low, outs[2 * j], outs[2 * j + 1]).astype(BF16)

    def block_group(g, carry):
        first = g * ATTN_UNROLL
        pending = scores_of(first)
        for i in range(ATTN_UNROLL):
            nxt = scores_of(first + i + 1) if i + 1 < ATTN_UNROLL else None
            finish(first + i, pending)
            pending = nxt
        return carry

    lax.fori_loop(0, nb // ATTN_UNROLL, block_group, 0)


def _attention(sink, q, k, v, kc, vc, bias):
    per_b = lambda n, w: pl.BlockSpec((1, n, w), lambda b: (b, 0, 0))
    return pl.pallas_call(
        _attn_body,
        grid=(BATCH,),
        in_specs=[pl.BlockSpec(memory_space=pltpu.SMEM),
                  per_b(SEQ, ATTN_W), per_b(SEQ, KV_W), per_b(SEQ, KV_W), per_b(CTX_LEN, KV_W), per_b(CTX_LEN, KV_W),
                  pl.BlockSpec((3, BLOCK, 3 * BLOCK), lambda b: (0, 0, 0))],
        out_specs=per_b(SEQ, ATTN_W),
        out_shape=jax.ShapeDtypeStruct((BATCH, SEQ, ATTN_W), BF16),
        scratch_shapes=[pltpu.VMEM((SEQ + 2 * BLOCK, KV_W), BF16)] * 3,
        compiler_params=_params(("parallel",)),
        name="window_attn",
    )(sink, q, k, v, kc, vc, bias)


def _merge_body(ya_ref, yb_ref, gt_ref, x_ref, g1_ref, sc2_ref, sh2_ref, n2_ref, wa_ref, wb_ref, wo_ref, rt_ref,
                xn_ref, h2_ref, aff_ref):
    d = D_MODEL
    groups = [slice(i * SUB_MERGE, (i + 1) * SUB_MERGE) for i in range(TM_MERGE // SUB_MERGE)]
    branch = [(_dot(ya_ref[0, s, :], wa_ref[...]), _dot(yb_ref[0, s, :], wb_ref[...])) for s in groups]
    mixed = [(gt_ref[0, s, :d].astype(F32) * ua + gt_ref[0, s, d:].astype(F32) * ub).astype(BF16)
             for s, (ua, ub) in zip(groups, branch)]
    proj = [_dot(u, wo_ref[...]) for u in mixed]
    h2_rows = []
    for s, m in zip(groups, proj):
        xn = x_ref[0, s, :] + g1_ref[0] * m
        xn_ref[0, s, :] = xn
        h2 = _rms_mod(xn, n2_ref[...], sc2_ref[0], sh2_ref[0]).astype(BF16)
        h2_ref[0, s, :] = h2
        h2_rows.append(h2)
    logits = [_dot_nt(rt_ref[...], h2) for h2 in h2_rows]
    for s, lt in zip(groups, logits):
        e = jnp.exp(lt - jnp.max(lt, axis=0, keepdims=True))
        aff_ref[0, :, s] = e / jnp.sum(e, axis=0, keepdims=True)


def _merge(ya, yb, gates, x, mods3, norm2, wa, wb, wo, router_t):
    d = D_MODEL
    nt = SEQ // TM_MERGE
    const = lambda shape: pl.BlockSpec(shape, lambda b, i: (0,) * len(shape))
    tok = lambda w: pl.BlockSpec((1, TM_MERGE, w), lambda b, i: (b, i, 0))
    mod = lambda k: pl.BlockSpec((1, 1, d), lambda b, i: (b, 0, k))
    return pl.pallas_call(
        _merge_body,
        grid=(BATCH, nt),
        in_specs=[tok(HYENA_W), tok(ATTN_W), tok(2 * d), tok(d), mod(2), mod(4), mod(3), const((1, d)),
                  const((HYENA_W, d)), const((ATTN_W, d)), const((d, d)), const((N_EXPERTS, d))],
        out_specs=[tok(d), tok(d), pl.BlockSpec((1, N_EXPERTS, TM_MERGE), lambda b, i: (b, 0, i))],
        out_shape=[jax.ShapeDtypeStruct((BATCH, SEQ, d), F32),
                   jax.ShapeDtypeStruct((BATCH, SEQ, d), BF16),
                   jax.ShapeDtypeStruct((BATCH, N_EXPERTS, SEQ), F32)],
        compiler_params=_params(("parallel", "parallel")),
        name="merge_outproj",
    )(ya, yb, gates, x, mods3, mods3, mods3, norm2, wa, wb, wo, router_t)


def _thresh_body(aff_ref, lo_ref, hi_ref):
    aff = aff_ref[...]
    rows = aff.shape[0]
    bits = pltpu.bitcast(aff, I32)

    def bisect_bits(_, carry):
        lo, hi = carry
        mid = lo + ((hi - lo) >> 1)
        ge = jnp.sum((bits >= mid).astype(I32), axis=1, keepdims=True) >= CAP
        return jnp.where(ge, mid, lo), jnp.where(ge, hi, mid)

    lo0 = jnp.zeros((rows, 1), I32)
    hi0 = jnp.full((rows, 1), 0x3F800001, I32)
    thr_bits, _ = lax.fori_loop(0, 31, bisect_bits, (lo0, hi0))
    thr = pltpu.bitcast(thr_bits, F32)

    def bisect_val(_, carry):
        lo, hi = carry
        mid = 0.5 * (lo + hi)
        ge = jnp.sum(jnp.where(aff >= mid, 1.0, 0.0), axis=1, keepdims=True) >= CAP
        return jnp.where(ge, mid, lo), jnp.where(ge, hi, mid)

    lo, hi = lax.fori_loop(0, 30, bisect_val, (0.5 * thr, jnp.maximum(2.0 * thr, 1e-30)))
    lo_ref[...] = jnp.broadcast_to(lo, lo_ref.shape)
    hi_ref[...] = jnp.broadcast_to(hi, hi_ref.shape)


def _thresholds(aff_rows):
    rows = BATCH * N_EXPERTS
    return pl.pallas_call(
        _thresh_body,
        grid=(1,),
        in_specs=[pl.BlockSpec((rows, SEQ), lambda i: (0, 0))],
        out_specs=[pl.BlockSpec((rows, LANES), lambda i: (0, 0)), pl.BlockSpec((rows, LANES), lambda i: (0, 0))],
        out_shape=[jax.ShapeDtypeStruct((rows, LANES), F32), jax.ShapeDtypeStruct((rows, LANES), F32)],
        compiler_params=_params(("arbitrary",)),
        name="route_threshold",
    )(aff_rows)


def _prefix_counts(mask):
    r = lax.broadcasted_iota(I32, (LANES, LANES), 0)
    c = lax.broadcasted_iota(I32, (LANES, LANES), 1)
    upper = jnp.where(r <= c, 1.0, 0.0).astype(BF16)
    offset = jnp.zeros((mask.shape[0], 1), F32)
    blocks = []
    for j in range(mask.shape[1] // LANES):
        blk = mask[:, j * LANES:(j + 1) * LANES]
        inc = _dot(blk.astype(BF16), upper)
        blocks.append(inc - blk + offset)
        offset = offset + inc[:, LANES - 1:LANES]
    return jnp.concatenate(blocks, axis=1)


def _route_body(aff_ref, lo_ref, hi_ref, h_ref, slot_ref, xin_ref, pos_ref, pbuf_ref):
    aff = aff_ref[0]
    above = jnp.where(aff >= hi_ref[:, 0:1], 1.0, 0.0)
    band = jnp.where(aff >= lo_ref[:, 0:1], 1.0, 0.0) - above
    need = CAP - jnp.sum(above, axis=1, keepdims=True)
    tie_rank = _prefix_counts(band)
    self = above + band * jnp.where(tie_rank < need, 1.0, 0.0)
    pos = _prefix_counts(self)
    posi = jnp.where(self > 0.5, pos.astype(I32), -1)
    pos_ref[0] = posi

    h = h_ref[0]
    slot = slot_ref[...]
    posb = jnp.where(self > 0.5, pos, -1.0).astype(BF16)
    one = jnp.ones((CAP, SEQ), BF16)
    zero = jnp.zeros((CAP, SEQ), BF16)
    for grp in range(N_EXPERTS // EXPERT_GROUP):
        for i in range(EXPERT_GROUP):
            e = grp * EXPERT_GROUP + i
            pbuf_ref[i * CAP:(i + 1) * CAP, :] = jnp.where(posb[e:e + 1, :] == slot, one, zero)
        rows = _dot(pbuf_ref[...], h)
        for i in range(EXPERT_GROUP):
            xin_ref[grp * EXPERT_GROUP + i] = rows[i * CAP:(i + 1) * CAP].astype(BF16)


def _route(aff_t, lo, hi, h2, slot_rows):
    d = D_MODEL
    return pl.pallas_call(
        _route_body,
        grid=(BATCH,),
        in_specs=[pl.BlockSpec((1, N_EXPERTS, SEQ), lambda b: (b, 0, 0)),
                  pl.BlockSpec((N_EXPERTS, LANES), lambda b: (b, 0)),
                  pl.BlockSpec((N_EXPERTS, LANES), lambda b: (b, 0)),
                  pl.BlockSpec((1, SEQ, d), lambda b: (b, 0, 0)),
                  pl.BlockSpec((CAP, SEQ), lambda b: (0, 0))],
        out_specs=[pl.BlockSpec((N_EXPERTS, CAP, d), lambda b: (0, b, 0)),
                   pl.BlockSpec((1, N_EXPERTS, SEQ), lambda b: (b, 0, 0))],
        out_shape=[jax.ShapeDtypeStruct((N_EXPERTS, BATCH * CAP, d), BF16),
                   jax.ShapeDtypeStruct((BATCH, N_EXPERTS, SEQ), I32)],
        scratch_shapes=[pltpu.VMEM((EXPERT_GROUP * CAP, SEQ), BF16)],
        compiler_params=_params(("parallel",)),
        name="route_gather",
    )(aff_t, lo, hi, h2, slot_rows)


def _expert_body(x_ref, wg_ref, wu_ref, wd_ref, o_ref, acc_ref, wgb_ref, wub_ref, wdb_ref):
    f = pl.program_id(1)

    @pl.when((pl.program_id(0) == 0) & (f == 0))
    def _():
        acc_ref[...] = jnp.zeros_like(acc_ref)

    carry = f > 0
    for m in range(BATCH * CAP // MC):
        rows = slice(m * MC, (m + 1) * MC)
        xm = x_ref[0, rows, :]
        if m == 0:
            wgb_ref[...] = wg_ref[0].astype(BF16)
        a = _dot(xm, wgb_ref[...])
        if m == 0:
            wub_ref[...] = wu_ref[0].astype(BF16)
        b = _dot(xm, wub_ref[...])
        hh = (a * jax.nn.sigmoid(a) * b).astype(BF16)
        if m == 0:
            wdb_ref[...] = wd_ref[0].astype(BF16)
        total = jnp.where(carry, acc_ref[rows, :], 0.0) + _dot(hh, wdb_ref[...])
        acc_ref[rows, :] = total
        o_ref[0, rows, :] = total.astype(BF16)


def _experts(xin, w_gate, w_up, w_down):
    d = D_MODEL
    rows = BATCH * CAP
    return pl.pallas_call(
        _expert_body,
        grid=(N_EXPERTS, D_FF // TF),
        in_specs=[pl.BlockSpec((1, rows, d), lambda e, f: (e, 0, 0)),
                  pl.BlockSpec((1, d, TF), lambda e, f: (e, 0, f)),
                  pl.BlockSpec((1, d, TF), lambda e, f: (e, 0, f)),
                  pl.BlockSpec((1, TF, d), lambda e, f: (e, f, 0))],
        out_specs=pl.BlockSpec((1, rows, d), lambda e, f: (e, 0, 0)),
        out_shape=jax.ShapeDtypeStruct((N_EXPERTS, rows, d), BF16),
        scratch_shapes=[pltpu.VMEM((rows, d), F32), pltpu.VMEM((d, TF), BF16), pltpu.VMEM((d, TF), BF16),
                        pltpu.VMEM((TF, d), BF16)],
        compiler_params=_params(("parallel", "arbitrary")),
        name="swiglu_experts",
    )(xin, w_gate, w_up, w_down)


def _scatter_body(pos_ref, aff_ref, y_ref, x_ref, g2_ref, o_ref):
    pos = pos_ref[0].T
    aff = aff_ref[0].T
    slot = lax.broadcasted_iota(I32, (SUB_S, CAP), 1)
    groups = [slice(i * SUB_S, (i + 1) * SUB_S) for i in range(TS // SUB_S)]
    onehots = [jnp.concatenate([jnp.where(pos[s, e:e + 1] == slot, aff[s, e:e + 1], 0.0).astype(BF16)
                                for e in range(N_EXPERTS)], axis=1) for s in groups]
    y = y_ref[...].reshape(N_EXPERTS * CAP, D_MODEL)
    mixed = [_dot(oh, y) for oh in onehots]
    for s, m in zip(groups, mixed):
        o_ref[0, s, :] = x_ref[0, s, :] + g2_ref[0] * m


def _scatter(pos, aff_t, y, xn, mods3):
    d = D_MODEL
    return pl.pallas_call(
        _scatter_body,
        grid=(BATCH, SEQ // TS),
        in_specs=[pl.BlockSpec((1, N_EXPERTS, TS), lambda b, i: (b, 0, i)),
                  pl.BlockSpec((1, N_EXPERTS, TS), lambda b, i: (b, 0, i)),
                  pl.BlockSpec((N_EXPERTS, CAP, d), lambda b, i: (0, b, 0)),
                  pl.BlockSpec((1, TS, d), lambda b, i: (b, i, 0)),
                  pl.BlockSpec((1, 1, d), lambda b, i: (b, 0, 5))],
        out_specs=pl.BlockSpec((1, TS, d), lambda b, i: (b, i, 0)),
        out_shape=jax.ShapeDtypeStruct((BATCH, SEQ, d), F32),
        compiler_params=_params(("parallel", "parallel")),
        name="scatter_residual",
    )(pos, aff_t, y, xn, mods3)


def _rope_tables():
    rows = SEQ // GRID_W
    row = np.repeat(np.arange(rows, dtype=np.float32), GRID_W)
    col = np.tile(np.arange(GRID_W, dtype=np.float32), rows)
    inv = (ROPE_BASE ** (-np.arange(0, AXIS_ROT, 2, dtype=np.float32) / AXIS_ROT)).astype(np.float32)
    ang = np.concatenate([row[:, None] * inv, col[:, None] * inv], axis=-1).astype(np.float64)
    cos = np.repeat(np.cos(ang), 2, axis=-1)
    sin = np.stack([-np.sin(ang), np.sin(ang)], axis=-1).reshape(SEQ, HEAD_DIM)
    reps = LANES // HEAD_DIM
    return jnp.asarray(np.tile(cos, (1, reps)), F32), jnp.asarray(np.tile(sin, (1, reps)), F32)


def _dft_tables():
    idx = np.arange(HALF, dtype=np.int64)
    t2p1 = 2 * idx + 1

    def cos_sin(f):
        ang = ((f[:, None] * t2p1[None, :]) % (2 * N_FFT)) * (math.pi / N_FFT)
        return np.cos(ang), np.sin(ang)

    ce, se = cos_sin(2 * idx)
    co, so = cos_sin(2 * idx + 1)
    t1 = jnp.asarray(np.concatenate([ce, so], axis=0), F32).astype(BF16)
    t2 = jnp.asarray(np.concatenate([co, se], axis=0), F32).astype(BF16)
    return t1, t2, t1.T, t2.T


def _phase_tables():
    idx = np.arange(HALF, dtype=np.float64)
    w = np.full((HALF,), 2.0 / N_FFT)
    we = w.copy()
    we[0] = 1.0 / N_FFT
    pe = (math.pi / N_FFT) * (2.0 * idx)
    po = (math.pi / N_FFT) * (2.0 * idx + 1.0)
    rot = np.stack([we * np.cos(pe), we * np.sin(pe), w * np.cos(po), w * np.sin(po)])
    return jnp.asarray(np.broadcast_to(rot[:, :, None], (4, HALF, CT)), F32)


def _fold_rows(a):
    return np.concatenate([a[:HALF], a[HALF:][::-1]], axis=0)


def _filter_features():
    t = np.linspace(0.0, 1.0, SEQ, dtype=np.float32).astype(np.float64)[:, None]
    w = 2.0 * math.pi * np.arange(SEQ, dtype=np.float64)[:, None] / SEQ
    fr = np.linspace(1e-4, FILTER_BANDS - 1, FILTER_BANDS, dtype=np.float32).astype(np.float64)[None, :]
    feat = np.concatenate([t, np.cos(fr * w), -np.sin(fr * w)], axis=-1)
    feat = np.pad(feat, ((0, 0), (0, FILTER_HIDDEN - FILTER_EMB)))
    min_decay = math.log(DECAY_TARGET) / SLOW_DECAY_PCT
    max_decay = math.log(DECAY_TARGET) / FAST_DECAY_PCT
    deltas = np.linspace(min_decay, max_decay, HYENA_W, dtype=np.float32).astype(np.float64)
    decay = np.exp(-t * np.abs(deltas))
    return jnp.asarray(_fold_rows(feat).T, F32), jnp.asarray(_fold_rows(decay), F32)


def _attn_bias():
    qi = np.arange(BLOCK)[:, None]
    kj = np.arange(3 * BLOCK)[None, :]
    band = np.abs(kj - BLOCK - qi) <= WINDOW
    first = band & (kj >= BLOCK)
    last = band & (kj < 2 * BLOCK)
    return jnp.asarray(np.where(np.stack([first, band, last]), 0.0, NEG), F32)


def _pair_heads(w, axis):
    heads = [lax.slice_in_dim(w, h * HEAD_DIM, (h + 1) * HEAD_DIM, axis=axis) for h in range(N_HEADS)]
    return jnp.concatenate([heads[j + GROUP * half] for j in range(GROUP) for half in range(N_KV_HEADS)], axis=axis)


def kernel(x, c, ctx, c_ctx, ada_w, ada_b, norm1, norm2, w_in, conv_w, conv_b, filt_w1, filt_b1, filt_w2, filt_b2,
           filt_w3, filt_b3, filt_freq, filt_out, hyena_bias, q_norm, k_norm, attn_sink, w_branch_a, w_branch_b,
           w_out, router, w_gate, w_up, w_down):
    d = D_MODEL
    assert ada_w.shape[0] == 1, "only the single-layer configuration is implemented"
    l = 0
    cos_t, sin_t = _rope_tables()
    t1, t2, t1t, t2t = _dft_tables()
    rot = _phase_tables()
    feat, decay = _filter_features()
    bias = _attn_bias()
    gmat = jnp.asarray(np.kron(np.eye(2 * LANES // HEAD_DIM), np.full((HEAD_DIM, HEAD_DIM), 1.0 / HEAD_DIM)), BF16)
    c16 = jnp.concatenate([c, c_ctx[None, :], jnp.zeros((MOD_ROWS - BATCH - 1, d), F32)], axis=0)

    mods3 = _ada(c16, ada_w[l], ada_b[l][None, :]).reshape(MOD_ROWS, 1, 6 * d)
    n1 = norm1[l][None, :]
    w_packed = _pack_in_weights(w_in[l])
    gk = jnp.tile(k_norm[l], N_KV_HEADS)[None, :]
    kf, kn = _filters(feat, jnp.pad(filt_w1[l].T, ((0, 0), (0, FILTER_HIDDEN - FILTER_EMB))), filt_b1[l][:, None],
                      filt_w2[l].T, filt_b2[l][:, None], filt_w3[l].T, filt_b3[l][:, None], filt_freq[l][:, None],
                      filt_out[l], decay, rot, t1, t2)
    kc, vc = _ctx_proj(ctx, mods3, n1, w_packed, gk, gmat)
    zh, q, k, v, gates = _inproj(x, mods3, n1, w_packed, jnp.tile(q_norm[l], N_HEADS)[None, :], gk, gmat, cos_t, sin_t)
    ya = _hyena(zh, conv_w[l], conv_b[l][None, :], hyena_bias[l], kf, kn, t1, t2, t1t, t2t)
    yb = _attention(attn_sink[l] * LOG2E, q, k, v, kc, vc, bias)
    xn, h2, aff_t = _merge(ya, yb, gates, x, mods3, norm2[l][None, :], w_branch_a[l].astype(BF16),
                           _pair_heads(w_branch_b[l], 0).astype(BF16), w_out[l].astype(BF16), router[l].T.astype(BF16))
    lo, hi = _thresholds(aff_t.reshape(BATCH * N_EXPERTS, SEQ))
    slot_rows = jnp.asarray(np.broadcast_to(np.arange(CAP)[:, None], (CAP, SEQ)), BF16)
    xin, pos = _route(aff_t, lo, hi, h2, slot_rows)
    y = _experts(xin, w_gate[l], w_up[l], w_down[l])
    return _scatter(pos, aff_t, y, xn, mods3)
```

```python
import math

import numpy as np
import jax
import jax.numpy as jnp
from jax import lax
from jax.experimental import pallas as pl
from jax.experimental.pallas import tpu as pltpu

F32 = jnp.float32
BF16 = jnp.bfloat16
I32 = jnp.int32
HIGHEST = lax.Precision.HIGHEST

D_MODEL = 1024
BATCH = 8
SEQ = 2048
GRID_W = 64
CTX_LEN = 256
N_HEADS = 8
N_KV_HEADS = 2
HEAD_DIM = 64
GROUP = N_HEADS // N_KV_HEADS
ATTN_W = N_HEADS * HEAD_DIM
KV_W = N_KV_HEADS * HEAD_DIM
WINDOW = 128
BLOCK = 128
HYENA_W = D_MODEL // 2
HYENA_ORDER = 2
FILTER_BANDS = 16
FILTER_EMB = 1 + 2 * FILTER_BANDS
FILTER_HIDDEN = 64
DECAY_TARGET = 1e-2
FAST_DECAY_PCT = 0.3
SLOW_DECAY_PCT = 1.5
ROPE_BASE = 10000.0
AXIS_ROT = HEAD_DIM // 2
N_EXPERTS = 16
EC_CAPACITY = 2
D_FF = 2048
EPS = 1e-6
NEG = -1e30
LOG2E = math.log2(math.e)

OFF_Q = 3 * HYENA_W
OFF_K = OFF_Q + ATTN_W
OFF_V = OFF_K + KV_W
OFF_G = OFF_V + KV_W

CAP = EC_CAPACITY * SEQ // N_EXPERTS
N_FFT = 2 * SEQ
HALF = SEQ // 2
MOD_ROWS = 16
LANES = 128

TM_IN = 1024
SUB_IN = 256
TM_MERGE = 1024
SUB_MERGE = 512
CT = 256
FC = 512
RB = 256
TF = 512
MC = 512
TS = 1024
SUB_S = 512
CTX_STEP = 4
EXPERT_GROUP = 4
ATTN_UNROLL = 8
VMEM_LIMIT = 56 * 1024 * 1024


def _dot(a, b, precision=None):
    return jnp.dot(a, b, preferred_element_type=F32, precision=precision)


def _dot_nt(a, b, precision=None):
    return lax.dot_general(a, b, (((1,), (1,)), ((), ())), preferred_element_type=F32, precision=precision)


def _params(sem, vmem=VMEM_LIMIT):
    return pltpu.CompilerParams(dimension_semantics=sem, vmem_limit_bytes=vmem)


def _rms_mod(x, g, sc, sh):
    ms = jnp.mean(x * x, axis=-1, keepdims=True)
    return (x * lax.rsqrt(ms + EPS) * g) * (1.0 + sc) + sh


def _head_norm_rope(z, g, gmat, cos, sin, scale):
    ms = _dot((z * z).astype(BF16), gmat)
    y = z * lax.rsqrt(ms + EPS) * g
    if cos is not None:
        slabs = []
        for s in range(z.shape[1] // LANES):
            ys = y[:, s * LANES:(s + 1) * LANES]
            lane = lax.broadcasted_iota(I32, ys.shape, 1)
            nxt = pltpu.roll(ys, LANES - 1, axis=1)
            prv = pltpu.roll(ys, 1, axis=1)
            slabs.append(ys * cos + jnp.where((lane & 1) == 0, nxt, prv) * sin)
        y = slabs[0] if len(slabs) == 1 else jnp.concatenate(slabs, axis=1)
    return y * scale


def _split_bf16(x):
    hi = x.astype(BF16)
    return hi, (x - hi.astype(F32)).astype(BF16)


def _ada_body(c_ref, w_ref, b_ref, o_ref):
    c = c_ref[...]
    s_hi, s_lo = _split_bf16(c * jax.nn.sigmoid(c))
    w_hi, w_lo = _split_bf16(w_ref[...])
    o_ref[:, 0, :] = _dot(s_hi, w_hi) + _dot(s_lo, w_hi) + _dot(s_hi, w_lo) + b_ref[...]


def _ada(c16, w, b):
    d = D_MODEL
    return pl.pallas_call(
        _ada_body,
        grid=(6,),
        in_specs=[pl.BlockSpec((MOD_ROWS, d), lambda j: (0, 0)),
                  pl.BlockSpec((d, d), lambda j: (0, j)),
                  pl.BlockSpec((1, d), lambda j: (0, j))],
        out_specs=pl.BlockSpec((MOD_ROWS, 1, d), lambda j: (0, 0, j)),
        out_shape=jax.ShapeDtypeStruct((MOD_ROWS, 1, 6 * d), F32),
        compiler_params=_params(("parallel",)),
        name="ada_mod",
    )(c16, w, b)


def _sign_rows(n):
    lane = lax.broadcasted_iota(I32, (8, n), 1)
    sub = lax.broadcasted_iota(I32, (8, n), 0)
    sg = jnp.where((lane & 1) == 0, 1.0, -1.0)
    return jnp.where(sub == 0, sg, 0.0).astype(BF16)


def _filt_body(feat_ref, w_ref, col_ref, fof_ref, fob_ref, dec_ref, rot_ref, t1_ref, t2_ref, kf_ref, kn_ref, hh_ref,
               hl_ref):
    @pl.when((pl.program_id(0) == 0) & (pl.program_id(1) == 0))
    def _():
        cols = col_ref[...]
        fq = cols[:, 3:4]
        h = feat_ref[...]
        for layer in range(3):
            h = jnp.sin(fq * (_dot(w_ref[layer], h, HIGHEST) + cols[:, layer:layer + 1]))
        hh_ref[...], hl_ref[...] = _split_bf16(h.T)

    def taps(fo_ref):
        f_hi, f_lo = _split_bf16(fo_ref[...])
        return _dot(hh_ref[...], f_hi) + _dot(hl_ref[...], f_hi) + _dot(hh_ref[...], f_lo)

    dec = dec_ref[...]
    hf = taps(fof_ref) * dec
    hb = taps(fob_ref) * dec
    row = lax.broadcasted_iota(I32, hf.shape, 0)
    hb = jnp.where(row == 0, 0.0, hb)
    a = hf + hb
    b = hf - hb
    pa = (a[:HALF] + a[HALF:]).astype(BF16)
    ma = (a[:HALF] - a[HALF:]).astype(BF16)
    pb = (b[:HALF] + b[HALF:]).astype(BF16)
    mb = (b[:HALF] - b[HALF:]).astype(BF16)
    t1 = t1_ref[...]
    t2 = t2_ref[...]
    a1 = _dot(t1, pa)
    a2 = _dot(t2, ma)
    b1 = _dot(t1, pb)
    b2 = _dot(t2, mb)
    ce, se, co, so = rot_ref[0], rot_ref[1], rot_ref[2], rot_ref[3]
    kf_ref[0, 0] = a1[:HALF] * ce + a2[HALF:] * se
    kf_ref[0, 1] = b2[HALF:] * ce - b1[:HALF] * se
    kf_ref[0, 2] = a2[:HALF] * co + a1[HALF:] * so
    kf_ref[0, 3] = b1[HALF:] * co - b2[:HALF] * so
    kn_ref[0] = _dot(_sign_rows(HALF), ma)[0:1] * (1.0 / N_FFT)


def _filters(feat, mlp_w, mlp_cols, fout, decay, rot, t1, t2):
    nct = HYENA_W // CT
    full = lambda shape: pl.BlockSpec(shape, lambda o, c: (0,) * len(shape))
    return pl.pallas_call(
        _filt_body,
        grid=(HYENA_ORDER, nct),
        in_specs=[full((FILTER_HIDDEN, SEQ)), full((3, FILTER_HIDDEN, FILTER_HIDDEN)), full((FILTER_HIDDEN, 4)),
                  pl.BlockSpec((FILTER_HIDDEN, CT), lambda o, c: (0, (o * 2 + 0) * nct + c)),
                  pl.BlockSpec((FILTER_HIDDEN, CT), lambda o, c: (0, (o * 2 + 1) * nct + c)),
                  pl.BlockSpec((SEQ, CT), lambda o, c: (0, c)),
                  full((4, HALF, CT)),
                  pl.BlockSpec((SEQ, HALF), lambda o, c: (0, 0), pipeline_mode=pl.Buffered(1)),
                  pl.BlockSpec((SEQ, HALF), lambda o, c: (0, 0), pipeline_mode=pl.Buffered(1))],
        out_specs=[pl.BlockSpec((1, 4, HALF, CT), lambda o, c: (o, 0, 0, c)),
                   pl.BlockSpec((1, 1, CT), lambda o, c: (o, 0, c))],
        out_shape=[jax.ShapeDtypeStruct((HYENA_ORDER, 4, HALF, HYENA_W), F32),
                   jax.ShapeDtypeStruct((HYENA_ORDER, 1, HYENA_W), F32)],
        scratch_shapes=[pltpu.VMEM((SEQ, FILTER_HIDDEN), BF16)] * 2,
        compiler_params=_params(("arbitrary", "arbitrary")),
        name="hyena_filters",
    )(feat, mlp_w, mlp_cols, fout, fout, decay, rot, t1, t2)


def _pair_head_lanes(z):
    slabs = [z[:, s * LANES:(s + 1) * LANES] for s in range(ATTN_W // LANES)]
    swapped = [pltpu.roll(sl, HEAD_DIM, axis=1) for sl in slabs]
    low = lax.broadcasted_iota(I32, slabs[0].shape, 1) < HEAD_DIM
    out = []
    for j in range(GROUP):
        first, second = j, j + GROUP
        lo_src = slabs[first // 2] if first % 2 == 0 else swapped[first // 2]
        hi_src = slabs[second // 2] if second % 2 == 1 else swapped[second // 2]
        out.append(jnp.where(low, lo_src, hi_src))
    return jnp.concatenate(out, axis=1)


def _inproj_body(x_ref, sc_ref, sh_ref, n1_ref, wh_ref, wq_ref, wkv_ref, wg_ref, gq_ref, gk_ref, gmat_ref,
                 cos_ref, sin_ref, zh_ref, q_ref, k_ref, v_ref, gate_ref):
    groups = [slice(i * SUB_IN, (i + 1) * SUB_IN) for i in range(TM_IN // SUB_IN)]
    hx = [_rms_mod(x_ref[0, s, :], n1_ref[...], sc_ref[0], sh_ref[0]).astype(BF16) for s in groups]
    for s, h in zip(groups, hx):
        zh_ref[0, s, :] = _dot(h, wh_ref[...]).astype(BF16)
    pair = 2 * LANES
    zq = [_pair_head_lanes(_dot(h, wq_ref[...])) for h in hx]
    for s, z in zip(groups, zq):
        for c in range(ATTN_W // pair):
            sl = slice(c * pair, (c + 1) * pair)
            q_ref[0, s, sl] = _head_norm_rope(z[:, sl], gq_ref[:, sl], gmat_ref[...], cos_ref[s, :], sin_ref[s, :],
                                              LOG2E * HEAD_DIM ** -0.5).astype(BF16)
    zkv = [_dot(h, wkv_ref[...]) for h in hx]
    for s, z in zip(groups, zkv):
        k_ref[0, s, :] = _head_norm_rope(z[:, :KV_W], gk_ref[...], gmat_ref[0:KV_W, 0:KV_W], cos_ref[s, :],
                                         sin_ref[s, :], 1.0).astype(BF16)
        v_ref[0, s, :] = z[:, KV_W:].astype(BF16)
    for s, h in zip(groups, hx):
        gate_ref[0, s, :] = jax.nn.sigmoid(_dot(h, wg_ref[...])).astype(BF16)


def _pack_in_weights(w):
    wb = w.astype(BF16)
    return jnp.concatenate([wb[:, :OFF_K], wb[:, OFF_G:], wb[:, OFF_K:OFF_G]], axis=1)


W_OFF_H = 0
W_OFF_Q = OFF_Q
W_OFF_G = OFF_Q + ATTN_W
W_OFF_KV = W_OFF_G + 2 * D_MODEL


def _inproj(x, mods3, norm1, w_packed, gq, gk, gmat, cos_t, sin_t):
    d = D_MODEL
    nt = SEQ // TM_IN
    const = lambda shape: pl.BlockSpec(shape, lambda b, i: (0,) * len(shape))

    def wcol(width, off):
        assert off % width == 0
        return pl.BlockSpec((d, width), lambda b, i: (0, off // width))

    tok = lambda w: pl.BlockSpec((1, TM_IN, w), lambda b, i: (b, i, 0))
    return pl.pallas_call(
        _inproj_body,
        grid=(BATCH, nt),
        in_specs=[tok(d),
                  pl.BlockSpec((1, 1, d), lambda b, i: (b, 0, 1)),
                  pl.BlockSpec((1, 1, d), lambda b, i: (b, 0, 0)),
                  const((1, d)), wcol(OFF_Q, W_OFF_H), wcol(ATTN_W, W_OFF_Q), wcol(2 * KV_W, W_OFF_KV),
                  wcol(2 * d, W_OFF_G), const((1, ATTN_W)), const((1, KV_W)), const((2 * LANES, 2 * LANES)),
                  pl.BlockSpec((TM_IN, LANES), lambda b, i: (i, 0)),
                  pl.BlockSpec((TM_IN, LANES), lambda b, i: (i, 0))],
        out_specs=[tok(OFF_Q), tok(ATTN_W), tok(KV_W), tok(KV_W), tok(2 * d)],
        out_shape=[jax.ShapeDtypeStruct((BATCH, SEQ, OFF_Q), BF16),
                   jax.ShapeDtypeStruct((BATCH, SEQ, ATTN_W), BF16),
                   jax.ShapeDtypeStruct((BATCH, SEQ, KV_W), BF16),
                   jax.ShapeDtypeStruct((BATCH, SEQ, KV_W), BF16),
                   jax.ShapeDtypeStruct((BATCH, SEQ, 2 * d), BF16)],
        compiler_params=_params(("parallel", "parallel")),
        name="in_proj",
    )(x, mods3, mods3, norm1, w_packed, w_packed, w_packed, w_packed, gq, gk, gmat, cos_t, sin_t)


def _ctx_body(c_ref, sc_ref, sh_ref, n1_ref, wkv_ref, gk_ref, gmat_ref, kc_ref, vc_ref):
    for i in range(CTX_STEP):
        hc = _rms_mod(c_ref[i], n1_ref[...], sc_ref[0], sh_ref[0]).astype(BF16)
        z = _dot(hc, wkv_ref[...])
        kc_ref[i] = _head_norm_rope(z[:, :KV_W], gk_ref[...], gmat_ref[0:KV_W, 0:KV_W], None, None, 1.0).astype(BF16)
        vc_ref[i] = z[:, KV_W:].astype(BF16)


def _ctx_proj(ctx, mods3, norm1, w_packed, gk, gmat):
    d = D_MODEL
    const = lambda shape: pl.BlockSpec(shape, lambda b: (0,) * len(shape))
    return pl.pallas_call(
        _ctx_body,
        grid=(BATCH // CTX_STEP,),
        in_specs=[pl.BlockSpec((CTX_STEP, CTX_LEN, d), lambda b: (b, 0, 0)),
                  pl.BlockSpec((1, 1, d), lambda b: (BATCH, 0, 1)),
                  pl.BlockSpec((1, 1, d), lambda b: (BATCH, 0, 0)),
                  const((1, d)), pl.BlockSpec((d, 2 * KV_W), lambda b: (0, W_OFF_KV // (2 * KV_W))),
                  const((1, KV_W)), const((2 * LANES, 2 * LANES))],
        out_specs=[pl.BlockSpec((CTX_STEP, CTX_LEN, KV_W), lambda b: (b, 0, 0)),
                   pl.BlockSpec((CTX_STEP, CTX_LEN, KV_W), lambda b: (b, 0, 0))],
        out_shape=[jax.ShapeDtypeStruct((BATCH, CTX_LEN, KV_W), BF16),
                   jax.ShapeDtypeStruct((BATCH, CTX_LEN, KV_W), BF16)],
        compiler_params=_params(("parallel",)),
        name="ctx_proj",
    )(ctx, mods3, mods3, norm1, w_packed, gk, gmat)


def _hyena_body(zv_ref, z1_ref, z2_ref, cwv_ref, cw1_ref, cw2_ref, cbv_ref, cb1_ref, cb2_ref, hb_ref, kf_ref,
                kn_ref, t1_ref, t2_ref, t1t_ref, t2t_ref, o_ref, lo_ref, hi_ref, g0lo_ref, g0hi_ref, g1lo_ref, g1hi_ref,
                p_ref, m_ref, za_ref, zb_ref):
    row = lax.broadcasted_iota(I32, (HALF, CT), 0)
    rr = lax.broadcasted_iota(I32, (RB, RB), 0)
    cc = lax.broadcasted_iota(I32, (RB, RB), 1)
    flip = jnp.where(rr + cc == RB - 1, 1.0, 0.0).astype(BF16)
    nrb = HALF // RB

    def reverse_upper_half(z_ref, hi_out):
        for j in range(nrb):
            hi_out[j * RB:(j + 1) * RB, :] = _dot(flip, z_ref[0, SEQ - RB * (j + 1):SEQ - RB * j, :])

    def folded_short_conv(z_ref, w_ref, b_ref, lo_out, hi_out):
        zlo = z_ref[0, 0:HALF, :].astype(F32)
        zhi = hi_out[...]
        w0, w1, w2 = w_ref[0:1, :], w_ref[1:2, :], w_ref[2:3, :]
        first, last = row == 0, row == HALF - 1
        lo_prev = jnp.where(first, 0.0, pltpu.roll(zlo, 1, axis=0))
        lo_next = jnp.where(last, zhi[HALF - 1:HALF, :], pltpu.roll(zlo, HALF - 1, axis=0))
        hi_prev = jnp.where(first, 0.0, pltpu.roll(zhi, 1, axis=0))
        hi_next = jnp.where(last, zlo[HALF - 1:HALF, :], pltpu.roll(zhi, HALF - 1, axis=0))
        lo_out[...] = lo_prev * w0 + zlo * w1 + lo_next * w2 + b_ref[...]
        hi_out[...] = hi_next * w0 + zhi * w1 + hi_prev * w2 + b_ref[...]

    gates = ((g0lo_ref, g0hi_ref), (g1lo_ref, g1hi_ref))
    reverse_upper_half(zv_ref, hi_ref)
    reverse_upper_half(z1_ref, g0hi_ref)
    reverse_upper_half(z2_ref, g1hi_ref)
    folded_short_conv(zv_ref, cwv_ref, cbv_ref, lo_ref, hi_ref)
    folded_short_conv(z1_ref, cw1_ref, cb1_ref, g0lo_ref, g0hi_ref)
    folded_short_conv(z2_ref, cw2_ref, cb2_ref, g1lo_ref, g1hi_ref)
    sign8 = _sign_rows(HALF)
    odd = (lax.broadcasted_iota(I32, (FC, CT), 0) & 1) == 1
    for o, (glo_ref, ghi_ref) in enumerate(gates):
        p_ref[...] = (lo_ref[...] + hi_ref[...]).astype(BF16)
        m_ref[...] = (lo_ref[...] - hi_ref[...]).astype(BF16)
        pv = p_ref[...]
        mv = m_ref[...]
        for c in range(HALF // FC):
            ev = slice(c * FC, (c + 1) * FC)
            od = slice(HALF + c * FC, HALF + (c + 1) * FC)
            xce = _dot(t1_ref[ev, :], pv)
            xso = _dot(t1_ref[od, :], pv)
            xco = _dot(t2_ref[ev, :], mv)
            xse = _dot(t2_ref[od, :], mv)
            kce, kse, kco, kso = kf_ref[o, 0, ev, :], kf_ref[o, 1, ev, :], kf_ref[o, 2, ev, :], kf_ref[o, 3, ev, :]
            za_ref[ev, :] = (xce * kce - xse * kse).astype(BF16)
            za_ref[od, :] = (xco * kso + xso * kco).astype(BF16)
            zb_ref[ev, :] = (xco * kco - xso * kso).astype(BF16)
            zb_ref[od, :] = (xce * kse + xse * kce).astype(BF16)
        zn = _dot(sign8, mv)[0:1] * kn_ref[o]
        bias = hb_ref[o:o + 1, :]
        za = za_ref[...]
        zb = zb_ref[...]
        for c in range(HALF // FC):
            rs = slice(c * FC, (c + 1) * FC)
            half_p = _dot(t1t_ref[rs, :], za)
            half_m = _dot(t2t_ref[rs, :], zb) + jnp.where(odd, -zn, zn)
            lo_ref[rs, :] = glo_ref[rs, :] * (half_p + half_m + bias * lo_ref[rs, :])
            hi_ref[rs, :] = ghi_ref[rs, :] * (half_p - half_m + bias * hi_ref[rs, :])
    o_ref[0, 0:HALF, :] = lo_ref[...].astype(BF16)
    for j in range(nrb):
        o_ref[0, SEQ - RB * (j + 1):SEQ - RB * j, :] = _dot(
            flip, hi_ref[j * RB:(j + 1) * RB, :].astype(BF16)).astype(BF16)


def _hyena(zh, conv_w, conv_b, hbias, kf, kn, t1, t2, t1t, t2t):
    nct = HYENA_W // CT
    zspec = lambda k: pl.BlockSpec((1, SEQ, CT), lambda c, b: (b, 0, k * nct + c))
    wspec = lambda k: pl.BlockSpec((3, CT), lambda c, b: (0, k * nct + c))
    bspec = lambda k: pl.BlockSpec((1, CT), lambda c, b: (0, k * nct + c))
    table = lambda shape: pl.BlockSpec(shape, lambda c, b: (0, 0), pipeline_mode=pl.Buffered(1))
    half_f32 = pltpu.VMEM((HALF, CT), F32)
    return pl.pallas_call(
        _hyena_body,
        grid=(nct, BATCH),
        in_specs=[zspec(0), zspec(1), zspec(2), wspec(0), wspec(1), wspec(2), bspec(0), bspec(1), bspec(2),
                  pl.BlockSpec((HYENA_ORDER, CT), lambda c, b: (0, c)),
                  pl.BlockSpec((HYENA_ORDER, 4, HALF, CT), lambda c, b: (0, 0, 0, c)),
                  pl.BlockSpec((HYENA_ORDER, 1, CT), lambda c, b: (0, 0, c)),
                  table((SEQ, HALF)), table((SEQ, HALF)), table((HALF, SEQ)), table((HALF, SEQ))],
        out_specs=pl.BlockSpec((1, SEQ, CT), lambda c, b: (b, 0, c)),
        out_shape=jax.ShapeDtypeStruct((BATCH, SEQ, HYENA_W), BF16),
        scratch_shapes=[half_f32] * 6 + [
            pltpu.VMEM((HALF, CT), BF16), pltpu.VMEM((HALF, CT), BF16),
            pltpu.VMEM((SEQ, CT), BF16), pltpu.VMEM((SEQ, CT), BF16)],
        compiler_params=_params(("parallel", "parallel")),
        name="hyena_conv",
    )(zh, zh, zh, conv_w, conv_w, conv_w, conv_b, conv_b, conv_b, hbias, kf, kn, t1, t2, t1t, t2t)


def _attn_body(sink_ref, q_ref, k_ref, v_ref, kc_ref, vc_ref, bias_ref, o_ref, kp_ref, vlo_ref, vhi_ref):
    nb = SEQ // BLOCK
    lane = lax.broadcasted_iota(I32, (BLOCK, LANES), 1)
    low = lane < HEAD_DIM
    mask_lo = jnp.where(low, 1.0, 0.0).astype(BF16)
    mask_hi = jnp.where(low, 0.0, 1.0).astype(BF16)

    def with_ones(v):
        lane_v = lax.broadcasted_iota(I32, v.shape, 1) < HEAD_DIM
        one = jnp.ones_like(v)
        return jnp.where(lane_v, v, one), jnp.where(lane_v, one, v)

    zpad = jnp.zeros((BLOCK, KV_W), BF16)
    kp_ref[0:BLOCK] = zpad
    kp_ref[BLOCK:BLOCK + SEQ] = k_ref[0]
    kp_ref[BLOCK + SEQ:] = zpad
    v_lo, v_hi = with_ones(v_ref[0])
    for ref, val in ((vlo_ref, v_lo), (vhi_ref, v_hi)):
        ref[0:BLOCK] = zpad
        ref[BLOCK:BLOCK + SEQ] = val
        ref[BLOCK + SEQ:] = zpad
    kc = kc_ref[0]
    vc_pair = with_ones(vc_ref[0])

    def scores_of(n):
        r = pl.multiple_of(n * BLOCK, BLOCK)
        kw = kp_ref[pl.ds(r, 3 * BLOCK), :]
        bias = bias_ref[jnp.where(n == 0, 0, jnp.where(n == nb - 1, 2, 1))]
        scores = []
        for j in range(GROUP):
            qs = q_ref[0, pl.ds(r, BLOCK), j * LANES:(j + 1) * LANES]
            for msk in (mask_lo, mask_hi):
                qm = qs * msk
                scores.append((_dot_nt(qm, kw) + bias, _dot_nt(qm, kc)))
        return scores

    def finish(n, scores):
        r = pl.multiple_of(n * BLOCK, BLOCK)
        vw_pair = (vlo_ref[pl.ds(r, 3 * BLOCK), :], vhi_ref[pl.ds(r, 3 * BLOCK), :])
        probs = []
        for idx, (sw, sc) in enumerate(scores):
            snk = sink_ref[idx // 2 + GROUP * (idx % 2)]
            m = jnp.maximum(jnp.maximum(jnp.max(sw, axis=-1, keepdims=True),
                                        jnp.max(sc, axis=-1, keepdims=True)), snk)
            probs.append((jnp.exp2(sw - m).astype(BF16), jnp.exp2(sc - m).astype(BF16), jnp.exp2(snk - m)))
        outs = []
        for idx, (pw, pc, psink) in enumerate(probs):
            acc = _dot(pw, vw_pair[idx % 2]) + _dot(pc, vc_pair[idx % 2])
            den = pltpu.roll(acc, HEAD_DIM, axis=1) + psink
            outs.append(acc / den)
        for j in range(GROUP):
            o_ref[0, pl.ds(r, BLOCK), j * LANES:(j + 1) * LANES] = jnp.where(
                low, outs[2 * j], outs[2 * j + 1]).astype(BF16)

    def block_group(g, carry):
        first = g * ATTN_UNROLL
        pending = scores_of(first)
        for i in range(ATTN_UNROLL):
            nxt = scores_of(first + i + 1) if i + 1 < ATTN_UNROLL else None
            finish(first + i, pending)
            pending = nxt
        return carry

    lax.fori_loop(0, nb // ATTN_UNROLL, block_group, 0)


def _attention(sink, q, k, v, kc, vc, bias):
    per_b = lambda n, w: pl.BlockSpec((1, n, w), lambda b: (b, 0, 0))
    return pl.pallas_call(
        _attn_body,
        grid=(BATCH,),
        in_specs=[pl.BlockSpec(memory_space=pltpu.SMEM),
                  per_b(SEQ, ATTN_W), per_b(SEQ, KV_W), per_b(SEQ, KV_W), per_b(CTX_LEN, KV_W), per_b(CTX_LEN, KV_W),
                  pl.BlockSpec((3, BLOCK, 3 * BLOCK), lambda b: (0, 0, 0))],
        out_specs=per_b(SEQ, ATTN_W),
        out_shape=jax.ShapeDtypeStruct((BATCH, SEQ, ATTN_W), BF16),
        scratch_shapes=[pltpu.VMEM((SEQ + 2 * BLOCK, KV_W), BF16)] * 3,
        compiler_params=_params(("parallel",)),
        name="window_attn",
    )(sink, q, k, v, kc, vc, bias)


def _merge_body(ya_ref, yb_ref, gt_ref, x_ref, g1_ref, sc2_ref, sh2_ref, n2_ref, wa_ref, wb_ref, wo_ref, rt_ref,
                xn_ref, h2_ref, aff_ref):
    d = D_MODEL
    groups = [slice(i * SUB_MERGE, (i + 1) * SUB_MERGE) for i in range(TM_MERGE // SUB_MERGE)]
    branch = [(_dot(ya_ref[0, s, :], wa_ref[...]), _dot(yb_ref[0, s, :], wb_ref[...])) for s in groups]
    mixed = [(gt_ref[0, s, :d].astype(F32) * ua + gt_ref[0, s, d:].astype(F32) * ub).astype(BF16)
             for s, (ua, ub) in zip(groups, branch)]
    proj = [_dot(u, wo_ref[...]) for u in mixed]
    h2_rows = []
    for s, m in zip(groups, proj):
        xn = x_ref[0, s, :] + g1_ref[0] * m
        xn_ref[0, s, :] = xn
        h2 = _rms_mod(xn, n2_ref[...], sc2_ref[0], sh2_ref[0]).astype(BF16)
        h2_ref[0, s, :] = h2
        h2_rows.append(h2)
    logits = [_dot_nt(rt_ref[...], h2) for h2 in h2_rows]
    for s, lt in zip(groups, logits):
        e = jnp.exp(lt - jnp.max(lt, axis=0, keepdims=True))
        aff_ref[0, :, s] = e / jnp.sum(e, axis=0, keepdims=True)


def _merge(ya, yb, gates, x, mods3, norm2, wa, wb, wo, router_t):
    d = D_MODEL
    nt = SEQ // TM_MERGE
    const = lambda shape: pl.BlockSpec(shape, lambda b, i: (0,) * len(shape))
    tok = lambda w: pl.BlockSpec((1, TM_MERGE, w), lambda b, i: (b, i, 0))
    mod = lambda k: pl.BlockSpec((1, 1, d), lambda b, i: (b, 0, k))
    return pl.pallas_call(
        _merge_body,
        grid=(BATCH, nt),
        in_specs=[tok(HYENA_W), tok(ATTN_W), tok(2 * d), tok(d), mod(2), mod(4), mod(3), const((1, d)),
                  const((HYENA_W, d)), const((ATTN_W, d)), const((d, d)), const((N_EXPERTS, d))],
        out_specs=[tok(d), tok(d), pl.BlockSpec((1, N_EXPERTS, TM_MERGE), lambda b, i: (b, 0, i))],
        out_shape=[jax.ShapeDtypeStruct((BATCH, SEQ, d), F32),
                   jax.ShapeDtypeStruct((BATCH, SEQ, d), BF16),
                   jax.ShapeDtypeStruct((BATCH, N_EXPERTS, SEQ), F32)],
        compiler_params=_params(("parallel", "parallel")),
        name="merge_outproj",
    )(ya, yb, gates, x, mods3, mods3, mods3, norm2, wa, wb, wo, router_t)


def _thresh_body(aff_ref, lo_ref, hi_ref):
    aff = aff_ref[...]
    rows = aff.shape[0]
    bits = pltpu.bitcast(aff, I32)

    def bisect_bits(_, carry):
        lo, hi = carry
        mid = lo + ((hi - lo) >> 1)
        ge = jnp.sum((bits >= mid).astype(I32), axis=1, keepdims=True) >= CAP
        return jnp.where(ge, mid, lo), jnp.where(ge, hi, mid)

    lo0 = jnp.zeros((rows, 1), I32)
    hi0 = jnp.full((rows, 1), 0x3F800001, I32)
    thr_bits, _ = lax.fori_loop(0, 31, bisect_bits, (lo0, hi0))
    thr = pltpu.bitcast(thr_bits, F32)

    def bisect_val(_, carry):
        lo, hi = carry
        mid = 0.5 * (lo + hi)
        ge = jnp.sum(jnp.where(aff >= mid, 1.0, 0.0), axis=1, keepdims=True) >= CAP
        return jnp.where(ge, mid, lo), jnp.where(ge, hi, mid)

    lo, hi = lax.fori_loop(0, 30, bisect_val, (0.5 * thr, jnp.maximum(2.0 * thr, 1e-30)))
    lo_ref[...] = jnp.broadcast_to(lo, lo_ref.shape)
    hi_ref[...] = jnp.broadcast_to(hi, hi_ref.shape)


def _thresholds(aff_rows):
    rows = BATCH * N_EXPERTS
    return pl.pallas_call(
        _thresh_body,
        grid=(1,),
        in_specs=[pl.BlockSpec((rows, SEQ), lambda i: (0, 0))],
        out_specs=[pl.BlockSpec((rows, LANES), lambda i: (0, 0)), pl.BlockSpec((rows, LANES), lambda i: (0, 0))],
        out_shape=[jax.ShapeDtypeStruct((rows, LANES), F32), jax.ShapeDtypeStruct((rows, LANES), F32)],
        compiler_params=_params(("arbitrary",)),
        name="route_threshold",
    )(aff_rows)


def _prefix_counts(mask):
    r = lax.broadcasted_iota(I32, (LANES, LANES), 0)
    c = lax.broadcasted_iota(I32, (LANES, LANES), 1)
    upper = jnp.where(r <= c, 1.0, 0.0).astype(BF16)
    offset = jnp.zeros((mask.shape[0], 1), F32)
    blocks = []
    for j in range(mask.shape[1] // LANES):
        blk = mask[:, j * LANES:(j + 1) * LANES]
        inc = _dot(blk.astype(BF16), upper)
        blocks.append(inc - blk + offset)
        offset = offset + inc[:, LANES - 1:LANES]
    return jnp.concatenate(blocks, axis=1)


def _route_body(aff_ref, lo_ref, hi_ref, h_ref, slot_ref, xin_ref, pos_ref, pbuf_ref):
    aff = aff_ref[0]
    above = jnp.where(aff >= hi_ref[:, 0:1], 1.0, 0.0)
    band = jnp.where(aff >= lo_ref[:, 0:1], 1.0, 0.0) - above
    need = CAP - jnp.sum(above, axis=1, keepdims=True)
    tie_rank = _prefix_counts(band)
    self = above + band * jnp.where(tie_rank < need, 1.0, 0.0)
    pos = _prefix_counts(self)
    posi = jnp.where(self > 0.5, pos.astype(I32), -1)
    pos_ref[0] = posi

    h = h_ref[0]
    slot = slot_ref[...]
    posb = jnp.where(self > 0.5, pos, -1.0).astype(BF16)
    one = jnp.ones((CAP, SEQ), BF16)
    zero = jnp.zeros((CAP, SEQ), BF16)
    for grp in range(N_EXPERTS // EXPERT_GROUP):
        for i in range(EXPERT_GROUP):
            e = grp * EXPERT_GROUP + i
            pbuf_ref[i * CAP:(i + 1) * CAP, :] = jnp.where(posb[e:e + 1, :] == slot, one, zero)
        rows = _dot(pbuf_ref[...], h)
        for i in range(EXPERT_GROUP):
            xin_ref[grp * EXPERT_GROUP + i] = rows[i * CAP:(i + 1) * CAP].astype(BF16)


def _route(aff_t, lo, hi, h2, slot_rows):
    d = D_MODEL
    return pl.pallas_call(
        _route_body,
        grid=(BATCH,),
        in_specs=[pl.BlockSpec((1, N_EXPERTS, SEQ), lambda b: (b, 0, 0)),
                  pl.BlockSpec((N_EXPERTS, LANES), lambda b: (b, 0)),
                  pl.BlockSpec((N_EXPERTS, LANES), lambda b: (b, 0)),
                  pl.BlockSpec((1, SEQ, d), lambda b: (b, 0, 0)),
                  pl.BlockSpec((CAP, SEQ), lambda b: (0, 0))],
        out_specs=[pl.BlockSpec((N_EXPERTS, CAP, d), lambda b: (0, b, 0)),
                   pl.BlockSpec((1, N_EXPERTS, SEQ), lambda b: (b, 0, 0))],
        out_shape=[jax.ShapeDtypeStruct((N_EXPERTS, BATCH * CAP, d), BF16),
                   jax.ShapeDtypeStruct((BATCH, N_EXPERTS, SEQ), I32)],
        scratch_shapes=[pltpu.VMEM((EXPERT_GROUP * CAP, SEQ), BF16)],
        compiler_params=_params(("parallel",)),
        name="route_gather",
    )(aff_t, lo, hi, h2, slot_rows)


def _expert_body(x_ref, wg_ref, wu_ref, wd_ref, o_ref, acc_ref, wgb_ref, wub_ref, wdb_ref):
    f = pl.program_id(1)

    @pl.when((pl.program_id(0) == 0) & (f == 0))
    def _():
        acc_ref[...] = jnp.zeros_like(acc_ref)

    carry = f > 0
    for m in range(BATCH * CAP // MC):
        rows = slice(m * MC, (m + 1) * MC)
        xm = x_ref[0, rows, :]
        if m == 0:
            wgb_ref[...] = wg_ref[0].astype(BF16)
        a = _dot(xm, wgb_ref[...])
        if m == 0:
            wub_ref[...] = wu_ref[0].astype(BF16)
        b = _dot(xm, wub_ref[...])
        hh = (a * jax.nn.sigmoid(a) * b).astype(BF16)
        if m == 0:
            wdb_ref[...] = wd_ref[0].astype(BF16)
        total = jnp.where(carry, acc_ref[rows, :], 0.0) + _dot(hh, wdb_ref[...])
        acc_ref[rows, :] = total
        o_ref[0, rows, :] = total.astype(BF16)


def _experts(xin, w_gate, w_up, w_down):
    d = D_MODEL
    rows = BATCH * CAP
    return pl.pallas_call(
        _expert_body,
        grid=(N_EXPERTS, D_FF // TF),
        in_specs=[pl.BlockSpec((1, rows, d), lambda e, f: (e, 0, 0)),
                  pl.BlockSpec((1, d, TF), lambda e, f: (e, 0, f)),
                  pl.BlockSpec((1, d, TF), lambda e, f: (e, 0, f)),
                  pl.BlockSpec((1, TF, d), lambda e, f: (e, f, 0))],
        out_specs=pl.BlockSpec((1, rows, d), lambda e, f: (e, 0, 0)),
        out_shape=jax.ShapeDtypeStruct((N_EXPERTS, rows, d), BF16),
        scratch_shapes=[pltpu.VMEM((rows, d), F32), pltpu.VMEM((d, TF), BF16), pltpu.VMEM((d, TF), BF16),
                        pltpu.VMEM((TF, d), BF16)],
        compiler_params=_params(("parallel", "arbitrary")),
        name="swiglu_experts",
    )(xin, w_gate, w_up, w_down)


def _scatter_body(pos_ref, aff_ref, y_ref, x_ref, g2_ref, o_ref):
    pos = pos_ref[0].T
    aff = aff_ref[0].T
    slot = lax.broadcasted_iota(I32, (SUB_S, CAP), 1)
    groups = [slice(i * SUB_S, (i + 1) * SUB_S) for i in range(TS // SUB_S)]
    onehots = [jnp.concatenate([jnp.where(pos[s, e:e + 1] == slot, aff[s, e:e + 1], 0.0).astype(BF16)
                                for e in range(N_EXPERTS)], axis=1) for s in groups]
    y = y_ref[...].reshape(N_EXPERTS * CAP, D_MODEL)
    mixed = [_dot(oh, y) for oh in onehots]
    for s, m in zip(groups, mixed):
        o_ref[0, s, :] = x_ref[0, s, :] + g2_ref[0] * m


def _scatter(pos, aff_t, y, xn, mods3):
    d = D_MODEL
    return pl.pallas_call(
        _scatter_body,
        grid=(BATCH, SEQ // TS),
        in_specs=[pl.BlockSpec((1, N_EXPERTS, TS), lambda b, i: (b, 0, i)),
                  pl.BlockSpec((1, N_EXPERTS, TS), lambda b, i: (b, 0, i)),
                  pl.BlockSpec((N_EXPERTS, CAP, d), lambda b, i: (0, b, 0)),
                  pl.BlockSpec((1, TS, d), lambda b, i: (b, i, 0)),
                  pl.BlockSpec((1, 1, d), lambda b, i: (b, 0, 5))],
        out_specs=pl.BlockSpec((1, TS, d), lambda b, i: (b, i, 0)),
        out_shape=jax.ShapeDtypeStruct((BATCH, SEQ, d), F32),
        compiler_params=_params(("parallel", "parallel")),
        name="scatter_residual",
    )(pos, aff_t, y, xn, mods3)


def _rope_tables():
    rows = SEQ // GRID_W
    row = np.repeat(np.arange(rows, dtype=np.float32), GRID_W)
    col = np.tile(np.arange(GRID_W, dtype=np.float32), rows)
    inv = (ROPE_BASE ** (-np.arange(0, AXIS_ROT, 2, dtype=np.float32) / AXIS_ROT)).astype(np.float32)
    ang = np.concatenate([row[:, None] * inv, col[:, None] * inv], axis=-1).astype(np.float64)
    cos = np.repeat(np.cos(ang), 2, axis=-1)
    sin = np.stack([-np.sin(ang), np.sin(ang)], axis=-1).reshape(SEQ, HEAD_DIM)
    reps = LANES // HEAD_DIM
    return jnp.asarray(np.tile(cos, (1, reps)), F32), jnp.asarray(np.tile(sin, (1, reps)), F32)


def _dft_tables():
    idx = np.arange(HALF, dtype=np.int64)
    t2p1 = 2 * idx + 1

    def cos_sin(f):
        ang = ((f[:, None] * t2p1[None, :]) % (2 * N_FFT)) * (math.pi / N_FFT)
        return np.cos(ang), np.sin(ang)

    ce, se = cos_sin(2 * idx)
    co, so = cos_sin(2 * idx + 1)
    t1 = jnp.asarray(np.concatenate([ce, so], axis=0), F32).astype(BF16)
    t2 = jnp.asarray(np.concatenate([co, se], axis=0), F32).astype(BF16)
    return t1, t2, t1.T, t2.T


def _phase_tables():
    idx = np.arange(HALF, dtype=np.float64)
    w = np.full((HALF,), 2.0 / N_FFT)
    we = w.copy()
    we[0] = 1.0 / N_FFT
    pe = (math.pi / N_FFT) * (2.0 * idx)
    po = (math.pi / N_FFT) * (2.0 * idx + 1.0)
    rot = np.stack([we * np.cos(pe), we * np.sin(pe), w * np.cos(po), w * np.sin(po)])
    return jnp.asarray(np.broadcast_to(rot[:, :, None], (4, HALF, CT)), F32)


def _fold_rows(a):
    return np.concatenate([a[:HALF], a[HALF:][::-1]], axis=0)


def _filter_features():
    t = np.linspace(0.0, 1.0, SEQ, dtype=np.float32).astype(np.float64)[:, None]
    w = 2.0 * math.pi * np.arange(SEQ, dtype=np.float64)[:, None] / SEQ
    fr = np.linspace(1e-4, FILTER_BANDS - 1, FILTER_BANDS, dtype=np.float32).astype(np.float64)[None, :]
    feat = np.concatenate([t, np.cos(fr * w), -np.sin(fr * w)], axis=-1)
    feat = np.pad(feat, ((0, 0), (0, FILTER_HIDDEN - FILTER_EMB)))
    min_decay = math.log(DECAY_TARGET) / SLOW_DECAY_PCT
    max_decay = math.log(DECAY_TARGET) / FAST_DECAY_PCT
    deltas = np.linspace(min_decay, max_decay, HYENA_W, dtype=np.float32).astype(np.float64)
    decay = np.exp(-t * np.abs(deltas))
    return jnp.asarray(_fold_rows(feat).T, F32), jnp.asarray(_fold_rows(decay), F32)


def _attn_bias():
    qi = np.arange(BLOCK)[:, None]
    kj = np.arange(3 * BLOCK)[None, :]
    band = np.abs(kj - BLOCK - qi) <= WINDOW
    first = band & (kj >= BLOCK)
    last = band & (kj < 2 * BLOCK)
    return jnp.asarray(np.where(np.stack([first, band, last]), 0.0, NEG), F32)


def _pair_heads(w, axis):
    heads = [lax.slice_in_dim(w, h * HEAD_DIM, (h + 1) * HEAD_DIM, axis=axis) for h in range(N_HEADS)]
    return jnp.concatenate([heads[j + GROUP * half] for j in range(GROUP) for half in range(N_KV_HEADS)], axis=axis)


def kernel(x, c, ctx, c_ctx, ada_w, ada_b, norm1, norm2, w_in, conv_w, conv_b, filt_w1, filt_b1, filt_w2, filt_b2,
           filt_w3, filt_b3, filt_freq, filt_out, hyena_bias, q_norm, k_norm, attn_sink, w_branch_a, w_branch_b,
           w_out, router, w_gate, w_up, w_down):
    d = D_MODEL
    assert ada_w.shape[0] == 1, "only the single-layer configuration is implemented"
    l = 0
    cos_t, sin_t = _rope_tables()
    t1, t2, t1t, t2t = _dft_tables()
    rot = _phase_tables()
    feat, decay = _filter_features()
    bias = _attn_bias()
    gmat = jnp.asarray(np.kron(np.eye(2 * LANES // HEAD_DIM), np.full((HEAD_DIM, HEAD_DIM), 1.0 / HEAD_DIM)), BF16)
    c16 = jnp.concatenate([c, c_ctx[None, :], jnp.zeros((MOD_ROWS - BATCH - 1, d), F32)], axis=0)

    mods3 = _ada(c16, ada_w[l], ada_b[l][None, :])
    n1 = norm1[l][None, :]
    w_packed = _pack_in_weights(w_in[l])
    gk = jnp.tile(k_norm[l], N_KV_HEADS)[None, :]
    w1_rows = jnp.pad(filt_w1[l], ((0, FILTER_HIDDEN - FILTER_EMB), (0, 0)))
    mlp_w = jnp.swapaxes(jnp.stack([w1_rows, filt_w2[l], filt_w3[l]]), 1, 2)
    mlp_cols = jnp.stack([filt_b1[l], filt_b2[l], filt_b3[l], filt_freq[l]], axis=1)
    kf, kn = _filters(feat, mlp_w, mlp_cols, filt_out[l], decay, rot, t1, t2)
    kc, vc = _ctx_proj(ctx, mods3, n1, w_packed, gk, gmat)
    zh, q, k, v, gates = _inproj(x, mods3, n1, w_packed, jnp.tile(q_norm[l], N_HEADS)[None, :], gk, gmat, cos_t, sin_t)
    ya = _hyena(zh, conv_w[l], conv_b[l][None, :], hyena_bias[l], kf, kn, t1, t2, t1t, t2t)
    yb = _attention(attn_sink[l] * LOG2E, q, k, v, kc, vc, bias)
    xn, h2, aff_t = _merge(ya, yb, gates, x, mods3, norm2[l][None, :], w_branch_a[l].astype(BF16),
                           _pair_heads(w_branch_b[l], 0).astype(BF16), w_out[l].astype(BF16), router[l].T.astype(BF16))
    lo, hi = _thresholds(aff_t.reshape(BATCH * N_EXPERTS, SEQ))
    slot_rows = jnp.asarray(np.broadcast_to(np.arange(CAP)[:, None], (CAP, SEQ)), BF16)
    xin, pos = _route(aff_t, lo, hi, h2, slot_rows)
    y = _experts(xin, w_gate[l], w_up[l], w_down[l])
    return _scatter(pos, aff_t, y, xn, mods3)
```

```python
import math

import numpy as np
import jax
import jax.numpy as jnp
from jax import lax
from jax.experimental import pallas as pl
from jax.experimental.pallas import tpu as pltpu

F32 = jnp.float32
BF16 = jnp.bfloat16
I32 = jnp.int32
HIGHEST = lax.Precision.HIGHEST

D_MODEL = 1024
BATCH = 8
SEQ = 2048
GRID_W = 64
CTX_LEN = 256
N_HEADS = 8
N_KV_HEADS = 2
HEAD_DIM = 64
GROUP = N_HEADS // N_KV_HEADS
ATTN_W = N_HEADS * HEAD_DIM
KV_W = N_KV_HEADS * HEAD_DIM
WINDOW = 128
BLOCK = 128
HYENA_W = D_MODEL // 2
HYENA_ORDER = 2
FILTER_BANDS = 16
FILTER_EMB = 1 + 2 * FILTER_BANDS
FILTER_HIDDEN = 64
DECAY_TARGET = 1e-2
FAST_DECAY_PCT = 0.3
SLOW_DECAY_PCT = 1.5
ROPE_BASE = 10000.0
AXIS_ROT = HEAD_DIM // 2
N_EXPERTS = 16
EC_CAPACITY = 2
D_FF = 2048
EPS = 1e-6
NEG = -1e30
LOG2E = math.log2(math.e)

OFF_Q = 3 * HYENA_W
OFF_K = OFF_Q + ATTN_W
OFF_V = OFF_K + KV_W
OFF_G = OFF_V + KV_W

CAP = EC_CAPACITY * SEQ // N_EXPERTS
N_FFT = 2 * SEQ
HALF = SEQ // 2
MOD_ROWS = 16
LANES = 128

TM_IN = 1024
SUB_IN = 256
TM_MERGE = 1024
SUB_MERGE = 512
CT = 256
FC = 512
RB = 256
TF = 512
MC = 512
TS = 1024
SUB_S = 512
CTX_STEP = 4
EXPERT_GROUP = 4
ATTN_UNROLL = 8
VMEM_LIMIT = 56 * 1024 * 1024


def _dot(a, b, precision=None):
    return jnp.dot(a, b, preferred_element_type=F32, precision=precision)


def _dot_nt(a, b, precision=None):
    return lax.dot_general(a, b, (((1,), (1,)), ((), ())), preferred_element_type=F32, precision=precision)


def _params(sem, vmem=VMEM_LIMIT):
    return pltpu.CompilerParams(dimension_semantics=sem, vmem_limit_bytes=vmem)


def _rms_mod(x, g, sc, sh):
    ms = jnp.mean(x * x, axis=-1, keepdims=True)
    return (x * lax.rsqrt(ms + EPS) * g) * (1.0 + sc) + sh


def _head_norm_rope(z, g, gmat, cos, sin, scale):
    ms = _dot((z * z).astype(BF16), gmat)
    y = z * lax.rsqrt(ms + EPS) * g
    if cos is not None:
        slabs = []
        for s in range(z.shape[1] // LANES):
            ys = y[:, s * LANES:(s + 1) * LANES]
            lane = lax.broadcasted_iota(I32, ys.shape, 1)
            nxt = pltpu.roll(ys, LANES - 1, axis=1)
            prv = pltpu.roll(ys, 1, axis=1)
            slabs.append(ys * cos + jnp.where((lane & 1) == 0, nxt, prv) * sin)
        y = slabs[0] if len(slabs) == 1 else jnp.concatenate(slabs, axis=1)
    return y * scale


def _split_bf16(x):
    hi = x.astype(BF16)
    return hi, (x - hi.astype(F32)).astype(BF16)


def _ada_body(c_ref, w_ref, b_ref, o_ref):
    c = c_ref[...]
    s_hi, s_lo = _split_bf16(c * jax.nn.sigmoid(c))
    w_hi, w_lo = _split_bf16(w_ref[...])
    o_ref[:, 0, :] = _dot(s_hi, w_hi) + _dot(s_lo, w_hi) + _dot(s_hi, w_lo) + b_ref[...]


def _ada(c16, w, b):
    d = D_MODEL
    return pl.pallas_call(
        _ada_body,
        grid=(6,),
        in_specs=[pl.BlockSpec((MOD_ROWS, d), lambda j: (0, 0)),
                  pl.BlockSpec((d, d), lambda j: (0, j)),
                  pl.BlockSpec((1, d), lambda j: (0, j))],
        out_specs=pl.BlockSpec((MOD_ROWS, 1, d), lambda j: (0, 0, j)),
        out_shape=jax.ShapeDtypeStruct((MOD_ROWS, 1, 6 * d), F32),
        compiler_params=_params(("parallel",)),
        name="ada_mod",
    )(c16, w, b)


def _sign_rows(n):
    lane = lax.broadcasted_iota(I32, (8, n), 1)
    sub = lax.broadcasted_iota(I32, (8, n), 0)
    sg = jnp.where((lane & 1) == 0, 1.0, -1.0)
    return jnp.where(sub == 0, sg, 0.0).astype(BF16)


def _filt_body(feat_ref, w_ref, col_ref, fof_ref, fob_ref, dec_ref, rot_ref, t1_ref, t2_ref, kf_ref, kn_ref, hs_ref):
    @pl.when((pl.program_id(0) == 0) & (pl.program_id(1) == 0))
    def _():
        cols = col_ref[...]
        fq = cols[:, 3:4]
        h = feat_ref[...]
        for layer in range(3):
            h = jnp.sin(fq * (_dot(w_ref[layer], h, HIGHEST) + cols[:, layer:layer + 1]))
        hi = h.astype(BF16).astype(F32)
        stacked = jnp.concatenate([hi, h - hi, hi, jnp.zeros_like(hi)], axis=0)
        hs_ref[...] = stacked.T.astype(BF16)

    def taps(fo_ref):
        f_hi, f_lo = _split_bf16(fo_ref[...])
        return _dot(hs_ref[...], jnp.concatenate([f_hi, f_hi, f_lo, jnp.zeros_like(f_hi)], axis=0))

    dec = dec_ref[...]
    hf = taps(fof_ref) * dec
    hb = taps(fob_ref) * dec
    row = lax.broadcasted_iota(I32, hf.shape, 0)
    hb = jnp.where(row == 0, 0.0, hb)
    a = hf + hb
    b = hf - hb
    pa = (a[:HALF] + a[HALF:]).astype(BF16)
    ma = (a[:HALF] - a[HALF:]).astype(BF16)
    pb = (b[:HALF] + b[HALF:]).astype(BF16)
    mb = (b[:HALF] - b[HALF:]).astype(BF16)
    t1 = t1_ref[...]
    t2 = t2_ref[...]
    a1 = _dot(t1, pa)
    a2 = _dot(t2, ma)
    b1 = _dot(t1, pb)
    b2 = _dot(t2, mb)
    ce, se, co, so = rot_ref[0], rot_ref[1], rot_ref[2], rot_ref[3]
    kf_ref[0, 0] = a1[:HALF] * ce + a2[HALF:] * se
    kf_ref[0, 1] = b2[HALF:] * ce - b1[:HALF] * se
    kf_ref[0, 2] = a2[:HALF] * co + a1[HALF:] * so
    kf_ref[0, 3] = b1[HALF:] * co - b2[:HALF] * so
    kn_ref[0] = _dot(_sign_rows(HALF), ma)[0:1] * (1.0 / N_FFT)


def _filters(feat, mlp_w, mlp_cols, fout, decay, rot, t1, t2):
    nct = HYENA_W // CT
    full = lambda shape: pl.BlockSpec(shape, lambda o, c: (0,) * len(shape))
    return pl.pallas_call(
        _filt_body,
        grid=(HYENA_ORDER, nct),
        in_specs=[full((FILTER_HIDDEN, SEQ)), full((3, FILTER_HIDDEN, FILTER_HIDDEN)), full((FILTER_HIDDEN, 4)),
                  pl.BlockSpec((FILTER_HIDDEN, CT), lambda o, c: (0, (o * 2 + 0) * nct + c)),
                  pl.BlockSpec((FILTER_HIDDEN, CT), lambda o, c: (0, (o * 2 + 1) * nct + c)),
                  pl.BlockSpec((SEQ, CT), lambda o, c: (0, c)),
                  full((4, HALF, CT)),
                  pl.BlockSpec((SEQ, HALF), lambda o, c: (0, 0), pipeline_mode=pl.Buffered(1)),
                  pl.BlockSpec((SEQ, HALF), lambda o, c: (0, 0), pipeline_mode=pl.Buffered(1))],
        out_specs=[pl.BlockSpec((1, 4, HALF, CT), lambda o, c: (o, 0, 0, c)),
                   pl.BlockSpec((1, 1, CT), lambda o, c: (o, 0, c))],
        out_shape=[jax.ShapeDtypeStruct((HYENA_ORDER, 4, HALF, HYENA_W), F32),
                   jax.ShapeDtypeStruct((HYENA_ORDER, 1, HYENA_W), F32)],
        scratch_shapes=[pltpu.VMEM((SEQ, 4 * FILTER_HIDDEN), BF16)],
        compiler_params=_params(("arbitrary", "arbitrary")),
        name="hyena_filters",
    )(feat, mlp_w, mlp_cols, fout, fout, decay, rot, t1, t2)


def _pair_head_lanes(z):
    slabs = [z[:, s * LANES:(s + 1) * LANES] for s in range(ATTN_W // LANES)]
    swapped = [pltpu.roll(sl, HEAD_DIM, axis=1) for sl in slabs]
    low = lax.broadcasted_iota(I32, slabs[0].shape, 1) < HEAD_DIM
    out = []
    for j in range(GROUP):
        first, second = j, j + GROUP
        lo_src = slabs[first // 2] if first % 2 == 0 else swapped[first // 2]
        hi_src = slabs[second // 2] if second % 2 == 1 else swapped[second // 2]
        out.append(jnp.where(low, lo_src, hi_src))
    return jnp.concatenate(out, axis=1)


def _inproj_body(x_ref, sc_ref, sh_ref, n1_ref, wh_ref, wq_ref, wkv_ref, wg_ref, gq_ref, gk_ref, gmat_ref,
                 cos_ref, sin_ref, zh_ref, q_ref, k_ref, v_ref, gate_ref):
    groups = [slice(i * SUB_IN, (i + 1) * SUB_IN) for i in range(TM_IN // SUB_IN)]
    hx = [_rms_mod(x_ref[0, s, :], n1_ref[...], sc_ref[0], sh_ref[0]).astype(BF16) for s in groups]
    for s, h in zip(groups, hx):
        zh_ref[0, s, :] = _dot(h, wh_ref[...]).astype(BF16)
    pair = 2 * LANES
    zq = [_pair_head_lanes(_dot(h, wq_ref[...])) for h in hx]
    for s, z in zip(groups, zq):
        for c in range(ATTN_W // pair):
            sl = slice(c * pair, (c + 1) * pair)
            q_ref[0, s, sl] = _head_norm_rope(z[:, sl], gq_ref[:, sl], gmat_ref[...], cos_ref[s, :], sin_ref[s, :],
                                              LOG2E * HEAD_DIM ** -0.5).astype(BF16)
    zkv = [_dot(h, wkv_ref[...]) for h in hx]
    for s, z in zip(groups, zkv):
        k_ref[0, s, :] = _head_norm_rope(z[:, :KV_W], gk_ref[...], gmat_ref[0:KV_W, 0:KV_W], cos_ref[s, :],
                                         sin_ref[s, :], 1.0).astype(BF16)
        v_ref[0, s, :] = z[:, KV_W:].astype(BF16)
    for s, h in zip(groups, hx):
        gate_ref[0, s, :] = jax.nn.sigmoid(_dot(h, wg_ref[...])).astype(BF16)


def _pack_in_weights(w):
    wb = w.astype(BF16)
    return jnp.concatenate([wb[:, :OFF_K], wb[:, OFF_G:], wb[:, OFF_K:OFF_G]], axis=1)


W_OFF_H = 0
W_OFF_Q = OFF_Q
W_OFF_G = OFF_Q + ATTN_W
W_OFF_KV = W_OFF_G + 2 * D_MODEL


def _inproj(x, mods3, norm1, w_packed, gq, gk, gmat, cos_t, sin_t):
    d = D_MODEL
    nt = SEQ // TM_IN
    const = lambda shape: pl.BlockSpec(shape, lambda b, i: (0,) * len(shape))

    def wcol(width, off):
        assert off % width == 0
        return pl.BlockSpec((d, width), lambda b, i: (0, off // width))

    tok = lambda w: pl.BlockSpec((1, TM_IN, w), lambda b, i: (b, i, 0))
    return pl.pallas_call(
        _inproj_body,
        grid=(BATCH, nt),
        in_specs=[tok(d),
                  pl.BlockSpec((1, 1, d), lambda b, i: (b, 0, 1)),
                  pl.BlockSpec((1, 1, d), lambda b, i: (b, 0, 0)),
                  const((1, d)), wcol(OFF_Q, W_OFF_H), wcol(ATTN_W, W_OFF_Q), wcol(2 * KV_W, W_OFF_KV),
                  wcol(2 * d, W_OFF_G), const((1, ATTN_W)), const((1, KV_W)), const((2 * LANES, 2 * LANES)),
                  pl.BlockSpec((TM_IN, LANES), lambda b, i: (i, 0)),
                  pl.BlockSpec((TM_IN, LANES), lambda b, i: (i, 0))],
        out_specs=[tok(OFF_Q), tok(ATTN_W), tok(KV_W), tok(KV_W), tok(2 * d)],
        out_shape=[jax.ShapeDtypeStruct((BATCH, SEQ, OFF_Q), BF16),
                   jax.ShapeDtypeStruct((BATCH, SEQ, ATTN_W), BF16),
                   jax.ShapeDtypeStruct((BATCH, SEQ, KV_W), BF16),
                   jax.ShapeDtypeStruct((BATCH, SEQ, KV_W), BF16),
                   jax.ShapeDtypeStruct((BATCH, SEQ, 2 * d), BF16)],
        compiler_params=_params(("parallel", "parallel")),
        name="in_proj",
    )(x, mods3, mods3, norm1, w_packed, w_packed, w_packed, w_packed, gq, gk, gmat, cos_t, sin_t)


def _ctx_body(c_ref, sc_ref, sh_ref, n1_ref, wkv_ref, gk_ref, gmat_ref, kc_ref, vc_ref):
    for i in range(CTX_STEP):
        hc = _rms_mod(c_ref[i], n1_ref[...], sc_ref[0], sh_ref[0]).astype(BF16)
        z = _dot(hc, wkv_ref[...])
        kc_ref[i] = _head_norm_rope(z[:, :KV_W], gk_ref[...], gmat_ref[0:KV_W, 0:KV_W], None, None, 1.0).astype(BF16)
        vc_ref[i] = z[:, KV_W:].astype(BF16)


def _ctx_proj(ctx, mods3, norm1, w_packed, gk, gmat):
    d = D_MODEL
    const = lambda shape: pl.BlockSpec(shape, lambda b: (0,) * len(shape))
    return pl.pallas_call(
        _ctx_body,
        grid=(BATCH // CTX_STEP,),
        in_specs=[pl.BlockSpec((CTX_STEP, CTX_LEN, d), lambda b: (b, 0, 0)),
                  pl.BlockSpec((1, 1, d), lambda b: (BATCH, 0, 1)),
                  pl.BlockSpec((1, 1, d), lambda b: (BATCH, 0, 0)),
                  const((1, d)), pl.BlockSpec((d, 2 * KV_W), lambda b: (0, W_OFF_KV // (2 * KV_W))),
                  const((1, KV_W)), const((2 * LANES, 2 * LANES))],
        out_specs=[pl.BlockSpec((CTX_STEP, CTX_LEN, KV_W), lambda b: (b, 0, 0)),
                   pl.BlockSpec((CTX_STEP, CTX_LEN, KV_W), lambda b: (b, 0, 0))],
        out_shape=[jax.ShapeDtypeStruct((BATCH, CTX_LEN, KV_W), BF16),
                   jax.ShapeDtypeStruct((BATCH, CTX_LEN, KV_W), BF16)],
        compiler_params=_params(("parallel",)),
        name="ctx_proj",
    )(ctx, mods3, mods3, norm1, w_packed, gk, gmat)


def _hyena_body(zv_ref, z1_ref, z2_ref, cwv_ref, cw1_ref, cw2_ref, cbv_ref, cb1_ref, cb2_ref, hb_ref, kf_ref,
                kn_ref, t1_ref, t2_ref, t1t_ref, t2t_ref, o_ref, lo_ref, hi_ref, g0lo_ref, g0hi_ref, g1lo_ref, g1hi_ref,
                p_ref, m_ref, za_ref, zb_ref):
    row = lax.broadcasted_iota(I32, (HALF, CT), 0)
    rr = lax.broadcasted_iota(I32, (RB, RB), 0)
    cc = lax.broadcasted_iota(I32, (RB, RB), 1)
    flip = jnp.where(rr + cc == RB - 1, 1.0, 0.0).astype(BF16)
    nrb = HALF // RB

    def reverse_upper_half(z_ref, hi_out):
        for j in range(nrb):
            hi_out[j * RB:(j + 1) * RB, :] = _dot(flip, z_ref[0, SEQ - RB * (j + 1):SEQ - RB * j, :])

    def folded_short_conv(z_ref, w_ref, b_ref, lo_out, hi_out):
        zlo = z_ref[0, 0:HALF, :].astype(F32)
        zhi = hi_out[...]
        w0, w1, w2 = w_ref[0:1, :], w_ref[1:2, :], w_ref[2:3, :]
        first, last = row == 0, row == HALF - 1
        lo_prev = jnp.where(first, 0.0, pltpu.roll(zlo, 1, axis=0))
        lo_next = jnp.where(last, zhi[HALF - 1:HALF, :], pltpu.roll(zlo, HALF - 1, axis=0))
        hi_prev = jnp.where(first, 0.0, pltpu.roll(zhi, 1, axis=0))
        hi_next = jnp.where(last, zlo[HALF - 1:HALF, :], pltpu.roll(zhi, HALF - 1, axis=0))
        lo_out[...] = lo_prev * w0 + zlo * w1 + lo_next * w2 + b_ref[...]
        hi_out[...] = hi_next * w0 + zhi * w1 + hi_prev * w2 + b_ref[...]

    gates = ((g0lo_ref, g0hi_ref), (g1lo_ref, g1hi_ref))
    reverse_upper_half(zv_ref, hi_ref)
    reverse_upper_half(z1_ref, g0hi_ref)
    reverse_upper_half(z2_ref, g1hi_ref)
    folded_short_conv(zv_ref, cwv_ref, cbv_ref, lo_ref, hi_ref)
    folded_short_conv(z1_ref, cw1_ref, cb1_ref, g0lo_ref, g0hi_ref)
    folded_short_conv(z2_ref, cw2_ref, cb2_ref, g1lo_ref, g1hi_ref)
    sign8 = _sign_rows(HALF)
    odd = (lax.broadcasted_iota(I32, (FC, CT), 0) & 1) == 1
    for o, (glo_ref, ghi_ref) in enumerate(gates):
        p_ref[...] = (lo_ref[...] + hi_ref[...]).astype(BF16)
        m_ref[...] = (lo_ref[...] - hi_ref[...]).astype(BF16)
        pv = p_ref[...]
        mv = m_ref[...]
        for c in range(HALF // FC):
            ev = slice(c * FC, (c + 1) * FC)
            od = slice(HALF + c * FC, HALF + (c + 1) * FC)
            xce = _dot(t1_ref[ev, :], pv)
            xso = _dot(t1_ref[od, :], pv)
            xco = _dot(t2_ref[ev, :], mv)
            xse = _dot(t2_ref[od, :], mv)
            kce, kse, kco, kso = kf_ref[o, 0, ev, :], kf_ref[o, 1, ev, :], kf_ref[o, 2, ev, :], kf_ref[o, 3, ev, :]
            za_ref[ev, :] = (xce * kce - xse * kse).astype(BF16)
            za_ref[od, :] = (xco * kso + xso * kco).astype(BF16)
            zb_ref[ev, :] = (xco * kco - xso * kso).astype(BF16)
            zb_ref[od, :] = (xce * kse + xse * kce).astype(BF16)
        zn = _dot(sign8, mv)[0:1] * kn_ref[o]
        bias = hb_ref[o:o + 1, :]
        za = za_ref[...]
        zb = zb_ref[...]
        for c in range(HALF // FC):
            rs = slice(c * FC, (c + 1) * FC)
            half_p = _dot(t1t_ref[rs, :], za)
            half_m = _dot(t2t_ref[rs, :], zb) + jnp.where(odd, -zn, zn)
            lo_ref[rs, :] = glo_ref[rs, :] * (half_p + half_m + bias * lo_ref[rs, :])
            hi_ref[rs, :] = ghi_ref[rs, :] * (half_p - half_m + bias * hi_ref[rs, :])
    o_ref[0, 0:HALF, :] = lo_ref[...].astype(BF16)
    for j in range(nrb):
        o_ref[0, SEQ - RB * (j + 1):SEQ - RB * j, :] = _dot(
            flip, hi_ref[j * RB:(j + 1) * RB, :].astype(BF16)).astype(BF16)


def _hyena(zh, conv_w, conv_b, hbias, kf, kn, t1, t2, t1t, t2t):
    nct = HYENA_W // CT
    zspec = lambda k: pl.BlockSpec((1, SEQ, CT), lambda c, b: (b, 0, k * nct + c))
    wspec = lambda k: pl.BlockSpec((3, CT), lambda c, b: (0, k * nct + c))
    bspec = lambda k: pl.BlockSpec((1, CT), lambda c, b: (0, k * nct + c))
    table = lambda shape: pl.BlockSpec(shape, lambda c, b: (0, 0), pipeline_mode=pl.Buffered(1))
    half_f32 = pltpu.VMEM((HALF, CT), F32)
    return pl.pallas_call(
        _hyena_body,
        grid=(nct, BATCH),
        in_specs=[zspec(0), zspec(1), zspec(2), wspec(0), wspec(1), wspec(2), bspec(0), bspec(1), bspec(2),
                  pl.BlockSpec((HYENA_ORDER, CT), lambda c, b: (0, c)),
                  pl.BlockSpec((HYENA_ORDER, 4, HALF, CT), lambda c, b: (0, 0, 0, c)),
                  pl.BlockSpec((HYENA_ORDER, 1, CT), lambda c, b: (0, 0, c)),
                  table((SEQ, HALF)), table((SEQ, HALF)), table((HALF, SEQ)), table((HALF, SEQ))],
        out_specs=pl.BlockSpec((1, SEQ, CT), lambda c, b: (b, 0, c)),
        out_shape=jax.ShapeDtypeStruct((BATCH, SEQ, HYENA_W), BF16),
        scratch_shapes=[half_f32] * 6 + [
            pltpu.VMEM((HALF, CT), BF16), pltpu.VMEM((HALF, CT), BF16),
            pltpu.VMEM((SEQ, CT), BF16), pltpu.VMEM((SEQ, CT), BF16)],
        compiler_params=_params(("parallel", "parallel")),
        name="hyena_conv",
    )(zh, zh, zh, conv_w, conv_w, conv_w, conv_b, conv_b, conv_b, hbias, kf, kn, t1, t2, t1t, t2t)


def _attn_body(sink_ref, q_ref, k_ref, v_ref, kc_ref, vc_ref, bias_ref, o_ref, kp_ref, vlo_ref, vhi_ref):
    nb = SEQ // BLOCK
    lane = lax.broadcasted_iota(I32, (BLOCK, LANES), 1)
    low = lane < HEAD_DIM
    mask_lo = jnp.where(low, 1.0, 0.0).astype(BF16)
    mask_hi = jnp.where(low, 0.0, 1.0).astype(BF16)

    def with_ones(v):
        lane_v = lax.broadcasted_iota(I32, v.shape, 1) < HEAD_DIM
        one = jnp.ones_like(v)
        return jnp.where(lane_v, v, one), jnp.where(lane_v, one, v)

    zpad = jnp.zeros((BLOCK, KV_W), BF16)
    kp_ref[0:BLOCK] = zpad
    kp_ref[BLOCK:BLOCK + SEQ] = k_ref[0]
    kp_ref[BLOCK + SEQ:] = zpad
    v_lo, v_hi = with_ones(v_ref[0])
    for ref, val in ((vlo_ref, v_lo), (vhi_ref, v_hi)):
        ref[0:BLOCK] = zpad
        ref[BLOCK:BLOCK + SEQ] = val
        ref[BLOCK + SEQ:] = zpad
    kc = kc_ref[0]
    vc_pair = with_ones(vc_ref[0])

    def scores_of(n):
        r = pl.multiple_of(n * BLOCK, BLOCK)
        kw = kp_ref[pl.ds(r, 3 * BLOCK), :]
        bias = bias_ref[jnp.where(n == 0, 0, jnp.where(n == nb - 1, 2, 1))]
        scores = []
        for j in range(GROUP):
            qs = q_ref[0, pl.ds(r, BLOCK), j * LANES:(j + 1) * LANES]
            for msk in (mask_lo, mask_hi):
                qm = qs * msk
                scores.append((_dot_nt(qm, kw) + bias, _dot_nt(qm, kc)))
        return scores

    def finish(n, scores):
        r = pl.multiple_of(n * BLOCK, BLOCK)
        vw_pair = (vlo_ref[pl.ds(r, 3 * BLOCK), :], vhi_ref[pl.ds(r, 3 * BLOCK), :])
        probs = []
        for idx, (sw, sc) in enumerate(scores):
            snk = sink_ref[idx // 2 + GROUP * (idx % 2)]
            m = jnp.maximum(jnp.maximum(jnp.max(sw, axis=-1, keepdims=True),
                                        jnp.max(sc, axis=-1, keepdims=True)), snk)
            probs.append((jnp.exp2(sw - m).astype(BF16), jnp.exp2(sc - m).astype(BF16), jnp.exp2(snk - m)))
        outs = []
        for idx, (pw, pc, psink) in enumerate(probs):
            acc = _dot(pw, vw_pair[idx % 2]) + _dot(pc, vc_pair[idx % 2])
            den = pltpu.roll(acc, HEAD_DIM, axis=1) + psink
            outs.append(acc / den)
        for j in range(GROUP):
            o_ref[0, pl.ds(r, BLOCK), j * LANES:(j + 1) * LANES] = jnp.where(
                low, outs[2 * j], outs[2 * j + 1]).astype(BF16)

    def block_group(g, carry):
        first = g * ATTN_UNROLL
        pending = scores_of(first)
        for i in range(ATTN_UNROLL):
            nxt = scores_of(first + i + 1) if i + 1 < ATTN_UNROLL else None
            finish(first + i, pending)
            pending = nxt
        return carry

    lax.fori_loop(0, nb // ATTN_UNROLL, block_group, 0)


def _attention(sink, q, k, v, kc, vc, bias):
    per_b = lambda n, w: pl.BlockSpec((1, n, w), lambda b: (b, 0, 0))
    return pl.pallas_call(
        _attn_body,
        grid=(BATCH,),
        in_specs=[pl.BlockSpec(memory_space=pltpu.SMEM),
                  per_b(SEQ, ATTN_W), per_b(SEQ, KV_W), per_b(SEQ, KV_W), per_b(CTX_LEN, KV_W), per_b(CTX_LEN, KV_W),
                  pl.BlockSpec((3, BLOCK, 3 * BLOCK), lambda b: (0, 0, 0))],
        out_specs=per_b(SEQ, ATTN_W),
        out_shape=jax.ShapeDtypeStruct((BATCH, SEQ, ATTN_W), BF16),
        scratch_shapes=[pltpu.VMEM((SEQ + 2 * BLOCK, KV_W), BF16)] * 3,
        compiler_params=_params(("parallel",)),
        name="window_attn",
    )(sink, q, k, v, kc, vc, bias)


def _merge_body(ya_ref, yb_ref, gt_ref, x_ref, g1_ref, sc2_ref, sh2_ref, n2_ref, wa_ref, wb_ref, wo_ref, rt_ref,
                xn_ref, h2_ref, aff_ref):
    d = D_MODEL
    groups = [slice(i * SUB_MERGE, (i + 1) * SUB_MERGE) for i in range(TM_MERGE // SUB_MERGE)]
    branch = [(_dot(ya_ref[0, s, :], wa_ref[...]), _dot(yb_ref[0, s, :], wb_ref[...])) for s in groups]
    mixed = [(gt_ref[0, s, :d].astype(F32) * ua + gt_ref[0, s, d:].astype(F32) * ub).astype(BF16)
             for s, (ua, ub) in zip(groups, branch)]
    proj = [_dot(u, wo_ref[...]) for u in mixed]
    h2_rows = []
    for s, m in zip(groups, proj):
        xn = x_ref[0, s, :] + g1_ref[0] * m
        xn_ref[0, s, :] = xn
        h2 = _rms_mod(xn, n2_ref[...], sc2_ref[0], sh2_ref[0]).astype(BF16)
        h2_ref[0, s, :] = h2
        h2_rows.append(h2)
    logits = [_dot_nt(rt_ref[...], h2) for h2 in h2_rows]
    for s, lt in zip(groups, logits):
        e = jnp.exp(lt - jnp.max(lt, axis=0, keepdims=True))
        aff_ref[0, :, s] = e / jnp.sum(e, axis=0, keepdims=True)


def _merge(ya, yb, gates, x, mods3, norm2, wa, wb, wo, router_t):
    d = D_MODEL
    nt = SEQ // TM_MERGE
    const = lambda shape: pl.BlockSpec(shape, lambda b, i: (0,) * len(shape))
    tok = lambda w: pl.BlockSpec((1, TM_MERGE, w), lambda b, i: (b, i, 0))
    mod = lambda k: pl.BlockSpec((1, 1, d), lambda b, i: (b, 0, k))
    return pl.pallas_call(
        _merge_body,
        grid=(BATCH, nt),
        in_specs=[tok(HYENA_W), tok(ATTN_W), tok(2 * d), tok(d), mod(2), mod(4), mod(3), const((1, d)),
                  const((HYENA_W, d)), const((ATTN_W, d)), const((d, d)), const((N_EXPERTS, d))],
        out_specs=[tok(d), tok(d), pl.BlockSpec((1, N_EXPERTS, TM_MERGE), lambda b, i: (b, 0, i))],
        out_shape=[jax.ShapeDtypeStruct((BATCH, SEQ, d), F32),
                   jax.ShapeDtypeStruct((BATCH, SEQ, d), BF16),
                   jax.ShapeDtypeStruct((BATCH, N_EXPERTS, SEQ), F32)],
        compiler_params=_params(("parallel", "parallel")),
        name="merge_outproj",
    )(ya, yb, gates, x, mods3, mods3, mods3, norm2, wa, wb, wo, router_t)


def _thresh_body(aff_ref, lo_ref, hi_ref):
    aff = aff_ref[...]
    rows = aff.shape[0]
    bits = pltpu.bitcast(aff, I32)

    def bisect_bits(_, carry):
        lo, hi = carry
        mid = lo + ((hi - lo) >> 1)
        ge = jnp.sum((bits >= mid).astype(I32), axis=1, keepdims=True) >= CAP
        return jnp.where(ge, mid, lo), jnp.where(ge, hi, mid)

    lo0 = jnp.zeros((rows, 1), I32)
    hi0 = jnp.full((rows, 1), 0x3F800001, I32)
    thr_bits, _ = lax.fori_loop(0, 31, bisect_bits, (lo0, hi0))
    thr = pltpu.bitcast(thr_bits, F32)

    def bisect_val(_, carry):
        lo, hi = carry
        mid = 0.5 * (lo + hi)
        ge = jnp.sum(jnp.where(aff >= mid, 1.0, 0.0), axis=1, keepdims=True) >= CAP
        return jnp.where(ge, mid, lo), jnp.where(ge, hi, mid)

    lo, hi = lax.fori_loop(0, 30, bisect_val, (0.5 * thr, jnp.maximum(2.0 * thr, 1e-30)))
    lo_ref[...] = jnp.broadcast_to(lo, lo_ref.shape)
    hi_ref[...] = jnp.broadcast_to(hi, hi_ref.shape)


def _thresholds(aff_rows):
    rows = BATCH * N_EXPERTS
    return pl.pallas_call(
        _thresh_body,
        grid=(1,),
        in_specs=[pl.BlockSpec((rows, SEQ), lambda i: (0, 0))],
        out_specs=[pl.BlockSpec((rows, LANES), lambda i: (0, 0)), pl.BlockSpec((rows, LANES), lambda i: (0, 0))],
        out_shape=[jax.ShapeDtypeStruct((rows, LANES), F32), jax.ShapeDtypeStruct((rows, LANES), F32)],
        compiler_params=_params(("arbitrary",)),
        name="route_threshold",
    )(aff_rows)


def _prefix_counts(mask):
    r = lax.broadcasted_iota(I32, (LANES, LANES), 0)
    c = lax.broadcasted_iota(I32, (LANES, LANES), 1)
    upper = jnp.where(r <= c, 1.0, 0.0).astype(BF16)
    offset = jnp.zeros((mask.shape[0], 1), F32)
    blocks = []
    for j in range(mask.shape[1] // LANES):
        blk = mask[:, j * LANES:(j + 1) * LANES]
        inc = _dot(blk.astype(BF16), upper)
        blocks.append(inc - blk + offset)
        offset = offset + inc[:, LANES - 1:LANES]
    return jnp.concatenate(blocks, axis=1)


def _route_body(aff_ref, lo_ref, hi_ref, h_ref, slot_ref, xin_ref, pos_ref, pbuf_ref):
    aff = aff_ref[0]
    above = jnp.where(aff >= hi_ref[:, 0:1], 1.0, 0.0)
    band = jnp.where(aff >= lo_ref[:, 0:1], 1.0, 0.0) - above
    need = CAP - jnp.sum(above, axis=1, keepdims=True)
    tie_rank = _prefix_counts(band)
    self = above + band * jnp.where(tie_rank < need, 1.0, 0.0)
    pos = _prefix_counts(self)
    posi = jnp.where(self > 0.5, pos.astype(I32), -1)
    pos_ref[0] = posi

    h = h_ref[0]
    slot = slot_ref[...]
    posb = jnp.where(self > 0.5, pos, -1.0).astype(BF16)
    one = jnp.ones((CAP, SEQ), BF16)
    zero = jnp.zeros((CAP, SEQ), BF16)
    for grp in range(N_EXPERTS // EXPERT_GROUP):
        for i in range(EXPERT_GROUP):
            e = grp * EXPERT_GROUP + i
            pbuf_ref[i * CAP:(i + 1) * CAP, :] = jnp.where(posb[e:e + 1, :] == slot, one, zero)
        rows = _dot(pbuf_ref[...], h)
        for i in range(EXPERT_GROUP):
            xin_ref[grp * EXPERT_GROUP + i] = rows[i * CAP:(i + 1) * CAP].astype(BF16)


def _route(aff_t, lo, hi, h2, slot_rows):
    d = D_MODEL
    return pl.pallas_call(
        _route_body,
        grid=(BATCH,),
        in_specs=[pl.BlockSpec((1, N_EXPERTS, SEQ), lambda b: (b, 0, 0)),
                  pl.BlockSpec((N_EXPERTS, LANES), lambda b: (b, 0)),
                  pl.BlockSpec((N_EXPERTS, LANES), lambda b: (b, 0)),
                  pl.BlockSpec((1, SEQ, d), lambda b: (b, 0, 0)),
                  pl.BlockSpec((CAP, SEQ), lambda b: (0, 0))],
        out_specs=[pl.BlockSpec((N_EXPERTS, CAP, d), lambda b: (0, b, 0)),
                   pl.BlockSpec((1, N_EXPERTS, SEQ), lambda b: (b, 0, 0))],
        out_shape=[jax.ShapeDtypeStruct((N_EXPERTS, BATCH * CAP, d), BF16),
                   jax.ShapeDtypeStruct((BATCH, N_EXPERTS, SEQ), I32)],
        scratch_shapes=[pltpu.VMEM((EXPERT_GROUP * CAP, SEQ), BF16)],
        compiler_params=_params(("parallel",)),
        name="route_gather",
    )(aff_t, lo, hi, h2, slot_rows)


def _expert_body(x_ref, wg_ref, wu_ref, wd_ref, o_ref, acc_ref, wgb_ref, wub_ref, wdb_ref):
    f = pl.program_id(1)

    @pl.when((pl.program_id(0) == 0) & (f == 0))
    def _():
        acc_ref[...] = jnp.zeros_like(acc_ref)

    carry = f > 0
    for m in range(BATCH * CAP // MC):
        rows = slice(m * MC, (m + 1) * MC)
        xm = x_ref[0, rows, :]
        if m == 0:
            wgb_ref[...] = wg_ref[0].astype(BF16)
        a = _dot(xm, wgb_ref[...])
        if m == 0:
            wub_ref[...] = wu_ref[0].astype(BF16)
        b = _dot(xm, wub_ref[...])
        hh = (a * jax.nn.sigmoid(a) * b).astype(BF16)
        if m == 0:
            wdb_ref[...] = wd_ref[0].astype(BF16)
        total = jnp.where(carry, acc_ref[rows, :], 0.0) + _dot(hh, wdb_ref[...])
        acc_ref[rows, :] = total
        o_ref[0, rows, :] = total.astype(BF16)


def _experts(xin, w_gate, w_up, w_down):
    d = D_MODEL
    rows = BATCH * CAP
    return pl.pallas_call(
        _expert_body,
        grid=(N_EXPERTS, D_FF // TF),
        in_specs=[pl.BlockSpec((1, rows, d), lambda e, f: (e, 0, 0)),
                  pl.BlockSpec((1, d, TF), lambda e, f: (e, 0, f)),
                  pl.BlockSpec((1, d, TF), lambda e, f: (e, 0, f)),
                  pl.BlockSpec((1, TF, d), lambda e, f: (e, f, 0))],
        out_specs=pl.BlockSpec((1, rows, d), lambda e, f: (e, 0, 0)),
        out_shape=jax.ShapeDtypeStruct((N_EXPERTS, rows, d), BF16),
        scratch_shapes=[pltpu.VMEM((rows, d), F32), pltpu.VMEM((d, TF), BF16), pltpu.VMEM((d, TF), BF16),
                        pltpu.VMEM((TF, d), BF16)],
        compiler_params=_params(("parallel", "arbitrary")),
        name="swiglu_experts",
    )(xin, w_gate, w_up, w_down)


def _scatter_body(pos_ref, aff_ref, y_ref, x_ref, g2_ref, o_ref):
    pos = pos_ref[0].T
    aff = aff_ref[0].T
    slot = lax.broadcasted_iota(I32, (SUB_S, CAP), 1)
    groups = [slice(i * SUB_S, (i + 1) * SUB_S) for i in range(TS // SUB_S)]
    onehots = [jnp.concatenate([jnp.where(pos[s, e:e + 1] == slot, aff[s, e:e + 1], 0.0).astype(BF16)
                                for e in range(N_EXPERTS)], axis=1) for s in groups]
    y = y_ref[...].reshape(N_EXPERTS * CAP, D_MODEL)
    mixed = [_dot(oh, y) for oh in onehots]
    for s, m in zip(groups, mixed):
        o_ref[0, s, :] = x_ref[0, s, :] + g2_ref[0] * m


def _scatter(pos, aff_t, y, xn, mods3):
    d = D_MODEL
    return pl.pallas_call(
        _scatter_body,
        grid=(BATCH, SEQ // TS),
        in_specs=[pl.BlockSpec((1, N_EXPERTS, TS), lambda b, i: (b, 0, i)),
                  pl.BlockSpec((1, N_EXPERTS, TS), lambda b, i: (b, 0, i)),
                  pl.BlockSpec((N_EXPERTS, CAP, d), lambda b, i: (0, b, 0)),
                  pl.BlockSpec((1, TS, d), lambda b, i: (b, i, 0)),
                  pl.BlockSpec((1, 1, d), lambda b, i: (b, 0, 5))],
        out_specs=pl.BlockSpec((1, TS, d), lambda b, i: (b, i, 0)),
        out_shape=jax.ShapeDtypeStruct((BATCH, SEQ, d), F32),
        compiler_params=_params(("parallel", "parallel")),
        name="scatter_residual",
    )(pos, aff_t, y, xn, mods3)


def _rope_tables():
    rows = SEQ // GRID_W
    row = np.repeat(np.arange(rows, dtype=np.float32), GRID_W)
    col = np.tile(np.arange(GRID_W, dtype=np.float32), rows)
    inv = (ROPE_BASE ** (-np.arange(0, AXIS_ROT, 2, dtype=np.float32) / AXIS_ROT)).astype(np.float32)
    ang = np.concatenate([row[:, None] * inv, col[:, None] * inv], axis=-1).astype(np.float64)
    cos = np.repeat(np.cos(ang), 2, axis=-1)
    sin = np.stack([-np.sin(ang), np.sin(ang)], axis=-1).reshape(SEQ, HEAD_DIM)
    reps = LANES // HEAD_DIM
    return jnp.asarray(np.tile(cos, (1, reps)), F32), jnp.asarray(np.tile(sin, (1, reps)), F32)


def _dft_tables():
    idx = np.arange(HALF, dtype=np.int64)
    t2p1 = 2 * idx + 1

    def cos_sin(f):
        ang = ((f[:, None] * t2p1[None, :]) % (2 * N_FFT)) * (math.pi / N_FFT)
        return np.cos(ang), np.sin(ang)

    ce, se = cos_sin(2 * idx)
    co, so = cos_sin(2 * idx + 1)
    t1 = jnp.asarray(np.concatenate([ce, so], axis=0), F32).astype(BF16)
    t2 = jnp.asarray(np.concatenate([co, se], axis=0), F32).astype(BF16)
    return t1, t2, t1.T, t2.T


def _phase_tables():
    idx = np.arange(HALF, dtype=np.float64)
    w = np.full((HALF,), 2.0 / N_FFT)
    we = w.copy()
    we[0] = 1.0 / N_FFT
    pe = (math.pi / N_FFT) * (2.0 * idx)
    po = (math.pi / N_FFT) * (2.0 * idx + 1.0)
    rot = np.stack([we * np.cos(pe), we * np.sin(pe), w * np.cos(po), w * np.sin(po)])
    return jnp.asarray(np.broadcast_to(rot[:, :, None], (4, HALF, CT)), F32)


def _fold_rows(a):
    return np.concatenate([a[:HALF], a[HALF:][::-1]], axis=0)


def _filter_features():
    t = np.linspace(0.0, 1.0, SEQ, dtype=np.float32).astype(np.float64)[:, None]
    w = 2.0 * math.pi * np.arange(SEQ, dtype=np.float64)[:, None] / SEQ
    fr = np.linspace(1e-4, FILTER_BANDS - 1, FILTER_BANDS, dtype=np.float32).astype(np.float64)[None, :]
    feat = np.concatenate([t, np.cos(fr * w), -np.sin(fr * w)], axis=-1)
    feat = np.pad(feat, ((0, 0), (0, FILTER_HIDDEN - FILTER_EMB)))
    min_decay = math.log(DECAY_TARGET) / SLOW_DECAY_PCT
    max_decay = math.log(DECAY_TARGET) / FAST_DECAY_PCT
    deltas = np.linspace(min_decay, max_decay, HYENA_W, dtype=np.float32).astype(np.float64)
    decay = np.exp(-t * np.abs(deltas))
    return jnp.asarray(_fold_rows(feat).T, F32), jnp.asarray(_fold_rows(decay), F32)


def _attn_bias():
    qi = np.arange(BLOCK)[:, None]
    kj = np.arange(3 * BLOCK)[None, :]
    band = np.abs(kj - BLOCK - qi) <= WINDOW
    first = band & (kj >= BLOCK)
    last = band & (kj < 2 * BLOCK)
    return jnp.asarray(np.where(np.stack([first, band, last]), 0.0, NEG), F32)


def _pair_heads(w, axis):
    heads = [lax.slice_in_dim(w, h * HEAD_DIM, (h + 1) * HEAD_DIM, axis=axis) for h in range(N_HEADS)]
    return jnp.concatenate([heads[j + GROUP * half] for j in range(GROUP) for half in range(N_KV_HEADS)], axis=axis)


def kernel(x, c, ctx, c_ctx, ada_w, ada_b, norm1, norm2, w_in, conv_w, conv_b, filt_w1, filt_b1, filt_w2, filt_b2,
           filt_w3, filt_b3, filt_freq, filt_out, hyena_bias, q_norm, k_norm, attn_sink, w_branch_a, w_branch_b,
           w_out, router, w_gate, w_up, w_down):
    d = D_MODEL
    assert ada_w.shape[0] == 1, "only the single-layer configuration is implemented"
    l = 0
    cos_t, sin_t = _rope_tables()
    t1, t2, t1t, t2t = _dft_tables()
    rot = _phase_tables()
    feat, decay = _filter_features()
    bias = _attn_bias()
    gmat = jnp.asarray(np.kron(np.eye(2 * LANES // HEAD_DIM), np.full((HEAD_DIM, HEAD_DIM), 1.0 / HEAD_DIM)), BF16)
    c16 = jnp.concatenate([c, c_ctx[None, :], jnp.zeros((MOD_ROWS - BATCH - 1, d), F32)], axis=0)

    mods3 = _ada(c16, ada_w[l], ada_b[l][None, :])
    n1 = norm1[l][None, :]
    w_packed = _pack_in_weights(w_in[l])
    gk = jnp.tile(k_norm[l], N_KV_HEADS)[None, :]
    w1_rows = jnp.pad(filt_w1[l], ((0, FILTER_HIDDEN - FILTER_EMB), (0, 0)))
    mlp_w = jnp.swapaxes(jnp.stack([w1_rows, filt_w2[l], filt_w3[l]]), 1, 2)
    mlp_cols = jnp.stack([filt_b1[l], filt_b2[l], filt_b3[l], filt_freq[l]], axis=1)
    kf, kn = _filters(feat, mlp_w, mlp_cols, filt_out[l], decay, rot, t1, t2)
    kc, vc = _ctx_proj(ctx, mods3, n1, w_packed, gk, gmat)
    zh, q, k, v, gates = _inproj(x, mods3, n1, w_packed, jnp.tile(q_norm[l], N_HEADS)[None, :], gk, gmat, cos_t, sin_t)
    ya = _hyena(zh, conv_w[l], conv_b[l][None, :], hyena_bias[l], kf, kn, t1, t2, t1t, t2t)
    yb = _attention(attn_sink[l] * LOG2E, q, k, v, kc, vc, bias)
    xn, h2, aff_t = _merge(ya, yb, gates, x, mods3, norm2[l][None, :], w_branch_a[l].astype(BF16),
                           _pair_heads(w_branch_b[l], 0).astype(BF16), w_out[l].astype(BF16), router[l].T.astype(BF16))
    lo, hi = _thresholds(aff_t.reshape(BATCH * N_EXPERTS, SEQ))
    slot_rows = jnp.asarray(np.broadcast_to(np.arange(CAP)[:, None], (CAP, SEQ)), BF16)
    xin, pos = _route(aff_t, lo, hi, h2, slot_rows)
    y = _experts(xin, w_gate[l], w_up[l], w_down[l])
    return _scatter(pos, aff_t, y, xn, mods3)
```

```python
import math

import numpy as np
import jax
import jax.numpy as jnp
from jax import lax
from jax.experimental import pallas as pl
from jax.experimental.pallas import tpu as pltpu

F32 = jnp.float32
BF16 = jnp.bfloat16
I32 = jnp.int32
HIGHEST = lax.Precision.HIGHEST

D_MODEL = 1024
BATCH = 8
SEQ = 2048
GRID_W = 64
CTX_LEN = 256
N_HEADS = 8
N_KV_HEADS = 2
HEAD_DIM = 64
GROUP = N_HEADS // N_KV_HEADS
ATTN_W = N_HEADS * HEAD_DIM
KV_W = N_KV_HEADS * HEAD_DIM
WINDOW = 128
BLOCK = 128
HYENA_W = D_MODEL // 2
HYENA_ORDER = 2
FILTER_BANDS = 16
FILTER_EMB = 1 + 2 * FILTER_BANDS
FILTER_HIDDEN = 64
DECAY_TARGET = 1e-2
FAST_DECAY_PCT = 0.3
SLOW_DECAY_PCT = 1.5
ROPE_BASE = 10000.0
AXIS_ROT = HEAD_DIM // 2
N_EXPERTS = 16
EC_CAPACITY = 2
D_FF = 2048
EPS = 1e-6
NEG = -1e30
LOG2E = math.log2(math.e)

OFF_Q = 3 * HYENA_W
OFF_K = OFF_Q + ATTN_W
OFF_V = OFF_K + KV_W
OFF_G = OFF_V + KV_W

CAP = EC_CAPACITY * SEQ // N_EXPERTS
N_FFT = 2 * SEQ
HALF = SEQ // 2
MOD_ROWS = 16
LANES = 128

TM_IN = 1024
SUB_IN = 256
TM_MERGE = 1024
SUB_MERGE = 512
CT = 256
FC = 512
RB = 256
TF = 512
MC = 512
TS = 1024
SUB_S = 512
CTX_STEP = 4
EXPERT_GROUP = 4
ATTN_UNROLL = 8
VMEM_LIMIT = 56 * 1024 * 1024


def _dot(a, b, precision=None):
    return jnp.dot(a, b, preferred_element_type=F32, precision=precision)


def _dot_nt(a, b, precision=None):
    return lax.dot_general(a, b, (((1,), (1,)), ((), ())), preferred_element_type=F32, precision=precision)


def _params(sem, vmem=VMEM_LIMIT):
    return pltpu.CompilerParams(dimension_semantics=sem, vmem_limit_bytes=vmem)


def _rms_mod(x, g, sc, sh):
    ms = jnp.mean(x * x, axis=-1, keepdims=True)
    return (x * lax.rsqrt(ms + EPS) * g) * (1.0 + sc) + sh


def _head_norm_rope(z, g, gmat, cos, sin, scale):
    ms = _dot((z * z).astype(BF16), gmat)
    y = z * lax.rsqrt(ms + EPS) * g
    if cos is not None:
        slabs = []
        for s in range(z.shape[1] // LANES):
            ys = y[:, s * LANES:(s + 1) * LANES]
            lane = lax.broadcasted_iota(I32, ys.shape, 1)
            nxt = pltpu.roll(ys, LANES - 1, axis=1)
            prv = pltpu.roll(ys, 1, axis=1)
            slabs.append(ys * cos + jnp.where((lane & 1) == 0, nxt, prv) * sin)
        y = slabs[0] if len(slabs) == 1 else jnp.concatenate(slabs, axis=1)
    return y * scale


def _split_bf16(x):
    hi = x.astype(BF16)
    return hi, (x - hi.astype(F32)).astype(BF16)


def _ada_body(c_ref, w_ref, b_ref, o_ref):
    c = c_ref[...]
    s_hi, s_lo = _split_bf16(c * jax.nn.sigmoid(c))
    w_hi, w_lo = _split_bf16(w_ref[...])
    o_ref[:, 0, :] = _dot(s_hi, w_hi) + _dot(s_lo, w_hi) + _dot(s_hi, w_lo) + b_ref[...]


def _ada(c16, w, b):
    d = D_MODEL
    return pl.pallas_call(
        _ada_body,
        grid=(6,),
        in_specs=[pl.BlockSpec((MOD_ROWS, d), lambda j: (0, 0)),
                  pl.BlockSpec((d, d), lambda j: (0, j)),
                  pl.BlockSpec((1, d), lambda j: (0, j))],
        out_specs=pl.BlockSpec((MOD_ROWS, 1, d), lambda j: (0, 0, j)),
        out_shape=jax.ShapeDtypeStruct((MOD_ROWS, 1, 6 * d), F32),
        compiler_params=_params(("parallel",)),
        name="ada_mod",
    )(c16, w, b)


def _sign_rows(n):
    lane = lax.broadcasted_iota(I32, (8, n), 1)
    sub = lax.broadcasted_iota(I32, (8, n), 0)
    sg = jnp.where((lane & 1) == 0, 1.0, -1.0)
    return jnp.where(sub == 0, sg, 0.0).astype(BF16)


def _filt_body(feat_ref, w_ref, col_ref, fof_ref, fob_ref, dec_ref, rot_ref, t1_ref, t2_ref, kf_ref, kn_ref, hs_ref):
    @pl.when((pl.program_id(0) == 0) & (pl.program_id(1) == 0))
    def _():
        cols = col_ref[...]
        fq = cols[:, 3:4]
        h = feat_ref[...]
        for layer in range(3):
            h = jnp.sin(fq * (_dot(w_ref[layer], h, HIGHEST) + cols[:, layer:layer + 1]))
        hi = h.astype(BF16).astype(F32)
        stacked = jnp.concatenate([hi, h - hi, hi, jnp.zeros_like(hi)], axis=0)
        hs_ref[...] = stacked.T.astype(BF16)

    def taps(fo_ref):
        f_hi, f_lo = _split_bf16(fo_ref[...])
        return _dot(hs_ref[...], jnp.concatenate([f_hi, f_hi, f_lo, jnp.zeros_like(f_hi)], axis=0))

    dec = dec_ref[...]
    hf = taps(fof_ref) * dec
    hb = taps(fob_ref) * dec
    row = lax.broadcasted_iota(I32, hf.shape, 0)
    hb = jnp.where(row == 0, 0.0, hb)
    a = hf + hb
    b = hf - hb
    pa = (a[:HALF] + a[HALF:]).astype(BF16)
    ma = (a[:HALF] - a[HALF:]).astype(BF16)
    pb = (b[:HALF] + b[HALF:]).astype(BF16)
    mb = (b[:HALF] - b[HALF:]).astype(BF16)
    t1 = t1_ref[...]
    t2 = t2_ref[...]
    ce, se, co, so = rot_ref[0], rot_ref[1], rot_ref[2], rot_ref[3]
    a1 = _dot(t1, pa)
    a2 = _dot(t2, ma)
    kf_ref[0, 0] = a1[:HALF] * ce + a2[HALF:] * se
    kf_ref[0, 2] = a2[:HALF] * co + a1[HALF:] * so
    b1 = _dot(t1, pb)
    b2 = _dot(t2, mb)
    kf_ref[0, 1] = b2[HALF:] * ce - b1[:HALF] * se
    kf_ref[0, 3] = b1[HALF:] * co - b2[:HALF] * so
    kn_ref[0] = _dot(_sign_rows(HALF), ma)[0:1] * (1.0 / N_FFT)


def _filters(feat, mlp_w, mlp_cols, fout, decay, rot, t1, t2):
    nct = HYENA_W // CT
    full = lambda shape: pl.BlockSpec(shape, lambda o, c: (0,) * len(shape))
    return pl.pallas_call(
        _filt_body,
        grid=(HYENA_ORDER, nct),
        in_specs=[full((FILTER_HIDDEN, SEQ)), full((3, FILTER_HIDDEN, FILTER_HIDDEN)), full((FILTER_HIDDEN, 4)),
                  pl.BlockSpec((FILTER_HIDDEN, CT), lambda o, c: (0, (o * 2 + 0) * nct + c)),
                  pl.BlockSpec((FILTER_HIDDEN, CT), lambda o, c: (0, (o * 2 + 1) * nct + c)),
                  pl.BlockSpec((SEQ, CT), lambda o, c: (0, c)),
                  full((4, HALF, CT)),
                  pl.BlockSpec((SEQ, HALF), lambda o, c: (0, 0), pipeline_mode=pl.Buffered(1)),
                  pl.BlockSpec((SEQ, HALF), lambda o, c: (0, 0), pipeline_mode=pl.Buffered(1))],
        out_specs=[pl.BlockSpec((1, 4, HALF, CT), lambda o, c: (o, 0, 0, c)),
                   pl.BlockSpec((1, 1, CT), lambda o, c: (o, 0, c))],
        out_shape=[jax.ShapeDtypeStruct((HYENA_ORDER, 4, HALF, HYENA_W), F32),
                   jax.ShapeDtypeStruct((HYENA_ORDER, 1, HYENA_W), F32)],
        scratch_shapes=[pltpu.VMEM((SEQ, 4 * FILTER_HIDDEN), BF16)],
        compiler_params=_params(("arbitrary", "arbitrary")),
        name="hyena_filters",
    )(feat, mlp_w, mlp_cols, fout, fout, decay, rot, t1, t2)


def _pair_head_lanes(z):
    slabs = [z[:, s * LANES:(s + 1) * LANES] for s in range(ATTN_W // LANES)]
    swapped = [pltpu.roll(sl, HEAD_DIM, axis=1) for sl in slabs]
    low = lax.broadcasted_iota(I32, slabs[0].shape, 1) < HEAD_DIM
    out = []
    for j in range(GROUP):
        first, second = j, j + GROUP
        lo_src = slabs[first // 2] if first % 2 == 0 else swapped[first // 2]
        hi_src = slabs[second // 2] if second % 2 == 1 else swapped[second // 2]
        out.append(jnp.where(low, lo_src, hi_src))
    return jnp.concatenate(out, axis=1)


def _inproj_body(x_ref, sc_ref, sh_ref, n1_ref, wh_ref, wq_ref, wkv_ref, wg_ref, gq_ref, gk_ref, gmat_ref,
                 cos_ref, sin_ref, zh_ref, q_ref, k_ref, v_ref, gate_ref):
    groups = [slice(i * SUB_IN, (i + 1) * SUB_IN) for i in range(TM_IN // SUB_IN)]
    hx = [_rms_mod(x_ref[0, s, :], n1_ref[...], sc_ref[0], sh_ref[0]).astype(BF16) for s in groups]
    for s, h in zip(groups, hx):
        zh_ref[0, s, :] = _dot(h, wh_ref[...]).astype(BF16)
    pair = 2 * LANES
    zq = [_pair_head_lanes(_dot(h, wq_ref[...])) for h in hx]
    for s, z in zip(groups, zq):
        for c in range(ATTN_W // pair):
            sl = slice(c * pair, (c + 1) * pair)
            q_ref[0, s, sl] = _head_norm_rope(z[:, sl], gq_ref[:, sl], gmat_ref[...], cos_ref[s, :], sin_ref[s, :],
                                              LOG2E * HEAD_DIM ** -0.5).astype(BF16)
    zkv = [_dot(h, wkv_ref[...]) for h in hx]
    for s, z in zip(groups, zkv):
        k_ref[0, s, :] = _head_norm_rope(z[:, :KV_W], gk_ref[...], gmat_ref[0:KV_W, 0:KV_W], cos_ref[s, :],
                                         sin_ref[s, :], 1.0).astype(BF16)
        v_ref[0, s, :] = z[:, KV_W:].astype(BF16)
    for s, h in zip(groups, hx):
        gate_ref[0, s, :] = jax.nn.sigmoid(_dot(h, wg_ref[...])).astype(BF16)


def _pack_in_weights(w):
    wb = w.astype(BF16)
    return jnp.concatenate([wb[:, :OFF_K], wb[:, OFF_G:], wb[:, OFF_K:OFF_G]], axis=1)


W_OFF_H = 0
W_OFF_Q = OFF_Q
W_OFF_G = OFF_Q + ATTN_W
W_OFF_KV = W_OFF_G + 2 * D_MODEL


def _inproj(x, mods3, norm1, w_packed, gq, gk, gmat, cos_t, sin_t):
    d = D_MODEL
    nt = SEQ // TM_IN
    const = lambda shape: pl.BlockSpec(shape, lambda b, i: (0,) * len(shape))

    def wcol(width, off):
        assert off % width == 0
        return pl.BlockSpec((d, width), lambda b, i: (0, off // width))

    tok = lambda w: pl.BlockSpec((1, TM_IN, w), lambda b, i: (b, i, 0))
    return pl.pallas_call(
        _inproj_body,
        grid=(BATCH, nt),
        in_specs=[tok(d),
                  pl.BlockSpec((1, 1, d), lambda b, i: (b, 0, 1)),
                  pl.BlockSpec((1, 1, d), lambda b, i: (b, 0, 0)),
                  const((1, d)), wcol(OFF_Q, W_OFF_H), wcol(ATTN_W, W_OFF_Q), wcol(2 * KV_W, W_OFF_KV),
                  wcol(2 * d, W_OFF_G), const((1, ATTN_W)), const((1, KV_W)), const((2 * LANES, 2 * LANES)),
                  pl.BlockSpec((TM_IN, LANES), lambda b, i: (i, 0)),
                  pl.BlockSpec((TM_IN, LANES), lambda b, i: (i, 0))],
        out_specs=[tok(OFF_Q), tok(ATTN_W), tok(KV_W), tok(KV_W), tok(2 * d)],
        out_shape=[jax.ShapeDtypeStruct((BATCH, SEQ, OFF_Q), BF16),
                   jax.ShapeDtypeStruct((BATCH, SEQ, ATTN_W), BF16),
                   jax.ShapeDtypeStruct((BATCH, SEQ, KV_W), BF16),
                   jax.ShapeDtypeStruct((BATCH, SEQ, KV_W), BF16),
                   jax.ShapeDtypeStruct((BATCH, SEQ, 2 * d), BF16)],
        compiler_params=_params(("parallel", "parallel")),
        name="in_proj",
    )(x, mods3, mods3, norm1, w_packed, w_packed, w_packed, w_packed, gq, gk, gmat, cos_t, sin_t)


def _ctx_body(c_ref, sc_ref, sh_ref, n1_ref, wkv_ref, gk_ref, gmat_ref, kc_ref, vc_ref):
    for i in range(CTX_STEP):
        hc = _rms_mod(c_ref[i], n1_ref[...], sc_ref[0], sh_ref[0]).astype(BF16)
        z = _dot(hc, wkv_ref[...])
        kc_ref[i] = _head_norm_rope(z[:, :KV_W], gk_ref[...], gmat_ref[0:KV_W, 0:KV_W], None, None, 1.0).astype(BF16)
        vc_ref[i] = z[:, KV_W:].astype(BF16)


def _ctx_proj(ctx, mods3, norm1, w_packed, gk, gmat):
    d = D_MODEL
    const = lambda shape: pl.BlockSpec(shape, lambda b: (0,) * len(shape))
    return pl.pallas_call(
        _ctx_body,
        grid=(BATCH // CTX_STEP,),
        in_specs=[pl.BlockSpec((CTX_STEP, CTX_LEN, d), lambda b: (b, 0, 0)),
                  pl.BlockSpec((1, 1, d), lambda b: (BATCH, 0, 1)),
                  pl.BlockSpec((1, 1, d), lambda b: (BATCH, 0, 0)),
                  const((1, d)), pl.BlockSpec((d, 2 * KV_W), lambda b: (0, W_OFF_KV // (2 * KV_W))),
                  const((1, KV_W)), const((2 * LANES, 2 * LANES))],
        out_specs=[pl.BlockSpec((CTX_STEP, CTX_LEN, KV_W), lambda b: (b, 0, 0)),
                   pl.BlockSpec((CTX_STEP, CTX_LEN, KV_W), lambda b: (b, 0, 0))],
        out_shape=[jax.ShapeDtypeStruct((BATCH, CTX_LEN, KV_W), BF16),
                   jax.ShapeDtypeStruct((BATCH, CTX_LEN, KV_W), BF16)],
        compiler_params=_params(("parallel",)),
        name="ctx_proj",
    )(ctx, mods3, mods3, norm1, w_packed, gk, gmat)


def _hyena_body(zv_ref, z1_ref, z2_ref, cwv_ref, cw1_ref, cw2_ref, cbv_ref, cb1_ref, cb2_ref, hb_ref, kf_ref,
                kn_ref, t1_ref, t2_ref, t1t_ref, t2t_ref, o_ref, lo_ref, hi_ref, g0lo_ref, g0hi_ref, g1lo_ref, g1hi_ref,
                p_ref, m_ref, za_ref, zb_ref):
    row = lax.broadcasted_iota(I32, (HALF, CT), 0)
    rr = lax.broadcasted_iota(I32, (RB, RB), 0)
    cc = lax.broadcasted_iota(I32, (RB, RB), 1)
    flip = jnp.where(rr + cc == RB - 1, 1.0, 0.0).astype(BF16)
    nrb = HALF // RB

    def reverse_upper_half(z_ref, hi_out):
        for j in range(nrb):
            hi_out[j * RB:(j + 1) * RB, :] = _dot(flip, z_ref[0, SEQ - RB * (j + 1):SEQ - RB * j, :])

    def folded_short_conv(z_ref, w_ref, b_ref, lo_out, hi_out):
        zlo = z_ref[0, 0:HALF, :].astype(F32)
        zhi = hi_out[...]
        w0, w1, w2 = w_ref[0:1, :], w_ref[1:2, :], w_ref[2:3, :]
        first, last = row == 0, row == HALF - 1
        lo_prev = jnp.where(first, 0.0, pltpu.roll(zlo, 1, axis=0))
        lo_next = jnp.where(last, zhi[HALF - 1:HALF, :], pltpu.roll(zlo, HALF - 1, axis=0))
        hi_prev = jnp.where(first, 0.0, pltpu.roll(zhi, 1, axis=0))
        hi_next = jnp.where(last, zlo[HALF - 1:HALF, :], pltpu.roll(zhi, HALF - 1, axis=0))
        lo_out[...] = lo_prev * w0 + zlo * w1 + lo_next * w2 + b_ref[...]
        hi_out[...] = hi_next * w0 + zhi * w1 + hi_prev * w2 + b_ref[...]

    gates = ((g0lo_ref, g0hi_ref), (g1lo_ref, g1hi_ref))
    reverse_upper_half(zv_ref, hi_ref)
    reverse_upper_half(z1_ref, g0hi_ref)
    reverse_upper_half(z2_ref, g1hi_ref)
    folded_short_conv(zv_ref, cwv_ref, cbv_ref, lo_ref, hi_ref)
    folded_short_conv(z1_ref, cw1_ref, cb1_ref, g0lo_ref, g0hi_ref)
    folded_short_conv(z2_ref, cw2_ref, cb2_ref, g1lo_ref, g1hi_ref)
    sign8 = _sign_rows(HALF)
    odd = (lax.broadcasted_iota(I32, (FC, CT), 0) & 1) == 1
    for o, (glo_ref, ghi_ref) in enumerate(gates):
        p_ref[...] = (lo_ref[...] + hi_ref[...]).astype(BF16)
        m_ref[...] = (lo_ref[...] - hi_ref[...]).astype(BF16)
        pv = p_ref[...]
        mv = m_ref[...]
        for c in range(HALF // FC):
            ev = slice(c * FC, (c + 1) * FC)
            od = slice(HALF + c * FC, HALF + (c + 1) * FC)
            xce = _dot(t1_ref[ev, :], pv)
            xse = _dot(t2_ref[od, :], mv)
            kce, kse = kf_ref[o, 0, ev, :], kf_ref[o, 1, ev, :]
            za_ref[ev, :] = (xce * kce - xse * kse).astype(BF16)
            zb_ref[od, :] = (xce * kse + xse * kce).astype(BF16)
            xco = _dot(t2_ref[ev, :], mv)
            xso = _dot(t1_ref[od, :], pv)
            kco, kso = kf_ref[o, 2, ev, :], kf_ref[o, 3, ev, :]
            za_ref[od, :] = (xco * kso + xso * kco).astype(BF16)
            zb_ref[ev, :] = (xco * kco - xso * kso).astype(BF16)
        zn = _dot(sign8, mv)[0:1] * kn_ref[o]
        bias = hb_ref[o:o + 1, :]
        za = za_ref[...]
        zb = zb_ref[...]
        for c in range(HALF // FC):
            rs = slice(c * FC, (c + 1) * FC)
            half_p = _dot(t1t_ref[rs, :], za)
            half_m = _dot(t2t_ref[rs, :], zb) + jnp.where(odd, -zn, zn)
            lo_ref[rs, :] = glo_ref[rs, :] * (half_p + half_m + bias * lo_ref[rs, :])
            hi_ref[rs, :] = ghi_ref[rs, :] * (half_p - half_m + bias * hi_ref[rs, :])
    o_ref[0, 0:HALF, :] = lo_ref[...].astype(BF16)
    for j in range(nrb):
        o_ref[0, SEQ - RB * (j + 1):SEQ - RB * j, :] = _dot(
            flip, hi_ref[j * RB:(j + 1) * RB, :].astype(BF16)).astype(BF16)


def _hyena(zh, conv_w, conv_b, hbias, kf, kn, t1, t2, t1t, t2t):
    nct = HYENA_W // CT
    zspec = lambda k: pl.BlockSpec((1, SEQ, CT), lambda c, b: (b, 0, k * nct + c))
    wspec = lambda k: pl.BlockSpec((3, CT), lambda c, b: (0, k * nct + c))
    bspec = lambda k: pl.BlockSpec((1, CT), lambda c, b: (0, k * nct + c))
    table = lambda shape: pl.BlockSpec(shape, lambda c, b: (0, 0), pipeline_mode=pl.Buffered(1))
    half_f32 = pltpu.VMEM((HALF, CT), F32)
    return pl.pallas_call(
        _hyena_body,
        grid=(nct, BATCH),
        in_specs=[zspec(0), zspec(1), zspec(2), wspec(0), wspec(1), wspec(2), bspec(0), bspec(1), bspec(2),
                  pl.BlockSpec((HYENA_ORDER, CT), lambda c, b: (0, c)),
                  pl.BlockSpec((HYENA_ORDER, 4, HALF, CT), lambda c, b: (0, 0, 0, c)),
                  pl.BlockSpec((HYENA_ORDER, 1, CT), lambda c, b: (0, 0, c)),
                  table((SEQ, HALF)), table((SEQ, HALF)), table((HALF, SEQ)), table((HALF, SEQ))],
        out_specs=pl.BlockSpec((1, SEQ, CT), lambda c, b: (b, 0, c)),
        out_shape=jax.ShapeDtypeStruct((BATCH, SEQ, HYENA_W), BF16),
        scratch_shapes=[half_f32] * 6 + [
            pltpu.VMEM((HALF, CT), BF16), pltpu.VMEM((HALF, CT), BF16),
            pltpu.VMEM((SEQ, CT), BF16), pltpu.VMEM((SEQ, CT), BF16)],
        compiler_params=_params(("parallel", "parallel")),
        name="hyena_conv",
    )(zh, zh, zh, conv_w, conv_w, conv_w, conv_b, conv_b, conv_b, hbias, kf, kn, t1, t2, t1t, t2t)


def _attn_body(sink_ref, q_ref, k_ref, v_ref, kc_ref, vc_ref, bias_ref, o_ref, kp_ref, vlo_ref, vhi_ref):
    nb = SEQ // BLOCK
    lane = lax.broadcasted_iota(I32, (BLOCK, LANES), 1)
    low = lane < HEAD_DIM
    mask_lo = jnp.where(low, 1.0, 0.0).astype(BF16)
    mask_hi = jnp.where(low, 0.0, 1.0).astype(BF16)

    def with_ones(v):
        lane_v = lax.broadcasted_iota(I32, v.shape, 1) < HEAD_DIM
        one = jnp.ones_like(v)
        return jnp.where(lane_v, v, one), jnp.where(lane_v, one, v)

    zpad = jnp.zeros((BLOCK, KV_W), BF16)
    kp_ref[0:BLOCK] = zpad
    kp_ref[BLOCK:BLOCK + SEQ] = k_ref[0]
    kp_ref[BLOCK + SEQ:] = zpad
    v_lo, v_hi = with_ones(v_ref[0])
    for ref, val in ((vlo_ref, v_lo), (vhi_ref, v_hi)):
        ref[0:BLOCK] = zpad
        ref[BLOCK:BLOCK + SEQ] = val
        ref[BLOCK + SEQ:] = zpad
    kc = kc_ref[0]
    vc_pair = with_ones(vc_ref[0])

    def scores_of(n):
        r = pl.multiple_of(n * BLOCK, BLOCK)
        kw = kp_ref[pl.ds(r, 3 * BLOCK), :]
        bias = bias_ref[jnp.where(n == 0, 0, jnp.where(n == nb - 1, 2, 1))]
        scores = []
        for j in range(GROUP):
            qs = q_ref[0, pl.ds(r, BLOCK), j * LANES:(j + 1) * LANES]
            for msk in (mask_lo, mask_hi):
                qm = qs * msk
                scores.append((_dot_nt(qm, kw) + bias, _dot_nt(qm, kc)))
        return scores

    def finish(n, scores):
        r = pl.multiple_of(n * BLOCK, BLOCK)
        vw_pair = (vlo_ref[pl.ds(r, 3 * BLOCK), :], vhi_ref[pl.ds(r, 3 * BLOCK), :])
        probs = []
        for idx, (sw, sc) in enumerate(scores):
            snk = sink_ref[idx // 2 + GROUP * (idx % 2)]
            m = jnp.maximum(jnp.maximum(jnp.max(sw, axis=-1, keepdims=True),
                                        jnp.max(sc, axis=-1, keepdims=True)), snk)
            probs.append((jnp.exp2(sw - m).astype(BF16), jnp.exp2(sc - m).astype(BF16), jnp.exp2(snk - m)))
        outs = []
        for idx, (pw, pc, psink) in enumerate(probs):
            acc = _dot(pw, vw_pair[idx % 2]) + _dot(pc, vc_pair[idx % 2])
            den = pltpu.roll(acc, HEAD_DIM, axis=1) + psink
            outs.append(acc / den)
        for j in range(GROUP):
            o_ref[0, pl.ds(r, BLOCK), j * LANES:(j + 1) * LANES] = jnp.where(
                low, outs[2 * j], outs[2 * j + 1]).astype(BF16)

    def block_group(g, carry):
        first = g * ATTN_UNROLL
        pending = scores_of(first)
        for i in range(ATTN_UNROLL):
            nxt = scores_of(first + i + 1) if i + 1 < ATTN_UNROLL else None
            finish(first + i, pending)
            pending = nxt
        return carry

    lax.fori_loop(0, nb // ATTN_UNROLL, block_group, 0)


def _attention(sink, q, k, v, kc, vc, bias):
    per_b = lambda n, w: pl.BlockSpec((1, n, w), lambda b: (b, 0, 0))
    return pl.pallas_call(
        _attn_body,
        grid=(BATCH,),
        in_specs=[pl.BlockSpec(memory_space=pltpu.SMEM),
                  per_b(SEQ, ATTN_W), per_b(SEQ, KV_W), per_b(SEQ, KV_W), per_b(CTX_LEN, KV_W), per_b(CTX_LEN, KV_W),
                  pl.BlockSpec((3, BLOCK, 3 * BLOCK), lambda b: (0, 0, 0))],
        out_specs=per_b(SEQ, ATTN_W),
        out_shape=jax.ShapeDtypeStruct((BATCH, SEQ, ATTN_W), BF16),
        scratch_shapes=[pltpu.VMEM((SEQ + 2 * BLOCK, KV_W), BF16)] * 3,
        compiler_params=_params(("parallel",)),
        name="window_attn",
    )(sink, q, k, v, kc, vc, bias)


def _merge_body(ya_ref, yb_ref, gt_ref, x_ref, g1_ref, sc2_ref, sh2_ref, n2_ref, wa_ref, wb_ref, wo_ref, rt_ref,
                xn_ref, h2_ref, aff_ref):
    d = D_MODEL
    groups = [slice(i * SUB_MERGE, (i + 1) * SUB_MERGE) for i in range(TM_MERGE // SUB_MERGE)]
    branch = [(_dot(ya_ref[0, s, :], wa_ref[...]), _dot(yb_ref[0, s, :], wb_ref[...])) for s in groups]
    mixed = [(gt_ref[0, s, :d].astype(F32) * ua + gt_ref[0, s, d:].astype(F32) * ub).astype(BF16)
             for s, (ua, ub) in zip(groups, branch)]
    proj = [_dot(u, wo_ref[...]) for u in mixed]
    h2_rows = []
    for s, m in zip(groups, proj):
        xn = x_ref[0, s, :] + g1_ref[0] * m
        xn_ref[0, s, :] = xn
        h2 = _rms_mod(xn, n2_ref[...], sc2_ref[0], sh2_ref[0]).astype(BF16)
        h2_ref[0, s, :] = h2
        h2_rows.append(h2)
    logits = [_dot_nt(rt_ref[...], h2) for h2 in h2_rows]
    for s, lt in zip(groups, logits):
        e = jnp.exp(lt - jnp.max(lt, axis=0, keepdims=True))
        aff_ref[0, :, s] = e / jnp.sum(e, axis=0, keepdims=True)


def _merge(ya, yb, gates, x, mods3, norm2, wa, wb, wo, router_t):
    d = D_MODEL
    nt = SEQ // TM_MERGE
    const = lambda shape: pl.BlockSpec(shape, lambda b, i: (0,) * len(shape))
    tok = lambda w: pl.BlockSpec((1, TM_MERGE, w), lambda b, i: (b, i, 0))
    mod = lambda k: pl.BlockSpec((1, 1, d), lambda b, i: (b, 0, k))
    return pl.pallas_call(
        _merge_body,
        grid=(BATCH, nt),
        in_specs=[tok(HYENA_W), tok(ATTN_W), tok(2 * d), tok(d), mod(2), mod(4), mod(3), const((1, d)),
                  const((HYENA_W, d)), const((ATTN_W, d)), const((d, d)), const((N_EXPERTS, d))],
        out_specs=[tok(d), tok(d), pl.BlockSpec((1, N_EXPERTS, TM_MERGE), lambda b, i: (b, 0, i))],
        out_shape=[jax.ShapeDtypeStruct((BATCH, SEQ, d), F32),
                   jax.ShapeDtypeStruct((BATCH, SEQ, d), BF16),
                   jax.ShapeDtypeStruct((BATCH, N_EXPERTS, SEQ), F32)],
        compiler_params=_params(("parallel", "parallel")),
        name="merge_outproj",
    )(ya, yb, gates, x, mods3, mods3, mods3, norm2, wa, wb, wo, router_t)


def _thresh_body(aff_ref, lo_ref, hi_ref):
    aff = aff_ref[...]
    rows = aff.shape[0]
    bits = pltpu.bitcast(aff, I32)

    def bisect_bits(_, carry):
        lo, hi = carry
        mid = lo + ((hi - lo) >> 1)
        ge = jnp.sum((bits >= mid).astype(I32), axis=1, keepdims=True) >= CAP
        return jnp.where(ge, mid, lo), jnp.where(ge, hi, mid)

    lo0 = jnp.zeros((rows, 1), I32)
    hi0 = jnp.full((rows, 1), 0x3F800001, I32)
    thr_bits, _ = lax.fori_loop(0, 31, bisect_bits, (lo0, hi0))
    thr = pltpu.bitcast(thr_bits, F32)

    def bisect_val(_, carry):
        lo, hi = carry
        mid = 0.5 * (lo + hi)
        ge = jnp.sum(jnp.where(aff >= mid, 1.0, 0.0), axis=1, keepdims=True) >= CAP
        return jnp.where(ge, mid, lo), jnp.where(ge, hi, mid)

    lo, hi = lax.fori_loop(0, 30, bisect_val, (0.5 * thr, jnp.maximum(2.0 * thr, 1e-30)))
    lo_ref[...] = jnp.broadcast_to(lo, lo_ref.shape)
    hi_ref[...] = jnp.broadcast_to(hi, hi_ref.shape)


def _thresholds(aff_rows):
    rows = BATCH * N_EXPERTS
    return pl.pallas_call(
        _thresh_body,
        grid=(1,),
        in_specs=[pl.BlockSpec((rows, SEQ), lambda i: (0, 0))],
        out_specs=[pl.BlockSpec((rows, LANES), lambda i: (0, 0)), pl.BlockSpec((rows, LANES), lambda i: (0, 0))],
        out_shape=[jax.ShapeDtypeStruct((rows, LANES), F32), jax.ShapeDtypeStruct((rows, LANES), F32)],
        compiler_params=_params(("arbitrary",)),
        name="route_threshold",
    )(aff_rows)


def _prefix_counts(mask):
    r = lax.broadcasted_iota(I32, (LANES, LANES), 0)
    c = lax.broadcasted_iota(I32, (LANES, LANES), 1)
    upper = jnp.where(r <= c, 1.0, 0.0).astype(BF16)
    offset = jnp.zeros((mask.shape[0], 1), F32)
    blocks = []
    for j in range(mask.shape[1] // LANES):
        blk = mask[:, j * LANES:(j + 1) * LANES]
        inc = _dot(blk.astype(BF16), upper)
        blocks.append(inc - blk + offset)
        offset = offset + inc[:, LANES - 1:LANES]
    return jnp.concatenate(blocks, axis=1)


def _route_body(aff_ref, lo_ref, hi_ref, h_ref, slot_ref, xin_ref, pos_ref, pbuf_ref):
    aff = aff_ref[0]
    above = jnp.where(aff >= hi_ref[:, 0:1], 1.0, 0.0)
    band = jnp.where(aff >= lo_ref[:, 0:1], 1.0, 0.0) - above
    need = CAP - jnp.sum(above, axis=1, keepdims=True)
    tie_rank = _prefix_counts(band)
    self = above + band * jnp.where(tie_rank < need, 1.0, 0.0)
    pos = _prefix_counts(self)
    posi = jnp.where(self > 0.5, pos.astype(I32), -1)
    pos_ref[0] = posi

    h = h_ref[0]
    slot = slot_ref[...]
    posb = jnp.where(self > 0.5, pos, -1.0).astype(BF16)
    one = jnp.ones((CAP, SEQ), BF16)
    zero = jnp.zeros((CAP, SEQ), BF16)
    for grp in range(N_EXPERTS // EXPERT_GROUP):
        for i in range(EXPERT_GROUP):
            e = grp * EXPERT_GROUP + i
            pbuf_ref[i * CAP:(i + 1) * CAP, :] = jnp.where(posb[e:e + 1, :] == slot, one, zero)
        rows = _dot(pbuf_ref[...], h)
        for i in range(EXPERT_GROUP):
            xin_ref[grp * EXPERT_GROUP + i] = rows[i * CAP:(i + 1) * CAP].astype(BF16)


def _route(aff_t, lo, hi, h2, slot_rows):
    d = D_MODEL
    return pl.pallas_call(
        _route_body,
        grid=(BATCH,),
        in_specs=[pl.BlockSpec((1, N_EXPERTS, SEQ), lambda b: (b, 0, 0)),
                  pl.BlockSpec((N_EXPERTS, LANES), lambda b: (b, 0)),
                  pl.BlockSpec((N_EXPERTS, LANES), lambda b: (b, 0)),
                  pl.BlockSpec((1, SEQ, d), lambda b: (b, 0, 0)),
                  pl.BlockSpec((CAP, SEQ), lambda b: (0, 0))],
        out_specs=[pl.BlockSpec((N_EXPERTS, CAP, d), lambda b: (0, b, 0)),
                   pl.BlockSpec((1, N_EXPERTS, SEQ), lambda b: (b, 0, 0))],
        out_shape=[jax.ShapeDtypeStruct((N_EXPERTS, BATCH * CAP, d), BF16),
                   jax.ShapeDtypeStruct((BATCH, N_EXPERTS, SEQ), I32)],
        scratch_shapes=[pltpu.VMEM((EXPERT_GROUP * CAP, SEQ), BF16)],
        compiler_params=_params(("parallel",)),
        name="route_gather",
    )(aff_t, lo, hi, h2, slot_rows)


def _expert_body(x_ref, wg_ref, wu_ref, wd_ref, o_ref, acc_ref, wgb_ref, wub_ref, wdb_ref):
    f = pl.program_id(1)

    @pl.when((pl.program_id(0) == 0) & (f == 0))
    def _():
        acc_ref[...] = jnp.zeros_like(acc_ref)

    carry = f > 0
    for m in range(BATCH * CAP // MC):
        rows = slice(m * MC, (m + 1) * MC)
        xm = x_ref[0, rows, :]
        if m == 0:
            wgb_ref[...] = wg_ref[0].astype(BF16)
        a = _dot(xm, wgb_ref[...])
        if m == 0:
            wub_ref[...] = wu_ref[0].astype(BF16)
        b = _dot(xm, wub_ref[...])
        hh = (a * jax.nn.sigmoid(a) * b).astype(BF16)
        if m == 0:
            wdb_ref[...] = wd_ref[0].astype(BF16)
        total = jnp.where(carry, acc_ref[rows, :], 0.0) + _dot(hh, wdb_ref[...])
        acc_ref[rows, :] = total
        o_ref[0, rows, :] = total.astype(BF16)


def _experts(xin, w_gate, w_up, w_down):
    d = D_MODEL
    rows = BATCH * CAP
    return pl.pallas_call(
        _expert_body,
        grid=(N_EXPERTS, D_FF // TF),
        in_specs=[pl.BlockSpec((1, rows, d), lambda e, f: (e, 0, 0)),
                  pl.BlockSpec((1, d, TF), lambda e, f: (e, 0, f)),
                  pl.BlockSpec((1, d, TF), lambda e, f: (e, 0, f)),
                  pl.BlockSpec((1, TF, d), lambda e, f: (e, f, 0))],
        out_specs=pl.BlockSpec((1, rows, d), lambda e, f: (e, 0, 0)),
        out_shape=jax.ShapeDtypeStruct((N_EXPERTS, rows, d), BF16),
        scratch_shapes=[pltpu.VMEM((rows, d), F32), pltpu.VMEM((d, TF), BF16), pltpu.VMEM((d, TF), BF16),
                        pltpu.VMEM((TF, d), BF16)],
        compiler_params=_params(("parallel", "arbitrary")),
        name="swiglu_experts",
    )(xin, w_gate, w_up, w_down)


def _scatter_body(pos_ref, aff_ref, y_ref, x_ref, g2_ref, o_ref):
    pos = pos_ref[0].T
    aff = aff_ref[0].T
    slot = lax.broadcasted_iota(I32, (SUB_S, CAP), 1)
    groups = [slice(i * SUB_S, (i + 1) * SUB_S) for i in range(TS // SUB_S)]
    onehots = [jnp.concatenate([jnp.where(pos[s, e:e + 1] == slot, aff[s, e:e + 1], 0.0).astype(BF16)
                                for e in range(N_EXPERTS)], axis=1) for s in groups]
    y = y_ref[...].reshape(N_EXPERTS * CAP, D_MODEL)
    mixed = [_dot(oh, y) for oh in onehots]
    for s, m in zip(groups, mixed):
        o_ref[0, s, :] = x_ref[0, s, :] + g2_ref[0] * m


def _scatter(pos, aff_t, y, xn, mods3):
    d = D_MODEL
    return pl.pallas_call(
        _scatter_body,
        grid=(BATCH, SEQ // TS),
        in_specs=[pl.BlockSpec((1, N_EXPERTS, TS), lambda b, i: (b, 0, i)),
                  pl.BlockSpec((1, N_EXPERTS, TS), lambda b, i: (b, 0, i)),
                  pl.BlockSpec((N_EXPERTS, CAP, d), lambda b, i: (0, b, 0)),
                  pl.BlockSpec((1, TS, d), lambda b, i: (b, i, 0)),
                  pl.BlockSpec((1, 1, d), lambda b, i: (b, 0, 5))],
        out_specs=pl.BlockSpec((1, TS, d), lambda b, i: (b, i, 0)),
        out_shape=jax.ShapeDtypeStruct((BATCH, SEQ, d), F32),
        compiler_params=_params(("parallel", "parallel")),
        name="scatter_residual",
    )(pos, aff_t, y, xn, mods3)


def _rope_tables():
    rows = SEQ // GRID_W
    row = np.repeat(np.arange(rows, dtype=np.float32), GRID_W)
    col = np.tile(np.arange(GRID_W, dtype=np.float32), rows)
    inv = (ROPE_BASE ** (-np.arange(0, AXIS_ROT, 2, dtype=np.float32) / AXIS_ROT)).astype(np.float32)
    ang = np.concatenate([row[:, None] * inv, col[:, None] * inv], axis=-1).astype(np.float64)
    cos = np.repeat(np.cos(ang), 2, axis=-1)
    sin = np.stack([-np.sin(ang), np.sin(ang)], axis=-1).reshape(SEQ, HEAD_DIM)
    reps = LANES // HEAD_DIM
    return jnp.asarray(np.tile(cos, (1, reps)), F32), jnp.asarray(np.tile(sin, (1, reps)), F32)


def _dft_tables():
    idx = np.arange(HALF, dtype=np.int64)
    t2p1 = 2 * idx + 1

    def cos_sin(f):
        ang = ((f[:, None] * t2p1[None, :]) % (2 * N_FFT)) * (math.pi / N_FFT)
        return np.cos(ang), np.sin(ang)

    ce, se = cos_sin(2 * idx)
    co, so = cos_sin(2 * idx + 1)
    t1 = jnp.asarray(np.concatenate([ce, so], axis=0), F32).astype(BF16)
    t2 = jnp.asarray(np.concatenate([co, se], axis=0), F32).astype(BF16)
    return t1, t2, t1.T, t2.T


def _phase_tables():
    idx = np.arange(HALF, dtype=np.float64)
    w = np.full((HALF,), 2.0 / N_FFT)
    we = w.copy()
    we[0] = 1.0 / N_FFT
    pe = (math.pi / N_FFT) * (2.0 * idx)
    po = (math.pi / N_FFT) * (2.0 * idx + 1.0)
    rot = np.stack([we * np.cos(pe), we * np.sin(pe), w * np.cos(po), w * np.sin(po)])
    return jnp.asarray(np.broadcast_to(rot[:, :, None], (4, HALF, CT)), F32)


def _fold_rows(a):
    return np.concatenate([a[:HALF], a[HALF:][::-1]], axis=0)


def _filter_features():
    t = np.linspace(0.0, 1.0, SEQ, dtype=np.float32).astype(np.float64)[:, None]
    w = 2.0 * math.pi * np.arange(SEQ, dtype=np.float64)[:, None] / SEQ
    fr = np.linspace(1e-4, FILTER_BANDS - 1, FILTER_BANDS, dtype=np.float32).astype(np.float64)[None, :]
    feat = np.concatenate([t, np.cos(fr * w), -np.sin(fr * w)], axis=-1)
    feat = np.pad(feat, ((0, 0), (0, FILTER_HIDDEN - FILTER_EMB)))
    min_decay = math.log(DECAY_TARGET) / SLOW_DECAY_PCT
    max_decay = math.log(DECAY_TARGET) / FAST_DECAY_PCT
    deltas = np.linspace(min_decay, max_decay, HYENA_W, dtype=np.float32).astype(np.float64)
    decay = np.exp(-t * np.abs(deltas))
    return jnp.asarray(_fold_rows(feat).T, F32), jnp.asarray(_fold_rows(decay), F32)


def _attn_bias():
    qi = np.arange(BLOCK)[:, None]
    kj = np.arange(3 * BLOCK)[None, :]
    band = np.abs(kj - BLOCK - qi) <= WINDOW
    first = band & (kj >= BLOCK)
    last = band & (kj < 2 * BLOCK)
    return jnp.asarray(np.where(np.stack([first, band, last]), 0.0, NEG), F32)


def _pair_heads(w, axis):
    heads = [lax.slice_in_dim(w, h * HEAD_DIM, (h + 1) * HEAD_DIM, axis=axis) for h in range(N_HEADS)]
    return jnp.concatenate([heads[j + GROUP * half] for j in range(GROUP) for half in range(N_KV_HEADS)], axis=axis)


def kernel(x, c, ctx, c_ctx, ada_w, ada_b, norm1, norm2, w_in, conv_w, conv_b, filt_w1, filt_b1, filt_w2, filt_b2,
           filt_w3, filt_b3, filt_freq, filt_out, hyena_bias, q_norm, k_norm, attn_sink, w_branch_a, w_branch_b,
           w_out, router, w_gate, w_up, w_down):
    d = D_MODEL
    assert ada_w.shape[0] == 1, "only the single-layer configuration is implemented"
    l = 0
    cos_t, sin_t = _rope_tables()
    t1, t2, t1t, t2t = _dft_tables()
    rot = _phase_tables()
    feat, decay = _filter_features()
    bias = _attn_bias()
    gmat = jnp.asarray(np.kron(np.eye(2 * LANES // HEAD_DIM), np.full((HEAD_DIM, HEAD_DIM), 1.0 / HEAD_DIM)), BF16)
    c16 = jnp.concatenate([c, c_ctx[None, :], jnp.zeros((MOD_ROWS - BATCH - 1, d), F32)], axis=0)

    mods3 = _ada(c16, ada_w[l], ada_b[l][None, :])
    n1 = norm1[l][None, :]
    w_packed = _pack_in_weights(w_in[l])
    gk = jnp.tile(k_norm[l], N_KV_HEADS)[None, :]
    w1_rows = jnp.pad(filt_w1[l], ((0, FILTER_HIDDEN - FILTER_EMB), (0, 0)))
    mlp_w = jnp.swapaxes(jnp.stack([w1_rows, filt_w2[l], filt_w3[l]]), 1, 2)
    mlp_cols = jnp.stack([filt_b1[l], filt_b2[l], filt_b3[l], filt_freq[l]], axis=1)
    kf, kn = _filters(feat, mlp_w, mlp_cols, filt_out[l], decay, rot, t1, t2)
    kc, vc = _ctx_proj(ctx, mods3, n1, w_packed, gk, gmat)
    zh, q, k, v, gates = _inproj(x, mods3, n1, w_packed, jnp.tile(q_norm[l], N_HEADS)[None, :], gk, gmat, cos_t, sin_t)
    ya = _hyena(zh, conv_w[l], conv_b[l][None, :], hyena_bias[l], kf, kn, t1, t2, t1t, t2t)
    yb = _attention(attn_sink[l] * LOG2E, q, k, v, kc, vc, bias)
    xn, h2, aff_t = _merge(ya, yb, gates, x, mods3, norm2[l][None, :], w_branch_a[l].astype(BF16),
                           _pair_heads(w_branch_b[l], 0).astype(BF16), w_out[l].astype(BF16), router[l].T.astype(BF16))
    lo, hi = _thresholds(aff_t.reshape(BATCH * N_EXPERTS, SEQ))
    slot_rows = jnp.asarray(np.broadcast_to(np.arange(CAP)[:, None], (CAP, SEQ)), BF16)
    xin, pos = _route(aff_t, lo, hi, h2, slot_rows)
    y = _experts(xin, w_gate[l], w_up[l], w_down[l])
    return _scatter(pos, aff_t, y, xn, mods3)
```

```python
import math

import numpy as np
import jax
import jax.numpy as jnp
from jax import lax
from jax.experimental import pallas as pl
from jax.experimental.pallas import tpu as pltpu

F32 = jnp.float32
BF16 = jnp.bfloat16
I32 = jnp.int32
HIGHEST = lax.Precision.HIGHEST

D_MODEL = 1024
BATCH = 8
SEQ = 2048
GRID_W = 64
CTX_LEN = 256
N_HEADS = 8
N_KV_HEADS = 2
HEAD_DIM = 64
GROUP = N_HEADS // N_KV_HEADS
ATTN_W = N_HEADS * HEAD_DIM
KV_W = N_KV_HEADS * HEAD_DIM
WINDOW = 128
BLOCK = 128
HYENA_W = D_MODEL // 2
HYENA_ORDER = 2
FILTER_BANDS = 16
FILTER_EMB = 1 + 2 * FILTER_BANDS
FILTER_HIDDEN = 64
DECAY_TARGET = 1e-2
FAST_DECAY_PCT = 0.3
SLOW_DECAY_PCT = 1.5
ROPE_BASE = 10000.0
AXIS_ROT = HEAD_DIM // 2
N_EXPERTS = 16
EC_CAPACITY = 2
D_FF = 2048
EPS = 1e-6
NEG = -1e30
LOG2E = math.log2(math.e)
F32_TINY = float(np.finfo(np.float32).tiny)

OFF_Q = 3 * HYENA_W
OFF_K = OFF_Q + ATTN_W
OFF_V = OFF_K + KV_W
OFF_G = OFF_V + KV_W

CAP = EC_CAPACITY * SEQ // N_EXPERTS
N_FFT = 2 * SEQ
HALF = SEQ // 2
MOD_ROWS = 16
LANES = 128

TM_IN = 1024
SUB_IN = 256
TM_MERGE = 1024
SUB_MERGE = 512
CT = 256
FC = 512
RB = 256
TF = 512
MC = 512
TS = 1024
SUB_S = 512
CTX_STEP = 4
EXPERT_GROUP = 4
ATTN_UNROLL = 8
VMEM_LIMIT = 56 * 1024 * 1024


def _dot(a, b, precision=None):
    return jnp.dot(a, b, preferred_element_type=F32, precision=precision)


def _dot_nt(a, b, precision=None):
    return lax.dot_general(a, b, (((1,), (1,)), ((), ())), preferred_element_type=F32, precision=precision)


def _params(sem, vmem=VMEM_LIMIT):
    return pltpu.CompilerParams(dimension_semantics=sem, vmem_limit_bytes=vmem)


def _rms_mod(x, g, sc, sh):
    ms = jnp.mean(x * x, axis=-1, keepdims=True)
    return (x * lax.rsqrt(ms + EPS) * g) * (1.0 + sc) + sh


def _head_norm_rope(z, g, gmat, cos, sin, scale):
    ms = _dot((z * z).astype(BF16), gmat)
    y = z * lax.rsqrt(ms + EPS) * g
    if cos is not None:
        slabs = []
        for s in range(z.shape[1] // LANES):
            ys = y[:, s * LANES:(s + 1) * LANES]
            lane = lax.broadcasted_iota(I32, ys.shape, 1)
            nxt = pltpu.roll(ys, LANES - 1, axis=1)
            prv = pltpu.roll(ys, 1, axis=1)
            slabs.append(ys * cos + jnp.where((lane & 1) == 0, nxt, prv) * sin)
        y = slabs[0] if len(slabs) == 1 else jnp.concatenate(slabs, axis=1)
    return y * scale


def _split_bf16(x):
    hi = x.astype(BF16)
    return hi, (x - hi.astype(F32)).astype(BF16)


def _ada_body(c_ref, w_ref, b_ref, o_ref):
    c = c_ref[...]
    s_hi, s_lo = _split_bf16(c * jax.nn.sigmoid(c))
    w_hi, w_lo = _split_bf16(w_ref[...])
    o_ref[:, 0, :] = _dot(s_hi, w_hi) + _dot(s_lo, w_hi) + _dot(s_hi, w_lo) + b_ref[...]


def _ada(c16, w, b):
    d = D_MODEL
    return pl.pallas_call(
        _ada_body,
        grid=(6,),
        in_specs=[pl.BlockSpec((MOD_ROWS, d), lambda j: (0, 0)),
                  pl.BlockSpec((d, d), lambda j: (0, j)),
                  pl.BlockSpec((1, d), lambda j: (0, j))],
        out_specs=pl.BlockSpec((MOD_ROWS, 1, d), lambda j: (0, 0, j)),
        out_shape=jax.ShapeDtypeStruct((MOD_ROWS, 1, 6 * d), F32),
        compiler_params=_params(("parallel",)),
        name="ada_mod",
    )(c16, w, b)


def _sign_rows(n):
    lane = lax.broadcasted_iota(I32, (8, n), 1)
    sub = lax.broadcasted_iota(I32, (8, n), 0)
    sg = jnp.where((lane & 1) == 0, 1.0, -1.0)
    return jnp.where(sub == 0, sg, 0.0).astype(BF16)


def _filt_body(feat_ref, w_ref, col_ref, fof_ref, fob_ref, dec_ref, rot_ref, t1_ref, t2_ref, kf_ref, kn_ref, hs_ref):
    @pl.when((pl.program_id(0) == 0) & (pl.program_id(1) == 0))
    def _():
        cols = col_ref[...]
        fq = cols[:, 3:4]
        h = feat_ref[...]
        for layer in range(3):
            h = jnp.sin(fq * (_dot(w_ref[layer], h, HIGHEST) + cols[:, layer:layer + 1]))
        hi = h.astype(BF16).astype(F32)
        stacked = jnp.concatenate([hi, h - hi, hi, jnp.zeros_like(hi)], axis=0)
        hs_ref[...] = stacked.T.astype(BF16)

    def taps(fo_ref):
        f_hi, f_lo = _split_bf16(fo_ref[...])
        return _dot(hs_ref[...], jnp.concatenate([f_hi, f_hi, f_lo, jnp.zeros_like(f_hi)], axis=0))

    dec = dec_ref[...]
    hf = taps(fof_ref) * dec
    hb = taps(fob_ref) * dec
    row = lax.broadcasted_iota(I32, hf.shape, 0)
    hb = jnp.where(row == 0, 0.0, hb)
    a = hf + hb
    b = hf - hb
    pa = (a[:HALF] + a[HALF:]).astype(BF16)
    ma = (a[:HALF] - a[HALF:]).astype(BF16)
    pb = (b[:HALF] + b[HALF:]).astype(BF16)
    mb = (b[:HALF] - b[HALF:]).astype(BF16)
    t1 = t1_ref[...]
    t2 = t2_ref[...]
    ce, se, co, so = rot_ref[0], rot_ref[1], rot_ref[2], rot_ref[3]
    a1 = _dot(t1, pa)
    a2 = _dot(t2, ma)
    kf_ref[0, 0] = a1[:HALF] * ce + a2[HALF:] * se
    kf_ref[0, 2] = a2[:HALF] * co + a1[HALF:] * so
    b1 = _dot(t1, pb)
    b2 = _dot(t2, mb)
    kf_ref[0, 1] = b2[HALF:] * ce - b1[:HALF] * se
    kf_ref[0, 3] = b1[HALF:] * co - b2[:HALF] * so
    kn_ref[0] = _dot(_sign_rows(HALF), ma)[0:1] * (1.0 / N_FFT)


def _filters(feat, mlp_w, mlp_cols, fout, decay, rot, t1, t2):
    nct = HYENA_W // CT
    full = lambda shape: pl.BlockSpec(shape, lambda o, c: (0,) * len(shape))
    return pl.pallas_call(
        _filt_body,
        grid=(HYENA_ORDER, nct),
        in_specs=[full((FILTER_HIDDEN, SEQ)), full((3, FILTER_HIDDEN, FILTER_HIDDEN)), full((FILTER_HIDDEN, 4)),
                  pl.BlockSpec((FILTER_HIDDEN, CT), lambda o, c: (0, (o * 2 + 0) * nct + c)),
                  pl.BlockSpec((FILTER_HIDDEN, CT), lambda o, c: (0, (o * 2 + 1) * nct + c)),
                  pl.BlockSpec((SEQ, CT), lambda o, c: (0, c)),
                  full((4, HALF, CT)),
                  pl.BlockSpec((SEQ, HALF), lambda o, c: (0, 0), pipeline_mode=pl.Buffered(1)),
                  pl.BlockSpec((SEQ, HALF), lambda o, c: (0, 0), pipeline_mode=pl.Buffered(1))],
        out_specs=[pl.BlockSpec((1, 4, HALF, CT), lambda o, c: (o, 0, 0, c)),
                   pl.BlockSpec((1, 1, CT), lambda o, c: (o, 0, c))],
        out_shape=[jax.ShapeDtypeStruct((HYENA_ORDER, 4, HALF, HYENA_W), F32),
                   jax.ShapeDtypeStruct((HYENA_ORDER, 1, HYENA_W), F32)],
        scratch_shapes=[pltpu.VMEM((SEQ, 4 * FILTER_HIDDEN), BF16)],
        compiler_params=_params(("arbitrary", "arbitrary")),
        name="hyena_filters",
    )(feat, mlp_w, mlp_cols, fout, fout, decay, rot, t1, t2)


def _pair_head_lanes(z):
    slabs = [z[:, s * LANES:(s + 1) * LANES] for s in range(ATTN_W // LANES)]
    swapped = [pltpu.roll(sl, HEAD_DIM, axis=1) for sl in slabs]
    low = lax.broadcasted_iota(I32, slabs[0].shape, 1) < HEAD_DIM
    out = []
    for j in range(GROUP):
        first, second = j, j + GROUP
        lo_src = slabs[first // 2] if first % 2 == 0 else swapped[first // 2]
        hi_src = slabs[second // 2] if second % 2 == 1 else swapped[second // 2]
        out.append(jnp.where(low, lo_src, hi_src))
    return jnp.concatenate(out, axis=1)


def _inproj_body(x_ref, sc_ref, sh_ref, n1_ref, wh_ref, wq_ref, wkv_ref, wg_ref, gq_ref, gk_ref, gmat_ref,
                 cos_ref, sin_ref, zh_ref, q_ref, k_ref, v_ref, gate_ref):
    groups = [slice(i * SUB_IN, (i + 1) * SUB_IN) for i in range(TM_IN // SUB_IN)]
    hx = [_rms_mod(x_ref[0, s, :], n1_ref[...], sc_ref[0], sh_ref[0]).astype(BF16) for s in groups]
    for s, h in zip(groups, hx):
        zh_ref[0, s, :] = _dot(h, wh_ref[...]).astype(BF16)
    pair = 2 * LANES
    zq = [_pair_head_lanes(_dot(h, wq_ref[...])) for h in hx]
    for s, z in zip(groups, zq):
        for c in range(ATTN_W // pair):
            sl = slice(c * pair, (c + 1) * pair)
            q_ref[0, s, sl] = _head_norm_rope(z[:, sl], gq_ref[:, sl], gmat_ref[...], cos_ref[s, :], sin_ref[s, :],
                                              LOG2E * HEAD_DIM ** -0.5).astype(BF16)
    zkv = [_dot(h, wkv_ref[...]) for h in hx]
    for s, z in zip(groups, zkv):
        k_ref[0, s, :] = _head_norm_rope(z[:, :KV_W], gk_ref[...], gmat_ref[0:KV_W, 0:KV_W], cos_ref[s, :],
                                         sin_ref[s, :], 1.0).astype(BF16)
        v_ref[0, s, :] = z[:, KV_W:].astype(BF16)
    for s, h in zip(groups, hx):
        gate_ref[0, s, :] = jax.nn.sigmoid(_dot(h, wg_ref[...])).astype(BF16)


def _pack_in_weights(w):
    wb = w.astype(BF16)
    return jnp.concatenate([wb[:, :OFF_K], wb[:, OFF_G:], wb[:, OFF_K:OFF_G]], axis=1)


W_OFF_H = 0
W_OFF_Q = OFF_Q
W_OFF_G = OFF_Q + ATTN_W
W_OFF_KV = W_OFF_G + 2 * D_MODEL


def _inproj(x, mods3, norm1, w_packed, gq, gk, gmat, cos_t, sin_t):
    d = D_MODEL
    nt = SEQ // TM_IN
    const = lambda shape: pl.BlockSpec(shape, lambda b, i: (0,) * len(shape))

    def wcol(width, off):
        assert off % width == 0
        return pl.BlockSpec((d, width), lambda b, i: (0, off // width))

    tok = lambda w: pl.BlockSpec((1, TM_IN, w), lambda b, i: (b, i, 0))
    return pl.pallas_call(
        _inproj_body,
        grid=(BATCH, nt),
        in_specs=[tok(d),
                  pl.BlockSpec((1, 1, d), lambda b, i: (b, 0, 1)),
                  pl.BlockSpec((1, 1, d), lambda b, i: (b, 0, 0)),
                  const((1, d)), wcol(OFF_Q, W_OFF_H), wcol(ATTN_W, W_OFF_Q), wcol(2 * KV_W, W_OFF_KV),
                  wcol(2 * d, W_OFF_G), const((1, ATTN_W)), const((1, KV_W)), const((2 * LANES, 2 * LANES)),
                  pl.BlockSpec((TM_IN, LANES), lambda b, i: (i, 0)),
                  pl.BlockSpec((TM_IN, LANES), lambda b, i: (i, 0))],
        out_specs=[tok(OFF_Q), tok(ATTN_W), tok(KV_W), tok(KV_W), tok(2 * d)],
        out_shape=[jax.ShapeDtypeStruct((BATCH, SEQ, OFF_Q), BF16),
                   jax.ShapeDtypeStruct((BATCH, SEQ, ATTN_W), BF16),
                   jax.ShapeDtypeStruct((BATCH, SEQ, KV_W), BF16),
                   jax.ShapeDtypeStruct((BATCH, SEQ, KV_W), BF16),
                   jax.ShapeDtypeStruct((BATCH, SEQ, 2 * d), BF16)],
        compiler_params=_params(("parallel", "parallel")),
        name="in_proj",
    )(x, mods3, mods3, norm1, w_packed, w_packed, w_packed, w_packed, gq, gk, gmat, cos_t, sin_t)


def _ctx_body(c_ref, sc_ref, sh_ref, n1_ref, wkv_ref, gk_ref, gmat_ref, kc_ref, vc_ref):
    for i in range(CTX_STEP):
        hc = _rms_mod(c_ref[i], n1_ref[...], sc_ref[0], sh_ref[0]).astype(BF16)
        z = _dot(hc, wkv_ref[...])
        kc_ref[i] = _head_norm_rope(z[:, :KV_W], gk_ref[...], gmat_ref[0:KV_W, 0:KV_W], None, None, 1.0).astype(BF16)
        vc_ref[i] = z[:, KV_W:].astype(BF16)


def _ctx_proj(ctx, mods3, norm1, w_packed, gk, gmat):
    d = D_MODEL
    const = lambda shape: pl.BlockSpec(shape, lambda b: (0,) * len(shape))
    return pl.pallas_call(
        _ctx_body,
        grid=(BATCH // CTX_STEP,),
        in_specs=[pl.BlockSpec((CTX_STEP, CTX_LEN, d), lambda b: (b, 0, 0)),
                  pl.BlockSpec((1, 1, d), lambda b: (BATCH, 0, 1)),
                  pl.BlockSpec((1, 1, d), lambda b: (BATCH, 0, 0)),
                  const((1, d)), pl.BlockSpec((d, 2 * KV_W), lambda b: (0, W_OFF_KV // (2 * KV_W))),
                  const((1, KV_W)), const((2 * LANES, 2 * LANES))],
        out_specs=[pl.BlockSpec((CTX_STEP, CTX_LEN, KV_W), lambda b: (b, 0, 0)),
                   pl.BlockSpec((CTX_STEP, CTX_LEN, KV_W), lambda b: (b, 0, 0))],
        out_shape=[jax.ShapeDtypeStruct((BATCH, CTX_LEN, KV_W), BF16),
                   jax.ShapeDtypeStruct((BATCH, CTX_LEN, KV_W), BF16)],
        compiler_params=_params(("parallel",)),
        name="ctx_proj",
    )(ctx, mods3, mods3, norm1, w_packed, gk, gmat)


def _hyena_body(zv_ref, z1_ref, z2_ref, cwv_ref, cw1_ref, cw2_ref, cbv_ref, cb1_ref, cb2_ref, hb_ref, kf_ref,
                kn_ref, t1_ref, t2_ref, t1t_ref, t2t_ref, o_ref, lo_ref, hi_ref, g0lo_ref, g0hi_ref, g1lo_ref, g1hi_ref,
                p_ref, m_ref, za_ref, zb_ref):
    row = lax.broadcasted_iota(I32, (HALF, CT), 0)
    rr = lax.broadcasted_iota(I32, (RB, RB), 0)
    cc = lax.broadcasted_iota(I32, (RB, RB), 1)
    flip = jnp.where(rr + cc == RB - 1, 1.0, 0.0).astype(BF16)
    nrb = HALF // RB

    def reverse_upper_half(z_ref, hi_out):
        for j in range(nrb):
            hi_out[j * RB:(j + 1) * RB, :] = _dot(flip, z_ref[0, SEQ - RB * (j + 1):SEQ - RB * j, :])

    def folded_short_conv(z_ref, w_ref, b_ref, lo_out, hi_out):
        zlo = z_ref[0, 0:HALF, :].astype(F32)
        zhi = hi_out[...]
        w0, w1, w2 = w_ref[0:1, :], w_ref[1:2, :], w_ref[2:3, :]
        first, last = row == 0, row == HALF - 1
        lo_prev = jnp.where(first, 0.0, pltpu.roll(zlo, 1, axis=0))
        lo_next = jnp.where(last, zhi[HALF - 1:HALF, :], pltpu.roll(zlo, HALF - 1, axis=0))
        hi_prev = jnp.where(first, 0.0, pltpu.roll(zhi, 1, axis=0))
        hi_next = jnp.where(last, zlo[HALF - 1:HALF, :], pltpu.roll(zhi, HALF - 1, axis=0))
        lo_out[...] = lo_prev * w0 + zlo * w1 + lo_next * w2 + b_ref[...]
        hi_out[...] = hi_next * w0 + zhi * w1 + hi_prev * w2 + b_ref[...]

    gates = ((g0lo_ref, g0hi_ref), (g1lo_ref, g1hi_ref))
    reverse_upper_half(zv_ref, hi_ref)
    reverse_upper_half(z1_ref, g0hi_ref)
    reverse_upper_half(z2_ref, g1hi_ref)
    folded_short_conv(zv_ref, cwv_ref, cbv_ref, lo_ref, hi_ref)
    folded_short_conv(z1_ref, cw1_ref, cb1_ref, g0lo_ref, g0hi_ref)
    folded_short_conv(z2_ref, cw2_ref, cb2_ref, g1lo_ref, g1hi_ref)
    sign8 = _sign_rows(HALF)
    odd = (lax.broadcasted_iota(I32, (FC, CT), 0) & 1) == 1
    for o, (glo_ref, ghi_ref) in enumerate(gates):
        p_ref[...] = (lo_ref[...] + hi_ref[...]).astype(BF16)
        m_ref[...] = (lo_ref[...] - hi_ref[...]).astype(BF16)
        pv = p_ref[...]
        mv = m_ref[...]
        for c in range(HALF // FC):
            ev = slice(c * FC, (c + 1) * FC)
            od = slice(HALF + c * FC, HALF + (c + 1) * FC)
            xce = _dot(t1_ref[ev, :], pv)
            xse = _dot(t2_ref[od, :], mv)
            kce, kse = kf_ref[o, 0, ev, :], kf_ref[o, 1, ev, :]
            za_ref[ev, :] = (xce * kce - xse * kse).astype(BF16)
            zb_ref[od, :] = (xce * kse + xse * kce).astype(BF16)
            xco = _dot(t2_ref[ev, :], mv)
            xso = _dot(t1_ref[od, :], pv)
            kco, kso = kf_ref[o, 2, ev, :], kf_ref[o, 3, ev, :]
            za_ref[od, :] = (xco * kso + xso * kco).astype(BF16)
            zb_ref[ev, :] = (xco * kco - xso * kso).astype(BF16)
        zn = _dot(sign8, mv)[0:1] * kn_ref[o]
        bias = hb_ref[o:o + 1, :]
        za = za_ref[...]
        zb = zb_ref[...]
        for c in range(HALF // FC):
            rs = slice(c * FC, (c + 1) * FC)
            half_p = _dot(t1t_ref[rs, :], za)
            half_m = _dot(t2t_ref[rs, :], zb) + jnp.where(odd, -zn, zn)
            lo_ref[rs, :] = glo_ref[rs, :] * (half_p + half_m + bias * lo_ref[rs, :])
            hi_ref[rs, :] = ghi_ref[rs, :] * (half_p - half_m + bias * hi_ref[rs, :])
    o_ref[0, 0:HALF, :] = lo_ref[...].astype(BF16)
    for j in range(nrb):
        o_ref[0, SEQ - RB * (j + 1):SEQ - RB * j, :] = _dot(
            flip, hi_ref[j * RB:(j + 1) * RB, :].astype(BF16)).astype(BF16)


def _hyena(zh, conv_w, conv_b, hbias, kf, kn, t1, t2, t1t, t2t):
    nct = HYENA_W // CT
    zspec = lambda k: pl.BlockSpec((1, SEQ, CT), lambda c, b: (b, 0, k * nct + c))
    wspec = lambda k: pl.BlockSpec((3, CT), lambda c, b: (0, k * nct + c))
    bspec = lambda k: pl.BlockSpec((1, CT), lambda c, b: (0, k * nct + c))
    table = lambda shape: pl.BlockSpec(shape, lambda c, b: (0, 0), pipeline_mode=pl.Buffered(1))
    half_f32 = pltpu.VMEM((HALF, CT), F32)
    return pl.pallas_call(
        _hyena_body,
        grid=(nct, BATCH),
        in_specs=[zspec(0), zspec(1), zspec(2), wspec(0), wspec(1), wspec(2), bspec(0), bspec(1), bspec(2),
                  pl.BlockSpec((HYENA_ORDER, CT), lambda c, b: (0, c)),
                  pl.BlockSpec((HYENA_ORDER, 4, HALF, CT), lambda c, b: (0, 0, 0, c)),
                  pl.BlockSpec((HYENA_ORDER, 1, CT), lambda c, b: (0, 0, c)),
                  table((SEQ, HALF)), table((SEQ, HALF)), table((HALF, SEQ)), table((HALF, SEQ))],
        out_specs=pl.BlockSpec((1, SEQ, CT), lambda c, b: (b, 0, c)),
        out_shape=jax.ShapeDtypeStruct((BATCH, SEQ, HYENA_W), BF16),
        scratch_shapes=[half_f32] * 6 + [
            pltpu.VMEM((HALF, CT), BF16), pltpu.VMEM((HALF, CT), BF16),
            pltpu.VMEM((SEQ, CT), BF16), pltpu.VMEM((SEQ, CT), BF16)],
        compiler_params=_params(("parallel", "parallel")),
        name="hyena_conv",
    )(zh, zh, zh, conv_w, conv_w, conv_w, conv_b, conv_b, conv_b, hbias, kf, kn, t1, t2, t1t, t2t)


def _attn_body(sink_ref, q_ref, k_ref, v_ref, kc_ref, vc_ref, bias_ref, o_ref, kp_ref, vlo_ref, vhi_ref):
    nb = SEQ // BLOCK
    lane = lax.broadcasted_iota(I32, (BLOCK, LANES), 1)
    low = lane < HEAD_DIM
    mask_lo = jnp.where(low, 1.0, 0.0).astype(BF16)
    mask_hi = jnp.where(low, 0.0, 1.0).astype(BF16)

    def with_ones(v):
        lane_v = lax.broadcasted_iota(I32, v.shape, 1) < HEAD_DIM
        one = jnp.ones_like(v)
        return jnp.where(lane_v, v, one), jnp.where(lane_v, one, v)

    zpad = jnp.zeros((BLOCK, KV_W), BF16)
    kp_ref[0:BLOCK] = zpad
    kp_ref[BLOCK:BLOCK + SEQ] = k_ref[0]
    kp_ref[BLOCK + SEQ:] = zpad
    v_lo, v_hi = with_ones(v_ref[0])
    for ref, val in ((vlo_ref, v_lo), (vhi_ref, v_hi)):
        ref[0:BLOCK] = zpad
        ref[BLOCK:BLOCK + SEQ] = val
        ref[BLOCK + SEQ:] = zpad
    kc = kc_ref[0]
    vc_pair = with_ones(vc_ref[0])

    def scores_of(n):
        r = pl.multiple_of(n * BLOCK, BLOCK)
        kw = kp_ref[pl.ds(r, 3 * BLOCK), :]
        bias = bias_ref[jnp.where(n == 0, 0, jnp.where(n == nb - 1, 2, 1))]
        scores = []
        for j in range(GROUP):
            qs = q_ref[0, pl.ds(r, BLOCK), j * LANES:(j + 1) * LANES]
            for msk in (mask_lo, mask_hi):
                qm = qs * msk
                scores.append((_dot_nt(qm, kw) + bias, _dot_nt(qm, kc)))
        return scores

    def finish(n, scores):
        r = pl.multiple_of(n * BLOCK, BLOCK)
        vw_pair = (vlo_ref[pl.ds(r, 3 * BLOCK), :], vhi_ref[pl.ds(r, 3 * BLOCK), :])
        probs = []
        for idx, (sw, sc) in enumerate(scores):
            snk = sink_ref[idx // 2 + GROUP * (idx % 2)]
            m = jnp.maximum(jnp.maximum(jnp.max(sw, axis=-1, keepdims=True),
                                        jnp.max(sc, axis=-1, keepdims=True)), snk)
            probs.append((jnp.exp2(sw - m).astype(BF16), jnp.exp2(sc - m).astype(BF16), jnp.exp2(snk - m)))
        outs = []
        for idx, (pw, pc, psink) in enumerate(probs):
            acc = _dot(pw, vw_pair[idx % 2]) + _dot(pc, vc_pair[idx % 2])
            den = pltpu.roll(acc, HEAD_DIM, axis=1) + psink
            outs.append(acc / den)
        for j in range(GROUP):
            o_ref[0, pl.ds(r, BLOCK), j * LANES:(j + 1) * LANES] = jnp.where(
                low, outs[2 * j], outs[2 * j + 1]).astype(BF16)

    def block_group(g, carry):
        first = g * ATTN_UNROLL
        pending = scores_of(first)
        for i in range(ATTN_UNROLL):
            nxt = scores_of(first + i + 1) if i + 1 < ATTN_UNROLL else None
            finish(first + i, pending)
            pending = nxt
        return carry

    lax.fori_loop(0, nb // ATTN_UNROLL, block_group, 0)


def _attention(sink, q, k, v, kc, vc, bias):
    per_b = lambda n, w: pl.BlockSpec((1, n, w), lambda b: (b, 0, 0))
    return pl.pallas_call(
        _attn_body,
        grid=(BATCH,),
        in_specs=[pl.BlockSpec(memory_space=pltpu.SMEM),
                  per_b(SEQ, ATTN_W), per_b(SEQ, KV_W), per_b(SEQ, KV_W), per_b(CTX_LEN, KV_W), per_b(CTX_LEN, KV_W),
                  pl.BlockSpec((3, BLOCK, 3 * BLOCK), lambda b: (0, 0, 0))],
        out_specs=per_b(SEQ, ATTN_W),
        out_shape=jax.ShapeDtypeStruct((BATCH, SEQ, ATTN_W), BF16),
        scratch_shapes=[pltpu.VMEM((SEQ + 2 * BLOCK, KV_W), BF16)] * 3,
        compiler_params=_params(("parallel",)),
        name="window_attn",
    )(sink, q, k, v, kc, vc, bias)


def _merge_body(ya_ref, yb_ref, gt_ref, x_ref, g1_ref, sc2_ref, sh2_ref, n2_ref, wa_ref, wb_ref, wo_ref, rt_ref,
                xn_ref, h2_ref, aff_ref):
    d = D_MODEL
    groups = [slice(i * SUB_MERGE, (i + 1) * SUB_MERGE) for i in range(TM_MERGE // SUB_MERGE)]
    branch = [(_dot(ya_ref[0, s, :], wa_ref[...]), _dot(yb_ref[0, s, :], wb_ref[...])) for s in groups]
    mixed = [(gt_ref[0, s, :d].astype(F32) * ua + gt_ref[0, s, d:].astype(F32) * ub).astype(BF16)
             for s, (ua, ub) in zip(groups, branch)]
    proj = [_dot(u, wo_ref[...]) for u in mixed]
    h2_rows = []
    for s, m in zip(groups, proj):
        xn = x_ref[0, s, :] + g1_ref[0] * m
        xn_ref[0, s, :] = xn
        h2 = _rms_mod(xn, n2_ref[...], sc2_ref[0], sh2_ref[0]).astype(BF16)
        h2_ref[0, s, :] = h2
        h2_rows.append(h2)
    logits = [_dot_nt(rt_ref[...], h2) for h2 in h2_rows]
    for s, lt in zip(groups, logits):
        e = jnp.exp(lt - jnp.max(lt, axis=0, keepdims=True))
        aff_ref[0, :, s] = e / jnp.sum(e, axis=0, keepdims=True)


def _merge(ya, yb, gates, x, mods3, norm2, wa, wb, wo, router_t):
    d = D_MODEL
    nt = SEQ // TM_MERGE
    const = lambda shape: pl.BlockSpec(shape, lambda b, i: (0,) * len(shape))
    tok = lambda w: pl.BlockSpec((1, TM_MERGE, w), lambda b, i: (b, i, 0))
    mod = lambda k: pl.BlockSpec((1, 1, d), lambda b, i: (b, 0, k))
    return pl.pallas_call(
        _merge_body,
        grid=(BATCH, nt),
        in_specs=[tok(HYENA_W), tok(ATTN_W), tok(2 * d), tok(d), mod(2), mod(4), mod(3), const((1, d)),
                  const((HYENA_W, d)), const((ATTN_W, d)), const((d, d)), const((N_EXPERTS, d))],
        out_specs=[tok(d), tok(d), pl.BlockSpec((1, N_EXPERTS, TM_MERGE), lambda b, i: (b, 0, i))],
        out_shape=[jax.ShapeDtypeStruct((BATCH, SEQ, d), F32),
                   jax.ShapeDtypeStruct((BATCH, SEQ, d), BF16),
                   jax.ShapeDtypeStruct((BATCH, N_EXPERTS, SEQ), F32)],
        compiler_params=_params(("parallel", "parallel")),
        name="merge_outproj",
    )(ya, yb, gates, x, mods3, mods3, mods3, norm2, wa, wb, wo, router_t)


def _thresh_body(aff_ref, lo_ref, hi_ref):
    aff = aff_ref[...]
    rows = aff.shape[0]
    bits = pltpu.bitcast(aff, I32)

    def bisect_bits(_, carry):
        lo, hi = carry
        mid = lo + ((hi - lo) >> 1)
        ge = jnp.sum((bits >= mid).astype(I32), axis=1, keepdims=True) >= CAP
        return jnp.where(ge, mid, lo), jnp.where(ge, hi, mid)

    lo0 = jnp.zeros((rows, 1), I32)
    hi0 = jnp.full((rows, 1), 0x3F800001, I32)
    thr_bits, _ = lax.fori_loop(0, 31, bisect_bits, (lo0, hi0))
    thr = pltpu.bitcast(thr_bits, F32)

    def bisect_val(_, carry):
        lo, hi = carry
        mid = 0.5 * (lo + hi)
        ge = jnp.sum(jnp.where(aff >= mid, 1.0, 0.0), axis=1, keepdims=True) >= CAP
        return jnp.where(ge, mid, lo), jnp.where(ge, hi, mid)

    top = jnp.where(thr > 0.0, 2.0 * thr, F32_TINY)
    lo, hi = lax.fori_loop(0, 30, bisect_val, (0.5 * thr, top))
    lo_ref[...] = jnp.broadcast_to(lo, lo_ref.shape)
    hi_ref[...] = jnp.broadcast_to(hi, hi_ref.shape)


def _thresholds(aff_rows):
    rows = BATCH * N_EXPERTS
    return pl.pallas_call(
        _thresh_body,
        grid=(1,),
        in_specs=[pl.BlockSpec((rows, SEQ), lambda i: (0, 0))],
        out_specs=[pl.BlockSpec((rows, LANES), lambda i: (0, 0)), pl.BlockSpec((rows, LANES), lambda i: (0, 0))],
        out_shape=[jax.ShapeDtypeStruct((rows, LANES), F32), jax.ShapeDtypeStruct((rows, LANES), F32)],
        compiler_params=_params(("arbitrary",)),
        name="route_threshold",
    )(aff_rows)


def _prefix_counts(mask):
    r = lax.broadcasted_iota(I32, (LANES, LANES), 0)
    c = lax.broadcasted_iota(I32, (LANES, LANES), 1)
    upper = jnp.where(r <= c, 1.0, 0.0).astype(BF16)
    offset = jnp.zeros((mask.shape[0], 1), F32)
    blocks = []
    for j in range(mask.shape[1] // LANES):
        blk = mask[:, j * LANES:(j + 1) * LANES]
        inc = _dot(blk.astype(BF16), upper)
        blocks.append(inc - blk + offset)
        offset = offset + inc[:, LANES - 1:LANES]
    return jnp.concatenate(blocks, axis=1)


def _route_body(aff_ref, lo_ref, hi_ref, h_ref, slot_ref, xin_ref, pos_ref, pbuf_ref):
    aff = aff_ref[0]
    above = jnp.where(aff >= hi_ref[:, 0:1], 1.0, 0.0)
    band = jnp.where(aff >= lo_ref[:, 0:1], 1.0, 0.0) - above
    need = CAP - jnp.sum(above, axis=1, keepdims=True)
    tie_rank = _prefix_counts(band)
    self = above + band * jnp.where(tie_rank < need, 1.0, 0.0)
    pos = _prefix_counts(self)
    posi = jnp.where(self > 0.5, pos.astype(I32), -1)
    pos_ref[0] = posi

    h = h_ref[0]
    slot = slot_ref[...]
    posb = jnp.where(self > 0.5, pos, -1.0).astype(BF16)
    one = jnp.ones((CAP, SEQ), BF16)
    zero = jnp.zeros((CAP, SEQ), BF16)
    for grp in range(N_EXPERTS // EXPERT_GROUP):
        for i in range(EXPERT_GROUP):
            e = grp * EXPERT_GROUP + i
            pbuf_ref[i * CAP:(i + 1) * CAP, :] = jnp.where(posb[e:e + 1, :] == slot, one, zero)
        rows = _dot(pbuf_ref[...], h)
        for i in range(EXPERT_GROUP):
            xin_ref[grp * EXPERT_GROUP + i] = rows[i * CAP:(i + 1) * CAP].astype(BF16)


def _route(aff_t, lo, hi, h2, slot_rows):
    d = D_MODEL
    return pl.pallas_call(
        _route_body,
        grid=(BATCH,),
        in_specs=[pl.BlockSpec((1, N_EXPERTS, SEQ), lambda b: (b, 0, 0)),
                  pl.BlockSpec((N_EXPERTS, LANES), lambda b: (b, 0)),
                  pl.BlockSpec((N_EXPERTS, LANES), lambda b: (b, 0)),
                  pl.BlockSpec((1, SEQ, d), lambda b: (b, 0, 0)),
                  pl.BlockSpec((CAP, SEQ), lambda b: (0, 0))],
        out_specs=[pl.BlockSpec((N_EXPERTS, CAP, d), lambda b: (0, b, 0)),
                   pl.BlockSpec((1, N_EXPERTS, SEQ), lambda b: (b, 0, 0))],
        out_shape=[jax.ShapeDtypeStruct((N_EXPERTS, BATCH * CAP, d), BF16),
                   jax.ShapeDtypeStruct((BATCH, N_EXPERTS, SEQ), I32)],
        scratch_shapes=[pltpu.VMEM((EXPERT_GROUP * CAP, SEQ), BF16)],
        compiler_params=_params(("parallel",)),
        name="route_gather",
    )(aff_t, lo, hi, h2, slot_rows)


def _expert_body(x_ref, wg_ref, wu_ref, wd_ref, o_ref, acc_ref, wgb_ref, wub_ref, wdb_ref):
    f = pl.program_id(1)

    @pl.when((pl.program_id(0) == 0) & (f == 0))
    def _():
        acc_ref[...] = jnp.zeros_like(acc_ref)

    carry = f > 0
    for m in range(BATCH * CAP // MC):
        rows = slice(m * MC, (m + 1) * MC)
        xm = x_ref[0, rows, :]
        if m == 0:
            wgb_ref[...] = wg_ref[0].astype(BF16)
        a = _dot(xm, wgb_ref[...])
        if m == 0:
            wub_ref[...] = wu_ref[0].astype(BF16)
        b = _dot(xm, wub_ref[...])
        hh = (a * jax.nn.sigmoid(a) * b).astype(BF16)
        if m == 0:
            wdb_ref[...] = wd_ref[0].astype(BF16)
        total = jnp.where(carry, acc_ref[rows, :], 0.0) + _dot(hh, wdb_ref[...])
        acc_ref[rows, :] = total
        o_ref[0, rows, :] = total.astype(BF16)


def _experts(xin, w_gate, w_up, w_down):
    d = D_MODEL
    rows = BATCH * CAP
    return pl.pallas_call(
        _expert_body,
        grid=(N_EXPERTS, D_FF // TF),
        in_specs=[pl.BlockSpec((1, rows, d), lambda e, f: (e, 0, 0)),
                  pl.BlockSpec((1, d, TF), lambda e, f: (e, 0, f)),
                  pl.BlockSpec((1, d, TF), lambda e, f: (e, 0, f)),
                  pl.BlockSpec((1, TF, d), lambda e, f: (e, f, 0))],
        out_specs=pl.BlockSpec((1, rows, d), lambda e, f: (e, 0, 0)),
        out_shape=jax.ShapeDtypeStruct((N_EXPERTS, rows, d), BF16),
        scratch_shapes=[pltpu.VMEM((rows, d), F32), pltpu.VMEM((d, TF), BF16), pltpu.VMEM((d, TF), BF16),
                        pltpu.VMEM((TF, d), BF16)],
        compiler_params=_params(("parallel", "arbitrary")),
        name="swiglu_experts",
    )(xin, w_gate, w_up, w_down)


def _scatter_body(pos_ref, aff_ref, y_ref, x_ref, g2_ref, o_ref):
    pos = pos_ref[0].T
    aff = aff_ref[0].T
    slot = lax.broadcasted_iota(I32, (SUB_S, CAP), 1)
    groups = [slice(i * SUB_S, (i + 1) * SUB_S) for i in range(TS // SUB_S)]
    onehots = [jnp.concatenate([jnp.where(pos[s, e:e + 1] == slot, aff[s, e:e + 1], 0.0).astype(BF16)
                                for e in range(N_EXPERTS)], axis=1) for s in groups]
    y = y_ref[...].reshape(N_EXPERTS * CAP, D_MODEL)
    mixed = [_dot(oh, y) for oh in onehots]
    for s, m in zip(groups, mixed):
        o_ref[0, s, :] = x_ref[0, s, :] + g2_ref[0] * m


def _scatter(pos, aff_t, y, xn, mods3):
    d = D_MODEL
    return pl.pallas_call(
        _scatter_body,
        grid=(BATCH, SEQ // TS),
        in_specs=[pl.BlockSpec((1, N_EXPERTS, TS), lambda b, i: (b, 0, i)),
                  pl.BlockSpec((1, N_EXPERTS, TS), lambda b, i: (b, 0, i)),
                  pl.BlockSpec((N_EXPERTS, CAP, d), lambda b, i: (0, b, 0)),
                  pl.BlockSpec((1, TS, d), lambda b, i: (b, i, 0)),
                  pl.BlockSpec((1, 1, d), lambda b, i: (b, 0, 5))],
        out_specs=pl.BlockSpec((1, TS, d), lambda b, i: (b, i, 0)),
        out_shape=jax.ShapeDtypeStruct((BATCH, SEQ, d), F32),
        compiler_params=_params(("parallel", "parallel")),
        name="scatter_residual",
    )(pos, aff_t, y, xn, mods3)


def _rope_tables():
    rows = SEQ // GRID_W
    row = np.repeat(np.arange(rows, dtype=np.float32), GRID_W)
    col = np.tile(np.arange(GRID_W, dtype=np.float32), rows)
    inv = (ROPE_BASE ** (-np.arange(0, AXIS_ROT, 2, dtype=np.float32) / AXIS_ROT)).astype(np.float32)
    ang = np.concatenate([row[:, None] * inv, col[:, None] * inv], axis=-1).astype(np.float64)
    cos = np.repeat(np.cos(ang), 2, axis=-1)
    sin = np.stack([-np.sin(ang), np.sin(ang)], axis=-1).reshape(SEQ, HEAD_DIM)
    reps = LANES // HEAD_DIM
    return jnp.asarray(np.tile(cos, (1, reps)), F32), jnp.asarray(np.tile(sin, (1, reps)), F32)


def _dft_tables():
    idx = np.arange(HALF, dtype=np.int64)
    t2p1 = 2 * idx + 1

    def cos_sin(f):
        ang = ((f[:, None] * t2p1[None, :]) % (2 * N_FFT)) * (math.pi / N_FFT)
        return np.cos(ang), np.sin(ang)

    ce, se = cos_sin(2 * idx)
    co, so = cos_sin(2 * idx + 1)
    t1 = jnp.asarray(np.concatenate([ce, so], axis=0), F32).astype(BF16)
    t2 = jnp.asarray(np.concatenate([co, se], axis=0), F32).astype(BF16)
    return t1, t2, t1.T, t2.T


def _phase_tables():
    idx = np.arange(HALF, dtype=np.float64)
    w = np.full((HALF,), 2.0 / N_FFT)
    we = w.copy()
    we[0] = 1.0 / N_FFT
    pe = (math.pi / N_FFT) * (2.0 * idx)
    po = (math.pi / N_FFT) * (2.0 * idx + 1.0)
    rot = np.stack([we * np.cos(pe), we * np.sin(pe), w * np.cos(po), w * np.sin(po)])
    return jnp.asarray(np.broadcast_to(rot[:, :, None], (4, HALF, CT)), F32)


def _fold_rows(a):
    return np.concatenate([a[:HALF], a[HALF:][::-1]], axis=0)


def _filter_features():
    t = np.linspace(0.0, 1.0, SEQ, dtype=np.float32).astype(np.float64)[:, None]
    w = 2.0 * math.pi * np.arange(SEQ, dtype=np.float64)[:, None] / SEQ
    fr = np.linspace(1e-4, FILTER_BANDS - 1, FILTER_BANDS, dtype=np.float32).astype(np.float64)[None, :]
    feat = np.concatenate([t, np.cos(fr * w), -np.sin(fr * w)], axis=-1)
    feat = np.pad(feat, ((0, 0), (0, FILTER_HIDDEN - FILTER_EMB)))
    min_decay = math.log(DECAY_TARGET) / SLOW_DECAY_PCT
    max_decay = math.log(DECAY_TARGET) / FAST_DECAY_PCT
    deltas = np.linspace(min_decay, max_decay, HYENA_W, dtype=np.float32).astype(np.float64)
    decay = np.exp(-t * np.abs(deltas))
    return jnp.asarray(_fold_rows(feat).T, F32), jnp.asarray(_fold_rows(decay), F32)


def _attn_bias():
    qi = np.arange(BLOCK)[:, None]
    kj = np.arange(3 * BLOCK)[None, :]
    band = np.abs(kj - BLOCK - qi) <= WINDOW
    first = band & (kj >= BLOCK)
    last = band & (kj < 2 * BLOCK)
    return jnp.asarray(np.where(np.stack([first, band, last]), 0.0, NEG), F32)


def _pair_heads(w, axis):
    heads = [lax.slice_in_dim(w, h * HEAD_DIM, (h + 1) * HEAD_DIM, axis=axis) for h in range(N_HEADS)]
    return jnp.concatenate([heads[j + GROUP * half] for j in range(GROUP) for half in range(N_KV_HEADS)], axis=axis)


def kernel(x, c, ctx, c_ctx, ada_w, ada_b, norm1, norm2, w_in, conv_w, conv_b, filt_w1, filt_b1, filt_w2, filt_b2,
           filt_w3, filt_b3, filt_freq, filt_out, hyena_bias, q_norm, k_norm, attn_sink, w_branch_a, w_branch_b,
           w_out, router, w_gate, w_up, w_down):
    d = D_MODEL
    assert ada_w.shape[0] == 1, "only the single-layer configuration is implemented"
    l = 0
    cos_t, sin_t = _rope_tables()
    t1, t2, t1t, t2t = _dft_tables()
    rot = _phase_tables()
    feat, decay = _filter_features()
    bias = _attn_bias()
    gmat = jnp.asarray(np.kron(np.eye(2 * LANES // HEAD_DIM), np.full((HEAD_DIM, HEAD_DIM), 1.0 / HEAD_DIM)), BF16)
    c16 = jnp.concatenate([c, c_ctx[None, :], jnp.zeros((MOD_ROWS - BATCH - 1, d), F32)], axis=0)

    mods3 = _ada(c16, ada_w[l], ada_b[l][None, :])
    n1 = norm1[l][None, :]
    w_packed = _pack_in_weights(w_in[l])
    gk = jnp.tile(k_norm[l], N_KV_HEADS)[None, :]
    w1_rows = jnp.pad(filt_w1[l], ((0, FILTER_HIDDEN - FILTER_EMB), (0, 0)))
    mlp_w = jnp.swapaxes(jnp.stack([w1_rows, filt_w2[l], filt_w3[l]]), 1, 2)
    mlp_cols = jnp.stack([filt_b1[l], filt_b2[l], filt_b3[l], filt_freq[l]], axis=1)
    kf, kn = _filters(feat, mlp_w, mlp_cols, filt_out[l], decay, rot, t1, t2)
    kc, vc = _ctx_proj(ctx, mods3, n1, w_packed, gk, gmat)
    zh, q, k, v, gates = _inproj(x, mods3, n1, w_packed, jnp.tile(q_norm[l], N_HEADS)[None, :], gk, gmat, cos_t, sin_t)
    ya = _hyena(zh, conv_w[l], conv_b[l][None, :], hyena_bias[l], kf, kn, t1, t2, t1t, t2t)
    yb = _attention(attn_sink[l] * LOG2E, q, k, v, kc, vc, bias)
    xn, h2, aff_t = _merge(ya, yb, gates, x, mods3, norm2[l][None, :], w_branch_a[l].astype(BF16),
                           _pair_heads(w_branch_b[l], 0).astype(BF16), w_out[l].astype(BF16), router[l].T.astype(BF16))
    lo, hi = _thresholds(aff_t.reshape(BATCH * N_EXPERTS, SEQ))
    slot_rows = jnp.asarray(np.broadcast_to(np.arange(CAP)[:, None], (CAP, SEQ)), BF16)
    xin, pos = _route(aff_t, lo, hi, h2, slot_rows)
    y = _experts(xin, w_gate[l], w_up[l], w_down[l])
    return _scatter(pos, aff_t, y, xn, mods3)
```

```python
import math

import numpy as np
import jax
import jax.numpy as jnp
from jax import lax
from jax.experimental import pallas as pl
from jax.experimental.pallas import tpu as pltpu

F32 = jnp.float32
BF16 = jnp.bfloat16
I32 = jnp.int32
HIGHEST = lax.Precision.HIGHEST

D_MODEL = 1024
BATCH = 8
SEQ = 2048
GRID_W = 64
CTX_LEN = 256
N_HEADS = 8
N_KV_HEADS = 2
HEAD_DIM = 64
GROUP = N_HEADS // N_KV_HEADS
ATTN_W = N_HEADS * HEAD_DIM
KV_W = N_KV_HEADS * HEAD_DIM
WINDOW = 128
BLOCK = 128
HYENA_W = D_MODEL // 2
HYENA_ORDER = 2
FILTER_BANDS = 16
FILTER_EMB = 1 + 2 * FILTER_BANDS
FILTER_HIDDEN = 64
DECAY_TARGET = 1e-2
FAST_DECAY_PCT = 0.3
SLOW_DECAY_PCT = 1.5
ROPE_BASE = 10000.0
AXIS_ROT = HEAD_DIM // 2
N_EXPERTS = 16
EC_CAPACITY = 2
D_FF = 2048
EPS = 1e-6
NEG = -1e30
LOG2E = math.log2(math.e)
F32_TINY = float(np.finfo(np.float32).tiny)

OFF_Q = 3 * HYENA_W
OFF_K = OFF_Q + ATTN_W
OFF_V = OFF_K + KV_W
OFF_G = OFF_V + KV_W

CAP = EC_CAPACITY * SEQ // N_EXPERTS
N_FFT = 2 * SEQ
HALF = SEQ // 2
MOD_ROWS = 16
LANES = 128

TM_IN = 1024
SUB_IN = 256
TM_MERGE = 1024
SUB_MERGE = 512
MERGE_BUFS = 3
CT = 256
FC = 512
RB = 256
TF = 512
MC = 512
TS = 1024
SUB_S = 512
CTX_STEP = 4
EXPERT_GROUP = 4
ATTN_UNROLL = 8
VMEM_LIMIT = 56 * 1024 * 1024


def _dot(a, b, precision=None):
    return jnp.dot(a, b, preferred_element_type=F32, precision=precision)


def _dot_nt(a, b, precision=None):
    return lax.dot_general(a, b, (((1,), (1,)), ((), ())), preferred_element_type=F32, precision=precision)


def _params(sem, vmem=VMEM_LIMIT):
    return pltpu.CompilerParams(dimension_semantics=sem, vmem_limit_bytes=vmem)


def _rms_mod(x, g, sc, sh):
    ms = jnp.mean(x * x, axis=-1, keepdims=True)
    return (x * lax.rsqrt(ms + EPS) * g) * (1.0 + sc) + sh


def _head_norm_rope(z, g, gmat, cos, sin, scale):
    ms = _dot((z * z).astype(BF16), gmat)
    y = z * lax.rsqrt(ms + EPS) * g
    if cos is not None:
        slabs = []
        for s in range(z.shape[1] // LANES):
            ys = y[:, s * LANES:(s + 1) * LANES]
            lane = lax.broadcasted_iota(I32, ys.shape, 1)
            nxt = pltpu.roll(ys, LANES - 1, axis=1)
            prv = pltpu.roll(ys, 1, axis=1)
            slabs.append(ys * cos + jnp.where((lane & 1) == 0, nxt, prv) * sin)
        y = slabs[0] if len(slabs) == 1 else jnp.concatenate(slabs, axis=1)
    return y * scale


def _split_bf16(x):
    hi = x.astype(BF16)
    return hi, (x - hi.astype(F32)).astype(BF16)


def _ada_body(c_ref, w_ref, b_ref, o_ref):
    c = c_ref[...]
    s_hi, s_lo = _split_bf16(c * jax.nn.sigmoid(c))
    w_hi, w_lo = _split_bf16(w_ref[...])
    o_ref[:, 0, :] = _dot(s_hi, w_hi) + _dot(s_lo, w_hi) + _dot(s_hi, w_lo) + b_ref[...]


def _ada(c16, w, b):
    d = D_MODEL
    return pl.pallas_call(
        _ada_body,
        grid=(6,),
        in_specs=[pl.BlockSpec((MOD_ROWS, d), lambda j: (0, 0)),
                  pl.BlockSpec((d, d), lambda j: (0, j)),
                  pl.BlockSpec((1, d), lambda j: (0, j))],
        out_specs=pl.BlockSpec((MOD_ROWS, 1, d), lambda j: (0, 0, j)),
        out_shape=jax.ShapeDtypeStruct((MOD_ROWS, 1, 6 * d), F32),
        compiler_params=_params(("parallel",)),
        name="ada_mod",
    )(c16, w, b)


def _sign_rows(n):
    lane = lax.broadcasted_iota(I32, (8, n), 1)
    sub = lax.broadcasted_iota(I32, (8, n), 0)
    sg = jnp.where((lane & 1) == 0, 1.0, -1.0)
    return jnp.where(sub == 0, sg, 0.0).astype(BF16)


def _filt_body(feat_ref, w_ref, col_ref, fof_ref, fob_ref, dec_ref, rot_ref, t1_ref, t2_ref, kf_ref, kn_ref, hs_ref):
    @pl.when((pl.program_id(0) == 0) & (pl.program_id(1) == 0))
    def _():
        cols = col_ref[...]
        fq = cols[:, 3:4]
        h = feat_ref[...]
        for layer in range(3):
            h = jnp.sin(fq * (_dot(w_ref[layer], h, HIGHEST) + cols[:, layer:layer + 1]))
        hi = h.astype(BF16).astype(F32)
        stacked = jnp.concatenate([hi, h - hi, hi, jnp.zeros_like(hi)], axis=0)
        hs_ref[...] = stacked.T.astype(BF16)

    def taps(fo_ref):
        f_hi, f_lo = _split_bf16(fo_ref[...])
        return _dot(hs_ref[...], jnp.concatenate([f_hi, f_hi, f_lo, jnp.zeros_like(f_hi)], axis=0))

    dec = dec_ref[...]
    hf = taps(fof_ref) * dec
    hb = taps(fob_ref) * dec
    row = lax.broadcasted_iota(I32, hf.shape, 0)
    hb = jnp.where(row == 0, 0.0, hb)
    a = hf + hb
    b = hf - hb
    pa = (a[:HALF] + a[HALF:]).astype(BF16)
    ma = (a[:HALF] - a[HALF:]).astype(BF16)
    pb = (b[:HALF] + b[HALF:]).astype(BF16)
    mb = (b[:HALF] - b[HALF:]).astype(BF16)
    t1 = t1_ref[...]
    t2 = t2_ref[...]
    ce, se, co, so = rot_ref[0], rot_ref[1], rot_ref[2], rot_ref[3]
    a1 = _dot(t1, pa)
    a2 = _dot(t2, ma)
    kf_ref[0, 0] = a1[:HALF] * ce + a2[HALF:] * se
    kf_ref[0, 2] = a2[:HALF] * co + a1[HALF:] * so
    b1 = _dot(t1, pb)
    b2 = _dot(t2, mb)
    kf_ref[0, 1] = b2[HALF:] * ce - b1[:HALF] * se
    kf_ref[0, 3] = b1[HALF:] * co - b2[:HALF] * so
    kn_ref[0] = _dot(_sign_rows(HALF), ma)[0:1] * (1.0 / N_FFT)


def _filters(feat, mlp_w, mlp_cols, fout, decay, rot, t1, t2):
    nct = HYENA_W // CT
    full = lambda shape: pl.BlockSpec(shape, lambda o, c: (0,) * len(shape))
    return pl.pallas_call(
        _filt_body,
        grid=(HYENA_ORDER, nct),
        in_specs=[full((FILTER_HIDDEN, SEQ)), full((3, FILTER_HIDDEN, FILTER_HIDDEN)), full((FILTER_HIDDEN, 4)),
                  pl.BlockSpec((FILTER_HIDDEN, CT), lambda o, c: (0, (o * 2 + 0) * nct + c)),
                  pl.BlockSpec((FILTER_HIDDEN, CT), lambda o, c: (0, (o * 2 + 1) * nct + c)),
                  pl.BlockSpec((SEQ, CT), lambda o, c: (0, c)),
                  full((4, HALF, CT)),
                  pl.BlockSpec((SEQ, HALF), lambda o, c: (0, 0), pipeline_mode=pl.Buffered(1)),
                  pl.BlockSpec((SEQ, HALF), lambda o, c: (0, 0), pipeline_mode=pl.Buffered(1))],
        out_specs=[pl.BlockSpec((1, 4, HALF, CT), lambda o, c: (o, 0, 0, c)),
                   pl.BlockSpec((1, 1, CT), lambda o, c: (o, 0, c))],
        out_shape=[jax.ShapeDtypeStruct((HYENA_ORDER, 4, HALF, HYENA_W), F32),
                   jax.ShapeDtypeStruct((HYENA_ORDER, 1, HYENA_W), F32)],
        scratch_shapes=[pltpu.VMEM((SEQ, 4 * FILTER_HIDDEN), BF16)],
        compiler_params=_params(("arbitrary", "arbitrary")),
        name="hyena_filters",
    )(feat, mlp_w, mlp_cols, fout, fout, decay, rot, t1, t2)


def _pair_head_lanes(z):
    slabs = [z[:, s * LANES:(s + 1) * LANES] for s in range(ATTN_W // LANES)]
    swapped = [pltpu.roll(sl, HEAD_DIM, axis=1) for sl in slabs]
    low = lax.broadcasted_iota(I32, slabs[0].shape, 1) < HEAD_DIM
    out = []
    for j in range(GROUP):
        first, second = j, j + GROUP
        lo_src = slabs[first // 2] if first % 2 == 0 else swapped[first // 2]
        hi_src = slabs[second // 2] if second % 2 == 1 else swapped[second // 2]
        out.append(jnp.where(low, lo_src, hi_src))
    return jnp.concatenate(out, axis=1)


def _inproj_body(x_ref, sc_ref, sh_ref, n1_ref, wh_ref, wq_ref, wkv_ref, wg_ref, gq_ref, gk_ref, gmat_ref,
                 cos_ref, sin_ref, zh_ref, q_ref, k_ref, v_ref, gate_ref):
    groups = [slice(i * SUB_IN, (i + 1) * SUB_IN) for i in range(TM_IN // SUB_IN)]
    hx = [_rms_mod(x_ref[0, s, :], n1_ref[...], sc_ref[0], sh_ref[0]).astype(BF16) for s in groups]
    for s, h in zip(groups, hx):
        zh_ref[0, s, :] = _dot(h, wh_ref[...]).astype(BF16)
    pair = 2 * LANES
    zq = [_pair_head_lanes(_dot(h, wq_ref[...])) for h in hx]
    for s, z in zip(groups, zq):
        for c in range(ATTN_W // pair):
            sl = slice(c * pair, (c + 1) * pair)
            q_ref[0, s, sl] = _head_norm_rope(z[:, sl], gq_ref[:, sl], gmat_ref[...], cos_ref[s, :], sin_ref[s, :],
                                              LOG2E * HEAD_DIM ** -0.5).astype(BF16)
    zkv = [_dot(h, wkv_ref[...]) for h in hx]
    for s, z in zip(groups, zkv):
        k_ref[0, s, :] = _head_norm_rope(z[:, :KV_W], gk_ref[...], gmat_ref[0:KV_W, 0:KV_W], cos_ref[s, :],
                                         sin_ref[s, :], 1.0).astype(BF16)
        v_ref[0, s, :] = z[:, KV_W:].astype(BF16)
    for s, h in zip(groups, hx):
        gate_ref[0, s, :] = jax.nn.sigmoid(_dot(h, wg_ref[...])).astype(BF16)


def _pack_in_weights(w):
    wb = w.astype(BF16)
    return jnp.concatenate([wb[:, :OFF_K], wb[:, OFF_G:], wb[:, OFF_K:OFF_G]], axis=1)


W_OFF_H = 0
W_OFF_Q = OFF_Q
W_OFF_G = OFF_Q + ATTN_W
W_OFF_KV = W_OFF_G + 2 * D_MODEL


def _inproj(x, mods3, norm1, w_packed, gq, gk, gmat, cos_t, sin_t):
    d = D_MODEL
    nt = SEQ // TM_IN
    const = lambda shape: pl.BlockSpec(shape, lambda b, i: (0,) * len(shape))

    def wcol(width, off):
        assert off % width == 0
        return pl.BlockSpec((d, width), lambda b, i: (0, off // width))

    tok = lambda w: pl.BlockSpec((1, TM_IN, w), lambda b, i: (b, i, 0))
    return pl.pallas_call(
        _inproj_body,
        grid=(BATCH, nt),
        in_specs=[tok(d),
                  pl.BlockSpec((1, 1, d), lambda b, i: (b, 0, 1)),
                  pl.BlockSpec((1, 1, d), lambda b, i: (b, 0, 0)),
                  const((1, d)), wcol(OFF_Q, W_OFF_H), wcol(ATTN_W, W_OFF_Q), wcol(2 * KV_W, W_OFF_KV),
                  wcol(2 * d, W_OFF_G), const((1, ATTN_W)), const((1, KV_W)), const((2 * LANES, 2 * LANES)),
                  pl.BlockSpec((TM_IN, LANES), lambda b, i: (i, 0)),
                  pl.BlockSpec((TM_IN, LANES), lambda b, i: (i, 0))],
        out_specs=[tok(OFF_Q), tok(ATTN_W), tok(KV_W), tok(KV_W), tok(2 * d)],
        out_shape=[jax.ShapeDtypeStruct((BATCH, SEQ, OFF_Q), BF16),
                   jax.ShapeDtypeStruct((BATCH, SEQ, ATTN_W), BF16),
                   jax.ShapeDtypeStruct((BATCH, SEQ, KV_W), BF16),
                   jax.ShapeDtypeStruct((BATCH, SEQ, KV_W), BF16),
                   jax.ShapeDtypeStruct((BATCH, SEQ, 2 * d), BF16)],
        compiler_params=_params(("parallel", "parallel")),
        name="in_proj",
    )(x, mods3, mods3, norm1, w_packed, w_packed, w_packed, w_packed, gq, gk, gmat, cos_t, sin_t)


def _ctx_body(c_ref, sc_ref, sh_ref, n1_ref, wkv_ref, gk_ref, gmat_ref, kc_ref, vc_ref):
    for i in range(CTX_STEP):
        hc = _rms_mod(c_ref[i], n1_ref[...], sc_ref[0], sh_ref[0]).astype(BF16)
        z = _dot(hc, wkv_ref[...])
        kc_ref[i] = _head_norm_rope(z[:, :KV_W], gk_ref[...], gmat_ref[0:KV_W, 0:KV_W], None, None, 1.0).astype(BF16)
        vc_ref[i] = z[:, KV_W:].astype(BF16)


def _ctx_proj(ctx, mods3, norm1, w_packed, gk, gmat):
    d = D_MODEL
    const = lambda shape: pl.BlockSpec(shape, lambda b: (0,) * len(shape))
    return pl.pallas_call(
        _ctx_body,
        grid=(BATCH // CTX_STEP,),
        in_specs=[pl.BlockSpec((CTX_STEP, CTX_LEN, d), lambda b: (b, 0, 0)),
                  pl.BlockSpec((1, 1, d), lambda b: (BATCH, 0, 1)),
                  pl.BlockSpec((1, 1, d), lambda b: (BATCH, 0, 0)),
                  const((1, d)), pl.BlockSpec((d, 2 * KV_W), lambda b: (0, W_OFF_KV // (2 * KV_W))),
                  const((1, KV_W)), const((2 * LANES, 2 * LANES))],
        out_specs=[pl.BlockSpec((CTX_STEP, CTX_LEN, KV_W), lambda b: (b, 0, 0)),
                   pl.BlockSpec((CTX_STEP, CTX_LEN, KV_W), lambda b: (b, 0, 0))],
        out_shape=[jax.ShapeDtypeStruct((BATCH, CTX_LEN, KV_W), BF16),
                   jax.ShapeDtypeStruct((BATCH, CTX_LEN, KV_W), BF16)],
        compiler_params=_params(("parallel",)),
        name="ctx_proj",
    )(ctx, mods3, mods3, norm1, w_packed, gk, gmat)


def _hyena_body(zv_ref, z1_ref, z2_ref, cwv_ref, cw1_ref, cw2_ref, cbv_ref, cb1_ref, cb2_ref, hb_ref, kf_ref,
                kn_ref, t1_ref, t2_ref, t1t_ref, t2t_ref, o_ref, lo_ref, hi_ref, g0lo_ref, g0hi_ref, g1lo_ref, g1hi_ref,
                p_ref, m_ref, za_ref, zb_ref):
    row = lax.broadcasted_iota(I32, (HALF, CT), 0)
    rr = lax.broadcasted_iota(I32, (RB, RB), 0)
    cc = lax.broadcasted_iota(I32, (RB, RB), 1)
    flip = jnp.where(rr + cc == RB - 1, 1.0, 0.0).astype(BF16)
    nrb = HALF // RB

    def reverse_upper_half(z_ref, hi_out):
        for j in range(nrb):
            hi_out[j * RB:(j + 1) * RB, :] = _dot(flip, z_ref[0, SEQ - RB * (j + 1):SEQ - RB * j, :])

    def folded_short_conv(z_ref, w_ref, b_ref, lo_out, hi_out):
        zlo = z_ref[0, 0:HALF, :].astype(F32)
        zhi = hi_out[...]
        w0, w1, w2 = w_ref[0:1, :], w_ref[1:2, :], w_ref[2:3, :]
        first, last = row == 0, row == HALF - 1
        lo_prev = jnp.where(first, 0.0, pltpu.roll(zlo, 1, axis=0))
        lo_next = jnp.where(last, zhi[HALF - 1:HALF, :], pltpu.roll(zlo, HALF - 1, axis=0))
        hi_prev = jnp.where(first, 0.0, pltpu.roll(zhi, 1, axis=0))
        hi_next = jnp.where(last, zlo[HALF - 1:HALF, :], pltpu.roll(zhi, HALF - 1, axis=0))
        lo_out[...] = lo_prev * w0 + zlo * w1 + lo_next * w2 + b_ref[...]
        hi_out[...] = hi_next * w0 + zhi * w1 + hi_prev * w2 + b_ref[...]

    gates = ((g0lo_ref, g0hi_ref), (g1lo_ref, g1hi_ref))
    reverse_upper_half(zv_ref, hi_ref)
    reverse_upper_half(z1_ref, g0hi_ref)
    reverse_upper_half(z2_ref, g1hi_ref)
    folded_short_conv(zv_ref, cwv_ref, cbv_ref, lo_ref, hi_ref)
    folded_short_conv(z1_ref, cw1_ref, cb1_ref, g0lo_ref, g0hi_ref)
    folded_short_conv(z2_ref, cw2_ref, cb2_ref, g1lo_ref, g1hi_ref)
    sign8 = _sign_rows(HALF)
    odd = (lax.broadcasted_iota(I32, (FC, CT), 0) & 1) == 1
    for o, (glo_ref, ghi_ref) in enumerate(gates):
        p_ref[...] = (lo_ref[...] + hi_ref[...]).astype(BF16)
        m_ref[...] = (lo_ref[...] - hi_ref[...]).astype(BF16)
        pv = p_ref[...]
        mv = m_ref[...]
        for c in range(HALF // FC):
            ev = slice(c * FC, (c + 1) * FC)
            od = slice(HALF + c * FC, HALF + (c + 1) * FC)
            xce = _dot(t1_ref[ev, :], pv)
            xse = _dot(t2_ref[od, :], mv)
            kce, kse = kf_ref[o, 0, ev, :], kf_ref[o, 1, ev, :]
            za_ref[ev, :] = (xce * kce - xse * kse).astype(BF16)
            zb_ref[od, :] = (xce * kse + xse * kce).astype(BF16)
            xco = _dot(t2_ref[ev, :], mv)
            xso = _dot(t1_ref[od, :], pv)
            kco, kso = kf_ref[o, 2, ev, :], kf_ref[o, 3, ev, :]
            za_ref[od, :] = (xco * kso + xso * kco).astype(BF16)
            zb_ref[ev, :] = (xco * kco - xso * kso).astype(BF16)
        zn = _dot(sign8, mv)[0:1] * kn_ref[o]
        bias = hb_ref[o:o + 1, :]
        za = za_ref[...]
        zb = zb_ref[...]
        for c in range(HALF // FC):
            rs = slice(c * FC, (c + 1) * FC)
            half_p = _dot(t1t_ref[rs, :], za)
            half_m = _dot(t2t_ref[rs, :], zb) + jnp.where(odd, -zn, zn)
            lo_ref[rs, :] = glo_ref[rs, :] * (half_p + half_m + bias * lo_ref[rs, :])
            hi_ref[rs, :] = ghi_ref[rs, :] * (half_p - half_m + bias * hi_ref[rs, :])
    o_ref[0, 0:HALF, :] = lo_ref[...].astype(BF16)
    for j in range(nrb):
        o_ref[0, SEQ - RB * (j + 1):SEQ - RB * j, :] = _dot(
            flip, hi_ref[j * RB:(j + 1) * RB, :].astype(BF16)).astype(BF16)


def _hyena(zh, conv_w, conv_b, hbias, kf, kn, t1, t2, t1t, t2t):
    nct = HYENA_W // CT
    zspec = lambda k: pl.BlockSpec((1, SEQ, CT), lambda c, b: (b, 0, k * nct + c))
    wspec = lambda k: pl.BlockSpec((3, CT), lambda c, b: (0, k * nct + c))
    bspec = lambda k: pl.BlockSpec((1, CT), lambda c, b: (0, k * nct + c))
    table = lambda shape: pl.BlockSpec(shape, lambda c, b: (0, 0), pipeline_mode=pl.Buffered(1))
    half_f32 = pltpu.VMEM((HALF, CT), F32)
    return pl.pallas_call(
        _hyena_body,
        grid=(nct, BATCH),
        in_specs=[zspec(0), zspec(1), zspec(2), wspec(0), wspec(1), wspec(2), bspec(0), bspec(1), bspec(2),
                  pl.BlockSpec((HYENA_ORDER, CT), lambda c, b: (0, c)),
                  pl.BlockSpec((HYENA_ORDER, 4, HALF, CT), lambda c, b: (0, 0, 0, c)),
                  pl.BlockSpec((HYENA_ORDER, 1, CT), lambda c, b: (0, 0, c)),
                  table((SEQ, HALF)), table((SEQ, HALF)), table((HALF, SEQ)), table((HALF, SEQ))],
        out_specs=pl.BlockSpec((1, SEQ, CT), lambda c, b: (b, 0, c)),
        out_shape=jax.ShapeDtypeStruct((BATCH, SEQ, HYENA_W), BF16),
        scratch_shapes=[half_f32] * 6 + [
            pltpu.VMEM((HALF, CT), BF16), pltpu.VMEM((HALF, CT), BF16),
            pltpu.VMEM((SEQ, CT), BF16), pltpu.VMEM((SEQ, CT), BF16)],
        compiler_params=_params(("parallel", "parallel")),
        name="hyena_conv",
    )(zh, zh, zh, conv_w, conv_w, conv_w, conv_b, conv_b, conv_b, hbias, kf, kn, t1, t2, t1t, t2t)


def _attn_body(sink_ref, q_ref, k_ref, v_ref, kc_ref, vc_ref, bias_ref, o_ref, kp_ref, vlo_ref, vhi_ref):
    nb = SEQ // BLOCK
    lane = lax.broadcasted_iota(I32, (BLOCK, LANES), 1)
    low = lane < HEAD_DIM
    mask_lo = jnp.where(low, 1.0, 0.0).astype(BF16)
    mask_hi = jnp.where(low, 0.0, 1.0).astype(BF16)

    def with_ones(v):
        lane_v = lax.broadcasted_iota(I32, v.shape, 1) < HEAD_DIM
        one = jnp.ones_like(v)
        return jnp.where(lane_v, v, one), jnp.where(lane_v, one, v)

    zpad = jnp.zeros((BLOCK, KV_W), BF16)
    kp_ref[0:BLOCK] = zpad
    kp_ref[BLOCK:BLOCK + SEQ] = k_ref[0]
    kp_ref[BLOCK + SEQ:] = zpad
    v_lo, v_hi = with_ones(v_ref[0])
    for ref, val in ((vlo_ref, v_lo), (vhi_ref, v_hi)):
        ref[0:BLOCK] = zpad
        ref[BLOCK:BLOCK + SEQ] = val
        ref[BLOCK + SEQ:] = zpad
    kc = kc_ref[0]
    vc_pair = with_ones(vc_ref[0])

    def scores_of(n):
        r = pl.multiple_of(n * BLOCK, BLOCK)
        kw = kp_ref[pl.ds(r, 3 * BLOCK), :]
        bias = bias_ref[jnp.where(n == 0, 0, jnp.where(n == nb - 1, 2, 1))]
        scores = []
        for j in range(GROUP):
            qs = q_ref[0, pl.ds(r, BLOCK), j * LANES:(j + 1) * LANES]
            for msk in (mask_lo, mask_hi):
                qm = qs * msk
                scores.append((_dot_nt(qm, kw) + bias, _dot_nt(qm, kc)))
        return scores

    def finish(n, scores):
        r = pl.multiple_of(n * BLOCK, BLOCK)
        vw_pair = (vlo_ref[pl.ds(r, 3 * BLOCK), :], vhi_ref[pl.ds(r, 3 * BLOCK), :])
        probs = []
        for idx, (sw, sc) in enumerate(scores):
            snk = sink_ref[idx // 2 + GROUP * (idx % 2)]
            m = jnp.maximum(jnp.maximum(jnp.max(sw, axis=-1, keepdims=True),
                                        jnp.max(sc, axis=-1, keepdims=True)), snk)
            probs.append((jnp.exp2(sw - m).astype(BF16), jnp.exp2(sc - m).astype(BF16), jnp.exp2(snk - m)))
        outs = []
        for idx, (pw, pc, psink) in enumerate(probs):
            acc = _dot(pw, vw_pair[idx % 2]) + _dot(pc, vc_pair[idx % 2])
            den = pltpu.roll(acc, HEAD_DIM, axis=1) + psink
            outs.append(acc / den)
        for j in range(GROUP):
            o_ref[0, pl.ds(r, BLOCK), j * LANES:(j + 1) * LANES] = jnp.where(
                low, outs[2 * j], outs[2 * j + 1]).astype(BF16)

    def block_group(g, carry):
        first = g * ATTN_UNROLL
        pending = scores_of(first)
        for i in range(ATTN_UNROLL):
            nxt = scores_of(first + i + 1) if i + 1 < ATTN_UNROLL else None
            finish(first + i, pending)
            pending = nxt
        return carry

    lax.fori_loop(0, nb // ATTN_UNROLL, block_group, 0)


def _attention(sink, q, k, v, kc, vc, bias):
    per_b = lambda n, w: pl.BlockSpec((1, n, w), lambda b: (b, 0, 0))
    return pl.pallas_call(
        _attn_body,
        grid=(BATCH,),
        in_specs=[pl.BlockSpec(memory_space=pltpu.SMEM),
                  per_b(SEQ, ATTN_W), per_b(SEQ, KV_W), per_b(SEQ, KV_W), per_b(CTX_LEN, KV_W), per_b(CTX_LEN, KV_W),
                  pl.BlockSpec((3, BLOCK, 3 * BLOCK), lambda b: (0, 0, 0))],
        out_specs=per_b(SEQ, ATTN_W),
        out_shape=jax.ShapeDtypeStruct((BATCH, SEQ, ATTN_W), BF16),
        scratch_shapes=[pltpu.VMEM((SEQ + 2 * BLOCK, KV_W), BF16)] * 3,
        compiler_params=_params(("parallel",)),
        name="window_attn",
    )(sink, q, k, v, kc, vc, bias)


def _merge_input_copies(gt_hbm, x_hbm, gt_buf, x_buf, sems, step, slot):
    nt = SEQ // TM_MERGE
    rows = pl.ds(pl.multiple_of((step % nt) * TM_MERGE, TM_MERGE), TM_MERGE)
    return (pltpu.make_async_copy(gt_hbm.at[step // nt, rows, :], gt_buf.at[slot], sems.at[0, slot]),
            pltpu.make_async_copy(x_hbm.at[step // nt, rows, :], x_buf.at[slot], sems.at[1, slot]))


def _merge_body(ya_ref, yb_ref, gt_hbm, x_hbm, g1_ref, sc2_ref, sh2_ref, n2_ref, wa_ref, wb_ref, wo_ref, rt_ref,
                xn_ref, h2_ref, aff_ref, gt_buf, x_buf, sems):
    d = D_MODEL
    nt = SEQ // TM_MERGE
    step = pl.program_id(0) * nt + pl.program_id(1)
    ahead = MERGE_BUFS - 1

    @pl.when(step == 0)
    def _():
        for s in range(ahead):
            for cp in _merge_input_copies(gt_hbm, x_hbm, gt_buf, x_buf, sems, s, s):
                cp.start()

    @pl.when(step + ahead < BATCH * nt)
    def _():
        for cp in _merge_input_copies(gt_hbm, x_hbm, gt_buf, x_buf, sems, step + ahead, (step + ahead) % MERGE_BUFS):
            cp.start()

    slot = step % MERGE_BUFS
    for cp in _merge_input_copies(gt_hbm, x_hbm, gt_buf, x_buf, sems, step, slot):
        cp.wait()
    gt_ref = gt_buf.at[slot]
    x_ref = x_buf.at[slot]
    groups = [slice(i * SUB_MERGE, (i + 1) * SUB_MERGE) for i in range(TM_MERGE // SUB_MERGE)]
    branch = [(_dot(ya_ref[0, s, :], wa_ref[...]), _dot(yb_ref[0, s, :], wb_ref[...])) for s in groups]
    mixed = [(gt_ref[s, :d].astype(F32) * ua + gt_ref[s, d:].astype(F32) * ub).astype(BF16)
             for s, (ua, ub) in zip(groups, branch)]
    proj = [_dot(u, wo_ref[...]) for u in mixed]
    h2_rows = []
    for s, m in zip(groups, proj):
        xn = x_ref[s, :] + g1_ref[0] * m
        xn_ref[0, s, :] = xn
        h2 = _rms_mod(xn, n2_ref[...], sc2_ref[0], sh2_ref[0]).astype(BF16)
        h2_ref[0, s, :] = h2
        h2_rows.append(h2)
    logits = [_dot_nt(rt_ref[...], h2) for h2 in h2_rows]
    for s, lt in zip(groups, logits):
        e = jnp.exp(lt - jnp.max(lt, axis=0, keepdims=True))
        aff_ref[0, :, s] = e / jnp.sum(e, axis=0, keepdims=True)


def _merge(ya, yb, gates, x, mods3, norm2, wa, wb, wo, router_t):
    d = D_MODEL
    nt = SEQ // TM_MERGE
    const = lambda shape: pl.BlockSpec(shape, lambda b, i: (0,) * len(shape))
    tok = lambda w: pl.BlockSpec((1, TM_MERGE, w), lambda b, i: (b, i, 0))
    mod = lambda k: pl.BlockSpec((1, 1, d), lambda b, i: (b, 0, k))
    return pl.pallas_call(
        _merge_body,
        grid=(BATCH, nt),
        in_specs=[tok(HYENA_W), tok(ATTN_W), pl.BlockSpec(memory_space=pl.ANY), pl.BlockSpec(memory_space=pl.ANY),
                  mod(2), mod(4), mod(3), const((1, d)),
                  const((HYENA_W, d)), const((ATTN_W, d)), const((d, d)), const((N_EXPERTS, d))],
        out_specs=[tok(d), tok(d), pl.BlockSpec((1, N_EXPERTS, TM_MERGE), lambda b, i: (b, 0, i))],
        out_shape=[jax.ShapeDtypeStruct((BATCH, SEQ, d), F32),
                   jax.ShapeDtypeStruct((BATCH, SEQ, d), BF16),
                   jax.ShapeDtypeStruct((BATCH, N_EXPERTS, SEQ), F32)],
        scratch_shapes=[pltpu.VMEM((MERGE_BUFS, TM_MERGE, 2 * d), BF16), pltpu.VMEM((MERGE_BUFS, TM_MERGE, d), F32),
                        pltpu.SemaphoreType.DMA((2, MERGE_BUFS))],
        compiler_params=_params(("arbitrary", "arbitrary")),
        name="merge_outproj",
    )(ya, yb, gates, x, mods3, mods3, mods3, norm2, wa, wb, wo, router_t)


def _thresh_body(aff_ref, lo_ref, hi_ref):
    aff = aff_ref[...]
    rows = aff.shape[0]
    bits = pltpu.bitcast(aff, I32)

    def bisect_bits(_, carry):
        lo, hi = carry
        mid = lo + ((hi - lo) >> 1)
        ge = jnp.sum((bits >= mid).astype(I32), axis=1, keepdims=True) >= CAP
        return jnp.where(ge, mid, lo), jnp.where(ge, hi, mid)

    lo0 = jnp.zeros((rows, 1), I32)
    hi0 = jnp.full((rows, 1), 0x3F800001, I32)
    thr_bits, _ = lax.fori_loop(0, 31, bisect_bits, (lo0, hi0))
    thr = pltpu.bitcast(thr_bits, F32)

    def bisect_val(_, carry):
        lo, hi = carry
        mid = 0.5 * (lo + hi)
        ge = jnp.sum(jnp.where(aff >= mid, 1.0, 0.0), axis=1, keepdims=True) >= CAP
        return jnp.where(ge, mid, lo), jnp.where(ge, hi, mid)

    top = jnp.where(thr > 0.0, 2.0 * thr, F32_TINY)
    lo, hi = lax.fori_loop(0, 30, bisect_val, (0.5 * thr, top))
    lo_ref[...] = jnp.broadcast_to(lo, lo_ref.shape)
    hi_ref[...] = jnp.broadcast_to(hi, hi_ref.shape)


def _thresholds(aff_rows):
    rows = BATCH * N_EXPERTS
    return pl.pallas_call(
        _thresh_body,
        grid=(1,),
        in_specs=[pl.BlockSpec((rows, SEQ), lambda i: (0, 0))],
        out_specs=[pl.BlockSpec((rows, LANES), lambda i: (0, 0)), pl.BlockSpec((rows, LANES), lambda i: (0, 0))],
        out_shape=[jax.ShapeDtypeStruct((rows, LANES), F32), jax.ShapeDtypeStruct((rows, LANES), F32)],
        compiler_params=_params(("arbitrary",)),
        name="route_threshold",
    )(aff_rows)


def _prefix_counts(mask):
    r = lax.broadcasted_iota(I32, (LANES, LANES), 0)
    c = lax.broadcasted_iota(I32, (LANES, LANES), 1)
    upper = jnp.where(r <= c, 1.0, 0.0).astype(BF16)
    offset = jnp.zeros((mask.shape[0], 1), F32)
    blocks = []
    for j in range(mask.shape[1] // LANES):
        blk = mask[:, j * LANES:(j + 1) * LANES]
        inc = _dot(blk.astype(BF16), upper)
        blocks.append(inc - blk + offset)
        offset = offset + inc[:, LANES - 1:LANES]
    return jnp.concatenate(blocks, axis=1)


def _route_body(aff_ref, lo_ref, hi_ref, h_ref, slot_ref, xin_ref, pos_ref, pbuf_ref):
    aff = aff_ref[0]
    above = jnp.where(aff >= hi_ref[:, 0:1], 1.0, 0.0)
    band = jnp.where(aff >= lo_ref[:, 0:1], 1.0, 0.0) - above
    need = CAP - jnp.sum(above, axis=1, keepdims=True)
    tie_rank = _prefix_counts(band)
    self = above + band * jnp.where(tie_rank < need, 1.0, 0.0)
    pos = _prefix_counts(self)
    posi = jnp.where(self > 0.5, pos.astype(I32), -1)
    pos_ref[0] = posi

    h = h_ref[0]
    slot = slot_ref[...]
    posb = jnp.where(self > 0.5, pos, -1.0).astype(BF16)
    one = jnp.ones((CAP, SEQ), BF16)
    zero = jnp.zeros((CAP, SEQ), BF16)
    for grp in range(N_EXPERTS // EXPERT_GROUP):
        for i in range(EXPERT_GROUP):
            e = grp * EXPERT_GROUP + i
            pbuf_ref[i * CAP:(i + 1) * CAP, :] = jnp.where(posb[e:e + 1, :] == slot, one, zero)
        rows = _dot(pbuf_ref[...], h)
        for i in range(EXPERT_GROUP):
            xin_ref[grp * EXPERT_GROUP + i] = rows[i * CAP:(i + 1) * CAP].astype(BF16)


def _route(aff_t, lo, hi, h2, slot_rows):
    d = D_MODEL
    return pl.pallas_call(
        _route_body,
        grid=(BATCH,),
        in_specs=[pl.BlockSpec((1, N_EXPERTS, SEQ), lambda b: (b, 0, 0)),
                  pl.BlockSpec((N_EXPERTS, LANES), lambda b: (b, 0)),
                  pl.BlockSpec((N_EXPERTS, LANES), lambda b: (b, 0)),
                  pl.BlockSpec((1, SEQ, d), lambda b: (b, 0, 0)),
                  pl.BlockSpec((CAP, SEQ), lambda b: (0, 0))],
        out_specs=[pl.BlockSpec((N_EXPERTS, CAP, d), lambda b: (0, b, 0)),
                   pl.BlockSpec((1, N_EXPERTS, SEQ), lambda b: (b, 0, 0))],
        out_shape=[jax.ShapeDtypeStruct((N_EXPERTS, BATCH * CAP, d), BF16),
                   jax.ShapeDtypeStruct((BATCH, N_EXPERTS, SEQ), I32)],
        scratch_shapes=[pltpu.VMEM((EXPERT_GROUP * CAP, SEQ), BF16)],
        compiler_params=_params(("parallel",)),
        name="route_gather",
    )(aff_t, lo, hi, h2, slot_rows)


def _expert_body(x_ref, wg_ref, wu_ref, wd_ref, o_ref, acc_ref, wgb_ref, wub_ref, wdb_ref):
    f = pl.program_id(1)

    @pl.when((pl.program_id(0) == 0) & (f == 0))
    def _():
        acc_ref[...] = jnp.zeros_like(acc_ref)

    carry = f > 0
    for m in range(BATCH * CAP // MC):
        rows = slice(m * MC, (m + 1) * MC)
        xm = x_ref[0, rows, :]
        if m == 0:
            wgb_ref[...] = wg_ref[0].astype(BF16)
        a = _dot(xm, wgb_ref[...])
        if m == 0:
            wub_ref[...] = wu_ref[0].astype(BF16)
        b = _dot(xm, wub_ref[...])
        hh = (a * jax.nn.sigmoid(a) * b).astype(BF16)
        if m == 0:
            wdb_ref[...] = wd_ref[0].astype(BF16)
        total = jnp.where(carry, acc_ref[rows, :], 0.0) + _dot(hh, wdb_ref[...])
        acc_ref[rows, :] = total
        o_ref[0, rows, :] = total.astype(BF16)


def _experts(xin, w_gate, w_up, w_down):
    d = D_MODEL
    rows = BATCH * CAP
    return pl.pallas_call(
        _expert_body,
        grid=(N_EXPERTS, D_FF // TF),
        in_specs=[pl.BlockSpec((1, rows, d), lambda e, f: (e, 0, 0)),
                  pl.BlockSpec((1, d, TF), lambda e, f: (e, 0, f)),
                  pl.BlockSpec((1, d, TF), lambda e, f: (e, 0, f)),
                  pl.BlockSpec((1, TF, d), lambda e, f: (e, f, 0))],
        out_specs=pl.BlockSpec((1, rows, d), lambda e, f: (e, 0, 0)),
        out_shape=jax.ShapeDtypeStruct((N_EXPERTS, rows, d), BF16),
        scratch_shapes=[pltpu.VMEM((rows, d), F32), pltpu.VMEM((d, TF), BF16), pltpu.VMEM((d, TF), BF16),
                        pltpu.VMEM((TF, d), BF16)],
        compiler_params=_params(("parallel", "arbitrary")),
        name="swiglu_experts",
    )(xin, w_gate, w_up, w_down)


def _scatter_body(pos_ref, aff_ref, y_ref, x_ref, g2_ref, o_ref):
    pos = pos_ref[0].T
    aff = aff_ref[0].T
    slot = lax.broadcasted_iota(I32, (SUB_S, CAP), 1)
    groups = [slice(i * SUB_S, (i + 1) * SUB_S) for i in range(TS // SUB_S)]
    onehots = [jnp.concatenate([jnp.where(pos[s, e:e + 1] == slot, aff[s, e:e + 1], 0.0).astype(BF16)
                                for e in range(N_EXPERTS)], axis=1) for s in groups]
    y = y_ref[...].reshape(N_EXPERTS * CAP, D_MODEL)
    mixed = [_dot(oh, y) for oh in onehots]
    for s, m in zip(groups, mixed):
        o_ref[0, s, :] = x_ref[0, s, :] + g2_ref[0] * m


def _scatter(pos, aff_t, y, xn, mods3):
    d = D_MODEL
    return pl.pallas_call(
        _scatter_body,
        grid=(BATCH, SEQ // TS),
        in_specs=[pl.BlockSpec((1, N_EXPERTS, TS), lambda b, i: (b, 0, i)),
                  pl.BlockSpec((1, N_EXPERTS, TS), lambda b, i: (b, 0, i)),
                  pl.BlockSpec((N_EXPERTS, CAP, d), lambda b, i: (0, b, 0)),
                  pl.BlockSpec((1, TS, d), lambda b, i: (b, i, 0)),
                  pl.BlockSpec((1, 1, d), lambda b, i: (b, 0, 5))],
        out_specs=pl.BlockSpec((1, TS, d), lambda b, i: (b, i, 0)),
        out_shape=jax.ShapeDtypeStruct((BATCH, SEQ, d), F32),
        compiler_params=_params(("parallel", "parallel")),
        name="scatter_residual",
    )(pos, aff_t, y, xn, mods3)


def _rope_tables():
    rows = SEQ // GRID_W
    row = np.repeat(np.arange(rows, dtype=np.float32), GRID_W)
    col = np.tile(np.arange(GRID_W, dtype=np.float32), rows)
    inv = (ROPE_BASE ** (-np.arange(0, AXIS_ROT, 2, dtype=np.float32) / AXIS_ROT)).astype(np.float32)
    ang = np.concatenate([row[:, None] * inv, col[:, None] * inv], axis=-1).astype(np.float64)
    cos = np.repeat(np.cos(ang), 2, axis=-1)
    sin = np.stack([-np.sin(ang), np.sin(ang)], axis=-1).reshape(SEQ, HEAD_DIM)
    reps = LANES // HEAD_DIM
    return jnp.asarray(np.tile(cos, (1, reps)), F32), jnp.asarray(np.tile(sin, (1, reps)), F32)


def _dft_tables():
    idx = np.arange(HALF, dtype=np.int64)
    t2p1 = 2 * idx + 1

    def cos_sin(f):
        ang = ((f[:, None] * t2p1[None, :]) % (2 * N_FFT)) * (math.pi / N_FFT)
        return np.cos(ang), np.sin(ang)

    ce, se = cos_sin(2 * idx)
    co, so = cos_sin(2 * idx + 1)
    t1 = jnp.asarray(np.concatenate([ce, so], axis=0), F32).astype(BF16)
    t2 = jnp.asarray(np.concatenate([co, se], axis=0), F32).astype(BF16)
    return t1, t2, t1.T, t2.T


def _phase_tables():
    idx = np.arange(HALF, dtype=np.float64)
    w = np.full((HALF,), 2.0 / N_FFT)
    we = w.copy()
    we[0] = 1.0 / N_FFT
    pe = (math.pi / N_FFT) * (2.0 * idx)
    po = (math.pi / N_FFT) * (2.0 * idx + 1.0)
    rot = np.stack([we * np.cos(pe), we * np.sin(pe), w * np.cos(po), w * np.sin(po)])
    return jnp.asarray(np.broadcast_to(rot[:, :, None], (4, HALF, CT)), F32)


def _fold_rows(a):
    return np.concatenate([a[:HALF], a[HALF:][::-1]], axis=0)


def _filter_features():
    t = np.linspace(0.0, 1.0, SEQ, dtype=np.float32).astype(np.float64)[:, None]
    w = 2.0 * math.pi * np.arange(SEQ, dtype=np.float64)[:, None] / SEQ
    fr = np.linspace(1e-4, FILTER_BANDS - 1, FILTER_BANDS, dtype=np.float32).astype(np.float64)[None, :]
    feat = np.concatenate([t, np.cos(fr * w), -np.sin(fr * w)], axis=-1)
    feat = np.pad(feat, ((0, 0), (0, FILTER_HIDDEN - FILTER_EMB)))
    min_decay = math.log(DECAY_TARGET) / SLOW_DECAY_PCT
    max_decay = math.log(DECAY_TARGET) / FAST_DECAY_PCT
    deltas = np.linspace(min_decay, max_decay, HYENA_W, dtype=np.float32).astype(np.float64)
    decay = np.exp(-t * np.abs(deltas))
    return jnp.asarray(_fold_rows(feat).T, F32), jnp.asarray(_fold_rows(decay), F32)


def _attn_bias():
    qi = np.arange(BLOCK)[:, None]
    kj = np.arange(3 * BLOCK)[None, :]
    band = np.abs(kj - BLOCK - qi) <= WINDOW
    first = band & (kj >= BLOCK)
    last = band & (kj < 2 * BLOCK)
    return jnp.asarray(np.where(np.stack([first, band, last]), 0.0, NEG), F32)


def _pair_heads(w, axis):
    heads = [lax.slice_in_dim(w, h * HEAD_DIM, (h + 1) * HEAD_DIM, axis=axis) for h in range(N_HEADS)]
    return jnp.concatenate([heads[j + GROUP * half] for j in range(GROUP) for half in range(N_KV_HEADS)], axis=axis)


def kernel(x, c, ctx, c_ctx, ada_w, ada_b, norm1, norm2, w_in, conv_w, conv_b, filt_w1, filt_b1, filt_w2, filt_b2,
           filt_w3, filt_b3, filt_freq, filt_out, hyena_bias, q_norm, k_norm, attn_sink, w_branch_a, w_branch_b,
           w_out, router, w_gate, w_up, w_down):
    d = D_MODEL
    assert ada_w.shape[0] == 1, "only the single-layer configuration is implemented"
    l = 0
    cos_t, sin_t = _rope_tables()
    t1, t2, t1t, t2t = _dft_tables()
    rot = _phase_tables()
    feat, decay = _filter_features()
    bias = _attn_bias()
    gmat = jnp.asarray(np.kron(np.eye(2 * LANES // HEAD_DIM), np.full((HEAD_DIM, HEAD_DIM), 1.0 / HEAD_DIM)), BF16)
    c16 = jnp.concatenate([c, c_ctx[None, :], jnp.zeros((MOD_ROWS - BATCH - 1, d), F32)], axis=0)

    mods3 = _ada(c16, ada_w[l], ada_b[l][None, :])
    n1 = norm1[l][None, :]
    w_packed = _pack_in_weights(w_in[l])
    gk = jnp.tile(k_norm[l], N_KV_HEADS)[None, :]
    w1_rows = jnp.pad(filt_w1[l], ((0, FILTER_HIDDEN - FILTER_EMB), (0, 0)))
    mlp_w = jnp.swapaxes(jnp.stack([w1_rows, filt_w2[l], filt_w3[l]]), 1, 2)
    mlp_cols = jnp.stack([filt_b1[l], filt_b2[l], filt_b3[l], filt_freq[l]], axis=1)
    kf, kn = _filters(feat, mlp_w, mlp_cols, filt_out[l], decay, rot, t1, t2)
    kc, vc = _ctx_proj(ctx, mods3, n1, w_packed, gk, gmat)
    zh, q, k, v, gates = _inproj(x, mods3, n1, w_packed, jnp.tile(q_norm[l], N_HEADS)[None, :], gk, gmat, cos_t, sin_t)
    ya = _hyena(zh, conv_w[l], conv_b[l][None, :], hyena_bias[l], kf, kn, t1, t2, t1t, t2t)
    yb = _attention(attn_sink[l] * LOG2E, q, k, v, kc, vc, bias)
    xn, h2, aff_t = _merge(ya, yb, gates, x, mods3, norm2[l][None, :], w_branch_a[l].astype(BF16),
                           _pair_heads(w_branch_b[l], 0).astype(BF16), w_out[l].astype(BF16), router[l].T.astype(BF16))
    lo, hi = _thresholds(aff_t.reshape(BATCH * N_EXPERTS, SEQ))
    slot_rows = jnp.asarray(np.broadcast_to(np.arange(CAP)[:, None], (CAP, SEQ)), BF16)
    xin, pos = _route(aff_t, lo, hi, h2, slot_rows)
    y = _experts(xin, w_gate[l], w_up[l], w_down[l])
    return _scatter(pos, aff_t, y, xn, mods3)
```

```python
import math

import numpy as np
import jax
import jax.numpy as jnp
from jax import lax
from jax.experimental import pallas as pl
from jax.experimental.pallas import tpu as pltpu

F32 = jnp.float32
BF16 = jnp.bfloat16
I32 = jnp.int32
HIGHEST = lax.Precision.HIGHEST

D_MODEL = 1024
BATCH = 8
SEQ = 2048
GRID_W = 64
CTX_LEN = 256
N_HEADS = 8
N_KV_HEADS = 2
HEAD_DIM = 64
GROUP = N_HEADS // N_KV_HEADS
ATTN_W = N_HEADS * HEAD_DIM
KV_W = N_KV_HEADS * HEAD_DIM
WINDOW = 128
BLOCK = 128
HYENA_W = D_MODEL // 2
HYENA_ORDER = 2
FILTER_BANDS = 16
FILTER_EMB = 1 + 2 * FILTER_BANDS
FILTER_HIDDEN = 64
DECAY_TARGET = 1e-2
FAST_DECAY_PCT = 0.3
SLOW_DECAY_PCT = 1.5
ROPE_BASE = 10000.0
AXIS_ROT = HEAD_DIM // 2
N_EXPERTS = 16
EC_CAPACITY = 2
D_FF = 2048
EPS = 1e-6
NEG = -1e30
LOG2E = math.log2(math.e)
F32_TINY = float(np.finfo(np.float32).tiny)

OFF_Q = 3 * HYENA_W
OFF_K = OFF_Q + ATTN_W
OFF_V = OFF_K + KV_W
OFF_G = OFF_V + KV_W

CAP = EC_CAPACITY * SEQ // N_EXPERTS
N_FFT = 2 * SEQ
HALF = SEQ // 2
MOD_ROWS = 16
LANES = 128

TM_IN = 1024
SUB_IN = 256
TM_MERGE = 1024
SUB_MERGE = 512
MERGE_BUFS = 3
CT = 256
FC = 512
RB = 256
TF = 512
MC = 512
TS = 1024
SUB_S = 512
CTX_STEP = 4
EXPERT_GROUP = 4
ATTN_UNROLL = 8
VMEM_LIMIT = 56 * 1024 * 1024


def _dot(a, b, precision=None):
    return jnp.dot(a, b, preferred_element_type=F32, precision=precision)


def _dot_nt(a, b, precision=None):
    return lax.dot_general(a, b, (((1,), (1,)), ((), ())), preferred_element_type=F32, precision=precision)


def _params(sem, vmem=VMEM_LIMIT):
    return pltpu.CompilerParams(dimension_semantics=sem, vmem_limit_bytes=vmem)


def _rms_mod(x, g, sc, sh):
    ms = jnp.mean(x * x, axis=-1, keepdims=True)
    return (x * lax.rsqrt(ms + EPS) * g) * (1.0 + sc) + sh


def _head_norm_rope(z, g, gmat, cos, sin, scale):
    ms = _dot((z * z).astype(BF16), gmat)
    y = z * lax.rsqrt(ms + EPS) * g
    if cos is not None:
        slabs = []
        for s in range(z.shape[1] // LANES):
            ys = y[:, s * LANES:(s + 1) * LANES]
            lane = lax.broadcasted_iota(I32, ys.shape, 1)
            nxt = pltpu.roll(ys, LANES - 1, axis=1)
            prv = pltpu.roll(ys, 1, axis=1)
            slabs.append(ys * cos + jnp.where((lane & 1) == 0, nxt, prv) * sin)
        y = slabs[0] if len(slabs) == 1 else jnp.concatenate(slabs, axis=1)
    return y * scale


def _split_bf16(x):
    hi = x.astype(BF16)
    return hi, (x - hi.astype(F32)).astype(BF16)


def _ada_body(c_ref, w_ref, b_ref, o_ref):
    c = c_ref[...]
    s_hi, s_lo = _split_bf16(c * jax.nn.sigmoid(c))
    w_hi, w_lo = _split_bf16(w_ref[...])
    o_ref[:, 0, :] = _dot(s_hi, w_hi) + _dot(s_lo, w_hi) + _dot(s_hi, w_lo) + b_ref[...]


def _ada(c16, w, b):
    d = D_MODEL
    return pl.pallas_call(
        _ada_body,
        grid=(6,),
        in_specs=[pl.BlockSpec((MOD_ROWS, d), lambda j: (0, 0)),
                  pl.BlockSpec((d, d), lambda j: (0, j)),
                  pl.BlockSpec((1, d), lambda j: (0, j))],
        out_specs=pl.BlockSpec((MOD_ROWS, 1, d), lambda j: (0, 0, j)),
        out_shape=jax.ShapeDtypeStruct((MOD_ROWS, 1, 6 * d), F32),
        compiler_params=_params(("parallel",)),
        name="ada_mod",
    )(c16, w, b)


def _sign_rows(n):
    lane = lax.broadcasted_iota(I32, (8, n), 1)
    sub = lax.broadcasted_iota(I32, (8, n), 0)
    sg = jnp.where((lane & 1) == 0, 1.0, -1.0)
    return jnp.where(sub == 0, sg, 0.0).astype(BF16)


def _filt_body(feat_ref, w_ref, col_ref, fof_ref, fob_ref, dec_ref, rot_ref, t1_ref, t2_ref, kf_ref, kn_ref, hs_ref):
    @pl.when((pl.program_id(0) == 0) & (pl.program_id(1) == 0))
    def _():
        cols = col_ref[...]
        fq = cols[:, 3:4]
        h = feat_ref[...]
        for layer in range(3):
            h = jnp.sin(fq * (_dot(w_ref[layer], h, HIGHEST) + cols[:, layer:layer + 1]))
        hi = h.astype(BF16).astype(F32)
        stacked = jnp.concatenate([hi, h - hi, hi, jnp.zeros_like(hi)], axis=0)
        hs_ref[...] = stacked.T.astype(BF16)

    def taps(fo_ref):
        f_hi, f_lo = _split_bf16(fo_ref[...])
        return _dot(hs_ref[...], jnp.concatenate([f_hi, f_hi, f_lo, jnp.zeros_like(f_hi)], axis=0))

    dec = dec_ref[...]
    hf = taps(fof_ref) * dec
    hb = taps(fob_ref) * dec
    row = lax.broadcasted_iota(I32, hf.shape, 0)
    hb = jnp.where(row == 0, 0.0, hb)
    a = hf + hb
    b = hf - hb
    pa = (a[:HALF] + a[HALF:]).astype(BF16)
    ma = (a[:HALF] - a[HALF:]).astype(BF16)
    pb = (b[:HALF] + b[HALF:]).astype(BF16)
    mb = (b[:HALF] - b[HALF:]).astype(BF16)
    t1 = t1_ref[...]
    t2 = t2_ref[...]
    ce, se, co, so = rot_ref[0], rot_ref[1], rot_ref[2], rot_ref[3]
    a1 = _dot(t1, pa)
    a2 = _dot(t2, ma)
    kf_ref[0, 0] = a1[:HALF] * ce + a2[HALF:] * se
    kf_ref[0, 2] = a2[:HALF] * co + a1[HALF:] * so
    b1 = _dot(t1, pb)
    b2 = _dot(t2, mb)
    kf_ref[0, 1] = b2[HALF:] * ce - b1[:HALF] * se
    kf_ref[0, 3] = b1[HALF:] * co - b2[:HALF] * so
    kn_ref[0] = _dot(_sign_rows(HALF), ma)[0:1] * (1.0 / N_FFT)


def _filters(feat, mlp_w, mlp_cols, fout, decay, rot, t1, t2):
    nct = HYENA_W // CT
    full = lambda shape: pl.BlockSpec(shape, lambda o, c: (0,) * len(shape))
    return pl.pallas_call(
        _filt_body,
        grid=(HYENA_ORDER, nct),
        in_specs=[full((FILTER_HIDDEN, SEQ)), full((3, FILTER_HIDDEN, FILTER_HIDDEN)), full((FILTER_HIDDEN, 4)),
                  pl.BlockSpec((FILTER_HIDDEN, CT), lambda o, c: (0, (o * 2 + 0) * nct + c)),
                  pl.BlockSpec((FILTER_HIDDEN, CT), lambda o, c: (0, (o * 2 + 1) * nct + c)),
                  pl.BlockSpec((SEQ, CT), lambda o, c: (0, c)),
                  full((4, HALF, CT)),
                  pl.BlockSpec((SEQ, HALF), lambda o, c: (0, 0), pipeline_mode=pl.Buffered(1)),
                  pl.BlockSpec((SEQ, HALF), lambda o, c: (0, 0), pipeline_mode=pl.Buffered(1))],
        out_specs=[pl.BlockSpec((1, 4, HALF, CT), lambda o, c: (o, 0, 0, c)),
                   pl.BlockSpec((1, 1, CT), lambda o, c: (o, 0, c))],
        out_shape=[jax.ShapeDtypeStruct((HYENA_ORDER, 4, HALF, HYENA_W), F32),
                   jax.ShapeDtypeStruct((HYENA_ORDER, 1, HYENA_W), F32)],
        scratch_shapes=[pltpu.VMEM((SEQ, 4 * FILTER_HIDDEN), BF16)],
        compiler_params=_params(("arbitrary", "arbitrary")),
        name="hyena_filters",
    )(feat, mlp_w, mlp_cols, fout, fout, decay, rot, t1, t2)


def _pair_head_lanes(z):
    slabs = [z[:, s * LANES:(s + 1) * LANES] for s in range(ATTN_W // LANES)]
    swapped = [pltpu.roll(sl, HEAD_DIM, axis=1) for sl in slabs]
    low = lax.broadcasted_iota(I32, slabs[0].shape, 1) < HEAD_DIM
    out = []
    for j in range(GROUP):
        first, second = j, j + GROUP
        lo_src = slabs[first // 2] if first % 2 == 0 else swapped[first // 2]
        hi_src = slabs[second // 2] if second % 2 == 1 else swapped[second // 2]
        out.append(jnp.where(low, lo_src, hi_src))
    return jnp.concatenate(out, axis=1)


def _inproj_body(x_ref, sc_ref, sh_ref, n1_ref, wh_ref, wq_ref, wkv_ref, wg_ref, gq_ref, gk_ref, gmat_ref,
                 cos_ref, sin_ref, zh_ref, q_ref, k_ref, v_ref, gate_ref):
    groups = [slice(i * SUB_IN, (i + 1) * SUB_IN) for i in range(TM_IN // SUB_IN)]
    hx = [_rms_mod(x_ref[0, s, :], n1_ref[...], sc_ref[0], sh_ref[0]).astype(BF16) for s in groups]
    for s, h in zip(groups, hx):
        zh_ref[0, s, :] = _dot(h, wh_ref[...]).astype(BF16)
    pair = 2 * LANES
    zq = [_pair_head_lanes(_dot(h, wq_ref[...])) for h in hx]
    for s, z in zip(groups, zq):
        for c in range(ATTN_W // pair):
            sl = slice(c * pair, (c + 1) * pair)
            q_ref[0, s, sl] = _head_norm_rope(z[:, sl], gq_ref[:, sl], gmat_ref[...], cos_ref[s, :], sin_ref[s, :],
                                              LOG2E * HEAD_DIM ** -0.5).astype(BF16)
    zkv = [_dot(h, wkv_ref[...]) for h in hx]
    for s, z in zip(groups, zkv):
        k_ref[0, s, :] = _head_norm_rope(z[:, :KV_W], gk_ref[...], gmat_ref[0:KV_W, 0:KV_W], cos_ref[s, :],
                                         sin_ref[s, :], 1.0).astype(BF16)
        v_ref[0, s, :] = z[:, KV_W:].astype(BF16)
    for s, h in zip(groups, hx):
        gate_ref[0, s, :] = jax.nn.sigmoid(_dot(h, wg_ref[...])).astype(BF16)


def _pack_in_weights(w):
    wb = w.astype(BF16)
    return jnp.concatenate([wb[:, :OFF_K], wb[:, OFF_G:], wb[:, OFF_K:OFF_G]], axis=1)


W_OFF_H = 0
W_OFF_Q = OFF_Q
W_OFF_G = OFF_Q + ATTN_W
W_OFF_KV = W_OFF_G + 2 * D_MODEL


def _inproj(x, mods3, norm1, w_packed, gq, gk, gmat, cos_t, sin_t):
    d = D_MODEL
    nt = SEQ // TM_IN
    const = lambda shape: pl.BlockSpec(shape, lambda b, i: (0,) * len(shape))

    def wcol(width, off):
        assert off % width == 0
        return pl.BlockSpec((d, width), lambda b, i: (0, off // width))

    tok = lambda w: pl.BlockSpec((1, TM_IN, w), lambda b, i: (b, i, 0))
    return pl.pallas_call(
        _inproj_body,
        grid=(BATCH, nt),
        in_specs=[tok(d),
                  pl.BlockSpec((1, 1, d), lambda b, i: (b, 0, 1)),
                  pl.BlockSpec((1, 1, d), lambda b, i: (b, 0, 0)),
                  const((1, d)), wcol(OFF_Q, W_OFF_H), wcol(ATTN_W, W_OFF_Q), wcol(2 * KV_W, W_OFF_KV),
                  wcol(2 * d, W_OFF_G), const((1, ATTN_W)), const((1, KV_W)), const((2 * LANES, 2 * LANES)),
                  pl.BlockSpec((TM_IN, LANES), lambda b, i: (i, 0)),
                  pl.BlockSpec((TM_IN, LANES), lambda b, i: (i, 0))],
        out_specs=[tok(OFF_Q), tok(ATTN_W), tok(KV_W), tok(KV_W), tok(2 * d)],
        out_shape=[jax.ShapeDtypeStruct((BATCH, SEQ, OFF_Q), BF16),
                   jax.ShapeDtypeStruct((BATCH, SEQ, ATTN_W), BF16),
                   jax.ShapeDtypeStruct((BATCH, SEQ, KV_W), BF16),
                   jax.ShapeDtypeStruct((BATCH, SEQ, KV_W), BF16),
                   jax.ShapeDtypeStruct((BATCH, SEQ, 2 * d), BF16)],
        compiler_params=_params(("parallel", "parallel")),
        name="in_proj",
    )(x, mods3, mods3, norm1, w_packed, w_packed, w_packed, w_packed, gq, gk, gmat, cos_t, sin_t)


def _ctx_body(c_ref, sc_ref, sh_ref, n1_ref, wkv_ref, gk_ref, gmat_ref, kc_ref, vc_ref):
    for i in range(CTX_STEP):
        hc = _rms_mod(c_ref[i], n1_ref[...], sc_ref[0], sh_ref[0]).astype(BF16)
        z = _dot(hc, wkv_ref[...])
        kc_ref[i] = _head_norm_rope(z[:, :KV_W], gk_ref[...], gmat_ref[0:KV_W, 0:KV_W], None, None, 1.0).astype(BF16)
        vc_ref[i] = z[:, KV_W:].astype(BF16)


def _ctx_proj(ctx, mods3, norm1, w_packed, gk, gmat):
    d = D_MODEL
    const = lambda shape: pl.BlockSpec(shape, lambda b: (0,) * len(shape))
    return pl.pallas_call(
        _ctx_body,
        grid=(BATCH // CTX_STEP,),
        in_specs=[pl.BlockSpec((CTX_STEP, CTX_LEN, d), lambda b: (b, 0, 0)),
                  pl.BlockSpec((1, 1, d), lambda b: (BATCH, 0, 1)),
                  pl.BlockSpec((1, 1, d), lambda b: (BATCH, 0, 0)),
                  const((1, d)), pl.BlockSpec((d, 2 * KV_W), lambda b: (0, W_OFF_KV // (2 * KV_W))),
                  const((1, KV_W)), const((2 * LANES, 2 * LANES))],
        out_specs=[pl.BlockSpec((CTX_STEP, CTX_LEN, KV_W), lambda b: (b, 0, 0)),
                   pl.BlockSpec((CTX_STEP, CTX_LEN, KV_W), lambda b: (b, 0, 0))],
        out_shape=[jax.ShapeDtypeStruct((BATCH, CTX_LEN, KV_W), BF16),
                   jax.ShapeDtypeStruct((BATCH, CTX_LEN, KV_W), BF16)],
        compiler_params=_params(("parallel",)),
        name="ctx_proj",
    )(ctx, mods3, mods3, norm1, w_packed, gk, gmat)


def _hyena_body(zv_ref, z1_ref, z2_ref, cwv_ref, cw1_ref, cw2_ref, cbv_ref, cb1_ref, cb2_ref, hb_ref, kf_ref,
                kn_ref, t1_ref, t2_ref, t1t_ref, t2t_ref, o_ref, lo_ref, hi_ref, g0lo_ref, g0hi_ref, g1lo_ref, g1hi_ref,
                p_ref, m_ref, za_ref, zb_ref):
    row = lax.broadcasted_iota(I32, (HALF, CT), 0)
    rr = lax.broadcasted_iota(I32, (RB, RB), 0)
    cc = lax.broadcasted_iota(I32, (RB, RB), 1)
    flip = jnp.where(rr + cc == RB - 1, 1.0, 0.0).astype(BF16)
    nrb = HALF // RB

    def reverse_upper_half(z_ref, hi_out):
        for j in range(nrb):
            hi_out[j * RB:(j + 1) * RB, :] = _dot(flip, z_ref[0, SEQ - RB * (j + 1):SEQ - RB * j, :])

    def folded_short_conv(z_ref, w_ref, b_ref, lo_out, hi_out):
        zlo = z_ref[0, 0:HALF, :].astype(F32)
        zhi = hi_out[...]
        w0, w1, w2 = w_ref[0:1, :], w_ref[1:2, :], w_ref[2:3, :]
        first, last = row == 0, row == HALF - 1
        lo_prev = jnp.where(first, 0.0, pltpu.roll(zlo, 1, axis=0))
        lo_next = jnp.where(last, zhi[HALF - 1:HALF, :], pltpu.roll(zlo, HALF - 1, axis=0))
        hi_prev = jnp.where(first, 0.0, pltpu.roll(zhi, 1, axis=0))
        hi_next = jnp.where(last, zlo[HALF - 1:HALF, :], pltpu.roll(zhi, HALF - 1, axis=0))
        lo_out[...] = lo_prev * w0 + zlo * w1 + lo_next * w2 + b_ref[...]
        hi_out[...] = hi_next * w0 + zhi * w1 + hi_prev * w2 + b_ref[...]

    gates = ((g0lo_ref, g0hi_ref), (g1lo_ref, g1hi_ref))
    reverse_upper_half(zv_ref, hi_ref)
    reverse_upper_half(z1_ref, g0hi_ref)
    reverse_upper_half(z2_ref, g1hi_ref)
    folded_short_conv(zv_ref, cwv_ref, cbv_ref, lo_ref, hi_ref)
    folded_short_conv(z1_ref, cw1_ref, cb1_ref, g0lo_ref, g0hi_ref)
    folded_short_conv(z2_ref, cw2_ref, cb2_ref, g1lo_ref, g1hi_ref)
    sign8 = _sign_rows(HALF)
    odd = (lax.broadcasted_iota(I32, (FC, CT), 0) & 1) == 1
    for o, (glo_ref, ghi_ref) in enumerate(gates):
        p_ref[...] = (lo_ref[...] + hi_ref[...]).astype(BF16)
        m_ref[...] = (lo_ref[...] - hi_ref[...]).astype(BF16)
        pv = p_ref[...]
        mv = m_ref[...]
        for c in range(HALF // FC):
            ev = slice(c * FC, (c + 1) * FC)
            od = slice(HALF + c * FC, HALF + (c + 1) * FC)
            xce = _dot(t1_ref[ev, :], pv)
            xse = _dot(t2_ref[od, :], mv)
            kce, kse = kf_ref[o, 0, ev, :], kf_ref[o, 1, ev, :]
            za_ref[ev, :] = (xce * kce - xse * kse).astype(BF16)
            zb_ref[od, :] = (xce * kse + xse * kce).astype(BF16)
            xco = _dot(t2_ref[ev, :], mv)
            xso = _dot(t1_ref[od, :], pv)
            kco, kso = kf_ref[o, 2, ev, :], kf_ref[o, 3, ev, :]
            za_ref[od, :] = (xco * kso + xso * kco).astype(BF16)
            zb_ref[ev, :] = (xco * kco - xso * kso).astype(BF16)
        zn = _dot(sign8, mv)[0:1] * kn_ref[o]
        bias = hb_ref[o:o + 1, :]
        za = za_ref[...]
        zb = zb_ref[...]
        for c in range(HALF // FC):
            rs = slice(c * FC, (c + 1) * FC)
            half_p = _dot(t1t_ref[rs, :], za)
            half_m = _dot(t2t_ref[rs, :], zb) + jnp.where(odd, -zn, zn)
            lo_ref[rs, :] = glo_ref[rs, :] * (half_p + half_m + bias * lo_ref[rs, :])
            hi_ref[rs, :] = ghi_ref[rs, :] * (half_p - half_m + bias * hi_ref[rs, :])
    o_ref[0, 0:HALF, :] = lo_ref[...].astype(BF16)
    for j in range(nrb):
        o_ref[0, SEQ - RB * (j + 1):SEQ - RB * j, :] = _dot(
            flip, hi_ref[j * RB:(j + 1) * RB, :].astype(BF16)).astype(BF16)


def _hyena(zh, conv_w, conv_b, hbias, kf, kn, t1, t2, t1t, t2t):
    nct = HYENA_W // CT
    zspec = lambda k: pl.BlockSpec((1, SEQ, CT), lambda c, b: (b, 0, k * nct + c))
    wspec = lambda k: pl.BlockSpec((3, CT), lambda c, b: (0, k * nct + c))
    bspec = lambda k: pl.BlockSpec((1, CT), lambda c, b: (0, k * nct + c))
    table = lambda shape: pl.BlockSpec(shape, lambda c, b: (0, 0), pipeline_mode=pl.Buffered(1))
    half_f32 = pltpu.VMEM((HALF, CT), F32)
    return pl.pallas_call(
        _hyena_body,
        grid=(nct, BATCH),
        in_specs=[zspec(0), zspec(1), zspec(2), wspec(0), wspec(1), wspec(2), bspec(0), bspec(1), bspec(2),
                  pl.BlockSpec((HYENA_ORDER, CT), lambda c, b: (0, c)),
                  pl.BlockSpec((HYENA_ORDER, 4, HALF, CT), lambda c, b: (0, 0, 0, c)),
                  pl.BlockSpec((HYENA_ORDER, 1, CT), lambda c, b: (0, 0, c)),
                  table((SEQ, HALF)), table((SEQ, HALF)), table((HALF, SEQ)), table((HALF, SEQ))],
        out_specs=pl.BlockSpec((1, SEQ, CT), lambda c, b: (b, 0, c)),
        out_shape=jax.ShapeDtypeStruct((BATCH, SEQ, HYENA_W), BF16),
        scratch_shapes=[half_f32] * 6 + [
            pltpu.VMEM((HALF, CT), BF16), pltpu.VMEM((HALF, CT), BF16),
            pltpu.VMEM((SEQ, CT), BF16), pltpu.VMEM((SEQ, CT), BF16)],
        compiler_params=_params(("parallel", "parallel")),
        name="hyena_conv",
    )(zh, zh, zh, conv_w, conv_w, conv_w, conv_b, conv_b, conv_b, hbias, kf, kn, t1, t2, t1t, t2t)


def _attn_body(sink_ref, q_ref, k_ref, v_ref, kc_ref, vc_ref, bias_ref, o_ref, kp_ref, vlo_ref, vhi_ref):
    nb = SEQ // BLOCK
    lane = lax.broadcasted_iota(I32, (BLOCK, LANES), 1)
    low = lane < HEAD_DIM
    mask_lo = jnp.where(low, 1.0, 0.0).astype(BF16)
    mask_hi = jnp.where(low, 0.0, 1.0).astype(BF16)

    def with_ones(v):
        lane_v = lax.broadcasted_iota(I32, v.shape, 1) < HEAD_DIM
        one = jnp.ones_like(v)
        return jnp.where(lane_v, v, one), jnp.where(lane_v, one, v)

    zpad = jnp.zeros((BLOCK, KV_W), BF16)
    kp_ref[0:BLOCK] = zpad
    kp_ref[BLOCK:BLOCK + SEQ] = k_ref[0]
    kp_ref[BLOCK + SEQ:] = zpad
    v_lo, v_hi = with_ones(v_ref[0])
    for ref, val in ((vlo_ref, v_lo), (vhi_ref, v_hi)):
        ref[0:BLOCK] = zpad
        ref[BLOCK:BLOCK + SEQ] = val
        ref[BLOCK + SEQ:] = zpad
    kc = kc_ref[0]
    vc_pair = with_ones(vc_ref[0])

    def scores_of(n):
        r = pl.multiple_of(n * BLOCK, BLOCK)
        kw = kp_ref[pl.ds(r, 3 * BLOCK), :]
        bias = bias_ref[jnp.where(n == 0, 0, jnp.where(n == nb - 1, 2, 1))]
        scores = []
        for j in range(GROUP):
            qs = q_ref[0, pl.ds(r, BLOCK), j * LANES:(j + 1) * LANES]
            for msk in (mask_lo, mask_hi):
                qm = qs * msk
                scores.append((_dot_nt(qm, kw) + bias, _dot_nt(qm, kc)))
        return scores

    def finish(n, scores):
        r = pl.multiple_of(n * BLOCK, BLOCK)
        vw_pair = (vlo_ref[pl.ds(r, 3 * BLOCK), :], vhi_ref[pl.ds(r, 3 * BLOCK), :])
        probs = []
        for idx, (sw, sc) in enumerate(scores):
            snk = sink_ref[idx // 2 + GROUP * (idx % 2)]
            m = jnp.maximum(jnp.maximum(jnp.max(sw, axis=-1, keepdims=True),
                                        jnp.max(sc, axis=-1, keepdims=True)), snk)
            probs.append((jnp.exp2(sw - m).astype(BF16), jnp.exp2(sc - m).astype(BF16), jnp.exp2(snk - m)))
        outs = []
        for idx, (pw, pc, psink) in enumerate(probs):
            acc = _dot(pw, vw_pair[idx % 2]) + _dot(pc, vc_pair[idx % 2])
            den = pltpu.roll(acc, HEAD_DIM, axis=1) + psink
            outs.append(acc / den)
        for j in range(GROUP):
            o_ref[0, pl.ds(r, BLOCK), j * LANES:(j + 1) * LANES] = jnp.where(
                low, outs[2 * j], outs[2 * j + 1]).astype(BF16)

    def block_group(g, carry):
        first = g * ATTN_UNROLL
        pending = scores_of(first)
        for i in range(ATTN_UNROLL):
            nxt = scores_of(first + i + 1) if i + 1 < ATTN_UNROLL else None
            finish(first + i, pending)
            pending = nxt
        return carry

    lax.fori_loop(0, nb // ATTN_UNROLL, block_group, 0)


def _attention(sink, q, k, v, kc, vc, bias):
    per_b = lambda n, w: pl.BlockSpec((1, n, w), lambda b: (b, 0, 0))
    return pl.pallas_call(
        _attn_body,
        grid=(BATCH,),
        in_specs=[pl.BlockSpec(memory_space=pltpu.SMEM),
                  per_b(SEQ, ATTN_W), per_b(SEQ, KV_W), per_b(SEQ, KV_W), per_b(CTX_LEN, KV_W), per_b(CTX_LEN, KV_W),
                  pl.BlockSpec((3, BLOCK, 3 * BLOCK), lambda b: (0, 0, 0))],
        out_specs=per_b(SEQ, ATTN_W),
        out_shape=jax.ShapeDtypeStruct((BATCH, SEQ, ATTN_W), BF16),
        scratch_shapes=[pltpu.VMEM((SEQ + 2 * BLOCK, KV_W), BF16)] * 3,
        compiler_params=_params(("parallel",)),
        name="window_attn",
    )(sink, q, k, v, kc, vc, bias)


def _merge_input_copies(gt_hbm, x_hbm, gt_buf, x_buf, sems, step, slot):
    nt = SEQ // TM_MERGE
    rows = pl.ds(pl.multiple_of((step % nt) * TM_MERGE, TM_MERGE), TM_MERGE)
    return (pltpu.make_async_copy(gt_hbm.at[step // nt, rows, :], gt_buf.at[slot], sems.at[0, slot]),
            pltpu.make_async_copy(x_hbm.at[step // nt, rows, :], x_buf.at[slot], sems.at[1, slot]))


def _merge_body(ya_ref, yb_ref, gt_hbm, x_hbm, g1_ref, sc2_ref, sh2_ref, n2_ref, wa_ref, wb_ref, wo_ref, rt_ref,
                xn_ref, h2_ref, aff_ref, gt_buf, x_buf, sems):
    d = D_MODEL
    nt = SEQ // TM_MERGE
    step = pl.program_id(0) * nt + pl.program_id(1)
    ahead = MERGE_BUFS - 1

    @pl.when(step == 0)
    def _():
        for s in range(ahead):
            for cp in _merge_input_copies(gt_hbm, x_hbm, gt_buf, x_buf, sems, s, s):
                cp.start()

    @pl.when(step + ahead < BATCH * nt)
    def _():
        for cp in _merge_input_copies(gt_hbm, x_hbm, gt_buf, x_buf, sems, step + ahead, (step + ahead) % MERGE_BUFS):
            cp.start()

    slot = step % MERGE_BUFS
    for cp in _merge_input_copies(gt_hbm, x_hbm, gt_buf, x_buf, sems, step, slot):
        cp.wait()
    gt_ref = gt_buf.at[slot]
    x_ref = x_buf.at[slot]
    groups = [slice(i * SUB_MERGE, (i + 1) * SUB_MERGE) for i in range(TM_MERGE // SUB_MERGE)]
    branch = [(_dot(ya_ref[0, s, :], wa_ref[...]), _dot(yb_ref[0, s, :], wb_ref[...])) for s in groups]
    mixed = [(gt_ref[s, :d].astype(F32) * ua + gt_ref[s, d:].astype(F32) * ub).astype(BF16)
             for s, (ua, ub) in zip(groups, branch)]
    proj = [_dot(u, wo_ref[...]) for u in mixed]
    h2_rows = []
    for s, m in zip(groups, proj):
        xn = x_ref[s, :] + g1_ref[0] * m
        xn_ref[0, s, :] = xn
        h2 = _rms_mod(xn, n2_ref[...], sc2_ref[0], sh2_ref[0]).astype(BF16)
        h2_ref[0, s, :] = h2
        h2_rows.append(h2)
    logits = [_dot_nt(rt_ref[...], h2) for h2 in h2_rows]
    for s, lt in zip(groups, logits):
        e = jnp.exp(lt - jnp.max(lt, axis=0, keepdims=True))
        aff_ref[0, :, s] = e / jnp.sum(e, axis=0, keepdims=True)


def _merge(ya, yb, gates, x, mods3, norm2, wa, wb, wo, router_t):
    d = D_MODEL
    nt = SEQ // TM_MERGE
    const = lambda shape: pl.BlockSpec(shape, lambda b, i: (0,) * len(shape))
    tok = lambda w: pl.BlockSpec((1, TM_MERGE, w), lambda b, i: (b, i, 0))
    mod = lambda k: pl.BlockSpec((1, 1, d), lambda b, i: (b, 0, k))
    return pl.pallas_call(
        _merge_body,
        grid=(BATCH, nt),
        in_specs=[tok(HYENA_W), tok(ATTN_W), pl.BlockSpec(memory_space=pl.ANY), pl.BlockSpec(memory_space=pl.ANY),
                  mod(2), mod(4), mod(3), const((1, d)),
                  const((HYENA_W, d)), const((ATTN_W, d)), const((d, d)), const((N_EXPERTS, d))],
        out_specs=[tok(d), tok(d), pl.BlockSpec((1, N_EXPERTS, TM_MERGE), lambda b, i: (b, 0, i))],
        out_shape=[jax.ShapeDtypeStruct((BATCH, SEQ, d), F32),
                   jax.ShapeDtypeStruct((BATCH, SEQ, d), BF16),
                   jax.ShapeDtypeStruct((BATCH, N_EXPERTS, SEQ), F32)],
        scratch_shapes=[pltpu.VMEM((MERGE_BUFS, TM_MERGE, 2 * d), BF16), pltpu.VMEM((MERGE_BUFS, TM_MERGE, d), F32),
                        pltpu.SemaphoreType.DMA((2, MERGE_BUFS))],
        compiler_params=_params(("arbitrary", "arbitrary")),
        name="merge_outproj",
    )(ya, yb, gates, x, mods3, mods3, mods3, norm2, wa, wb, wo, router_t)


def _thresh_body(aff_ref, lo_ref, hi_ref):
    aff = aff_ref[...]
    rows = aff.shape[0]
    bits = pltpu.bitcast(aff, I32)

    def bisect_bits(_, carry):
        lo, hi = carry
        mid = lo + ((hi - lo) >> 1)
        ge = jnp.sum((bits >= mid).astype(I32), axis=1, keepdims=True) >= CAP
        return jnp.where(ge, mid, lo), jnp.where(ge, hi, mid)

    lo0 = jnp.zeros((rows, 1), I32)
    hi0 = jnp.full((rows, 1), 0x3F800001, I32)
    thr_bits, _ = lax.fori_loop(0, 31, bisect_bits, (lo0, hi0))
    thr = pltpu.bitcast(thr_bits, F32)

    def bisect_val(_, carry):
        lo, hi = carry
        mid = 0.5 * (lo + hi)
        ge = jnp.sum(jnp.where(aff >= mid, 1.0, 0.0), axis=1, keepdims=True) >= CAP
        return jnp.where(ge, mid, lo), jnp.where(ge, hi, mid)

    top = jnp.where(thr > 0.0, 2.0 * thr, F32_TINY)
    lo, hi = lax.fori_loop(0, 30, bisect_val, (0.5 * thr, top))
    lo_ref[...] = jnp.broadcast_to(lo, lo_ref.shape)
    hi_ref[...] = jnp.broadcast_to(hi, hi_ref.shape)


def _thresholds(aff_rows):
    rows = BATCH * N_EXPERTS
    return pl.pallas_call(
        _thresh_body,
        grid=(1,),
        in_specs=[pl.BlockSpec((rows, SEQ), lambda i: (0, 0))],
        out_specs=[pl.BlockSpec((rows, LANES), lambda i: (0, 0)), pl.BlockSpec((rows, LANES), lambda i: (0, 0))],
        out_shape=[jax.ShapeDtypeStruct((rows, LANES), F32), jax.ShapeDtypeStruct((rows, LANES), F32)],
        compiler_params=_params(("arbitrary",)),
        name="route_threshold",
    )(aff_rows)


def _prefix_counts(mask):
    r = lax.broadcasted_iota(I32, (LANES, LANES), 0)
    c = lax.broadcasted_iota(I32, (LANES, LANES), 1)
    upper = jnp.where(r <= c, 1.0, 0.0).astype(BF16)
    offset = jnp.zeros((mask.shape[0], 1), F32)
    blocks = []
    for j in range(mask.shape[1] // LANES):
        blk = mask[:, j * LANES:(j + 1) * LANES]
        inc = _dot(blk.astype(BF16), upper)
        blocks.append(inc - blk + offset)
        offset = offset + inc[:, LANES - 1:LANES]
    return jnp.concatenate(blocks, axis=1)


def _route_body(aff_ref, lo_ref, hi_ref, h_ref, slot_ref, xin_ref, pos_ref, pbuf_ref):
    aff = aff_ref[0]
    above = jnp.where(aff >= hi_ref[:, 0:1], 1.0, 0.0)
    band = jnp.where(aff >= lo_ref[:, 0:1], 1.0, 0.0) - above
    need = CAP - jnp.sum(above, axis=1, keepdims=True)
    tie_rank = _prefix_counts(band)
    self = above + band * jnp.where(tie_rank < need, 1.0, 0.0)
    pos = _prefix_counts(self)
    posi = jnp.where(self > 0.5, pos.astype(I32), -1)
    pos_ref[0] = posi

    h = h_ref[0]
    slot = slot_ref[...]
    posb = jnp.where(self > 0.5, pos, -1.0).astype(BF16)
    one = jnp.ones((CAP, SEQ), BF16)
    zero = jnp.zeros((CAP, SEQ), BF16)
    for grp in range(N_EXPERTS // EXPERT_GROUP):
        for i in range(EXPERT_GROUP):
            e = grp * EXPERT_GROUP + i
            pbuf_ref[i * CAP:(i + 1) * CAP, :] = jnp.where(posb[e:e + 1, :] == slot, one, zero)
        rows = _dot(pbuf_ref[...], h)
        for i in range(EXPERT_GROUP):
            xin_ref[grp * EXPERT_GROUP + i] = rows[i * CAP:(i + 1) * CAP].astype(BF16)


def _route(aff_t, lo, hi, h2, slot_rows):
    d = D_MODEL
    return pl.pallas_call(
        _route_body,
        grid=(BATCH,),
        in_specs=[pl.BlockSpec((1, N_EXPERTS, SEQ), lambda b: (b, 0, 0)),
                  pl.BlockSpec((N_EXPERTS, LANES), lambda b: (b, 0)),
                  pl.BlockSpec((N_EXPERTS, LANES), lambda b: (b, 0)),
                  pl.BlockSpec((1, SEQ, d), lambda b: (b, 0, 0)),
                  pl.BlockSpec((CAP, SEQ), lambda b: (0, 0))],
        out_specs=[pl.BlockSpec((N_EXPERTS, CAP, d), lambda b: (0, b, 0)),
                   pl.BlockSpec((1, N_EXPERTS, SEQ), lambda b: (b, 0, 0))],
        out_shape=[jax.ShapeDtypeStruct((N_EXPERTS, BATCH * CAP, d), BF16),
                   jax.ShapeDtypeStruct((BATCH, N_EXPERTS, SEQ), I32)],
        scratch_shapes=[pltpu.VMEM((EXPERT_GROUP * CAP, SEQ), BF16)],
        compiler_params=_params(("parallel",)),
        name="route_gather",
    )(aff_t, lo, hi, h2, slot_rows)


def _expert_body(x_ref, wg_ref, wu_ref, wd_ref, o_ref, acc_ref, wgb_ref, wub_ref, wdb_ref):
    f = pl.program_id(1)

    @pl.when((pl.program_id(0) == 0) & (f == 0))
    def _():
        acc_ref[...] = jnp.zeros_like(acc_ref)

    carry = f > 0
    for m in range(BATCH * CAP // MC):
        rows = slice(m * MC, (m + 1) * MC)
        xm = x_ref[0, rows, :]
        if m == 0:
            wgb_ref[...] = wg_ref[0].astype(BF16)
        a = _dot(xm, wgb_ref[...])
        if m == 0:
            wub_ref[...] = wu_ref[0].astype(BF16)
        b = _dot(xm, wub_ref[...])
        hh = (a * jax.nn.sigmoid(a) * b).astype(BF16)
        if m == 0:
            wdb_ref[...] = wd_ref[0].astype(BF16)
        total = jnp.where(carry, acc_ref[rows, :], 0.0) + _dot(hh, wdb_ref[...])
        acc_ref[rows, :] = total
        o_ref[0, rows, :] = total.astype(BF16)


def _experts(xin, w_gate, w_up, w_down):
    d = D_MODEL
    rows = BATCH * CAP
    return pl.pallas_call(
        _expert_body,
        grid=(N_EXPERTS, D_FF // TF),
        in_specs=[pl.BlockSpec((1, rows, d), lambda e, f: (e, 0, 0)),
                  pl.BlockSpec((1, d, TF), lambda e, f: (e, 0, f)),
                  pl.BlockSpec((1, d, TF), lambda e, f: (e, 0, f)),
                  pl.BlockSpec((1, TF, d), lambda e, f: (e, f, 0))],
        out_specs=pl.BlockSpec((1, rows, d), lambda e, f: (e, 0, 0)),
        out_shape=jax.ShapeDtypeStruct((N_EXPERTS, rows, d), BF16),
        scratch_shapes=[pltpu.VMEM((rows, d), F32), pltpu.VMEM((d, TF), BF16), pltpu.VMEM((d, TF), BF16),
                        pltpu.VMEM((TF, d), BF16)],
        compiler_params=_params(("parallel", "arbitrary")),
        name="swiglu_experts",
    )(xin, w_gate, w_up, w_down)


def _scatter_body(pos_ref, aff_ref, y_ref, x_ref, g2_ref, o_ref):
    posb = pos_ref[0].astype(F32).astype(BF16)
    affb = aff_ref[0].astype(BF16)
    slot = lax.broadcasted_iota(I32, (CAP, SUB_S), 0).astype(F32).astype(BF16)
    zero = jnp.zeros((CAP, SUB_S), BF16)
    groups = [slice(i * SUB_S, (i + 1) * SUB_S) for i in range(TS // SUB_S)]
    onehots = [jnp.concatenate([jnp.where(posb[e:e + 1, s] == slot, jnp.broadcast_to(affb[e:e + 1, s], (CAP, SUB_S)), zero)
                                for e in range(N_EXPERTS)], axis=0) for s in groups]
    y = y_ref[...].reshape(N_EXPERTS * CAP, D_MODEL)
    mixed = [lax.dot_general(oh, y, (((0,), (0,)), ((), ())), preferred_element_type=F32) for oh in onehots]
    for s, m in zip(groups, mixed):
        o_ref[0, s, :] = x_ref[0, s, :] + g2_ref[0] * m


def _scatter(pos, aff_t, y, xn, mods3):
    d = D_MODEL
    return pl.pallas_call(
        _scatter_body,
        grid=(BATCH, SEQ // TS),
        in_specs=[pl.BlockSpec((1, N_EXPERTS, TS), lambda b, i: (b, 0, i)),
                  pl.BlockSpec((1, N_EXPERTS, TS), lambda b, i: (b, 0, i)),
                  pl.BlockSpec((N_EXPERTS, CAP, d), lambda b, i: (0, b, 0)),
                  pl.BlockSpec((1, TS, d), lambda b, i: (b, i, 0)),
                  pl.BlockSpec((1, 1, d), lambda b, i: (b, 0, 5))],
        out_specs=pl.BlockSpec((1, TS, d), lambda b, i: (b, i, 0)),
        out_shape=jax.ShapeDtypeStruct((BATCH, SEQ, d), F32),
        compiler_params=_params(("parallel", "parallel")),
        name="scatter_residual",
    )(pos, aff_t, y, xn, mods3)


def _rope_tables():
    rows = SEQ // GRID_W
    row = np.repeat(np.arange(rows, dtype=np.float32), GRID_W)
    col = np.tile(np.arange(GRID_W, dtype=np.float32), rows)
    inv = (ROPE_BASE ** (-np.arange(0, AXIS_ROT, 2, dtype=np.float32) / AXIS_ROT)).astype(np.float32)
    ang = np.concatenate([row[:, None] * inv, col[:, None] * inv], axis=-1).astype(np.float64)
    cos = np.repeat(np.cos(ang), 2, axis=-1)
    sin = np.stack([-np.sin(ang), np.sin(ang)], axis=-1).reshape(SEQ, HEAD_DIM)
    reps = LANES // HEAD_DIM
    return jnp.asarray(np.tile(cos, (1, reps)), F32), jnp.asarray(np.tile(sin, (1, reps)), F32)


def _dft_tables():
    idx = np.arange(HALF, dtype=np.int64)
    t2p1 = 2 * idx + 1

    def cos_sin(f):
        ang = ((f[:, None] * t2p1[None, :]) % (2 * N_FFT)) * (math.pi / N_FFT)
        return np.cos(ang), np.sin(ang)

    ce, se = cos_sin(2 * idx)
    co, so = cos_sin(2 * idx + 1)
    t1 = jnp.asarray(np.concatenate([ce, so], axis=0), F32).astype(BF16)
    t2 = jnp.asarray(np.concatenate([co, se], axis=0), F32).astype(BF16)
    return t1, t2, t1.T, t2.T


def _phase_tables():
    idx = np.arange(HALF, dtype=np.float64)
    w = np.full((HALF,), 2.0 / N_FFT)
    we = w.copy()
    we[0] = 1.0 / N_FFT
    pe = (math.pi / N_FFT) * (2.0 * idx)
    po = (math.pi / N_FFT) * (2.0 * idx + 1.0)
    rot = np.stack([we * np.cos(pe), we * np.sin(pe), w * np.cos(po), w * np.sin(po)])
    return jnp.asarray(np.broadcast_to(rot[:, :, None], (4, HALF, CT)), F32)


def _fold_rows(a):
    return np.concatenate([a[:HALF], a[HALF:][::-1]], axis=0)


def _filter_features():
    t = np.linspace(0.0, 1.0, SEQ, dtype=np.float32).astype(np.float64)[:, None]
    w = 2.0 * math.pi * np.arange(SEQ, dtype=np.float64)[:, None] / SEQ
    fr = np.linspace(1e-4, FILTER_BANDS - 1, FILTER_BANDS, dtype=np.float32).astype(np.float64)[None, :]
    feat = np.concatenate([t, np.cos(fr * w), -np.sin(fr * w)], axis=-1)
    feat = np.pad(feat, ((0, 0), (0, FILTER_HIDDEN - FILTER_EMB)))
    min_decay = math.log(DECAY_TARGET) / SLOW_DECAY_PCT
    max_decay = math.log(DECAY_TARGET) / FAST_DECAY_PCT
    deltas = np.linspace(min_decay, max_decay, HYENA_W, dtype=np.float32).astype(np.float64)
    decay = np.exp(-t * np.abs(deltas))
    return jnp.asarray(_fold_rows(feat).T, F32), jnp.asarray(_fold_rows(decay), F32)


def _attn_bias():
    qi = np.arange(BLOCK)[:, None]
    kj = np.arange(3 * BLOCK)[None, :]
    band = np.abs(kj - BLOCK - qi) <= WINDOW
    first = band & (kj >= BLOCK)
    last = band & (kj < 2 * BLOCK)
    return jnp.asarray(np.where(np.stack([first, band, last]), 0.0, NEG), F32)


def _pair_heads(w, axis):
    heads = [lax.slice_in_dim(w, h * HEAD_DIM, (h + 1) * HEAD_DIM, axis=axis) for h in range(N_HEADS)]
    return jnp.concatenate([heads[j + GROUP * half] for j in range(GROUP) for half in range(N_KV_HEADS)], axis=axis)


def kernel(x, c, ctx, c_ctx, ada_w, ada_b, norm1, norm2, w_in, conv_w, conv_b, filt_w1, filt_b1, filt_w2, filt_b2,
           filt_w3, filt_b3, filt_freq, filt_out, hyena_bias, q_norm, k_norm, attn_sink, w_branch_a, w_branch_b,
           w_out, router, w_gate, w_up, w_down):
    d = D_MODEL
    assert ada_w.shape[0] == 1, "only the single-layer configuration is implemented"
    l = 0
    cos_t, sin_t = _rope_tables()
    t1, t2, t1t, t2t = _dft_tables()
    rot = _phase_tables()
    feat, decay = _filter_features()
    bias = _attn_bias()
    gmat = jnp.asarray(np.kron(np.eye(2 * LANES // HEAD_DIM), np.full((HEAD_DIM, HEAD_DIM), 1.0 / HEAD_DIM)), BF16)
    c16 = jnp.concatenate([c, c_ctx[None, :], jnp.zeros((MOD_ROWS - BATCH - 1, d), F32)], axis=0)

    mods3 = _ada(c16, ada_w[l], ada_b[l][None, :])
    n1 = norm1[l][None, :]
    w_packed = _pack_in_weights(w_in[l])
    gk = jnp.tile(k_norm[l], N_KV_HEADS)[None, :]
    w1_rows = jnp.pad(filt_w1[l], ((0, FILTER_HIDDEN - FILTER_EMB), (0, 0)))
    mlp_w = jnp.swapaxes(jnp.stack([w1_rows, filt_w2[l], filt_w3[l]]), 1, 2)
    mlp_cols = jnp.stack([filt_b1[l], filt_b2[l], filt_b3[l], filt_freq[l]], axis=1)
    kf, kn = _filters(feat, mlp_w, mlp_cols, filt_out[l], decay, rot, t1, t2)
    kc, vc = _ctx_proj(ctx, mods3, n1, w_packed, gk, gmat)
    zh, q, k, v, gates = _inproj(x, mods3, n1, w_packed, jnp.tile(q_norm[l], N_HEADS)[None, :], gk, gmat, cos_t, sin_t)
    ya = _hyena(zh, conv_w[l], conv_b[l][None, :], hyena_bias[l], kf, kn, t1, t2, t1t, t2t)
    yb = _attention(attn_sink[l] * LOG2E, q, k, v, kc, vc, bias)
    xn, h2, aff_t = _merge(ya, yb, gates, x, mods3, norm2[l][None, :], w_branch_a[l].astype(BF16),
                           _pair_heads(w_branch_b[l], 0).astype(BF16), w_out[l].astype(BF16), router[l].T.astype(BF16))
    lo, hi = _thresholds(aff_t.reshape(BATCH * N_EXPERTS, SEQ))
    slot_rows = jnp.asarray(np.broadcast_to(np.arange(CAP)[:, None], (CAP, SEQ)), BF16)
    xin, pos = _route(aff_t, lo, hi, h2, slot_rows)
    y = _experts(xin, w_gate[l], w_up[l], w_down[l])
    return _scatter(pos, aff_t, y, xn, mods3)
```
